```python
import math
import jax, jax.numpy as jnp
from jax import lax
import numpy as np

D_MODEL = 2048
BATCH = 8
SEQ = 4096
DEPTH = 2

HEAD_DIM = 64
N_Q_HEADS = 16
N_KV_HEADS = 4
GQA_GROUP = N_Q_HEADS // N_KV_HEADS
ATTN_WIDTH = N_Q_HEADS * HEAD_DIM
KV_WIDTH = N_KV_HEADS * HEAD_DIM
WINDOW = 128
BLOCK = 128
SSM_WIDTH = D_MODEL // 2
SSM_GROUP_CH = 16
SSM_GROUPS = SSM_WIDTH // SSM_GROUP_CH
SSM_STATE = 64
DT_MIN = 1e-3
DT_MAX = 1e-1
D_FF = -(-8 * D_MODEL // (3 * 256)) * 256
OFF_Q = 0
OFF_K = OFF_Q + ATTN_WIDTH
OFF_V = OFF_K + KV_WIDTH
OFF_U = OFF_V + KV_WIDTH
OFF_G = OFF_U + SSM_WIDTH
IN_WIDTH = OFF_G + 2 * D_MODEL
RMS_EPS = 1e-6

kernel_name = "hybrid_swa_sink_s5_gated_block"


def rmsnorm(x, g):
    xf = x.astype(jnp.float32)
    y = xf * lax.rsqrt(jnp.mean(xf * xf, axis=-1, keepdims=True) + RMS_EPS)
    return (y * g.astype(jnp.float32)).astype(x.dtype)


def alibi_slopes():
    return jnp.exp2(-8.0 * jnp.arange(1, N_Q_HEADS + 1, dtype=jnp.float32) / N_Q_HEADS)


def sliding_window_attention(q, k, v, q_gain, k_gain, sinks):
    B, L = q.shape[0], q.shape[1]
    nb = L // BLOCK
    q = rmsnorm(q, q_gain).astype(jnp.float32)
    k = rmsnorm(k, k_gain).astype(jnp.float32)
    v = v.astype(jnp.float32)
    qb = q.reshape(B, nb, BLOCK, N_KV_HEADS, GQA_GROUP, HEAD_DIM)
    pad = ((0, 0), (BLOCK, 0), (0, 0), (0, 0))
    kp = jnp.pad(k, pad)[:, :L].reshape(B, nb, BLOCK, N_KV_HEADS, HEAD_DIM)
    vp = jnp.pad(v, pad)[:, :L].reshape(B, nb, BLOCK, N_KV_HEADS, HEAD_DIM)
    kb = jnp.concatenate([kp, k.reshape(B, nb, BLOCK, N_KV_HEADS, HEAD_DIM)], axis=2)
    vb = jnp.concatenate([vp, v.reshape(B, nb, BLOCK, N_KV_HEADS, HEAD_DIM)], axis=2)
    scores = jnp.einsum('bnqkgd,bnskd->bnkgqs', qb, kb) * (HEAD_DIM ** -0.5)
    t_loc = jnp.arange(BLOCK)
    s_loc = jnp.arange(2 * BLOCK) - BLOCK
    dist = (t_loc[:, None] - s_loc[None, :]).astype(jnp.float32)
    s_abs = jnp.arange(nb)[:, None] * BLOCK + s_loc[None, :]
    valid = (dist >= 0)[None] & (dist < WINDOW)[None] & (s_abs >= 0)[:, None, :]
    bias = (-alibi_slopes()[:, None, None] * dist[None]).reshape(N_KV_HEADS, GQA_GROUP, BLOCK, 2 * BLOCK)
    scores = jnp.where(valid[None, :, None, None], scores + bias[None, None], -jnp.inf)
    sink = sinks.astype(jnp.float32).reshape(1, 1, N_KV_HEADS, GQA_GROUP, 1, 1)
    m = jnp.maximum(jnp.max(scores, axis=-1, keepdims=True), sink)
    p = jnp.exp(scores - m)
    denom = jnp.sum(p, axis=-1, keepdims=True) + jnp.exp(sink - m)
    out = jnp.einsum('bnkgqs,bnskd->bnqkgd', p / denom, vb)
    return out.reshape(B, L, ATTN_WIDTH)


def s5_ssm(u, lam_re, lam_im, log_dt, b_re, b_im, c_re, c_im, d_skip):
    B, L = u.shape[0], u.shape[1]
    uf = u.astype(jnp.float32).reshape(B, L, SSM_GROUPS, SSM_GROUP_CH)
    lr = lam_re.astype(jnp.float32)
    li = lam_im.astype(jnp.float32)
    dt = jnp.exp(log_dt.astype(jnp.float32))[:, None]
    mag = jnp.exp(lr * dt)
    ar = mag * jnp.cos(li * dt)
    ai = mag * jnp.sin(li * dt)
    den = lr * lr + li * li
    fr = ((ar - 1.0) * lr + ai * li) / den
    fi = (ai * lr - (ar - 1.0) * li) / den
    br = b_re.astype(jnp.float32)
    bi = b_im.astype(jnp.float32)
    bbar_r = fr[:, :, None] * br - fi[:, :, None] * bi
    bbar_i = fr[:, :, None] * bi + fi[:, :, None] * br
    bu_r = jnp.einsum('blgh,gph->blgp', uf, bbar_r)
    bu_i = jnp.einsum('blgh,gph->blgp', uf, bbar_i)
    a_r = jnp.broadcast_to(ar[None, None], (1, L, SSM_GROUPS, SSM_STATE))
    a_i = jnp.broadcast_to(ai[None, None], (1, L, SSM_GROUPS, SSM_STATE))

    def combine(e1, e2):
        ar1, ai1, br1, bi1 = e1
        ar2, ai2, br2, bi2 = e2
        return (ar2 * ar1 - ai2 * ai1,
                ar2 * ai1 + ai2 * ar1,
                ar2 * br1 - ai2 * bi1 + br2,
                ar2 * bi1 + ai2 * br1 + bi2)

    _, _, s_r, s_i = lax.associative_scan(combine, (a_r, a_i, bu_r, bu_i), axis=1)
    y = (jnp.einsum('blgp,ghp->blgh', s_r, c_re.astype(jnp.float32))
         - jnp.einsum('blgp,ghp->blgh', s_i, c_im.astype(jnp.float32))
         + d_skip.astype(jnp.float32).reshape(SSM_GROUPS, SSM_GROUP_CH) * uf)
    return y.reshape(B, L, SSM_WIDTH)


def _fwd_setup_inputs(seed: int = 0) -> dict:
    key = jax.random.key(seed)
    ks = jax.random.split(key, 24)
    f32 = jnp.float32
    nrm = lambda k, shape, scale: jax.random.normal(k, shape, f32) * scale
    x = jax.random.normal(ks[0], (BATCH, SEQ, D_MODEL), f32)
    norm_mix_g = 1.0 + nrm(ks[1], (DEPTH, D_MODEL), 0.02)
    w_in = nrm(ks[2], (DEPTH, D_MODEL, IN_WIDTH), D_MODEL ** -0.5)
    gate_bias = nrm(ks[3], (DEPTH, 2 * D_MODEL), 0.02)
    q_norm_g = 1.0 + nrm(ks[4], (DEPTH, HEAD_DIM), 0.02)
    k_norm_g = 1.0 + nrm(ks[5], (DEPTH, HEAD_DIM), 0.02)
    attn_sinks = nrm(ks[6], (DEPTH, N_Q_HEADS), 0.5)
    ssm_lambda_re = -0.5 + nrm(ks[7], (DEPTH, SSM_GROUPS, SSM_STATE), 0.01)
    ssm_lambda_im = (math.pi * jnp.arange(SSM_STATE, dtype=f32))[None, None, :] + nrm(ks[8], (DEPTH, SSM_GROUPS, SSM_STATE), 0.01)
    ssm_log_dt = jax.random.uniform(ks[9], (DEPTH, SSM_GROUPS), f32, math.log(DT_MIN), math.log(DT_MAX))
    ssm_b_re = nrm(ks[10], (DEPTH, SSM_GROUPS, SSM_STATE, SSM_GROUP_CH), (2 * SSM_GROUP_CH) ** -0.5)
    ssm_b_im = nrm(ks[11], (DEPTH, SSM_GROUPS, SSM_STATE, SSM_GROUP_CH), (2 * SSM_GROUP_CH) ** -0.5)
    ssm_c_re = nrm(ks[12], (DEPTH, SSM_GROUPS, SSM_GROUP_CH, SSM_STATE), (2 * SSM_STATE) ** -0.5)
    ssm_c_im = nrm(ks[13], (DEPTH, SSM_GROUPS, SSM_GROUP_CH, SSM_STATE), (2 * SSM_STATE) ** -0.5)
    ssm_d = nrm(ks[14], (DEPTH, SSM_WIDTH), 1.0)
    ssm_glu_w = nrm(ks[15], (DEPTH, SSM_WIDTH, SSM_WIDTH), SSM_WIDTH ** -0.5)
    ssm_glu_b = nrm(ks[16], (DEPTH, SSM_WIDTH), 0.02)
    w_attn_branch = nrm(ks[17], (DEPTH, ATTN_WIDTH, D_MODEL), ATTN_WIDTH ** -0.5)
    w_ssm_branch = nrm(ks[18], (DEPTH, SSM_WIDTH, D_MODEL), SSM_WIDTH ** -0.5)
    w_out = nrm(ks[19], (DEPTH, D_MODEL, D_MODEL), D_MODEL ** -0.5)
    norm_ffn_g = 1.0 + nrm(ks[20], (DEPTH, D_MODEL), 0.02)
    w_ffn_in = nrm(ks[21], (DEPTH, D_MODEL, 2 * D_FF), D_MODEL ** -0.5)
    w_ffn_out = nrm(ks[22], (DEPTH, D_FF, D_MODEL), D_FF ** -0.5)
    return {"x": x, "norm_mix_g": norm_mix_g, "w_in": w_in, "gate_bias": gate_bias,
            "q_norm_g": q_norm_g, "k_norm_g": k_norm_g, "attn_sinks": attn_sinks,
            "ssm_lambda_re": ssm_lambda_re, "ssm_lambda_im": ssm_lambda_im, "ssm_log_dt": ssm_log_dt,
            "ssm_b_re": ssm_b_re, "ssm_b_im": ssm_b_im, "ssm_c_re": ssm_c_re, "ssm_c_im": ssm_c_im,
            "ssm_d": ssm_d, "ssm_glu_w": ssm_glu_w, "ssm_glu_b": ssm_glu_b,
            "w_attn_branch": w_attn_branch, "w_ssm_branch": w_ssm_branch, "w_out": w_out,
            "norm_ffn_g": norm_ffn_g, "w_ffn_in": w_ffn_in, "w_ffn_out": w_ffn_out}


def _fwd_reference(x, norm_mix_g, w_in, gate_bias, q_norm_g, k_norm_g, attn_sinks,
              ssm_lambda_re, ssm_lambda_im, ssm_log_dt, ssm_b_re, ssm_b_im, ssm_c_re, ssm_c_im,
              ssm_d, ssm_glu_w, ssm_glu_b, w_attn_branch, w_ssm_branch, w_out,
              norm_ffn_g, w_ffn_in, w_ffn_out):
    B, L = x.shape[0], x.shape[1]
    for l in range(DEPTH):
        h = rmsnorm(x, norm_mix_g[l])
        z = h @ w_in[l]
        q = z[..., OFF_Q:OFF_K].reshape(B, L, N_Q_HEADS, HEAD_DIM)
        k = z[..., OFF_K:OFF_V].reshape(B, L, N_KV_HEADS, HEAD_DIM)
        v = z[..., OFF_V:OFF_U].reshape(B, L, N_KV_HEADS, HEAD_DIM)
        u = z[..., OFF_U:OFF_G]
        gates = jax.nn.sigmoid(z[..., OFF_G:] + gate_bias[l])
        g_attn = gates[..., :D_MODEL]
        g_ssm = gates[..., D_MODEL:]
        y_attn = sliding_window_attention(q, k, v, q_norm_g[l], k_norm_g[l], attn_sinks[l]).astype(x.dtype)
        y_ssm = s5_ssm(u, ssm_lambda_re[l], ssm_lambda_im[l], ssm_log_dt[l], ssm_b_re[l], ssm_b_im[l],
                       ssm_c_re[l], ssm_c_im[l], ssm_d[l])
        y_ssm = jax.nn.gelu(y_ssm)
        y_ssm = (y_ssm * jax.nn.sigmoid(y_ssm @ ssm_glu_w[l].astype(jnp.float32)
                                        + ssm_glu_b[l].astype(jnp.float32))).astype(x.dtype)
        merged = g_attn * (y_attn @ w_attn_branch[l]) + g_ssm * (y_ssm @ w_ssm_branch[l])
        x = x + merged @ w_out[l]
        h2 = rmsnorm(x, norm_ffn_g[l])
        gu = h2 @ w_ffn_in[l]
        x = x + (jax.nn.silu(gu[..., :D_FF]) * gu[..., D_FF:]) @ w_ffn_out[l]
    return x


import jax as _jax
import jax.numpy as _jnp

TWIN_FORMAT = 'train_step'
FWD_PARAMS = ['x', 'norm_mix_g', 'w_in', 'gate_bias', 'q_norm_g', 'k_norm_g', 'attn_sinks', 'ssm_lambda_re', 'ssm_lambda_im', 'ssm_log_dt', 'ssm_b_re', 'ssm_b_im', 'ssm_c_re', 'ssm_c_im', 'ssm_d', 'ssm_glu_w', 'ssm_glu_b', 'w_attn_branch', 'w_ssm_branch', 'w_out', 'norm_ffn_g', 'w_ffn_in', 'w_ffn_out']
TWIN_WEIGHTS = ['norm_mix_g', 'w_in', 'gate_bias', 'q_norm_g', 'k_norm_g', 'attn_sinks', 'ssm_lambda_re', 'ssm_lambda_im', 'ssm_log_dt', 'ssm_b_re', 'ssm_b_im', 'ssm_c_re', 'ssm_c_im', 'ssm_d', 'ssm_glu_w', 'ssm_glu_b', 'w_attn_branch', 'w_ssm_branch', 'w_out', 'norm_ffn_g', 'w_ffn_in', 'w_ffn_out']
TWIN_DIFF_INPUT = 'x'
TWIN_INPUTS = ['x', 'norm_mix_g', 'w_in', 'gate_bias', 'q_norm_g', 'k_norm_g', 'attn_sinks', 'ssm_lambda_re', 'ssm_lambda_im', 'ssm_log_dt', 'ssm_b_re', 'ssm_b_im', 'ssm_c_re', 'ssm_c_im', 'ssm_d', 'ssm_glu_w', 'ssm_glu_b', 'w_attn_branch', 'w_ssm_branch', 'w_out', 'norm_ffn_g', 'w_ffn_in', 'w_ffn_out', 'loss_target', 'm_norm_mix_g', 'm_w_in', 'm_gate_bias', 'm_q_norm_g', 'm_k_norm_g', 'm_attn_sinks', 'm_ssm_lambda_re', 'm_ssm_lambda_im', 'm_ssm_log_dt', 'm_ssm_b_re', 'm_ssm_b_im', 'm_ssm_c_re', 'm_ssm_c_im', 'm_ssm_d', 'm_ssm_glu_w', 'm_ssm_glu_b', 'm_w_attn_branch', 'm_w_ssm_branch', 'm_w_out', 'm_norm_ffn_g', 'm_w_ffn_in', 'm_w_ffn_out', 'v_norm_mix_g', 'v_w_in', 'v_gate_bias', 'v_q_norm_g', 'v_k_norm_g', 'v_attn_sinks', 'v_ssm_lambda_re', 'v_ssm_lambda_im', 'v_ssm_log_dt', 'v_ssm_b_re', 'v_ssm_b_im', 'v_ssm_c_re', 'v_ssm_c_im', 'v_ssm_d', 'v_ssm_glu_w', 'v_ssm_glu_b', 'v_w_attn_branch', 'v_w_ssm_branch', 'v_w_out', 'v_norm_ffn_g', 'v_w_ffn_in', 'v_w_ffn_out']
TWIN_OUTPUTS = ['loss', 'grad_x', 'grad_norm_mix_g', 'grad_w_in', 'grad_gate_bias', 'grad_q_norm_g', 'grad_k_norm_g', 'grad_attn_sinks', 'grad_ssm_lambda_re', 'grad_ssm_lambda_im', 'grad_ssm_log_dt', 'grad_ssm_b_re', 'grad_ssm_b_im', 'grad_ssm_c_re', 'grad_ssm_c_im', 'grad_ssm_d', 'grad_ssm_glu_w', 'grad_ssm_glu_b', 'grad_w_attn_branch', 'grad_w_ssm_branch', 'grad_w_out', 'grad_norm_ffn_g', 'grad_w_ffn_in', 'grad_w_ffn_out', 'delta_norm_mix_g', 'delta_w_in', 'delta_gate_bias', 'delta_q_norm_g', 'delta_k_norm_g', 'delta_attn_sinks', 'delta_ssm_lambda_re', 'delta_ssm_lambda_im', 'delta_ssm_log_dt', 'delta_ssm_b_re', 'delta_ssm_b_im', 'delta_ssm_c_re', 'delta_ssm_c_im', 'delta_ssm_d', 'delta_ssm_glu_w', 'delta_ssm_glu_b', 'delta_w_attn_branch', 'delta_w_ssm_branch', 'delta_w_out', 'delta_norm_ffn_g', 'delta_w_ffn_in', 'delta_w_ffn_out', 'new_m_norm_mix_g', 'new_m_w_in', 'new_m_gate_bias', 'new_m_q_norm_g', 'new_m_k_norm_g', 'new_m_attn_sinks', 'new_m_ssm_lambda_re', 'new_m_ssm_lambda_im', 'new_m_ssm_log_dt', 'new_m_ssm_b_re', 'new_m_ssm_b_im', 'new_m_ssm_c_re', 'new_m_ssm_c_im', 'new_m_ssm_d', 'new_m_ssm_glu_w', 'new_m_ssm_glu_b', 'new_m_w_attn_branch', 'new_m_w_ssm_branch', 'new_m_w_out', 'new_m_norm_ffn_g', 'new_m_w_ffn_in', 'new_m_w_ffn_out', 'new_v_norm_mix_g', 'new_v_w_in', 'new_v_gate_bias', 'new_v_q_norm_g', 'new_v_k_norm_g', 'new_v_attn_sinks', 'new_v_ssm_lambda_re', 'new_v_ssm_lambda_im', 'new_v_ssm_log_dt', 'new_v_ssm_b_re', 'new_v_ssm_b_im', 'new_v_ssm_c_re', 'new_v_ssm_c_im', 'new_v_ssm_d', 'new_v_ssm_glu_w', 'new_v_ssm_glu_b', 'new_v_w_attn_branch', 'new_v_w_ssm_branch', 'new_v_w_out', 'new_v_norm_ffn_g', 'new_v_w_ffn_in', 'new_v_w_ffn_out']
TWIN_LEAF_KINDS = {'loss': 'loss', 'grad_x': 'grad_x', 'grad_norm_mix_g': 'grad_w', 'grad_w_in': 'grad_w', 'grad_gate_bias': 'grad_w', 'grad_q_norm_g': 'grad_w', 'grad_k_norm_g': 'grad_w', 'grad_attn_sinks': 'grad_w', 'grad_ssm_lambda_re': 'grad_w', 'grad_ssm_lambda_im': 'grad_w', 'grad_ssm_log_dt': 'grad_w', 'grad_ssm_b_re': 'grad_w', 'grad_ssm_b_im': 'grad_w', 'grad_ssm_c_re': 'grad_w', 'grad_ssm_c_im': 'grad_w', 'grad_ssm_d': 'grad_w', 'grad_ssm_glu_w': 'grad_w', 'grad_ssm_glu_b': 'grad_w', 'grad_w_attn_branch': 'grad_w', 'grad_w_ssm_branch': 'grad_w', 'grad_w_out': 'grad_w', 'grad_norm_ffn_g': 'grad_w', 'grad_w_ffn_in': 'grad_w', 'grad_w_ffn_out': 'grad_w', 'delta_norm_mix_g': 'delta_w', 'delta_w_in': 'delta_w', 'delta_gate_bias': 'delta_w', 'delta_q_norm_g': 'delta_w', 'delta_k_norm_g': 'delta_w', 'delta_attn_sinks': 'delta_w', 'delta_ssm_lambda_re': 'delta_w', 'delta_ssm_lambda_im': 'delta_w', 'delta_ssm_log_dt': 'delta_w', 'delta_ssm_b_re': 'delta_w', 'delta_ssm_b_im': 'delta_w', 'delta_ssm_c_re': 'delta_w', 'delta_ssm_c_im': 'delta_w', 'delta_ssm_d': 'delta_w', 'delta_ssm_glu_w': 'delta_w', 'delta_ssm_glu_b': 'delta_w', 'delta_w_attn_branch': 'delta_w', 'delta_w_ssm_branch': 'delta_w', 'delta_w_out': 'delta_w', 'delta_norm_ffn_g': 'delta_w', 'delta_w_ffn_in': 'delta_w', 'delta_w_ffn_out': 'delta_w', 'new_m_norm_mix_g': 'new_m', 'new_m_w_in': 'new_m', 'new_m_gate_bias': 'new_m', 'new_m_q_norm_g': 'new_m', 'new_m_k_norm_g': 'new_m', 'new_m_attn_sinks': 'new_m', 'new_m_ssm_lambda_re': 'new_m', 'new_m_ssm_lambda_im': 'new_m', 'new_m_ssm_log_dt': 'new_m', 'new_m_ssm_b_re': 'new_m', 'new_m_ssm_b_im': 'new_m', 'new_m_ssm_c_re': 'new_m', 'new_m_ssm_c_im': 'new_m', 'new_m_ssm_d': 'new_m', 'new_m_ssm_glu_w': 'new_m', 'new_m_ssm_glu_b': 'new_m', 'new_m_w_attn_branch': 'new_m', 'new_m_w_ssm_branch': 'new_m', 'new_m_w_out': 'new_m', 'new_m_norm_ffn_g': 'new_m', 'new_m_w_ffn_in': 'new_m', 'new_m_w_ffn_out': 'new_m', 'new_v_norm_mix_g': 'new_v', 'new_v_w_in': 'new_v', 'new_v_gate_bias': 'new_v', 'new_v_q_norm_g': 'new_v', 'new_v_k_norm_g': 'new_v', 'new_v_attn_sinks': 'new_v', 'new_v_ssm_lambda_re': 'new_v', 'new_v_ssm_lambda_im': 'new_v', 'new_v_ssm_log_dt': 'new_v', 'new_v_ssm_b_re': 'new_v', 'new_v_ssm_b_im': 'new_v', 'new_v_ssm_c_re': 'new_v', 'new_v_ssm_c_im': 'new_v', 'new_v_ssm_d': 'new_v', 'new_v_ssm_glu_w': 'new_v', 'new_v_ssm_glu_b': 'new_v', 'new_v_w_attn_branch': 'new_v', 'new_v_w_ssm_branch': 'new_v', 'new_v_w_out': 'new_v', 'new_v_norm_ffn_g': 'new_v', 'new_v_w_ffn_in': 'new_v', 'new_v_w_ffn_out': 'new_v'}


def _forward(args):
    return _fwd_reference(*[args[k] for k in FWD_PARAMS])


def _output_shape():
    def fwd():
        inp = _fwd_setup_inputs(0)
        return _fwd_reference(*[inp[k] for k in FWD_PARAMS])
    out = _jax.eval_shape(fwd)
    return out.shape, out.dtype

N_MICROBATCH = 1
ADAM_LR = 0.001
ADAM_B1 = 0.9
ADAM_B2 = 0.999
ADAM_EPS = 1e-08
ADAM_WD = 0.01
ADAM_STEP = 10
PER_EXAMPLE_BATCH_AXIS = {'x': 0, 'loss_target': 0}
SHARED_INPUTS = []
_WEIGHT_DTYPES = {'norm_mix_g': _jnp.float32, 'w_in': _jnp.float32, 'gate_bias': _jnp.float32, 'q_norm_g': _jnp.float32, 'k_norm_g': _jnp.float32, 'attn_sinks': _jnp.float32, 'ssm_lambda_re': _jnp.float32, 'ssm_lambda_im': _jnp.float32, 'ssm_log_dt': _jnp.float32, 'ssm_b_re': _jnp.float32, 'ssm_b_im': _jnp.float32, 'ssm_c_re': _jnp.float32, 'ssm_c_im': _jnp.float32, 'ssm_d': _jnp.float32, 'ssm_glu_w': _jnp.float32, 'ssm_glu_b': _jnp.float32, 'w_attn_branch': _jnp.float32, 'w_ssm_branch': _jnp.float32, 'w_out': _jnp.float32, 'norm_ffn_g': _jnp.float32, 'w_ffn_in': _jnp.float32, 'w_ffn_out': _jnp.float32}
MOMENT_SCALE = {'norm_mix_g': 1.216283e+00, 'w_in': 7.749239e-02, 'gate_bias': 2.128820e-01, 'q_norm_g': 5.925702e+00, 'k_norm_g': 5.909890e+00, 'attn_sinks': 1.072151e+01, 'ssm_lambda_re': 7.166769e-03, 'ssm_lambda_im': 8.239345e-03, 'ssm_log_dt': 1.928326e+00, 'ssm_b_re': 5.513539e-03, 'ssm_b_im': 5.116493e-03, 'ssm_c_re': 9.958151e-03, 'ssm_c_im': 1.025711e-02, 'ssm_d': 1.597933e+00, 'ssm_glu_w': 3.180208e-01, 'ssm_glu_b': 9.288096e-01, 'w_attn_branch': 7.835253e-02, 'w_ssm_branch': 3.475467e-01, 'w_out': 3.162376e-01, 'norm_ffn_g': 1.234338e+01, 'w_ffn_in': 1.196781e-01, 'w_ffn_out': 1.447811e-01}


def _to_microbatches(a, axis):
    t = _jnp.moveaxis(a, axis, 0)
    t = t.reshape((N_MICROBATCH, t.shape[0] // N_MICROBATCH) + t.shape[1:])
    return _jnp.moveaxis(t, 1, axis + 1)


def setup_inputs(seed: int = 0) -> dict:
    inp = _fwd_setup_inputs(seed)
    key = _jax.random.fold_in(_jax.random.key(seed), 7919)
    shape, _ = _output_shape()
    out = dict(inp)
    out["loss_target"] = _jax.random.normal(_jax.random.fold_in(key, 0), shape, _jnp.float32)
    for i, name in enumerate(TWIN_WEIGHTS):
        w = inp[name].astype(_jnp.float32)
        if MOMENT_SCALE is None:
            s = _jnp.sqrt(_jnp.mean(_jnp.square(w)) + 1e-30)
        else:
            s = MOMENT_SCALE[name]
        km, kv = _jax.random.split(_jax.random.fold_in(key, i + 1))
        out[name] = w
        out["m_" + name] = s * _jax.random.normal(km, w.shape, _jnp.float32)
        out["v_" + name] = (s * s) * _jax.random.uniform(kv, w.shape, _jnp.float32, 0.5, 1.5)
    if N_MICROBATCH > 1:
        for name, axis in PER_EXAMPLE_BATCH_AXIS.items():
            out[name] = _to_microbatches(out[name], axis)
    return {'x': out['x'], 'norm_mix_g': out['norm_mix_g'], 'w_in': out['w_in'], 'gate_bias': out['gate_bias'], 'q_norm_g': out['q_norm_g'], 'k_norm_g': out['k_norm_g'], 'attn_sinks': out['attn_sinks'], 'ssm_lambda_re': out['ssm_lambda_re'], 'ssm_lambda_im': out['ssm_lambda_im'], 'ssm_log_dt': out['ssm_log_dt'], 'ssm_b_re': out['ssm_b_re'], 'ssm_b_im': out['ssm_b_im'], 'ssm_c_re': out['ssm_c_re'], 'ssm_c_im': out['ssm_c_im'], 'ssm_d': out['ssm_d'], 'ssm_glu_w': out['ssm_glu_w'], 'ssm_glu_b': out['ssm_glu_b'], 'w_attn_branch': out['w_attn_branch'], 'w_ssm_branch': out['w_ssm_branch'], 'w_out': out['w_out'], 'norm_ffn_g': out['norm_ffn_g'], 'w_ffn_in': out['w_ffn_in'], 'w_ffn_out': out['w_ffn_out'], 'loss_target': out['loss_target'], 'm_norm_mix_g': out['m_norm_mix_g'], 'm_w_in': out['m_w_in'], 'm_gate_bias': out['m_gate_bias'], 'm_q_norm_g': out['m_q_norm_g'], 'm_k_norm_g': out['m_k_norm_g'], 'm_attn_sinks': out['m_attn_sinks'], 'm_ssm_lambda_re': out['m_ssm_lambda_re'], 'm_ssm_lambda_im': out['m_ssm_lambda_im'], 'm_ssm_log_dt': out['m_ssm_log_dt'], 'm_ssm_b_re': out['m_ssm_b_re'], 'm_ssm_b_im': out['m_ssm_b_im'], 'm_ssm_c_re': out['m_ssm_c_re'], 'm_ssm_c_im': out['m_ssm_c_im'], 'm_ssm_d': out['m_ssm_d'], 'm_ssm_glu_w': out['m_ssm_glu_w'], 'm_ssm_glu_b': out['m_ssm_glu_b'], 'm_w_attn_branch': out['m_w_attn_branch'], 'm_w_ssm_branch': out['m_w_ssm_branch'], 'm_w_out': out['m_w_out'], 'm_norm_ffn_g': out['m_norm_ffn_g'], 'm_w_ffn_in': out['m_w_ffn_in'], 'm_w_ffn_out': out['m_w_ffn_out'], 'v_norm_mix_g': out['v_norm_mix_g'], 'v_w_in': out['v_w_in'], 'v_gate_bias': out['v_gate_bias'], 'v_q_norm_g': out['v_q_norm_g'], 'v_k_norm_g': out['v_k_norm_g'], 'v_attn_sinks': out['v_attn_sinks'], 'v_ssm_lambda_re': out['v_ssm_lambda_re'], 'v_ssm_lambda_im': out['v_ssm_lambda_im'], 'v_ssm_log_dt': out['v_ssm_log_dt'], 'v_ssm_b_re': out['v_ssm_b_re'], 'v_ssm_b_im': out['v_ssm_b_im'], 'v_ssm_c_re': out['v_ssm_c_re'], 'v_ssm_c_im': out['v_ssm_c_im'], 'v_ssm_d': out['v_ssm_d'], 'v_ssm_glu_w': out['v_ssm_glu_w'], 'v_ssm_glu_b': out['v_ssm_glu_b'], 'v_w_attn_branch': out['v_w_attn_branch'], 'v_w_ssm_branch': out['v_w_ssm_branch'], 'v_w_out': out['v_w_out'], 'v_norm_ffn_g': out['v_norm_ffn_g'], 'v_w_ffn_in': out['v_w_ffn_in'], 'v_w_ffn_out': out['v_w_ffn_out']}


def _loss(weights, diff, rest, loss_target):
    with _jax.named_scope("forward"):
        args = {**rest, TWIN_DIFF_INPUT: diff, **{k: w.astype(_WEIGHT_DTYPES[k]) for k, w in weights.items()}}
        y = _forward(args)
    with _jax.named_scope("loss_head"):
        err = _jnp.square(y.astype(_jnp.float32) - loss_target)
        return 0.5 * _jnp.sum(_jnp.mean(err, axis=-1)) if err.ndim else 0.5 * err


def _adamw(w, g, m, v):
    m = ADAM_B1 * m + (1.0 - ADAM_B1) * g
    v = ADAM_B2 * v + (1.0 - ADAM_B2) * _jnp.square(g)
    m_hat = m / (1.0 - ADAM_B1 ** ADAM_STEP)
    v_hat = v / (1.0 - ADAM_B2 ** ADAM_STEP)
    delta = -ADAM_LR * (m_hat / (_jnp.sqrt(v_hat) + ADAM_EPS) + ADAM_WD * w)
    return delta, m, v


def reference(x, norm_mix_g, w_in, gate_bias, q_norm_g, k_norm_g, attn_sinks, ssm_lambda_re, ssm_lambda_im, ssm_log_dt, ssm_b_re, ssm_b_im, ssm_c_re, ssm_c_im, ssm_d, ssm_glu_w, ssm_glu_b, w_attn_branch, w_ssm_branch, w_out, norm_ffn_g, w_ffn_in, w_ffn_out, loss_target, m_norm_mix_g, m_w_in, m_gate_bias, m_q_norm_g, m_k_norm_g, m_attn_sinks, m_ssm_lambda_re, m_ssm_lambda_im, m_ssm_log_dt, m_ssm_b_re, m_ssm_b_im, m_ssm_c_re, m_ssm_c_im, m_ssm_d, m_ssm_glu_w, m_ssm_glu_b, m_w_attn_branch, m_w_ssm_branch, m_w_out, m_norm_ffn_g, m_w_ffn_in, m_w_ffn_out, v_norm_mix_g, v_w_in, v_gate_bias, v_q_norm_g, v_k_norm_g, v_attn_sinks, v_ssm_lambda_re, v_ssm_lambda_im, v_ssm_log_dt, v_ssm_b_re, v_ssm_b_im, v_ssm_c_re, v_ssm_c_im, v_ssm_d, v_ssm_glu_w, v_ssm_glu_b, v_w_attn_branch, v_w_ssm_branch, v_w_out, v_norm_ffn_g, v_w_ffn_in, v_w_ffn_out):
    given = dict(x=x, norm_mix_g=norm_mix_g, w_in=w_in, gate_bias=gate_bias, q_norm_g=q_norm_g, k_norm_g=k_norm_g, attn_sinks=attn_sinks, ssm_lambda_re=ssm_lambda_re, ssm_lambda_im=ssm_lambda_im, ssm_log_dt=ssm_log_dt, ssm_b_re=ssm_b_re, ssm_b_im=ssm_b_im, ssm_c_re=ssm_c_re, ssm_c_im=ssm_c_im, ssm_d=ssm_d, ssm_glu_w=ssm_glu_w, ssm_glu_b=ssm_glu_b, w_attn_branch=w_attn_branch, w_ssm_branch=w_ssm_branch, w_out=w_out, norm_ffn_g=norm_ffn_g, w_ffn_in=w_ffn_in, w_ffn_out=w_ffn_out, loss_target=loss_target, m_norm_mix_g=m_norm_mix_g, m_w_in=m_w_in, m_gate_bias=m_gate_bias, m_q_norm_g=m_q_norm_g, m_k_norm_g=m_k_norm_g, m_attn_sinks=m_attn_sinks, m_ssm_lambda_re=m_ssm_lambda_re, m_ssm_lambda_im=m_ssm_lambda_im, m_ssm_log_dt=m_ssm_log_dt, m_ssm_b_re=m_ssm_b_re, m_ssm_b_im=m_ssm_b_im, m_ssm_c_re=m_ssm_c_re, m_ssm_c_im=m_ssm_c_im, m_ssm_d=m_ssm_d, m_ssm_glu_w=m_ssm_glu_w, m_ssm_glu_b=m_ssm_glu_b, m_w_attn_branch=m_w_attn_branch, m_w_ssm_branch=m_w_ssm_branch, m_w_out=m_w_out, m_norm_ffn_g=m_norm_ffn_g, m_w_ffn_in=m_w_ffn_in, m_w_ffn_out=m_w_ffn_out, v_norm_mix_g=v_norm_mix_g, v_w_in=v_w_in, v_gate_bias=v_gate_bias, v_q_norm_g=v_q_norm_g, v_k_norm_g=v_k_norm_g, v_attn_sinks=v_attn_sinks, v_ssm_lambda_re=v_ssm_lambda_re, v_ssm_lambda_im=v_ssm_lambda_im, v_ssm_log_dt=v_ssm_log_dt, v_ssm_b_re=v_ssm_b_re, v_ssm_b_im=v_ssm_b_im, v_ssm_c_re=v_ssm_c_re, v_ssm_c_im=v_ssm_c_im, v_ssm_d=v_ssm_d, v_ssm_glu_w=v_ssm_glu_w, v_ssm_glu_b=v_ssm_glu_b, v_w_attn_branch=v_w_attn_branch, v_w_ssm_branch=v_w_ssm_branch, v_w_out=v_w_out, v_norm_ffn_g=v_norm_ffn_g, v_w_ffn_in=v_w_ffn_in, v_w_ffn_out=v_w_ffn_out)
    weights = {n: given[n] for n in TWIN_WEIGHTS}
    shared = {n: given[n] for n in SHARED_INPUTS}
    per_example = {n: given[n] for n in ['x']}
    grad_fn = _jax.value_and_grad(_loss, argnums=(0, 1))

    def one_microbatch(ex, loss_target):
        ex = dict(ex)
        diff = ex.pop(TWIN_DIFF_INPUT)
        return grad_fn(weights, diff, {**shared, **ex}, loss_target)

    if N_MICROBATCH == 1:
        loss, (grad_w, grad_x) = one_microbatch(per_example, given["loss_target"])
    else:
        def body(carry, xs):
            loss_sum, grad_sum = carry
            l_k, (gw_k, gx_k) = one_microbatch(xs[0], xs[1])
            with _jax.named_scope("update"):
                return (loss_sum + l_k, _jax.tree.map(_jnp.add, grad_sum, gw_k)), gx_k

        init = (_jnp.zeros((), _jnp.float32), _jax.tree.map(_jnp.zeros_like, weights))
        (loss, grad_w), grad_x = _jax.lax.scan(body, init, (per_example, given["loss_target"]))
    with _jax.named_scope("update"):
        delta_w, new_m, new_v = {}, {}, {}
        for n in TWIN_WEIGHTS:
            delta_w[n], new_m[n], new_v[n] = _adamw(weights[n], grad_w[n], given["m_" + n], given["v_" + n])
    return (loss, grad_x, *[grad_w[n] for n in TWIN_WEIGHTS], *[delta_w[n] for n in TWIN_WEIGHTS],
            *[new_m[n] for n in TWIN_WEIGHTS], *[new_v[n] for n in TWIN_WEIGHTS])
```

```python
import functools
import math

import jax
import jax.numpy as jnp
from jax import lax
from jax.experimental import pallas as pl
from jax.experimental.pallas import tpu as pltpu

HEAD_DIM = 64
WINDOW = 128
SSM_GROUP_CH = 16
SSM_LANE_GROUPS = 8
RMS_EPS = 1e-6
ADAM_LR = 0.001
ADAM_B1 = 0.9
ADAM_B2 = 0.999
ADAM_EPS = 1e-08
ADAM_WD = 0.01
ADAM_STEP = 10
NEG_BIG = -1e30
MESH_AXES = ("x", "y", "c")
N_CHIPS = 4
N_DEV = 8
VMEM_LIMIT_BYTES = 56 * 1024 * 1024
BF16 = jnp.bfloat16
F32 = jnp.float32


def _cparams(*semantics):
    return pltpu.CompilerParams(dimension_semantics=semantics, vmem_limit_bytes=VMEM_LIMIT_BYTES)


def _pick(n, target, mult):
    if n <= target:
        return n
    best = None
    for d in range(mult, target + 1, mult):
        if n % d == 0:
            best = d
    assert best is not None, (n, target, mult)
    return best


def _rowmap(fn, ins, outs, *, rows, tm, ncol=1, name):
    n_in = len(ins)
    nrow = rows // tm
    assert nrow * tm == rows

    in_specs = []
    for arr, kind, width, coloff in ins:
        if kind == "row":
            in_specs.append(pl.BlockSpec((tm, width), lambda j, i, o=coloff: (i, o + j)))
        elif kind == "vec":
            in_specs.append(pl.BlockSpec((1, width), lambda j, i, o=coloff: (0, o + j)))
        else:
            nd = arr.ndim
            in_specs.append(pl.BlockSpec(arr.shape, lambda j, i, nd=nd: (0,) * nd))
    out_specs, out_shapes = [], []
    for cols, dtype, kind, width in outs:
        if kind == "row":
            out_specs.append(pl.BlockSpec((tm, width), lambda j, i: (i, j)))
            out_shapes.append(jax.ShapeDtypeStruct((rows, cols), dtype))
        else:
            out_specs.append(pl.BlockSpec((1, width), lambda j, i: (0, j)))
            out_shapes.append(jax.ShapeDtypeStruct((1, cols), dtype))

    def body(*refs):
        i = pl.program_id(1)
        res = fn(*[r[...] for r in refs[:n_in]])
        if not isinstance(res, (tuple, list)):
            res = (res,)
        for (cols, dtype, kind, width), ref, val in zip(outs, refs[n_in:], res):
            if kind == "row":
                ref[...] = val.astype(ref.dtype)
            else:
                @pl.when(i == 0)
                def _():
                    ref[...] = jnp.zeros_like(ref)
                ref[...] += val.astype(ref.dtype)

    res = pl.pallas_call(
        body,
        out_shape=tuple(out_shapes),
        grid=(ncol, nrow),
        in_specs=in_specs,
        out_specs=tuple(out_specs),
        compiler_params=_cparams("parallel", "arbitrary"),
        name=name,
    )(*[a[0] for a in ins])
    return res


def _mm_body(dims, nk, has_add):
    def body(*refs):
        if has_add:
            a_ref, b_ref, add_ref, o_ref = refs[:4]
            rest = refs[4:]
        else:
            a_ref, b_ref, o_ref = refs[:3]
            add_ref = None
            rest = refs[3:]
        part = lax.dot_general(a_ref[...], b_ref[...], (dims, ((), ())), preferred_element_type=F32)
        if nk == 1:
            if add_ref is not None:
                part = part + add_ref[...]
            o_ref[...] = part.astype(o_ref.dtype)
        else:
            acc_ref = rest[0]
            k = pl.program_id(2)

            @pl.when(k == 0)
            def _():
                acc_ref[...] = part

            @pl.when(k > 0)
            def _():
                acc_ref[...] += part

            @pl.when(k == nk - 1)
            def _():
                r = acc_ref[...]
                if add_ref is not None:
                    r = r + add_ref[...]
                o_ref[...] = r.astype(o_ref.dtype)
    return body


class _Weight:
    def __init__(self, arr, layer, kind):
        self.arr, self.layer, self.kind = arr, layer, kind
        self.s, _, self.r, self.c = arr.shape
        self.rows = self.r * (self.s if kind == "row" else 1)
        self.cols = self.c * (self.s if kind == "col" else 1)

    def tiles(self, tr, tc):
        return _pick(self.r, tr, 128), _pick(self.c, tc, 128)

    def index(self, tr, tc):
        layer = self.layer
        if self.kind == "col":
            per = self.c // tc
            return lambda rb, cb: (cb // per, layer, rb, cb % per)
        per = self.r // tr
        return lambda rb, cb: (rb // per, layer, rb % per, cb)


def _shard_index(kind, r, c, tr, tc):
    if kind == "col":
        per = c // tc
        return lambda rb, cb: (cb // per, rb, cb % per)
    per = r // tr
    return lambda rb, cb: (rb // per, rb % per, cb)


def _mm_nn(a, w, *, out_dtype, tm, tn, tk, name, add=None):
    m, k = a.shape
    assert k == w.rows
    tm = _pick(m, tm, 16)
    tk, tn = w.tiles(tk, tn)
    nk = k // tk
    widx = w.index(tk, tn)
    in_specs = [pl.BlockSpec((tm, tk), lambda n, i, kk: (i, kk)),
                pl.BlockSpec((None, None, tk, tn), lambda n, i, kk: widx(kk, n))]
    args = [a, w.arr]
    if add is not None:
        in_specs.append(pl.BlockSpec((tm, tn), lambda n, i, kk: (i, n)))
        args.append(add)
    return pl.pallas_call(
        _mm_body(((1,), (0,)), nk, add is not None),
        out_shape=jax.ShapeDtypeStruct((m, w.cols), out_dtype),
        grid=(w.cols // tn, m // tm, nk),
        in_specs=in_specs,
        out_specs=pl.BlockSpec((tm, tn), lambda n, i, kk: (i, n)),
        scratch_shapes=[pltpu.VMEM((tm, tn), F32)] if nk > 1 else [],
        compiler_params=_cparams("parallel", "parallel", "arbitrary"),
        name=name,
    )(*args)


def _mm_nt(a, w, *, out_dtype, tm, tn, tko, name):
    m, n = a.shape
    assert n == w.cols
    tm = _pick(m, tm, 16)
    tko, tn = w.tiles(tko, tn)
    nk = n // tn
    widx = w.index(tko, tn)
    return pl.pallas_call(
        _mm_body(((1,), (1,)), nk, False),
        out_shape=jax.ShapeDtypeStruct((m, w.rows), out_dtype),
        grid=(w.rows // tko, m // tm, nk),
        in_specs=[pl.BlockSpec((tm, tn), lambda ko, i, nn: (i, nn)),
                  pl.BlockSpec((None, None, tko, tn), lambda ko, i, nn: widx(ko, nn))],
        out_specs=pl.BlockSpec((tm, tko), lambda ko, i, nn: (i, ko)),
        scratch_shapes=[pltpu.VMEM((tm, tko), F32)] if nk > 1 else [],
        compiler_params=_cparams("parallel", "parallel", "arbitrary"),
        name=name,
    )(a, w.arr)


def _mm_tn(a, c, w, *, tm, tn, tko, name):
    m, k = a.shape
    mc, n = c.shape
    assert mc == m and k == w.rows and n == w.cols
    tm = _pick(m, tm, 16)
    tko, tn = w.tiles(tko, tn)
    nk = m // tm
    oidx = _shard_index(w.kind, w.r, w.c, tko, tn)
    return pl.pallas_call(
        _mm_body(((0,), (0,)), nk, False),
        out_shape=jax.ShapeDtypeStruct((w.s, w.r, w.c), F32),
        grid=(k // tko, n // tn, nk),
        in_specs=[pl.BlockSpec((tm, tko), lambda ko, nn, mm: (mm, ko)),
                  pl.BlockSpec((tm, tn), lambda ko, nn, mm: (mm, nn))],
        out_specs=pl.BlockSpec((None, tko, tn), lambda ko, nn, mm: oidx(ko, nn)),
        scratch_shapes=[pltpu.VMEM((tko, tn), F32)] if nk > 1 else [],
        compiler_params=_cparams("parallel", "parallel", "arbitrary"),
        name=name,
    )(a, c)


def _rms(x, g):
    r = lax.rsqrt(jnp.mean(x * x, axis=-1, keepdims=True) + RMS_EPS)
    return x * r * g, r


def _rms_bwd(x, r, g, dy):
    dg = jnp.sum(dy * x * r, axis=0, keepdims=True)
    t = dy * g
    dx = r * t - x * (r * r * r) * jnp.mean(t * x, axis=-1, keepdims=True)
    return dx, dg


def _attn_consts(n_q, n_kv, sinks):
    group = n_q // n_kv
    t = jnp.arange(WINDOW, dtype=jnp.int32)[:, None]
    s = jnp.arange(2 * WINDOW, dtype=jnp.int32)[None, :] - WINDOW
    dist = (t - s).astype(F32)
    valid = (dist >= 0) & (dist < WINDOW)
    slopes = jnp.exp2(-8.0 * jnp.arange(1, n_q + 1, dtype=F32) / n_q)
    bias = jnp.where(valid[None], -slopes[:, None, None] * dist[None], NEG_BIG)
    bias = bias.reshape(n_kv, group * WINDOW, 2 * WINDOW)
    sink = jnp.broadcast_to(sinks.astype(F32).reshape(n_kv, group, 1, 1), (n_kv, group, WINDOW, 1))
    return bias, sink.reshape(n_kv, group * WINDOW, 1)


def _attn_probs(q_ref, kc_ref, kp_ref, vc_ref, vp_ref, qg, kg, sink, bias, first_mask, kv, group):
    sl = slice(kv * HEAD_DIM, (kv + 1) * HEAD_DIM)
    k2 = jnp.concatenate([kp_ref[:, sl], kc_ref[:, sl]], axis=0)
    v2 = jnp.concatenate([vp_ref[:, sl], vc_ref[:, sl]], axis=0)
    k2n, rk = _rms(k2, kg)
    qx, qn, rq = [], [], []
    for g in range(group):
        h = kv * group + g
        x = q_ref[:, h * HEAD_DIM:(h + 1) * HEAD_DIM]
        y, r = _rms(x, qg)
        qx.append(x); qn.append(y); rq.append(r)
    qs = jnp.concatenate(qn, axis=0).astype(BF16)
    k2b = k2n.astype(BF16)
    s = lax.dot_general(qs, k2b, (((1,), (1,)), ((), ())), preferred_element_type=F32) * (HEAD_DIM ** -0.5)
    s = jnp.where(first_mask, NEG_BIG, s + bias)
    m = jnp.maximum(jnp.max(s, axis=-1, keepdims=True), sink)
    p = jnp.exp(s - m)
    esink = jnp.exp(sink - m)
    denom = jnp.sum(p, axis=-1, keepdims=True) + esink
    pn = p / denom
    return dict(k2=k2, rk=rk, k2b=k2b, v2b=v2.astype(BF16), qx=qx, rq=rq, qs=qs, pn=pn, psink=esink / denom)


def _attn_specs(n_q, n_kv):
    aw, kvw = n_q * HEAD_DIM, n_kv * HEAD_DIM
    group = n_q // n_kv
    kblk, vblk = aw // kvw, aw // kvw + 1

    def specs(nb):
        cur = lambda n: jnp.minimum(n, nb - 1)
        prev = lambda n: jnp.maximum(jnp.minimum(n, nb - 1) - 1, 0)
        return [
            pl.BlockSpec((WINDOW, aw), lambda n: (cur(n), 0)),
            pl.BlockSpec((WINDOW, kvw), lambda n: (cur(n), kblk)),
            pl.BlockSpec((WINDOW, kvw), lambda n: (prev(n), kblk)),
            pl.BlockSpec((WINDOW, kvw), lambda n: (cur(n), vblk)),
            pl.BlockSpec((WINDOW, kvw), lambda n: (prev(n), vblk)),
        ]
    const_specs = [
        pl.BlockSpec((1, HEAD_DIM), lambda n: (0, 0)),
        pl.BlockSpec((1, HEAD_DIM), lambda n: (0, 0)),
        pl.BlockSpec((n_kv, group * WINDOW, 1), lambda n: (0, 0, 0)),
        pl.BlockSpec((n_kv, group * WINDOW, 2 * WINDOW), lambda n: (0, 0, 0)),
    ]
    return specs, const_specs


def _attn_fwd(z, qg, kg, sinks, *, n_q, n_kv, name):
    L = z.shape[0]
    nb = L // WINDOW
    aw = n_q * HEAD_DIM
    group = n_q // n_kv
    bias, sink = _attn_consts(n_q, n_kv, sinks)
    specs, const_specs = _attn_specs(n_q, n_kv)

    def body(q_ref, kc_ref, kp_ref, vc_ref, vp_ref, qg_ref, kg_ref, sink_ref, bias_ref, o_ref):
        n = pl.program_id(0)
        col = lax.broadcasted_iota(jnp.int32, (group * WINDOW, 2 * WINDOW), 1)
        first_mask = jnp.logical_and(n == 0, col < WINDOW)
        for kv in range(n_kv):
            a = _attn_probs(q_ref, kc_ref, kp_ref, vc_ref, vp_ref, qg_ref[...], kg_ref[...],
                            sink_ref[kv], bias_ref[kv], first_mask, kv, group)
            o = jnp.dot(a["pn"].astype(BF16), a["v2b"], preferred_element_type=F32)
            for g in range(group):
                h = kv * group + g
                o_ref[:, h * HEAD_DIM:(h + 1) * HEAD_DIM] = o[g * WINDOW:(g + 1) * WINDOW].astype(o_ref.dtype)

    return pl.pallas_call(
        body,
        out_shape=jax.ShapeDtypeStruct((L, aw), BF16),
        grid=(nb,),
        in_specs=specs(nb) + const_specs,
        out_specs=pl.BlockSpec((WINDOW, aw), lambda n: (n, 0)),
        compiler_params=_cparams("parallel"),
        name=name,
    )(z, z, z, z, z, qg, kg, sink, bias)


def _attn_bwd(z, do, qg, kg, sinks, *, n_q, n_kv, name):
    L = z.shape[0]
    nb = L // WINDOW
    aw, kvw = n_q * HEAD_DIM, n_kv * HEAD_DIM
    group = n_q // n_kv
    bias, sink = _attn_consts(n_q, n_kv, sinks)
    specs, const_specs = _attn_specs(n_q, n_kv)
    scale = HEAD_DIM ** -0.5

    def body(q_ref, kc_ref, kp_ref, vc_ref, vp_ref, do_ref, qg_ref, kg_ref, sink_ref, bias_ref,
             dq_ref, dkv_ref, dqg_ref, dkg_ref, dsink_ref, carry_ref):
        n = pl.program_id(0)

        @pl.when(n == 0)
        def _():
            dqg_ref[...] = jnp.zeros_like(dqg_ref)
            dkg_ref[...] = jnp.zeros_like(dkg_ref)
            dsink_ref[...] = jnp.zeros_like(dsink_ref)
            carry_ref[...] = jnp.zeros_like(carry_ref)

        @pl.when(n < nb)
        def _():
            col = lax.broadcasted_iota(jnp.int32, (group * WINDOW, 2 * WINDOW), 1)
            first_mask = jnp.logical_and(n == 0, col < WINDOW)
            head_lane = lax.broadcasted_iota(jnp.int32, (1, n_q), 1)
            qg, kg = qg_ref[...], kg_ref[...]
            dqg = jnp.zeros((1, HEAD_DIM), F32)
            dkg = jnp.zeros((1, HEAD_DIM), F32)
            dsink = jnp.zeros((1, n_q), F32)
            for kv in range(n_kv):
                a = _attn_probs(q_ref, kc_ref, kp_ref, vc_ref, vp_ref, qg, kg,
                                sink_ref[kv], bias_ref[kv], first_mask, kv, group)
                pn = a["pn"]
                dos = jnp.concatenate(
                    [do_ref[:, (kv * group + g) * HEAD_DIM:(kv * group + g + 1) * HEAD_DIM] for g in range(group)],
                    axis=0).astype(BF16)
                dpn = lax.dot_general(dos, a["v2b"], (((1,), (1,)), ((), ())), preferred_element_type=F32)
                dv2 = lax.dot_general(pn.astype(BF16), dos, (((0,), (0,)), ((), ())), preferred_element_type=F32)
                delta = jnp.sum(pn * dpn, axis=-1, keepdims=True)
                ds = (pn * (dpn - delta)).astype(BF16)
                dsk = -a["psink"] * delta
                dqn = lax.dot_general(ds, a["k2b"], (((1,), (0,)), ((), ())), preferred_element_type=F32) * scale
                dk2n = lax.dot_general(ds, a["qs"], (((0,), (0,)), ((), ())), preferred_element_type=F32) * scale
                for g in range(group):
                    h = kv * group + g
                    rows = slice(g * WINDOW, (g + 1) * WINDOW)
                    dx, dgq = _rms_bwd(a["qx"][g], a["rq"][g], qg, dqn[rows])
                    dq_ref[:, h * HEAD_DIM:(h + 1) * HEAD_DIM] = dx.astype(dq_ref.dtype)
                    dqg = dqg + dgq
                    dsink = dsink + jnp.where(head_lane == h, jnp.sum(dsk[rows], axis=0, keepdims=True), 0.0)
                dk2, dgk = _rms_bwd(a["k2"], a["rk"], kg, dk2n)
                dkg = dkg + dgk
                ksl = slice(kv * HEAD_DIM, (kv + 1) * HEAD_DIM)
                vsl = slice(kvw + kv * HEAD_DIM, kvw + (kv + 1) * HEAD_DIM)
                dkv_ref[:, ksl] = (carry_ref[:, ksl] + dk2[:WINDOW]).astype(dkv_ref.dtype)
                dkv_ref[:, vsl] = (carry_ref[:, vsl] + dv2[:WINDOW]).astype(dkv_ref.dtype)
                carry_ref[:, ksl] = dk2[WINDOW:]
                carry_ref[:, vsl] = dv2[WINDOW:]
            dqg_ref[...] += dqg
            dkg_ref[...] += dkg
            dsink_ref[...] += dsink

        @pl.when(n == nb)
        def _():
            dkv_ref[...] = carry_ref[...].astype(dkv_ref.dtype)

    in_specs = specs(nb) + [pl.BlockSpec((WINDOW, aw), lambda n: (jnp.minimum(n, nb - 1), 0))] + const_specs
    return pl.pallas_call(
        body,
        out_shape=(jax.ShapeDtypeStruct((L, aw), BF16), jax.ShapeDtypeStruct((L, 2 * kvw), BF16),
                   jax.ShapeDtypeStruct((1, HEAD_DIM), F32), jax.ShapeDtypeStruct((1, HEAD_DIM), F32),
                   jax.ShapeDtypeStruct((1, n_q), F32)),
        grid=(nb + 1,),
        in_specs=in_specs,
        out_specs=(pl.BlockSpec((WINDOW, aw), lambda n: (jnp.minimum(n, nb - 1), 0)),
                   pl.BlockSpec((WINDOW, 2 * kvw), lambda n: (jnp.maximum(n - 1, 0), 0)),
                   pl.BlockSpec((1, HEAD_DIM), lambda n: (0, 0)),
                   pl.BlockSpec((1, HEAD_DIM), lambda n: (0, 0)),
                   pl.BlockSpec((1, n_q), lambda n: (0, 0))),
        scratch_shapes=[pltpu.VMEM((WINDOW, 2 * kvw), F32)],
        compiler_params=_cparams("arbitrary"),
        name=name,
    )(z, z, z, z, z, do, qg, kg, sink, bias)


def _cmul(ar, ai, br, bi):
    return ar * br - ai * bi, ar * bi + ai * br


def _scan_consts(ar, ai, n, reverse):
    row = lax.broadcasted_iota(jnp.int32, (8, n), 0)
    mults = []
    pr, pi = ar, ai
    for k in (1, 2, 4):
        keep = (row < 8 - k) if reverse else (row >= k)
        mults.append(((8 - k) if reverse else k, jnp.where(keep, pr, 0.0), jnp.where(keep, pi, 0.0)))
        pr, pi = _cmul(pr, pi, pr, pi)
    pwr = jnp.zeros((8, n), F32)
    pwi = jnp.zeros((8, n), F32)
    pr, pi = ar, ai
    for e in range(1, 9):
        sel = (row == 8 - e) if reverse else (row == e - 1)
        pwr = jnp.where(sel, pr, pwr)
        pwi = jnp.where(sel, pi, pwi)
        pr, pi = _cmul(pr, pi, ar, ai)
    return mults, pwr, pwi


def _scan8(xr, xi, mults, pwr, pwi, cr, ci):
    for shift, mr, mi in mults:
        sr = pltpu.roll(xr, shift, 0)
        si = pltpu.roll(xi, shift, 0)
        xr, xi = xr + mr * sr - mi * si, xi + mr * si + mi * sr
    return xr + pwr * cr - pwi * ci, xi + pwr * ci + pwi * cr


def _blockdiag(x):
    g, a, b = x.shape
    j = g // SSM_LANE_GROUPS
    eye = jnp.eye(SSM_LANE_GROUPS, dtype=x.dtype)
    y = x.reshape(j, SSM_LANE_GROUPS, a, 1, b) * eye[None, :, None, :, None]
    return y.reshape(j, SSM_LANE_GROUPS * a, SSM_LANE_GROUPS * b)


def _blockdiag_extract(y, a, b):
    j = y.shape[0]
    y = y.reshape(j, SSM_LANE_GROUPS, a, SSM_LANE_GROUPS, b)
    return jnp.einsum("jgahb,gh->jgab", y, jnp.eye(SSM_LANE_GROUPS, dtype=y.dtype)).reshape(j * SSM_LANE_GROUPS, a, b)


def _ssm_disc(lr, li, ldt, brt, bit):
    dt = jnp.exp(ldt)
    mag = jnp.exp(lr * dt)
    ar = mag * jnp.cos(li * dt)
    ai = mag * jnp.sin(li * dt)
    den = lr * lr + li * li
    fr = ((ar - 1.0) * lr + ai * li) / den
    fi = (ai * lr - (ar - 1.0) * li) / den
    bbr = fr[:, None, :] * brt - fi[:, None, :] * bit
    bbi = fr[:, None, :] * bit + fi[:, None, :] * brt
    return ar, ai, bbr, bbi


def _ssm_prep(lr, li, ldt, brt, bit, *, name):
    g, h, p = brt.shape

    def body(lr_ref, li_ref, ldt_ref, brt_ref, bit_ref, ar_ref, ai_ref, bbr_ref, bbi_ref):
        ar, ai, bbr, bbi = _ssm_disc(lr_ref[...], li_ref[...], ldt_ref[...], brt_ref[...], bit_ref[...])
        ar_ref[...] = ar
        ai_ref[...] = ai
        bbr_ref[...] = bbr
        bbi_ref[...] = bbi

    gp = jax.ShapeDtypeStruct((g, p), F32)
    ghp = jax.ShapeDtypeStruct((g, h, p), F32)
    return pl.pallas_call(body, out_shape=(gp, gp, ghp, ghp), name=name)(lr, li, ldt, brt, bit)


def _ssm_prep_bwd(lr, li, ldt, brt, bit, dar, dai, dbbr, dbbi, *, name):
    g, h, p = brt.shape

    def body(lr_ref, li_ref, ldt_ref, brt_ref, bit_ref, dar_ref, dai_ref, dbbr_ref, dbbi_ref,
             dlr_ref, dli_ref, dldt_ref, dbrt_ref, dbit_ref):
        _, vjp = jax.vjp(_ssm_disc, lr_ref[...], li_ref[...], ldt_ref[...], brt_ref[...], bit_ref[...])
        dlr, dli, dldt, dbrt, dbit = vjp((dar_ref[...], dai_ref[...], dbbr_ref[...], dbbi_ref[...]))
        dlr_ref[...] = dlr
        dli_ref[...] = dli
        dldt_ref[...] = dldt
        dbrt_ref[...] = dbrt
        dbit_ref[...] = dbit

    gp = jax.ShapeDtypeStruct((g, p), F32)
    ghp = jax.ShapeDtypeStruct((g, h, p), F32)
    return pl.pallas_call(body, out_shape=(gp, gp, jax.ShapeDtypeStruct((g, 1), F32), ghp, ghp), name=name)(
        lr, li, ldt, brt, bit, dar, dai, dbbr, dbbi)


def _ssm_specs(tc, nlanes, nch, u_colblk, chunk_of):
    return [
        pl.BlockSpec((tc, nch), lambda j, c: (chunk_of(c), u_colblk + j)),
        pl.BlockSpec((1, nlanes), lambda j, c: (0, j)),
        pl.BlockSpec((1, nlanes), lambda j, c: (0, j)),
        pl.BlockSpec((None, nch, nlanes), lambda j, c: (j, 0, 0)),
        pl.BlockSpec((None, nch, nlanes), lambda j, c: (j, 0, 0)),
        pl.BlockSpec((None, nlanes, nch), lambda j, c: (j, 0, 0)),
        pl.BlockSpec((None, nlanes, nch), lambda j, c: (j, 0, 0)),
        pl.BlockSpec((1, nch), lambda j, c: (0, j)),
    ]


def _ssm_fwd(z, ar, ai, bblk_r, bblk_i, cblk_r, cblk_i, d, *, u_col, tc, name):
    L = z.shape[0]
    nj, nch, nlanes = bblk_r.shape
    w = nj * nch
    nc = L // tc
    ng = tc // 8

    def body(u_ref, ar_ref, ai_ref, br_ref, bi_ref, cr_ref, ci_ref, d_ref, y_ref, s0r_ref, s0i_ref,
             xr_ref, xi_ref, carr_ref, cari_ref):
        c = pl.program_id(1)

        @pl.when(c == 0)
        def _():
            carr_ref[...] = jnp.zeros_like(carr_ref)
            cari_ref[...] = jnp.zeros_like(cari_ref)

        s0r_ref[...] = carr_ref[...]
        s0i_ref[...] = cari_ref[...]
        u = u_ref[...]
        ub = u.astype(BF16)
        xr_ref[...] = jnp.dot(ub, br_ref[...].astype(BF16), preferred_element_type=F32)
        xi_ref[...] = jnp.dot(ub, bi_ref[...].astype(BF16), preferred_element_type=F32)
        mults, pwr, pwi = _scan_consts(ar_ref[...], ai_ref[...], nlanes, False)

        def step(g, carry):
            rows = pl.ds(pl.multiple_of(g * 8, 8), 8)
            sr, si = _scan8(xr_ref[rows, :], xi_ref[rows, :], mults, pwr, pwi, carry[0], carry[1])
            xr_ref[rows, :] = sr
            xi_ref[rows, :] = si
            return sr[7:8], si[7:8]

        cr, ci = lax.fori_loop(0, ng, step, (carr_ref[...], cari_ref[...]))
        carr_ref[...] = cr
        cari_ref[...] = ci
        y = (jnp.dot(xr_ref[...].astype(BF16), cr_ref[...].astype(BF16), preferred_element_type=F32)
             - jnp.dot(xi_ref[...].astype(BF16), ci_ref[...].astype(BF16), preferred_element_type=F32)
             + d_ref[...] * u)
        y_ref[...] = y

    state = jax.ShapeDtypeStruct((nc, 1, nj * nlanes), F32)
    state_spec = pl.BlockSpec((None, 1, nlanes), lambda j, c: (c, 0, j))
    return pl.pallas_call(
        body,
        out_shape=(jax.ShapeDtypeStruct((L, w), F32), state, state),
        grid=(nj, nc),
        in_specs=_ssm_specs(tc, nlanes, nch, u_col // nch, lambda c: c),
        out_specs=(pl.BlockSpec((tc, nch), lambda j, c: (c, j)), state_spec, state_spec),
        scratch_shapes=[pltpu.VMEM((tc, nlanes), F32), pltpu.VMEM((tc, nlanes), F32),
                        pltpu.VMEM((1, nlanes), F32), pltpu.VMEM((1, nlanes), F32)],
        compiler_params=_cparams("parallel", "arbitrary"),
        name=name,
    )(z, ar, ai, bblk_r, bblk_i, cblk_r, cblk_i, d)


def _ssm_bwd(z, dy, s0r, s0i, ar, ai, bblk_r, bblk_i, cblk_r, cblk_i, d, *, u_col, tc, name):
    L = z.shape[0]
    nj, nch, nlanes = bblk_r.shape
    w = nj * nch
    nc = L // tc
    ng = tc // 8
    chunk_of = lambda c: nc - 1 - c

    def body(u_ref, ar_ref, ai_ref, br_ref, bi_ref, cr_ref, ci_ref, d_ref, dy_ref, s0r_ref, s0i_ref,
             du_ref, dbr_ref, dbi_ref, dcr_ref, dci_ref, dar_ref, dai_ref, dd_ref,
             sr_ref, si_ref, pr_ref, pi_ref, lr_ref, li_ref, carr_ref, cari_ref):
        c = pl.program_id(1)

        @pl.when(c == 0)
        def _():
            for ref in (dbr_ref, dbi_ref, dcr_ref, dci_ref, dar_ref, dai_ref, dd_ref, carr_ref, cari_ref):
                ref[...] = jnp.zeros_like(ref)

        u = u_ref[...]
        dyv = dy_ref[...]
        ub = u.astype(BF16)
        dyb = dyv.astype(BF16)
        brb = br_ref[...].astype(BF16)
        bib = bi_ref[...].astype(BF16)
        crb = cr_ref[...].astype(BF16)
        cib = ci_ref[...].astype(BF16)
        a_r, a_i = ar_ref[...], ai_ref[...]

        sr_ref[...] = jnp.dot(ub, brb, preferred_element_type=F32)
        si_ref[...] = jnp.dot(ub, bib, preferred_element_type=F32)
        mults, pwr, pwi = _scan_consts(a_r, a_i, nlanes, False)
        row = lax.broadcasted_iota(jnp.int32, (8, nlanes), 0)

        def fstep(g, carry):
            rows = pl.ds(pl.multiple_of(g * 8, 8), 8)
            sr, si = _scan8(sr_ref[rows, :], si_ref[rows, :], mults, pwr, pwi, carry[0], carry[1])
            sr_ref[rows, :] = sr
            si_ref[rows, :] = si
            pr_ref[rows, :] = jnp.where(row == 0, carry[0], pltpu.roll(sr, 1, 0))
            pi_ref[rows, :] = jnp.where(row == 0, carry[1], pltpu.roll(si, 1, 0))
            return sr[7:8], si[7:8]

        lax.fori_loop(0, ng, fstep, (s0r_ref[...], s0i_ref[...]))

        nt = (((1,), (1,)), ((), ()))
        lr_ref[...] = lax.dot_general(dyb, crb, nt, preferred_element_type=F32)
        li_ref[...] = -lax.dot_general(dyb, cib, nt, preferred_element_type=F32)
        rmults, rpwr, rpwi = _scan_consts(a_r, -a_i, nlanes, True)

        def rstep(gg, carry):
            cr, ci, acc_r, acc_i = carry
            rows = pl.ds(pl.multiple_of((ng - 1 - gg) * 8, 8), 8)
            lr, li = _scan8(lr_ref[rows, :], li_ref[rows, :], rmults, rpwr, rpwi, cr, ci)
            lr_ref[rows, :] = lr
            li_ref[rows, :] = li
            pr, pi = pr_ref[rows, :], pi_ref[rows, :]
            return lr[0:1], li[0:1], acc_r + lr * pr + li * pi, acc_i + li * pr - lr * pi

        zero8 = jnp.zeros((8, nlanes), F32)
        cr, ci, acc_r, acc_i = lax.fori_loop(0, ng, rstep, (carr_ref[...], cari_ref[...], zero8, zero8))
        carr_ref[...] = cr
        cari_ref[...] = ci
        dar_ref[...] += jnp.sum(acc_r, axis=0, keepdims=True)
        dai_ref[...] += jnp.sum(acc_i, axis=0, keepdims=True)

        tn = (((0,), (0,)), ((), ()))
        lrb = lr_ref[...].astype(BF16)
        lib = li_ref[...].astype(BF16)
        dcr_ref[...] += lax.dot_general(sr_ref[...].astype(BF16), dyb, tn, preferred_element_type=F32)
        dci_ref[...] -= lax.dot_general(si_ref[...].astype(BF16), dyb, tn, preferred_element_type=F32)
        dbr_ref[...] += lax.dot_general(ub, lrb, tn, preferred_element_type=F32)
        dbi_ref[...] += lax.dot_general(ub, lib, tn, preferred_element_type=F32)
        du = (lax.dot_general(lrb, brb, nt, preferred_element_type=F32)
              + lax.dot_general(lib, bib, nt, preferred_element_type=F32)
              + d_ref[...] * dyv)
        du_ref[...] = du.astype(du_ref.dtype)
        dd_ref[...] += jnp.sum(dyv * u, axis=0, keepdims=True)

    state_spec = pl.BlockSpec((None, 1, nlanes), lambda j, c: (chunk_of(c), 0, j))
    bshape = jax.ShapeDtypeStruct((nj, nch, nlanes), F32)
    cshape = jax.ShapeDtypeStruct((nj, nlanes, nch), F32)
    ashape = jax.ShapeDtypeStruct((1, nj * nlanes), F32)
    bspec = pl.BlockSpec((None, nch, nlanes), lambda j, c: (j, 0, 0))
    cspec = pl.BlockSpec((None, nlanes, nch), lambda j, c: (j, 0, 0))
    aspec = pl.BlockSpec((1, nlanes), lambda j, c: (0, j))
    big = pltpu.VMEM((tc, nlanes), F32)
    return pl.pallas_call(
        body,
        out_shape=(jax.ShapeDtypeStruct((L, w), BF16), bshape, bshape, cshape, cshape, ashape, ashape,
                   jax.ShapeDtypeStruct((1, w), F32)),
        grid=(nj, nc),
        in_specs=_ssm_specs(tc, nlanes, nch, u_col // nch, chunk_of)
        + [pl.BlockSpec((tc, nch), lambda j, c: (chunk_of(c), j)), state_spec, state_spec],
        out_specs=(pl.BlockSpec((tc, nch), lambda j, c: (chunk_of(c), j)), bspec, bspec, cspec, cspec, aspec, aspec,
                   pl.BlockSpec((1, nch), lambda j, c: (0, j))),
        scratch_shapes=[big, big, big, big, big, big, pltpu.VMEM((1, nlanes), F32), pltpu.VMEM((1, nlanes), F32)],
        compiler_params=_cparams("parallel", "arbitrary"),
        name=name,
    )(z, ar, ai, bblk_r, bblk_i, cblk_r, cblk_i, d, dy, s0r, s0i)


def _rmsnorm_rows(x, g):
    return x * lax.rsqrt(jnp.mean(x * x, axis=-1, keepdims=True) + RMS_EPS) * g


def _glu_out(y_raw, pre, b):
    yg = jax.nn.gelu(y_raw)
    return yg * jax.nn.sigmoid(pre + b)


def _gate_merge(za, zs, ba, bs, a, bm):
    return jax.nn.sigmoid(za + ba) * a + jax.nn.sigmoid(zs + bs) * bm


def _swiglu(g, u):
    return jax.nn.silu(g) * u


def _row_tile(width_bytes_per_row, rows):
    budget = VMEM_LIMIT_BYTES // 3
    t = max(8, min(1024, budget // (2 * max(width_bytes_per_row, 1))))
    return _pick(rows, t, 16)


def _ssm_params(p, prefix):
    g, pst = p["lam_re"].shape
    ar, ai, bbr, bbi = _ssm_prep(p["lam_re"], p["lam_im"], p["log_dt"], p["b_re_t"], p["b_im_t"], name=prefix + "_ssm_prep")
    return dict(ar=ar.reshape(1, g * pst), ai=ai.reshape(1, g * pst),
                bblk_r=_blockdiag(bbr), bblk_i=_blockdiag(bbi),
                cblk_r=_blockdiag(jnp.swapaxes(p["c_re"], 1, 2)), cblk_i=_blockdiag(jnp.swapaxes(p["c_im"], 1, 2)))


def _layer_fwd(x, p, dims, prefix):
    t, d = x.shape
    aw, kvw, sw, ff = dims["aw"], dims["kvw"], dims["sw"], dims["ff"]
    off_u = aw + 2 * kvw
    off_g = off_u + sw
    gblk = _pick(d, 512, 128)
    assert off_g % gblk == 0 and off_u % (SSM_LANE_GROUPS * SSM_GROUP_CH) == 0
    sv = {"x": x}

    h, = _rowmap(_rmsnorm_rows, [(x, "row", d, 0), (p["norm_mix_g"], "vec", d, 0)], [(d, BF16, "row", d)],
                 rows=t, tm=_row_tile(6 * d, t), name=prefix + "_norm_mix")
    z = _mm_nn(h, p["w_in"], out_dtype=F32, tm=512, tn=1664, tk=2048, name=prefix + "_mm_in")
    ya = _attn_fwd(z, p["q_norm_g"], p["k_norm_g"], p["attn_sinks"], n_q=dims["n_q"], n_kv=dims["n_kv"],
                   name=prefix + "_attn_fwd")
    sp = _ssm_params(p, prefix)
    y_raw, s0r, s0i = _ssm_fwd(z, sp["ar"], sp["ai"], sp["bblk_r"], sp["bblk_i"], sp["cblk_r"], sp["cblk_i"], p["ssm_d"],
                               u_col=off_u, tc=dims["tc"], name=prefix + "_ssm_fwd")
    yg, = _rowmap(jax.nn.gelu, [(y_raw, "row", sw, 0)], [(sw, BF16, "row", sw)],
                  rows=t, tm=_row_tile(6 * sw, t), name=prefix + "_gelu")
    pre = _mm_nn(yg, p["ssm_glu_w"], out_dtype=F32, tm=1024, tn=1024, tk=256, name=prefix + "_mm_glu")
    y2, = _rowmap(_glu_out, [(y_raw, "row", sw, 0), (pre, "row", sw, 0), (p["ssm_glu_b"], "vec", sw, 0)],
                  [(sw, BF16, "row", sw)], rows=t, tm=_row_tile(10 * sw, t), name=prefix + "_glu_out")
    a = _mm_nn(ya, p["w_attn_branch"], out_dtype=F32, tm=1024, tn=512, tk=1024, name=prefix + "_mm_ab")
    bm = _mm_nn(y2, p["w_ssm_branch"], out_dtype=F32, tm=1024, tn=512, tk=1024, name=prefix + "_mm_sb")
    ngb = d // gblk
    merged, = _rowmap(
        _gate_merge,
        [(z, "row", gblk, off_g // gblk), (z, "row", gblk, off_g // gblk + ngb),
         (p["gate_bias"], "vec", gblk, 0), (p["gate_bias"], "vec", gblk, ngb),
         (a, "row", gblk, 0), (bm, "row", gblk, 0)],
        [(d, BF16, "row", gblk)], rows=t, tm=_row_tile(18 * gblk, t), ncol=ngb, name=prefix + "_gate")
    x1 = _mm_nn(merged, p["w_out"], out_dtype=F32, tm=512, tn=2048, tk=512, name=prefix + "_mm_out", add=x)
    h2, = _rowmap(_rmsnorm_rows, [(x1, "row", d, 0), (p["norm_ffn_g"], "vec", d, 0)], [(d, BF16, "row", d)],
                  rows=t, tm=_row_tile(6 * d, t), name=prefix + "_norm_ffn")
    gu = _mm_nn(h2, p["w_ffn_in"], out_dtype=F32, tm=512, tn=1408, tk=2048, name=prefix + "_mm_ffn_in")
    fblk = _pick(ff, 1408, 128)
    nfb = ff // fblk
    act, = _rowmap(_swiglu, [(gu, "row", fblk, 0), (gu, "row", fblk, nfb)], [(ff, BF16, "row", fblk)],
                   rows=t, tm=_row_tile(10 * fblk, t), ncol=nfb, name=prefix + "_swiglu")
    x2 = _mm_nn(act, p["w_ffn_out"], out_dtype=F32, tm=512, tn=1024, tk=1408, name=prefix + "_mm_ffn_out", add=x1)
    sv.update(h=h, z=z, ya=ya, sp=sp, y_raw=y_raw, s0r=s0r, s0i=s0i, yg=yg, pre=pre, y2=y2, a=a, bm=bm,
              merged=merged, x1=x1, h2=h2, gu=gu, act=act)
    return x2, sv


def _layer_bwd(dx2, dx2b, sv, p, dims, prefix):
    t, d = dx2.shape
    aw, kvw, sw, ff = dims["aw"], dims["kvw"], dims["sw"], dims["ff"]
    off_u = aw + 2 * kvw
    off_g = off_u + sw
    gblk = _pick(d, 512, 128)
    ngb = d // gblk
    fblk = _pick(ff, 1408, 128)
    nfb = ff // fblk
    g = {}

    dact = _mm_nt(dx2b, p["w_ffn_out"], out_dtype=F32, tm=512, tn=2048, tko=1408, name=prefix + "_mm_dact")
    g["w_ffn_out"] = _mm_tn(sv["act"], dx2b, p["w_ffn_out"], tm=512, tn=1024, tko=1408, name=prefix + "_mm_dw_ffn_out")

    def swiglu_bwd(gg, uu, da):
        _, vjp = jax.vjp(_swiglu, gg, uu)
        return vjp(da)

    dgu_g, dgu_u = _rowmap(swiglu_bwd, [(sv["gu"], "row", fblk, 0), (sv["gu"], "row", fblk, nfb), (dact, "row", fblk, 0)],
                           [(ff, BF16, "row", fblk), (ff, BF16, "row", fblk)],
                           rows=t, tm=_row_tile(16 * fblk, t), ncol=nfb, name=prefix + "_swiglu_bwd")
    dgu = jnp.concatenate([dgu_g, dgu_u], axis=1)
    dh2 = _mm_nt(dgu, p["w_ffn_in"], out_dtype=F32, tm=512, tn=1408, tko=2048, name=prefix + "_mm_dh2")
    g["w_ffn_in"] = _mm_tn(sv["h2"], dgu, p["w_ffn_in"], tm=512, tn=1408, tko=1024, name=prefix + "_mm_dw_ffn_in")

    def norm_bwd(xx, gg, dh, dres):
        _, vjp = jax.vjp(_rmsnorm_rows, xx, gg)
        dxx, dgg = vjp(dh)
        dxx = dxx + dres
        return dxx, dxx, dgg

    dx1, dx1b, g["norm_ffn_g"] = _rowmap(
        norm_bwd, [(sv["x1"], "row", d, 0), (p["norm_ffn_g"], "vec", d, 0), (dh2, "row", d, 0), (dx2, "row", d, 0)],
        [(d, F32, "row", d), (d, BF16, "row", d), (d, F32, "acc", d)],
        rows=t, tm=_row_tile(22 * d, t), name=prefix + "_norm_ffn_bwd")

    dmerged = _mm_nt(dx1b, p["w_out"], out_dtype=F32, tm=1024, tn=2048, tko=512, name=prefix + "_mm_dmerged")
    g["w_out"] = _mm_tn(sv["merged"], dx1b, p["w_out"], tm=1024, tn=2048, tko=512, name=prefix + "_mm_dw_out")

    def gate_bwd(za, zs, ba, bs, aa, bb, dm):
        _, vjp = jax.vjp(_gate_merge, za, zs, ba, bs, aa, bb)
        dza, dzs, dba, dbs, daa, dbb = vjp(dm)
        return daa, dbb, dza, dzs, dba, dbs

    z = sv["z"]
    da, dbm, dza, dzs, dba, dbs = _rowmap(
        gate_bwd,
        [(z, "row", gblk, off_g // gblk), (z, "row", gblk, off_g // gblk + ngb),
         (p["gate_bias"], "vec", gblk, 0), (p["gate_bias"], "vec", gblk, ngb),
         (sv["a"], "row", gblk, 0), (sv["bm"], "row", gblk, 0), (dmerged, "row", gblk, 0)],
        [(d, BF16, "row", gblk), (d, BF16, "row", gblk), (d, BF16, "row", gblk), (d, BF16, "row", gblk),
         (d, F32, "acc", gblk), (d, F32, "acc", gblk)],
        rows=t, tm=_row_tile(32 * gblk, t), ncol=ngb, name=prefix + "_gate_bwd")
    g["gate_bias"] = jnp.concatenate([dba, dbs], axis=1)
    dya = _mm_nt(da, p["w_attn_branch"], out_dtype=F32, tm=1024, tn=512, tko=1024, name=prefix + "_mm_dya")
    g["w_attn_branch"] = _mm_tn(sv["ya"], da, p["w_attn_branch"], tm=1024, tn=512, tko=1024, name=prefix + "_mm_dw_ab")
    dy2 = _mm_nt(dbm, p["w_ssm_branch"], out_dtype=F32, tm=1024, tn=512, tko=1024, name=prefix + "_mm_dy2")
    g["w_ssm_branch"] = _mm_tn(sv["y2"], dbm, p["w_ssm_branch"], tm=1024, tn=512, tko=1024, name=prefix + "_mm_dw_sb")

    def glu_bwd(y_raw, pre, b, dy):
        yg = jax.nn.gelu(y_raw)
        _, vjp = jax.vjp(lambda a_, b_, c_: a_ * jax.nn.sigmoid(b_ + c_), yg, pre, b)
        dyg, dpre, db = vjp(dy)
        return dyg, dpre, db

    dyg_direct, dpre, g["ssm_glu_b"] = _rowmap(
        glu_bwd, [(sv["y_raw"], "row", sw, 0), (sv["pre"], "row", sw, 0), (p["ssm_glu_b"], "vec", sw, 0), (dy2, "row", sw, 0)],
        [(sw, F32, "row", sw), (sw, BF16, "row", sw), (sw, F32, "acc", sw)],
        rows=t, tm=_row_tile(24 * sw, t), name=prefix + "_glu_bwd")
    dyg2 = _mm_nt(dpre, p["ssm_glu_w"], out_dtype=F32, tm=1024, tn=1024, tko=256, name=prefix + "_mm_dyg")
    g["ssm_glu_w"] = _mm_tn(sv["yg"], dpre, p["ssm_glu_w"], tm=1024, tn=1024, tko=256, name=prefix + "_mm_dw_glu")

    def gelu_bwd(y_raw, d1, d2):
        _, vjp = jax.vjp(jax.nn.gelu, y_raw)
        return vjp(d1 + d2)[0]

    dy_raw, = _rowmap(gelu_bwd, [(sv["y_raw"], "row", sw, 0), (dyg_direct, "row", sw, 0), (dyg2, "row", sw, 0)],
                      [(sw, F32, "row", sw)], rows=t, tm=_row_tile(20 * sw, t), name=prefix + "_gelu_bwd")
    sp = sv["sp"]
    du, dbr, dbi, dcr, dci, dar, dai, g["ssm_d"] = _ssm_bwd(
        z, dy_raw, sv["s0r"], sv["s0i"], sp["ar"], sp["ai"], sp["bblk_r"], sp["bblk_i"], sp["cblk_r"], sp["cblk_i"],
        p["ssm_d"], u_col=off_u, tc=dims["tc"], name=prefix + "_ssm_bwd")
    ngr, pst = p["lam_re"].shape
    hch = SSM_GROUP_CH
    dlr, dli, dldt, dbrt, dbit = _ssm_prep_bwd(
        p["lam_re"], p["lam_im"], p["log_dt"], p["b_re_t"], p["b_im_t"],
        dar.reshape(ngr, pst), dai.reshape(ngr, pst), _blockdiag_extract(dbr, hch, pst), _blockdiag_extract(dbi, hch, pst),
        name=prefix + "_ssm_prep_bwd")
    g.update(ssm_lambda_re=dlr, ssm_lambda_im=dli, ssm_log_dt=dldt.reshape(ngr),
             ssm_b_re=jnp.swapaxes(dbrt, 1, 2), ssm_b_im=jnp.swapaxes(dbit, 1, 2),
             ssm_c_re=jnp.swapaxes(_blockdiag_extract(dcr, pst, hch), 1, 2),
             ssm_c_im=jnp.swapaxes(_blockdiag_extract(dci, pst, hch), 1, 2))

    dq, dkv, g["q_norm_g"], g["k_norm_g"], g["attn_sinks"] = _attn_bwd(
        z, dya, p["q_norm_g"], p["k_norm_g"], p["attn_sinks"], n_q=dims["n_q"], n_kv=dims["n_kv"], name=prefix + "_attn_bwd")

    dz = jnp.concatenate([dq, dkv, du, dza, dzs], axis=1)
    dh = _mm_nt(dz, p["w_in"], out_dtype=F32, tm=512, tn=1664, tko=2048, name=prefix + "_mm_dh")
    g["w_in"] = _mm_tn(sv["h"], dz, p["w_in"], tm=512, tn=1664, tko=1024, name=prefix + "_mm_dw_in")
    dx, dxb, g["norm_mix_g"] = _rowmap(
        norm_bwd, [(sv["x"], "row", d, 0), (p["norm_mix_g"], "vec", d, 0), (dh, "row", d, 0), (dx1, "row", d, 0)],
        [(d, F32, "row", d), (d, BF16, "row", d), (d, F32, "acc", d)],
        rows=t, tm=_row_tile(22 * d, t), name=prefix + "_norm_mix_bwd")
    return dx, dxb, g


def _loss_and_grad(y, target):
    t, d = y.shape

    def fn(yy, tt):
        e = yy - tt
        dy = e * (1.0 / d)
        return dy, dy, jnp.sum(e * e, keepdims=True).reshape(1, 1)

    dy, dyb, sq = _rowmap(fn, [(y, "row", d, 0), (target, "row", d, 0)],
                          [(d, F32, "row", d), (d, BF16, "row", d), (1, F32, "acc", 1)],
                          rows=t, tm=_row_tile(14 * d, t), name="loss")
    return sq, dy, dyb


def _local_step(x, target, params, dims):
    saved = []
    h = x
    for l, p in enumerate(params):
        h, sv = _layer_fwd(h, p, dims, "l%d" % l)
        saved.append(sv)
    sq, dy, dyb = _loss_and_grad(h, target)
    grads = [None] * len(params)
    for l in reversed(range(len(params))):
        dy, dyb, grads[l] = _layer_bwd(dy, dyb, saved[l], params[l], dims, "l%d" % l)
    return sq, dy, grads


COL_SHARDED = ("w_in", "w_attn_branch", "w_ssm_branch", "w_ffn_in")
ROW_SHARDED = ("ssm_glu_w", "w_out", "w_ffn_out")
BIG_WEIGHTS = COL_SHARDED + ROW_SHARDED
WEIGHT_NAMES = ("norm_mix_g", "w_in", "gate_bias", "q_norm_g", "k_norm_g", "attn_sinks", "ssm_lambda_re",
                "ssm_lambda_im", "ssm_log_dt", "ssm_b_re", "ssm_b_im", "ssm_c_re", "ssm_c_im", "ssm_d", "ssm_glu_w",
                "ssm_glu_b", "w_attn_branch", "w_ssm_branch", "w_out", "norm_ffn_g", "w_ffn_in", "w_ffn_out")
SMALL_WEIGHTS = tuple(n for n in WEIGHT_NAMES if n not in BIG_WEIGHTS)


def _dims(d, big, tc):
    s, _, aw, _ = big["w_attn_branch"].shape
    sw = big["w_ssm_branch"].shape[2]
    in_w = big["w_in"].shape[3] * s
    kvw = (in_w - aw - sw - 2 * d) // 2
    ff = big["w_ffn_out"].shape[2] * s
    return dict(aw=aw, kvw=kvw, sw=sw, ff=ff, n_q=aw // HEAD_DIM, n_kv=kvw // HEAD_DIM, tc=tc)


def _layer_params(l, small, big):
    p = {n: _Weight(big[n], l, "col") for n in COL_SHARDED}
    p.update({n: _Weight(big[n], l, "row") for n in ROW_SHARDED})
    for n in ("norm_mix_g", "gate_bias", "q_norm_g", "k_norm_g", "ssm_d", "ssm_glu_b", "norm_ffn_g"):
        p[n] = small[n][l][None]
    p["attn_sinks"] = small["attn_sinks"][l]
    p["lam_re"] = small["ssm_lambda_re"][l]
    p["lam_im"] = small["ssm_lambda_im"][l]
    p["log_dt"] = small["ssm_log_dt"][l][:, None]
    p["b_re_t"] = jnp.swapaxes(small["ssm_b_re"][l], 1, 2)
    p["b_im_t"] = jnp.swapaxes(small["ssm_b_im"][l], 1, 2)
    p["c_re"] = small["ssm_c_re"][l]
    p["c_im"] = small["ssm_c_im"][l]
    return p


_ANY = pl.BlockSpec(memory_space=pl.ANY)
_MESH_ID = pl.DeviceIdType.MESH


def _coords():
    return lax.axis_index("x"), lax.axis_index("y"), lax.axis_index("c")


def _remote(src, dst, send_sems, recv_sems, k, to):
    return pltpu.make_async_remote_copy(src_ref=src, dst_ref=dst, send_sem=send_sems.at[k], recv_sem=recv_sems.at[k],
                                        device_id=to, device_id_type=_MESH_ID)


def _comm_call(body, ins, out_shapes, n_remote, n_local, name):
    return pl.pallas_call(
        body,
        out_shape=tuple(out_shapes),
        in_specs=[_ANY] * len(ins),
        out_specs=tuple([_ANY] * len(out_shapes)),
        scratch_shapes=[pltpu.SemaphoreType.DMA((n_remote,)), pltpu.SemaphoreType.DMA((n_remote,)),
                        pltpu.SemaphoreType.DMA((n_local,))],
        compiler_params=pltpu.CompilerParams(has_side_effects=True),
        name=name,
    )(*ins)


def _allgather_weights(shards, *, name):
    n = len(shards)

    def body(*refs):
        ins, outs = refs[:n], refs[n:2 * n]
        send_sems, recv_sems, local_sems = refs[2 * n:]
        x, y, c = _coords()
        me = 2 * x + y
        chips = [(1 - x, y), (x, 1 - y), (1 - x, 1 - y)]
        sibling = (x, y, 1 - c)

        def half(ref, i, hc):
            rh = shards[i].shape[1] // 2
            return ref.at[:, pl.ds(hc * rh, rh), :]

        local = [pltpu.make_async_copy(ins[i], outs[i].at[me], local_sems.at[i]) for i in range(n)]
        for cp in local:
            cp.start()
        first = [_remote(half(ins[i], i, c), half(outs[i].at[me], i, c), send_sems, recv_sems, 6 * i + k, (px, py, c))
                 for i in range(n) for k, (px, py) in enumerate(chips)]
        for cp in first:
            cp.start()
        passed = []
        for k, (px, py) in enumerate(chips):
            for i in range(n):
                landed = half(outs[i].at[2 * px + py], i, c)
                _remote(landed, landed, send_sems, recv_sems, 6 * i + k, (px, py, c)).wait_recv()
                fw = _remote(landed, landed, send_sems, recv_sems, 6 * i + 3 + k, sibling)
                fw.start()
                passed.append(fw)
        for k, (px, py) in enumerate(chips):
            for i in range(n):
                other = half(outs[i].at[2 * px + py], i, 1 - c)
                _remote(other, other, send_sems, recv_sems, 6 * i + 3 + k, sibling).wait_recv()
        for cp in first + passed:
            cp.wait_send()
        for cp in local:
            cp.wait()

    outs = [jax.ShapeDtypeStruct((N_CHIPS,) + s.shape, s.dtype) for s in shards]
    return _comm_call(body, shards, outs, 6 * n, n, name)


def _exchange_halves(grads, *, name):
    n = len(grads)

    def body(*refs):
        ins, theirs, mine = refs[:n], refs[n:2 * n], refs[2 * n:3 * n]
        send_sems, recv_sems, local_sems = refs[3 * n:]
        x, y, c = _coords()
        sibling = (x, y, 1 - c)
        local, sends = [], []
        for i in range(n):
            rh = grads[i].shape[1] // 2
            cp = pltpu.make_async_copy(ins[i].at[:, pl.ds(c * rh, rh), :], mine[i], local_sems.at[i])
            cp.start()
            local.append(cp)
            cp = _remote(ins[i].at[:, pl.ds((1 - c) * rh, rh), :], theirs[i], send_sems, recv_sems, i, sibling)
            cp.start()
            sends.append(cp)
        for cp in sends:
            cp.wait_recv()
        for cp in sends:
            cp.wait_send()
        for cp in local:
            cp.wait()

    shapes = [jax.ShapeDtypeStruct((g.shape[0], g.shape[1] // 2, g.shape[2]), g.dtype) for g in grads]
    res = _comm_call(body, grads, shapes + shapes, n, n, name)
    return res[:n], res[n:]


def _exchange_chips(parts, *, name):
    n = len(parts)

    def body(*refs):
        ins, outs = refs[:n], refs[n:2 * n]
        send_sems, recv_sems, local_sems = refs[2 * n:]
        x, y, c = _coords()
        me = 2 * x + y
        chips = [(1 - x, y), (x, 1 - y), (1 - x, 1 - y)]
        local = [pltpu.make_async_copy(ins[i].at[me], outs[i].at[me], local_sems.at[i]) for i in range(n)]
        for cp in local:
            cp.start()
        sends = [_remote(ins[i].at[2 * px + py], outs[i].at[me], send_sems, recv_sems, 3 * i + k, (px, py, c))
                 for i in range(n) for k, (px, py) in enumerate(chips)]
        for cp in sends:
            cp.start()
        for i in range(n):
            for k, (px, py) in enumerate(chips):
                slot = outs[i].at[2 * px + py]
                _remote(slot, slot, send_sems, recv_sems, 3 * i + k, (px, py, c)).wait_recv()
        for cp in sends:
            cp.wait_send()
        for cp in local:
            cp.wait()

    return _comm_call(body, parts, [jax.ShapeDtypeStruct(p.shape, p.dtype) for p in parts], 3 * n, n, name)


def _join_halves(halves, n_layers, *, name):
    n = len(halves)
    nw = n // n_layers

    def body(*refs):
        ins, outs = refs[:n], refs[n:n + nw]
        send_sems, recv_sems, local_sems = refs[n + nw:]
        x, y, c = _coords()
        sibling = (x, y, 1 - c)
        local, sends = [], []
        for i in range(n):
            w, l = divmod(i, n_layers)
            cp = pltpu.make_async_copy(ins[i], outs[w].at[l, c], local_sems.at[i])
            cp.start()
            local.append(cp)
            cp = _remote(ins[i], outs[w].at[l, c], send_sems, recv_sems, i, sibling)
            cp.start()
            sends.append(cp)
        for i in range(n):
            w, l = divmod(i, n_layers)
            slot = outs[w].at[l, 1 - c]
            _remote(slot, slot, send_sems, recv_sems, i, sibling).wait_recv()
        for cp in sends:
            cp.wait_send()
        for cp in local:
            cp.wait()

    shapes = [jax.ShapeDtypeStruct((n_layers, 2) + halves[w * n_layers].shape, halves[w * n_layers].dtype)
              for w in range(nw)]
    return _comm_call(body, halves, shapes, n, n, name)


def _allgather_all(buf, *, name):
    def body(in_ref, out_ref, send_sems, recv_sems, local_sems):
        x, y, c = _coords()
        flip = lambda v, b: (1 - v) if b else v
        peers = [(flip(x, r & 4), flip(y, r & 2), flip(c, r & 1)) for r in range(1, N_DEV)]
        slot_of = lambda p: 4 * p[0] + 2 * p[1] + p[2]
        local = pltpu.make_async_copy(in_ref, out_ref.at[slot_of((x, y, c))], local_sems.at[0])
        local.start()
        sends = [_remote(in_ref, out_ref.at[slot_of((x, y, c))], send_sems, recv_sems, k, p) for k, p in enumerate(peers)]
        for cp in sends:
            cp.start()
        for k, p in enumerate(peers):
            slot = out_ref.at[slot_of(p)]
            _remote(slot, slot, send_sems, recv_sems, k, p).wait_recv()
        for cp in sends:
            cp.wait_send()
        local.wait()

    return _comm_call(body, [buf], [jax.ShapeDtypeStruct((N_DEV,) + buf.shape, buf.dtype)], N_DEV - 1, 1, name)[0]


def _sum_slots(arr, *, name):
    s, r, c = arr.shape
    tm = _row_tile(4 * c * (s + 1), r)

    def body(*refs):
        acc = refs[0][...]
        for ref in refs[1:s]:
            acc = acc + ref[...]
        refs[s][...] = acc

    return pl.pallas_call(
        body,
        out_shape=jax.ShapeDtypeStruct((r, c), arr.dtype),
        grid=(r // tm,),
        in_specs=[pl.BlockSpec((None, tm, c), lambda i, k=k: (k, i, 0)) for k in range(s)],
        out_specs=pl.BlockSpec((tm, c), lambda i: (i, 0)),
        compiler_params=_cparams("parallel"),
        name=name,
    )(*([arr] * s))


def _reduce_scatter(grads, n_layers):
    theirs, mine = _exchange_halves(grads, name="rs_exchange_halves")
    parts = []
    for i, (a, b) in enumerate(zip(mine, theirs)):
        s, rh, c = a.shape
        p, = _rowmap(lambda u, v: u + v, [(a.reshape(s * rh, c), "row", c, 0), (b.reshape(s * rh, c), "row", c, 0)],
                     [(c, F32, "row", c)], rows=s * rh, tm=_row_tile(12 * c, s * rh), name="rs_add_halves_%d" % i)
        parts.append(p.reshape(s, rh, c))
    got = _exchange_chips(parts, name="rs_exchange_chips")
    halves = [_sum_slots(g, name="rs_sum_chips_%d" % i) for i, g in enumerate(got)]
    joined = _join_halves(halves, n_layers, name="rs_join_halves")
    return [j.reshape(j.shape[0], 2 * j.shape[2], j.shape[3]) for j in joined]


def _adamw_fn(w, g, m, v):
    m = ADAM_B1 * m + (1.0 - ADAM_B1) * g
    v = ADAM_B2 * v + (1.0 - ADAM_B2) * jnp.square(g)
    m_hat = m / (1.0 - ADAM_B1 ** ADAM_STEP)
    v_hat = v / (1.0 - ADAM_B2 ** ADAM_STEP)
    delta = -ADAM_LR * (m_hat / (jnp.sqrt(v_hat) + ADAM_EPS) + ADAM_WD * w)
    return delta, m, v


def _adamw(w, g, m, v, *, name):
    rows, cols = w.shape
    ins = [(a, "row", cols, 0) for a in (w, g, m, v)]
    outs = [(cols, F32, "row", cols)] * 3
    return _rowmap(_adamw_fn, ins, outs, rows=rows, tm=_row_tile(56 * cols, rows), name=name)


def _pack(arrays):
    flat = jnp.concatenate([a.reshape(-1) for a in arrays])
    pad = (-flat.shape[0]) % (256 * 128)
    return jnp.pad(flat, (0, pad)).reshape(-1, 128)


def _unpack(buf, shapes):
    flat = buf.reshape(-1)
    out, off = [], 0
    for s in shapes:
        n = math.prod(s)
        out.append(flat[off:off + n].reshape(s))
        off += n
    return out


def kernel(x, norm_mix_g, w_in, gate_bias, q_norm_g, k_norm_g, attn_sinks, ssm_lambda_re, ssm_lambda_im, ssm_log_dt, ssm_b_re, ssm_b_im, ssm_c_re, ssm_c_im, ssm_d, ssm_glu_w, ssm_glu_b, w_attn_branch, w_ssm_branch, w_out, norm_ffn_g, w_ffn_in, w_ffn_out, loss_target, m_norm_mix_g, m_w_in, m_gate_bias, m_q_norm_g, m_k_norm_g, m_attn_sinks, m_ssm_lambda_re, m_ssm_lambda_im, m_ssm_log_dt, m_ssm_b_re, m_ssm_b_im, m_ssm_c_re, m_ssm_c_im, m_ssm_d, m_ssm_glu_w, m_ssm_glu_b, m_w_attn_branch, m_w_ssm_branch, m_w_out, m_norm_ffn_g, m_w_ffn_in, m_w_ffn_out, v_norm_mix_g, v_w_in, v_gate_bias, v_q_norm_g, v_k_norm_g, v_attn_sinks, v_ssm_lambda_re, v_ssm_lambda_im, v_ssm_log_dt, v_ssm_b_re, v_ssm_b_im, v_ssm_c_re, v_ssm_c_im, v_ssm_d, v_ssm_glu_w, v_ssm_glu_b, v_w_attn_branch, v_w_ssm_branch, v_w_out, v_norm_ffn_g, v_w_ffn_in, v_w_ffn_out):
    w = dict(norm_mix_g=norm_mix_g, w_in=w_in, gate_bias=gate_bias, q_norm_g=q_norm_g, k_norm_g=k_norm_g,
             attn_sinks=attn_sinks, ssm_lambda_re=ssm_lambda_re, ssm_lambda_im=ssm_lambda_im, ssm_log_dt=ssm_log_dt,
             ssm_b_re=ssm_b_re, ssm_b_im=ssm_b_im, ssm_c_re=ssm_c_re, ssm_c_im=ssm_c_im, ssm_d=ssm_d,
             ssm_glu_w=ssm_glu_w, ssm_glu_b=ssm_glu_b, w_attn_branch=w_attn_branch, w_ssm_branch=w_ssm_branch,
             w_out=w_out, norm_ffn_g=norm_ffn_g, w_ffn_in=w_ffn_in, w_ffn_out=w_ffn_out)
    m = dict(norm_mix_g=m_norm_mix_g, w_in=m_w_in, gate_bias=m_gate_bias, q_norm_g=m_q_norm_g, k_norm_g=m_k_norm_g,
             attn_sinks=m_attn_sinks, ssm_lambda_re=m_ssm_lambda_re, ssm_lambda_im=m_ssm_lambda_im,
             ssm_log_dt=m_ssm_log_dt, ssm_b_re=m_ssm_b_re, ssm_b_im=m_ssm_b_im, ssm_c_re=m_ssm_c_re,
             ssm_c_im=m_ssm_c_im, ssm_d=m_ssm_d, ssm_glu_w=m_ssm_glu_w, ssm_glu_b=m_ssm_glu_b,
             w_attn_branch=m_w_attn_branch, w_ssm_branch=m_w_ssm_branch, w_out=m_w_out, norm_ffn_g=m_norm_ffn_g,
             w_ffn_in=m_w_ffn_in, w_ffn_out=m_w_ffn_out)
    v = dict(norm_mix_g=v_norm_mix_g, w_in=v_w_in, gate_bias=v_gate_bias, q_norm_g=v_q_norm_g, k_norm_g=v_k_norm_g,
             attn_sinks=v_attn_sinks, ssm_lambda_re=v_ssm_lambda_re, ssm_lambda_im=v_ssm_lambda_im,
             ssm_log_dt=v_ssm_log_dt, ssm_b_re=v_ssm_b_re, ssm_b_im=v_ssm_b_im, ssm_c_re=v_ssm_c_re,
             ssm_c_im=v_ssm_c_im, ssm_d=v_ssm_d, ssm_glu_w=v_ssm_glu_w, ssm_glu_b=v_ssm_glu_b,
             w_attn_branch=v_w_attn_branch, w_ssm_branch=v_w_ssm_branch, w_out=v_w_out, norm_ffn_g=v_norm_ffn_g,
             w_ffn_in=v_w_ffn_in, w_ffn_out=v_w_ffn_out)
    n_layers = norm_mix_g.shape[0]
    d_model = x.shape[-1]
    seq = x.shape[1]

    gathered = _allgather_weights([w[n].astype(BF16) for n in BIG_WEIGHTS], name="allgather_weights")
    big = dict(zip(BIG_WEIGHTS, gathered))
    dims = _dims(d_model, big, min(512, seq))
    params = [_layer_params(l, w, big) for l in range(n_layers)]
    sq, dx, grads = _local_step(x[0], loss_target[0], params, dims)
    loss = lax.psum(sq[0, 0], MESH_AXES) * (0.5 / d_model)

    reduced = _reduce_scatter([grads[l][n] for n in BIG_WEIGHTS for l in range(n_layers)], n_layers)
    grad = dict(zip(BIG_WEIGHTS, reduced))
    small_local = [jnp.stack([grads[l][n].reshape(w[n].shape[1:]) for l in range(n_layers)]) for n in SMALL_WEIGHTS]
    small_sum = _sum_slots(_allgather_all(_pack(small_local), name="allgather_small_grads"), name="sum_small_grads")
    grad.update(zip(SMALL_WEIGHTS, _unpack(small_sum, [w[n].shape for n in SMALL_WEIGHTS])))

    delta, new_m, new_v = {}, {}, {}
    for n in BIG_WEIGHTS:
        shape = w[n].shape
        flat = lambda a: a.reshape(shape[0] * shape[1], shape[2])
        res = _adamw(flat(w[n]), flat(grad[n]), flat(m[n]), flat(v[n]), name="adamw_" + n)
        delta[n], new_m[n], new_v[n] = [r.reshape(shape) for r in res]
    small_shapes = [w[n].shape for n in SMALL_WEIGHTS]
    res = _adamw(_pack([w[n] for n in SMALL_WEIGHTS]), small_sum, _pack([m[n] for n in SMALL_WEIGHTS]),
                 _pack([v[n] for n in SMALL_WEIGHTS]), name="adamw_small")
    for out, packed in zip((delta, new_m, new_v), res):
        out.update(zip(SMALL_WEIGHTS, _unpack(packed, small_shapes)))

    return (loss, dx[None], *[grad[n] for n in WEIGHT_NAMES], *[delta[n] for n in WEIGHT_NAMES],
            *[new_m[n] for n in WEIGHT_NAMES], *[new_v[n] for n in WEIGHT_NAMES])
```

```python
import functools
import math

import jax
import jax.numpy as jnp
from jax import lax
from jax.experimental import pallas as pl
from jax.experimental.pallas import tpu as pltpu

HEAD_DIM = 64
WINDOW = 128
SSM_GROUP_CH = 16
SSM_LANE_GROUPS = 8
RMS_EPS = 1e-6
ADAM_LR = 0.001
ADAM_B1 = 0.9
ADAM_B2 = 0.999
ADAM_EPS = 1e-08
ADAM_WD = 0.01
ADAM_STEP = 10
NEG_BIG = -1e30
MESH_AXES = ("x", "y", "c")
N_CHIPS = 4
N_DEV = 8
VMEM_LIMIT_BYTES = 56 * 1024 * 1024
BF16 = jnp.bfloat16
F32 = jnp.float32


def _cparams(*semantics):
    return pltpu.CompilerParams(dimension_semantics=semantics, vmem_limit_bytes=VMEM_LIMIT_BYTES)


def _pick(n, target, mult):
    if n <= target:
        return n
    best = None
    for d in range(mult, target + 1, mult):
        if n % d == 0:
            best = d
    assert best is not None, (n, target, mult)
    return best


def _rowmap(fn, ins, outs, *, rows, tm, ncol=1, name):
    n_in = len(ins)
    nrow = rows // tm
    assert nrow * tm == rows

    in_specs = []
    for arr, kind, width, coloff in ins:
        if kind == "row":
            in_specs.append(pl.BlockSpec((tm, width), lambda j, i, o=coloff: (i, o + j)))
        elif kind == "vec":
            in_specs.append(pl.BlockSpec((1, width), lambda j, i, o=coloff: (0, o + j)))
        else:
            nd = arr.ndim
            in_specs.append(pl.BlockSpec(arr.shape, lambda j, i, nd=nd: (0,) * nd))
    out_specs, out_shapes = [], []
    for cols, dtype, kind, width in outs:
        if kind == "row":
            out_specs.append(pl.BlockSpec((tm, width), lambda j, i: (i, j)))
            out_shapes.append(jax.ShapeDtypeStruct((rows, cols), dtype))
        else:
            out_specs.append(pl.BlockSpec((1, width), lambda j, i: (0, j)))
            out_shapes.append(jax.ShapeDtypeStruct((1, cols), dtype))

    def body(*refs):
        i = pl.program_id(1)
        res = fn(*[r[...] for r in refs[:n_in]])
        if not isinstance(res, (tuple, list)):
            res = (res,)
        for (cols, dtype, kind, width), ref, val in zip(outs, refs[n_in:], res):
            if kind == "row":
                ref[...] = val.astype(ref.dtype)
            else:
                @pl.when(i == 0)
                def _():
                    ref[...] = jnp.zeros_like(ref)
                ref[...] += val.astype(ref.dtype)

    res = pl.pallas_call(
        body,
        out_shape=tuple(out_shapes),
        grid=(ncol, nrow),
        in_specs=in_specs,
        out_specs=tuple(out_specs),
        compiler_params=_cparams("parallel", "arbitrary"),
        name=name,
    )(*[a[0] for a in ins])
    return res


def _mm_body(dims, nk, has_add, unused_in=0):
    def body(*refs):
        if has_add:
            a_ref, b_ref, add_ref, o_ref = refs[:4]
            rest = refs[4:]
        else:
            a_ref, b_ref = refs[:2]
            o_ref = refs[2 + unused_in]
            add_ref = None
            rest = refs[3 + unused_in:]
        part = lax.dot_general(a_ref[...], b_ref[...], (dims, ((), ())), preferred_element_type=F32)
        if nk == 1:
            if add_ref is not None:
                part = part + add_ref[...]
            o_ref[...] = part.astype(o_ref.dtype)
        else:
            acc_ref = rest[0]
            k = pl.program_id(2)

            @pl.when(k == 0)
            def _():
                acc_ref[...] = part

            @pl.when(k > 0)
            def _():
                acc_ref[...] += part

            @pl.when(k == nk - 1)
            def _():
                r = acc_ref[...]
                if add_ref is not None:
                    r = r + add_ref[...]
                o_ref[...] = r.astype(o_ref.dtype)
    return body


class _Weight:
    def __init__(self, arr, layer, kind):
        self.arr, self.layer, self.kind = arr, layer, kind
        self.s, _, self.r, self.c = arr.shape
        self.rows = self.r * (self.s if kind == "row" else 1)
        self.cols = self.c * (self.s if kind == "col" else 1)

    def tiles(self, tr, tc):
        return _pick(self.r, tr, 128), _pick(self.c, tc, 128)

    def index(self, tr, tc):
        layer = self.layer
        if self.kind == "col":
            per = self.c // tc
            return lambda rb, cb: (cb // per, layer, rb, cb % per)
        per = self.r // tr
        return lambda rb, cb: (rb // per, layer, rb % per, cb)


def _shard_index(kind, r, c, tr, tc):
    if kind == "col":
        per = c // tc
        return lambda rb, cb: (cb // per, rb, cb % per)
    per = r // tr
    return lambda rb, cb: (rb // per, rb % per, cb)


def _mm_nn(a, w, *, out_dtype, tm, tn, tk, name, add=None):
    m, k = a.shape
    assert k == w.rows
    tm = _pick(m, tm, 16)
    tk, tn = w.tiles(tk, tn)
    nk = k // tk
    widx = w.index(tk, tn)
    in_specs = [pl.BlockSpec((tm, tk), lambda n, i, kk: (i, kk)),
                pl.BlockSpec((None, None, tk, tn), lambda n, i, kk: widx(kk, n))]
    args = [a, w.arr]
    if add is not None:
        in_specs.append(pl.BlockSpec((tm, tn), lambda n, i, kk: (i, n)))
        args.append(add)
    return pl.pallas_call(
        _mm_body(((1,), (0,)), nk, add is not None),
        out_shape=jax.ShapeDtypeStruct((m, w.cols), out_dtype),
        grid=(w.cols // tn, m // tm, nk),
        in_specs=in_specs,
        out_specs=pl.BlockSpec((tm, tn), lambda n, i, kk: (i, n)),
        scratch_shapes=[pltpu.VMEM((tm, tn), F32)] if nk > 1 else [],
        compiler_params=_cparams("parallel", "parallel", "arbitrary"),
        name=name,
    )(*args)


def _mm_nt(a, w, *, out_dtype, tm, tn, tko, name):
    m, n = a.shape
    assert n == w.cols
    tm = _pick(m, tm, 16)
    tko, tn = w.tiles(tko, tn)
    nk = n // tn
    widx = w.index(tko, tn)
    return pl.pallas_call(
        _mm_body(((1,), (1,)), nk, False),
        out_shape=jax.ShapeDtypeStruct((m, w.rows), out_dtype),
        grid=(w.rows // tko, m // tm, nk),
        in_specs=[pl.BlockSpec((tm, tn), lambda ko, i, nn: (i, nn)),
                  pl.BlockSpec((None, None, tko, tn), lambda ko, i, nn: widx(ko, nn))],
        out_specs=pl.BlockSpec((tm, tko), lambda ko, i, nn: (i, ko)),
        scratch_shapes=[pltpu.VMEM((tm, tko), F32)] if nk > 1 else [],
        compiler_params=_cparams("parallel", "parallel", "arbitrary"),
        name=name,
    )(a, w.arr)


def _mm_tn(a, c, w, *, into, tm, tn, tko, name):
    m, k = a.shape
    mc, n = c.shape
    assert mc == m and k == w.rows and n == w.cols
    tm = _pick(m, tm, 16)
    tko, tn = w.tiles(tko, tn)
    nk = m // tm
    oidx = _shard_index(w.kind, w.r, w.c, tko, tn)
    layer = w.layer
    in_specs = [pl.BlockSpec((tm, tko), lambda ko, nn, mm: (mm, ko)),
                pl.BlockSpec((tm, tn), lambda ko, nn, mm: (mm, nn))]
    args = [a, c]
    if into is not None:
        in_specs.append(pl.BlockSpec(memory_space=pl.ANY))
        args.append(into)
    return pl.pallas_call(
        _mm_body(((0,), (0,)), nk, False, unused_in=len(args) - 2),
        out_shape=jax.ShapeDtypeStruct((w.arr.shape[1], w.s, w.r, w.c), F32),
        grid=(k // tko, n // tn, nk),
        in_specs=in_specs,
        out_specs=pl.BlockSpec((None, None, tko, tn), lambda ko, nn, mm: (layer,) + oidx(ko, nn)),
        scratch_shapes=[pltpu.VMEM((tko, tn), F32)] if nk > 1 else [],
        input_output_aliases={2: 0} if into is not None else {},
        compiler_params=_cparams("parallel", "parallel", "arbitrary"),
        name=name,
    )(*args)


def _rms(x, g):
    r = lax.rsqrt(jnp.mean(x * x, axis=-1, keepdims=True) + RMS_EPS)
    return x * r * g, r


def _rms_bwd(x, r, g, dy):
    dg = jnp.sum(dy * x * r, axis=0, keepdims=True)
    t = dy * g
    dx = r * t - x * (r * r * r) * jnp.mean(t * x, axis=-1, keepdims=True)
    return dx, dg


def _attn_consts(n_q, n_kv, sinks):
    group = n_q // n_kv
    t = jnp.arange(WINDOW, dtype=jnp.int32)[:, None]
    s = jnp.arange(2 * WINDOW, dtype=jnp.int32)[None, :] - WINDOW
    dist = (t - s).astype(F32)
    valid = (dist >= 0) & (dist < WINDOW)
    slopes = jnp.exp2(-8.0 * jnp.arange(1, n_q + 1, dtype=F32) / n_q)
    bias = jnp.where(valid[None], -slopes[:, None, None] * dist[None], NEG_BIG)
    bias = bias.reshape(n_kv, group * WINDOW, 2 * WINDOW)
    sink = jnp.broadcast_to(sinks.astype(F32).reshape(n_kv, group, 1, 1), (n_kv, group, WINDOW, 1))
    return bias, sink.reshape(n_kv, group * WINDOW, 1)


def _attn_probs(q_ref, kc_ref, kp_ref, vc_ref, vp_ref, qg, kg, sink, bias, first_mask, kv, group):
    sl = slice(kv * HEAD_DIM, (kv + 1) * HEAD_DIM)
    k2 = jnp.concatenate([kp_ref[:, sl], kc_ref[:, sl]], axis=0)
    v2 = jnp.concatenate([vp_ref[:, sl], vc_ref[:, sl]], axis=0)
    k2n, rk = _rms(k2, kg)
    qx, qn, rq = [], [], []
    for g in range(group):
        h = kv * group + g
        x = q_ref[:, h * HEAD_DIM:(h + 1) * HEAD_DIM]
        y, r = _rms(x, qg)
        qx.append(x); qn.append(y); rq.append(r)
    qs = jnp.concatenate(qn, axis=0).astype(BF16)
    k2b = k2n.astype(BF16)
    s = lax.dot_general(qs, k2b, (((1,), (1,)), ((), ())), preferred_element_type=F32) * (HEAD_DIM ** -0.5)
    s = jnp.where(first_mask, NEG_BIG, s + bias)
    m = jnp.maximum(jnp.max(s, axis=-1, keepdims=True), sink)
    p = jnp.exp(s - m)
    esink = jnp.exp(sink - m)
    denom = jnp.sum(p, axis=-1, keepdims=True) + esink
    pn = p / denom
    return dict(k2=k2, rk=rk, k2b=k2b, v2b=v2.astype(BF16), qx=qx, rq=rq, qs=qs, pn=pn, psink=esink / denom)


def _attn_specs(n_q, n_kv):
    aw, kvw = n_q * HEAD_DIM, n_kv * HEAD_DIM
    group = n_q // n_kv
    kblk, vblk = aw // kvw, aw // kvw + 1

    def specs(nb):
        cur = lambda n: jnp.minimum(n, nb - 1)
        prev = lambda n: jnp.maximum(jnp.minimum(n, nb - 1) - 1, 0)
        return [
            pl.BlockSpec((WINDOW, aw), lambda n: (cur(n), 0)),
            pl.BlockSpec((WINDOW, kvw), lambda n: (cur(n), kblk)),
            pl.BlockSpec((WINDOW, kvw), lambda n: (prev(n), kblk)),
            pl.BlockSpec((WINDOW, kvw), lambda n: (cur(n), vblk)),
            pl.BlockSpec((WINDOW, kvw), lambda n: (prev(n), vblk)),
        ]
    const_specs = [
        pl.BlockSpec((1, HEAD_DIM), lambda n: (0, 0)),
        pl.BlockSpec((1, HEAD_DIM), lambda n: (0, 0)),
        pl.BlockSpec((n_kv, group * WINDOW, 1), lambda n: (0, 0, 0)),
        pl.BlockSpec((n_kv, group * WINDOW, 2 * WINDOW), lambda n: (0, 0, 0)),
    ]
    return specs, const_specs


def _attn_fwd(z, qg, kg, sinks, *, n_q, n_kv, name):
    L = z.shape[0]
    nb = L // WINDOW
    aw = n_q * HEAD_DIM
    group = n_q // n_kv
    bias, sink = _attn_consts(n_q, n_kv, sinks)
    specs, const_specs = _attn_specs(n_q, n_kv)

    def body(q_ref, kc_ref, kp_ref, vc_ref, vp_ref, qg_ref, kg_ref, sink_ref, bias_ref, o_ref):
        n = pl.program_id(0)
        col = lax.broadcasted_iota(jnp.int32, (group * WINDOW, 2 * WINDOW), 1)
        first_mask = jnp.logical_and(n == 0, col < WINDOW)
        for kv in range(n_kv):
            a = _attn_probs(q_ref, kc_ref, kp_ref, vc_ref, vp_ref, qg_ref[...], kg_ref[...],
                            sink_ref[kv], bias_ref[kv], first_mask, kv, group)
            o = jnp.dot(a["pn"].astype(BF16), a["v2b"], preferred_element_type=F32)
            for g in range(group):
                h = kv * group + g
                o_ref[:, h * HEAD_DIM:(h + 1) * HEAD_DIM] = o[g * WINDOW:(g + 1) * WINDOW].astype(o_ref.dtype)

    return pl.pallas_call(
        body,
        out_shape=jax.ShapeDtypeStruct((L, aw), BF16),
        grid=(nb,),
        in_specs=specs(nb) + const_specs,
        out_specs=pl.BlockSpec((WINDOW, aw), lambda n: (n, 0)),
        compiler_params=_cparams("parallel"),
        name=name,
    )(z, z, z, z, z, qg, kg, sink, bias)


def _attn_bwd(z, do, qg, kg, sinks, *, n_q, n_kv, name):
    L = z.shape[0]
    nb = L // WINDOW
    aw, kvw = n_q * HEAD_DIM, n_kv * HEAD_DIM
    group = n_q // n_kv
    bias, sink = _attn_consts(n_q, n_kv, sinks)
    specs, const_specs = _attn_specs(n_q, n_kv)
    scale = HEAD_DIM ** -0.5

    def body(q_ref, kc_ref, kp_ref, vc_ref, vp_ref, do_ref, qg_ref, kg_ref, sink_ref, bias_ref,
             dq_ref, dkv_ref, dqg_ref, dkg_ref, dsink_ref, carry_ref):
        n = pl.program_id(0)

        @pl.when(n == 0)
        def _():
            dqg_ref[...] = jnp.zeros_like(dqg_ref)
            dkg_ref[...] = jnp.zeros_like(dkg_ref)
            dsink_ref[...] = jnp.zeros_like(dsink_ref)
            carry_ref[...] = jnp.zeros_like(carry_ref)

        @pl.when(n < nb)
        def _():
            col = lax.broadcasted_iota(jnp.int32, (group * WINDOW, 2 * WINDOW), 1)
            first_mask = jnp.logical_and(n == 0, col < WINDOW)
            head_lane = lax.broadcasted_iota(jnp.int32, (1, n_q), 1)
            qg, kg = qg_ref[...], kg_ref[...]
            dqg = jnp.zeros((1, HEAD_DIM), F32)
            dkg = jnp.zeros((1, HEAD_DIM), F32)
            dsink = jnp.zeros((1, n_q), F32)
            for kv in range(n_kv):
                a = _attn_probs(q_ref, kc_ref, kp_ref, vc_ref, vp_ref, qg, kg,
                                sink_ref[kv], bias_ref[kv], first_mask, kv, group)
                pn = a["pn"]
                dos = jnp.concatenate(
                    [do_ref[:, (kv * group + g) * HEAD_DIM:(kv * group + g + 1) * HEAD_DIM] for g in range(group)],
                    axis=0).astype(BF16)
                dpn = lax.dot_general(dos, a["v2b"], (((1,), (1,)), ((), ())), preferred_element_type=F32)
                dv2 = lax.dot_general(pn.astype(BF16), dos, (((0,), (0,)), ((), ())), preferred_element_type=F32)
                delta = jnp.sum(pn * dpn, axis=-1, keepdims=True)
                ds = (pn * (dpn - delta)).astype(BF16)
                dsk = -a["psink"] * delta
                dqn = lax.dot_general(ds, a["k2b"], (((1,), (0,)), ((), ())), preferred_element_type=F32) * scale
                dk2n = lax.dot_general(ds, a["qs"], (((0,), (0,)), ((), ())), preferred_element_type=F32) * scale
                for g in range(group):
                    h = kv * group + g
                    rows = slice(g * WINDOW, (g + 1) * WINDOW)
                    dx, dgq = _rms_bwd(a["qx"][g], a["rq"][g], qg, dqn[rows])
                    dq_ref[:, h * HEAD_DIM:(h + 1) * HEAD_DIM] = dx.astype(dq_ref.dtype)
                    dqg = dqg + dgq
                    dsink = dsink + jnp.where(head_lane == h, jnp.sum(dsk[rows], axis=0, keepdims=True), 0.0)
                dk2, dgk = _rms_bwd(a["k2"], a["rk"], kg, dk2n)
                dkg = dkg + dgk
                ksl = slice(kv * HEAD_DIM, (kv + 1) * HEAD_DIM)
                vsl = slice(kvw + kv * HEAD_DIM, kvw + (kv + 1) * HEAD_DIM)
                dkv_ref[:, ksl] = (carry_ref[:, ksl] + dk2[:WINDOW]).astype(dkv_ref.dtype)
                dkv_ref[:, vsl] = (carry_ref[:, vsl] + dv2[:WINDOW]).astype(dkv_ref.dtype)
                carry_ref[:, ksl] = dk2[WINDOW:]
                carry_ref[:, vsl] = dv2[WINDOW:]
            dqg_ref[...] += dqg
            dkg_ref[...] += dkg
            dsink_ref[...] += dsink

        @pl.when(n == nb)
        def _():
            dkv_ref[...] = carry_ref[...].astype(dkv_ref.dtype)

    in_specs = specs(nb) + [pl.BlockSpec((WINDOW, aw), lambda n: (jnp.minimum(n, nb - 1), 0))] + const_specs
    return pl.pallas_call(
        body,
        out_shape=(jax.ShapeDtypeStruct((L, aw), BF16), jax.ShapeDtypeStruct((L, 2 * kvw), BF16),
                   jax.ShapeDtypeStruct((1, HEAD_DIM), F32), jax.ShapeDtypeStruct((1, HEAD_DIM), F32),
                   jax.ShapeDtypeStruct((1, n_q), F32)),
        grid=(nb + 1,),
        in_specs=in_specs,
        out_specs=(pl.BlockSpec((WINDOW, aw), lambda n: (jnp.minimum(n, nb - 1), 0)),
                   pl.BlockSpec((WINDOW, 2 * kvw), lambda n: (jnp.maximum(n - 1, 0), 0)),
                   pl.BlockSpec((1, HEAD_DIM), lambda n: (0, 0)),
                   pl.BlockSpec((1, HEAD_DIM), lambda n: (0, 0)),
                   pl.BlockSpec((1, n_q), lambda n: (0, 0))),
        scratch_shapes=[pltpu.VMEM((WINDOW, 2 * kvw), F32)],
        compiler_params=_cparams("arbitrary"),
        name=name,
    )(z, z, z, z, z, do, qg, kg, sink, bias)


def _cmul(ar, ai, br, bi):
    return ar * br - ai * bi, ar * bi + ai * br


def _scan_consts(ar, ai, n, reverse):
    row = lax.broadcasted_iota(jnp.int32, (8, n), 0)
    mults = []
    pr, pi = ar, ai
    for k in (1, 2, 4):
        keep = (row < 8 - k) if reverse else (row >= k)
        mults.append(((8 - k) if reverse else k, jnp.where(keep, pr, 0.0), jnp.where(keep, pi, 0.0)))
        pr, pi = _cmul(pr, pi, pr, pi)
    pwr = jnp.zeros((8, n), F32)
    pwi = jnp.zeros((8, n), F32)
    pr, pi = ar, ai
    for e in range(1, 9):
        sel = (row == 8 - e) if reverse else (row == e - 1)
        pwr = jnp.where(sel, pr, pwr)
        pwi = jnp.where(sel, pi, pwi)
        pr, pi = _cmul(pr, pi, ar, ai)
    return mults, pwr, pwi


def _scan8(xr, xi, mults, pwr, pwi, cr, ci):
    for shift, mr, mi in mults:
        sr = pltpu.roll(xr, shift, 0)
        si = pltpu.roll(xi, shift, 0)
        xr, xi = xr + mr * sr - mi * si, xi + mr * si + mi * sr
    return xr + pwr * cr - pwi * ci, xi + pwr * ci + pwi * cr


def _blockdiag(x):
    g, a, b = x.shape
    j = g // SSM_LANE_GROUPS
    eye = jnp.eye(SSM_LANE_GROUPS, dtype=x.dtype)
    y = x.reshape(j, SSM_LANE_GROUPS, a, 1, b) * eye[None, :, None, :, None]
    return y.reshape(j, SSM_LANE_GROUPS * a, SSM_LANE_GROUPS * b)


def _blockdiag_extract(y, a, b):
    j = y.shape[0]
    y = y.reshape(j, SSM_LANE_GROUPS, a, SSM_LANE_GROUPS, b)
    return jnp.einsum("jgahb,gh->jgab", y, jnp.eye(SSM_LANE_GROUPS, dtype=y.dtype)).reshape(j * SSM_LANE_GROUPS, a, b)


def _ssm_disc(lr, li, ldt, brt, bit):
    dt = jnp.exp(ldt)
    mag = jnp.exp(lr * dt)
    ar = mag * jnp.cos(li * dt)
    ai = mag * jnp.sin(li * dt)
    den = lr * lr + li * li
    fr = ((ar - 1.0) * lr + ai * li) / den
    fi = (ai * lr - (ar - 1.0) * li) / den
    bbr = fr[:, None, :] * brt - fi[:, None, :] * bit
    bbi = fr[:, None, :] * bit + fi[:, None, :] * brt
    return ar, ai, bbr, bbi


def _ssm_prep(lr, li, ldt, brt, bit, *, name):
    g, h, p = brt.shape

    def body(lr_ref, li_ref, ldt_ref, brt_ref, bit_ref, ar_ref, ai_ref, bbr_ref, bbi_ref):
        ar, ai, bbr, bbi = _ssm_disc(lr_ref[...], li_ref[...], ldt_ref[...], brt_ref[...], bit_ref[...])
        ar_ref[...] = ar
        ai_ref[...] = ai
        bbr_ref[...] = bbr
        bbi_ref[...] = bbi

    gp = jax.ShapeDtypeStruct((g, p), F32)
    ghp = jax.ShapeDtypeStruct((g, h, p), F32)
    return pl.pallas_call(body, out_shape=(gp, gp, ghp, ghp), name=name)(lr, li, ldt, brt, bit)


def _ssm_prep_bwd(lr, li, ldt, brt, bit, dar, dai, dbbr, dbbi, *, name):
    g, h, p = brt.shape

    def body(lr_ref, li_ref, ldt_ref, brt_ref, bit_ref, dar_ref, dai_ref, dbbr_ref, dbbi_ref,
             dlr_ref, dli_ref, dldt_ref, dbrt_ref, dbit_ref):
        _, vjp = jax.vjp(_ssm_disc, lr_ref[...], li_ref[...], ldt_ref[...], brt_ref[...], bit_ref[...])
        dlr, dli, dldt, dbrt, dbit = vjp((dar_ref[...], dai_ref[...], dbbr_ref[...], dbbi_ref[...]))
        dlr_ref[...] = dlr
        dli_ref[...] = dli
        dldt_ref[...] = dldt
        dbrt_ref[...] = dbrt
        dbit_ref[...] = dbit

    gp = jax.ShapeDtypeStruct((g, p), F32)
    ghp = jax.ShapeDtypeStruct((g, h, p), F32)
    return pl.pallas_call(body, out_shape=(gp, gp, jax.ShapeDtypeStruct((g, 1), F32), ghp, ghp), name=name)(
        lr, li, ldt, brt, bit, dar, dai, dbbr, dbbi)


def _ssm_specs(tc, nlanes, nch, u_colblk, chunk_of):
    return [
        pl.BlockSpec((tc, nch), lambda j, c: (chunk_of(c), u_colblk + j)),
        pl.BlockSpec((1, nlanes), lambda j, c: (0, j)),
        pl.BlockSpec((1, nlanes), lambda j, c: (0, j)),
        pl.BlockSpec((None, nch, nlanes), lambda j, c: (j, 0, 0)),
        pl.BlockSpec((None, nch, nlanes), lambda j, c: (j, 0, 0)),
        pl.BlockSpec((None, nlanes, nch), lambda j, c: (j, 0, 0)),
        pl.BlockSpec((None, nlanes, nch), lambda j, c: (j, 0, 0)),
        pl.BlockSpec((1, nch), lambda j, c: (0, j)),
    ]


def _ssm_fwd(z, ar, ai, bblk_r, bblk_i, cblk_r, cblk_i, d, *, u_col, tc, name):
    L = z.shape[0]
    nj, nch, nlanes = bblk_r.shape
    w = nj * nch
    nc = L // tc
    ng = tc // 8

    def body(u_ref, ar_ref, ai_ref, br_ref, bi_ref, cr_ref, ci_ref, d_ref, y_ref, s0r_ref, s0i_ref,
             xr_ref, xi_ref, carr_ref, cari_ref):
        c = pl.program_id(1)

        @pl.when(c == 0)
        def _():
            carr_ref[...] = jnp.zeros_like(carr_ref)
            cari_ref[...] = jnp.zeros_like(cari_ref)

        s0r_ref[...] = carr_ref[...]
        s0i_ref[...] = cari_ref[...]
        u = u_ref[...]
        ub = u.astype(BF16)
        xr_ref[...] = jnp.dot(ub, br_ref[...].astype(BF16), preferred_element_type=F32)
        xi_ref[...] = jnp.dot(ub, bi_ref[...].astype(BF16), preferred_element_type=F32)
        mults, pwr, pwi = _scan_consts(ar_ref[...], ai_ref[...], nlanes, False)

        def step(g, carry):
            rows = pl.ds(pl.multiple_of(g * 8, 8), 8)
            sr, si = _scan8(xr_ref[rows, :], xi_ref[rows, :], mults, pwr, pwi, carry[0], carry[1])
            xr_ref[rows, :] = sr
            xi_ref[rows, :] = si
            return sr[7:8], si[7:8]

        cr, ci = lax.fori_loop(0, ng, step, (carr_ref[...], cari_ref[...]))
        carr_ref[...] = cr
        cari_ref[...] = ci
        y = (jnp.dot(xr_ref[...].astype(BF16), cr_ref[...].astype(BF16), preferred_element_type=F32)
             - jnp.dot(xi_ref[...].astype(BF16), ci_ref[...].astype(BF16), preferred_element_type=F32)
             + d_ref[...] * u)
        y_ref[...] = y

    state = jax.ShapeDtypeStruct((nc, 1, nj * nlanes), F32)
    state_spec = pl.BlockSpec((None, 1, nlanes), lambda j, c: (c, 0, j))
    return pl.pallas_call(
        body,
        out_shape=(jax.ShapeDtypeStruct((L, w), F32), state, state),
        grid=(nj, nc),
        in_specs=_ssm_specs(tc, nlanes, nch, u_col // nch, lambda c: c),
        out_specs=(pl.BlockSpec((tc, nch), lambda j, c: (c, j)), state_spec, state_spec),
        scratch_shapes=[pltpu.VMEM((tc, nlanes), F32), pltpu.VMEM((tc, nlanes), F32),
                        pltpu.VMEM((1, nlanes), F32), pltpu.VMEM((1, nlanes), F32)],
        compiler_params=_cparams("parallel", "arbitrary"),
        name=name,
    )(z, ar, ai, bblk_r, bblk_i, cblk_r, cblk_i, d)


def _ssm_bwd(z, dy, s0r, s0i, ar, ai, bblk_r, bblk_i, cblk_r, cblk_i, d, *, u_col, tc, name):
    L = z.shape[0]
    nj, nch, nlanes = bblk_r.shape
    w = nj * nch
    nc = L // tc
    ng = tc // 8
    chunk_of = lambda c: nc - 1 - c

    def body(u_ref, ar_ref, ai_ref, br_ref, bi_ref, cr_ref, ci_ref, d_ref, dy_ref, s0r_ref, s0i_ref,
             du_ref, dbr_ref, dbi_ref, dcr_ref, dci_ref, dar_ref, dai_ref, dd_ref,
             sr_ref, si_ref, pr_ref, pi_ref, lr_ref, li_ref, carr_ref, cari_ref):
        c = pl.program_id(1)

        @pl.when(c == 0)
        def _():
            for ref in (dbr_ref, dbi_ref, dcr_ref, dci_ref, dar_ref, dai_ref, dd_ref, carr_ref, cari_ref):
                ref[...] = jnp.zeros_like(ref)

        u = u_ref[...]
        dyv = dy_ref[...]
        ub = u.astype(BF16)
        dyb = dyv.astype(BF16)
        brb = br_ref[...].astype(BF16)
        bib = bi_ref[...].astype(BF16)
        crb = cr_ref[...].astype(BF16)
        cib = ci_ref[...].astype(BF16)
        a_r, a_i = ar_ref[...], ai_ref[...]

        sr_ref[...] = jnp.dot(ub, brb, preferred_element_type=F32)
        si_ref[...] = jnp.dot(ub, bib, preferred_element_type=F32)
        mults, pwr, pwi = _scan_consts(a_r, a_i, nlanes, False)
        row = lax.broadcasted_iota(jnp.int32, (8, nlanes), 0)

        def fstep(g, carry):
            rows = pl.ds(pl.multiple_of(g * 8, 8), 8)
            sr, si = _scan8(sr_ref[rows, :], si_ref[rows, :], mults, pwr, pwi, carry[0], carry[1])
            sr_ref[rows, :] = sr
            si_ref[rows, :] = si
            pr_ref[rows, :] = jnp.where(row == 0, carry[0], pltpu.roll(sr, 1, 0))
            pi_ref[rows, :] = jnp.where(row == 0, carry[1], pltpu.roll(si, 1, 0))
            return sr[7:8], si[7:8]

        lax.fori_loop(0, ng, fstep, (s0r_ref[...], s0i_ref[...]))

        nt = (((1,), (1,)), ((), ()))
        lr_ref[...] = lax.dot_general(dyb, crb, nt, preferred_element_type=F32)
        li_ref[...] = -lax.dot_general(dyb, cib, nt, preferred_element_type=F32)
        rmults, rpwr, rpwi = _scan_consts(a_r, -a_i, nlanes, True)

        def rstep(gg, carry):
            cr, ci, acc_r, acc_i = carry
            rows = pl.ds(pl.multiple_of((ng - 1 - gg) * 8, 8), 8)
            lr, li = _scan8(lr_ref[rows, :], li_ref[rows, :], rmults, rpwr, rpwi, cr, ci)
            lr_ref[rows, :] = lr
            li_ref[rows, :] = li
            pr, pi = pr_ref[rows, :], pi_ref[rows, :]
            return lr[0:1], li[0:1], acc_r + lr * pr + li * pi, acc_i + li * pr - lr * pi

        zero8 = jnp.zeros((8, nlanes), F32)
        cr, ci, acc_r, acc_i = lax.fori_loop(0, ng, rstep, (carr_ref[...], cari_ref[...], zero8, zero8))
        carr_ref[...] = cr
        cari_ref[...] = ci
        dar_ref[...] += jnp.sum(acc_r, axis=0, keepdims=True)
        dai_ref[...] += jnp.sum(acc_i, axis=0, keepdims=True)

        tn = (((0,), (0,)), ((), ()))
        lrb = lr_ref[...].astype(BF16)
        lib = li_ref[...].astype(BF16)
        dcr_ref[...] += lax.dot_general(sr_ref[...].astype(BF16), dyb, tn, preferred_element_type=F32)
        dci_ref[...] -= lax.dot_general(si_ref[...].astype(BF16), dyb, tn, preferred_element_type=F32)
        dbr_ref[...] += lax.dot_general(ub, lrb, tn, preferred_element_type=F32)
        dbi_ref[...] += lax.dot_general(ub, lib, tn, preferred_element_type=F32)
        du = (lax.dot_general(lrb, brb, nt, preferred_element_type=F32)
              + lax.dot_general(lib, bib, nt, preferred_element_type=F32)
              + d_ref[...] * dyv)
        du_ref[...] = du.astype(du_ref.dtype)
        dd_ref[...] += jnp.sum(dyv * u, axis=0, keepdims=True)

    state_spec = pl.BlockSpec((None, 1, nlanes), lambda j, c: (chunk_of(c), 0, j))
    bshape = jax.ShapeDtypeStruct((nj, nch, nlanes), F32)
    cshape = jax.ShapeDtypeStruct((nj, nlanes, nch), F32)
    ashape = jax.ShapeDtypeStruct((1, nj * nlanes), F32)
    bspec = pl.BlockSpec((None, nch, nlanes), lambda j, c: (j, 0, 0))
    cspec = pl.BlockSpec((None, nlanes, nch), lambda j, c: (j, 0, 0))
    aspec = pl.BlockSpec((1, nlanes), lambda j, c: (0, j))
    big = pltpu.VMEM((tc, nlanes), F32)
    return pl.pallas_call(
        body,
        out_shape=(jax.ShapeDtypeStruct((L, w), BF16), bshape, bshape, cshape, cshape, ashape, ashape,
                   jax.ShapeDtypeStruct((1, w), F32)),
        grid=(nj, nc),
        in_specs=_ssm_specs(tc, nlanes, nch, u_col // nch, chunk_of)
        + [pl.BlockSpec((tc, nch), lambda j, c: (chunk_of(c), j)), state_spec, state_spec],
        out_specs=(pl.BlockSpec((tc, nch), lambda j, c: (chunk_of(c), j)), bspec, bspec, cspec, cspec, aspec, aspec,
                   pl.BlockSpec((1, nch), lambda j, c: (0, j))),
        scratch_shapes=[big, big, big, big, big, big, pltpu.VMEM((1, nlanes), F32), pltpu.VMEM((1, nlanes), F32)],
        compiler_params=_cparams("parallel", "arbitrary"),
        name=name,
    )(z, ar, ai, bblk_r, bblk_i, cblk_r, cblk_i, d, dy, s0r, s0i)


def _rmsnorm_rows(x, g):
    return x * lax.rsqrt(jnp.mean(x * x, axis=-1, keepdims=True) + RMS_EPS) * g


def _glu_out(y_raw, pre, b):
    yg = jax.nn.gelu(y_raw)
    return yg * jax.nn.sigmoid(pre + b)


def _gate_merge(za, zs, ba, bs, a, bm):
    return jax.nn.sigmoid(za + ba) * a + jax.nn.sigmoid(zs + bs) * bm


def _swiglu(g, u):
    return jax.nn.silu(g) * u


def _row_tile(width_bytes_per_row, rows):
    budget = VMEM_LIMIT_BYTES // 3
    t = max(8, min(1024, budget // (2 * max(width_bytes_per_row, 1))))
    return _pick(rows, t, 16)


def _ssm_params(p, prefix):
    g, pst = p["lam_re"].shape
    ar, ai, bbr, bbi = _ssm_prep(p["lam_re"], p["lam_im"], p["log_dt"], p["b_re_t"], p["b_im_t"], name=prefix + "_ssm_prep")
    return dict(ar=ar.reshape(1, g * pst), ai=ai.reshape(1, g * pst),
                bblk_r=_blockdiag(bbr), bblk_i=_blockdiag(bbi),
                cblk_r=_blockdiag(jnp.swapaxes(p["c_re"], 1, 2)), cblk_i=_blockdiag(jnp.swapaxes(p["c_im"], 1, 2)))


def _layer_fwd(x, p, dims, prefix):
    t, d = x.shape
    aw, kvw, sw, ff = dims["aw"], dims["kvw"], dims["sw"], dims["ff"]
    off_u = aw + 2 * kvw
    off_g = off_u + sw
    gblk = _pick(d, 512, 128)
    assert off_g % gblk == 0 and off_u % (SSM_LANE_GROUPS * SSM_GROUP_CH) == 0
    sv = {"x": x}

    h, = _rowmap(_rmsnorm_rows, [(x, "row", d, 0), (p["norm_mix_g"], "vec", d, 0)], [(d, BF16, "row", d)],
                 rows=t, tm=_row_tile(6 * d, t), name=prefix + "_norm_mix")
    z = _mm_nn(h, p["w_in"], out_dtype=F32, tm=512, tn=1664, tk=2048, name=prefix + "_mm_in")
    ya = _attn_fwd(z, p["q_norm_g"], p["k_norm_g"], p["attn_sinks"], n_q=dims["n_q"], n_kv=dims["n_kv"],
                   name=prefix + "_attn_fwd")
    sp = _ssm_params(p, prefix)
    y_raw, s0r, s0i = _ssm_fwd(z, sp["ar"], sp["ai"], sp["bblk_r"], sp["bblk_i"], sp["cblk_r"], sp["cblk_i"], p["ssm_d"],
                               u_col=off_u, tc=dims["tc"], name=prefix + "_ssm_fwd")
    yg, = _rowmap(jax.nn.gelu, [(y_raw, "row", sw, 0)], [(sw, BF16, "row", sw)],
                  rows=t, tm=_row_tile(6 * sw, t), name=prefix + "_gelu")
    pre = _mm_nn(yg, p["ssm_glu_w"], out_dtype=F32, tm=1024, tn=1024, tk=256, name=prefix + "_mm_glu")
    y2, = _rowmap(_glu_out, [(y_raw, "row", sw, 0), (pre, "row", sw, 0), (p["ssm_glu_b"], "vec", sw, 0)],
                  [(sw, BF16, "row", sw)], rows=t, tm=_row_tile(10 * sw, t), name=prefix + "_glu_out")
    a = _mm_nn(ya, p["w_attn_branch"], out_dtype=F32, tm=1024, tn=512, tk=1024, name=prefix + "_mm_ab")
    bm = _mm_nn(y2, p["w_ssm_branch"], out_dtype=F32, tm=1024, tn=512, tk=1024, name=prefix + "_mm_sb")
    ngb = d // gblk
    merged, = _rowmap(
        _gate_merge,
        [(z, "row", gblk, off_g // gblk), (z, "row", gblk, off_g // gblk + ngb),
         (p["gate_bias"], "vec", gblk, 0), (p["gate_bias"], "vec", gblk, ngb),
         (a, "row", gblk, 0), (bm, "row", gblk, 0)],
        [(d, BF16, "row", gblk)], rows=t, tm=_row_tile(18 * gblk, t), ncol=ngb, name=prefix + "_gate")
    x1 = _mm_nn(merged, p["w_out"], out_dtype=F32, tm=512, tn=2048, tk=512, name=prefix + "_mm_out", add=x)
    h2, = _rowmap(_rmsnorm_rows, [(x1, "row", d, 0), (p["norm_ffn_g"], "vec", d, 0)], [(d, BF16, "row", d)],
                  rows=t, tm=_row_tile(6 * d, t), name=prefix + "_norm_ffn")
    gu = _mm_nn(h2, p["w_ffn_in"], out_dtype=F32, tm=512, tn=1408, tk=2048, name=prefix + "_mm_ffn_in")
    fblk = _pick(ff, 1408, 128)
    nfb = ff // fblk
    act, = _rowmap(_swiglu, [(gu, "row", fblk, 0), (gu, "row", fblk, nfb)], [(ff, BF16, "row", fblk)],
                   rows=t, tm=_row_tile(10 * fblk, t), ncol=nfb, name=prefix + "_swiglu")
    x2 = _mm_nn(act, p["w_ffn_out"], out_dtype=F32, tm=512, tn=1024, tk=1408, name=prefix + "_mm_ffn_out", add=x1)
    sv.update(h=h, z=z, ya=ya, sp=sp, y_raw=y_raw, s0r=s0r, s0i=s0i, yg=yg, pre=pre, y2=y2, a=a, bm=bm,
              merged=merged, x1=x1, h2=h2, gu=gu, act=act)
    return x2, sv


def _layer_bwd(dx2, dx2b, sv, p, dims, prefix, gbuf):
    t, d = dx2.shape
    aw, kvw, sw, ff = dims["aw"], dims["kvw"], dims["sw"], dims["ff"]
    off_u = aw + 2 * kvw
    off_g = off_u + sw
    gblk = _pick(d, 512, 128)
    ngb = d // gblk
    fblk = _pick(ff, 1408, 128)
    nfb = ff // fblk
    g = {}

    dact = _mm_nt(dx2b, p["w_ffn_out"], out_dtype=F32, tm=512, tn=2048, tko=1408, name=prefix + "_mm_dact")
    g["w_ffn_out"] = _mm_tn(sv["act"], dx2b, p["w_ffn_out"], into=gbuf.get("w_ffn_out"), tm=512, tn=1024, tko=1408,
                            name=prefix + "_mm_dw_ffn_out")

    def swiglu_bwd(gg, uu, da):
        _, vjp = jax.vjp(_swiglu, gg, uu)
        return vjp(da)

    dgu_g, dgu_u = _rowmap(swiglu_bwd, [(sv["gu"], "row", fblk, 0), (sv["gu"], "row", fblk, nfb), (dact, "row", fblk, 0)],
                           [(ff, BF16, "row", fblk), (ff, BF16, "row", fblk)],
                           rows=t, tm=_row_tile(16 * fblk, t), ncol=nfb, name=prefix + "_swiglu_bwd")
    dgu = jnp.concatenate([dgu_g, dgu_u], axis=1)
    dh2 = _mm_nt(dgu, p["w_ffn_in"], out_dtype=F32, tm=512, tn=1408, tko=2048, name=prefix + "_mm_dh2")
    g["w_ffn_in"] = _mm_tn(sv["h2"], dgu, p["w_ffn_in"], into=gbuf.get("w_ffn_in"), tm=512, tn=1408, tko=1024,
                           name=prefix + "_mm_dw_ffn_in")

    def norm_bwd(xx, gg, dh, dres):
        _, vjp = jax.vjp(_rmsnorm_rows, xx, gg)
        dxx, dgg = vjp(dh)
        dxx = dxx + dres
        return dxx, dxx, dgg

    dx1, dx1b, g["norm_ffn_g"] = _rowmap(
        norm_bwd, [(sv["x1"], "row", d, 0), (p["norm_ffn_g"], "vec", d, 0), (dh2, "row", d, 0), (dx2, "row", d, 0)],
        [(d, F32, "row", d), (d, BF16, "row", d), (d, F32, "acc", d)],
        rows=t, tm=_row_tile(22 * d, t), name=prefix + "_norm_ffn_bwd")

    dmerged = _mm_nt(dx1b, p["w_out"], out_dtype=F32, tm=1024, tn=2048, tko=512, name=prefix + "_mm_dmerged")
    g["w_out"] = _mm_tn(sv["merged"], dx1b, p["w_out"], into=gbuf.get("w_out"), tm=1024, tn=2048, tko=512,
                        name=prefix + "_mm_dw_out")

    def gate_bwd(za, zs, ba, bs, aa, bb, dm):
        _, vjp = jax.vjp(_gate_merge, za, zs, ba, bs, aa, bb)
        dza, dzs, dba, dbs, daa, dbb = vjp(dm)
        return daa, dbb, dza, dzs, dba, dbs

    z = sv["z"]
    da, dbm, dza, dzs, dba, dbs = _rowmap(
        gate_bwd,
        [(z, "row", gblk, off_g // gblk), (z, "row", gblk, off_g // gblk + ngb),
         (p["gate_bias"], "vec", gblk, 0), (p["gate_bias"], "vec", gblk, ngb),
         (sv["a"], "row", gblk, 0), (sv["bm"], "row", gblk, 0), (dmerged, "row", gblk, 0)],
        [(d, BF16, "row", gblk), (d, BF16, "row", gblk), (d, BF16, "row", gblk), (d, BF16, "row", gblk),
         (d, F32, "acc", gblk), (d, F32, "acc", gblk)],
        rows=t, tm=_row_tile(32 * gblk, t), ncol=ngb, name=prefix + "_gate_bwd")
    g["gate_bias"] = jnp.concatenate([dba, dbs], axis=1)
    dya = _mm_nt(da, p["w_attn_branch"], out_dtype=F32, tm=1024, tn=512, tko=1024, name=prefix + "_mm_dya")
    g["w_attn_branch"] = _mm_tn(sv["ya"], da, p["w_attn_branch"], into=gbuf.get("w_attn_branch"), tm=1024, tn=512,
                                tko=1024, name=prefix + "_mm_dw_ab")
    dy2 = _mm_nt(dbm, p["w_ssm_branch"], out_dtype=F32, tm=1024, tn=512, tko=1024, name=prefix + "_mm_dy2")
    g["w_ssm_branch"] = _mm_tn(sv["y2"], dbm, p["w_ssm_branch"], into=gbuf.get("w_ssm_branch"), tm=1024, tn=512,
                               tko=1024, name=prefix + "_mm_dw_sb")

    def glu_bwd(y_raw, pre, b, dy):
        yg = jax.nn.gelu(y_raw)
        _, vjp = jax.vjp(lambda a_, b_, c_: a_ * jax.nn.sigmoid(b_ + c_), yg, pre, b)
        dyg, dpre, db = vjp(dy)
        return dyg, dpre, db

    dyg_direct, dpre, g["ssm_glu_b"] = _rowmap(
        glu_bwd, [(sv["y_raw"], "row", sw, 0), (sv["pre"], "row", sw, 0), (p["ssm_glu_b"], "vec", sw, 0), (dy2, "row", sw, 0)],
        [(sw, F32, "row", sw), (sw, BF16, "row", sw), (sw, F32, "acc", sw)],
        rows=t, tm=_row_tile(24 * sw, t), name=prefix + "_glu_bwd")
    dyg2 = _mm_nt(dpre, p["ssm_glu_w"], out_dtype=F32, tm=1024, tn=1024, tko=256, name=prefix + "_mm_dyg")
    g["ssm_glu_w"] = _mm_tn(sv["yg"], dpre, p["ssm_glu_w"], into=gbuf.get("ssm_glu_w"), tm=1024, tn=1024, tko=256,
                            name=prefix + "_mm_dw_glu")

    def gelu_bwd(y_raw, d1, d2):
        _, vjp = jax.vjp(jax.nn.gelu, y_raw)
        return vjp(d1 + d2)[0]

    dy_raw, = _rowmap(gelu_bwd, [(sv["y_raw"], "row", sw, 0), (dyg_direct, "row", sw, 0), (dyg2, "row", sw, 0)],
                      [(sw, F32, "row", sw)], rows=t, tm=_row_tile(20 * sw, t), name=prefix + "_gelu_bwd")
    sp = sv["sp"]
    du, dbr, dbi, dcr, dci, dar, dai, g["ssm_d"] = _ssm_bwd(
        z, dy_raw, sv["s0r"], sv["s0i"], sp["ar"], sp["ai"], sp["bblk_r"], sp["bblk_i"], sp["cblk_r"], sp["cblk_i"],
        p["ssm_d"], u_col=off_u, tc=dims["tc"], name=prefix + "_ssm_bwd")
    ngr, pst = p["lam_re"].shape
    hch = SSM_GROUP_CH
    dlr, dli, dldt, dbrt, dbit = _ssm_prep_bwd(
        p["lam_re"], p["lam_im"], p["log_dt"], p["b_re_t"], p["b_im_t"],
        dar.reshape(ngr, pst), dai.reshape(ngr, pst), _blockdiag_extract(dbr, hch, pst), _blockdiag_extract(dbi, hch, pst),
        name=prefix + "_ssm_prep_bwd")
    g.update(ssm_lambda_re=dlr, ssm_lambda_im=dli, ssm_log_dt=dldt.reshape(ngr),
             ssm_b_re=jnp.swapaxes(dbrt, 1, 2), ssm_b_im=jnp.swapaxes(dbit, 1, 2),
             ssm_c_re=jnp.swapaxes(_blockdiag_extract(dcr, pst, hch), 1, 2),
             ssm_c_im=jnp.swapaxes(_blockdiag_extract(dci, pst, hch), 1, 2))

    dq, dkv, g["q_norm_g"], g["k_norm_g"], g["attn_sinks"] = _attn_bwd(
        z, dya, p["q_norm_g"], p["k_norm_g"], p["attn_sinks"], n_q=dims["n_q"], n_kv=dims["n_kv"], name=prefix + "_attn_bwd")

    dz = jnp.concatenate([dq, dkv, du, dza, dzs], axis=1)
    dh = _mm_nt(dz, p["w_in"], out_dtype=F32, tm=512, tn=1664, tko=2048, name=prefix + "_mm_dh")
    g["w_in"] = _mm_tn(sv["h"], dz, p["w_in"], into=gbuf.get("w_in"), tm=512, tn=1664, tko=1024,
                       name=prefix + "_mm_dw_in")
    dx, dxb, g["norm_mix_g"] = _rowmap(
        norm_bwd, [(sv["x"], "row", d, 0), (p["norm_mix_g"], "vec", d, 0), (dh, "row", d, 0), (dx1, "row", d, 0)],
        [(d, F32, "row", d), (d, BF16, "row", d), (d, F32, "acc", d)],
        rows=t, tm=_row_tile(22 * d, t), name=prefix + "_norm_mix_bwd")
    return dx, dxb, g


def _loss_and_grad(y, target):
    t, d = y.shape

    def fn(yy, tt):
        e = yy - tt
        dy = e * (1.0 / d)
        return dy, dy, jnp.sum(e * e, keepdims=True).reshape(1, 1)

    dy, dyb, sq = _rowmap(fn, [(y, "row", d, 0), (target, "row", d, 0)],
                          [(d, F32, "row", d), (d, BF16, "row", d), (1, F32, "acc", 1)],
                          rows=t, tm=_row_tile(14 * d, t), name="loss")
    return sq, dy, dyb


def _local_step(x, target, params, dims):
    saved = []
    h = x
    for l, p in enumerate(params):
        h, sv = _layer_fwd(h, p, dims, "l%d" % l)
        saved.append(sv)
    sq, dy, dyb = _loss_and_grad(h, target)
    grads = [None] * len(params)
    gbuf = {}
    for l in reversed(range(len(params))):
        dy, dyb, grads[l] = _layer_bwd(dy, dyb, saved[l], params[l], dims, "l%d" % l, gbuf)
        gbuf = {n: grads[l].pop(n) for n in BIG_WEIGHTS}
    return sq, dy, grads, gbuf


COL_SHARDED = ("w_in", "w_attn_branch", "w_ssm_branch", "w_ffn_in")
ROW_SHARDED = ("ssm_glu_w", "w_out", "w_ffn_out")
BIG_WEIGHTS = COL_SHARDED + ROW_SHARDED
WEIGHT_NAMES = ("norm_mix_g", "w_in", "gate_bias", "q_norm_g", "k_norm_g", "attn_sinks", "ssm_lambda_re",
                "ssm_lambda_im", "ssm_log_dt", "ssm_b_re", "ssm_b_im", "ssm_c_re", "ssm_c_im", "ssm_d", "ssm_glu_w",
                "ssm_glu_b", "w_attn_branch", "w_ssm_branch", "w_out", "norm_ffn_g", "w_ffn_in", "w_ffn_out")
SMALL_WEIGHTS = tuple(n for n in WEIGHT_NAMES if n not in BIG_WEIGHTS)


def _dims(d, big, tc):
    s, _, aw, _ = big["w_attn_branch"].shape
    sw = big["w_ssm_branch"].shape[2]
    in_w = big["w_in"].shape[3] * s
    kvw = (in_w - aw - sw - 2 * d) // 2
    ff = big["w_ffn_out"].shape[2] * s
    return dict(aw=aw, kvw=kvw, sw=sw, ff=ff, n_q=aw // HEAD_DIM, n_kv=kvw // HEAD_DIM, tc=tc)


def _layer_params(l, small, big):
    p = {n: _Weight(big[n], l, "col") for n in COL_SHARDED}
    p.update({n: _Weight(big[n], l, "row") for n in ROW_SHARDED})
    for n in ("norm_mix_g", "gate_bias", "q_norm_g", "k_norm_g", "ssm_d", "ssm_glu_b", "norm_ffn_g"):
        p[n] = small[n][l][None]
    p["attn_sinks"] = small["attn_sinks"][l]
    p["lam_re"] = small["ssm_lambda_re"][l]
    p["lam_im"] = small["ssm_lambda_im"][l]
    p["log_dt"] = small["ssm_log_dt"][l][:, None]
    p["b_re_t"] = jnp.swapaxes(small["ssm_b_re"][l], 1, 2)
    p["b_im_t"] = jnp.swapaxes(small["ssm_b_im"][l], 1, 2)
    p["c_re"] = small["ssm_c_re"][l]
    p["c_im"] = small["ssm_c_im"][l]
    return p


_ANY = pl.BlockSpec(memory_space=pl.ANY)
_MESH_ID = pl.DeviceIdType.MESH


def _coords():
    return lax.axis_index("x"), lax.axis_index("y"), lax.axis_index("c")


def _remote(src, dst, send_sems, recv_sems, k, to):
    return pltpu.make_async_remote_copy(src_ref=src, dst_ref=dst, send_sem=send_sems.at[k], recv_sem=recv_sems.at[k],
                                        device_id=to, device_id_type=_MESH_ID)


def _comm_call(body, ins, out_shapes, n_remote, name, aliases=None, scratch=()):
    return pl.pallas_call(
        body,
        out_shape=tuple(out_shapes),
        in_specs=[_ANY] * len(ins),
        out_specs=tuple([_ANY] * len(out_shapes)),
        scratch_shapes=[pltpu.SemaphoreType.DMA((n_remote,)), pltpu.SemaphoreType.DMA((n_remote,))] + list(scratch),
        input_output_aliases=aliases or {},
        compiler_params=pltpu.CompilerParams(has_side_effects=True),
        name=name,
    )(*ins)


def _my_chip():
    return (2 * lax.axis_index("x") + lax.axis_index("y")).astype(jnp.int32).reshape(1)


def _my_core():
    return lax.axis_index("c").astype(jnp.int32).reshape(1)


def _cast_into_slot(w, *, name):
    nl, r, c = w.shape
    tm = _row_tile(12 * c, r)

    def body(me_ref, w_ref, o_ref):
        o_ref[...] = w_ref[...].astype(o_ref.dtype)

    return pl.pallas_call(
        body,
        out_shape=jax.ShapeDtypeStruct((N_CHIPS, nl, r, c), BF16),
        grid_spec=pltpu.PrefetchScalarGridSpec(
            num_scalar_prefetch=1,
            grid=(nl, r // tm),
            in_specs=[pl.BlockSpec((None, tm, c), lambda l, i, me: (l, i, 0))],
            out_specs=pl.BlockSpec((None, None, tm, c), lambda l, i, me: (me[0], l, i, 0)),
        ),
        compiler_params=_cparams("parallel", "parallel"),
        name=name,
    )(_my_chip(), w)


def _allgather_weights(bufs, *, name):
    n = len(bufs)

    def body(*refs):
        outs = refs[n:2 * n]
        send_sems, recv_sems = refs[2 * n:]
        x, y, c = _coords()
        me = 2 * x + y
        chips = [(1 - x, y), (x, 1 - y), (1 - x, 1 - y)]
        sibling = (x, y, 1 - c)

        def half(i, slot, hc):
            rh = bufs[i].shape[2] // 2
            return outs[i].at[slot, :, pl.ds(hc * rh, rh), :]

        first = [_remote(half(i, me, c), half(i, me, c), send_sems, recv_sems, 6 * i + k, (px, py, c))
                 for i in range(n) for k, (px, py) in enumerate(chips)]
        for cp in first:
            cp.start()
        passed = []
        for k, (px, py) in enumerate(chips):
            for i in range(n):
                landed = half(i, 2 * px + py, c)
                _remote(landed, landed, send_sems, recv_sems, 6 * i + k, (px, py, c)).wait_recv()
                fw = _remote(landed, landed, send_sems, recv_sems, 6 * i + 3 + k, sibling)
                fw.start()
                passed.append(fw)
        for k, (px, py) in enumerate(chips):
            for i in range(n):
                other = half(i, 2 * px + py, 1 - c)
                _remote(other, other, send_sems, recv_sems, 6 * i + 3 + k, sibling).wait_recv()
        for cp in first + passed:
            cp.wait_send()

    outs = [jax.ShapeDtypeStruct(b.shape, b.dtype) for b in bufs]
    return _comm_call(body, bufs, outs, 6 * n, name, aliases={i: i for i in range(n)})


def _send_other_half(grads, *, name):
    n = len(grads)

    def body(*refs):
        ins, theirs = refs[:n], refs[n:2 * n]
        send_sems, recv_sems = refs[2 * n:]
        x, y, c = _coords()
        sibling = (x, y, 1 - c)
        sends = []
        for i in range(n):
            rh = grads[i].shape[1] // 2
            cp = _remote(ins[i].at[:, pl.ds((1 - c) * rh, rh), :], theirs[i], send_sems, recv_sems, i, sibling)
            cp.start()
            sends.append(cp)
        for cp in sends:
            cp.wait_recv()
        for cp in sends:
            cp.wait_send()

    shapes = [jax.ShapeDtypeStruct((g.shape[0], g.shape[1] // 2, g.shape[2]), g.dtype) for g in grads]
    return _comm_call(body, grads, shapes, n, name)


def _add_own_half(g, theirs, *, name):
    s, r, c = g.shape
    rh = r // 2
    tm = _row_tile(10 * c, rh)
    nb = rh // tm

    def body(core_ref, g_ref, t_ref, o_ref):
        o_ref[...] = (g_ref[...] + t_ref[...]).astype(o_ref.dtype)

    return pl.pallas_call(
        body,
        out_shape=jax.ShapeDtypeStruct((s, rh, c), BF16),
        grid_spec=pltpu.PrefetchScalarGridSpec(
            num_scalar_prefetch=1,
            grid=(s, nb),
            in_specs=[pl.BlockSpec((None, tm, c), lambda k, i, core: (k, core[0] * nb + i, 0)),
                      pl.BlockSpec((None, tm, c), lambda k, i, core: (k, i, 0))],
            out_specs=pl.BlockSpec((None, tm, c), lambda k, i, core: (k, i, 0)),
        ),
        compiler_params=_cparams("parallel", "parallel"),
        name=name,
    )(_my_core(), g, theirs)


def _exchange_chips(parts, *, name):
    n = len(parts)

    def body(*refs):
        ins, outs = refs[:n], refs[n:2 * n]
        send_sems, recv_sems = refs[2 * n:]
        x, y, c = _coords()
        me = 2 * x + y
        chips = [(1 - x, y), (x, 1 - y), (1 - x, 1 - y)]
        sends = [_remote(ins[i].at[:, 2 * px + py], outs[i].at[:, me], send_sems, recv_sems, 3 * i + k, (px, py, c))
                 for i in range(n) for k, (px, py) in enumerate(chips)]
        for cp in sends:
            cp.start()
        for i in range(n):
            for k, (px, py) in enumerate(chips):
                slot = outs[i].at[:, 2 * px + py]
                _remote(slot, slot, send_sems, recv_sems, 3 * i + k, (px, py, c)).wait_recv()
        for cp in sends:
            cp.wait_send()

    return _comm_call(body, parts, [jax.ShapeDtypeStruct(p.shape, p.dtype) for p in parts], 3 * n, name)


def _sum_chips(part, got, *, name):
    nl, s, rh, c = part.shape
    tm = _row_tile(14 * c, rh)

    def body(me_ref, p_ref, a_ref, b_ref, c_ref, o_ref):
        o_ref[...] = ((p_ref[...].astype(F32) + a_ref[...].astype(F32)) + b_ref[...].astype(F32)) + c_ref[...].astype(F32)

    slot = lambda k: (lambda l, i, me: (l, (me[0] + k) % s, i, 0))
    return pl.pallas_call(
        body,
        out_shape=jax.ShapeDtypeStruct((nl, rh, c), F32),
        grid_spec=pltpu.PrefetchScalarGridSpec(
            num_scalar_prefetch=1,
            grid=(nl, rh // tm),
            in_specs=[pl.BlockSpec((None, None, tm, c), slot(k)) for k in range(s)],
            out_specs=pl.BlockSpec((None, tm, c), lambda l, i, me: (l, i, 0)),
        ),
        compiler_params=_cparams("parallel", "parallel"),
        name=name,
    )(_my_chip(), part, got, got, got)


def _send_to_sibling(arrays, *, name):
    n = len(arrays)

    def body(*refs):
        ins, outs = refs[:n], refs[n:2 * n]
        send_sems, recv_sems = refs[2 * n:]
        x, y, c = _coords()
        sends = [_remote(ins[i], outs[i], send_sems, recv_sems, i, (x, y, 1 - c)) for i in range(n)]
        for cp in sends:
            cp.start()
        for cp in sends:
            cp.wait_recv()
        for cp in sends:
            cp.wait_send()

    return _comm_call(body, arrays, [jax.ShapeDtypeStruct(a.shape, a.dtype) for a in arrays], n, name)


def _allgather_all(buf, *, name):
    def body(in_ref, out_ref, send_sems, recv_sems, stage_ref, local_sems):
        x, y, c = _coords()
        flip = lambda v, b: (1 - v) if b else v
        peers = [(flip(x, r & 4), flip(y, r & 2), flip(c, r & 1)) for r in range(1, N_DEV)]
        slot_of = lambda p: 4 * p[0] + 2 * p[1] + p[2]
        mine = out_ref.at[slot_of((x, y, c))]
        sends = [_remote(in_ref, mine, send_sems, recv_sems, k, p) for k, p in enumerate(peers)]
        for cp in sends:
            cp.start()
        load = pltpu.make_async_copy(in_ref, stage_ref, local_sems.at[0])
        load.start()
        load.wait()
        store = pltpu.make_async_copy(stage_ref, mine, local_sems.at[1])
        store.start()
        for k, p in enumerate(peers):
            slot = out_ref.at[slot_of(p)]
            _remote(slot, slot, send_sems, recv_sems, k, p).wait_recv()
        for cp in sends:
            cp.wait_send()
        store.wait()

    return _comm_call(body, [buf], [jax.ShapeDtypeStruct((N_DEV,) + buf.shape, buf.dtype)], N_DEV - 1, name,
                      scratch=[pltpu.VMEM(buf.shape, buf.dtype), pltpu.SemaphoreType.DMA((2,))])[0]


def _sum_slots(arr, *, name):
    s, r, c = arr.shape
    tm = _row_tile(4 * c * (s + 1), r)

    def body(*refs):
        acc = refs[0][...]
        for ref in refs[1:s]:
            acc = acc + ref[...]
        refs[s][...] = acc

    return pl.pallas_call(
        body,
        out_shape=jax.ShapeDtypeStruct((r, c), arr.dtype),
        grid=(r // tm,),
        in_specs=[pl.BlockSpec((None, tm, c), lambda i, k=k: (k, i, 0)) for k in range(s)],
        out_specs=pl.BlockSpec((tm, c), lambda i: (i, 0)),
        compiler_params=_cparams("parallel"),
        name=name,
    )(*([arr] * s))


def _reduce_scatter(grads):
    flat = [g.reshape((g.shape[0] * g.shape[1],) + g.shape[2:]) for g in grads]
    theirs = _send_other_half(flat, name="rs_send_other_half")
    parts = [_add_own_half(f, t, name="rs_add_own_half_%d" % i).reshape(g.shape[:2] + (g.shape[2] // 2, g.shape[3]))
             for i, (f, t, g) in enumerate(zip(flat, theirs, grads))]
    got = _exchange_chips(parts, name="rs_exchange_chips")
    mine = [_sum_chips(p, g, name="rs_sum_chips_%d" % i) for i, (p, g) in enumerate(zip(parts, got))]
    return mine, _send_to_sibling(mine, name="rs_send_to_sibling")


def _adamw_fn(w, g, m, v):
    m = ADAM_B1 * m + (1.0 - ADAM_B1) * g
    v = ADAM_B2 * v + (1.0 - ADAM_B2) * jnp.square(g)
    m_hat = m / (1.0 - ADAM_B1 ** ADAM_STEP)
    v_hat = v / (1.0 - ADAM_B2 ** ADAM_STEP)
    delta = -ADAM_LR * (m_hat / (jnp.sqrt(v_hat) + ADAM_EPS) + ADAM_WD * w)
    return delta, m, v


def _adamw(w, g, m, v, *, name):
    rows, cols = w.shape
    ins = [(a, "row", cols, 0) for a in (w, g, m, v)]
    outs = [(cols, F32, "row", cols)] * 3
    return _rowmap(_adamw_fn, ins, outs, rows=rows, tm=_row_tile(56 * cols, rows), name=name)


def _adamw_sharded(w, m, v, g_mine, g_sibling, *, name):
    nl, r, c = w.shape
    rh = r // 2
    tm = _row_tile(40 * c, rh)
    nb = rh // tm

    def body(core_ref, w_ref, m_ref, v_ref, a_ref, b_ref, g_ref, d_ref, nm_ref, nv_ref):
        g = jnp.where(pl.program_id(1) == core_ref[0], a_ref[...], b_ref[...])
        delta, nm, nv = _adamw_fn(w_ref[...], g, m_ref[...], v_ref[...])
        g_ref[...] = g
        d_ref[...] = delta
        nm_ref[...] = nm
        nv_ref[...] = nv

    whole = pl.BlockSpec((None, tm, c), lambda l, h, i, core: (l, h * nb + i, 0))
    half = pl.BlockSpec((None, tm, c), lambda l, h, i, core: (l, i, 0))
    shape = jax.ShapeDtypeStruct((nl, r, c), F32)
    return pl.pallas_call(
        body,
        out_shape=(shape, shape, shape, shape),
        grid_spec=pltpu.PrefetchScalarGridSpec(
            num_scalar_prefetch=1,
            grid=(nl, 2, nb),
            in_specs=[whole, whole, whole, half, half],
            out_specs=(whole, whole, whole, whole),
        ),
        compiler_params=_cparams("parallel", "parallel", "parallel"),
        name=name,
    )(_my_core(), w, m, v, g_mine, g_sibling)


def _pack(arrays):
    flat = jnp.concatenate([a.reshape(-1) for a in arrays])
    pad = (-flat.shape[0]) % (256 * 128)
    return jnp.pad(flat, (0, pad)).reshape(-1, 128)


def _unpack(buf, shapes):
    flat = buf.reshape(-1)
    out, off = [], 0
    for s in shapes:
        n = math.prod(s)
        out.append(flat[off:off + n].reshape(s))
        off += n
    return out


def kernel(x, norm_mix_g, w_in, gate_bias, q_norm_g, k_norm_g, attn_sinks, ssm_lambda_re, ssm_lambda_im, ssm_log_dt, ssm_b_re, ssm_b_im, ssm_c_re, ssm_c_im, ssm_d, ssm_glu_w, ssm_glu_b, w_attn_branch, w_ssm_branch, w_out, norm_ffn_g, w_ffn_in, w_ffn_out, loss_target, m_norm_mix_g, m_w_in, m_gate_bias, m_q_norm_g, m_k_norm_g, m_attn_sinks, m_ssm_lambda_re, m_ssm_lambda_im, m_ssm_log_dt, m_ssm_b_re, m_ssm_b_im, m_ssm_c_re, m_ssm_c_im, m_ssm_d, m_ssm_glu_w, m_ssm_glu_b, m_w_attn_branch, m_w_ssm_branch, m_w_out, m_norm_ffn_g, m_w_ffn_in, m_w_ffn_out, v_norm_mix_g, v_w_in, v_gate_bias, v_q_norm_g, v_k_norm_g, v_attn_sinks, v_ssm_lambda_re, v_ssm_lambda_im, v_ssm_log_dt, v_ssm_b_re, v_ssm_b_im, v_ssm_c_re, v_ssm_c_im, v_ssm_d, v_ssm_glu_w, v_ssm_glu_b, v_w_attn_branch, v_w_ssm_branch, v_w_out, v_norm_ffn_g, v_w_ffn_in, v_w_ffn_out):
    w = dict(norm_mix_g=norm_mix_g, w_in=w_in, gate_bias=gate_bias, q_norm_g=q_norm_g, k_norm_g=k_norm_g,
             attn_sinks=attn_sinks, ssm_lambda_re=ssm_lambda_re, ssm_lambda_im=ssm_lambda_im, ssm_log_dt=ssm_log_dt,
             ssm_b_re=ssm_b_re, ssm_b_im=ssm_b_im, ssm_c_re=ssm_c_re, ssm_c_im=ssm_c_im, ssm_d=ssm_d,
             ssm_glu_w=ssm_glu_w, ssm_glu_b=ssm_glu_b, w_attn_branch=w_attn_branch, w_ssm_branch=w_ssm_branch,
             w_out=w_out, norm_ffn_g=norm_ffn_g, w_ffn_in=w_ffn_in, w_ffn_out=w_ffn_out)
    m = dict(norm_mix_g=m_norm_mix_g, w_in=m_w_in, gate_bias=m_gate_bias, q_norm_g=m_q_norm_g, k_norm_g=m_k_norm_g,
             attn_sinks=m_attn_sinks, ssm_lambda_re=m_ssm_lambda_re, ssm_lambda_im=m_ssm_lambda_im,
             ssm_log_dt=m_ssm_log_dt, ssm_b_re=m_ssm_b_re, ssm_b_im=m_ssm_b_im, ssm_c_re=m_ssm_c_re,
             ssm_c_im=m_ssm_c_im, ssm_d=m_ssm_d, ssm_glu_w=m_ssm_glu_w, ssm_glu_b=m_ssm_glu_b,
             w_attn_branch=m_w_attn_branch, w_ssm_branch=m_w_ssm_branch, w_out=m_w_out, norm_ffn_g=m_norm_ffn_g,
             w_ffn_in=m_w_ffn_in, w_ffn_out=m_w_ffn_out)
    v = dict(norm_mix_g=v_norm_mix_g, w_in=v_w_in, gate_bias=v_gate_bias, q_norm_g=v_q_norm_g, k_norm_g=v_k_norm_g,
             attn_sinks=v_attn_sinks, ssm_lambda_re=v_ssm_lambda_re, ssm_lambda_im=v_ssm_lambda_im,
             ssm_log_dt=v_ssm_log_dt, ssm_b_re=v_ssm_b_re, ssm_b_im=v_ssm_b_im, ssm_c_re=v_ssm_c_re,
             ssm_c_im=v_ssm_c_im, ssm_d=v_ssm_d, ssm_glu_w=v_ssm_glu_w, ssm_glu_b=v_ssm_glu_b,
             w_attn_branch=v_w_attn_branch, w_ssm_branch=v_w_ssm_branch, w_out=v_w_out, norm_ffn_g=v_norm_ffn_g,
             w_ffn_in=v_w_ffn_in, w_ffn_out=v_w_ffn_out)
    n_layers = norm_mix_g.shape[0]
    d_model = x.shape[-1]
    seq = x.shape[1]

    gathered = _allgather_weights([_cast_into_slot(w[n], name="cast_" + n) for n in BIG_WEIGHTS],
                                  name="allgather_weights")
    big = dict(zip(BIG_WEIGHTS, gathered))
    dims = _dims(d_model, big, min(512, seq))
    params = [_layer_params(l, w, big) for l in range(n_layers)]
    sq, dx, grads, big_grads = _local_step(x[0], loss_target[0], params, dims)
    loss = lax.psum(sq[0, 0], MESH_AXES) * (0.5 / d_model)

    g_mine, g_sibling = _reduce_scatter([big_grads[n] for n in BIG_WEIGHTS])
    grad, delta, new_m, new_v = {}, {}, {}, {}
    for n, ga, gb in zip(BIG_WEIGHTS, g_mine, g_sibling):
        grad[n], delta[n], new_m[n], new_v[n] = _adamw_sharded(w[n], m[n], v[n], ga, gb, name="adamw_" + n)
    small_local = [jnp.stack([grads[l][n].reshape(w[n].shape[1:]) for l in range(n_layers)]) for n in SMALL_WEIGHTS]
    small_sum = _sum_slots(_allgather_all(_pack(small_local), name="allgather_small_grads"), name="sum_small_grads")
    grad.update(zip(SMALL_WEIGHTS, _unpack(small_sum, [w[n].shape for n in SMALL_WEIGHTS])))
    small_shapes = [w[n].shape for n in SMALL_WEIGHTS]
    res = _adamw(_pack([w[n] for n in SMALL_WEIGHTS]), small_sum, _pack([m[n] for n in SMALL_WEIGHTS]),
                 _pack([v[n] for n in SMALL_WEIGHTS]), name="adamw_small")
    for out, packed in zip((delta, new_m, new_v), res):
        out.update(zip(SMALL_WEIGHTS, _unpack(packed, small_shapes)))

    return (loss, dx[None], *[grad[n] for n in WEIGHT_NAMES], *[delta[n] for n in WEIGHT_NAMES],
            *[new_m[n] for n in WEIGHT_NAMES], *[new_v[n] for n in WEIGHT_NAMES])
```

```python
import functools
import math

import jax
import jax.numpy as jnp
from jax import lax
from jax.experimental import pallas as pl
from jax.experimental.pallas import tpu as pltpu

HEAD_DIM = 64
WINDOW = 128
SSM_GROUP_CH = 16
SSM_LANE_GROUPS = 8
RMS_EPS = 1e-6
ADAM_LR = 0.001
ADAM_B1 = 0.9
ADAM_B2 = 0.999
ADAM_EPS = 1e-08
ADAM_WD = 0.01
ADAM_STEP = 10
NEG_BIG = -1e30
MESH_AXES = ("x", "y", "c")
N_CHIPS = 4
N_DEV = 8
VMEM_LIMIT_BYTES = 56 * 1024 * 1024
BF16 = jnp.bfloat16
F32 = jnp.float32


def _cparams(*semantics):
    return pltpu.CompilerParams(dimension_semantics=semantics, vmem_limit_bytes=VMEM_LIMIT_BYTES)


def _pick(n, target, mult):
    if n <= target:
        return n
    best = None
    for d in range(mult, target + 1, mult):
        if n % d == 0:
            best = d
    assert best is not None, (n, target, mult)
    return best


def _rowmap(fn, ins, outs, *, rows, tm, ncol=1, name, deps=()):
    n_in = len(ins)
    nrow = rows // tm
    assert nrow * tm == rows

    in_specs = []
    for arr, kind, width, coloff in ins:
        if kind == "row":
            in_specs.append(pl.BlockSpec((tm, width), lambda j, i, o=coloff: (i, o + j)))
        elif kind == "vec":
            in_specs.append(pl.BlockSpec((1, width), lambda j, i, o=coloff: (0, o + j)))
        else:
            nd = arr.ndim
            in_specs.append(pl.BlockSpec(arr.shape, lambda j, i, nd=nd: (0,) * nd))
    out_specs, out_shapes = [], []
    for cols, dtype, kind, width in outs:
        if kind == "row":
            out_specs.append(pl.BlockSpec((tm, width), lambda j, i: (i, j)))
            out_shapes.append(jax.ShapeDtypeStruct((rows, cols), dtype))
        else:
            out_specs.append(pl.BlockSpec((1, width), lambda j, i: (0, j)))
            out_shapes.append(jax.ShapeDtypeStruct((1, cols), dtype))

    in_specs += [pl.BlockSpec(memory_space=pl.ANY)] * len(deps)

    def body(*refs):
        i = pl.program_id(1)
        res = fn(*[r[...] for r in refs[:n_in]])
        if not isinstance(res, (tuple, list)):
            res = (res,)
        for (cols, dtype, kind, width), ref, val in zip(outs, refs[n_in + len(deps):], res):
            if kind == "row":
                ref[...] = val.astype(ref.dtype)
            else:
                @pl.when(i == 0)
                def _():
                    ref[...] = jnp.zeros_like(ref)
                ref[...] += val.astype(ref.dtype)

    res = pl.pallas_call(
        body,
        out_shape=tuple(out_shapes),
        grid=(ncol, nrow),
        in_specs=in_specs,
        out_specs=tuple(out_specs),
        compiler_params=_cparams("parallel", "arbitrary"),
        name=name,
    )(*[a[0] for a in ins], *deps)
    return res


def _mm_body(dims, nk, has_add, unused_in=0):
    def body(*refs):
        if has_add:
            a_ref, b_ref, add_ref, o_ref = refs[:4]
            rest = refs[4:]
        else:
            a_ref, b_ref = refs[:2]
            o_ref = refs[2 + unused_in]
            add_ref = None
            rest = refs[3 + unused_in:]
        part = lax.dot_general(a_ref[...], b_ref[...], (dims, ((), ())), preferred_element_type=F32)
        if nk == 1:
            if add_ref is not None:
                part = part + add_ref[...]
            o_ref[...] = part.astype(o_ref.dtype)
        else:
            acc_ref = rest[0]
            k = pl.program_id(2)

            @pl.when(k == 0)
            def _():
                acc_ref[...] = part

            @pl.when(k > 0)
            def _():
                acc_ref[...] += part

            @pl.when(k == nk - 1)
            def _():
                r = acc_ref[...]
                if add_ref is not None:
                    r = r + add_ref[...]
                o_ref[...] = r.astype(o_ref.dtype)
    return body


class _Weight:
    def __init__(self, arr, layer, kind):
        self.arr, self.layer, self.kind = arr, layer, kind
        self.s, _, self.r, self.c = arr.shape
        self.rows = self.r * (self.s if kind == "row" else 1)
        self.cols = self.c * (self.s if kind == "col" else 1)

    def tiles(self, tr, tc):
        return _pick(self.r, tr, 128), _pick(self.c, tc, 128)

    def index(self, tr, tc):
        layer = self.layer
        if self.kind == "col":
            per = self.c // tc
            return lambda rb, cb: (cb // per, layer, rb, cb % per)
        per = self.r // tr
        return lambda rb, cb: (rb // per, layer, rb % per, cb)


def _shard_index(kind, r, c, tr, tc):
    if kind == "col":
        per = c // tc
        return lambda rb, cb: (cb // per, rb, cb % per)
    per = r // tr
    return lambda rb, cb: (rb // per, rb % per, cb)


def _mm_nn(a, w, *, out_dtype, tm, tn, tk, name, add=None):
    m, k = a.shape
    assert k == w.rows
    tm = _pick(m, tm, 16)
    tk, tn = w.tiles(tk, tn)
    nk = k // tk
    widx = w.index(tk, tn)
    in_specs = [pl.BlockSpec((tm, tk), lambda n, i, kk: (i, kk)),
                pl.BlockSpec((None, None, tk, tn), lambda n, i, kk: widx(kk, n))]
    args = [a, w.arr]
    if add is not None:
        in_specs.append(pl.BlockSpec((tm, tn), lambda n, i, kk: (i, n)))
        args.append(add)
    return pl.pallas_call(
        _mm_body(((1,), (0,)), nk, add is not None),
        out_shape=jax.ShapeDtypeStruct((m, w.cols), out_dtype),
        grid=(w.cols // tn, m // tm, nk),
        in_specs=in_specs,
        out_specs=pl.BlockSpec((tm, tn), lambda n, i, kk: (i, n)),
        scratch_shapes=[pltpu.VMEM((tm, tn), F32)] if nk > 1 else [],
        compiler_params=_cparams("parallel", "parallel", "arbitrary"),
        name=name,
    )(*args)


def _mm_nt(a, w, *, out_dtype, tm, tn, tko, name, deps=()):
    m, n = a.shape
    assert n == w.cols
    tm = _pick(m, tm, 16)
    tko, tn = w.tiles(tko, tn)
    nk = n // tn
    widx = w.index(tko, tn)
    return pl.pallas_call(
        _mm_body(((1,), (1,)), nk, False, unused_in=len(deps)),
        out_shape=jax.ShapeDtypeStruct((m, w.rows), out_dtype),
        grid=(w.rows // tko, m // tm, nk),
        in_specs=[pl.BlockSpec((tm, tn), lambda ko, i, nn: (i, nn)),
                  pl.BlockSpec((None, None, tko, tn), lambda ko, i, nn: widx(ko, nn))]
        + [pl.BlockSpec(memory_space=pl.ANY)] * len(deps),
        out_specs=pl.BlockSpec((tm, tko), lambda ko, i, nn: (i, ko)),
        scratch_shapes=[pltpu.VMEM((tm, tko), F32)] if nk > 1 else [],
        compiler_params=_cparams("parallel", "parallel", "arbitrary"),
        name=name,
    )(a, w.arr, *deps)


def _mm_tn(a, c, w, *, into, tm, tn, tko, name):
    m, k = a.shape
    mc, n = c.shape
    assert mc == m and k == w.rows and n == w.cols
    tm = _pick(m, tm, 16)
    tko, tn = w.tiles(tko, tn)
    nk = m // tm
    oidx = _shard_index(w.kind, w.r, w.c, tko, tn)
    layer = w.layer
    in_specs = [pl.BlockSpec((tm, tko), lambda ko, nn, mm: (mm, ko)),
                pl.BlockSpec((tm, tn), lambda ko, nn, mm: (mm, nn))]
    args = [a, c]
    if into is not None:
        in_specs.append(pl.BlockSpec(memory_space=pl.ANY))
        args.append(into)
    return pl.pallas_call(
        _mm_body(((0,), (0,)), nk, False, unused_in=len(args) - 2),
        out_shape=jax.ShapeDtypeStruct((w.arr.shape[1], w.s, w.r, w.c), F32),
        grid=(k // tko, n // tn, nk),
        in_specs=in_specs,
        out_specs=pl.BlockSpec((None, None, tko, tn), lambda ko, nn, mm: (layer,) + oidx(ko, nn)),
        scratch_shapes=[pltpu.VMEM((tko, tn), F32)] if nk > 1 else [],
        input_output_aliases={2: 0} if into is not None else {},
        compiler_params=_cparams("parallel", "parallel", "arbitrary"),
        name=name,
    )(*args)


def _rms(x, g):
    r = lax.rsqrt(jnp.mean(x * x, axis=-1, keepdims=True) + RMS_EPS)
    return x * r * g, r


def _rms_bwd(x, r, g, dy):
    dg = jnp.sum(dy * x * r, axis=0, keepdims=True)
    t = dy * g
    dx = r * t - x * (r * r * r) * jnp.mean(t * x, axis=-1, keepdims=True)
    return dx, dg


def _attn_consts(n_q, n_kv, sinks):
    group = n_q // n_kv
    t = jnp.arange(WINDOW, dtype=jnp.int32)[:, None]
    s = jnp.arange(2 * WINDOW, dtype=jnp.int32)[None, :] - WINDOW
    dist = (t - s).astype(F32)
    valid = (dist >= 0) & (dist < WINDOW)
    slopes = jnp.exp2(-8.0 * jnp.arange(1, n_q + 1, dtype=F32) / n_q)
    bias = jnp.where(valid[None], -slopes[:, None, None] * dist[None], NEG_BIG)
    bias = bias.reshape(n_kv, group * WINDOW, 2 * WINDOW)
    sink = jnp.broadcast_to(sinks.astype(F32).reshape(n_kv, group, 1, 1), (n_kv, group, WINDOW, 1))
    return bias, sink.reshape(n_kv, group * WINDOW, 1)


def _attn_probs(q_ref, kc_ref, kp_ref, vc_ref, vp_ref, qg, kg, sink, bias, first_mask, kv, group):
    sl = slice(kv * HEAD_DIM, (kv + 1) * HEAD_DIM)
    k2 = jnp.concatenate([kp_ref[:, sl], kc_ref[:, sl]], axis=0)
    v2 = jnp.concatenate([vp_ref[:, sl], vc_ref[:, sl]], axis=0)
    k2n, rk = _rms(k2, kg)
    qx, qn, rq = [], [], []
    for g in range(group):
        h = kv * group + g
        x = q_ref[:, h * HEAD_DIM:(h + 1) * HEAD_DIM]
        y, r = _rms(x, qg)
        qx.append(x); qn.append(y); rq.append(r)
    qs = jnp.concatenate(qn, axis=0).astype(BF16)
    k2b = k2n.astype(BF16)
    s = lax.dot_general(qs, k2b, (((1,), (1,)), ((), ())), preferred_element_type=F32) * (HEAD_DIM ** -0.5)
    s = jnp.where(first_mask, NEG_BIG, s + bias)
    m = jnp.maximum(jnp.max(s, axis=-1, keepdims=True), sink)
    p = jnp.exp(s - m)
    esink = jnp.exp(sink - m)
    denom = jnp.sum(p, axis=-1, keepdims=True) + esink
    pn = p / denom
    return dict(k2=k2, rk=rk, k2b=k2b, v2b=v2.astype(BF16), qx=qx, rq=rq, qs=qs, pn=pn, psink=esink / denom)


def _attn_specs(n_q, n_kv):
    aw, kvw = n_q * HEAD_DIM, n_kv * HEAD_DIM
    group = n_q // n_kv
    kblk, vblk = aw // kvw, aw // kvw + 1

    def specs(nb):
        cur = lambda n: jnp.minimum(n, nb - 1)
        prev = lambda n: jnp.maximum(jnp.minimum(n, nb - 1) - 1, 0)
        return [
            pl.BlockSpec((WINDOW, aw), lambda n: (cur(n), 0)),
            pl.BlockSpec((WINDOW, kvw), lambda n: (cur(n), kblk)),
            pl.BlockSpec((WINDOW, kvw), lambda n: (prev(n), kblk)),
            pl.BlockSpec((WINDOW, kvw), lambda n: (cur(n), vblk)),
            pl.BlockSpec((WINDOW, kvw), lambda n: (prev(n), vblk)),
        ]
    const_specs = [
        pl.BlockSpec((1, HEAD_DIM), lambda n: (0, 0)),
        pl.BlockSpec((1, HEAD_DIM), lambda n: (0, 0)),
        pl.BlockSpec((n_kv, group * WINDOW, 1), lambda n: (0, 0, 0)),
        pl.BlockSpec((n_kv, group * WINDOW, 2 * WINDOW), lambda n: (0, 0, 0)),
    ]
    return specs, const_specs


def _attn_fwd(z, qg, kg, sinks, *, n_q, n_kv, name):
    L = z.shape[0]
    nb = L // WINDOW
    aw = n_q * HEAD_DIM
    group = n_q // n_kv
    bias, sink = _attn_consts(n_q, n_kv, sinks)
    specs, const_specs = _attn_specs(n_q, n_kv)

    def body(q_ref, kc_ref, kp_ref, vc_ref, vp_ref, qg_ref, kg_ref, sink_ref, bias_ref, o_ref):
        n = pl.program_id(0)
        col = lax.broadcasted_iota(jnp.int32, (group * WINDOW, 2 * WINDOW), 1)
        first_mask = jnp.logical_and(n == 0, col < WINDOW)
        for kv in range(n_kv):
            a = _attn_probs(q_ref, kc_ref, kp_ref, vc_ref, vp_ref, qg_ref[...], kg_ref[...],
                            sink_ref[kv], bias_ref[kv], first_mask, kv, group)
            o = jnp.dot(a["pn"].astype(BF16), a["v2b"], preferred_element_type=F32)
            for g in range(group):
                h = kv * group + g
                o_ref[:, h * HEAD_DIM:(h + 1) * HEAD_DIM] = o[g * WINDOW:(g + 1) * WINDOW].astype(o_ref.dtype)

    return pl.pallas_call(
        body,
        out_shape=jax.ShapeDtypeStruct((L, aw), BF16),
        grid=(nb,),
        in_specs=specs(nb) + const_specs,
        out_specs=pl.BlockSpec((WINDOW, aw), lambda n: (n, 0)),
        compiler_params=_cparams("parallel"),
        name=name,
    )(z, z, z, z, z, qg, kg, sink, bias)


def _attn_bwd(z, do, qg, kg, sinks, *, n_q, n_kv, name):
    L = z.shape[0]
    nb = L // WINDOW
    aw, kvw = n_q * HEAD_DIM, n_kv * HEAD_DIM
    group = n_q // n_kv
    bias, sink = _attn_consts(n_q, n_kv, sinks)
    specs, const_specs = _attn_specs(n_q, n_kv)
    scale = HEAD_DIM ** -0.5

    def body(q_ref, kc_ref, kp_ref, vc_ref, vp_ref, do_ref, qg_ref, kg_ref, sink_ref, bias_ref,
             dq_ref, dkv_ref, dqg_ref, dkg_ref, dsink_ref, carry_ref):
        n = pl.program_id(0)

        @pl.when(n == 0)
        def _():
            dqg_ref[...] = jnp.zeros_like(dqg_ref)
            dkg_ref[...] = jnp.zeros_like(dkg_ref)
            dsink_ref[...] = jnp.zeros_like(dsink_ref)
            carry_ref[...] = jnp.zeros_like(carry_ref)

        @pl.when(n < nb)
        def _():
            col = lax.broadcasted_iota(jnp.int32, (group * WINDOW, 2 * WINDOW), 1)
            first_mask = jnp.logical_and(n == 0, col < WINDOW)
            head_lane = lax.broadcasted_iota(jnp.int32, (1, n_q), 1)
            qg, kg = qg_ref[...], kg_ref[...]
            dqg = jnp.zeros((1, HEAD_DIM), F32)
            dkg = jnp.zeros((1, HEAD_DIM), F32)
            dsink = jnp.zeros((1, n_q), F32)
            for kv in range(n_kv):
                a = _attn_probs(q_ref, kc_ref, kp_ref, vc_ref, vp_ref, qg, kg,
                                sink_ref[kv], bias_ref[kv], first_mask, kv, group)
                pn = a["pn"]
                dos = jnp.concatenate(
                    [do_ref[:, (kv * group + g) * HEAD_DIM:(kv * group + g + 1) * HEAD_DIM] for g in range(group)],
                    axis=0).astype(BF16)
                dpn = lax.dot_general(dos, a["v2b"], (((1,), (1,)), ((), ())), preferred_element_type=F32)
                dv2 = lax.dot_general(pn.astype(BF16), dos, (((0,), (0,)), ((), ())), preferred_element_type=F32)
                delta = jnp.sum(pn * dpn, axis=-1, keepdims=True)
                ds = (pn * (dpn - delta)).astype(BF16)
                dsk = -a["psink"] * delta
                dqn = lax.dot_general(ds, a["k2b"], (((1,), (0,)), ((), ())), preferred_element_type=F32) * scale
                dk2n = lax.dot_general(ds, a["qs"], (((0,), (0,)), ((), ())), preferred_element_type=F32) * scale
                for g in range(group):
                    h = kv * group + g
                    rows = slice(g * WINDOW, (g + 1) * WINDOW)
                    dx, dgq = _rms_bwd(a["qx"][g], a["rq"][g], qg, dqn[rows])
                    dq_ref[:, h * HEAD_DIM:(h + 1) * HEAD_DIM] = dx.astype(dq_ref.dtype)
                    dqg = dqg + dgq
                    dsink = dsink + jnp.where(head_lane == h, jnp.sum(dsk[rows], axis=0, keepdims=True), 0.0)
                dk2, dgk = _rms_bwd(a["k2"], a["rk"], kg, dk2n)
                dkg = dkg + dgk
                ksl = slice(kv * HEAD_DIM, (kv + 1) * HEAD_DIM)
                vsl = slice(kvw + kv * HEAD_DIM, kvw + (kv + 1) * HEAD_DIM)
                dkv_ref[:, ksl] = (carry_ref[:, ksl] + dk2[:WINDOW]).astype(dkv_ref.dtype)
                dkv_ref[:, vsl] = (carry_ref[:, vsl] + dv2[:WINDOW]).astype(dkv_ref.dtype)
                carry_ref[:, ksl] = dk2[WINDOW:]
                carry_ref[:, vsl] = dv2[WINDOW:]
            dqg_ref[...] += dqg
            dkg_ref[...] += dkg
            dsink_ref[...] += dsink

        @pl.when(n == nb)
        def _():
            dkv_ref[...] = carry_ref[...].astype(dkv_ref.dtype)

    in_specs = specs(nb) + [pl.BlockSpec((WINDOW, aw), lambda n: (jnp.minimum(n, nb - 1), 0))] + const_specs
    return pl.pallas_call(
        body,
        out_shape=(jax.ShapeDtypeStruct((L, aw), BF16), jax.ShapeDtypeStruct((L, 2 * kvw), BF16),
                   jax.ShapeDtypeStruct((1, HEAD_DIM), F32), jax.ShapeDtypeStruct((1, HEAD_DIM), F32),
                   jax.ShapeDtypeStruct((1, n_q), F32)),
        grid=(nb + 1,),
        in_specs=in_specs,
        out_specs=(pl.BlockSpec((WINDOW, aw), lambda n: (jnp.minimum(n, nb - 1), 0)),
                   pl.BlockSpec((WINDOW, 2 * kvw), lambda n: (jnp.maximum(n - 1, 0), 0)),
                   pl.BlockSpec((1, HEAD_DIM), lambda n: (0, 0)),
                   pl.BlockSpec((1, HEAD_DIM), lambda n: (0, 0)),
                   pl.BlockSpec((1, n_q), lambda n: (0, 0))),
        scratch_shapes=[pltpu.VMEM((WINDOW, 2 * kvw), F32)],
        compiler_params=_cparams("arbitrary"),
        name=name,
    )(z, z, z, z, z, do, qg, kg, sink, bias)


def _cmul(ar, ai, br, bi):
    return ar * br - ai * bi, ar * bi + ai * br


def _scan_consts(ar, ai, n, reverse):
    row = lax.broadcasted_iota(jnp.int32, (8, n), 0)
    mults = []
    pr, pi = ar, ai
    for k in (1, 2, 4):
        keep = (row < 8 - k) if reverse else (row >= k)
        mults.append(((8 - k) if reverse else k, jnp.where(keep, pr, 0.0), jnp.where(keep, pi, 0.0)))
        pr, pi = _cmul(pr, pi, pr, pi)
    pwr = jnp.zeros((8, n), F32)
    pwi = jnp.zeros((8, n), F32)
    pr, pi = ar, ai
    for e in range(1, 9):
        sel = (row == 8 - e) if reverse else (row == e - 1)
        pwr = jnp.where(sel, pr, pwr)
        pwi = jnp.where(sel, pi, pwi)
        pr, pi = _cmul(pr, pi, ar, ai)
    return mults, pwr, pwi


def _scan8(xr, xi, mults, pwr, pwi, cr, ci):
    for shift, mr, mi in mults:
        sr = pltpu.roll(xr, shift, 0)
        si = pltpu.roll(xi, shift, 0)
        xr, xi = xr + mr * sr - mi * si, xi + mr * si + mi * sr
    return xr + pwr * cr - pwi * ci, xi + pwr * ci + pwi * cr


def _blockdiag(x):
    g, a, b = x.shape
    j = g // SSM_LANE_GROUPS
    eye = jnp.eye(SSM_LANE_GROUPS, dtype=x.dtype)
    y = x.reshape(j, SSM_LANE_GROUPS, a, 1, b) * eye[None, :, None, :, None]
    return y.reshape(j, SSM_LANE_GROUPS * a, SSM_LANE_GROUPS * b)


def _blockdiag_extract(y, a, b):
    j = y.shape[0]
    y = y.reshape(j, SSM_LANE_GROUPS, a, SSM_LANE_GROUPS, b)
    return jnp.einsum("jgahb,gh->jgab", y, jnp.eye(SSM_LANE_GROUPS, dtype=y.dtype)).reshape(j * SSM_LANE_GROUPS, a, b)


def _ssm_disc(lr, li, ldt, brt, bit):
    dt = jnp.exp(ldt)
    mag = jnp.exp(lr * dt)
    ar = mag * jnp.cos(li * dt)
    ai = mag * jnp.sin(li * dt)
    den = lr * lr + li * li
    fr = ((ar - 1.0) * lr + ai * li) / den
    fi = (ai * lr - (ar - 1.0) * li) / den
    bbr = fr[:, None, :] * brt - fi[:, None, :] * bit
    bbi = fr[:, None, :] * bit + fi[:, None, :] * brt
    return ar, ai, bbr, bbi


def _ssm_prep(lr, li, ldt, brt, bit, *, name):
    g, h, p = brt.shape

    def body(lr_ref, li_ref, ldt_ref, brt_ref, bit_ref, ar_ref, ai_ref, bbr_ref, bbi_ref):
        ar, ai, bbr, bbi = _ssm_disc(lr_ref[...], li_ref[...], ldt_ref[...], brt_ref[...], bit_ref[...])
        ar_ref[...] = ar
        ai_ref[...] = ai
        bbr_ref[...] = bbr
        bbi_ref[...] = bbi

    gp = jax.ShapeDtypeStruct((g, p), F32)
    ghp = jax.ShapeDtypeStruct((g, h, p), F32)
    return pl.pallas_call(body, out_shape=(gp, gp, ghp, ghp), name=name)(lr, li, ldt, brt, bit)


def _ssm_prep_bwd(lr, li, ldt, brt, bit, dar, dai, dbbr, dbbi, *, name):
    g, h, p = brt.shape

    def body(lr_ref, li_ref, ldt_ref, brt_ref, bit_ref, dar_ref, dai_ref, dbbr_ref, dbbi_ref,
             dlr_ref, dli_ref, dldt_ref, dbrt_ref, dbit_ref):
        _, vjp = jax.vjp(_ssm_disc, lr_ref[...], li_ref[...], ldt_ref[...], brt_ref[...], bit_ref[...])
        dlr, dli, dldt, dbrt, dbit = vjp((dar_ref[...], dai_ref[...], dbbr_ref[...], dbbi_ref[...]))
        dlr_ref[...] = dlr
        dli_ref[...] = dli
        dldt_ref[...] = dldt
        dbrt_ref[...] = dbrt
        dbit_ref[...] = dbit

    gp = jax.ShapeDtypeStruct((g, p), F32)
    ghp = jax.ShapeDtypeStruct((g, h, p), F32)
    return pl.pallas_call(body, out_shape=(gp, gp, jax.ShapeDtypeStruct((g, 1), F32), ghp, ghp), name=name)(
        lr, li, ldt, brt, bit, dar, dai, dbbr, dbbi)


def _ssm_specs(tc, nlanes, nch, u_colblk, chunk_of):
    return [
        pl.BlockSpec((tc, nch), lambda j, c: (chunk_of(c), u_colblk + j)),
        pl.BlockSpec((1, nlanes), lambda j, c: (0, j)),
        pl.BlockSpec((1, nlanes), lambda j, c: (0, j)),
        pl.BlockSpec((None, nch, nlanes), lambda j, c: (j, 0, 0)),
        pl.BlockSpec((None, nch, nlanes), lambda j, c: (j, 0, 0)),
        pl.BlockSpec((None, nlanes, nch), lambda j, c: (j, 0, 0)),
        pl.BlockSpec((None, nlanes, nch), lambda j, c: (j, 0, 0)),
        pl.BlockSpec((1, nch), lambda j, c: (0, j)),
    ]


def _ssm_fwd(z, ar, ai, bblk_r, bblk_i, cblk_r, cblk_i, d, *, u_col, tc, name):
    L = z.shape[0]
    nj, nch, nlanes = bblk_r.shape
    w = nj * nch
    nc = L // tc
    ng = tc // 8

    def body(u_ref, ar_ref, ai_ref, br_ref, bi_ref, cr_ref, ci_ref, d_ref, y_ref, s0r_ref, s0i_ref,
             xr_ref, xi_ref, carr_ref, cari_ref):
        c = pl.program_id(1)

        @pl.when(c == 0)
        def _():
            carr_ref[...] = jnp.zeros_like(carr_ref)
            cari_ref[...] = jnp.zeros_like(cari_ref)

        s0r_ref[...] = carr_ref[...]
        s0i_ref[...] = cari_ref[...]
        u = u_ref[...]
        ub = u.astype(BF16)
        xr_ref[...] = jnp.dot(ub, br_ref[...].astype(BF16), preferred_element_type=F32)
        xi_ref[...] = jnp.dot(ub, bi_ref[...].astype(BF16), preferred_element_type=F32)
        mults, pwr, pwi = _scan_consts(ar_ref[...], ai_ref[...], nlanes, False)

        def step(g, carry):
            rows = pl.ds(pl.multiple_of(g * 8, 8), 8)
            sr, si = _scan8(xr_ref[rows, :], xi_ref[rows, :], mults, pwr, pwi, carry[0], carry[1])
            xr_ref[rows, :] = sr
            xi_ref[rows, :] = si
            return sr[7:8], si[7:8]

        cr, ci = lax.fori_loop(0, ng, step, (carr_ref[...], cari_ref[...]))
        carr_ref[...] = cr
        cari_ref[...] = ci
        y = (jnp.dot(xr_ref[...].astype(BF16), cr_ref[...].astype(BF16), preferred_element_type=F32)
             - jnp.dot(xi_ref[...].astype(BF16), ci_ref[...].astype(BF16), preferred_element_type=F32)
             + d_ref[...] * u)
        y_ref[...] = y

    state = jax.ShapeDtypeStruct((nc, 1, nj * nlanes), F32)
    state_spec = pl.BlockSpec((None, 1, nlanes), lambda j, c: (c, 0, j))
    return pl.pallas_call(
        body,
        out_shape=(jax.ShapeDtypeStruct((L, w), F32), state, state),
        grid=(nj, nc),
        in_specs=_ssm_specs(tc, nlanes, nch, u_col // nch, lambda c: c),
        out_specs=(pl.BlockSpec((tc, nch), lambda j, c: (c, j)), state_spec, state_spec),
        scratch_shapes=[pltpu.VMEM((tc, nlanes), F32), pltpu.VMEM((tc, nlanes), F32),
                        pltpu.VMEM((1, nlanes), F32), pltpu.VMEM((1, nlanes), F32)],
        compiler_params=_cparams("parallel", "arbitrary"),
        name=name,
    )(z, ar, ai, bblk_r, bblk_i, cblk_r, cblk_i, d)


def _ssm_bwd(z, dy, s0r, s0i, ar, ai, bblk_r, bblk_i, cblk_r, cblk_i, d, *, u_col, tc, name):
    L = z.shape[0]
    nj, nch, nlanes = bblk_r.shape
    w = nj * nch
    nc = L // tc
    ng = tc // 8
    chunk_of = lambda c: nc - 1 - c

    def body(u_ref, ar_ref, ai_ref, br_ref, bi_ref, cr_ref, ci_ref, d_ref, dy_ref, s0r_ref, s0i_ref,
             du_ref, dbr_ref, dbi_ref, dcr_ref, dci_ref, dar_ref, dai_ref, dd_ref,
             sr_ref, si_ref, pr_ref, pi_ref, lr_ref, li_ref, carr_ref, cari_ref):
        c = pl.program_id(1)

        @pl.when(c == 0)
        def _():
            for ref in (dbr_ref, dbi_ref, dcr_ref, dci_ref, dar_ref, dai_ref, dd_ref, carr_ref, cari_ref):
                ref[...] = jnp.zeros_like(ref)

        u = u_ref[...]
        dyv = dy_ref[...]
        ub = u.astype(BF16)
        dyb = dyv.astype(BF16)
        brb = br_ref[...].astype(BF16)
        bib = bi_ref[...].astype(BF16)
        crb = cr_ref[...].astype(BF16)
        cib = ci_ref[...].astype(BF16)
        a_r, a_i = ar_ref[...], ai_ref[...]

        sr_ref[...] = jnp.dot(ub, brb, preferred_element_type=F32)
        si_ref[...] = jnp.dot(ub, bib, preferred_element_type=F32)
        mults, pwr, pwi = _scan_consts(a_r, a_i, nlanes, False)
        row = lax.broadcasted_iota(jnp.int32, (8, nlanes), 0)

        def fstep(g, carry):
            rows = pl.ds(pl.multiple_of(g * 8, 8), 8)
            sr, si = _scan8(sr_ref[rows, :], si_ref[rows, :], mults, pwr, pwi, carry[0], carry[1])
            sr_ref[rows, :] = sr
            si_ref[rows, :] = si
            pr_ref[rows, :] = jnp.where(row == 0, carry[0], pltpu.roll(sr, 1, 0))
            pi_ref[rows, :] = jnp.where(row == 0, carry[1], pltpu.roll(si, 1, 0))
            return sr[7:8], si[7:8]

        lax.fori_loop(0, ng, fstep, (s0r_ref[...], s0i_ref[...]))

        nt = (((1,), (1,)), ((), ()))
        lr_ref[...] = lax.dot_general(dyb, crb, nt, preferred_element_type=F32)
        li_ref[...] = -lax.dot_general(dyb, cib, nt, preferred_element_type=F32)
        rmults, rpwr, rpwi = _scan_consts(a_r, -a_i, nlanes, True)

        def rstep(gg, carry):
            cr, ci, acc_r, acc_i = carry
            rows = pl.ds(pl.multiple_of((ng - 1 - gg) * 8, 8), 8)
            lr, li = _scan8(lr_ref[rows, :], li_ref[rows, :], rmults, rpwr, rpwi, cr, ci)
            lr_ref[rows, :] = lr
            li_ref[rows, :] = li
            pr, pi = pr_ref[rows, :], pi_ref[rows, :]
            return lr[0:1], li[0:1], acc_r + lr * pr + li * pi, acc_i + li * pr - lr * pi

        zero8 = jnp.zeros((8, nlanes), F32)
        cr, ci, acc_r, acc_i = lax.fori_loop(0, ng, rstep, (carr_ref[...], cari_ref[...], zero8, zero8))
        carr_ref[...] = cr
        cari_ref[...] = ci
        dar_ref[...] += jnp.sum(acc_r, axis=0, keepdims=True)
        dai_ref[...] += jnp.sum(acc_i, axis=0, keepdims=True)

        tn = (((0,), (0,)), ((), ()))
        lrb = lr_ref[...].astype(BF16)
        lib = li_ref[...].astype(BF16)
        dcr_ref[...] += lax.dot_general(sr_ref[...].astype(BF16), dyb, tn, preferred_element_type=F32)
        dci_ref[...] -= lax.dot_general(si_ref[...].astype(BF16), dyb, tn, preferred_element_type=F32)
        dbr_ref[...] += lax.dot_general(ub, lrb, tn, preferred_element_type=F32)
        dbi_ref[...] += lax.dot_general(ub, lib, tn, preferred_element_type=F32)
        du = (lax.dot_general(lrb, brb, nt, preferred_element_type=F32)
              + lax.dot_general(lib, bib, nt, preferred_element_type=F32)
              + d_ref[...] * dyv)
        du_ref[...] = du.astype(du_ref.dtype)
        dd_ref[...] += jnp.sum(dyv * u, axis=0, keepdims=True)

    state_spec = pl.BlockSpec((None, 1, nlanes), lambda j, c: (chunk_of(c), 0, j))
    bshape = jax.ShapeDtypeStruct((nj, nch, nlanes), F32)
    cshape = jax.ShapeDtypeStruct((nj, nlanes, nch), F32)
    ashape = jax.ShapeDtypeStruct((1, nj * nlanes), F32)
    bspec = pl.BlockSpec((None, nch, nlanes), lambda j, c: (j, 0, 0))
    cspec = pl.BlockSpec((None, nlanes, nch), lambda j, c: (j, 0, 0))
    aspec = pl.BlockSpec((1, nlanes), lambda j, c: (0, j))
    big = pltpu.VMEM((tc, nlanes), F32)
    return pl.pallas_call(
        body,
        out_shape=(jax.ShapeDtypeStruct((L, w), BF16), bshape, bshape, cshape, cshape, ashape, ashape,
                   jax.ShapeDtypeStruct((1, w), F32)),
        grid=(nj, nc),
        in_specs=_ssm_specs(tc, nlanes, nch, u_col // nch, chunk_of)
        + [pl.BlockSpec((tc, nch), lambda j, c: (chunk_of(c), j)), state_spec, state_spec],
        out_specs=(pl.BlockSpec((tc, nch), lambda j, c: (chunk_of(c), j)), bspec, bspec, cspec, cspec, aspec, aspec,
                   pl.BlockSpec((1, nch), lambda j, c: (0, j))),
        scratch_shapes=[big, big, big, big, big, big, pltpu.VMEM((1, nlanes), F32), pltpu.VMEM((1, nlanes), F32)],
        compiler_params=_cparams("parallel", "arbitrary"),
        name=name,
    )(z, ar, ai, bblk_r, bblk_i, cblk_r, cblk_i, d, dy, s0r, s0i)


def _rmsnorm_rows(x, g):
    return x * lax.rsqrt(jnp.mean(x * x, axis=-1, keepdims=True) + RMS_EPS) * g


def _glu_out(y_raw, pre, b):
    yg = jax.nn.gelu(y_raw)
    return yg * jax.nn.sigmoid(pre + b)


def _gate_merge(za, zs, ba, bs, a, bm):
    return jax.nn.sigmoid(za + ba) * a + jax.nn.sigmoid(zs + bs) * bm


def _swiglu(g, u):
    return jax.nn.silu(g) * u


def _row_tile(width_bytes_per_row, rows):
    budget = VMEM_LIMIT_BYTES // 3
    t = max(8, min(1024, budget // (2 * max(width_bytes_per_row, 1))))
    return _pick(rows, t, 16)


def _ssm_params(p, prefix):
    g, pst = p["lam_re"].shape
    ar, ai, bbr, bbi = _ssm_prep(p["lam_re"], p["lam_im"], p["log_dt"], p["b_re_t"], p["b_im_t"], name=prefix + "_ssm_prep")
    return dict(ar=ar.reshape(1, g * pst), ai=ai.reshape(1, g * pst),
                bblk_r=_blockdiag(bbr), bblk_i=_blockdiag(bbi),
                cblk_r=_blockdiag(jnp.swapaxes(p["c_re"], 1, 2)), cblk_i=_blockdiag(jnp.swapaxes(p["c_im"], 1, 2)))


def _layer_fwd(x, p, dims, prefix, deps=()):
    t, d = x.shape
    aw, kvw, sw, ff = dims["aw"], dims["kvw"], dims["sw"], dims["ff"]
    off_u = aw + 2 * kvw
    off_g = off_u + sw
    gblk = _pick(d, 512, 128)
    assert off_g % gblk == 0 and off_u % (SSM_LANE_GROUPS * SSM_GROUP_CH) == 0
    sv = {"x": x}

    h, = _rowmap(_rmsnorm_rows, [(x, "row", d, 0), (p["norm_mix_g"], "vec", d, 0)], [(d, BF16, "row", d)],
                 rows=t, tm=_row_tile(6 * d, t), name=prefix + "_norm_mix", deps=deps)
    z = _mm_nn(h, p["w_in"], out_dtype=F32, tm=512, tn=1664, tk=2048, name=prefix + "_mm_in")
    ya = _attn_fwd(z, p["q_norm_g"], p["k_norm_g"], p["attn_sinks"], n_q=dims["n_q"], n_kv=dims["n_kv"],
                   name=prefix + "_attn_fwd")
    sp = _ssm_params(p, prefix)
    y_raw, s0r, s0i = _ssm_fwd(z, sp["ar"], sp["ai"], sp["bblk_r"], sp["bblk_i"], sp["cblk_r"], sp["cblk_i"], p["ssm_d"],
                               u_col=off_u, tc=dims["tc"], name=prefix + "_ssm_fwd")
    yg, = _rowmap(jax.nn.gelu, [(y_raw, "row", sw, 0)], [(sw, BF16, "row", sw)],
                  rows=t, tm=_row_tile(6 * sw, t), name=prefix + "_gelu")
    pre = _mm_nn(yg, p["ssm_glu_w"], out_dtype=F32, tm=1024, tn=1024, tk=256, name=prefix + "_mm_glu")
    y2, = _rowmap(_glu_out, [(y_raw, "row", sw, 0), (pre, "row", sw, 0), (p["ssm_glu_b"], "vec", sw, 0)],
                  [(sw, BF16, "row", sw)], rows=t, tm=_row_tile(10 * sw, t), name=prefix + "_glu_out")
    a = _mm_nn(ya, p["w_attn_branch"], out_dtype=F32, tm=1024, tn=512, tk=1024, name=prefix + "_mm_ab")
    bm = _mm_nn(y2, p["w_ssm_branch"], out_dtype=F32, tm=1024, tn=512, tk=1024, name=prefix + "_mm_sb")
    ngb = d // gblk
    merged, = _rowmap(
        _gate_merge,
        [(z, "row", gblk, off_g // gblk), (z, "row", gblk, off_g // gblk + ngb),
         (p["gate_bias"], "vec", gblk, 0), (p["gate_bias"], "vec", gblk, ngb),
         (a, "row", gblk, 0), (bm, "row", gblk, 0)],
        [(d, BF16, "row", gblk)], rows=t, tm=_row_tile(18 * gblk, t), ncol=ngb, name=prefix + "_gate")
    x1 = _mm_nn(merged, p["w_out"], out_dtype=F32, tm=512, tn=2048, tk=512, name=prefix + "_mm_out", add=x)
    h2, = _rowmap(_rmsnorm_rows, [(x1, "row", d, 0), (p["norm_ffn_g"], "vec", d, 0)], [(d, BF16, "row", d)],
                  rows=t, tm=_row_tile(6 * d, t), name=prefix + "_norm_ffn")
    gu = _mm_nn(h2, p["w_ffn_in"], out_dtype=F32, tm=512, tn=1408, tk=2048, name=prefix + "_mm_ffn_in")
    fblk = _pick(ff, 1408, 128)
    nfb = ff // fblk
    act, = _rowmap(_swiglu, [(gu, "row", fblk, 0), (gu, "row", fblk, nfb)], [(ff, BF16, "row", fblk)],
                   rows=t, tm=_row_tile(10 * fblk, t), ncol=nfb, name=prefix + "_swiglu")
    x2 = _mm_nn(act, p["w_ffn_out"], out_dtype=F32, tm=512, tn=1024, tk=1408, name=prefix + "_mm_ffn_out", add=x1)
    sv.update(h=h, z=z, ya=ya, sp=sp, y_raw=y_raw, s0r=s0r, s0i=s0i, yg=yg, pre=pre, y2=y2, a=a, bm=bm,
              merged=merged, x1=x1, h2=h2, gu=gu, act=act)
    return x2, sv


def _layer_bwd(dx2, dx2b, sv, p, dims, prefix, gbuf, deps=(), before_mixer=None, before_in=None):
    t, d = dx2.shape
    aw, kvw, sw, ff = dims["aw"], dims["kvw"], dims["sw"], dims["ff"]
    off_u = aw + 2 * kvw
    off_g = off_u + sw
    gblk = _pick(d, 512, 128)
    ngb = d // gblk
    fblk = _pick(ff, 1408, 128)
    nfb = ff // fblk
    g = {}

    dact = _mm_nt(dx2b, p["w_ffn_out"], out_dtype=F32, tm=512, tn=2048, tko=1408, name=prefix + "_mm_dact", deps=deps)
    g["w_ffn_out"] = _mm_tn(sv["act"], dx2b, p["w_ffn_out"], into=gbuf.get("w_ffn_out"), tm=512, tn=1024, tko=1408,
                            name=prefix + "_mm_dw_ffn_out")

    def swiglu_bwd(gg, uu, da):
        _, vjp = jax.vjp(_swiglu, gg, uu)
        return vjp(da)

    dgu_g, dgu_u = _rowmap(swiglu_bwd, [(sv["gu"], "row", fblk, 0), (sv["gu"], "row", fblk, nfb), (dact, "row", fblk, 0)],
                           [(ff, BF16, "row", fblk), (ff, BF16, "row", fblk)],
                           rows=t, tm=_row_tile(16 * fblk, t), ncol=nfb, name=prefix + "_swiglu_bwd")
    dgu = jnp.concatenate([dgu_g, dgu_u], axis=1)
    dh2 = _mm_nt(dgu, p["w_ffn_in"], out_dtype=F32, tm=512, tn=1408, tko=2048, name=prefix + "_mm_dh2")
    g["w_ffn_in"] = _mm_tn(sv["h2"], dgu, p["w_ffn_in"], into=gbuf.get("w_ffn_in"), tm=512, tn=1408, tko=1024,
                           name=prefix + "_mm_dw_ffn_in")

    def norm_bwd(xx, gg, dh, dres):
        _, vjp = jax.vjp(_rmsnorm_rows, xx, gg)
        dxx, dgg = vjp(dh)
        dxx = dxx + dres
        return dxx, dxx, dgg

    dx1, dx1b, g["norm_ffn_g"] = _rowmap(
        norm_bwd, [(sv["x1"], "row", d, 0), (p["norm_ffn_g"], "vec", d, 0), (dh2, "row", d, 0), (dx2, "row", d, 0)],
        [(d, F32, "row", d), (d, BF16, "row", d), (d, F32, "acc", d)],
        rows=t, tm=_row_tile(22 * d, t), name=prefix + "_norm_ffn_bwd")

    deps = before_mixer(dx1) if before_mixer else ()
    dmerged = _mm_nt(dx1b, p["w_out"], out_dtype=F32, tm=1024, tn=2048, tko=512, name=prefix + "_mm_dmerged", deps=deps)
    g["w_out"] = _mm_tn(sv["merged"], dx1b, p["w_out"], into=gbuf.get("w_out"), tm=1024, tn=2048, tko=512,
                        name=prefix + "_mm_dw_out")

    def gate_bwd(za, zs, ba, bs, aa, bb, dm):
        _, vjp = jax.vjp(_gate_merge, za, zs, ba, bs, aa, bb)
        dza, dzs, dba, dbs, daa, dbb = vjp(dm)
        return daa, dbb, dza, dzs, dba, dbs

    z = sv["z"]
    da, dbm, dza, dzs, dba, dbs = _rowmap(
        gate_bwd,
        [(z, "row", gblk, off_g // gblk), (z, "row", gblk, off_g // gblk + ngb),
         (p["gate_bias"], "vec", gblk, 0), (p["gate_bias"], "vec", gblk, ngb),
         (sv["a"], "row", gblk, 0), (sv["bm"], "row", gblk, 0), (dmerged, "row", gblk, 0)],
        [(d, BF16, "row", gblk), (d, BF16, "row", gblk), (d, BF16, "row", gblk), (d, BF16, "row", gblk),
         (d, F32, "acc", gblk), (d, F32, "acc", gblk)],
        rows=t, tm=_row_tile(32 * gblk, t), ncol=ngb, name=prefix + "_gate_bwd")
    g["gate_bias"] = jnp.concatenate([dba, dbs], axis=1)
    dya = _mm_nt(da, p["w_attn_branch"], out_dtype=F32, tm=1024, tn=512, tko=1024, name=prefix + "_mm_dya")
    g["w_attn_branch"] = _mm_tn(sv["ya"], da, p["w_attn_branch"], into=gbuf.get("w_attn_branch"), tm=1024, tn=512,
                                tko=1024, name=prefix + "_mm_dw_ab")
    dy2 = _mm_nt(dbm, p["w_ssm_branch"], out_dtype=F32, tm=1024, tn=512, tko=1024, name=prefix + "_mm_dy2")
    g["w_ssm_branch"] = _mm_tn(sv["y2"], dbm, p["w_ssm_branch"], into=gbuf.get("w_ssm_branch"), tm=1024, tn=512,
                               tko=1024, name=prefix + "_mm_dw_sb")

    def glu_bwd(y_raw, pre, b, dy):
        yg = jax.nn.gelu(y_raw)
        _, vjp = jax.vjp(lambda a_, b_, c_: a_ * jax.nn.sigmoid(b_ + c_), yg, pre, b)
        dyg, dpre, db = vjp(dy)
        return dyg, dpre, db

    dyg_direct, dpre, g["ssm_glu_b"] = _rowmap(
        glu_bwd, [(sv["y_raw"], "row", sw, 0), (sv["pre"], "row", sw, 0), (p["ssm_glu_b"], "vec", sw, 0), (dy2, "row", sw, 0)],
        [(sw, F32, "row", sw), (sw, BF16, "row", sw), (sw, F32, "acc", sw)],
        rows=t, tm=_row_tile(24 * sw, t), name=prefix + "_glu_bwd")
    dyg2 = _mm_nt(dpre, p["ssm_glu_w"], out_dtype=F32, tm=1024, tn=1024, tko=256, name=prefix + "_mm_dyg")
    g["ssm_glu_w"] = _mm_tn(sv["yg"], dpre, p["ssm_glu_w"], into=gbuf.get("ssm_glu_w"), tm=1024, tn=1024, tko=256,
                            name=prefix + "_mm_dw_glu")

    def gelu_bwd(y_raw, d1, d2):
        _, vjp = jax.vjp(jax.nn.gelu, y_raw)
        return vjp(d1 + d2)[0]

    dy_raw, = _rowmap(gelu_bwd, [(sv["y_raw"], "row", sw, 0), (dyg_direct, "row", sw, 0), (dyg2, "row", sw, 0)],
                      [(sw, F32, "row", sw)], rows=t, tm=_row_tile(20 * sw, t), name=prefix + "_gelu_bwd")
    sp = sv["sp"]
    du, dbr, dbi, dcr, dci, dar, dai, g["ssm_d"] = _ssm_bwd(
        z, dy_raw, sv["s0r"], sv["s0i"], sp["ar"], sp["ai"], sp["bblk_r"], sp["bblk_i"], sp["cblk_r"], sp["cblk_i"],
        p["ssm_d"], u_col=off_u, tc=dims["tc"], name=prefix + "_ssm_bwd")
    ngr, pst = p["lam_re"].shape
    hch = SSM_GROUP_CH
    dlr, dli, dldt, dbrt, dbit = _ssm_prep_bwd(
        p["lam_re"], p["lam_im"], p["log_dt"], p["b_re_t"], p["b_im_t"],
        dar.reshape(ngr, pst), dai.reshape(ngr, pst), _blockdiag_extract(dbr, hch, pst), _blockdiag_extract(dbi, hch, pst),
        name=prefix + "_ssm_prep_bwd")
    g.update(ssm_lambda_re=dlr, ssm_lambda_im=dli, ssm_log_dt=dldt.reshape(ngr),
             ssm_b_re=jnp.swapaxes(dbrt, 1, 2), ssm_b_im=jnp.swapaxes(dbit, 1, 2),
             ssm_c_re=jnp.swapaxes(_blockdiag_extract(dcr, pst, hch), 1, 2),
             ssm_c_im=jnp.swapaxes(_blockdiag_extract(dci, pst, hch), 1, 2))

    dq, dkv, g["q_norm_g"], g["k_norm_g"], g["attn_sinks"] = _attn_bwd(
        z, dya, p["q_norm_g"], p["k_norm_g"], p["attn_sinks"], n_q=dims["n_q"], n_kv=dims["n_kv"], name=prefix + "_attn_bwd")

    dz = jnp.concatenate([dq, dkv, du, dza, dzs], axis=1)
    deps = before_in(dq) if before_in else ()
    dh = _mm_nt(dz, p["w_in"], out_dtype=F32, tm=512, tn=1664, tko=2048, name=prefix + "_mm_dh", deps=deps)
    g["w_in"] = _mm_tn(sv["h"], dz, p["w_in"], into=gbuf.get("w_in"), tm=512, tn=1664, tko=1024,
                       name=prefix + "_mm_dw_in")
    dx, dxb, g["norm_mix_g"] = _rowmap(
        norm_bwd, [(sv["x"], "row", d, 0), (p["norm_mix_g"], "vec", d, 0), (dh, "row", d, 0), (dx1, "row", d, 0)],
        [(d, F32, "row", d), (d, BF16, "row", d), (d, F32, "acc", d)],
        rows=t, tm=_row_tile(22 * d, t), name=prefix + "_norm_mix_bwd")
    return dx, dxb, g


def _loss_and_grad(y, target):
    t, d = y.shape

    def fn(yy, tt):
        e = yy - tt
        dy = e * (1.0 / d)
        return dy, dy, jnp.sum(e * e, keepdims=True).reshape(1, 1)

    dy, dyb, sq = _rowmap(fn, [(y, "row", d, 0), (target, "row", d, 0)],
                          [(d, F32, "row", d), (d, BF16, "row", d), (1, F32, "acc", 1)],
                          rows=t, tm=_row_tile(14 * d, t), name="loss")
    return sq, dy, dyb


def _local_step(x, target, n_layers, weights_of, dims, reducer_of=None):
    saved, params = [], []
    h = x
    for l in range(n_layers):
        p, deps = weights_of(l, h)
        params.append(p)
        h, sv = _layer_fwd(h, p, dims, "l%d" % l, deps)
        saved.append(sv)
    sq, dy, dyb = _loss_and_grad(h, target)
    small = [None] * n_layers
    big = [None] * n_layers
    red = None
    for l in reversed(range(n_layers)):
        if red is None:
            dy, dyb, g = _layer_bwd(dy, dyb, saved[l], params[l], dims, "l%d" % l, {})
        else:
            dy, dyb, g = _layer_bwd(dy, dyb, saved[l], params[l], dims, "l%d" % l, {}, deps=(red.start(),),
                                    before_mixer=red.halves_to_chips, before_in=red.chips_to_sibling)
            big[l + 1] = red.finish(dy)
        big[l] = {n: g.pop(n) for n in BIG_WEIGHTS}
        small[l] = g
        red = reducer_of(l, big[l]) if reducer_of else None
    if red is not None:
        red.start()
        red.halves_to_chips(dy)
        red.chips_to_sibling(dy)
        big[0] = red.finish(dy)
    return sq, dy, small, big


COL_SHARDED = ("w_in", "w_attn_branch", "w_ssm_branch", "w_ffn_in")
ROW_SHARDED = ("ssm_glu_w", "w_out", "w_ffn_out")
BIG_WEIGHTS = COL_SHARDED + ROW_SHARDED
WEIGHT_NAMES = ("norm_mix_g", "w_in", "gate_bias", "q_norm_g", "k_norm_g", "attn_sinks", "ssm_lambda_re",
                "ssm_lambda_im", "ssm_log_dt", "ssm_b_re", "ssm_b_im", "ssm_c_re", "ssm_c_im", "ssm_d", "ssm_glu_w",
                "ssm_glu_b", "w_attn_branch", "w_ssm_branch", "w_out", "norm_ffn_g", "w_ffn_in", "w_ffn_out")
SMALL_WEIGHTS = tuple(n for n in WEIGHT_NAMES if n not in BIG_WEIGHTS)


def _dims(d, big, tc):
    s, _, aw, _ = big["w_attn_branch"].shape
    sw = big["w_ssm_branch"].shape[2]
    in_w = big["w_in"].shape[3] * s
    kvw = (in_w - aw - sw - 2 * d) // 2
    ff = big["w_ffn_out"].shape[2] * s
    return dict(aw=aw, kvw=kvw, sw=sw, ff=ff, n_q=aw // HEAD_DIM, n_kv=kvw // HEAD_DIM, tc=tc)


def _layer_params(l, small, big):
    p = {n: _Weight(big[n], 0, "col") for n in COL_SHARDED}
    p.update({n: _Weight(big[n], 0, "row") for n in ROW_SHARDED})
    for n in ("norm_mix_g", "gate_bias", "q_norm_g", "k_norm_g", "ssm_d", "ssm_glu_b", "norm_ffn_g"):
        p[n] = small[n][l][None]
    p["attn_sinks"] = small["attn_sinks"][l]
    p["lam_re"] = small["ssm_lambda_re"][l]
    p["lam_im"] = small["ssm_lambda_im"][l]
    p["log_dt"] = small["ssm_log_dt"][l][:, None]
    p["b_re_t"] = jnp.swapaxes(small["ssm_b_re"][l], 1, 2)
    p["b_im_t"] = jnp.swapaxes(small["ssm_b_im"][l], 1, 2)
    p["c_re"] = small["ssm_c_re"][l]
    p["c_im"] = small["ssm_c_im"][l]
    return p


_ANY = pl.BlockSpec(memory_space=pl.ANY)
_MESH_ID = pl.DeviceIdType.MESH


def _coords():
    return lax.axis_index("x"), lax.axis_index("y"), lax.axis_index("c")


def _remote(src, dst, send_sems, recv_sems, k, to):
    return pltpu.make_async_remote_copy(src_ref=src, dst_ref=dst, send_sem=send_sems.at[k], recv_sem=recv_sems.at[k],
                                        device_id=to, device_id_type=_MESH_ID)


def _comm_call(body, ins, out_shapes, n_remote, name, aliases=None, scratch=()):
    return pl.pallas_call(
        body,
        out_shape=tuple(out_shapes),
        in_specs=[_ANY] * len(ins),
        out_specs=tuple([_ANY] * len(out_shapes)),
        scratch_shapes=[pltpu.SemaphoreType.DMA((n_remote,)), pltpu.SemaphoreType.DMA((n_remote,))] + list(scratch),
        input_output_aliases=aliases or {},
        compiler_params=pltpu.CompilerParams(has_side_effects=True),
        name=name,
    )(*ins)


def _my_chip():
    return (2 * lax.axis_index("x") + lax.axis_index("y")).astype(jnp.int32).reshape(1)


def _my_core():
    return lax.axis_index("c").astype(jnp.int32).reshape(1)


def _cast_into_slot(w, layer, *, name):
    _, r, c = w.shape
    tm = _row_tile(12 * c, r)

    def body(me_ref, w_ref, o_ref):
        o_ref[...] = w_ref[...].astype(o_ref.dtype)

    return pl.pallas_call(
        body,
        out_shape=jax.ShapeDtypeStruct((N_CHIPS, 1, r, c), BF16),
        grid_spec=pltpu.PrefetchScalarGridSpec(
            num_scalar_prefetch=1,
            grid=(r // tm,),
            in_specs=[pl.BlockSpec((None, tm, c), lambda i, me: (layer, i, 0))],
            out_specs=pl.BlockSpec((None, None, tm, c), lambda i, me: (me[0], 0, i, 0)),
        ),
        compiler_params=_cparams("parallel"),
        name=name,
    )(_my_chip(), w)


def _allgather_weights(bufs, *, name):
    n = len(bufs)

    def body(*refs):
        outs = refs[n:2 * n]
        send_sems, recv_sems = refs[2 * n:]
        x, y, c = _coords()
        me = 2 * x + y
        chips = [(1 - x, y), (x, 1 - y), (1 - x, 1 - y)]
        sibling = (x, y, 1 - c)

        def half(i, slot, hc):
            rh = bufs[i].shape[2] // 2
            return outs[i].at[slot, :, pl.ds(hc * rh, rh), :]

        first = [_remote(half(i, me, c), half(i, me, c), send_sems, recv_sems, 6 * i + k, (px, py, c))
                 for i in range(n) for k, (px, py) in enumerate(chips)]
        for cp in first:
            cp.start()
        passed = []
        for k, (px, py) in enumerate(chips):
            for i in range(n):
                landed = half(i, 2 * px + py, c)
                _remote(landed, landed, send_sems, recv_sems, 6 * i + k, (px, py, c)).wait_recv()
                fw = _remote(landed, landed, send_sems, recv_sems, 6 * i + 3 + k, sibling)
                fw.start()
                passed.append(fw)
        for k, (px, py) in enumerate(chips):
            for i in range(n):
                other = half(i, 2 * px + py, 1 - c)
                _remote(other, other, send_sems, recv_sems, 6 * i + 3 + k, sibling).wait_recv()
        for cp in first + passed:
            cp.wait_send()

    outs = [jax.ShapeDtypeStruct(b.shape, b.dtype) for b in bufs]
    return _comm_call(body, bufs, outs, 6 * n, name, aliases={i: i for i in range(n)})


_HBM = pl.BlockSpec(memory_space=pltpu.HBM)
_SEM = pl.BlockSpec(memory_space=pltpu.SEMAPHORE)
_DATAFLOW = pltpu.SideEffectType.DATAFLOW_SIDE_EFFECTING


class _SplitExchange:
    def __init__(self, srcs, lands, build, n_copies, name):
        self.build, self.n, self.name = build, n_copies, name
        self.ns, self.nl = len(srcs), len(lands)
        self.bufs = [pltpu.with_memory_space_constraint(a, pltpu.HBM) for a in list(srcs) + list(lands)]

    def _copies(self, refs, send_sems, recv_sems):
        triples = self.build(refs[:self.ns], refs[self.ns:self.ns + self.nl])
        assert len(triples) == self.n
        return [pltpu.make_async_remote_copy(src_ref=s, dst_ref=d, send_sem=send_sems.at[k], recv_sem=recv_sems.at[k],
                                             device_id=to, device_id_type=_MESH_ID) for k, (s, d, to) in enumerate(triples)]

    def start(self, deps=()):
        nb = self.ns + self.nl

        def body(*refs):
            outs = refs[nb + len(deps):]
            for cp in self._copies(refs, outs[0], outs[1]):
                cp.start()
            outs[-1][...] = jnp.zeros_like(outs[-1])

        sems = pltpu.SemaphoreType.DMA((self.n,))
        res = pl.pallas_call(
            body,
            out_shape=(sems, sems, *[pltpu.HBM(b.shape, b.dtype) for b in self.bufs], jax.ShapeDtypeStruct((8, 128), F32)),
            in_specs=[_HBM] * nb + [_ANY] * len(deps),
            out_specs=(_SEM, _SEM, *[_HBM] * nb, pl.BlockSpec(memory_space=pltpu.VMEM)),
            input_output_aliases={i: 2 + i for i in range(nb)},
            compiler_params=pltpu.CompilerParams(has_side_effects=_DATAFLOW),
            name=self.name + "_start",
        )(*self.bufs, *deps)
        self.send_sems, self.recv_sems = res[0], res[1]
        self.bufs = list(res[2:2 + nb])
        return res[-1]

    def wait(self, after=()):
        nb = self.ns + self.nl

        def body(*refs):
            for cp in self._copies(refs, refs[nb], refs[nb + 1]):
                cp.wait_send()
                cp.wait_recv()

        res = pl.pallas_call(
            body,
            out_shape=tuple(pltpu.HBM(b.shape, b.dtype) for b in self.bufs),
            in_specs=[_HBM] * nb + [_SEM, _SEM] + [_ANY] * len(after),
            out_specs=tuple([_HBM] * nb),
            input_output_aliases={i: i for i in range(nb)},
            compiler_params=pltpu.CompilerParams(has_side_effects=_DATAFLOW),
            name=self.name + "_wait",
        )(*self.bufs, self.send_sems, self.recv_sems, *after)
        res = list(res)
        return res[:self.ns], res[self.ns:]


def _allgather_direct(bufs, *, name):
    n = len(bufs)

    def build(srcs, lands):
        x, y, c = _coords()
        me = 2 * x + y
        return [(srcs[i].at[me], srcs[i].at[me], (px, py, c))
                for i in range(n) for px, py in [(1 - x, y), (x, 1 - y), (1 - x, 1 - y)]]

    return _SplitExchange(bufs, [], build, 3 * n, name)


class _GradReducer:
    def __init__(self, grads, tag):
        self.tag = tag
        self.n = n = len(grads)
        self.flat = [g.reshape(g.shape[1:]) for g in grads]
        self.half_shape = [(g.shape[1], g.shape[2] // 2, g.shape[3]) for g in grads]
        theirs = [lax.empty(s, F32) for s in self.half_shape]

        def build(srcs, lands):
            x, y, c = _coords()
            return [(srcs[i].at[:, pl.ds((1 - c) * self.half_shape[i][1], self.half_shape[i][1]), :], lands[i],
                     (x, y, 1 - c)) for i in range(n)]

        self.ex = _SplitExchange(self.flat, theirs, build, n, tag + "_halves")

    def start(self):
        return self.ex.start()

    def halves_to_chips(self, after):
        n = self.n
        flat, theirs = self.ex.wait((after,))
        parts = [_add_own_half(f, t, name="%s_add_own_half_%d" % (self.tag, i)) for i, (f, t) in enumerate(zip(flat, theirs))]
        got = [lax.empty(p.shape, p.dtype) for p in parts]

        def build(srcs, lands):
            x, y, c = _coords()
            me = 2 * x + y
            return [(srcs[i].at[2 * px + py], lands[i].at[me], (px, py, c))
                    for i in range(n) for px, py in [(1 - x, y), (x, 1 - y), (1 - x, 1 - y)]]

        self.ex = _SplitExchange(parts, got, build, 3 * n, self.tag + "_chips")
        return (self.ex.start(),)

    def chips_to_sibling(self, after):
        n = self.n
        parts, got = self.ex.wait((after,))
        self.mine = [_sum_chips(p, g, name="%s_sum_chips_%d" % (self.tag, i)) for i, (p, g) in enumerate(zip(parts, got))]
        sib = [lax.empty(m.shape, m.dtype) for m in self.mine]

        def build(srcs, lands):
            x, y, c = _coords()
            return [(srcs[i], lands[i], (x, y, 1 - c)) for i in range(n)]

        self.ex = _SplitExchange(self.mine, sib, build, n, self.tag + "_sibling")
        return (self.ex.start(),)

    def finish(self, after):
        mine, sib = self.ex.wait((after,))
        return list(zip(mine, sib))


def _add_own_half(g, theirs, *, name):
    s, r, c = g.shape
    rh = r // 2
    tm = _row_tile(10 * c, rh)
    nb = rh // tm

    def body(core_ref, g_ref, t_ref, o_ref):
        o_ref[...] = (g_ref[...] + t_ref[...]).astype(o_ref.dtype)

    return pl.pallas_call(
        body,
        out_shape=jax.ShapeDtypeStruct((s, rh, c), BF16),
        grid_spec=pltpu.PrefetchScalarGridSpec(
            num_scalar_prefetch=1,
            grid=(s, nb),
            in_specs=[pl.BlockSpec((None, tm, c), lambda k, i, core: (k, core[0] * nb + i, 0)),
                      pl.BlockSpec((None, tm, c), lambda k, i, core: (k, i, 0))],
            out_specs=pl.BlockSpec((None, tm, c), lambda k, i, core: (k, i, 0)),
        ),
        compiler_params=_cparams("parallel", "parallel"),
        name=name,
    )(_my_core(), g, theirs)


def _sum_chips(part, got, *, name):
    s, rh, c = part.shape
    tm = _row_tile(14 * c, rh)

    def body(me_ref, p_ref, a_ref, b_ref, c_ref, o_ref):
        o_ref[...] = ((p_ref[...].astype(F32) + a_ref[...].astype(F32)) + b_ref[...].astype(F32)) + c_ref[...].astype(F32)

    slot = lambda k: (lambda i, me: ((me[0] + k) % s, i, 0))
    return pl.pallas_call(
        body,
        out_shape=jax.ShapeDtypeStruct((rh, c), F32),
        grid_spec=pltpu.PrefetchScalarGridSpec(
            num_scalar_prefetch=1,
            grid=(rh // tm,),
            in_specs=[pl.BlockSpec((None, tm, c), slot(k)) for k in range(s)],
            out_specs=pl.BlockSpec((tm, c), lambda i, me: (i, 0)),
        ),
        compiler_params=_cparams("parallel"),
        name=name,
    )(_my_chip(), part, got, got, got)


def _allgather_all(buf, *, name):
    def body(in_ref, out_ref, send_sems, recv_sems, stage_ref, local_sems):
        x, y, c = _coords()
        flip = lambda v, b: (1 - v) if b else v
        peers = [(flip(x, r & 4), flip(y, r & 2), flip(c, r & 1)) for r in range(1, N_DEV)]
        slot_of = lambda p: 4 * p[0] + 2 * p[1] + p[2]
        mine = out_ref.at[slot_of((x, y, c))]
        sends = [_remote(in_ref, mine, send_sems, recv_sems, k, p) for k, p in enumerate(peers)]
        for cp in sends:
            cp.start()
        load = pltpu.make_async_copy(in_ref, stage_ref, local_sems.at[0])
        load.start()
        load.wait()
        store = pltpu.make_async_copy(stage_ref, mine, local_sems.at[1])
        store.start()
        for k, p in enumerate(peers):
            slot = out_ref.at[slot_of(p)]
            _remote(slot, slot, send_sems, recv_sems, k, p).wait_recv()
        for cp in sends:
            cp.wait_send()
        store.wait()

    return _comm_call(body, [buf], [jax.ShapeDtypeStruct((N_DEV,) + buf.shape, buf.dtype)], N_DEV - 1, name,
                      scratch=[pltpu.VMEM(buf.shape, buf.dtype), pltpu.SemaphoreType.DMA((2,))])[0]


def _sum_slots(arr, *, name):
    s, r, c = arr.shape
    tm = _row_tile(4 * c * (s + 1), r)

    def body(*refs):
        acc = refs[0][...]
        for ref in refs[1:s]:
            acc = acc + ref[...]
        refs[s][...] = acc

    return pl.pallas_call(
        body,
        out_shape=jax.ShapeDtypeStruct((r, c), arr.dtype),
        grid=(r // tm,),
        in_specs=[pl.BlockSpec((None, tm, c), lambda i, k=k: (k, i, 0)) for k in range(s)],
        out_specs=pl.BlockSpec((tm, c), lambda i: (i, 0)),
        compiler_params=_cparams("parallel"),
        name=name,
    )(*([arr] * s))


def _adamw_fn(w, g, m, v):
    m = ADAM_B1 * m + (1.0 - ADAM_B1) * g
    v = ADAM_B2 * v + (1.0 - ADAM_B2) * jnp.square(g)
    m_hat = m / (1.0 - ADAM_B1 ** ADAM_STEP)
    v_hat = v / (1.0 - ADAM_B2 ** ADAM_STEP)
    delta = -ADAM_LR * (m_hat / (jnp.sqrt(v_hat) + ADAM_EPS) + ADAM_WD * w)
    return delta, m, v


def _adamw(w, g, m, v, *, name):
    rows, cols = w.shape
    ins = [(a, "row", cols, 0) for a in (w, g, m, v)]
    outs = [(cols, F32, "row", cols)] * 3
    return _rowmap(_adamw_fn, ins, outs, rows=rows, tm=_row_tile(56 * cols, rows), name=name)


def _adamw_sharded(w, m, v, g_mine, g_sibling, layer, into, *, name):
    nl, r, c = w.shape
    rh = r // 2
    tm = _row_tile(40 * c, rh)
    nb = rh // tm
    n_into = 0 if into is None else 4

    def body(core_ref, w_ref, m_ref, v_ref, a_ref, b_ref, *rest):
        g_ref, d_ref, nm_ref, nv_ref = rest[n_into:]
        g = jnp.where(pl.program_id(0) == core_ref[0], a_ref[...], b_ref[...])
        delta, nm, nv = _adamw_fn(w_ref[...], g, m_ref[...], v_ref[...])
        g_ref[...] = g
        d_ref[...] = delta
        nm_ref[...] = nm
        nv_ref[...] = nv

    whole = pl.BlockSpec((None, tm, c), lambda h, i, core: (layer, h * nb + i, 0))
    half = pl.BlockSpec((tm, c), lambda h, i, core: (i, 0))
    shape = jax.ShapeDtypeStruct((nl, r, c), F32)
    return pl.pallas_call(
        body,
        out_shape=(shape, shape, shape, shape),
        grid_spec=pltpu.PrefetchScalarGridSpec(
            num_scalar_prefetch=1,
            grid=(2, nb),
            in_specs=[whole, whole, whole, half, half] + [pl.BlockSpec(memory_space=pl.ANY)] * n_into,
            out_specs=(whole, whole, whole, whole),
        ),
        input_output_aliases={6 + k: k for k in range(n_into)},
        compiler_params=_cparams("parallel", "parallel"),
        name=name,
    )(_my_core(), w, m, v, g_mine, g_sibling, *(into or ()))


def _pack(arrays):
    flat = jnp.concatenate([a.reshape(-1) for a in arrays])
    pad = (-flat.shape[0]) % (256 * 128)
    return jnp.pad(flat, (0, pad)).reshape(-1, 128)


def _unpack(buf, shapes):
    flat = buf.reshape(-1)
    out, off = [], 0
    for s in shapes:
        n = math.prod(s)
        out.append(flat[off:off + n].reshape(s))
        off += n
    return out


def kernel(x, norm_mix_g, w_in, gate_bias, q_norm_g, k_norm_g, attn_sinks, ssm_lambda_re, ssm_lambda_im, ssm_log_dt, ssm_b_re, ssm_b_im, ssm_c_re, ssm_c_im, ssm_d, ssm_glu_w, ssm_glu_b, w_attn_branch, w_ssm_branch, w_out, norm_ffn_g, w_ffn_in, w_ffn_out, loss_target, m_norm_mix_g, m_w_in, m_gate_bias, m_q_norm_g, m_k_norm_g, m_attn_sinks, m_ssm_lambda_re, m_ssm_lambda_im, m_ssm_log_dt, m_ssm_b_re, m_ssm_b_im, m_ssm_c_re, m_ssm_c_im, m_ssm_d, m_ssm_glu_w, m_ssm_glu_b, m_w_attn_branch, m_w_ssm_branch, m_w_out, m_norm_ffn_g, m_w_ffn_in, m_w_ffn_out, v_norm_mix_g, v_w_in, v_gate_bias, v_q_norm_g, v_k_norm_g, v_attn_sinks, v_ssm_lambda_re, v_ssm_lambda_im, v_ssm_log_dt, v_ssm_b_re, v_ssm_b_im, v_ssm_c_re, v_ssm_c_im, v_ssm_d, v_ssm_glu_w, v_ssm_glu_b, v_w_attn_branch, v_w_ssm_branch, v_w_out, v_norm_ffn_g, v_w_ffn_in, v_w_ffn_out):
    w = dict(norm_mix_g=norm_mix_g, w_in=w_in, gate_bias=gate_bias, q_norm_g=q_norm_g, k_norm_g=k_norm_g,
             attn_sinks=attn_sinks, ssm_lambda_re=ssm_lambda_re, ssm_lambda_im=ssm_lambda_im, ssm_log_dt=ssm_log_dt,
             ssm_b_re=ssm_b_re, ssm_b_im=ssm_b_im, ssm_c_re=ssm_c_re, ssm_c_im=ssm_c_im, ssm_d=ssm_d,
             ssm_glu_w=ssm_glu_w, ssm_glu_b=ssm_glu_b, w_attn_branch=w_attn_branch, w_ssm_branch=w_ssm_branch,
             w_out=w_out, norm_ffn_g=norm_ffn_g, w_ffn_in=w_ffn_in, w_ffn_out=w_ffn_out)
    m = dict(norm_mix_g=m_norm_mix_g, w_in=m_w_in, gate_bias=m_gate_bias, q_norm_g=m_q_norm_g, k_norm_g=m_k_norm_g,
             attn_sinks=m_attn_sinks, ssm_lambda_re=m_ssm_lambda_re, ssm_lambda_im=m_ssm_lambda_im,
             ssm_log_dt=m_ssm_log_dt, ssm_b_re=m_ssm_b_re, ssm_b_im=m_ssm_b_im, ssm_c_re=m_ssm_c_re,
             ssm_c_im=m_ssm_c_im, ssm_d=m_ssm_d, ssm_glu_w=m_ssm_glu_w, ssm_glu_b=m_ssm_glu_b,
             w_attn_branch=m_w_attn_branch, w_ssm_branch=m_w_ssm_branch, w_out=m_w_out, norm_ffn_g=m_norm_ffn_g,
             w_ffn_in=m_w_ffn_in, w_ffn_out=m_w_ffn_out)
    v = dict(norm_mix_g=v_norm_mix_g, w_in=v_w_in, gate_bias=v_gate_bias, q_norm_g=v_q_norm_g, k_norm_g=v_k_norm_g,
             attn_sinks=v_attn_sinks, ssm_lambda_re=v_ssm_lambda_re, ssm_lambda_im=v_ssm_lambda_im,
             ssm_log_dt=v_ssm_log_dt, ssm_b_re=v_ssm_b_re, ssm_b_im=v_ssm_b_im, ssm_c_re=v_ssm_c_re,
             ssm_c_im=v_ssm_c_im, ssm_d=v_ssm_d, ssm_glu_w=v_ssm_glu_w, ssm_glu_b=v_ssm_glu_b,
             w_attn_branch=v_w_attn_branch, w_ssm_branch=v_w_ssm_branch, w_out=v_w_out, norm_ffn_g=v_norm_ffn_g,
             w_ffn_in=v_w_ffn_in, w_ffn_out=v_w_ffn_out)
    n_layers = norm_mix_g.shape[0]
    d_model = x.shape[-1]
    seq = x.shape[1]

    first = _allgather_weights([_cast_into_slot(w[n], 0, name="cast0_" + n) for n in BIG_WEIGHTS],
                               name="allgather_weights0")
    later = {}
    for l in range(1, n_layers):
        ex = _allgather_direct([_cast_into_slot(w[n], l, name="cast%d_%s" % (l, n)) for n in BIG_WEIGHTS],
                               name="allgather_weights%d" % l)
        later[l] = (ex, ex.start((first[0],)))
    dims = _dims(d_model, dict(zip(BIG_WEIGHTS, first)), min(512, seq))

    def weights_of(l, h):
        if l == 0:
            deps = tuple(token for _, token in later.values())
            return _layer_params(0, w, dict(zip(BIG_WEIGHTS, first))), deps
        bufs, _ = later[l][0].wait((h,))
        return _layer_params(l, w, dict(zip(BIG_WEIGHTS, bufs))), ()

    def reducer_of(l, big_grads):
        return _GradReducer([big_grads[n] for n in BIG_WEIGHTS], "rs%d" % l)

    sq, dx, grads, reduced = _local_step(x[0], loss_target[0], n_layers, weights_of, dims, reducer_of)
    loss = lax.psum(sq[0, 0], MESH_AXES) * (0.5 / d_model)

    grad, delta, new_m, new_v = {}, {}, {}, {}
    for k, n in enumerate(BIG_WEIGHTS):
        outs = None
        for l in reversed(range(n_layers)):
            mine, sibling = reduced[l][k]
            outs = _adamw_sharded(w[n], m[n], v[n], mine, sibling, l, outs, name="adamw%d_%s" % (l, n))
        grad[n], delta[n], new_m[n], new_v[n] = outs
    small_local = [jnp.stack([grads[l][n].reshape(w[n].shape[1:]) for l in range(n_layers)]) for n in SMALL_WEIGHTS]
    small_sum = _sum_slots(_allgather_all(_pack(small_local), name="allgather_small_grads"), name="sum_small_grads")
    grad.update(zip(SMALL_WEIGHTS, _unpack(small_sum, [w[n].shape for n in SMALL_WEIGHTS])))
    small_shapes = [w[n].shape for n in SMALL_WEIGHTS]
    res = _adamw(_pack([w[n] for n in SMALL_WEIGHTS]), small_sum, _pack([m[n] for n in SMALL_WEIGHTS]),
                 _pack([v[n] for n in SMALL_WEIGHTS]), name="adamw_small")
    for out, packed in zip((delta, new_m, new_v), res):
        out.update(zip(SMALL_WEIGHTS, _unpack(packed, small_shapes)))

    return (loss, dx[None], *[grad[n] for n in WEIGHT_NAMES], *[delta[n] for n in WEIGHT_NAMES],
            *[new_m[n] for n in WEIGHT_NAMES], *[new_v[n] for n in WEIGHT_NAMES])
```

```python
import functools
import math

import jax
import jax.numpy as jnp
from jax import lax
from jax.experimental import pallas as pl
from jax.experimental.pallas import tpu as pltpu

HEAD_DIM = 64
WINDOW = 128
SSM_GROUP_CH = 16
SSM_LANE_GROUPS = 8
RMS_EPS = 1e-6
ADAM_LR = 0.001
ADAM_B1 = 0.9
ADAM_B2 = 0.999
ADAM_EPS = 1e-08
ADAM_WD = 0.01
ADAM_STEP = 10
NEG_BIG = -1e30
MESH_AXES = ("x", "y", "c")
N_CHIPS = 4
N_DEV = 8
VMEM_LIMIT_BYTES = 56 * 1024 * 1024
BF16 = jnp.bfloat16
F32 = jnp.float32


def _cparams(*semantics):
    return pltpu.CompilerParams(dimension_semantics=semantics, vmem_limit_bytes=VMEM_LIMIT_BYTES)


def _pick(n, target, mult):
    if n <= target:
        return n
    best = None
    for d in range(mult, target + 1, mult):
        if n % d == 0:
            best = d
    assert best is not None, (n, target, mult)
    return best


def _rowmap(fn, ins, outs, *, rows, tm, ncol=1, name, deps=()):
    n_in = len(ins)
    nrow = rows // tm
    assert nrow * tm == rows

    in_specs = []
    for arr, kind, width, coloff in ins:
        if kind == "row":
            in_specs.append(pl.BlockSpec((tm, width), lambda j, i, o=coloff: (i, o + j)))
        elif kind == "vec":
            in_specs.append(pl.BlockSpec((1, width), lambda j, i, o=coloff: (0, o + j)))
        else:
            nd = arr.ndim
            in_specs.append(pl.BlockSpec(arr.shape, lambda j, i, nd=nd: (0,) * nd))
    out_specs, out_shapes = [], []
    for cols, dtype, kind, width in outs:
        if kind == "row":
            out_specs.append(pl.BlockSpec((tm, width), lambda j, i: (i, j)))
            out_shapes.append(jax.ShapeDtypeStruct((rows, cols), dtype))
        else:
            out_specs.append(pl.BlockSpec((1, width), lambda j, i: (0, j)))
            out_shapes.append(jax.ShapeDtypeStruct((1, cols), dtype))

    in_specs += [pl.BlockSpec(memory_space=pl.ANY)] * len(deps)

    def body(*refs):
        i = pl.program_id(1)
        res = fn(*[r[...] for r in refs[:n_in]])
        if not isinstance(res, (tuple, list)):
            res = (res,)
        for (cols, dtype, kind, width), ref, val in zip(outs, refs[n_in + len(deps):], res):
            if kind == "row":
                ref[...] = val.astype(ref.dtype)
            else:
                @pl.when(i == 0)
                def _():
                    ref[...] = jnp.zeros_like(ref)
                ref[...] += val.astype(ref.dtype)

    res = pl.pallas_call(
        body,
        out_shape=tuple(out_shapes),
        grid=(ncol, nrow),
        in_specs=in_specs,
        out_specs=tuple(out_specs),
        compiler_params=_cparams("parallel", "arbitrary"),
        name=name,
    )(*[a[0] for a in ins], *deps)
    return res


def _mm_body(dims, nk, has_add, unused_in=0):
    def body(*refs):
        if has_add:
            a_ref, b_ref, add_ref, o_ref = refs[:4]
            rest = refs[4:]
        else:
            a_ref, b_ref = refs[:2]
            o_ref = refs[2 + unused_in]
            add_ref = None
            rest = refs[3 + unused_in:]
        part = lax.dot_general(a_ref[...], b_ref[...], (dims, ((), ())), preferred_element_type=F32)
        if nk == 1:
            if add_ref is not None:
                part = part + add_ref[...]
            o_ref[...] = part.astype(o_ref.dtype)
        else:
            acc_ref = rest[0]
            k = pl.program_id(2)

            @pl.when(k == 0)
            def _():
                acc_ref[...] = part

            @pl.when(k > 0)
            def _():
                acc_ref[...] += part

            @pl.when(k == nk - 1)
            def _():
                r = acc_ref[...]
                if add_ref is not None:
                    r = r + add_ref[...]
                o_ref[...] = r.astype(o_ref.dtype)
    return body


class _Weight:
    def __init__(self, arr, layer, kind):
        self.arr, self.layer, self.kind = arr, layer, kind
        self.s, _, self.r, self.c = arr.shape
        self.rows = self.r * (self.s if kind == "row" else 1)
        self.cols = self.c * (self.s if kind == "col" else 1)

    def tiles(self, tr, tc):
        return _pick(self.r, tr, 128), _pick(self.c, tc, 128)

    def index(self, tr, tc):
        layer = self.layer
        if self.kind == "col":
            per = self.c // tc
            return lambda rb, cb: (cb // per, layer, rb, cb % per)
        per = self.r // tr
        return lambda rb, cb: (rb // per, layer, rb % per, cb)


def _shard_index(kind, r, c, tr, tc):
    if kind == "col":
        per = c // tc
        return lambda rb, cb: (cb // per, rb, cb % per)
    per = r // tr
    return lambda rb, cb: (rb // per, rb % per, cb)


def _mm_nn(a, w, *, out_dtype, tm, tn, tk, name, add=None):
    m, k = a.shape
    assert k == w.rows
    tm = _pick(m, tm, 16)
    tk, tn = w.tiles(tk, tn)
    nk = k // tk
    widx = w.index(tk, tn)
    in_specs = [pl.BlockSpec((tm, tk), lambda n, i, kk: (i, kk)),
                pl.BlockSpec((None, None, tk, tn), lambda n, i, kk: widx(kk, n))]
    args = [a, w.arr]
    if add is not None:
        in_specs.append(pl.BlockSpec((tm, tn), lambda n, i, kk: (i, n)))
        args.append(add)
    return pl.pallas_call(
        _mm_body(((1,), (0,)), nk, add is not None),
        out_shape=jax.ShapeDtypeStruct((m, w.cols), out_dtype),
        grid=(w.cols // tn, m // tm, nk),
        in_specs=in_specs,
        out_specs=pl.BlockSpec((tm, tn), lambda n, i, kk: (i, n)),
        scratch_shapes=[pltpu.VMEM((tm, tn), F32)] if nk > 1 else [],
        compiler_params=_cparams("parallel", "parallel", "arbitrary"),
        name=name,
    )(*args)


def _mm_nt(a, w, *, out_dtype, tm, tn, tko, name, deps=()):
    m, n = a.shape
    assert n == w.cols
    tm = _pick(m, tm, 16)
    tko, tn = w.tiles(tko, tn)
    nk = n // tn
    widx = w.index(tko, tn)
    return pl.pallas_call(
        _mm_body(((1,), (1,)), nk, False, unused_in=len(deps)),
        out_shape=jax.ShapeDtypeStruct((m, w.rows), out_dtype),
        grid=(w.rows // tko, m // tm, nk),
        in_specs=[pl.BlockSpec((tm, tn), lambda ko, i, nn: (i, nn)),
                  pl.BlockSpec((None, None, tko, tn), lambda ko, i, nn: widx(ko, nn))]
        + [pl.BlockSpec(memory_space=pl.ANY)] * len(deps),
        out_specs=pl.BlockSpec((tm, tko), lambda ko, i, nn: (i, ko)),
        scratch_shapes=[pltpu.VMEM((tm, tko), F32)] if nk > 1 else [],
        compiler_params=_cparams("parallel", "parallel", "arbitrary"),
        name=name,
    )(a, w.arr, *deps)


def _mm_tn(a, c, w, *, into, tm, tn, tko, name):
    m, k = a.shape
    mc, n = c.shape
    assert mc == m and k == w.rows and n == w.cols
    tm = _pick(m, tm, 16)
    tko, tn = w.tiles(tko, tn)
    nk = m // tm
    oidx = _shard_index(w.kind, w.r, w.c, tko, tn)
    layer = w.layer
    in_specs = [pl.BlockSpec((tm, tko), lambda ko, nn, mm: (mm, ko)),
                pl.BlockSpec((tm, tn), lambda ko, nn, mm: (mm, nn))]
    args = [a, c]
    if into is not None:
        in_specs.append(pl.BlockSpec(memory_space=pl.ANY))
        args.append(into)
    return pl.pallas_call(
        _mm_body(((0,), (0,)), nk, False, unused_in=len(args) - 2),
        out_shape=jax.ShapeDtypeStruct((w.arr.shape[1], w.s, w.r, w.c), F32),
        grid=(k // tko, n // tn, nk),
        in_specs=in_specs,
        out_specs=pl.BlockSpec((None, None, tko, tn), lambda ko, nn, mm: (layer,) + oidx(ko, nn)),
        scratch_shapes=[pltpu.VMEM((tko, tn), F32)] if nk > 1 else [],
        input_output_aliases={2: 0} if into is not None else {},
        compiler_params=_cparams("parallel", "parallel", "arbitrary"),
        name=name,
    )(*args)


def _rms(x, g):
    r = lax.rsqrt(jnp.mean(x * x, axis=-1, keepdims=True) + RMS_EPS)
    return x * r * g, r


def _rms_bwd(x, r, g, dy):
    dg = jnp.sum(dy * x * r, axis=0, keepdims=True)
    t = dy * g
    dx = r * t - x * (r * r * r) * jnp.mean(t * x, axis=-1, keepdims=True)
    return dx, dg


def _attn_consts(n_q, n_kv, sinks):
    group = n_q // n_kv
    t = jnp.arange(WINDOW, dtype=jnp.int32)[:, None]
    s = jnp.arange(2 * WINDOW, dtype=jnp.int32)[None, :] - WINDOW
    dist = (t - s).astype(F32)
    valid = (dist >= 0) & (dist < WINDOW)
    slopes = jnp.exp2(-8.0 * jnp.arange(1, n_q + 1, dtype=F32) / n_q)
    bias = jnp.where(valid[None], -slopes[:, None, None] * dist[None], NEG_BIG)
    bias = bias.reshape(n_kv, group * WINDOW, 2 * WINDOW)
    sink = jnp.broadcast_to(sinks.astype(F32).reshape(n_kv, group, 1, 1), (n_kv, group, WINDOW, 1))
    return bias, sink.reshape(n_kv, group * WINDOW, 1)


def _attn_probs(q_ref, kc_ref, kp_ref, vc_ref, vp_ref, qg, kg, sink, bias, first_mask, kv, group):
    sl = slice(kv * HEAD_DIM, (kv + 1) * HEAD_DIM)
    k2 = jnp.concatenate([kp_ref[:, sl], kc_ref[:, sl]], axis=0)
    v2 = jnp.concatenate([vp_ref[:, sl], vc_ref[:, sl]], axis=0)
    k2n, rk = _rms(k2, kg)
    qx, qn, rq = [], [], []
    for g in range(group):
        h = kv * group + g
        x = q_ref[:, h * HEAD_DIM:(h + 1) * HEAD_DIM]
        y, r = _rms(x, qg)
        qx.append(x); qn.append(y); rq.append(r)
    qs = jnp.concatenate(qn, axis=0).astype(BF16)
    k2b = k2n.astype(BF16)
    s = lax.dot_general(qs, k2b, (((1,), (1,)), ((), ())), preferred_element_type=F32) * (HEAD_DIM ** -0.5)
    s = jnp.where(first_mask, NEG_BIG, s + bias)
    m = jnp.maximum(jnp.max(s, axis=-1, keepdims=True), sink)
    p = jnp.exp(s - m)
    esink = jnp.exp(sink - m)
    denom = jnp.sum(p, axis=-1, keepdims=True) + esink
    pn = p / denom
    return dict(k2=k2, rk=rk, k2b=k2b, v2b=v2.astype(BF16), qx=qx, rq=rq, qs=qs, pn=pn, psink=esink / denom)


def _attn_specs(n_q, n_kv):
    aw, kvw = n_q * HEAD_DIM, n_kv * HEAD_DIM
    group = n_q // n_kv
    kblk, vblk = aw // kvw, aw // kvw + 1

    def specs(nb):
        cur = lambda n: jnp.minimum(n, nb - 1)
        prev = lambda n: jnp.maximum(jnp.minimum(n, nb - 1) - 1, 0)
        return [
            pl.BlockSpec((WINDOW, aw), lambda n: (cur(n), 0)),
            pl.BlockSpec((WINDOW, kvw), lambda n: (cur(n), kblk)),
            pl.BlockSpec((WINDOW, kvw), lambda n: (prev(n), kblk)),
            pl.BlockSpec((WINDOW, kvw), lambda n: (cur(n), vblk)),
            pl.BlockSpec((WINDOW, kvw), lambda n: (prev(n), vblk)),
        ]
    const_specs = [
        pl.BlockSpec((1, HEAD_DIM), lambda n: (0, 0)),
        pl.BlockSpec((1, HEAD_DIM), lambda n: (0, 0)),
        pl.BlockSpec((n_kv, group * WINDOW, 1), lambda n: (0, 0, 0)),
        pl.BlockSpec((n_kv, group * WINDOW, 2 * WINDOW), lambda n: (0, 0, 0)),
    ]
    return specs, const_specs


def _attn_fwd(z, qg, kg, sinks, *, n_q, n_kv, name):
    L = z.shape[0]
    nb = L // WINDOW
    aw = n_q * HEAD_DIM
    group = n_q // n_kv
    bias, sink = _attn_consts(n_q, n_kv, sinks)
    specs, const_specs = _attn_specs(n_q, n_kv)

    def body(q_ref, kc_ref, kp_ref, vc_ref, vp_ref, qg_ref, kg_ref, sink_ref, bias_ref, o_ref):
        n = pl.program_id(0)
        col = lax.broadcasted_iota(jnp.int32, (group * WINDOW, 2 * WINDOW), 1)
        first_mask = jnp.logical_and(n == 0, col < WINDOW)
        for kv in range(n_kv):
            a = _attn_probs(q_ref, kc_ref, kp_ref, vc_ref, vp_ref, qg_ref[...], kg_ref[...],
                            sink_ref[kv], bias_ref[kv], first_mask, kv, group)
            o = jnp.dot(a["pn"].astype(BF16), a["v2b"], preferred_element_type=F32)
            for g in range(group):
                h = kv * group + g
                o_ref[:, h * HEAD_DIM:(h + 1) * HEAD_DIM] = o[g * WINDOW:(g + 1) * WINDOW].astype(o_ref.dtype)

    return pl.pallas_call(
        body,
        out_shape=jax.ShapeDtypeStruct((L, aw), BF16),
        grid=(nb,),
        in_specs=specs(nb) + const_specs,
        out_specs=pl.BlockSpec((WINDOW, aw), lambda n: (n, 0)),
        compiler_params=_cparams("parallel"),
        name=name,
    )(z, z, z, z, z, qg, kg, sink, bias)


def _attn_bwd(z, do, qg, kg, sinks, *, n_q, n_kv, name):
    L = z.shape[0]
    nb = L // WINDOW
    aw, kvw = n_q * HEAD_DIM, n_kv * HEAD_DIM
    group = n_q // n_kv
    bias, sink = _attn_consts(n_q, n_kv, sinks)
    specs, const_specs = _attn_specs(n_q, n_kv)
    scale = HEAD_DIM ** -0.5

    def body(q_ref, kc_ref, kp_ref, vc_ref, vp_ref, do_ref, qg_ref, kg_ref, sink_ref, bias_ref,
             dq_ref, dkv_ref, dqg_ref, dkg_ref, dsink_ref, carry_ref):
        n = pl.program_id(0)

        @pl.when(n == 0)
        def _():
            dqg_ref[...] = jnp.zeros_like(dqg_ref)
            dkg_ref[...] = jnp.zeros_like(dkg_ref)
            dsink_ref[...] = jnp.zeros_like(dsink_ref)
            carry_ref[...] = jnp.zeros_like(carry_ref)

        @pl.when(n < nb)
        def _():
            col = lax.broadcasted_iota(jnp.int32, (group * WINDOW, 2 * WINDOW), 1)
            first_mask = jnp.logical_and(n == 0, col < WINDOW)
            head_lane = lax.broadcasted_iota(jnp.int32, (1, n_q), 1)
            qg, kg = qg_ref[...], kg_ref[...]
            dqg = jnp.zeros((1, HEAD_DIM), F32)
            dkg = jnp.zeros((1, HEAD_DIM), F32)
            dsink = jnp.zeros((1, n_q), F32)
            for kv in range(n_kv):
                a = _attn_probs(q_ref, kc_ref, kp_ref, vc_ref, vp_ref, qg, kg,
                                sink_ref[kv], bias_ref[kv], first_mask, kv, group)
                pn = a["pn"]
                dos = jnp.concatenate(
                    [do_ref[:, (kv * group + g) * HEAD_DIM:(kv * group + g + 1) * HEAD_DIM] for g in range(group)],
                    axis=0).astype(BF16)
                dpn = lax.dot_general(dos, a["v2b"], (((1,), (1,)), ((), ())), preferred_element_type=F32)
                dv2 = lax.dot_general(pn.astype(BF16), dos, (((0,), (0,)), ((), ())), preferred_element_type=F32)
                delta = jnp.sum(pn * dpn, axis=-1, keepdims=True)
                ds = (pn * (dpn - delta)).astype(BF16)
                dsk = -a["psink"] * delta
                dqn = lax.dot_general(ds, a["k2b"], (((1,), (0,)), ((), ())), preferred_element_type=F32) * scale
                dk2n = lax.dot_general(ds, a["qs"], (((0,), (0,)), ((), ())), preferred_element_type=F32) * scale
                for g in range(group):
                    h = kv * group + g
                    rows = slice(g * WINDOW, (g + 1) * WINDOW)
                    dx, dgq = _rms_bwd(a["qx"][g], a["rq"][g], qg, dqn[rows])
                    dq_ref[:, h * HEAD_DIM:(h + 1) * HEAD_DIM] = dx.astype(dq_ref.dtype)
                    dqg = dqg + dgq
                    dsink = dsink + jnp.where(head_lane == h, jnp.sum(dsk[rows], axis=0, keepdims=True), 0.0)
                dk2, dgk = _rms_bwd(a["k2"], a["rk"], kg, dk2n)
                dkg = dkg + dgk
                ksl = slice(kv * HEAD_DIM, (kv + 1) * HEAD_DIM)
                vsl = slice(kvw + kv * HEAD_DIM, kvw + (kv + 1) * HEAD_DIM)
                dkv_ref[:, ksl] = (carry_ref[:, ksl] + dk2[:WINDOW]).astype(dkv_ref.dtype)
                dkv_ref[:, vsl] = (carry_ref[:, vsl] + dv2[:WINDOW]).astype(dkv_ref.dtype)
                carry_ref[:, ksl] = dk2[WINDOW:]
                carry_ref[:, vsl] = dv2[WINDOW:]
            dqg_ref[...] += dqg
            dkg_ref[...] += dkg
            dsink_ref[...] += dsink

        @pl.when(n == nb)
        def _():
            dkv_ref[...] = carry_ref[...].astype(dkv_ref.dtype)

    in_specs = specs(nb) + [pl.BlockSpec((WINDOW, aw), lambda n: (jnp.minimum(n, nb - 1), 0))] + const_specs
    return pl.pallas_call(
        body,
        out_shape=(jax.ShapeDtypeStruct((L, aw), BF16), jax.ShapeDtypeStruct((L, 2 * kvw), BF16),
                   jax.ShapeDtypeStruct((1, HEAD_DIM), F32), jax.ShapeDtypeStruct((1, HEAD_DIM), F32),
                   jax.ShapeDtypeStruct((1, n_q), F32)),
        grid=(nb + 1,),
        in_specs=in_specs,
        out_specs=(pl.BlockSpec((WINDOW, aw), lambda n: (jnp.minimum(n, nb - 1), 0)),
                   pl.BlockSpec((WINDOW, 2 * kvw), lambda n: (jnp.maximum(n - 1, 0), 0)),
                   pl.BlockSpec((1, HEAD_DIM), lambda n: (0, 0)),
                   pl.BlockSpec((1, HEAD_DIM), lambda n: (0, 0)),
                   pl.BlockSpec((1, n_q), lambda n: (0, 0))),
        scratch_shapes=[pltpu.VMEM((WINDOW, 2 * kvw), F32)],
        compiler_params=_cparams("arbitrary"),
        name=name,
    )(z, z, z, z, z, do, qg, kg, sink, bias)


def _cmul(ar, ai, br, bi):
    return ar * br - ai * bi, ar * bi + ai * br


def _scan_consts(ar, ai, n, reverse):
    row = lax.broadcasted_iota(jnp.int32, (8, n), 0)
    mults = []
    pr, pi = ar, ai
    for k in (1, 2, 4):
        keep = (row < 8 - k) if reverse else (row >= k)
        mults.append(((8 - k) if reverse else k, jnp.where(keep, pr, 0.0), jnp.where(keep, pi, 0.0)))
        pr, pi = _cmul(pr, pi, pr, pi)
    pwr = jnp.zeros((8, n), F32)
    pwi = jnp.zeros((8, n), F32)
    pr, pi = ar, ai
    for e in range(1, 9):
        sel = (row == 8 - e) if reverse else (row == e - 1)
        pwr = jnp.where(sel, pr, pwr)
        pwi = jnp.where(sel, pi, pwi)
        pr, pi = _cmul(pr, pi, ar, ai)
    return mults, pwr, pwi


def _scan8(xr, xi, mults, pwr, pwi, cr, ci):
    for shift, mr, mi in mults:
        sr = pltpu.roll(xr, shift, 0)
        si = pltpu.roll(xi, shift, 0)
        xr, xi = xr + mr * sr - mi * si, xi + mr * si + mi * sr
    return xr + pwr * cr - pwi * ci, xi + pwr * ci + pwi * cr


def _blockdiag(x):
    g, a, b = x.shape
    j = g // SSM_LANE_GROUPS
    eye = jnp.eye(SSM_LANE_GROUPS, dtype=x.dtype)
    y = x.reshape(j, SSM_LANE_GROUPS, a, 1, b) * eye[None, :, None, :, None]
    return y.reshape(j, SSM_LANE_GROUPS * a, SSM_LANE_GROUPS * b)


def _blockdiag_extract(y, a, b):
    j = y.shape[0]
    y = y.reshape(j, SSM_LANE_GROUPS, a, SSM_LANE_GROUPS, b)
    return jnp.einsum("jgahb,gh->jgab", y, jnp.eye(SSM_LANE_GROUPS, dtype=y.dtype)).reshape(j * SSM_LANE_GROUPS, a, b)


def _ssm_disc(lr, li, ldt, brt, bit):
    dt = jnp.exp(ldt)
    mag = jnp.exp(lr * dt)
    ar = mag * jnp.cos(li * dt)
    ai = mag * jnp.sin(li * dt)
    den = lr * lr + li * li
    fr = ((ar - 1.0) * lr + ai * li) / den
    fi = (ai * lr - (ar - 1.0) * li) / den
    bbr = fr[:, None, :] * brt - fi[:, None, :] * bit
    bbi = fr[:, None, :] * bit + fi[:, None, :] * brt
    return ar, ai, bbr, bbi


def _ssm_prep(lr, li, ldt, brt, bit, *, name):
    g, h, p = brt.shape

    def body(lr_ref, li_ref, ldt_ref, brt_ref, bit_ref, ar_ref, ai_ref, bbr_ref, bbi_ref):
        ar, ai, bbr, bbi = _ssm_disc(lr_ref[...], li_ref[...], ldt_ref[...], brt_ref[...], bit_ref[...])
        ar_ref[...] = ar
        ai_ref[...] = ai
        bbr_ref[...] = bbr
        bbi_ref[...] = bbi

    gp = jax.ShapeDtypeStruct((g, p), F32)
    ghp = jax.ShapeDtypeStruct((g, h, p), F32)
    return pl.pallas_call(body, out_shape=(gp, gp, ghp, ghp), name=name)(lr, li, ldt, brt, bit)


def _ssm_prep_bwd(lr, li, ldt, brt, bit, dar, dai, dbbr, dbbi, *, name):
    g, h, p = brt.shape

    def body(lr_ref, li_ref, ldt_ref, brt_ref, bit_ref, dar_ref, dai_ref, dbbr_ref, dbbi_ref,
             dlr_ref, dli_ref, dldt_ref, dbrt_ref, dbit_ref):
        _, vjp = jax.vjp(_ssm_disc, lr_ref[...], li_ref[...], ldt_ref[...], brt_ref[...], bit_ref[...])
        dlr, dli, dldt, dbrt, dbit = vjp((dar_ref[...], dai_ref[...], dbbr_ref[...], dbbi_ref[...]))
        dlr_ref[...] = dlr
        dli_ref[...] = dli
        dldt_ref[...] = dldt
        dbrt_ref[...] = dbrt
        dbit_ref[...] = dbit

    gp = jax.ShapeDtypeStruct((g, p), F32)
    ghp = jax.ShapeDtypeStruct((g, h, p), F32)
    return pl.pallas_call(body, out_shape=(gp, gp, jax.ShapeDtypeStruct((g, 1), F32), ghp, ghp), name=name)(
        lr, li, ldt, brt, bit, dar, dai, dbbr, dbbi)


def _ssm_specs(tc, nlanes, nch, u_colblk, chunk_of):
    return [
        pl.BlockSpec((tc, nch), lambda j, c: (chunk_of(c), u_colblk + j)),
        pl.BlockSpec((1, nlanes), lambda j, c: (0, j)),
        pl.BlockSpec((1, nlanes), lambda j, c: (0, j)),
        pl.BlockSpec((None, nch, nlanes), lambda j, c: (j, 0, 0)),
        pl.BlockSpec((None, nch, nlanes), lambda j, c: (j, 0, 0)),
        pl.BlockSpec((None, nlanes, nch), lambda j, c: (j, 0, 0)),
        pl.BlockSpec((None, nlanes, nch), lambda j, c: (j, 0, 0)),
        pl.BlockSpec((1, nch), lambda j, c: (0, j)),
    ]


def _ssm_fwd(z, ar, ai, bblk_r, bblk_i, cblk_r, cblk_i, d, *, u_col, tc, name):
    L = z.shape[0]
    nj, nch, nlanes = bblk_r.shape
    w = nj * nch
    nc = L // tc
    ng = tc // 8

    def body(u_ref, ar_ref, ai_ref, br_ref, bi_ref, cr_ref, ci_ref, d_ref, y_ref, s0r_ref, s0i_ref,
             xr_ref, xi_ref, carr_ref, cari_ref):
        c = pl.program_id(1)

        @pl.when(c == 0)
        def _():
            carr_ref[...] = jnp.zeros_like(carr_ref)
            cari_ref[...] = jnp.zeros_like(cari_ref)

        s0r_ref[...] = carr_ref[...]
        s0i_ref[...] = cari_ref[...]
        u = u_ref[...]
        ub = u.astype(BF16)
        xr_ref[...] = jnp.dot(ub, br_ref[...].astype(BF16), preferred_element_type=F32)
        xi_ref[...] = jnp.dot(ub, bi_ref[...].astype(BF16), preferred_element_type=F32)
        mults, pwr, pwi = _scan_consts(ar_ref[...], ai_ref[...], nlanes, False)

        def step(g, carry):
            rows = pl.ds(pl.multiple_of(g * 8, 8), 8)
            sr, si = _scan8(xr_ref[rows, :], xi_ref[rows, :], mults, pwr, pwi, carry[0], carry[1])
            xr_ref[rows, :] = sr
            xi_ref[rows, :] = si
            return sr[7:8], si[7:8]

        cr, ci = lax.fori_loop(0, ng, step, (carr_ref[...], cari_ref[...]))
        carr_ref[...] = cr
        cari_ref[...] = ci
        y = (jnp.dot(xr_ref[...].astype(BF16), cr_ref[...].astype(BF16), preferred_element_type=F32)
             - jnp.dot(xi_ref[...].astype(BF16), ci_ref[...].astype(BF16), preferred_element_type=F32)
             + d_ref[...] * u)
        y_ref[...] = y

    state = jax.ShapeDtypeStruct((nc, 1, nj * nlanes), F32)
    state_spec = pl.BlockSpec((None, 1, nlanes), lambda j, c: (c, 0, j))
    return pl.pallas_call(
        body,
        out_shape=(jax.ShapeDtypeStruct((L, w), F32), state, state),
        grid=(nj, nc),
        in_specs=_ssm_specs(tc, nlanes, nch, u_col // nch, lambda c: c),
        out_specs=(pl.BlockSpec((tc, nch), lambda j, c: (c, j)), state_spec, state_spec),
        scratch_shapes=[pltpu.VMEM((tc, nlanes), F32), pltpu.VMEM((tc, nlanes), F32),
                        pltpu.VMEM((1, nlanes), F32), pltpu.VMEM((1, nlanes), F32)],
        compiler_params=_cparams("parallel", "arbitrary"),
        name=name,
    )(z, ar, ai, bblk_r, bblk_i, cblk_r, cblk_i, d)


def _ssm_bwd(z, dy, s0r, s0i, ar, ai, bblk_r, bblk_i, cblk_r, cblk_i, d, *, u_col, tc, name):
    L = z.shape[0]
    nj, nch, nlanes = bblk_r.shape
    w = nj * nch
    nc = L // tc
    ng = tc // 8
    chunk_of = lambda c: nc - 1 - c

    def body(u_ref, ar_ref, ai_ref, br_ref, bi_ref, cr_ref, ci_ref, d_ref, dy_ref, s0r_ref, s0i_ref,
             du_ref, dbr_ref, dbi_ref, dcr_ref, dci_ref, dar_ref, dai_ref, dd_ref,
             sr_ref, si_ref, pr_ref, pi_ref, lr_ref, li_ref, carr_ref, cari_ref):
        c = pl.program_id(1)

        @pl.when(c == 0)
        def _():
            for ref in (dbr_ref, dbi_ref, dcr_ref, dci_ref, dar_ref, dai_ref, dd_ref, carr_ref, cari_ref):
                ref[...] = jnp.zeros_like(ref)

        u = u_ref[...]
        dyv = dy_ref[...]
        ub = u.astype(BF16)
        dyb = dyv.astype(BF16)
        brb = br_ref[...].astype(BF16)
        bib = bi_ref[...].astype(BF16)
        crb = cr_ref[...].astype(BF16)
        cib = ci_ref[...].astype(BF16)
        a_r, a_i = ar_ref[...], ai_ref[...]

        sr_ref[...] = jnp.dot(ub, brb, preferred_element_type=F32)
        si_ref[...] = jnp.dot(ub, bib, preferred_element_type=F32)
        mults, pwr, pwi = _scan_consts(a_r, a_i, nlanes, False)
        row = lax.broadcasted_iota(jnp.int32, (8, nlanes), 0)

        def fstep(g, carry):
            rows = pl.ds(pl.multiple_of(g * 8, 8), 8)
            sr, si = _scan8(sr_ref[rows, :], si_ref[rows, :], mults, pwr, pwi, carry[0], carry[1])
            sr_ref[rows, :] = sr
            si_ref[rows, :] = si
            pr_ref[rows, :] = jnp.where(row == 0, carry[0], pltpu.roll(sr, 1, 0))
            pi_ref[rows, :] = jnp.where(row == 0, carry[1], pltpu.roll(si, 1, 0))
            return sr[7:8], si[7:8]

        lax.fori_loop(0, ng, fstep, (s0r_ref[...], s0i_ref[...]))

        nt = (((1,), (1,)), ((), ()))
        lr_ref[...] = lax.dot_general(dyb, crb, nt, preferred_element_type=F32)
        li_ref[...] = -lax.dot_general(dyb, cib, nt, preferred_element_type=F32)
        rmults, rpwr, rpwi = _scan_consts(a_r, -a_i, nlanes, True)

        def rstep(gg, carry):
            cr, ci, acc_r, acc_i = carry
            rows = pl.ds(pl.multiple_of((ng - 1 - gg) * 8, 8), 8)
            lr, li = _scan8(lr_ref[rows, :], li_ref[rows, :], rmults, rpwr, rpwi, cr, ci)
            lr_ref[rows, :] = lr
            li_ref[rows, :] = li
            pr, pi = pr_ref[rows, :], pi_ref[rows, :]
            return lr[0:1], li[0:1], acc_r + lr * pr + li * pi, acc_i + li * pr - lr * pi

        zero8 = jnp.zeros((8, nlanes), F32)
        cr, ci, acc_r, acc_i = lax.fori_loop(0, ng, rstep, (carr_ref[...], cari_ref[...], zero8, zero8))
        carr_ref[...] = cr
        cari_ref[...] = ci
        dar_ref[...] += jnp.sum(acc_r, axis=0, keepdims=True)
        dai_ref[...] += jnp.sum(acc_i, axis=0, keepdims=True)

        tn = (((0,), (0,)), ((), ()))
        lrb = lr_ref[...].astype(BF16)
        lib = li_ref[...].astype(BF16)
        dcr_ref[...] += lax.dot_general(sr_ref[...].astype(BF16), dyb, tn, preferred_element_type=F32)
        dci_ref[...] -= lax.dot_general(si_ref[...].astype(BF16), dyb, tn, preferred_element_type=F32)
        dbr_ref[...] += lax.dot_general(ub, lrb, tn, preferred_element_type=F32)
        dbi_ref[...] += lax.dot_general(ub, lib, tn, preferred_element_type=F32)
        du = (lax.dot_general(lrb, brb, nt, preferred_element_type=F32)
              + lax.dot_general(lib, bib, nt, preferred_element_type=F32)
              + d_ref[...] * dyv)
        du_ref[...] = du.astype(du_ref.dtype)
        dd_ref[...] += jnp.sum(dyv * u, axis=0, keepdims=True)

    state_spec = pl.BlockSpec((None, 1, nlanes), lambda j, c: (chunk_of(c), 0, j))
    bshape = jax.ShapeDtypeStruct((nj, nch, nlanes), F32)
    cshape = jax.ShapeDtypeStruct((nj, nlanes, nch), F32)
    ashape = jax.ShapeDtypeStruct((1, nj * nlanes), F32)
    bspec = pl.BlockSpec((None, nch, nlanes), lambda j, c: (j, 0, 0))
    cspec = pl.BlockSpec((None, nlanes, nch), lambda j, c: (j, 0, 0))
    aspec = pl.BlockSpec((1, nlanes), lambda j, c: (0, j))
    big = pltpu.VMEM((tc, nlanes), F32)
    return pl.pallas_call(
        body,
        out_shape=(jax.ShapeDtypeStruct((L, w), BF16), bshape, bshape, cshape, cshape, ashape, ashape,
                   jax.ShapeDtypeStruct((1, w), F32)),
        grid=(nj, nc),
        in_specs=_ssm_specs(tc, nlanes, nch, u_col // nch, chunk_of)
        + [pl.BlockSpec((tc, nch), lambda j, c: (chunk_of(c), j)), state_spec, state_spec],
        out_specs=(pl.BlockSpec((tc, nch), lambda j, c: (chunk_of(c), j)), bspec, bspec, cspec, cspec, aspec, aspec,
                   pl.BlockSpec((1, nch), lambda j, c: (0, j))),
        scratch_shapes=[big, big, big, big, big, big, pltpu.VMEM((1, nlanes), F32), pltpu.VMEM((1, nlanes), F32)],
        compiler_params=_cparams("parallel", "arbitrary"),
        name=name,
    )(z, ar, ai, bblk_r, bblk_i, cblk_r, cblk_i, d, dy, s0r, s0i)


def _rmsnorm_rows(x, g):
    return x * lax.rsqrt(jnp.mean(x * x, axis=-1, keepdims=True) + RMS_EPS) * g


def _glu_out(y_raw, pre, b):
    yg = jax.nn.gelu(y_raw)
    return yg * jax.nn.sigmoid(pre + b)


def _gate_merge(za, zs, ba, bs, a, bm):
    return jax.nn.sigmoid(za + ba) * a + jax.nn.sigmoid(zs + bs) * bm


def _swiglu(g, u):
    return jax.nn.silu(g) * u


def _row_tile(width_bytes_per_row, rows):
    budget = VMEM_LIMIT_BYTES // 3
    t = max(8, min(1024, budget // (2 * max(width_bytes_per_row, 1))))
    return _pick(rows, t, 16)


def _ssm_params(p, prefix):
    g, pst = p["lam_re"].shape
    ar, ai, bbr, bbi = _ssm_prep(p["lam_re"], p["lam_im"], p["log_dt"], p["b_re_t"], p["b_im_t"], name=prefix + "_ssm_prep")
    return dict(ar=ar.reshape(1, g * pst), ai=ai.reshape(1, g * pst),
                bblk_r=_blockdiag(bbr), bblk_i=_blockdiag(bbi),
                cblk_r=_blockdiag(jnp.swapaxes(p["c_re"], 1, 2)), cblk_i=_blockdiag(jnp.swapaxes(p["c_im"], 1, 2)))


def _layer_fwd(x, p, dims, prefix, deps=()):
    t, d = x.shape
    aw, kvw, sw, ff = dims["aw"], dims["kvw"], dims["sw"], dims["ff"]
    off_u = aw + 2 * kvw
    off_g = off_u + sw
    gblk = _pick(d, 512, 128)
    assert off_g % gblk == 0 and off_u % (SSM_LANE_GROUPS * SSM_GROUP_CH) == 0
    sv = {"x": x}

    h, = _rowmap(_rmsnorm_rows, [(x, "row", d, 0), (p["norm_mix_g"], "vec", d, 0)], [(d, BF16, "row", d)],
                 rows=t, tm=_row_tile(6 * d, t), name=prefix + "_norm_mix", deps=deps)
    z = _mm_nn(h, p["w_in"], out_dtype=F32, tm=512, tn=1664, tk=2048, name=prefix + "_mm_in")
    ya = _attn_fwd(z, p["q_norm_g"], p["k_norm_g"], p["attn_sinks"], n_q=dims["n_q"], n_kv=dims["n_kv"],
                   name=prefix + "_attn_fwd")
    sp = _ssm_params(p, prefix)
    y_raw, s0r, s0i = _ssm_fwd(z, sp["ar"], sp["ai"], sp["bblk_r"], sp["bblk_i"], sp["cblk_r"], sp["cblk_i"], p["ssm_d"],
                               u_col=off_u, tc=dims["tc"], name=prefix + "_ssm_fwd")
    yg, = _rowmap(jax.nn.gelu, [(y_raw, "row", sw, 0)], [(sw, BF16, "row", sw)],
                  rows=t, tm=_row_tile(6 * sw, t), name=prefix + "_gelu")
    pre = _mm_nn(yg, p["ssm_glu_w"], out_dtype=F32, tm=1024, tn=1024, tk=1024, name=prefix + "_mm_glu")
    y2, = _rowmap(_glu_out, [(y_raw, "row", sw, 0), (pre, "row", sw, 0), (p["ssm_glu_b"], "vec", sw, 0)],
                  [(sw, BF16, "row", sw)], rows=t, tm=_row_tile(10 * sw, t), name=prefix + "_glu_out")
    a = _mm_nn(ya, p["w_attn_branch"], out_dtype=F32, tm=1024, tn=512, tk=1024, name=prefix + "_mm_ab")
    bm = _mm_nn(y2, p["w_ssm_branch"], out_dtype=F32, tm=1024, tn=512, tk=1024, name=prefix + "_mm_sb")
    ngb = d // gblk
    merged, = _rowmap(
        _gate_merge,
        [(z, "row", gblk, off_g // gblk), (z, "row", gblk, off_g // gblk + ngb),
         (p["gate_bias"], "vec", gblk, 0), (p["gate_bias"], "vec", gblk, ngb),
         (a, "row", gblk, 0), (bm, "row", gblk, 0)],
        [(d, BF16, "row", gblk)], rows=t, tm=_row_tile(18 * gblk, t), ncol=ngb, name=prefix + "_gate")
    x1 = _mm_nn(merged, p["w_out"], out_dtype=F32, tm=512, tn=1024, tk=2048, name=prefix + "_mm_out", add=x)
    h2, = _rowmap(_rmsnorm_rows, [(x1, "row", d, 0), (p["norm_ffn_g"], "vec", d, 0)], [(d, BF16, "row", d)],
                  rows=t, tm=_row_tile(6 * d, t), name=prefix + "_norm_ffn")
    gu = _mm_nn(h2, p["w_ffn_in"], out_dtype=F32, tm=512, tn=1408, tk=2048, name=prefix + "_mm_ffn_in")
    fblk = _pick(ff, 1408, 128)
    nfb = ff // fblk
    act, = _rowmap(_swiglu, [(gu, "row", fblk, 0), (gu, "row", fblk, nfb)], [(ff, BF16, "row", fblk)],
                   rows=t, tm=_row_tile(10 * fblk, t), ncol=nfb, name=prefix + "_swiglu")
    x2 = _mm_nn(act, p["w_ffn_out"], out_dtype=F32, tm=512, tn=512, tk=5632, name=prefix + "_mm_ffn_out", add=x1)
    sv.update(h=h, z=z, ya=ya, sp=sp, y_raw=y_raw, s0r=s0r, s0i=s0i, yg=yg, pre=pre, y2=y2, a=a, bm=bm,
              merged=merged, x1=x1, h2=h2, gu=gu, act=act)
    return x2, sv


def _layer_bwd(dx2, dx2b, sv, p, dims, prefix, gbuf, deps=(), before_mixer=None, before_in=None):
    t, d = dx2.shape
    aw, kvw, sw, ff = dims["aw"], dims["kvw"], dims["sw"], dims["ff"]
    off_u = aw + 2 * kvw
    off_g = off_u + sw
    gblk = _pick(d, 512, 128)
    ngb = d // gblk
    fblk = _pick(ff, 1408, 128)
    nfb = ff // fblk
    g = {}

    dact = _mm_nt(dx2b, p["w_ffn_out"], out_dtype=F32, tm=512, tn=2048, tko=1408, name=prefix + "_mm_dact", deps=deps)
    g["w_ffn_out"] = _mm_tn(sv["act"], dx2b, p["w_ffn_out"], into=gbuf.get("w_ffn_out"), tm=4096, tn=1024, tko=512,
                            name=prefix + "_mm_dw_ffn_out")

    def swiglu_bwd(gg, uu, da):
        _, vjp = jax.vjp(_swiglu, gg, uu)
        return vjp(da)

    dgu_g, dgu_u = _rowmap(swiglu_bwd, [(sv["gu"], "row", fblk, 0), (sv["gu"], "row", fblk, nfb), (dact, "row", fblk, 0)],
                           [(ff, BF16, "row", fblk), (ff, BF16, "row", fblk)],
                           rows=t, tm=_row_tile(16 * fblk, t), ncol=nfb, name=prefix + "_swiglu_bwd")
    dgu = jnp.concatenate([dgu_g, dgu_u], axis=1)
    dh2 = _mm_nt(dgu, p["w_ffn_in"], out_dtype=F32, tm=512, tn=1408, tko=2048, name=prefix + "_mm_dh2")
    g["w_ffn_in"] = _mm_tn(sv["h2"], dgu, p["w_ffn_in"], into=gbuf.get("w_ffn_in"), tm=4096, tn=1408, tko=512,
                           name=prefix + "_mm_dw_ffn_in")

    def norm_bwd(xx, gg, dh, dres):
        _, vjp = jax.vjp(_rmsnorm_rows, xx, gg)
        dxx, dgg = vjp(dh)
        dxx = dxx + dres
        return dxx, dxx, dgg

    dx1, dx1b, g["norm_ffn_g"] = _rowmap(
        norm_bwd, [(sv["x1"], "row", d, 0), (p["norm_ffn_g"], "vec", d, 0), (dh2, "row", d, 0), (dx2, "row", d, 0)],
        [(d, F32, "row", d), (d, BF16, "row", d), (d, F32, "acc", d)],
        rows=t, tm=_row_tile(22 * d, t), name=prefix + "_norm_ffn_bwd")

    deps = before_mixer(dx1) if before_mixer else ()
    dmerged = _mm_nt(dx1b, p["w_out"], out_dtype=F32, tm=1024, tn=2048, tko=1024, name=prefix + "_mm_dmerged", deps=deps)
    g["w_out"] = _mm_tn(sv["merged"], dx1b, p["w_out"], into=gbuf.get("w_out"), tm=4096, tn=1024, tko=512,
                        name=prefix + "_mm_dw_out")

    def gate_bwd(za, zs, ba, bs, aa, bb, dm):
        _, vjp = jax.vjp(_gate_merge, za, zs, ba, bs, aa, bb)
        dza, dzs, dba, dbs, daa, dbb = vjp(dm)
        return daa, dbb, dza, dzs, dba, dbs

    z = sv["z"]
    da, dbm, dza, dzs, dba, dbs = _rowmap(
        gate_bwd,
        [(z, "row", gblk, off_g // gblk), (z, "row", gblk, off_g // gblk + ngb),
         (p["gate_bias"], "vec", gblk, 0), (p["gate_bias"], "vec", gblk, ngb),
         (sv["a"], "row", gblk, 0), (sv["bm"], "row", gblk, 0), (dmerged, "row", gblk, 0)],
        [(d, BF16, "row", gblk), (d, BF16, "row", gblk), (d, BF16, "row", gblk), (d, BF16, "row", gblk),
         (d, F32, "acc", gblk), (d, F32, "acc", gblk)],
        rows=t, tm=_row_tile(32 * gblk, t), ncol=ngb, name=prefix + "_gate_bwd")
    g["gate_bias"] = jnp.concatenate([dba, dbs], axis=1)
    dya = _mm_nt(da, p["w_attn_branch"], out_dtype=F32, tm=1024, tn=512, tko=1024, name=prefix + "_mm_dya")
    g["w_attn_branch"] = _mm_tn(sv["ya"], da, p["w_attn_branch"], into=gbuf.get("w_attn_branch"), tm=4096, tn=512,
                                tko=512, name=prefix + "_mm_dw_ab")
    dy2 = _mm_nt(dbm, p["w_ssm_branch"], out_dtype=F32, tm=1024, tn=512, tko=1024, name=prefix + "_mm_dy2")
    g["w_ssm_branch"] = _mm_tn(sv["y2"], dbm, p["w_ssm_branch"], into=gbuf.get("w_ssm_branch"), tm=4096, tn=512,
                               tko=512, name=prefix + "_mm_dw_sb")

    def glu_bwd(y_raw, pre, b, dy):
        yg = jax.nn.gelu(y_raw)
        _, vjp = jax.vjp(lambda a_, b_, c_: a_ * jax.nn.sigmoid(b_ + c_), yg, pre, b)
        dyg, dpre, db = vjp(dy)
        return dyg, dpre, db

    dyg_direct, dpre, g["ssm_glu_b"] = _rowmap(
        glu_bwd, [(sv["y_raw"], "row", sw, 0), (sv["pre"], "row", sw, 0), (p["ssm_glu_b"], "vec", sw, 0), (dy2, "row", sw, 0)],
        [(sw, F32, "row", sw), (sw, BF16, "row", sw), (sw, F32, "acc", sw)],
        rows=t, tm=_row_tile(24 * sw, t), name=prefix + "_glu_bwd")
    dyg2 = _mm_nt(dpre, p["ssm_glu_w"], out_dtype=F32, tm=1024, tn=1024, tko=1024, name=prefix + "_mm_dyg")
    g["ssm_glu_w"] = _mm_tn(sv["yg"], dpre, p["ssm_glu_w"], into=gbuf.get("ssm_glu_w"), tm=4096, tn=1024, tko=512,
                            name=prefix + "_mm_dw_glu")

    def gelu_bwd(y_raw, d1, d2):
        _, vjp = jax.vjp(jax.nn.gelu, y_raw)
        return vjp(d1 + d2)[0]

    dy_raw, = _rowmap(gelu_bwd, [(sv["y_raw"], "row", sw, 0), (dyg_direct, "row", sw, 0), (dyg2, "row", sw, 0)],
                      [(sw, F32, "row", sw)], rows=t, tm=_row_tile(20 * sw, t), name=prefix + "_gelu_bwd")
    sp = sv["sp"]
    du, dbr, dbi, dcr, dci, dar, dai, g["ssm_d"] = _ssm_bwd(
        z, dy_raw, sv["s0r"], sv["s0i"], sp["ar"], sp["ai"], sp["bblk_r"], sp["bblk_i"], sp["cblk_r"], sp["cblk_i"],
        p["ssm_d"], u_col=off_u, tc=dims["tc"], name=prefix + "_ssm_bwd")
    ngr, pst = p["lam_re"].shape
    hch = SSM_GROUP_CH
    dlr, dli, dldt, dbrt, dbit = _ssm_prep_bwd(
        p["lam_re"], p["lam_im"], p["log_dt"], p["b_re_t"], p["b_im_t"],
        dar.reshape(ngr, pst), dai.reshape(ngr, pst), _blockdiag_extract(dbr, hch, pst), _blockdiag_extract(dbi, hch, pst),
        name=prefix + "_ssm_prep_bwd")
    g.update(ssm_lambda_re=dlr, ssm_lambda_im=dli, ssm_log_dt=dldt.reshape(ngr),
             ssm_b_re=jnp.swapaxes(dbrt, 1, 2), ssm_b_im=jnp.swapaxes(dbit, 1, 2),
             ssm_c_re=jnp.swapaxes(_blockdiag_extract(dcr, pst, hch), 1, 2),
             ssm_c_im=jnp.swapaxes(_blockdiag_extract(dci, pst, hch), 1, 2))

    dq, dkv, g["q_norm_g"], g["k_norm_g"], g["attn_sinks"] = _attn_bwd(
        z, dya, p["q_norm_g"], p["k_norm_g"], p["attn_sinks"], n_q=dims["n_q"], n_kv=dims["n_kv"], name=prefix + "_attn_bwd")

    dz = jnp.concatenate([dq, dkv, du, dza, dzs], axis=1)
    deps = before_in(dq) if before_in else ()
    dh = _mm_nt(dz, p["w_in"], out_dtype=F32, tm=512, tn=1664, tko=2048, name=prefix + "_mm_dh", deps=deps)
    g["w_in"] = _mm_tn(sv["h"], dz, p["w_in"], into=gbuf.get("w_in"), tm=4096, tn=1664, tko=512,
                       name=prefix + "_mm_dw_in")
    dx, dxb, g["norm_mix_g"] = _rowmap(
        norm_bwd, [(sv["x"], "row", d, 0), (p["norm_mix_g"], "vec", d, 0), (dh, "row", d, 0), (dx1, "row", d, 0)],
        [(d, F32, "row", d), (d, BF16, "row", d), (d, F32, "acc", d)],
        rows=t, tm=_row_tile(22 * d, t), name=prefix + "_norm_mix_bwd")
    return dx, dxb, g


def _loss_and_grad(y, target):
    t, d = y.shape

    def fn(yy, tt):
        e = yy - tt
        dy = e * (1.0 / d)
        return dy, dy, jnp.sum(e * e, keepdims=True).reshape(1, 1)

    dy, dyb, sq = _rowmap(fn, [(y, "row", d, 0), (target, "row", d, 0)],
                          [(d, F32, "row", d), (d, BF16, "row", d), (1, F32, "acc", 1)],
                          rows=t, tm=_row_tile(14 * d, t), name="loss")
    return sq, dy, dyb


def _local_step(x, target, n_layers, weights_of, dims, reducer_of=None):
    saved, params = [], []
    h = x
    for l in range(n_layers):
        p, deps = weights_of(l, h)
        params.append(p)
        h, sv = _layer_fwd(h, p, dims, "l%d" % l, deps)
        saved.append(sv)
    sq, dy, dyb = _loss_and_grad(h, target)
    small = [None] * n_layers
    big = [None] * n_layers
    red = None
    for l in reversed(range(n_layers)):
        if red is None:
            dy, dyb, g = _layer_bwd(dy, dyb, saved[l], params[l], dims, "l%d" % l, {})
        else:
            dy, dyb, g = _layer_bwd(dy, dyb, saved[l], params[l], dims, "l%d" % l, {}, deps=(red.start(),),
                                    before_mixer=red.halves_to_chips, before_in=red.chips_to_sibling)
            big[l + 1] = red.finish(dy)
        big[l] = {n: g.pop(n) for n in BIG_WEIGHTS}
        small[l] = g
        red = reducer_of(l, big[l]) if reducer_of else None
    if red is not None:
        red.start()
        red.halves_to_chips(dy)
        red.chips_to_sibling(dy)
        big[0] = red.finish(dy)
    return sq, dy, small, big


COL_SHARDED = ("w_in", "w_attn_branch", "w_ssm_branch", "w_ffn_in")
ROW_SHARDED = ("ssm_glu_w", "w_out", "w_ffn_out")
BIG_WEIGHTS = COL_SHARDED + ROW_SHARDED
WEIGHT_NAMES = ("norm_mix_g", "w_in", "gate_bias", "q_norm_g", "k_norm_g", "attn_sinks", "ssm_lambda_re",
                "ssm_lambda_im", "ssm_log_dt", "ssm_b_re", "ssm_b_im", "ssm_c_re", "ssm_c_im", "ssm_d", "ssm_glu_w",
                "ssm_glu_b", "w_attn_branch", "w_ssm_branch", "w_out", "norm_ffn_g", "w_ffn_in", "w_ffn_out")
SMALL_WEIGHTS = tuple(n for n in WEIGHT_NAMES if n not in BIG_WEIGHTS)


def _dims(d, big, tc):
    s, _, aw, _ = big["w_attn_branch"].shape
    sw = big["w_ssm_branch"].shape[2]
    in_w = big["w_in"].shape[3] * s
    kvw = (in_w - aw - sw - 2 * d) // 2
    ff = big["w_ffn_out"].shape[2] * s
    return dict(aw=aw, kvw=kvw, sw=sw, ff=ff, n_q=aw // HEAD_DIM, n_kv=kvw // HEAD_DIM, tc=tc)


def _layer_params(l, small, big):
    p = {n: _Weight(big[n], 0, "col") for n in COL_SHARDED}
    p.update({n: _Weight(big[n].reshape(1, 1, -1, big[n].shape[-1]), 0, "col") for n in ROW_SHARDED})
    for n in ("norm_mix_g", "gate_bias", "q_norm_g", "k_norm_g", "ssm_d", "ssm_glu_b", "norm_ffn_g"):
        p[n] = small[n][l][None]
    p["attn_sinks"] = small["attn_sinks"][l]
    p["lam_re"] = small["ssm_lambda_re"][l]
    p["lam_im"] = small["ssm_lambda_im"][l]
    p["log_dt"] = small["ssm_log_dt"][l][:, None]
    p["b_re_t"] = jnp.swapaxes(small["ssm_b_re"][l], 1, 2)
    p["b_im_t"] = jnp.swapaxes(small["ssm_b_im"][l], 1, 2)
    p["c_re"] = small["ssm_c_re"][l]
    p["c_im"] = small["ssm_c_im"][l]
    return p


_ANY = pl.BlockSpec(memory_space=pl.ANY)
_MESH_ID = pl.DeviceIdType.MESH


def _coords():
    return lax.axis_index("x"), lax.axis_index("y"), lax.axis_index("c")


def _remote(src, dst, send_sems, recv_sems, k, to):
    return pltpu.make_async_remote_copy(src_ref=src, dst_ref=dst, send_sem=send_sems.at[k], recv_sem=recv_sems.at[k],
                                        device_id=to, device_id_type=_MESH_ID)


def _comm_call(body, ins, out_shapes, n_remote, name, aliases=None, scratch=()):
    return pl.pallas_call(
        body,
        out_shape=tuple(out_shapes),
        in_specs=[_ANY] * len(ins),
        out_specs=tuple([_ANY] * len(out_shapes)),
        scratch_shapes=[pltpu.SemaphoreType.DMA((n_remote,)), pltpu.SemaphoreType.DMA((n_remote,))] + list(scratch),
        input_output_aliases=aliases or {},
        compiler_params=pltpu.CompilerParams(has_side_effects=True),
        name=name,
    )(*ins)


def _my_chip():
    return (2 * lax.axis_index("x") + lax.axis_index("y")).astype(jnp.int32).reshape(1)


def _my_core():
    return lax.axis_index("c").astype(jnp.int32).reshape(1)


def _cast_into_slot(w, layer, *, name):
    _, r, c = w.shape
    tm = _row_tile(12 * c, r)

    def body(me_ref, w_ref, o_ref):
        o_ref[...] = w_ref[...].astype(o_ref.dtype)

    return pl.pallas_call(
        body,
        out_shape=jax.ShapeDtypeStruct((N_CHIPS, 1, r, c), BF16),
        grid_spec=pltpu.PrefetchScalarGridSpec(
            num_scalar_prefetch=1,
            grid=(r // tm,),
            in_specs=[pl.BlockSpec((None, tm, c), lambda i, me: (layer, i, 0))],
            out_specs=pl.BlockSpec((None, None, tm, c), lambda i, me: (me[0], 0, i, 0)),
        ),
        compiler_params=_cparams("parallel"),
        name=name,
    )(_my_chip(), w)


def _allgather_weights(bufs, *, name):
    n = len(bufs)

    def body(*refs):
        outs = refs[n:2 * n]
        send_sems, recv_sems = refs[2 * n:]
        x, y, c = _coords()
        me = 2 * x + y
        chips = [(1 - x, y), (x, 1 - y), (1 - x, 1 - y)]
        sibling = (x, y, 1 - c)

        def half(i, slot, hc):
            rh = bufs[i].shape[2] // 2
            return outs[i].at[slot, :, pl.ds(hc * rh, rh), :]

        first = [_remote(half(i, me, c), half(i, me, c), send_sems, recv_sems, 6 * i + k, (px, py, c))
                 for i in range(n) for k, (px, py) in enumerate(chips)]
        for cp in first:
            cp.start()
        passed = []
        for k, (px, py) in enumerate(chips):
            for i in range(n):
                landed = half(i, 2 * px + py, c)
                _remote(landed, landed, send_sems, recv_sems, 6 * i + k, (px, py, c)).wait_recv()
                fw = _remote(landed, landed, send_sems, recv_sems, 6 * i + 3 + k, sibling)
                fw.start()
                passed.append(fw)
        for k, (px, py) in enumerate(chips):
            for i in range(n):
                other = half(i, 2 * px + py, 1 - c)
                _remote(other, other, send_sems, recv_sems, 6 * i + 3 + k, sibling).wait_recv()
        for cp in first + passed:
            cp.wait_send()

    outs = [jax.ShapeDtypeStruct(b.shape, b.dtype) for b in bufs]
    return _comm_call(body, bufs, outs, 6 * n, name, aliases={i: i for i in range(n)})


_HBM = pl.BlockSpec(memory_space=pltpu.HBM)
_SEM = pl.BlockSpec(memory_space=pltpu.SEMAPHORE)
_DATAFLOW = pltpu.SideEffectType.DATAFLOW_SIDE_EFFECTING


class _SplitExchange:
    def __init__(self, srcs, lands, build, n_copies, name):
        self.build, self.n, self.name = build, n_copies, name
        self.ns, self.nl = len(srcs), len(lands)
        self.bufs = [pltpu.with_memory_space_constraint(a, pltpu.HBM) for a in list(srcs) + list(lands)]

    def _copies(self, refs, send_sems, recv_sems):
        triples = self.build(refs[:self.ns], refs[self.ns:self.ns + self.nl])
        assert len(triples) == self.n
        return [pltpu.make_async_remote_copy(src_ref=s, dst_ref=d, send_sem=send_sems.at[k], recv_sem=recv_sems.at[k],
                                             device_id=to, device_id_type=_MESH_ID) for k, (s, d, to) in enumerate(triples)]

    def start(self, deps=()):
        nb = self.ns + self.nl

        def body(*refs):
            outs = refs[nb + len(deps):]
            for cp in self._copies(refs, outs[0], outs[1]):
                cp.start()
            outs[-1][...] = jnp.zeros_like(outs[-1])

        sems = pltpu.SemaphoreType.DMA((self.n,))
        res = pl.pallas_call(
            body,
            out_shape=(sems, sems, *[pltpu.HBM(b.shape, b.dtype) for b in self.bufs], jax.ShapeDtypeStruct((8, 128), F32)),
            in_specs=[_HBM] * nb + [_ANY] * len(deps),
            out_specs=(_SEM, _SEM, *[_HBM] * nb, pl.BlockSpec(memory_space=pltpu.VMEM)),
            input_output_aliases={i: 2 + i for i in range(nb)},
            compiler_params=pltpu.CompilerParams(has_side_effects=_DATAFLOW),
            name=self.name + "_start",
        )(*self.bufs, *deps)
        self.send_sems, self.recv_sems = res[0], res[1]
        self.bufs = list(res[2:2 + nb])
        return res[-1]

    def wait(self, after=()):
        nb = self.ns + self.nl

        def body(*refs):
            for cp in self._copies(refs, refs[nb], refs[nb + 1]):
                cp.wait_send()
                cp.wait_recv()

        res = pl.pallas_call(
            body,
            out_shape=tuple(pltpu.HBM(b.shape, b.dtype) for b in self.bufs),
            in_specs=[_HBM] * nb + [_SEM, _SEM] + [_ANY] * len(after),
            out_specs=tuple([_HBM] * nb),
            input_output_aliases={i: i for i in range(nb)},
            compiler_params=pltpu.CompilerParams(has_side_effects=_DATAFLOW),
            name=self.name + "_wait",
        )(*self.bufs, self.send_sems, self.recv_sems, *after)
        res = list(res)
        return res[:self.ns], res[self.ns:]


def _allgather_direct(bufs, *, name):
    n = len(bufs)

    def build(srcs, lands):
        x, y, c = _coords()
        me = 2 * x + y
        return [(srcs[i].at[me], srcs[i].at[me], (px, py, c))
                for i in range(n) for px, py in [(1 - x, y), (x, 1 - y), (1 - x, 1 - y)]]

    return _SplitExchange(bufs, [], build, 3 * n, name)


class _GradReducer:
    def __init__(self, grads, tag):
        self.tag = tag
        self.n = n = len(grads)
        self.flat = [g.reshape(g.shape[1:]) for g in grads]
        self.half_shape = [(g.shape[1], g.shape[2] // 2, g.shape[3]) for g in grads]
        theirs = [lax.empty(s, F32) for s in self.half_shape]

        def build(srcs, lands):
            x, y, c = _coords()
            return [(srcs[i].at[:, pl.ds((1 - c) * self.half_shape[i][1], self.half_shape[i][1]), :], lands[i],
                     (x, y, 1 - c)) for i in range(n)]

        self.ex = _SplitExchange(self.flat, theirs, build, n, tag + "_halves")

    def start(self):
        return self.ex.start()

    def halves_to_chips(self, after):
        n = self.n
        flat, theirs = self.ex.wait((after,))
        parts = [_add_own_half(f, t, name="%s_add_own_half_%d" % (self.tag, i)) for i, (f, t) in enumerate(zip(flat, theirs))]
        got = [lax.empty(p.shape, p.dtype) for p in parts]

        def build(srcs, lands):
            x, y, c = _coords()
            me = 2 * x + y
            return [(srcs[i].at[2 * px + py], lands[i].at[me], (px, py, c))
                    for i in range(n) for px, py in [(1 - x, y), (x, 1 - y), (1 - x, 1 - y)]]

        self.ex = _SplitExchange(parts, got, build, 3 * n, self.tag + "_chips")
        return (self.ex.start(),)

    def chips_to_sibling(self, after):
        n = self.n
        parts, got = self.ex.wait((after,))
        self.mine = [_sum_chips(p, g, name="%s_sum_chips_%d" % (self.tag, i)) for i, (p, g) in enumerate(zip(parts, got))]
        sib = [lax.empty(m.shape, m.dtype) for m in self.mine]

        def build(srcs, lands):
            x, y, c = _coords()
            return [(srcs[i], lands[i], (x, y, 1 - c)) for i in range(n)]

        self.ex = _SplitExchange(self.mine, sib, build, n, self.tag + "_sibling")
        return (self.ex.start(),)

    def finish(self, after):
        mine, sib = self.ex.wait((after,))
        return list(zip(mine, sib))


def _add_own_half(g, theirs, *, name):
    s, r, c = g.shape
    rh = r // 2
    tm = _row_tile(10 * c, rh)
    nb = rh // tm

    def body(core_ref, g_ref, t_ref, o_ref):
        o_ref[...] = (g_ref[...] + t_ref[...]).astype(o_ref.dtype)

    return pl.pallas_call(
        body,
        out_shape=jax.ShapeDtypeStruct((s, rh, c), BF16),
        grid_spec=pltpu.PrefetchScalarGridSpec(
            num_scalar_prefetch=1,
            grid=(s, nb),
            in_specs=[pl.BlockSpec((None, tm, c), lambda k, i, core: (k, core[0] * nb + i, 0)),
                      pl.BlockSpec((None, tm, c), lambda k, i, core: (k, i, 0))],
            out_specs=pl.BlockSpec((None, tm, c), lambda k, i, core: (k, i, 0)),
        ),
        compiler_params=_cparams("parallel", "parallel"),
        name=name,
    )(_my_core(), g, theirs)


def _sum_chips(part, got, *, name):
    s, rh, c = part.shape
    tm = _row_tile(14 * c, rh)

    def body(me_ref, p_ref, a_ref, b_ref, c_ref, o_ref):
        o_ref[...] = ((p_ref[...].astype(F32) + a_ref[...].astype(F32)) + b_ref[...].astype(F32)) + c_ref[...].astype(F32)

    slot = lambda k: (lambda i, me: ((me[0] + k) % s, i, 0))
    return pl.pallas_call(
        body,
        out_shape=jax.ShapeDtypeStruct((rh, c), F32),
        grid_spec=pltpu.PrefetchScalarGridSpec(
            num_scalar_prefetch=1,
            grid=(rh // tm,),
            in_specs=[pl.BlockSpec((None, tm, c), slot(k)) for k in range(s)],
            out_specs=pl.BlockSpec((tm, c), lambda i, me: (i, 0)),
        ),
        compiler_params=_cparams("parallel"),
        name=name,
    )(_my_chip(), part, got, got, got)


def _allgather_all(buf, *, name):
    def body(in_ref, out_ref, send_sems, recv_sems, stage_ref, local_sems):
        x, y, c = _coords()
        flip = lambda v, b: (1 - v) if b else v
        peers = [(flip(x, r & 4), flip(y, r & 2), flip(c, r & 1)) for r in range(1, N_DEV)]
        slot_of = lambda p: 4 * p[0] + 2 * p[1] + p[2]
        mine = out_ref.at[slot_of((x, y, c))]
        sends = [_remote(in_ref, mine, send_sems, recv_sems, k, p) for k, p in enumerate(peers)]
        for cp in sends:
            cp.start()
        load = pltpu.make_async_copy(in_ref, stage_ref, local_sems.at[0])
        load.start()
        load.wait()
        store = pltpu.make_async_copy(stage_ref, mine, local_sems.at[1])
        store.start()
        for k, p in enumerate(peers):
            slot = out_ref.at[slot_of(p)]
            _remote(slot, slot, send_sems, recv_sems, k, p).wait_recv()
        for cp in sends:
            cp.wait_send()
        store.wait()

    return _comm_call(body, [buf], [jax.ShapeDtypeStruct((N_DEV,) + buf.shape, buf.dtype)], N_DEV - 1, name,
                      scratch=[pltpu.VMEM(buf.shape, buf.dtype), pltpu.SemaphoreType.DMA((2,))])[0]


def _sum_slots(arr, *, name):
    s, r, c = arr.shape
    tm = _row_tile(4 * c * (s + 1), r)

    def body(*refs):
        acc = refs[0][...]
        for ref in refs[1:s]:
            acc = acc + ref[...]
        refs[s][...] = acc

    return pl.pallas_call(
        body,
        out_shape=jax.ShapeDtypeStruct((r, c), arr.dtype),
        grid=(r // tm,),
        in_specs=[pl.BlockSpec((None, tm, c), lambda i, k=k: (k, i, 0)) for k in range(s)],
        out_specs=pl.BlockSpec((tm, c), lambda i: (i, 0)),
        compiler_params=_cparams("parallel"),
        name=name,
    )(*([arr] * s))


def _adamw_fn(w, g, m, v):
    m = ADAM_B1 * m + (1.0 - ADAM_B1) * g
    v = ADAM_B2 * v + (1.0 - ADAM_B2) * jnp.square(g)
    m_hat = m / (1.0 - ADAM_B1 ** ADAM_STEP)
    v_hat = v / (1.0 - ADAM_B2 ** ADAM_STEP)
    delta = -ADAM_LR * (m_hat / (jnp.sqrt(v_hat) + ADAM_EPS) + ADAM_WD * w)
    return delta, m, v


def _adamw(w, g, m, v, *, name):
    rows, cols = w.shape
    ins = [(a, "row", cols, 0) for a in (w, g, m, v)]
    outs = [(cols, F32, "row", cols)] * 3
    return _rowmap(_adamw_fn, ins, outs, rows=rows, tm=_row_tile(56 * cols, rows), name=name)


def _adamw_sharded(w, m, v, g_mine, g_sibling, layer, into, *, name):
    nl, r, c = w.shape
    rh = r // 2
    tm = _row_tile(40 * c, rh)
    nb = rh // tm
    n_into = 0 if into is None else 4

    def body(core_ref, w_ref, m_ref, v_ref, a_ref, b_ref, *rest):
        g_ref, d_ref, nm_ref, nv_ref = rest[n_into:]
        g = jnp.where(pl.program_id(0) == core_ref[0], a_ref[...], b_ref[...])
        delta, nm, nv = _adamw_fn(w_ref[...], g, m_ref[...], v_ref[...])
        g_ref[...] = g
        d_ref[...] = delta
        nm_ref[...] = nm
        nv_ref[...] = nv

    whole = pl.BlockSpec((None, tm, c), lambda h, i, core: (layer, h * nb + i, 0))
    half = pl.BlockSpec((tm, c), lambda h, i, core: (i, 0))
    shape = jax.ShapeDtypeStruct((nl, r, c), F32)
    return pl.pallas_call(
        body,
        out_shape=(shape, shape, shape, shape),
        grid_spec=pltpu.PrefetchScalarGridSpec(
            num_scalar_prefetch=1,
            grid=(2, nb),
            in_specs=[whole, whole, whole, half, half] + [pl.BlockSpec(memory_space=pl.ANY)] * n_into,
            out_specs=(whole, whole, whole, whole),
        ),
        input_output_aliases={6 + k: k for k in range(n_into)},
        compiler_params=_cparams("parallel", "parallel"),
        name=name,
    )(_my_core(), w, m, v, g_mine, g_sibling, *(into or ()))


def _pack(arrays):
    flat = jnp.concatenate([a.reshape(-1) for a in arrays])
    pad = (-flat.shape[0]) % (256 * 128)
    return jnp.pad(flat, (0, pad)).reshape(-1, 128)


def _unpack(buf, shapes):
    flat = buf.reshape(-1)
    out, off = [], 0
    for s in shapes:
        n = math.prod(s)
        out.append(flat[off:off + n].reshape(s))
        off += n
    return out


def kernel(x, norm_mix_g, w_in, gate_bias, q_norm_g, k_norm_g, attn_sinks, ssm_lambda_re, ssm_lambda_im, ssm_log_dt, ssm_b_re, ssm_b_im, ssm_c_re, ssm_c_im, ssm_d, ssm_glu_w, ssm_glu_b, w_attn_branch, w_ssm_branch, w_out, norm_ffn_g, w_ffn_in, w_ffn_out, loss_target, m_norm_mix_g, m_w_in, m_gate_bias, m_q_norm_g, m_k_norm_g, m_attn_sinks, m_ssm_lambda_re, m_ssm_lambda_im, m_ssm_log_dt, m_ssm_b_re, m_ssm_b_im, m_ssm_c_re, m_ssm_c_im, m_ssm_d, m_ssm_glu_w, m_ssm_glu_b, m_w_attn_branch, m_w_ssm_branch, m_w_out, m_norm_ffn_g, m_w_ffn_in, m_w_ffn_out, v_norm_mix_g, v_w_in, v_gate_bias, v_q_norm_g, v_k_norm_g, v_attn_sinks, v_ssm_lambda_re, v_ssm_lambda_im, v_ssm_log_dt, v_ssm_b_re, v_ssm_b_im, v_ssm_c_re, v_ssm_c_im, v_ssm_d, v_ssm_glu_w, v_ssm_glu_b, v_w_attn_branch, v_w_ssm_branch, v_w_out, v_norm_ffn_g, v_w_ffn_in, v_w_ffn_out):
    w = dict(norm_mix_g=norm_mix_g, w_in=w_in, gate_bias=gate_bias, q_norm_g=q_norm_g, k_norm_g=k_norm_g,
             attn_sinks=attn_sinks, ssm_lambda_re=ssm_lambda_re, ssm_lambda_im=ssm_lambda_im, ssm_log_dt=ssm_log_dt,
             ssm_b_re=ssm_b_re, ssm_b_im=ssm_b_im, ssm_c_re=ssm_c_re, ssm_c_im=ssm_c_im, ssm_d=ssm_d,
             ssm_glu_w=ssm_glu_w, ssm_glu_b=ssm_glu_b, w_attn_branch=w_attn_branch, w_ssm_branch=w_ssm_branch,
             w_out=w_out, norm_ffn_g=norm_ffn_g, w_ffn_in=w_ffn_in, w_ffn_out=w_ffn_out)
    m = dict(norm_mix_g=m_norm_mix_g, w_in=m_w_in, gate_bias=m_gate_bias, q_norm_g=m_q_norm_g, k_norm_g=m_k_norm_g,
             attn_sinks=m_attn_sinks, ssm_lambda_re=m_ssm_lambda_re, ssm_lambda_im=m_ssm_lambda_im,
             ssm_log_dt=m_ssm_log_dt, ssm_b_re=m_ssm_b_re, ssm_b_im=m_ssm_b_im, ssm_c_re=m_ssm_c_re,
             ssm_c_im=m_ssm_c_im, ssm_d=m_ssm_d, ssm_glu_w=m_ssm_glu_w, ssm_glu_b=m_ssm_glu_b,
             w_attn_branch=m_w_attn_branch, w_ssm_branch=m_w_ssm_branch, w_out=m_w_out, norm_ffn_g=m_norm_ffn_g,
             w_ffn_in=m_w_ffn_in, w_ffn_out=m_w_ffn_out)
    v = dict(norm_mix_g=v_norm_mix_g, w_in=v_w_in, gate_bias=v_gate_bias, q_norm_g=v_q_norm_g, k_norm_g=v_k_norm_g,
             attn_sinks=v_attn_sinks, ssm_lambda_re=v_ssm_lambda_re, ssm_lambda_im=v_ssm_lambda_im,
             ssm_log_dt=v_ssm_log_dt, ssm_b_re=v_ssm_b_re, ssm_b_im=v_ssm_b_im, ssm_c_re=v_ssm_c_re,
             ssm_c_im=v_ssm_c_im, ssm_d=v_ssm_d, ssm_glu_w=v_ssm_glu_w, ssm_glu_b=v_ssm_glu_b,
             w_attn_branch=v_w_attn_branch, w_ssm_branch=v_w_ssm_branch, w_out=v_w_out, norm_ffn_g=v_norm_ffn_g,
             w_ffn_in=v_w_ffn_in, w_ffn_out=v_w_ffn_out)
    n_layers = norm_mix_g.shape[0]
    d_model = x.shape[-1]
    seq = x.shape[1]

    first = _allgather_weights([_cast_into_slot(w[n], 0, name="cast0_" + n) for n in BIG_WEIGHTS],
                               name="allgather_weights0")
    later = {}
    for l in range(1, n_layers):
        ex = _allgather_direct([_cast_into_slot(w[n], l, name="cast%d_%s" % (l, n)) for n in BIG_WEIGHTS],
                               name="allgather_weights%d" % l)
        later[l] = (ex, ex.start((first[0],)))
    dims = _dims(d_model, dict(zip(BIG_WEIGHTS, first)), min(512, seq))

    def weights_of(l, h):
        if l == 0:
            deps = tuple(token for _, token in later.values())
            return _layer_params(0, w, dict(zip(BIG_WEIGHTS, first))), deps
        bufs, _ = later[l][0].wait((h,))
        return _layer_params(l, w, dict(zip(BIG_WEIGHTS, bufs))), ()

    def reducer_of(l, big_grads):
        return _GradReducer([big_grads[n].reshape(1, N_CHIPS, -1, big_grads[n].shape[-1]) for n in BIG_WEIGHTS],
                            "rs%d" % l)

    sq, dx, grads, reduced = _local_step(x[0], loss_target[0], n_layers, weights_of, dims, reducer_of)
    loss = lax.psum(sq[0, 0], MESH_AXES) * (0.5 / d_model)

    grad, delta, new_m, new_v = {}, {}, {}, {}
    for k, n in enumerate(BIG_WEIGHTS):
        outs = None
        for l in reversed(range(n_layers)):
            mine, sibling = reduced[l][k]
            outs = _adamw_sharded(w[n], m[n], v[n], mine, sibling, l, outs, name="adamw%d_%s" % (l, n))
        grad[n], delta[n], new_m[n], new_v[n] = outs
    small_local = [jnp.stack([grads[l][n].reshape(w[n].shape[1:]) for l in range(n_layers)]) for n in SMALL_WEIGHTS]
    small_sum = _sum_slots(_allgather_all(_pack(small_local), name="allgather_small_grads"), name="sum_small_grads")
    grad.update(zip(SMALL_WEIGHTS, _unpack(small_sum, [w[n].shape for n in SMALL_WEIGHTS])))
    small_shapes = [w[n].shape for n in SMALL_WEIGHTS]
    res = _adamw(_pack([w[n] for n in SMALL_WEIGHTS]), small_sum, _pack([m[n] for n in SMALL_WEIGHTS]),
                 _pack([v[n] for n in SMALL_WEIGHTS]), name="adamw_small")
    for out, packed in zip((delta, new_m, new_v), res):
        out.update(zip(SMALL_WEIGHTS, _unpack(packed, small_shapes)))

    return (loss, dx[None], *[grad[n] for n in WEIGHT_NAMES], *[delta[n] for n in WEIGHT_NAMES],
            *[new_m[n] for n in WEIGHT_NAMES], *[new_v[n] for n in WEIGHT_NAMES])
```

```python
import functools
import math

import jax
import jax.numpy as jnp
from jax import lax
from jax.experimental import pallas as pl
from jax.experimental.pallas import tpu as pltpu

HEAD_DIM = 64
WINDOW = 128
SSM_GROUP_CH = 16
SSM_LANE_GROUPS = 8
RMS_EPS = 1e-6
ADAM_LR = 0.001
ADAM_B1 = 0.9
ADAM_B2 = 0.999
ADAM_EPS = 1e-08
ADAM_WD = 0.01
ADAM_STEP = 10
NEG_BIG = -1e30
MESH_AXES = ("x", "y", "c")
N_CHIPS = 4
N_DEV = 8
VMEM_LIMIT_BYTES = 56 * 1024 * 1024
BF16 = jnp.bfloat16
F32 = jnp.float32


def _cparams(*semantics):
    return pltpu.CompilerParams(dimension_semantics=semantics, vmem_limit_bytes=VMEM_LIMIT_BYTES)


def _pick(n, target, mult):
    if n <= target:
        return n
    best = None
    for d in range(mult, target + 1, mult):
        if n % d == 0:
            best = d
    assert best is not None, (n, target, mult)
    return best


def _rowmap(fn, ins, outs, *, rows, tm, ncol=1, name, deps=()):
    n_in = len(ins)
    nrow = rows // tm
    assert nrow * tm == rows

    in_specs = []
    for arr, kind, width, coloff in ins:
        if kind == "row":
            in_specs.append(pl.BlockSpec((tm, width), lambda j, i, o=coloff: (i, o + j)))
        elif kind == "vec":
            in_specs.append(pl.BlockSpec((1, width), lambda j, i, o=coloff: (0, o + j)))
        else:
            nd = arr.ndim
            in_specs.append(pl.BlockSpec(arr.shape, lambda j, i, nd=nd: (0,) * nd))
    out_specs, out_shapes = [], []
    for cols, dtype, kind, width in outs:
        if kind == "row":
            out_specs.append(pl.BlockSpec((tm, width), lambda j, i: (i, j)))
            out_shapes.append(jax.ShapeDtypeStruct((rows, cols), dtype))
        else:
            out_specs.append(pl.BlockSpec((1, width), lambda j, i: (0, j)))
            out_shapes.append(jax.ShapeDtypeStruct((1, cols), dtype))

    in_specs += [pl.BlockSpec(memory_space=pl.ANY)] * len(deps)

    def body(*refs):
        i = pl.program_id(1)
        res = fn(*[r[...] for r in refs[:n_in]])
        if not isinstance(res, (tuple, list)):
            res = (res,)
        for (cols, dtype, kind, width), ref, val in zip(outs, refs[n_in + len(deps):], res):
            if kind == "row":
                ref[...] = val.astype(ref.dtype)
            else:
                @pl.when(i == 0)
                def _():
                    ref[...] = jnp.zeros_like(ref)
                ref[...] += val.astype(ref.dtype)

    res = pl.pallas_call(
        body,
        out_shape=tuple(out_shapes),
        grid=(ncol, nrow),
        in_specs=in_specs,
        out_specs=tuple(out_specs),
        compiler_params=_cparams("parallel", "arbitrary"),
        name=name,
    )(*[a[0] for a in ins], *deps)
    return res


def _mm_body(dims, nk, has_add, unused_in=0):
    def body(*refs):
        if has_add:
            a_ref, b_ref, add_ref = refs[:3]
            o_ref = refs[3 + unused_in]
            rest = refs[4 + unused_in:]
        else:
            a_ref, b_ref = refs[:2]
            o_ref = refs[2 + unused_in]
            add_ref = None
            rest = refs[3 + unused_in:]
        part = lax.dot_general(a_ref[...], b_ref[...], (dims, ((), ())), preferred_element_type=F32)
        if nk == 1:
            if add_ref is not None:
                part = part + add_ref[...]
            o_ref[...] = part.astype(o_ref.dtype)
        else:
            acc_ref = rest[0]
            k = pl.program_id(2)

            @pl.when(k == 0)
            def _():
                acc_ref[...] = part

            @pl.when(k > 0)
            def _():
                acc_ref[...] += part

            @pl.when(k == nk - 1)
            def _():
                r = acc_ref[...]
                if add_ref is not None:
                    r = r + add_ref[...]
                o_ref[...] = r.astype(o_ref.dtype)
    return body


class _Weight:
    def __init__(self, arr, layer, kind):
        self.arr, self.layer, self.kind = arr, layer, kind
        self.s, _, self.r, self.c = arr.shape
        self.rows = self.r * (self.s if kind == "row" else 1)
        self.cols = self.c * (self.s if kind == "col" else 1)

    def tiles(self, tr, tc):
        return _pick(self.r, tr, 128), _pick(self.c, tc, 128)

    def index(self, tr, tc):
        layer = self.layer
        if self.kind == "col":
            per = self.c // tc
            return lambda rb, cb: (cb // per, layer, rb, cb % per)
        per = self.r // tr
        return lambda rb, cb: (rb // per, layer, rb % per, cb)


def _shard_index(kind, r, c, tr, tc):
    if kind == "col":
        per = c // tc
        return lambda rb, cb: (cb // per, rb, cb % per)
    per = r // tr
    return lambda rb, cb: (rb // per, rb % per, cb)


def _mm_nn(a, w, *, out_dtype, tm, tn, tk, name, add=None, deps=()):
    m, k = a.shape
    assert k == w.rows
    tm = _pick(m, tm, 16)
    tk, tn = w.tiles(tk, tn)
    nk = k // tk
    widx = w.index(tk, tn)
    in_specs = [pl.BlockSpec((tm, tk), lambda n, i, kk: (i, kk)),
                pl.BlockSpec((None, None, tk, tn), lambda n, i, kk: widx(kk, n))]
    args = [a, w.arr]
    if add is not None:
        in_specs.append(pl.BlockSpec((tm, tn), lambda n, i, kk: (i, n)))
        args.append(add)
    in_specs += [pl.BlockSpec(memory_space=pl.ANY)] * len(deps)
    args += list(deps)
    return pl.pallas_call(
        _mm_body(((1,), (0,)), nk, add is not None, unused_in=len(deps)),
        out_shape=jax.ShapeDtypeStruct((m, w.cols), out_dtype),
        grid=(w.cols // tn, m // tm, nk),
        in_specs=in_specs,
        out_specs=pl.BlockSpec((tm, tn), lambda n, i, kk: (i, n)),
        scratch_shapes=[pltpu.VMEM((tm, tn), F32)] if nk > 1 else [],
        compiler_params=_cparams("parallel", "parallel", "arbitrary"),
        name=name,
    )(*args)


def _mm_nt(a, w, *, out_dtype, tm, tn, tko, name, deps=()):
    m, n = a.shape
    assert n == w.cols
    tm = _pick(m, tm, 16)
    tko, tn = w.tiles(tko, tn)
    nk = n // tn
    widx = w.index(tko, tn)
    return pl.pallas_call(
        _mm_body(((1,), (1,)), nk, False, unused_in=len(deps)),
        out_shape=jax.ShapeDtypeStruct((m, w.rows), out_dtype),
        grid=(w.rows // tko, m // tm, nk),
        in_specs=[pl.BlockSpec((tm, tn), lambda ko, i, nn: (i, nn)),
                  pl.BlockSpec((None, None, tko, tn), lambda ko, i, nn: widx(ko, nn))]
        + [pl.BlockSpec(memory_space=pl.ANY)] * len(deps),
        out_specs=pl.BlockSpec((tm, tko), lambda ko, i, nn: (i, ko)),
        scratch_shapes=[pltpu.VMEM((tm, tko), F32)] if nk > 1 else [],
        compiler_params=_cparams("parallel", "parallel", "arbitrary"),
        name=name,
    )(a, w.arr, *deps)


def _mm_tn(a, c, w, *, into, tm, tn, tko, name):
    m, k = a.shape
    mc, n = c.shape
    assert mc == m and k == w.rows and n == w.cols
    tm = _pick(m, tm, 16)
    tko, tn = w.tiles(tko, tn)
    nk = m // tm
    oidx = _shard_index(w.kind, w.r, w.c, tko, tn)
    layer = w.layer
    in_specs = [pl.BlockSpec((tm, tko), lambda ko, nn, mm: (mm, ko)),
                pl.BlockSpec((tm, tn), lambda ko, nn, mm: (mm, nn))]
    args = [a, c]
    if into is not None:
        in_specs.append(pl.BlockSpec(memory_space=pl.ANY))
        args.append(into)
    return pl.pallas_call(
        _mm_body(((0,), (0,)), nk, False, unused_in=len(args) - 2),
        out_shape=jax.ShapeDtypeStruct((w.arr.shape[1], w.s, w.r, w.c), F32),
        grid=(k // tko, n // tn, nk),
        in_specs=in_specs,
        out_specs=pl.BlockSpec((None, None, tko, tn), lambda ko, nn, mm: (layer,) + oidx(ko, nn)),
        scratch_shapes=[pltpu.VMEM((tko, tn), F32)] if nk > 1 else [],
        input_output_aliases={2: 0} if into is not None else {},
        compiler_params=_cparams("parallel", "parallel", "arbitrary"),
        name=name,
    )(*args)


def _rms(x, g):
    r = lax.rsqrt(jnp.mean(x * x, axis=-1, keepdims=True) + RMS_EPS)
    return x * r * g, r


def _rms_bwd(x, r, g, dy):
    dg = jnp.sum(dy * x * r, axis=0, keepdims=True)
    t = dy * g
    dx = r * t - x * (r * r * r) * jnp.mean(t * x, axis=-1, keepdims=True)
    return dx, dg


def _attn_consts(n_q, n_kv, sinks):
    group = n_q // n_kv
    t = jnp.arange(WINDOW, dtype=jnp.int32)[:, None]
    s = jnp.arange(2 * WINDOW, dtype=jnp.int32)[None, :] - WINDOW
    dist = (t - s).astype(F32)
    valid = (dist >= 0) & (dist < WINDOW)
    slopes = jnp.exp2(-8.0 * jnp.arange(1, n_q + 1, dtype=F32) / n_q)
    bias = jnp.where(valid[None], -slopes[:, None, None] * dist[None], NEG_BIG)
    bias = bias.reshape(n_kv, group * WINDOW, 2 * WINDOW)
    sink = jnp.broadcast_to(sinks.astype(F32).reshape(n_kv, group, 1, 1), (n_kv, group, WINDOW, 1))
    return bias, sink.reshape(n_kv, group * WINDOW, 1)


def _attn_probs(q_ref, kc_ref, kp_ref, vc_ref, vp_ref, qg, kg, sink, bias, first_mask, kv, group):
    sl = slice(kv * HEAD_DIM, (kv + 1) * HEAD_DIM)
    k2 = jnp.concatenate([kp_ref[:, sl], kc_ref[:, sl]], axis=0)
    v2 = jnp.concatenate([vp_ref[:, sl], vc_ref[:, sl]], axis=0)
    k2n, rk = _rms(k2, kg)
    qx, qn, rq = [], [], []
    for g in range(group):
        h = kv * group + g
        x = q_ref[:, h * HEAD_DIM:(h + 1) * HEAD_DIM]
        y, r = _rms(x, qg)
        qx.append(x); qn.append(y); rq.append(r)
    qs = jnp.concatenate(qn, axis=0).astype(BF16)
    k2b = k2n.astype(BF16)
    s = lax.dot_general(qs, k2b, (((1,), (1,)), ((), ())), preferred_element_type=F32) * (HEAD_DIM ** -0.5)
    s = jnp.where(first_mask, NEG_BIG, s + bias)
    m = jnp.maximum(jnp.max(s, axis=-1, keepdims=True), sink)
    p = jnp.exp(s - m)
    esink = jnp.exp(sink - m)
    denom = jnp.sum(p, axis=-1, keepdims=True) + esink
    pn = p / denom
    return dict(k2=k2, rk=rk, k2b=k2b, v2b=v2.astype(BF16), qx=qx, rq=rq, qs=qs, pn=pn, psink=esink / denom)


def _attn_specs(n_q, n_kv):
    aw, kvw = n_q * HEAD_DIM, n_kv * HEAD_DIM
    group = n_q // n_kv
    kblk, vblk = aw // kvw, aw // kvw + 1

    def specs(nb):
        cur = lambda n: jnp.minimum(n, nb - 1)
        prev = lambda n: jnp.maximum(jnp.minimum(n, nb - 1) - 1, 0)
        return [
            pl.BlockSpec((WINDOW, aw), lambda n: (cur(n), 0)),
            pl.BlockSpec((WINDOW, kvw), lambda n: (cur(n), kblk)),
            pl.BlockSpec((WINDOW, kvw), lambda n: (prev(n), kblk)),
            pl.BlockSpec((WINDOW, kvw), lambda n: (cur(n), vblk)),
            pl.BlockSpec((WINDOW, kvw), lambda n: (prev(n), vblk)),
        ]
    const_specs = [
        pl.BlockSpec((1, HEAD_DIM), lambda n: (0, 0)),
        pl.BlockSpec((1, HEAD_DIM), lambda n: (0, 0)),
        pl.BlockSpec((n_kv, group * WINDOW, 1), lambda n: (0, 0, 0)),
        pl.BlockSpec((n_kv, group * WINDOW, 2 * WINDOW), lambda n: (0, 0, 0)),
    ]
    return specs, const_specs


def _attn_fwd(z, qg, kg, sinks, *, n_q, n_kv, name, deps=()):
    L = z.shape[0]
    nb = L // WINDOW
    aw = n_q * HEAD_DIM
    group = n_q // n_kv
    bias, sink = _attn_consts(n_q, n_kv, sinks)
    specs, const_specs = _attn_specs(n_q, n_kv)

    def body(q_ref, kc_ref, kp_ref, vc_ref, vp_ref, qg_ref, kg_ref, sink_ref, bias_ref, *rest):
        o_ref = rest[-1]
        n = pl.program_id(0)
        col = lax.broadcasted_iota(jnp.int32, (group * WINDOW, 2 * WINDOW), 1)
        first_mask = jnp.logical_and(n == 0, col < WINDOW)
        for kv in range(n_kv):
            a = _attn_probs(q_ref, kc_ref, kp_ref, vc_ref, vp_ref, qg_ref[...], kg_ref[...],
                            sink_ref[kv], bias_ref[kv], first_mask, kv, group)
            o = jnp.dot(a["pn"].astype(BF16), a["v2b"], preferred_element_type=F32)
            for g in range(group):
                h = kv * group + g
                o_ref[:, h * HEAD_DIM:(h + 1) * HEAD_DIM] = o[g * WINDOW:(g + 1) * WINDOW].astype(o_ref.dtype)

    return pl.pallas_call(
        body,
        out_shape=jax.ShapeDtypeStruct((L, aw), BF16),
        grid=(nb,),
        in_specs=specs(nb) + const_specs + [pl.BlockSpec(memory_space=pl.ANY)] * len(deps),
        out_specs=pl.BlockSpec((WINDOW, aw), lambda n: (n, 0)),
        compiler_params=_cparams("parallel"),
        name=name,
    )(z, z, z, z, z, qg, kg, sink, bias, *deps)


def _attn_bwd(z, do, qg, kg, sinks, *, n_q, n_kv, name):
    L = z.shape[0]
    nb = L // WINDOW
    aw, kvw = n_q * HEAD_DIM, n_kv * HEAD_DIM
    group = n_q // n_kv
    bias, sink = _attn_consts(n_q, n_kv, sinks)
    specs, const_specs = _attn_specs(n_q, n_kv)
    scale = HEAD_DIM ** -0.5

    def body(q_ref, kc_ref, kp_ref, vc_ref, vp_ref, do_ref, qg_ref, kg_ref, sink_ref, bias_ref,
             dq_ref, dkv_ref, dqg_ref, dkg_ref, dsink_ref, carry_ref):
        n = pl.program_id(0)

        @pl.when(n == 0)
        def _():
            dqg_ref[...] = jnp.zeros_like(dqg_ref)
            dkg_ref[...] = jnp.zeros_like(dkg_ref)
            dsink_ref[...] = jnp.zeros_like(dsink_ref)
            carry_ref[...] = jnp.zeros_like(carry_ref)

        @pl.when(n < nb)
        def _():
            col = lax.broadcasted_iota(jnp.int32, (group * WINDOW, 2 * WINDOW), 1)
            first_mask = jnp.logical_and(n == 0, col < WINDOW)
            head_lane = lax.broadcasted_iota(jnp.int32, (1, n_q), 1)
            qg, kg = qg_ref[...], kg_ref[...]
            dqg = jnp.zeros((1, HEAD_DIM), F32)
            dkg = jnp.zeros((1, HEAD_DIM), F32)
            dsink = jnp.zeros((1, n_q), F32)
            for kv in range(n_kv):
                a = _attn_probs(q_ref, kc_ref, kp_ref, vc_ref, vp_ref, qg, kg,
                                sink_ref[kv], bias_ref[kv], first_mask, kv, group)
                pn = a["pn"]
                dos = jnp.concatenate(
                    [do_ref[:, (kv * group + g) * HEAD_DIM:(kv * group + g + 1) * HEAD_DIM] for g in range(group)],
                    axis=0).astype(BF16)
                dpn = lax.dot_general(dos, a["v2b"], (((1,), (1,)), ((), ())), preferred_element_type=F32)
                dv2 = lax.dot_general(pn.astype(BF16), dos, (((0,), (0,)), ((), ())), preferred_element_type=F32)
                delta = jnp.sum(pn * dpn, axis=-1, keepdims=True)
                ds = (pn * (dpn - delta)).astype(BF16)
                dsk = -a["psink"] * delta
                dqn = lax.dot_general(ds, a["k2b"], (((1,), (0,)), ((), ())), preferred_element_type=F32) * scale
                dk2n = lax.dot_general(ds, a["qs"], (((0,), (0,)), ((), ())), preferred_element_type=F32) * scale
                for g in range(group):
                    h = kv * group + g
                    rows = slice(g * WINDOW, (g + 1) * WINDOW)
                    dx, dgq = _rms_bwd(a["qx"][g], a["rq"][g], qg, dqn[rows])
                    dq_ref[:, h * HEAD_DIM:(h + 1) * HEAD_DIM] = dx.astype(dq_ref.dtype)
                    dqg = dqg + dgq
                    dsink = dsink + jnp.where(head_lane == h, jnp.sum(dsk[rows], axis=0, keepdims=True), 0.0)
                dk2, dgk = _rms_bwd(a["k2"], a["rk"], kg, dk2n)
                dkg = dkg + dgk
                ksl = slice(kv * HEAD_DIM, (kv + 1) * HEAD_DIM)
                vsl = slice(kvw + kv * HEAD_DIM, kvw + (kv + 1) * HEAD_DIM)
                dkv_ref[:, ksl] = (carry_ref[:, ksl] + dk2[:WINDOW]).astype(dkv_ref.dtype)
                dkv_ref[:, vsl] = (carry_ref[:, vsl] + dv2[:WINDOW]).astype(dkv_ref.dtype)
                carry_ref[:, ksl] = dk2[WINDOW:]
                carry_ref[:, vsl] = dv2[WINDOW:]
            dqg_ref[...] += dqg
            dkg_ref[...] += dkg
            dsink_ref[...] += dsink

        @pl.when(n == nb)
        def _():
            dkv_ref[...] = carry_ref[...].astype(dkv_ref.dtype)

    in_specs = specs(nb) + [pl.BlockSpec((WINDOW, aw), lambda n: (jnp.minimum(n, nb - 1), 0))] + const_specs
    return pl.pallas_call(
        body,
        out_shape=(jax.ShapeDtypeStruct((L, aw), BF16), jax.ShapeDtypeStruct((L, 2 * kvw), BF16),
                   jax.ShapeDtypeStruct((1, HEAD_DIM), F32), jax.ShapeDtypeStruct((1, HEAD_DIM), F32),
                   jax.ShapeDtypeStruct((1, n_q), F32)),
        grid=(nb + 1,),
        in_specs=in_specs,
        out_specs=(pl.BlockSpec((WINDOW, aw), lambda n: (jnp.minimum(n, nb - 1), 0)),
                   pl.BlockSpec((WINDOW, 2 * kvw), lambda n: (jnp.maximum(n - 1, 0), 0)),
                   pl.BlockSpec((1, HEAD_DIM), lambda n: (0, 0)),
                   pl.BlockSpec((1, HEAD_DIM), lambda n: (0, 0)),
                   pl.BlockSpec((1, n_q), lambda n: (0, 0))),
        scratch_shapes=[pltpu.VMEM((WINDOW, 2 * kvw), F32)],
        compiler_params=_cparams("arbitrary"),
        name=name,
    )(z, z, z, z, z, do, qg, kg, sink, bias)


def _cmul(ar, ai, br, bi):
    return ar * br - ai * bi, ar * bi + ai * br


def _scan_consts(ar, ai, n, reverse):
    row = lax.broadcasted_iota(jnp.int32, (8, n), 0)
    mults = []
    pr, pi = ar, ai
    for k in (1, 2, 4):
        keep = (row < 8 - k) if reverse else (row >= k)
        mults.append(((8 - k) if reverse else k, jnp.where(keep, pr, 0.0), jnp.where(keep, pi, 0.0)))
        pr, pi = _cmul(pr, pi, pr, pi)
    pwr = jnp.zeros((8, n), F32)
    pwi = jnp.zeros((8, n), F32)
    pr, pi = ar, ai
    for e in range(1, 9):
        sel = (row == 8 - e) if reverse else (row == e - 1)
        pwr = jnp.where(sel, pr, pwr)
        pwi = jnp.where(sel, pi, pwi)
        pr, pi = _cmul(pr, pi, ar, ai)
    return mults, pwr, pwi


def _scan8(xr, xi, mults, pwr, pwi, cr, ci):
    for shift, mr, mi in mults:
        sr = pltpu.roll(xr, shift, 0)
        si = pltpu.roll(xi, shift, 0)
        xr, xi = xr + mr * sr - mi * si, xi + mr * si + mi * sr
    return xr + pwr * cr - pwi * ci, xi + pwr * ci + pwi * cr


def _blockdiag(x):
    g, a, b = x.shape
    j = g // SSM_LANE_GROUPS
    eye = jnp.eye(SSM_LANE_GROUPS, dtype=x.dtype)
    y = x.reshape(j, SSM_LANE_GROUPS, a, 1, b) * eye[None, :, None, :, None]
    return y.reshape(j, SSM_LANE_GROUPS * a, SSM_LANE_GROUPS * b)


def _blockdiag_extract(y, a, b):
    j = y.shape[0]
    y = y.reshape(j, SSM_LANE_GROUPS, a, SSM_LANE_GROUPS, b)
    return jnp.einsum("jgahb,gh->jgab", y, jnp.eye(SSM_LANE_GROUPS, dtype=y.dtype)).reshape(j * SSM_LANE_GROUPS, a, b)


def _ssm_disc(lr, li, ldt, brt, bit):
    dt = jnp.exp(ldt)
    mag = jnp.exp(lr * dt)
    ar = mag * jnp.cos(li * dt)
    ai = mag * jnp.sin(li * dt)
    den = lr * lr + li * li
    fr = ((ar - 1.0) * lr + ai * li) / den
    fi = (ai * lr - (ar - 1.0) * li) / den
    bbr = fr[:, None, :] * brt - fi[:, None, :] * bit
    bbi = fr[:, None, :] * bit + fi[:, None, :] * brt
    return ar, ai, bbr, bbi


def _ssm_prep(lr, li, ldt, brt, bit, *, name):
    g, h, p = brt.shape

    def body(lr_ref, li_ref, ldt_ref, brt_ref, bit_ref, ar_ref, ai_ref, bbr_ref, bbi_ref):
        ar, ai, bbr, bbi = _ssm_disc(lr_ref[...], li_ref[...], ldt_ref[...], brt_ref[...], bit_ref[...])
        ar_ref[...] = ar
        ai_ref[...] = ai
        bbr_ref[...] = bbr
        bbi_ref[...] = bbi

    gp = jax.ShapeDtypeStruct((g, p), F32)
    ghp = jax.ShapeDtypeStruct((g, h, p), F32)
    return pl.pallas_call(body, out_shape=(gp, gp, ghp, ghp), name=name)(lr, li, ldt, brt, bit)


def _ssm_prep_bwd(lr, li, ldt, brt, bit, dar, dai, dbbr, dbbi, *, name):
    g, h, p = brt.shape

    def body(lr_ref, li_ref, ldt_ref, brt_ref, bit_ref, dar_ref, dai_ref, dbbr_ref, dbbi_ref,
             dlr_ref, dli_ref, dldt_ref, dbrt_ref, dbit_ref):
        _, vjp = jax.vjp(_ssm_disc, lr_ref[...], li_ref[...], ldt_ref[...], brt_ref[...], bit_ref[...])
        dlr, dli, dldt, dbrt, dbit = vjp((dar_ref[...], dai_ref[...], dbbr_ref[...], dbbi_ref[...]))
        dlr_ref[...] = dlr
        dli_ref[...] = dli
        dldt_ref[...] = dldt
        dbrt_ref[...] = dbrt
        dbit_ref[...] = dbit

    gp = jax.ShapeDtypeStruct((g, p), F32)
    ghp = jax.ShapeDtypeStruct((g, h, p), F32)
    return pl.pallas_call(body, out_shape=(gp, gp, jax.ShapeDtypeStruct((g, 1), F32), ghp, ghp), name=name)(
        lr, li, ldt, brt, bit, dar, dai, dbbr, dbbi)


def _ssm_specs(tc, nlanes, nch, u_colblk, chunk_of):
    return [
        pl.BlockSpec((tc, nch), lambda j, c: (chunk_of(c), u_colblk + j)),
        pl.BlockSpec((1, nlanes), lambda j, c: (0, j)),
        pl.BlockSpec((1, nlanes), lambda j, c: (0, j)),
        pl.BlockSpec((None, nch, nlanes), lambda j, c: (j, 0, 0)),
        pl.BlockSpec((None, nch, nlanes), lambda j, c: (j, 0, 0)),
        pl.BlockSpec((None, nlanes, nch), lambda j, c: (j, 0, 0)),
        pl.BlockSpec((None, nlanes, nch), lambda j, c: (j, 0, 0)),
        pl.BlockSpec((1, nch), lambda j, c: (0, j)),
    ]


def _ssm_fwd(z, ar, ai, bblk_r, bblk_i, cblk_r, cblk_i, d, *, u_col, tc, name, deps=()):
    L = z.shape[0]
    nj, nch, nlanes = bblk_r.shape
    w = nj * nch
    nc = L // tc
    ng = tc // 8

    def body(u_ref, ar_ref, ai_ref, br_ref, bi_ref, cr_ref, ci_ref, d_ref, *rest):
        y_ref, s0r_ref, s0i_ref, xr_ref, xi_ref, carr_ref, cari_ref = rest[len(deps):]
        c = pl.program_id(1)

        @pl.when(c == 0)
        def _():
            carr_ref[...] = jnp.zeros_like(carr_ref)
            cari_ref[...] = jnp.zeros_like(cari_ref)

        s0r_ref[...] = carr_ref[...]
        s0i_ref[...] = cari_ref[...]
        u = u_ref[...]
        ub = u.astype(BF16)
        xr_ref[...] = jnp.dot(ub, br_ref[...].astype(BF16), preferred_element_type=F32)
        xi_ref[...] = jnp.dot(ub, bi_ref[...].astype(BF16), preferred_element_type=F32)
        mults, pwr, pwi = _scan_consts(ar_ref[...], ai_ref[...], nlanes, False)

        def step(g, carry):
            rows = pl.ds(pl.multiple_of(g * 8, 8), 8)
            sr, si = _scan8(xr_ref[rows, :], xi_ref[rows, :], mults, pwr, pwi, carry[0], carry[1])
            xr_ref[rows, :] = sr
            xi_ref[rows, :] = si
            return sr[7:8], si[7:8]

        cr, ci = lax.fori_loop(0, ng, step, (carr_ref[...], cari_ref[...]))
        carr_ref[...] = cr
        cari_ref[...] = ci
        y = (jnp.dot(xr_ref[...].astype(BF16), cr_ref[...].astype(BF16), preferred_element_type=F32)
             - jnp.dot(xi_ref[...].astype(BF16), ci_ref[...].astype(BF16), preferred_element_type=F32)
             + d_ref[...] * u)
        y_ref[...] = y

    state = jax.ShapeDtypeStruct((nc, 1, nj * nlanes), F32)
    state_spec = pl.BlockSpec((None, 1, nlanes), lambda j, c: (c, 0, j))
    return pl.pallas_call(
        body,
        out_shape=(jax.ShapeDtypeStruct((L, w), F32), state, state),
        grid=(nj, nc),
        in_specs=_ssm_specs(tc, nlanes, nch, u_col // nch, lambda c: c) + [pl.BlockSpec(memory_space=pl.ANY)] * len(deps),
        out_specs=(pl.BlockSpec((tc, nch), lambda j, c: (c, j)), state_spec, state_spec),
        scratch_shapes=[pltpu.VMEM((tc, nlanes), F32), pltpu.VMEM((tc, nlanes), F32),
                        pltpu.VMEM((1, nlanes), F32), pltpu.VMEM((1, nlanes), F32)],
        compiler_params=_cparams("parallel", "arbitrary"),
        name=name,
    )(z, ar, ai, bblk_r, bblk_i, cblk_r, cblk_i, d, *deps)


def _ssm_bwd(z, dy, s0r, s0i, ar, ai, bblk_r, bblk_i, cblk_r, cblk_i, d, *, u_col, tc, name):
    L = z.shape[0]
    nj, nch, nlanes = bblk_r.shape
    w = nj * nch
    nc = L // tc
    ng = tc // 8
    chunk_of = lambda c: nc - 1 - c

    def body(u_ref, ar_ref, ai_ref, br_ref, bi_ref, cr_ref, ci_ref, d_ref, dy_ref, s0r_ref, s0i_ref,
             du_ref, dbr_ref, dbi_ref, dcr_ref, dci_ref, dar_ref, dai_ref, dd_ref,
             sr_ref, si_ref, pr_ref, pi_ref, lr_ref, li_ref, carr_ref, cari_ref):
        c = pl.program_id(1)

        @pl.when(c == 0)
        def _():
            for ref in (dbr_ref, dbi_ref, dcr_ref, dci_ref, dar_ref, dai_ref, dd_ref, carr_ref, cari_ref):
                ref[...] = jnp.zeros_like(ref)

        u = u_ref[...]
        dyv = dy_ref[...]
        ub = u.astype(BF16)
        dyb = dyv.astype(BF16)
        brb = br_ref[...].astype(BF16)
        bib = bi_ref[...].astype(BF16)
        crb = cr_ref[...].astype(BF16)
        cib = ci_ref[...].astype(BF16)
        a_r, a_i = ar_ref[...], ai_ref[...]

        sr_ref[...] = jnp.dot(ub, brb, preferred_element_type=F32)
        si_ref[...] = jnp.dot(ub, bib, preferred_element_type=F32)
        mults, pwr, pwi = _scan_consts(a_r, a_i, nlanes, False)
        row = lax.broadcasted_iota(jnp.int32, (8, nlanes), 0)

        def fstep(g, carry):
            rows = pl.ds(pl.multiple_of(g * 8, 8), 8)
            sr, si = _scan8(sr_ref[rows, :], si_ref[rows, :], mults, pwr, pwi, carry[0], carry[1])
            sr_ref[rows, :] = sr
            si_ref[rows, :] = si
            pr_ref[rows, :] = jnp.where(row == 0, carry[0], pltpu.roll(sr, 1, 0))
            pi_ref[rows, :] = jnp.where(row == 0, carry[1], pltpu.roll(si, 1, 0))
            return sr[7:8], si[7:8]

        lax.fori_loop(0, ng, fstep, (s0r_ref[...], s0i_ref[...]))

        nt = (((1,), (1,)), ((), ()))
        lr_ref[...] = lax.dot_general(dyb, crb, nt, preferred_element_type=F32)
        li_ref[...] = -lax.dot_general(dyb, cib, nt, preferred_element_type=F32)
        rmults, rpwr, rpwi = _scan_consts(a_r, -a_i, nlanes, True)

        def rstep(gg, carry):
            cr, ci, acc_r, acc_i = carry
            rows = pl.ds(pl.multiple_of((ng - 1 - gg) * 8, 8), 8)
            lr, li = _scan8(lr_ref[rows, :], li_ref[rows, :], rmults, rpwr, rpwi, cr, ci)
            lr_ref[rows, :] = lr
            li_ref[rows, :] = li
            pr, pi = pr_ref[rows, :], pi_ref[rows, :]
            return lr[0:1], li[0:1], acc_r + lr * pr + li * pi, acc_i + li * pr - lr * pi

        zero8 = jnp.zeros((8, nlanes), F32)
        cr, ci, acc_r, acc_i = lax.fori_loop(0, ng, rstep, (carr_ref[...], cari_ref[...], zero8, zero8))
        carr_ref[...] = cr
        cari_ref[...] = ci
        dar_ref[...] += jnp.sum(acc_r, axis=0, keepdims=True)
        dai_ref[...] += jnp.sum(acc_i, axis=0, keepdims=True)

        tn = (((0,), (0,)), ((), ()))
        lrb = lr_ref[...].astype(BF16)
        lib = li_ref[...].astype(BF16)
        dcr_ref[...] += lax.dot_general(sr_ref[...].astype(BF16), dyb, tn, preferred_element_type=F32)
        dci_ref[...] -= lax.dot_general(si_ref[...].astype(BF16), dyb, tn, preferred_element_type=F32)
        dbr_ref[...] += lax.dot_general(ub, lrb, tn, preferred_element_type=F32)
        dbi_ref[...] += lax.dot_general(ub, lib, tn, preferred_element_type=F32)
        du = (lax.dot_general(lrb, brb, nt, preferred_element_type=F32)
              + lax.dot_general(lib, bib, nt, preferred_element_type=F32)
              + d_ref[...] * dyv)
        du_ref[...] = du.astype(du_ref.dtype)
        dd_ref[...] += jnp.sum(dyv * u, axis=0, keepdims=True)

    state_spec = pl.BlockSpec((None, 1, nlanes), lambda j, c: (chunk_of(c), 0, j))
    bshape = jax.ShapeDtypeStruct((nj, nch, nlanes), F32)
    cshape = jax.ShapeDtypeStruct((nj, nlanes, nch), F32)
    ashape = jax.ShapeDtypeStruct((1, nj * nlanes), F32)
    bspec = pl.BlockSpec((None, nch, nlanes), lambda j, c: (j, 0, 0))
    cspec = pl.BlockSpec((None, nlanes, nch), lambda j, c: (j, 0, 0))
    aspec = pl.BlockSpec((1, nlanes), lambda j, c: (0, j))
    big = pltpu.VMEM((tc, nlanes), F32)
    return pl.pallas_call(
        body,
        out_shape=(jax.ShapeDtypeStruct((L, w), BF16), bshape, bshape, cshape, cshape, ashape, ashape,
                   jax.ShapeDtypeStruct((1, w), F32)),
        grid=(nj, nc),
        in_specs=_ssm_specs(tc, nlanes, nch, u_col // nch, chunk_of)
        + [pl.BlockSpec((tc, nch), lambda j, c: (chunk_of(c), j)), state_spec, state_spec],
        out_specs=(pl.BlockSpec((tc, nch), lambda j, c: (chunk_of(c), j)), bspec, bspec, cspec, cspec, aspec, aspec,
                   pl.BlockSpec((1, nch), lambda j, c: (0, j))),
        scratch_shapes=[big, big, big, big, big, big, pltpu.VMEM((1, nlanes), F32), pltpu.VMEM((1, nlanes), F32)],
        compiler_params=_cparams("parallel", "arbitrary"),
        name=name,
    )(z, ar, ai, bblk_r, bblk_i, cblk_r, cblk_i, d, dy, s0r, s0i)


def _rmsnorm_rows(x, g):
    return x * lax.rsqrt(jnp.mean(x * x, axis=-1, keepdims=True) + RMS_EPS) * g


def _glu_out(y_raw, pre, b):
    yg = jax.nn.gelu(y_raw)
    return yg * jax.nn.sigmoid(pre + b)


def _gate_merge(za, zs, ba, bs, a, bm):
    return jax.nn.sigmoid(za + ba) * a + jax.nn.sigmoid(zs + bs) * bm


def _swiglu(g, u):
    return jax.nn.silu(g) * u


def _row_tile(width_bytes_per_row, rows):
    budget = VMEM_LIMIT_BYTES // 3
    t = max(8, min(1024, budget // (2 * max(width_bytes_per_row, 1))))
    return _pick(rows, t, 16)


def _ssm_params(p, prefix):
    g, pst = p["lam_re"].shape
    ar, ai, bbr, bbi = _ssm_prep(p["lam_re"], p["lam_im"], p["log_dt"], p["b_re_t"], p["b_im_t"], name=prefix + "_ssm_prep")
    return dict(ar=ar.reshape(1, g * pst), ai=ai.reshape(1, g * pst),
                bblk_r=_blockdiag(bbr), bblk_i=_blockdiag(bbi),
                cblk_r=_blockdiag(jnp.swapaxes(p["c_re"], 1, 2)), cblk_i=_blockdiag(jnp.swapaxes(p["c_im"], 1, 2)))


def _layer_fwd(x, p, dims, prefix, deps=(), tick=None):
    tick = tick or (lambda point, arr: ())
    t, d = x.shape
    aw, kvw, sw, ff = dims["aw"], dims["kvw"], dims["sw"], dims["ff"]
    off_u = aw + 2 * kvw
    off_g = off_u + sw
    gblk = _pick(d, 512, 128)
    assert off_g % gblk == 0 and off_u % (SSM_LANE_GROUPS * SSM_GROUP_CH) == 0
    sv = {"x": x}

    h, = _rowmap(_rmsnorm_rows, [(x, "row", d, 0), (p["norm_mix_g"], "vec", d, 0)], [(d, BF16, "row", d)],
                 rows=t, tm=_row_tile(6 * d, t), name=prefix + "_norm_mix", deps=deps)
    z = _mm_nn(h, p["w_in"], out_dtype=F32, tm=512, tn=1664, tk=2048, name=prefix + "_mm_in")
    ya = _attn_fwd(z, p["q_norm_g"], p["k_norm_g"], p["attn_sinks"], n_q=dims["n_q"], n_kv=dims["n_kv"],
                   name=prefix + "_attn_fwd", deps=tick("in", z))
    sp = _ssm_params(p, prefix)
    y_raw, s0r, s0i = _ssm_fwd(z, sp["ar"], sp["ai"], sp["bblk_r"], sp["bblk_i"], sp["cblk_r"], sp["cblk_i"], p["ssm_d"],
                               u_col=off_u, tc=dims["tc"], name=prefix + "_ssm_fwd", deps=tick("attn", ya))
    yg, = _rowmap(jax.nn.gelu, [(y_raw, "row", sw, 0)], [(sw, BF16, "row", sw)],
                  rows=t, tm=_row_tile(6 * sw, t), name=prefix + "_gelu", deps=tick("ssm", y_raw))
    pre = _mm_nn(yg, p["ssm_glu_w"], out_dtype=F32, tm=1024, tn=1024, tk=1024, name=prefix + "_mm_glu")
    y2, = _rowmap(_glu_out, [(y_raw, "row", sw, 0), (pre, "row", sw, 0), (p["ssm_glu_b"], "vec", sw, 0)],
                  [(sw, BF16, "row", sw)], rows=t, tm=_row_tile(10 * sw, t), name=prefix + "_glu_out")
    a = _mm_nn(ya, p["w_attn_branch"], out_dtype=F32, tm=1024, tn=512, tk=1024, name=prefix + "_mm_ab")
    bm = _mm_nn(y2, p["w_ssm_branch"], out_dtype=F32, tm=1024, tn=512, tk=1024, name=prefix + "_mm_sb")
    ngb = d // gblk
    merged, = _rowmap(
        _gate_merge,
        [(z, "row", gblk, off_g // gblk), (z, "row", gblk, off_g // gblk + ngb),
         (p["gate_bias"], "vec", gblk, 0), (p["gate_bias"], "vec", gblk, ngb),
         (a, "row", gblk, 0), (bm, "row", gblk, 0)],
        [(d, BF16, "row", gblk)], rows=t, tm=_row_tile(18 * gblk, t), ncol=ngb, name=prefix + "_gate")
    x1 = _mm_nn(merged, p["w_out"], out_dtype=F32, tm=512, tn=1024, tk=2048, name=prefix + "_mm_out", add=x)
    h2, = _rowmap(_rmsnorm_rows, [(x1, "row", d, 0), (p["norm_ffn_g"], "vec", d, 0)], [(d, BF16, "row", d)],
                  rows=t, tm=_row_tile(6 * d, t), name=prefix + "_norm_ffn", deps=tick("out", x1))
    gu = _mm_nn(h2, p["w_ffn_in"], out_dtype=F32, tm=512, tn=1408, tk=2048, name=prefix + "_mm_ffn_in")
    fblk = _pick(ff, 1408, 128)
    nfb = ff // fblk
    act, = _rowmap(_swiglu, [(gu, "row", fblk, 0), (gu, "row", fblk, nfb)], [(ff, BF16, "row", fblk)],
                   rows=t, tm=_row_tile(10 * fblk, t), ncol=nfb, name=prefix + "_swiglu", deps=tick("ffn_in", gu))
    x2 = _mm_nn(act, p["w_ffn_out"], out_dtype=F32, tm=512, tn=512, tk=5632, name=prefix + "_mm_ffn_out", add=x1)
    sv.update(h=h, z=z, ya=ya, sp=sp, y_raw=y_raw, s0r=s0r, s0i=s0i, yg=yg, pre=pre, y2=y2, a=a, bm=bm,
              merged=merged, x1=x1, h2=h2, gu=gu, act=act)
    return x2, sv


def _layer_bwd(dx2, dx2b, sv, p, dims, prefix, gbuf, deps=(), before_mixer=None, before_in=None):
    t, d = dx2.shape
    aw, kvw, sw, ff = dims["aw"], dims["kvw"], dims["sw"], dims["ff"]
    off_u = aw + 2 * kvw
    off_g = off_u + sw
    gblk = _pick(d, 512, 128)
    ngb = d // gblk
    fblk = _pick(ff, 1408, 128)
    nfb = ff // fblk
    g = {}

    dact = _mm_nt(dx2b, p["w_ffn_out"], out_dtype=F32, tm=512, tn=2048, tko=1408, name=prefix + "_mm_dact", deps=deps)
    g["w_ffn_out"] = _mm_tn(sv["act"], dx2b, p["w_ffn_out"], into=gbuf.get("w_ffn_out"), tm=4096, tn=1024, tko=512,
                            name=prefix + "_mm_dw_ffn_out")

    def swiglu_bwd(gg, uu, da):
        _, vjp = jax.vjp(_swiglu, gg, uu)
        return vjp(da)

    dgu_g, dgu_u = _rowmap(swiglu_bwd, [(sv["gu"], "row", fblk, 0), (sv["gu"], "row", fblk, nfb), (dact, "row", fblk, 0)],
                           [(ff, BF16, "row", fblk), (ff, BF16, "row", fblk)],
                           rows=t, tm=_row_tile(16 * fblk, t), ncol=nfb, name=prefix + "_swiglu_bwd")
    dgu = jnp.concatenate([dgu_g, dgu_u], axis=1)
    dh2 = _mm_nt(dgu, p["w_ffn_in"], out_dtype=F32, tm=512, tn=1408, tko=2048, name=prefix + "_mm_dh2")
    g["w_ffn_in"] = _mm_tn(sv["h2"], dgu, p["w_ffn_in"], into=gbuf.get("w_ffn_in"), tm=4096, tn=1408, tko=512,
                           name=prefix + "_mm_dw_ffn_in")

    def norm_bwd(xx, gg, dh, dres):
        _, vjp = jax.vjp(_rmsnorm_rows, xx, gg)
        dxx, dgg = vjp(dh)
        dxx = dxx + dres
        return dxx, dxx, dgg

    dx1, dx1b, g["norm_ffn_g"] = _rowmap(
        norm_bwd, [(sv["x1"], "row", d, 0), (p["norm_ffn_g"], "vec", d, 0), (dh2, "row", d, 0), (dx2, "row", d, 0)],
        [(d, F32, "row", d), (d, BF16, "row", d), (d, F32, "acc", d)],
        rows=t, tm=_row_tile(22 * d, t), name=prefix + "_norm_ffn_bwd")

    deps = before_mixer(dx1, g) if before_mixer else ()
    dmerged = _mm_nt(dx1b, p["w_out"], out_dtype=F32, tm=1024, tn=2048, tko=1024, name=prefix + "_mm_dmerged", deps=deps)
    g["w_out"] = _mm_tn(sv["merged"], dx1b, p["w_out"], into=gbuf.get("w_out"), tm=4096, tn=1024, tko=512,
                        name=prefix + "_mm_dw_out")

    def gate_bwd(za, zs, ba, bs, aa, bb, dm):
        _, vjp = jax.vjp(_gate_merge, za, zs, ba, bs, aa, bb)
        dza, dzs, dba, dbs, daa, dbb = vjp(dm)
        return daa, dbb, dza, dzs, dba, dbs

    z = sv["z"]
    da, dbm, dza, dzs, dba, dbs = _rowmap(
        gate_bwd,
        [(z, "row", gblk, off_g // gblk), (z, "row", gblk, off_g // gblk + ngb),
         (p["gate_bias"], "vec", gblk, 0), (p["gate_bias"], "vec", gblk, ngb),
         (sv["a"], "row", gblk, 0), (sv["bm"], "row", gblk, 0), (dmerged, "row", gblk, 0)],
        [(d, BF16, "row", gblk), (d, BF16, "row", gblk), (d, BF16, "row", gblk), (d, BF16, "row", gblk),
         (d, F32, "acc", gblk), (d, F32, "acc", gblk)],
        rows=t, tm=_row_tile(32 * gblk, t), ncol=ngb, name=prefix + "_gate_bwd")
    g["gate_bias"] = jnp.concatenate([dba, dbs], axis=1)
    dya = _mm_nt(da, p["w_attn_branch"], out_dtype=F32, tm=1024, tn=512, tko=1024, name=prefix + "_mm_dya")
    g["w_attn_branch"] = _mm_tn(sv["ya"], da, p["w_attn_branch"], into=gbuf.get("w_attn_branch"), tm=4096, tn=512,
                                tko=512, name=prefix + "_mm_dw_ab")
    dy2 = _mm_nt(dbm, p["w_ssm_branch"], out_dtype=F32, tm=1024, tn=512, tko=1024, name=prefix + "_mm_dy2")
    g["w_ssm_branch"] = _mm_tn(sv["y2"], dbm, p["w_ssm_branch"], into=gbuf.get("w_ssm_branch"), tm=4096, tn=512,
                               tko=512, name=prefix + "_mm_dw_sb")

    def glu_bwd(y_raw, pre, b, dy):
        yg = jax.nn.gelu(y_raw)
        _, vjp = jax.vjp(lambda a_, b_, c_: a_ * jax.nn.sigmoid(b_ + c_), yg, pre, b)
        dyg, dpre, db = vjp(dy)
        return dyg, dpre, db

    dyg_direct, dpre, g["ssm_glu_b"] = _rowmap(
        glu_bwd, [(sv["y_raw"], "row", sw, 0), (sv["pre"], "row", sw, 0), (p["ssm_glu_b"], "vec", sw, 0), (dy2, "row", sw, 0)],
        [(sw, F32, "row", sw), (sw, BF16, "row", sw), (sw, F32, "acc", sw)],
        rows=t, tm=_row_tile(24 * sw, t), name=prefix + "_glu_bwd")
    dyg2 = _mm_nt(dpre, p["ssm_glu_w"], out_dtype=F32, tm=1024, tn=1024, tko=1024, name=prefix + "_mm_dyg")
    g["ssm_glu_w"] = _mm_tn(sv["yg"], dpre, p["ssm_glu_w"], into=gbuf.get("ssm_glu_w"), tm=4096, tn=1024, tko=512,
                            name=prefix + "_mm_dw_glu")

    def gelu_bwd(y_raw, d1, d2):
        _, vjp = jax.vjp(jax.nn.gelu, y_raw)
        return vjp(d1 + d2)[0]

    dy_raw, = _rowmap(gelu_bwd, [(sv["y_raw"], "row", sw, 0), (dyg_direct, "row", sw, 0), (dyg2, "row", sw, 0)],
                      [(sw, F32, "row", sw)], rows=t, tm=_row_tile(20 * sw, t), name=prefix + "_gelu_bwd")
    sp = sv["sp"]
    du, dbr, dbi, dcr, dci, dar, dai, g["ssm_d"] = _ssm_bwd(
        z, dy_raw, sv["s0r"], sv["s0i"], sp["ar"], sp["ai"], sp["bblk_r"], sp["bblk_i"], sp["cblk_r"], sp["cblk_i"],
        p["ssm_d"], u_col=off_u, tc=dims["tc"], name=prefix + "_ssm_bwd")
    ngr, pst = p["lam_re"].shape
    hch = SSM_GROUP_CH
    dlr, dli, dldt, dbrt, dbit = _ssm_prep_bwd(
        p["lam_re"], p["lam_im"], p["log_dt"], p["b_re_t"], p["b_im_t"],
        dar.reshape(ngr, pst), dai.reshape(ngr, pst), _blockdiag_extract(dbr, hch, pst), _blockdiag_extract(dbi, hch, pst),
        name=prefix + "_ssm_prep_bwd")
    g.update(ssm_lambda_re=dlr, ssm_lambda_im=dli, ssm_log_dt=dldt.reshape(ngr),
             ssm_b_re=jnp.swapaxes(dbrt, 1, 2), ssm_b_im=jnp.swapaxes(dbit, 1, 2),
             ssm_c_re=jnp.swapaxes(_blockdiag_extract(dcr, pst, hch), 1, 2),
             ssm_c_im=jnp.swapaxes(_blockdiag_extract(dci, pst, hch), 1, 2))

    dq, dkv, g["q_norm_g"], g["k_norm_g"], g["attn_sinks"] = _attn_bwd(
        z, dya, p["q_norm_g"], p["k_norm_g"], p["attn_sinks"], n_q=dims["n_q"], n_kv=dims["n_kv"], name=prefix + "_attn_bwd")

    dz = jnp.concatenate([dq, dkv, du, dza, dzs], axis=1)
    deps = before_in(dq, g) if before_in else ()
    dh = _mm_nt(dz, p["w_in"], out_dtype=F32, tm=512, tn=1664, tko=2048, name=prefix + "_mm_dh", deps=deps)
    g["w_in"] = _mm_tn(sv["h"], dz, p["w_in"], into=gbuf.get("w_in"), tm=4096, tn=1664, tko=512,
                       name=prefix + "_mm_dw_in")
    dx, dxb, g["norm_mix_g"] = _rowmap(
        norm_bwd, [(sv["x"], "row", d, 0), (p["norm_mix_g"], "vec", d, 0), (dh, "row", d, 0), (dx1, "row", d, 0)],
        [(d, F32, "row", d), (d, BF16, "row", d), (d, F32, "acc", d)],
        rows=t, tm=_row_tile(22 * d, t), name=prefix + "_norm_mix_bwd")
    return dx, dxb, g


def _loss_and_grad(y, target):
    t, d = y.shape

    def fn(yy, tt):
        e = yy - tt
        dy = e * (1.0 / d)
        return dy, dy, jnp.sum(e * e, keepdims=True).reshape(1, 1)

    dy, dyb, sq = _rowmap(fn, [(y, "row", d, 0), (target, "row", d, 0)],
                          [(d, F32, "row", d), (d, BF16, "row", d), (1, F32, "acc", 1)],
                          rows=t, tm=_row_tile(14 * d, t), name="loss")
    return sq, dy, dyb


def _local_step(x, target, n_layers, weights_of, dims, fwd_tick=None, bwd_tick=None):
    saved, params = [], []
    h = x
    for l in range(n_layers):
        p, deps = weights_of(l, h)
        params.append(p)
        tick = (lambda point, arr, l=l: fwd_tick(l, point, arr)) if fwd_tick else None
        h, sv = _layer_fwd(h, p, dims, "l%d" % l, deps, tick)
        saved.append(sv)
    sq, dy, dyb = _loss_and_grad(h, target)
    grads = [None] * n_layers
    deps = ()
    for l in reversed(range(n_layers)):
        if bwd_tick:
            ffn_done = lambda arr, g, l=l: bwd_tick(l, "ffn", arr, {n: g[n] for n in ("w_ffn_in", "w_ffn_out")})
            mixer_done = lambda arr, g, l=l: bwd_tick(l, "mixer", arr, {n: g[n] for n in MIXER_WEIGHTS})
        else:
            ffn_done = mixer_done = None
        dy, dyb, grads[l] = _layer_bwd(dy, dyb, saved[l], params[l], dims, "l%d" % l, {}, deps=deps,
                                       before_mixer=ffn_done, before_in=mixer_done)
        if bwd_tick:
            deps = bwd_tick(l, "in", dy, {"w_in": grads[l]["w_in"]})
    return sq, dy, grads


COL_SHARDED = ("w_in", "w_attn_branch", "w_ssm_branch", "w_ffn_in")
ROW_SHARDED = ("ssm_glu_w", "w_out", "w_ffn_out")
BIG_WEIGHTS = COL_SHARDED + ROW_SHARDED
WEIGHT_NAMES = ("norm_mix_g", "w_in", "gate_bias", "q_norm_g", "k_norm_g", "attn_sinks", "ssm_lambda_re",
                "ssm_lambda_im", "ssm_log_dt", "ssm_b_re", "ssm_b_im", "ssm_c_re", "ssm_c_im", "ssm_d", "ssm_glu_w",
                "ssm_glu_b", "w_attn_branch", "w_ssm_branch", "w_out", "norm_ffn_g", "w_ffn_in", "w_ffn_out")
SMALL_WEIGHTS = tuple(n for n in WEIGHT_NAMES if n not in BIG_WEIGHTS)
MIXER_WEIGHTS = ("w_out", "w_attn_branch", "w_ssm_branch", "ssm_glu_w")
WEIGHT_GROUPS = {"in": ("w_in",), "mixer": MIXER_WEIGHTS, "ffn": ("w_ffn_in", "w_ffn_out")}


def _dims(d, shapes, tc):
    s, _, aw, _ = shapes["w_attn_branch"]
    sw = shapes["w_ssm_branch"][2]
    in_w = shapes["w_in"][3] * s
    kvw = (in_w - aw - sw - 2 * d) // 2
    ff = shapes["w_ffn_out"][2] * s
    return dict(aw=aw, kvw=kvw, sw=sw, ff=ff, n_q=aw // HEAD_DIM, n_kv=kvw // HEAD_DIM, tc=tc)


def _big_params(big):
    p = {n: _Weight(a, 0, "col") for n, a in big.items() if n in COL_SHARDED}
    p.update({n: _Weight(a.reshape(1, 1, -1, a.shape[-1]), 0, "col") for n, a in big.items() if n in ROW_SHARDED})
    return p


def _layer_params(l, small):
    p = {}
    for n in ("norm_mix_g", "gate_bias", "q_norm_g", "k_norm_g", "ssm_d", "ssm_glu_b", "norm_ffn_g"):
        p[n] = small[n][l][None]
    p["attn_sinks"] = small["attn_sinks"][l]
    p["lam_re"] = small["ssm_lambda_re"][l]
    p["lam_im"] = small["ssm_lambda_im"][l]
    p["log_dt"] = small["ssm_log_dt"][l][:, None]
    p["b_re_t"] = jnp.swapaxes(small["ssm_b_re"][l], 1, 2)
    p["b_im_t"] = jnp.swapaxes(small["ssm_b_im"][l], 1, 2)
    p["c_re"] = small["ssm_c_re"][l]
    p["c_im"] = small["ssm_c_im"][l]
    return p


_ANY = pl.BlockSpec(memory_space=pl.ANY)
_MESH_ID = pl.DeviceIdType.MESH


def _coords():
    return lax.axis_index("x"), lax.axis_index("y"), lax.axis_index("c")


def _remote(src, dst, send_sems, recv_sems, k, to):
    return pltpu.make_async_remote_copy(src_ref=src, dst_ref=dst, send_sem=send_sems.at[k], recv_sem=recv_sems.at[k],
                                        device_id=to, device_id_type=_MESH_ID)


def _comm_call(body, ins, out_shapes, n_remote, name, aliases=None, scratch=()):
    return pl.pallas_call(
        body,
        out_shape=tuple(out_shapes),
        in_specs=[_ANY] * len(ins),
        out_specs=tuple([_ANY] * len(out_shapes)),
        scratch_shapes=[pltpu.SemaphoreType.DMA((n_remote,)), pltpu.SemaphoreType.DMA((n_remote,))] + list(scratch),
        input_output_aliases=aliases or {},
        compiler_params=pltpu.CompilerParams(has_side_effects=True),
        name=name,
    )(*ins)


def _my_chip():
    return (2 * lax.axis_index("x") + lax.axis_index("y")).astype(jnp.int32).reshape(1)


def _my_core():
    return lax.axis_index("c").astype(jnp.int32).reshape(1)


def _cast_into_slot(w, layer, *, name):
    _, r, c = w.shape
    tm = _row_tile(12 * c, r)

    def body(me_ref, w_ref, o_ref):
        o_ref[...] = w_ref[...].astype(o_ref.dtype)

    return pl.pallas_call(
        body,
        out_shape=jax.ShapeDtypeStruct((N_CHIPS, 1, r, c), BF16),
        grid_spec=pltpu.PrefetchScalarGridSpec(
            num_scalar_prefetch=1,
            grid=(r // tm,),
            in_specs=[pl.BlockSpec((None, tm, c), lambda i, me: (layer, i, 0))],
            out_specs=pl.BlockSpec((None, None, tm, c), lambda i, me: (me[0], 0, i, 0)),
        ),
        compiler_params=_cparams("parallel"),
        name=name,
    )(_my_chip(), w)


def _allgather_weights(bufs, *, name):
    n = len(bufs)

    def body(*refs):
        outs = refs[n:2 * n]
        send_sems, recv_sems = refs[2 * n:]
        x, y, c = _coords()
        me = 2 * x + y
        chips = [(1 - x, y), (x, 1 - y), (1 - x, 1 - y)]
        sibling = (x, y, 1 - c)

        def half(i, slot, hc):
            rh = bufs[i].shape[2] // 2
            return outs[i].at[slot, :, pl.ds(hc * rh, rh), :]

        first = [_remote(half(i, me, c), half(i, me, c), send_sems, recv_sems, 6 * i + k, (px, py, c))
                 for i in range(n) for k, (px, py) in enumerate(chips)]
        for cp in first:
            cp.start()
        passed = []
        for k, (px, py) in enumerate(chips):
            for i in range(n):
                landed = half(i, 2 * px + py, c)
                _remote(landed, landed, send_sems, recv_sems, 6 * i + k, (px, py, c)).wait_recv()
                fw = _remote(landed, landed, send_sems, recv_sems, 6 * i + 3 + k, sibling)
                fw.start()
                passed.append(fw)
        for k, (px, py) in enumerate(chips):
            for i in range(n):
                other = half(i, 2 * px + py, 1 - c)
                _remote(other, other, send_sems, recv_sems, 6 * i + 3 + k, sibling).wait_recv()
        for cp in first + passed:
            cp.wait_send()

    outs = [jax.ShapeDtypeStruct(b.shape, b.dtype) for b in bufs]
    return _comm_call(body, bufs, outs, 6 * n, name, aliases={i: i for i in range(n)})


_HBM = pl.BlockSpec(memory_space=pltpu.HBM)
_SEM = pl.BlockSpec(memory_space=pltpu.SEMAPHORE)
_DATAFLOW = pltpu.SideEffectType.DATAFLOW_SIDE_EFFECTING


class _SplitExchange:
    def __init__(self, srcs, lands, build, n_copies, name):
        self.build, self.n, self.name = build, n_copies, name
        self.ns, self.nl = len(srcs), len(lands)
        self.bufs = [pltpu.with_memory_space_constraint(a, pltpu.HBM) for a in list(srcs) + list(lands)]

    def _copies(self, refs, send_sems, recv_sems):
        triples = self.build(refs[:self.ns], refs[self.ns:self.ns + self.nl])
        assert len(triples) == self.n
        return [pltpu.make_async_remote_copy(src_ref=s, dst_ref=d, send_sem=send_sems.at[k], recv_sem=recv_sems.at[k],
                                             device_id=to, device_id_type=_MESH_ID) for k, (s, d, to) in enumerate(triples)]

    def start(self, deps=()):
        nb = self.ns + self.nl

        def body(*refs):
            outs = refs[nb + len(deps):]
            for cp in self._copies(refs, outs[0], outs[1]):
                cp.start()
            outs[-1][...] = jnp.zeros_like(outs[-1])

        sems = pltpu.SemaphoreType.DMA((self.n,))
        res = pl.pallas_call(
            body,
            out_shape=(sems, sems, *[pltpu.HBM(b.shape, b.dtype) for b in self.bufs], jax.ShapeDtypeStruct((8, 128), F32)),
            in_specs=[_HBM] * nb + [_ANY] * len(deps),
            out_specs=(_SEM, _SEM, *[_HBM] * nb, pl.BlockSpec(memory_space=pltpu.VMEM)),
            input_output_aliases={i: 2 + i for i in range(nb)},
            compiler_params=pltpu.CompilerParams(has_side_effects=_DATAFLOW),
            name=self.name + "_start",
        )(*self.bufs, *deps)
        self.send_sems, self.recv_sems = res[0], res[1]
        self.bufs = list(res[2:2 + nb])
        return res[-1]

    def wait(self, after=()):
        nb = self.ns + self.nl

        def body(*refs):
            for cp in self._copies(refs, refs[nb], refs[nb + 1]):
                cp.wait_send()
                cp.wait_recv()

        res = pl.pallas_call(
            body,
            out_shape=tuple(pltpu.HBM(b.shape, b.dtype) for b in self.bufs),
            in_specs=[_HBM] * nb + [_SEM, _SEM] + [_ANY] * len(after),
            out_specs=tuple([_HBM] * nb),
            input_output_aliases={i: i for i in range(nb)},
            compiler_params=pltpu.CompilerParams(has_side_effects=_DATAFLOW),
            name=self.name + "_wait",
        )(*self.bufs, self.send_sems, self.recv_sems, *after)
        res = list(res)
        return res[:self.ns], res[self.ns:]


def _other_chips(x, y):
    return [(1 - x, y), (x, 1 - y), (1 - x, 1 - y)]


def _gather_steps(bufs, tag, deps, publish):
    n = len(bufs)
    half = lambda ref, i, slot, hc: ref.at[slot, :, pl.ds(hc * (bufs[i].shape[2] // 2), bufs[i].shape[2] // 2), :]

    def over_ici(srcs, lands):
        x, y, c = _coords()
        me = 2 * x + y
        return [(half(srcs[i], i, me, c), half(srcs[i], i, me, c), (px, py, c))
                for i in range(n) for px, py in _other_chips(x, y)]

    def to_sibling(srcs, lands):
        x, y, c = _coords()
        return [(half(srcs[i], i, 2 * px + py, c), half(srcs[i], i, 2 * px + py, c), (x, y, 1 - c))
                for i in range(n) for px, py in _other_chips(x, y)]

    ex = _SplitExchange(bufs, [], over_ici, 3 * n, tag + "_ici")
    after = yield ex.start(deps)
    bufs, _ = ex.wait((after,))
    ex = _SplitExchange(bufs, [], to_sibling, 3 * n, tag + "_d2d")
    after = yield ex.start()
    bufs, _ = ex.wait((after,))
    publish(bufs)


def _reduce_steps(grads, tag, publish):
    n = len(grads)
    rh = [g.shape[1] // 2 for g in grads]
    theirs = [lax.empty((g.shape[0], g.shape[1] // 2, g.shape[2]), F32) for g in grads]

    def halves(srcs, lands):
        x, y, c = _coords()
        return [(srcs[i].at[:, pl.ds((1 - c) * rh[i], rh[i]), :], lands[i], (x, y, 1 - c)) for i in range(n)]

    def chips(srcs, lands):
        x, y, c = _coords()
        me = 2 * x + y
        return [(srcs[i].at[2 * px + py], lands[i].at[me], (px, py, c)) for i in range(n) for px, py in _other_chips(x, y)]

    def sibling(srcs, lands):
        x, y, c = _coords()
        return [(srcs[i], lands[i], (x, y, 1 - c)) for i in range(n)]

    ex = _SplitExchange(grads, theirs, halves, n, tag + "_halves")
    after = yield ex.start()
    grads, theirs = ex.wait((after,))
    parts = [_add_own_half(g, t, name="%s_add_own_half_%d" % (tag, i)) for i, (g, t) in enumerate(zip(grads, theirs))]
    ex = _SplitExchange(parts, [lax.empty(p.shape, p.dtype) for p in parts], chips, 3 * n, tag + "_chips")
    after = yield ex.start()
    parts, got = ex.wait((after,))
    mine = [_sum_chips(p, g, name="%s_sum_chips_%d" % (tag, i)) for i, (p, g) in enumerate(zip(parts, got))]
    ex = _SplitExchange(mine, [lax.empty(m.shape, m.dtype) for m in mine], sibling, n, tag + "_sibling")
    after = yield ex.start()
    mine, theirs = ex.wait((after,))
    publish(list(zip(mine, theirs)))


def _share_steps(buf, tag, publish):
    def build(srcs, lands):
        x, y, c = _coords()
        flip = lambda v, b: (1 - v) if b else v
        mine = 4 * x + 2 * y + c
        return [(srcs[0].at[mine], srcs[0].at[mine], (flip(x, r & 4), flip(y, r & 2), flip(c, r & 1)))
                for r in range(1, N_DEV)]

    ex = _SplitExchange([buf], [], build, N_DEV - 1, tag)
    after = yield ex.start()
    bufs, _ = ex.wait((after,))
    publish(bufs[0])


class _Exchanges:
    def __init__(self):
        self.running = []

    def launch(self, steps):
        self.running.append(steps)
        return next(steps)

    def advance(self, steps, after):
        try:
            return steps.send(after)
        except StopIteration:
            self.running.remove(steps)
            return None

    def advance_all(self, after):
        tokens = [self.advance(steps, after) for steps in list(self.running)]
        return tuple(t for t in tokens if t is not None)


def _add_own_half(g, theirs, *, name):
    s, r, c = g.shape
    rh = r // 2
    tm = _row_tile(10 * c, rh)
    nb = rh // tm

    def body(core_ref, g_ref, t_ref, o_ref):
        o_ref[...] = (g_ref[...] + t_ref[...]).astype(o_ref.dtype)

    return pl.pallas_call(
        body,
        out_shape=jax.ShapeDtypeStruct((s, rh, c), BF16),
        grid_spec=pltpu.PrefetchScalarGridSpec(
            num_scalar_prefetch=1,
            grid=(s, nb),
            in_specs=[pl.BlockSpec((None, tm, c), lambda k, i, core: (k, core[0] * nb + i, 0)),
                      pl.BlockSpec((None, tm, c), lambda k, i, core: (k, i, 0))],
            out_specs=pl.BlockSpec((None, tm, c), lambda k, i, core: (k, i, 0)),
        ),
        compiler_params=_cparams("parallel", "parallel"),
        name=name,
    )(_my_core(), g, theirs)


def _sum_chips(part, got, *, name):
    s, rh, c = part.shape
    tm = _row_tile(14 * c, rh)

    def body(me_ref, p_ref, a_ref, b_ref, c_ref, o_ref):
        o_ref[...] = ((p_ref[...].astype(F32) + a_ref[...].astype(F32)) + b_ref[...].astype(F32)) + c_ref[...].astype(F32)

    slot = lambda k: (lambda i, me: ((me[0] + k) % s, i, 0))
    return pl.pallas_call(
        body,
        out_shape=jax.ShapeDtypeStruct((rh, c), F32),
        grid_spec=pltpu.PrefetchScalarGridSpec(
            num_scalar_prefetch=1,
            grid=(rh // tm,),
            in_specs=[pl.BlockSpec((None, tm, c), slot(k)) for k in range(s)],
            out_specs=pl.BlockSpec((tm, c), lambda i, me: (i, 0)),
        ),
        compiler_params=_cparams("parallel"),
        name=name,
    )(_my_chip(), part, got, got, got)


def _place_into_slot(buf, *, name):
    r, c = buf.shape
    tm = _row_tile(8 * c, r)
    mine = (4 * lax.axis_index("x") + 2 * lax.axis_index("y") + lax.axis_index("c")).astype(jnp.int32).reshape(1)

    def body(me_ref, i_ref, o_ref):
        o_ref[...] = i_ref[...]

    return pl.pallas_call(
        body,
        out_shape=jax.ShapeDtypeStruct((N_DEV, r, c), buf.dtype),
        grid_spec=pltpu.PrefetchScalarGridSpec(
            num_scalar_prefetch=1,
            grid=(r // tm,),
            in_specs=[pl.BlockSpec((tm, c), lambda i, me: (i, 0))],
            out_specs=pl.BlockSpec((None, tm, c), lambda i, me: (me[0], i, 0)),
        ),
        compiler_params=_cparams("parallel"),
        name=name,
    )(mine, buf)


def _sum_slots(arr, *, name):
    s, r, c = arr.shape
    tm = _row_tile(4 * c * (s + 1), r)

    def body(*refs):
        acc = refs[0][...]
        for ref in refs[1:s]:
            acc = acc + ref[...]
        refs[s][...] = acc

    return pl.pallas_call(
        body,
        out_shape=jax.ShapeDtypeStruct((r, c), arr.dtype),
        grid=(r // tm,),
        in_specs=[pl.BlockSpec((None, tm, c), lambda i, k=k: (k, i, 0)) for k in range(s)],
        out_specs=pl.BlockSpec((tm, c), lambda i: (i, 0)),
        compiler_params=_cparams("parallel"),
        name=name,
    )(*([arr] * s))


def _adamw_fn(w, g, m, v):
    m = ADAM_B1 * m + (1.0 - ADAM_B1) * g
    v = ADAM_B2 * v + (1.0 - ADAM_B2) * jnp.square(g)
    m_hat = m / (1.0 - ADAM_B1 ** ADAM_STEP)
    v_hat = v / (1.0 - ADAM_B2 ** ADAM_STEP)
    delta = -ADAM_LR * (m_hat / (jnp.sqrt(v_hat) + ADAM_EPS) + ADAM_WD * w)
    return delta, m, v


def _adamw(w, g, m, v, *, name):
    rows, cols = w.shape
    ins = [(a, "row", cols, 0) for a in (w, g, m, v)]
    outs = [(cols, F32, "row", cols)] * 3
    return _rowmap(_adamw_fn, ins, outs, rows=rows, tm=_row_tile(56 * cols, rows), name=name)


def _adamw_sharded(w, m, v, g_mine, g_sibling, layer, into, *, name, deps=()):
    nl, r, c = w.shape
    rh = r // 2
    tm = _row_tile(40 * c, rh)
    nb = rh // tm
    n_into = 0 if into is None else 4

    def body(core_ref, w_ref, m_ref, v_ref, a_ref, b_ref, *rest):
        g_ref, d_ref, nm_ref, nv_ref = rest[n_into + len(deps):]
        g = jnp.where(pl.program_id(0) == core_ref[0], a_ref[...], b_ref[...])
        delta, nm, nv = _adamw_fn(w_ref[...], g, m_ref[...], v_ref[...])
        g_ref[...] = g
        d_ref[...] = delta
        nm_ref[...] = nm
        nv_ref[...] = nv

    whole = pl.BlockSpec((None, tm, c), lambda h, i, core: (layer, h * nb + i, 0))
    half = pl.BlockSpec((tm, c), lambda h, i, core: (i, 0))
    shape = jax.ShapeDtypeStruct((nl, r, c), F32)
    return pl.pallas_call(
        body,
        out_shape=(shape, shape, shape, shape),
        grid_spec=pltpu.PrefetchScalarGridSpec(
            num_scalar_prefetch=1,
            grid=(2, nb),
            in_specs=[whole, whole, whole, half, half] + [pl.BlockSpec(memory_space=pl.ANY)] * (n_into + len(deps)),
            out_specs=(whole, whole, whole, whole),
        ),
        input_output_aliases={6 + k: k for k in range(n_into)},
        compiler_params=_cparams("parallel", "parallel"),
        name=name,
    )(_my_core(), w, m, v, g_mine, g_sibling, *(into or ()), *deps)


def _pack(arrays):
    flat = jnp.concatenate([a.reshape(-1) for a in arrays])
    pad = (-flat.shape[0]) % (256 * 128)
    return jnp.pad(flat, (0, pad)).reshape(-1, 128)


def _unpack(buf, shapes):
    flat = buf.reshape(-1)
    out, off = [], 0
    for s in shapes:
        n = math.prod(s)
        out.append(flat[off:off + n].reshape(s))
        off += n
    return out


def kernel(x, norm_mix_g, w_in, gate_bias, q_norm_g, k_norm_g, attn_sinks, ssm_lambda_re, ssm_lambda_im, ssm_log_dt, ssm_b_re, ssm_b_im, ssm_c_re, ssm_c_im, ssm_d, ssm_glu_w, ssm_glu_b, w_attn_branch, w_ssm_branch, w_out, norm_ffn_g, w_ffn_in, w_ffn_out, loss_target, m_norm_mix_g, m_w_in, m_gate_bias, m_q_norm_g, m_k_norm_g, m_attn_sinks, m_ssm_lambda_re, m_ssm_lambda_im, m_ssm_log_dt, m_ssm_b_re, m_ssm_b_im, m_ssm_c_re, m_ssm_c_im, m_ssm_d, m_ssm_glu_w, m_ssm_glu_b, m_w_attn_branch, m_w_ssm_branch, m_w_out, m_norm_ffn_g, m_w_ffn_in, m_w_ffn_out, v_norm_mix_g, v_w_in, v_gate_bias, v_q_norm_g, v_k_norm_g, v_attn_sinks, v_ssm_lambda_re, v_ssm_lambda_im, v_ssm_log_dt, v_ssm_b_re, v_ssm_b_im, v_ssm_c_re, v_ssm_c_im, v_ssm_d, v_ssm_glu_w, v_ssm_glu_b, v_w_attn_branch, v_w_ssm_branch, v_w_out, v_norm_ffn_g, v_w_ffn_in, v_w_ffn_out):
    w = dict(norm_mix_g=norm_mix_g, w_in=w_in, gate_bias=gate_bias, q_norm_g=q_norm_g, k_norm_g=k_norm_g,
             attn_sinks=attn_sinks, ssm_lambda_re=ssm_lambda_re, ssm_lambda_im=ssm_lambda_im, ssm_log_dt=ssm_log_dt,
             ssm_b_re=ssm_b_re, ssm_b_im=ssm_b_im, ssm_c_re=ssm_c_re, ssm_c_im=ssm_c_im, ssm_d=ssm_d,
             ssm_glu_w=ssm_glu_w, ssm_glu_b=ssm_glu_b, w_attn_branch=w_attn_branch, w_ssm_branch=w_ssm_branch,
             w_out=w_out, norm_ffn_g=norm_ffn_g, w_ffn_in=w_ffn_in, w_ffn_out=w_ffn_out)
    m = dict(norm_mix_g=m_norm_mix_g, w_in=m_w_in, gate_bias=m_gate_bias, q_norm_g=m_q_norm_g, k_norm_g=m_k_norm_g,
             attn_sinks=m_attn_sinks, ssm_lambda_re=m_ssm_lambda_re, ssm_lambda_im=m_ssm_lambda_im,
             ssm_log_dt=m_ssm_log_dt, ssm_b_re=m_ssm_b_re, ssm_b_im=m_ssm_b_im, ssm_c_re=m_ssm_c_re,
             ssm_c_im=m_ssm_c_im, ssm_d=m_ssm_d, ssm_glu_w=m_ssm_glu_w, ssm_glu_b=m_ssm_glu_b,
             w_attn_branch=m_w_attn_branch, w_ssm_branch=m_w_ssm_branch, w_out=m_w_out, norm_ffn_g=m_norm_ffn_g,
             w_ffn_in=m_w_ffn_in, w_ffn_out=m_w_ffn_out)
    v = dict(norm_mix_g=v_norm_mix_g, w_in=v_w_in, gate_bias=v_gate_bias, q_norm_g=v_q_norm_g, k_norm_g=v_k_norm_g,
             attn_sinks=v_attn_sinks, ssm_lambda_re=v_ssm_lambda_re, ssm_lambda_im=v_ssm_lambda_im,
             ssm_log_dt=v_ssm_log_dt, ssm_b_re=v_ssm_b_re, ssm_b_im=v_ssm_b_im, ssm_c_re=v_ssm_c_re,
             ssm_c_im=v_ssm_c_im, ssm_d=v_ssm_d, ssm_glu_w=v_ssm_glu_w, ssm_glu_b=v_ssm_glu_b,
             w_attn_branch=v_w_attn_branch, w_ssm_branch=v_w_ssm_branch, w_out=v_w_out, norm_ffn_g=v_norm_ffn_g,
             w_ffn_in=v_w_ffn_in, w_ffn_out=v_w_ffn_out)
    n_layers = norm_mix_g.shape[0]
    d_model = x.shape[-1]
    seq = x.shape[1]

    exchanges = _Exchanges()
    params = [_layer_params(l, w) for l in range(n_layers)]
    gathers = {}

    def gather(l, group, deps=()):
        names = WEIGHT_GROUPS[group]
        bufs = [_cast_into_slot(w[n], l, name="cast%d_%s" % (l, n)) for n in names]
        steps = _gather_steps(bufs, "ag%d_%s" % (l, group), deps,
                              lambda got: params[l].update(_big_params(dict(zip(names, got)))))
        gathers[l, group] = steps
        return exchanges.launch(steps)

    w_in0 = _allgather_weights([_cast_into_slot(w["w_in"], 0, name="cast0_w_in")], name="ag0_in")[0]
    params[0].update(_big_params({"w_in": w_in0}))
    first_deps = (gather(0, "mixer", (w_in0,)), gather(0, "ffn", (w_in0,)))
    sizes = {n: (N_CHIPS, 1) + w[n].shape[1:] for n in BIG_WEIGHTS}
    dims = _dims(d_model, sizes, min(512, seq))

    def weights_of(l, h):
        if l == 0:
            return params[0], first_deps
        for group in WEIGHT_GROUPS:
            exchanges.advance(gathers[l, group], h)
        return params[l], ()

    def fwd_tick(l, point, arr):
        tokens = []
        if l == 0 and point in ("in", "attn"):
            tokens.append(exchanges.advance(gathers[0, "mixer"], arr))
        if l == 0 and point in ("ssm", "out"):
            tokens.append(exchanges.advance(gathers[0, "ffn"], arr))
        if l + 1 < n_layers and point == "attn":
            tokens += [gather(l + 1, group) for group in WEIGHT_GROUPS]
        if l + 1 < n_layers and point == "ffn_in":
            tokens += [exchanges.advance(gathers[l + 1, group], arr) for group in WEIGHT_GROUPS]
        return tuple(t for t in tokens if t is not None)

    reduced, ready = {}, []

    def bwd_tick(l, stage, arr, stage_grads):
        tokens = exchanges.advance_all(arr)
        names = tuple(stage_grads)

        def publish(halves):
            reduced.update({(l, n): h for n, h in zip(names, halves)})
            ready.append((l, names))

        grads4 = [stage_grads[n].reshape(N_CHIPS, -1, stage_grads[n].shape[-1]) for n in names]
        return tokens + (exchanges.launch(_reduce_steps(grads4, "rs%d_%s" % (l, stage), publish)),)

    sq, dx, grads = _local_step(x[0], loss_target[0], n_layers, weights_of, dims, fwd_tick, bwd_tick)
    loss = lax.psum(sq[0, 0], MESH_AXES) * (0.5 / d_model)

    small_shapes = [w[n].shape for n in SMALL_WEIGHTS]
    small_local = [jnp.stack([grads[l][n].reshape(w[n].shape[1:]) for l in range(n_layers)]) for n in SMALL_WEIGHTS]
    shared = []
    tokens = (exchanges.launch(_share_steps(_place_into_slot(_pack(small_local), name="place_small_grads"),
                                            "share_small_grads", shared.append)),)

    adam = {n: None for n in BIG_WEIGHTS}
    grad, delta, new_m, new_v = {}, {}, {}, {}
    after = dx
    while exchanges.running or ready or shared:
        for l, names in ready:
            for n in names:
                mine, sibling = reduced[l, n]
                adam[n] = _adamw_sharded(w[n], m[n], v[n], mine, sibling, l, adam[n], name="adamw%d_%s" % (l, n), deps=tokens)
                after = adam[n][1]
        del ready[:]
        if shared:
            small_sum = _sum_slots(shared.pop(), name="sum_small_grads")
            grad.update(zip(SMALL_WEIGHTS, _unpack(small_sum, small_shapes)))
            res = _adamw(_pack([w[n] for n in SMALL_WEIGHTS]), small_sum, _pack([m[n] for n in SMALL_WEIGHTS]),
                         _pack([v[n] for n in SMALL_WEIGHTS]), name="adamw_small")
            for out, packed in zip((delta, new_m, new_v), res):
                out.update(zip(SMALL_WEIGHTS, _unpack(packed, small_shapes)))
            after = res[0]
        tokens = exchanges.advance_all(after)
    for n in BIG_WEIGHTS:
        grad[n], delta[n], new_m[n], new_v[n] = adam[n]

    return (loss, dx[None], *[grad[n] for n in WEIGHT_NAMES], *[delta[n] for n in WEIGHT_NAMES],
            *[new_m[n] for n in WEIGHT_NAMES], *[new_v[n] for n in WEIGHT_NAMES])
```

```python
import functools
import math

import jax
import jax.numpy as jnp
from jax import lax
from jax.experimental import pallas as pl
from jax.experimental.pallas import tpu as pltpu

HEAD_DIM = 64
WINDOW = 128
SSM_GROUP_CH = 16
SSM_LANE_GROUPS = 8
RMS_EPS = 1e-6
ADAM_LR = 0.001
ADAM_B1 = 0.9
ADAM_B2 = 0.999
ADAM_EPS = 1e-08
ADAM_WD = 0.01
ADAM_STEP = 10
NEG_BIG = -1e30
MESH_AXES = ("x", "y", "c")
N_CHIPS = 4
N_DEV = 8
VMEM_LIMIT_BYTES = 56 * 1024 * 1024
BF16 = jnp.bfloat16
F32 = jnp.float32


def _cparams(*semantics):
    return pltpu.CompilerParams(dimension_semantics=semantics, vmem_limit_bytes=VMEM_LIMIT_BYTES)


def _pick(n, target, mult):
    if n <= target:
        return n
    best = None
    for d in range(mult, target + 1, mult):
        if n % d == 0:
            best = d
    assert best is not None, (n, target, mult)
    return best


def _rowmap(fn, ins, outs, *, rows, tm, ncol=1, name, deps=()):
    n_in = len(ins)
    nrow = rows // tm
    assert nrow * tm == rows

    in_specs = []
    for arr, kind, width, coloff in ins:
        if kind == "row":
            in_specs.append(pl.BlockSpec((tm, width), lambda j, i, o=coloff: (i, o + j)))
        elif kind == "vec":
            in_specs.append(pl.BlockSpec((1, width), lambda j, i, o=coloff: (0, o + j)))
        else:
            nd = arr.ndim
            in_specs.append(pl.BlockSpec(arr.shape, lambda j, i, nd=nd: (0,) * nd))
    out_specs, out_shapes = [], []
    for cols, dtype, kind, width in outs:
        if kind == "row":
            out_specs.append(pl.BlockSpec((tm, width), lambda j, i: (i, j)))
            out_shapes.append(jax.ShapeDtypeStruct((rows, cols), dtype))
        else:
            out_specs.append(pl.BlockSpec((1, width), lambda j, i: (0, j)))
            out_shapes.append(jax.ShapeDtypeStruct((1, cols), dtype))

    in_specs += [pl.BlockSpec(memory_space=pl.ANY)] * len(deps)

    def body(*refs):
        i = pl.program_id(1)
        res = fn(*[r[...] for r in refs[:n_in]])
        if not isinstance(res, (tuple, list)):
            res = (res,)
        for (cols, dtype, kind, width), ref, val in zip(outs, refs[n_in + len(deps):], res):
            if kind == "row":
                ref[...] = val.astype(ref.dtype)
            else:
                @pl.when(i == 0)
                def _():
                    ref[...] = jnp.zeros_like(ref)
                ref[...] += val.astype(ref.dtype)

    res = pl.pallas_call(
        body,
        out_shape=tuple(out_shapes),
        grid=(ncol, nrow),
        in_specs=in_specs,
        out_specs=tuple(out_specs),
        compiler_params=_cparams("parallel", "arbitrary"),
        name=name,
    )(*[a[0] for a in ins], *deps)
    return res


def _mm_body(dims, nk, has_add, unused_in=0):
    def body(*refs):
        if has_add:
            a_ref, b_ref, add_ref = refs[:3]
            o_ref = refs[3 + unused_in]
            rest = refs[4 + unused_in:]
        else:
            a_ref, b_ref = refs[:2]
            o_ref = refs[2 + unused_in]
            add_ref = None
            rest = refs[3 + unused_in:]
        part = lax.dot_general(a_ref[...], b_ref[...], (dims, ((), ())), preferred_element_type=F32)
        if nk == 1:
            if add_ref is not None:
                part = part + add_ref[...]
            o_ref[...] = part.astype(o_ref.dtype)
        else:
            acc_ref = rest[0]
            k = pl.program_id(2)

            @pl.when(k == 0)
            def _():
                acc_ref[...] = part

            @pl.when(k > 0)
            def _():
                acc_ref[...] += part

            @pl.when(k == nk - 1)
            def _():
                r = acc_ref[...]
                if add_ref is not None:
                    r = r + add_ref[...]
                o_ref[...] = r.astype(o_ref.dtype)
    return body


class _Weight:
    def __init__(self, arr, layer, kind):
        self.arr, self.layer, self.kind = arr, layer, kind
        self.s, _, self.r, self.c = arr.shape
        self.rows = self.r * (self.s if kind == "row" else 1)
        self.cols = self.c * (self.s if kind == "col" else 1)

    def tiles(self, tr, tc):
        return _pick(self.r, tr, 128), _pick(self.c, tc, 128)

    def index(self, tr, tc):
        layer = self.layer
        if self.kind == "col":
            per = self.c // tc
            return lambda rb, cb: (cb // per, layer, rb, cb % per)
        per = self.r // tr
        return lambda rb, cb: (rb // per, layer, rb % per, cb)


def _shard_index(kind, r, c, tr, tc):
    if kind == "col":
        per = c // tc
        return lambda rb, cb: (cb // per, rb, cb % per)
    per = r // tr
    return lambda rb, cb: (rb // per, rb % per, cb)


def _mm_nn(a, w, *, out_dtype, tm, tn, tk, name, add=None, deps=()):
    m, k = a.shape
    assert k == w.rows
    tm = _pick(m, tm, 16)
    tk, tn = w.tiles(tk, tn)
    nk = k // tk
    widx = w.index(tk, tn)
    in_specs = [pl.BlockSpec((tm, tk), lambda n, i, kk: (i, kk)),
                pl.BlockSpec((None, None, tk, tn), lambda n, i, kk: widx(kk, n))]
    args = [a, w.arr]
    if add is not None:
        in_specs.append(pl.BlockSpec((tm, tn), lambda n, i, kk: (i, n)))
        args.append(add)
    in_specs += [pl.BlockSpec(memory_space=pl.ANY)] * len(deps)
    args += list(deps)
    return pl.pallas_call(
        _mm_body(((1,), (0,)), nk, add is not None, unused_in=len(deps)),
        out_shape=jax.ShapeDtypeStruct((m, w.cols), out_dtype),
        grid=(w.cols // tn, m // tm, nk),
        in_specs=in_specs,
        out_specs=pl.BlockSpec((tm, tn), lambda n, i, kk: (i, n)),
        scratch_shapes=[pltpu.VMEM((tm, tn), F32)] if nk > 1 else [],
        compiler_params=_cparams("parallel", "parallel", "arbitrary"),
        name=name,
    )(*args)


def _mm_nt(a, w, *, out_dtype, tm, tn, tko, name, deps=()):
    m, n = a.shape
    assert n == w.cols
    tm = _pick(m, tm, 16)
    if w.kind == "col" and w.s > 1:
        tko = _pick(w.r, tko, 128)
        layer, nsh, width = w.layer, w.s, w.c

        def body(a_ref, w_ref, *rest):
            o_ref = rest[len(deps)]
            acc = None
            for s in range(nsh):
                part = lax.dot_general(a_ref[:, s * width:(s + 1) * width], w_ref[s], (((1,), (1,)), ((), ())),
                                       preferred_element_type=F32)
                acc = part if acc is None else acc + part
            o_ref[...] = acc.astype(o_ref.dtype)

        return pl.pallas_call(
            body,
            out_shape=jax.ShapeDtypeStruct((m, w.rows), out_dtype),
            grid=(w.rows // tko, m // tm),
            in_specs=[pl.BlockSpec((tm, n), lambda ko, i: (i, 0)),
                      pl.BlockSpec((nsh, None, tko, width), lambda ko, i: (0, layer, ko, 0))]
            + [pl.BlockSpec(memory_space=pl.ANY)] * len(deps),
            out_specs=pl.BlockSpec((tm, tko), lambda ko, i: (i, ko)),
            compiler_params=_cparams("parallel", "parallel"),
            name=name,
        )(a, w.arr, *deps)
    tko, tn = w.tiles(tko, tn)
    nk = n // tn
    widx = w.index(tko, tn)
    return pl.pallas_call(
        _mm_body(((1,), (1,)), nk, False, unused_in=len(deps)),
        out_shape=jax.ShapeDtypeStruct((m, w.rows), out_dtype),
        grid=(w.rows // tko, m // tm, nk),
        in_specs=[pl.BlockSpec((tm, tn), lambda ko, i, nn: (i, nn)),
                  pl.BlockSpec((None, None, tko, tn), lambda ko, i, nn: widx(ko, nn))]
        + [pl.BlockSpec(memory_space=pl.ANY)] * len(deps),
        out_specs=pl.BlockSpec((tm, tko), lambda ko, i, nn: (i, ko)),
        scratch_shapes=[pltpu.VMEM((tm, tko), F32)] if nk > 1 else [],
        compiler_params=_cparams("parallel", "parallel", "arbitrary"),
        name=name,
    )(a, w.arr, *deps)


def _mm_tn(a, c, w, *, into, tm, tn, tko, name):
    m, k = a.shape
    mc, n = c.shape
    assert mc == m and k == w.rows and n == w.cols
    tm = _pick(m, tm, 16)
    tko, tn = w.tiles(tko, tn)
    nk = m // tm
    oidx = _shard_index(w.kind, w.r, w.c, tko, tn)
    layer = w.layer
    in_specs = [pl.BlockSpec((tm, tko), lambda ko, nn, mm: (mm, ko)),
                pl.BlockSpec((tm, tn), lambda ko, nn, mm: (mm, nn))]
    args = [a, c]
    if into is not None:
        in_specs.append(pl.BlockSpec(memory_space=pl.ANY))
        args.append(into)
    return pl.pallas_call(
        _mm_body(((0,), (0,)), nk, False, unused_in=len(args) - 2),
        out_shape=jax.ShapeDtypeStruct((w.arr.shape[1], w.s, w.r, w.c), F32),
        grid=(k // tko, n // tn, nk),
        in_specs=in_specs,
        out_specs=pl.BlockSpec((None, None, tko, tn), lambda ko, nn, mm: (layer,) + oidx(ko, nn)),
        scratch_shapes=[pltpu.VMEM((tko, tn), F32)] if nk > 1 else [],
        input_output_aliases={2: 0} if into is not None else {},
        compiler_params=_cparams("parallel", "parallel", "arbitrary"),
        name=name,
    )(*args)


def _rms(x, g):
    r = lax.rsqrt(jnp.mean(x * x, axis=-1, keepdims=True) + RMS_EPS)
    return x * r * g, r


def _rms_bwd(x, r, g, dy):
    dg = jnp.sum(dy * x * r, axis=0, keepdims=True)
    t = dy * g
    dx = r * t - x * (r * r * r) * jnp.mean(t * x, axis=-1, keepdims=True)
    return dx, dg


def _attn_consts(n_q, n_kv, sinks):
    group = n_q // n_kv
    t = jnp.arange(WINDOW, dtype=jnp.int32)[:, None]
    s = jnp.arange(2 * WINDOW, dtype=jnp.int32)[None, :] - WINDOW
    dist = (t - s).astype(F32)
    valid = (dist >= 0) & (dist < WINDOW)
    slopes = jnp.exp2(-8.0 * jnp.arange(1, n_q + 1, dtype=F32) / n_q)
    bias = jnp.where(valid[None], -slopes[:, None, None] * dist[None], NEG_BIG)
    bias = bias.reshape(n_kv, group * WINDOW, 2 * WINDOW)
    sink = jnp.broadcast_to(sinks.astype(F32).reshape(n_kv, group, 1, 1), (n_kv, group, WINDOW, 1))
    return bias, sink.reshape(n_kv, group * WINDOW, 1)


def _attn_probs(q_ref, kc_ref, kp_ref, vc_ref, vp_ref, qg, kg, sink, bias, first_mask, kv, group):
    sl = slice(kv * HEAD_DIM, (kv + 1) * HEAD_DIM)
    k2 = jnp.concatenate([kp_ref[:, sl], kc_ref[:, sl]], axis=0)
    v2 = jnp.concatenate([vp_ref[:, sl], vc_ref[:, sl]], axis=0)
    k2n, rk = _rms(k2, kg)
    qx, qn, rq = [], [], []
    for g in range(group):
        h = kv * group + g
        x = q_ref[:, h * HEAD_DIM:(h + 1) * HEAD_DIM]
        y, r = _rms(x, qg)
        qx.append(x); qn.append(y); rq.append(r)
    qs = jnp.concatenate(qn, axis=0).astype(BF16)
    k2b = k2n.astype(BF16)
    s = lax.dot_general(qs, k2b, (((1,), (1,)), ((), ())), preferred_element_type=F32) * (HEAD_DIM ** -0.5)
    s = jnp.where(first_mask, NEG_BIG, s + bias)
    m = jnp.maximum(jnp.max(s, axis=-1, keepdims=True), sink)
    p = jnp.exp(s - m)
    esink = jnp.exp(sink - m)
    denom = jnp.sum(p, axis=-1, keepdims=True) + esink
    pn = p / denom
    return dict(k2=k2, rk=rk, k2b=k2b, v2b=v2.astype(BF16), qx=qx, rq=rq, qs=qs, pn=pn, psink=esink / denom)


def _attn_specs(n_q, n_kv):
    aw, kvw = n_q * HEAD_DIM, n_kv * HEAD_DIM
    group = n_q // n_kv
    kblk, vblk = aw // kvw, aw // kvw + 1

    def specs(nb):
        cur = lambda n: jnp.minimum(n, nb - 1)
        prev = lambda n: jnp.maximum(jnp.minimum(n, nb - 1) - 1, 0)
        return [
            pl.BlockSpec((WINDOW, aw), lambda n: (cur(n), 0)),
            pl.BlockSpec((WINDOW, kvw), lambda n: (cur(n), kblk)),
            pl.BlockSpec((WINDOW, kvw), lambda n: (prev(n), kblk)),
            pl.BlockSpec((WINDOW, kvw), lambda n: (cur(n), vblk)),
            pl.BlockSpec((WINDOW, kvw), lambda n: (prev(n), vblk)),
        ]
    const_specs = [
        pl.BlockSpec((1, HEAD_DIM), lambda n: (0, 0)),
        pl.BlockSpec((1, HEAD_DIM), lambda n: (0, 0)),
        pl.BlockSpec((n_kv, group * WINDOW, 1), lambda n: (0, 0, 0)),
        pl.BlockSpec((n_kv, group * WINDOW, 2 * WINDOW), lambda n: (0, 0, 0)),
    ]
    return specs, const_specs


def _attn_fwd(z, qg, kg, sinks, *, n_q, n_kv, name, deps=()):
    L = z.shape[0]
    nb = L // WINDOW
    aw = n_q * HEAD_DIM
    group = n_q // n_kv
    bias, sink = _attn_consts(n_q, n_kv, sinks)
    specs, const_specs = _attn_specs(n_q, n_kv)

    def body(q_ref, kc_ref, kp_ref, vc_ref, vp_ref, qg_ref, kg_ref, sink_ref, bias_ref, *rest):
        o_ref = rest[-1]
        n = pl.program_id(0)
        col = lax.broadcasted_iota(jnp.int32, (group * WINDOW, 2 * WINDOW), 1)
        first_mask = jnp.logical_and(n == 0, col < WINDOW)
        for kv in range(n_kv):
            a = _attn_probs(q_ref, kc_ref, kp_ref, vc_ref, vp_ref, qg_ref[...], kg_ref[...],
                            sink_ref[kv], bias_ref[kv], first_mask, kv, group)
            o = jnp.dot(a["pn"].astype(BF16), a["v2b"], preferred_element_type=F32)
            for g in range(group):
                h = kv * group + g
                o_ref[:, h * HEAD_DIM:(h + 1) * HEAD_DIM] = o[g * WINDOW:(g + 1) * WINDOW].astype(o_ref.dtype)

    return pl.pallas_call(
        body,
        out_shape=jax.ShapeDtypeStruct((L, aw), BF16),
        grid=(nb,),
        in_specs=specs(nb) + const_specs + [pl.BlockSpec(memory_space=pl.ANY)] * len(deps),
        out_specs=pl.BlockSpec((WINDOW, aw), lambda n: (n, 0)),
        compiler_params=_cparams("parallel"),
        name=name,
    )(z, z, z, z, z, qg, kg, sink, bias, *deps)


def _attn_bwd(z, do, qg, kg, sinks, *, n_q, n_kv, name):
    L = z.shape[0]
    nb = L // WINDOW
    aw, kvw = n_q * HEAD_DIM, n_kv * HEAD_DIM
    group = n_q // n_kv
    bias, sink = _attn_consts(n_q, n_kv, sinks)
    specs, const_specs = _attn_specs(n_q, n_kv)
    scale = HEAD_DIM ** -0.5

    def body(q_ref, kc_ref, kp_ref, vc_ref, vp_ref, do_ref, qg_ref, kg_ref, sink_ref, bias_ref,
             dq_ref, dkv_ref, dqg_ref, dkg_ref, dsink_ref, carry_ref):
        n = pl.program_id(0)

        @pl.when(n == 0)
        def _():
            dqg_ref[...] = jnp.zeros_like(dqg_ref)
            dkg_ref[...] = jnp.zeros_like(dkg_ref)
            dsink_ref[...] = jnp.zeros_like(dsink_ref)
            carry_ref[...] = jnp.zeros_like(carry_ref)

        @pl.when(n < nb)
        def _():
            col = lax.broadcasted_iota(jnp.int32, (group * WINDOW, 2 * WINDOW), 1)
            first_mask = jnp.logical_and(n == 0, col < WINDOW)
            head_lane = lax.broadcasted_iota(jnp.int32, (1, n_q), 1)
            qg, kg = qg_ref[...], kg_ref[...]
            dqg = jnp.zeros((1, HEAD_DIM), F32)
            dkg = jnp.zeros((1, HEAD_DIM), F32)
            dsink = jnp.zeros((1, n_q), F32)
            for kv in range(n_kv):
                a = _attn_probs(q_ref, kc_ref, kp_ref, vc_ref, vp_ref, qg, kg,
                                sink_ref[kv], bias_ref[kv], first_mask, kv, group)
                pn = a["pn"]
                dos = jnp.concatenate(
                    [do_ref[:, (kv * group + g) * HEAD_DIM:(kv * group + g + 1) * HEAD_DIM] for g in range(group)],
                    axis=0).astype(BF16)
                dpn = lax.dot_general(dos, a["v2b"], (((1,), (1,)), ((), ())), preferred_element_type=F32)
                dv2 = lax.dot_general(pn.astype(BF16), dos, (((0,), (0,)), ((), ())), preferred_element_type=F32)
                delta = jnp.sum(pn * dpn, axis=-1, keepdims=True)
                ds = (pn * (dpn - delta)).astype(BF16)
                dsk = -a["psink"] * delta
                dqn = lax.dot_general(ds, a["k2b"], (((1,), (0,)), ((), ())), preferred_element_type=F32) * scale
                dk2n = lax.dot_general(ds, a["qs"], (((0,), (0,)), ((), ())), preferred_element_type=F32) * scale
                for g in range(group):
                    h = kv * group + g
                    rows = slice(g * WINDOW, (g + 1) * WINDOW)
                    dx, dgq = _rms_bwd(a["qx"][g], a["rq"][g], qg, dqn[rows])
                    dq_ref[:, h * HEAD_DIM:(h + 1) * HEAD_DIM] = dx.astype(dq_ref.dtype)
                    dqg = dqg + dgq
                    dsink = dsink + jnp.where(head_lane == h, jnp.sum(dsk[rows], axis=0, keepdims=True), 0.0)
                dk2, dgk = _rms_bwd(a["k2"], a["rk"], kg, dk2n)
                dkg = dkg + dgk
                ksl = slice(kv * HEAD_DIM, (kv + 1) * HEAD_DIM)
                vsl = slice(kvw + kv * HEAD_DIM, kvw + (kv + 1) * HEAD_DIM)
                dkv_ref[:, ksl] = (carry_ref[:, ksl] + dk2[:WINDOW]).astype(dkv_ref.dtype)
                dkv_ref[:, vsl] = (carry_ref[:, vsl] + dv2[:WINDOW]).astype(dkv_ref.dtype)
                carry_ref[:, ksl] = dk2[WINDOW:]
                carry_ref[:, vsl] = dv2[WINDOW:]
            dqg_ref[...] += dqg
            dkg_ref[...] += dkg
            dsink_ref[...] += dsink

        @pl.when(n == nb)
        def _():
            dkv_ref[...] = carry_ref[...].astype(dkv_ref.dtype)

    in_specs = specs(nb) + [pl.BlockSpec((WINDOW, aw), lambda n: (jnp.minimum(n, nb - 1), 0))] + const_specs
    return pl.pallas_call(
        body,
        out_shape=(jax.ShapeDtypeStruct((L, aw), BF16), jax.ShapeDtypeStruct((L, 2 * kvw), BF16),
                   jax.ShapeDtypeStruct((1, HEAD_DIM), F32), jax.ShapeDtypeStruct((1, HEAD_DIM), F32),
                   jax.ShapeDtypeStruct((1, n_q), F32)),
        grid=(nb + 1,),
        in_specs=in_specs,
        out_specs=(pl.BlockSpec((WINDOW, aw), lambda n: (jnp.minimum(n, nb - 1), 0)),
                   pl.BlockSpec((WINDOW, 2 * kvw), lambda n: (jnp.maximum(n - 1, 0), 0)),
                   pl.BlockSpec((1, HEAD_DIM), lambda n: (0, 0)),
                   pl.BlockSpec((1, HEAD_DIM), lambda n: (0, 0)),
                   pl.BlockSpec((1, n_q), lambda n: (0, 0))),
        scratch_shapes=[pltpu.VMEM((WINDOW, 2 * kvw), F32)],
        compiler_params=_cparams("arbitrary"),
        name=name,
    )(z, z, z, z, z, do, qg, kg, sink, bias)


def _cmul(ar, ai, br, bi):
    return ar * br - ai * bi, ar * bi + ai * br


def _time_permutation(tc):
    r = jnp.arange(tc)
    src = (r % 8) * (tc // 8) + r // 8
    p = (src[:, None] == jnp.arange(tc)[None, :]).astype(BF16)
    return p, p.T


def _unpermute(pt, x):
    hi = x.astype(BF16)
    lo = (x - hi.astype(F32)).astype(BF16)
    return jnp.dot(pt, hi, preferred_element_type=F32) + jnp.dot(pt, lo, preferred_element_type=F32)


def _segment_scan(xr_ref, xi_ref, ar, ai, cr, ci, ng, reverse):
    n = ar.shape[-1]
    row = lax.broadcasted_iota(jnp.int32, (8, n), 0)
    seeded = 7 if reverse else 0
    a8r = jnp.broadcast_to(ar, (8, n))
    a8i = jnp.broadcast_to(ai, (8, n))
    rows_of = lambda g: pl.ds(pl.multiple_of(((ng - 1 - g) if reverse else g) * 8, 8), 8)

    def recur(g, s):
        rows = rows_of(g)
        sr = a8r * s[0] - a8i * s[1] + xr_ref[rows, :]
        si = a8r * s[1] + a8i * s[0] + xi_ref[rows, :]
        xr_ref[rows, :] = sr
        xi_ref[rows, :] = si
        return sr, si

    fr, fi = lax.fori_loop(0, ng, recur, (jnp.where(row == seeded, cr, 0.0), jnp.where(row == seeded, ci, 0.0)))
    pr, pi = ar, ai
    for _ in range(ng.bit_length() - 1):
        pr, pi = _cmul(pr, pi, pr, pi)
    for k in (1, 2, 4):
        keep = (row < 8 - k) if reverse else (row >= k)
        shift = (8 - k) if reverse else k
        mr, mi = jnp.where(keep, pr, 0.0), jnp.where(keep, pi, 0.0)
        tr, ti = pltpu.roll(fr, shift, 0), pltpu.roll(fi, shift, 0)
        fr, fi = fr + mr * tr - mi * ti, fi + mr * ti + mi * tr
        pr, pi = _cmul(pr, pi, pr, pi)
    shift = 7 if reverse else 1
    before_r, before_i = pltpu.roll(fr, shift, 0), pltpu.roll(fi, shift, 0)

    def inherit(g, d):
        rows = rows_of(g)
        dr = a8r * d[0] - a8i * d[1]
        di = a8r * d[1] + a8i * d[0]
        xr_ref[rows, :] = xr_ref[rows, :] + dr
        xi_ref[rows, :] = xi_ref[rows, :] + di
        return dr, di

    lax.fori_loop(0, ng, inherit, (jnp.where(row == seeded, 0.0, before_r), jnp.where(row == seeded, 0.0, before_i)))
    out = 0 if reverse else 7
    return (fr[out:out + 1], fi[out:out + 1],
            jnp.where(row == seeded, cr, before_r), jnp.where(row == seeded, ci, before_i))


def _blockdiag(x):
    g, a, b = x.shape
    j = g // SSM_LANE_GROUPS
    eye = jnp.eye(SSM_LANE_GROUPS, dtype=x.dtype)
    y = x.reshape(j, SSM_LANE_GROUPS, a, 1, b) * eye[None, :, None, :, None]
    return y.reshape(j, SSM_LANE_GROUPS * a, SSM_LANE_GROUPS * b)


def _blockdiag_extract(y, a, b):
    j = y.shape[0]
    y = y.reshape(j, SSM_LANE_GROUPS, a, SSM_LANE_GROUPS, b)
    return jnp.einsum("jgahb,gh->jgab", y, jnp.eye(SSM_LANE_GROUPS, dtype=y.dtype)).reshape(j * SSM_LANE_GROUPS, a, b)


def _ssm_disc(lr, li, ldt, brt, bit):
    dt = jnp.exp(ldt)
    mag = jnp.exp(lr * dt)
    ar = mag * jnp.cos(li * dt)
    ai = mag * jnp.sin(li * dt)
    den = lr * lr + li * li
    fr = ((ar - 1.0) * lr + ai * li) / den
    fi = (ai * lr - (ar - 1.0) * li) / den
    bbr = fr[:, None, :] * brt - fi[:, None, :] * bit
    bbi = fr[:, None, :] * bit + fi[:, None, :] * brt
    return ar, ai, bbr, bbi


def _ssm_prep(lr, li, ldt, brt, bit, *, name):
    g, h, p = brt.shape

    def body(lr_ref, li_ref, ldt_ref, brt_ref, bit_ref, ar_ref, ai_ref, bbr_ref, bbi_ref):
        ar, ai, bbr, bbi = _ssm_disc(lr_ref[...], li_ref[...], ldt_ref[...], brt_ref[...], bit_ref[...])
        ar_ref[...] = ar
        ai_ref[...] = ai
        bbr_ref[...] = bbr
        bbi_ref[...] = bbi

    gp = jax.ShapeDtypeStruct((g, p), F32)
    ghp = jax.ShapeDtypeStruct((g, h, p), F32)
    return pl.pallas_call(body, out_shape=(gp, gp, ghp, ghp), name=name)(lr, li, ldt, brt, bit)


def _ssm_prep_bwd(lr, li, ldt, brt, bit, dar, dai, dbbr, dbbi, *, name):
    g, h, p = brt.shape

    def body(lr_ref, li_ref, ldt_ref, brt_ref, bit_ref, dar_ref, dai_ref, dbbr_ref, dbbi_ref,
             dlr_ref, dli_ref, dldt_ref, dbrt_ref, dbit_ref):
        _, vjp = jax.vjp(_ssm_disc, lr_ref[...], li_ref[...], ldt_ref[...], brt_ref[...], bit_ref[...])
        dlr, dli, dldt, dbrt, dbit = vjp((dar_ref[...], dai_ref[...], dbbr_ref[...], dbbi_ref[...]))
        dlr_ref[...] = dlr
        dli_ref[...] = dli
        dldt_ref[...] = dldt
        dbrt_ref[...] = dbrt
        dbit_ref[...] = dbit

    gp = jax.ShapeDtypeStruct((g, p), F32)
    ghp = jax.ShapeDtypeStruct((g, h, p), F32)
    return pl.pallas_call(body, out_shape=(gp, gp, jax.ShapeDtypeStruct((g, 1), F32), ghp, ghp), name=name)(
        lr, li, ldt, brt, bit, dar, dai, dbbr, dbbi)


def _ssm_specs(tc, nlanes, nch, u_colblk, chunk_of):
    return [
        pl.BlockSpec((tc, nch), lambda j, c: (chunk_of(c), u_colblk + j)),
        pl.BlockSpec((1, nlanes), lambda j, c: (0, j)),
        pl.BlockSpec((1, nlanes), lambda j, c: (0, j)),
        pl.BlockSpec((None, nch, nlanes), lambda j, c: (j, 0, 0)),
        pl.BlockSpec((None, nch, nlanes), lambda j, c: (j, 0, 0)),
        pl.BlockSpec((None, nlanes, nch), lambda j, c: (j, 0, 0)),
        pl.BlockSpec((None, nlanes, nch), lambda j, c: (j, 0, 0)),
        pl.BlockSpec((1, nch), lambda j, c: (0, j)),
        pl.BlockSpec((tc, tc), lambda j, c: (0, 0)),
        pl.BlockSpec((tc, tc), lambda j, c: (0, 0)),
    ]


def _ssm_fwd(z, ar, ai, bblk_r, bblk_i, cblk_r, cblk_i, d, *, u_col, tc, name, deps=()):
    L = z.shape[0]
    nj, nch, nlanes = bblk_r.shape
    w = nj * nch
    nc = L // tc
    ng = tc // 8

    assert ng & (ng - 1) == 0
    perm, perm_t = _time_permutation(tc)

    def body(u_ref, ar_ref, ai_ref, br_ref, bi_ref, cr_ref, ci_ref, d_ref, p_ref, pt_ref, *rest):
        y_ref, s0r_ref, s0i_ref, xr_ref, xi_ref, carr_ref, cari_ref = rest[len(deps):]
        c = pl.program_id(1)

        @pl.when(c == 0)
        def _():
            carr_ref[...] = jnp.zeros_like(carr_ref)
            cari_ref[...] = jnp.zeros_like(cari_ref)

        s0r_ref[...] = carr_ref[...]
        s0i_ref[...] = cari_ref[...]
        u = u_ref[...]
        ub = jnp.dot(p_ref[...], u.astype(BF16), preferred_element_type=F32).astype(BF16)
        xr_ref[...] = jnp.dot(ub, br_ref[...].astype(BF16), preferred_element_type=F32)
        xi_ref[...] = jnp.dot(ub, bi_ref[...].astype(BF16), preferred_element_type=F32)
        cr, ci, _, _ = _segment_scan(xr_ref, xi_ref, ar_ref[...], ai_ref[...], carr_ref[...], cari_ref[...], ng, False)
        carr_ref[...] = cr
        cari_ref[...] = ci
        y = (jnp.dot(xr_ref[...].astype(BF16), cr_ref[...].astype(BF16), preferred_element_type=F32)
             - jnp.dot(xi_ref[...].astype(BF16), ci_ref[...].astype(BF16), preferred_element_type=F32))
        y_ref[...] = _unpermute(pt_ref[...], y) + d_ref[...] * u

    state = jax.ShapeDtypeStruct((nc, 1, nj * nlanes), F32)
    state_spec = pl.BlockSpec((None, 1, nlanes), lambda j, c: (c, 0, j))
    return pl.pallas_call(
        body,
        out_shape=(jax.ShapeDtypeStruct((L, w), F32), state, state),
        grid=(nj, nc),
        in_specs=_ssm_specs(tc, nlanes, nch, u_col // nch, lambda c: c) + [pl.BlockSpec(memory_space=pl.ANY)] * len(deps),
        out_specs=(pl.BlockSpec((tc, nch), lambda j, c: (c, j)), state_spec, state_spec),
        scratch_shapes=[pltpu.VMEM((tc, nlanes), F32), pltpu.VMEM((tc, nlanes), F32),
                        pltpu.VMEM((1, nlanes), F32), pltpu.VMEM((1, nlanes), F32)],
        compiler_params=_cparams("parallel", "arbitrary"),
        name=name,
    )(z, ar, ai, bblk_r, bblk_i, cblk_r, cblk_i, d, perm, perm_t, *deps)


def _ssm_bwd(z, dy, s0r, s0i, ar, ai, bblk_r, bblk_i, cblk_r, cblk_i, d, *, u_col, tc, name):
    L = z.shape[0]
    nj, nch, nlanes = bblk_r.shape
    w = nj * nch
    nc = L // tc
    ng = tc // 8
    chunk_of = lambda c: nc - 1 - c
    assert ng & (ng - 1) == 0
    perm, perm_t = _time_permutation(tc)

    def body(u_ref, ar_ref, ai_ref, br_ref, bi_ref, cr_ref, ci_ref, d_ref, p_ref, pt_ref, dy_ref, s0r_ref, s0i_ref,
             du_ref, dbr_ref, dbi_ref, dcr_ref, dci_ref, dar_ref, dai_ref, dd_ref,
             sr_ref, si_ref, lr_ref, li_ref, carr_ref, cari_ref):
        c = pl.program_id(1)

        @pl.when(c == 0)
        def _():
            for ref in (dbr_ref, dbi_ref, dcr_ref, dci_ref, dar_ref, dai_ref, dd_ref, carr_ref, cari_ref):
                ref[...] = jnp.zeros_like(ref)

        u = u_ref[...]
        dyv = dy_ref[...]
        both = jnp.dot(p_ref[...], jnp.concatenate([u.astype(BF16), dyv.astype(BF16)], axis=1), preferred_element_type=F32)
        ub = both[:, :nch].astype(BF16)
        dyb = both[:, nch:].astype(BF16)
        brb = br_ref[...].astype(BF16)
        bib = bi_ref[...].astype(BF16)
        crb = cr_ref[...].astype(BF16)
        cib = ci_ref[...].astype(BF16)
        a_r, a_i = ar_ref[...], ai_ref[...]

        sr_ref[...] = jnp.dot(ub, brb, preferred_element_type=F32)
        si_ref[...] = jnp.dot(ub, bib, preferred_element_type=F32)
        _, _, start_r, start_i = _segment_scan(sr_ref, si_ref, a_r, a_i, s0r_ref[...], s0i_ref[...], ng, False)

        nt = (((1,), (1,)), ((), ()))
        lr_ref[...] = lax.dot_general(dyb, crb, nt, preferred_element_type=F32)
        li_ref[...] = -lax.dot_general(dyb, cib, nt, preferred_element_type=F32)
        cr, ci, _, _ = _segment_scan(lr_ref, li_ref, a_r, -a_i, carr_ref[...], cari_ref[...], ng, True)
        carr_ref[...] = cr
        cari_ref[...] = ci

        def accumulate(g, carry):
            pr, pi, acc_r, acc_i = carry
            rows = pl.ds(pl.multiple_of(g * 8, 8), 8)
            lr, li = lr_ref[rows, :], li_ref[rows, :]
            return sr_ref[rows, :], si_ref[rows, :], acc_r + lr * pr + li * pi, acc_i + li * pr - lr * pi

        zero8 = jnp.zeros((8, nlanes), F32)
        _, _, acc_r, acc_i = lax.fori_loop(0, ng, accumulate, (start_r, start_i, zero8, zero8))
        dar_ref[...] += jnp.sum(acc_r, axis=0, keepdims=True)
        dai_ref[...] += jnp.sum(acc_i, axis=0, keepdims=True)

        tn = (((0,), (0,)), ((), ()))
        lrb = lr_ref[...].astype(BF16)
        lib = li_ref[...].astype(BF16)
        dcr_ref[...] += lax.dot_general(sr_ref[...].astype(BF16), dyb, tn, preferred_element_type=F32)
        dci_ref[...] -= lax.dot_general(si_ref[...].astype(BF16), dyb, tn, preferred_element_type=F32)
        dbr_ref[...] += lax.dot_general(ub, lrb, tn, preferred_element_type=F32)
        dbi_ref[...] += lax.dot_general(ub, lib, tn, preferred_element_type=F32)
        du = (lax.dot_general(lrb, brb, nt, preferred_element_type=F32)
              + lax.dot_general(lib, bib, nt, preferred_element_type=F32))
        du_ref[...] = (_unpermute(pt_ref[...], du) + d_ref[...] * dyv).astype(du_ref.dtype)
        dd_ref[...] += jnp.sum(dyv * u, axis=0, keepdims=True)

    state_spec = pl.BlockSpec((None, 1, nlanes), lambda j, c: (chunk_of(c), 0, j))
    bshape = jax.ShapeDtypeStruct((nj, nch, nlanes), F32)
    cshape = jax.ShapeDtypeStruct((nj, nlanes, nch), F32)
    ashape = jax.ShapeDtypeStruct((1, nj * nlanes), F32)
    bspec = pl.BlockSpec((None, nch, nlanes), lambda j, c: (j, 0, 0))
    cspec = pl.BlockSpec((None, nlanes, nch), lambda j, c: (j, 0, 0))
    aspec = pl.BlockSpec((1, nlanes), lambda j, c: (0, j))
    big = pltpu.VMEM((tc, nlanes), F32)
    return pl.pallas_call(
        body,
        out_shape=(jax.ShapeDtypeStruct((L, w), BF16), bshape, bshape, cshape, cshape, ashape, ashape,
                   jax.ShapeDtypeStruct((1, w), F32)),
        grid=(nj, nc),
        in_specs=_ssm_specs(tc, nlanes, nch, u_col // nch, chunk_of)
        + [pl.BlockSpec((tc, nch), lambda j, c: (chunk_of(c), j)), state_spec, state_spec],
        out_specs=(pl.BlockSpec((tc, nch), lambda j, c: (chunk_of(c), j)), bspec, bspec, cspec, cspec, aspec, aspec,
                   pl.BlockSpec((1, nch), lambda j, c: (0, j))),
        scratch_shapes=[big, big, big, big, pltpu.VMEM((1, nlanes), F32), pltpu.VMEM((1, nlanes), F32)],
        compiler_params=_cparams("parallel", "arbitrary"),
        name=name,
    )(z, ar, ai, bblk_r, bblk_i, cblk_r, cblk_i, d, perm, perm_t, dy, s0r, s0i)


def _rmsnorm_rows(x, g):
    return x * lax.rsqrt(jnp.mean(x * x, axis=-1, keepdims=True) + RMS_EPS) * g


def _glu_out(y_raw, pre, b):
    yg = jax.nn.gelu(y_raw)
    return yg * jax.nn.sigmoid(pre + b)


def _gate_merge(za, zs, ba, bs, a, bm):
    return jax.nn.sigmoid(za + ba) * a + jax.nn.sigmoid(zs + bs) * bm


def _swiglu(g, u):
    return jax.nn.silu(g) * u


def _row_tile(width_bytes_per_row, rows):
    budget = VMEM_LIMIT_BYTES // 3
    t = max(8, min(1024, budget // (2 * max(width_bytes_per_row, 1))))
    return _pick(rows, t, 16)


def _ssm_params(p, prefix):
    g, pst = p["lam_re"].shape
    ar, ai, bbr, bbi = _ssm_prep(p["lam_re"], p["lam_im"], p["log_dt"], p["b_re_t"], p["b_im_t"], name=prefix + "_ssm_prep")
    return dict(ar=ar.reshape(1, g * pst), ai=ai.reshape(1, g * pst),
                bblk_r=_blockdiag(bbr), bblk_i=_blockdiag(bbi),
                cblk_r=_blockdiag(jnp.swapaxes(p["c_re"], 1, 2)), cblk_i=_blockdiag(jnp.swapaxes(p["c_im"], 1, 2)))


def _layer_fwd(x, p, dims, prefix, deps=(), tick=None):
    tick = tick or (lambda point, arr: ())
    t, d = x.shape
    aw, kvw, sw, ff = dims["aw"], dims["kvw"], dims["sw"], dims["ff"]
    off_u = aw + 2 * kvw
    off_g = off_u + sw
    gblk = _pick(d, 512, 128)
    assert off_g % gblk == 0 and off_u % (SSM_LANE_GROUPS * SSM_GROUP_CH) == 0
    sv = {"x": x}

    h, = _rowmap(_rmsnorm_rows, [(x, "row", d, 0), (p["norm_mix_g"], "vec", d, 0)], [(d, BF16, "row", d)],
                 rows=t, tm=_row_tile(6 * d, t), name=prefix + "_norm_mix", deps=deps)
    z = _mm_nn(h, p["w_in"], out_dtype=F32, tm=1024, tn=1664, tk=2048, name=prefix + "_mm_in")
    ya = _attn_fwd(z, p["q_norm_g"], p["k_norm_g"], p["attn_sinks"], n_q=dims["n_q"], n_kv=dims["n_kv"],
                   name=prefix + "_attn_fwd", deps=tick("in", z))
    sp = _ssm_params(p, prefix)
    y_raw, s0r, s0i = _ssm_fwd(z, sp["ar"], sp["ai"], sp["bblk_r"], sp["bblk_i"], sp["cblk_r"], sp["cblk_i"], p["ssm_d"],
                               u_col=off_u, tc=dims["tc"], name=prefix + "_ssm_fwd", deps=tick("attn", ya))
    yg, = _rowmap(jax.nn.gelu, [(y_raw, "row", sw, 0)], [(sw, BF16, "row", sw)],
                  rows=t, tm=_row_tile(6 * sw, t), name=prefix + "_gelu", deps=tick("ssm", y_raw))
    pre = _mm_nn(yg, p["ssm_glu_w"], out_dtype=F32, tm=1024, tn=1024, tk=1024, name=prefix + "_mm_glu")
    y2, = _rowmap(_glu_out, [(y_raw, "row", sw, 0), (pre, "row", sw, 0), (p["ssm_glu_b"], "vec", sw, 0)],
                  [(sw, BF16, "row", sw)], rows=t, tm=_row_tile(10 * sw, t), name=prefix + "_glu_out")
    a = _mm_nn(ya, p["w_attn_branch"], out_dtype=F32, tm=1024, tn=512, tk=1024, name=prefix + "_mm_ab")
    bm = _mm_nn(y2, p["w_ssm_branch"], out_dtype=F32, tm=1024, tn=512, tk=1024, name=prefix + "_mm_sb")
    ngb = d // gblk
    merged, = _rowmap(
        _gate_merge,
        [(z, "row", gblk, off_g // gblk), (z, "row", gblk, off_g // gblk + ngb),
         (p["gate_bias"], "vec", gblk, 0), (p["gate_bias"], "vec", gblk, ngb),
         (a, "row", gblk, 0), (bm, "row", gblk, 0)],
        [(d, BF16, "row", gblk)], rows=t, tm=_row_tile(18 * gblk, t), ncol=ngb, name=prefix + "_gate")
    x1 = _mm_nn(merged, p["w_out"], out_dtype=F32, tm=512, tn=1024, tk=2048, name=prefix + "_mm_out", add=x)
    h2, = _rowmap(_rmsnorm_rows, [(x1, "row", d, 0), (p["norm_ffn_g"], "vec", d, 0)], [(d, BF16, "row", d)],
                  rows=t, tm=_row_tile(6 * d, t), name=prefix + "_norm_ffn", deps=tick("out", x1))
    gu = _mm_nn(h2, p["w_ffn_in"], out_dtype=F32, tm=1024, tn=1408, tk=2048, name=prefix + "_mm_ffn_in")
    fblk = _pick(ff, 1408, 128)
    nfb = ff // fblk
    act, = _rowmap(_swiglu, [(gu, "row", fblk, 0), (gu, "row", fblk, nfb)], [(ff, BF16, "row", fblk)],
                   rows=t, tm=_row_tile(10 * fblk, t), ncol=nfb, name=prefix + "_swiglu", deps=tick("ffn_in", gu))
    x2 = _mm_nn(act, p["w_ffn_out"], out_dtype=F32, tm=512, tn=512, tk=5632, name=prefix + "_mm_ffn_out", add=x1)
    sv.update(h=h, z=z, ya=ya, sp=sp, y_raw=y_raw, s0r=s0r, s0i=s0i, yg=yg, pre=pre, y2=y2, a=a, bm=bm,
              merged=merged, x1=x1, h2=h2, gu=gu, act=act)
    return x2, sv


def _layer_bwd(dx2, dx2b, sv, p, dims, prefix, gbuf, deps=(), before_mixer=None, before_in=None):
    t, d = dx2.shape
    aw, kvw, sw, ff = dims["aw"], dims["kvw"], dims["sw"], dims["ff"]
    off_u = aw + 2 * kvw
    off_g = off_u + sw
    gblk = _pick(d, 512, 128)
    ngb = d // gblk
    fblk = _pick(ff, 1408, 128)
    nfb = ff // fblk
    g = {}

    dact = _mm_nt(dx2b, p["w_ffn_out"], out_dtype=F32, tm=512, tn=2048, tko=1408, name=prefix + "_mm_dact", deps=deps)
    g["w_ffn_out"] = _mm_tn(sv["act"], dx2b, p["w_ffn_out"], into=gbuf.get("w_ffn_out"), tm=4096, tn=1024, tko=512,
                            name=prefix + "_mm_dw_ffn_out")

    def swiglu_bwd(gg, uu, da):
        _, vjp = jax.vjp(_swiglu, gg, uu)
        return vjp(da)

    dgu_g, dgu_u = _rowmap(swiglu_bwd, [(sv["gu"], "row", fblk, 0), (sv["gu"], "row", fblk, nfb), (dact, "row", fblk, 0)],
                           [(ff, BF16, "row", fblk), (ff, BF16, "row", fblk)],
                           rows=t, tm=_row_tile(16 * fblk, t), ncol=nfb, name=prefix + "_swiglu_bwd")
    dgu = jnp.concatenate([dgu_g, dgu_u], axis=1)
    dh2 = _mm_nt(dgu, p["w_ffn_in"], out_dtype=F32, tm=256, tn=1408, tko=512, name=prefix + "_mm_dh2")
    g["w_ffn_in"] = _mm_tn(sv["h2"], dgu, p["w_ffn_in"], into=gbuf.get("w_ffn_in"), tm=4096, tn=1408, tko=512,
                           name=prefix + "_mm_dw_ffn_in")

    def norm_bwd(xx, gg, dh, dres):
        _, vjp = jax.vjp(_rmsnorm_rows, xx, gg)
        dxx, dgg = vjp(dh)
        dxx = dxx + dres
        return dxx, dxx, dgg

    dx1, dx1b, g["norm_ffn_g"] = _rowmap(
        norm_bwd, [(sv["x1"], "row", d, 0), (p["norm_ffn_g"], "vec", d, 0), (dh2, "row", d, 0), (dx2, "row", d, 0)],
        [(d, F32, "row", d), (d, BF16, "row", d), (d, F32, "acc", d)],
        rows=t, tm=_row_tile(22 * d, t), name=prefix + "_norm_ffn_bwd")

    deps = before_mixer(dx1, g) if before_mixer else ()
    dmerged = _mm_nt(dx1b, p["w_out"], out_dtype=F32, tm=1024, tn=2048, tko=1024, name=prefix + "_mm_dmerged", deps=deps)
    g["w_out"] = _mm_tn(sv["merged"], dx1b, p["w_out"], into=gbuf.get("w_out"), tm=4096, tn=1024, tko=512,
                        name=prefix + "_mm_dw_out")

    def gate_bwd(za, zs, ba, bs, aa, bb, dm):
        _, vjp = jax.vjp(_gate_merge, za, zs, ba, bs, aa, bb)
        dza, dzs, dba, dbs, daa, dbb = vjp(dm)
        return daa, dbb, dza, dzs, dba, dbs

    z = sv["z"]
    da, dbm, dza, dzs, dba, dbs = _rowmap(
        gate_bwd,
        [(z, "row", gblk, off_g // gblk), (z, "row", gblk, off_g // gblk + ngb),
         (p["gate_bias"], "vec", gblk, 0), (p["gate_bias"], "vec", gblk, ngb),
         (sv["a"], "row", gblk, 0), (sv["bm"], "row", gblk, 0), (dmerged, "row", gblk, 0)],
        [(d, BF16, "row", gblk), (d, BF16, "row", gblk), (d, BF16, "row", gblk), (d, BF16, "row", gblk),
         (d, F32, "acc", gblk), (d, F32, "acc", gblk)],
        rows=t, tm=_row_tile(32 * gblk, t), ncol=ngb, name=prefix + "_gate_bwd")
    g["gate_bias"] = jnp.concatenate([dba, dbs], axis=1)
    dya = _mm_nt(da, p["w_attn_branch"], out_dtype=F32, tm=1024, tn=512, tko=1024, name=prefix + "_mm_dya")
    g["w_attn_branch"] = _mm_tn(sv["ya"], da, p["w_attn_branch"], into=gbuf.get("w_attn_branch"), tm=4096, tn=512,
                                tko=512, name=prefix + "_mm_dw_ab")
    dy2 = _mm_nt(dbm, p["w_ssm_branch"], out_dtype=F32, tm=1024, tn=512, tko=1024, name=prefix + "_mm_dy2")
    g["w_ssm_branch"] = _mm_tn(sv["y2"], dbm, p["w_ssm_branch"], into=gbuf.get("w_ssm_branch"), tm=4096, tn=512,
                               tko=512, name=prefix + "_mm_dw_sb")

    def glu_bwd(y_raw, pre, b, dy):
        yg = jax.nn.gelu(y_raw)
        _, vjp = jax.vjp(lambda a_, b_, c_: a_ * jax.nn.sigmoid(b_ + c_), yg, pre, b)
        dyg, dpre, db = vjp(dy)
        return dyg, dpre, db

    dyg_direct, dpre, g["ssm_glu_b"] = _rowmap(
        glu_bwd, [(sv["y_raw"], "row", sw, 0), (sv["pre"], "row", sw, 0), (p["ssm_glu_b"], "vec", sw, 0), (dy2, "row", sw, 0)],
        [(sw, F32, "row", sw), (sw, BF16, "row", sw), (sw, F32, "acc", sw)],
        rows=t, tm=_row_tile(24 * sw, t), name=prefix + "_glu_bwd")
    dyg2 = _mm_nt(dpre, p["ssm_glu_w"], out_dtype=F32, tm=1024, tn=1024, tko=1024, name=prefix + "_mm_dyg")
    g["ssm_glu_w"] = _mm_tn(sv["yg"], dpre, p["ssm_glu_w"], into=gbuf.get("ssm_glu_w"), tm=4096, tn=1024, tko=512,
                            name=prefix + "_mm_dw_glu")

    def gelu_bwd(y_raw, d1, d2):
        _, vjp = jax.vjp(jax.nn.gelu, y_raw)
        return vjp(d1 + d2)[0]

    dy_raw, = _rowmap(gelu_bwd, [(sv["y_raw"], "row", sw, 0), (dyg_direct, "row", sw, 0), (dyg2, "row", sw, 0)],
                      [(sw, F32, "row", sw)], rows=t, tm=_row_tile(20 * sw, t), name=prefix + "_gelu_bwd")
    sp = sv["sp"]
    du, dbr, dbi, dcr, dci, dar, dai, g["ssm_d"] = _ssm_bwd(
        z, dy_raw, sv["s0r"], sv["s0i"], sp["ar"], sp["ai"], sp["bblk_r"], sp["bblk_i"], sp["cblk_r"], sp["cblk_i"],
        p["ssm_d"], u_col=off_u, tc=dims["tc"], name=prefix + "_ssm_bwd")
    ngr, pst = p["lam_re"].shape
    hch = SSM_GROUP_CH
    dlr, dli, dldt, dbrt, dbit = _ssm_prep_bwd(
        p["lam_re"], p["lam_im"], p["log_dt"], p["b_re_t"], p["b_im_t"],
        dar.reshape(ngr, pst), dai.reshape(ngr, pst), _blockdiag_extract(dbr, hch, pst), _blockdiag_extract(dbi, hch, pst),
        name=prefix + "_ssm_prep_bwd")
    g.update(ssm_lambda_re=dlr, ssm_lambda_im=dli, ssm_log_dt=dldt.reshape(ngr),
             ssm_b_re=jnp.swapaxes(dbrt, 1, 2), ssm_b_im=jnp.swapaxes(dbit, 1, 2),
             ssm_c_re=jnp.swapaxes(_blockdiag_extract(dcr, pst, hch), 1, 2),
             ssm_c_im=jnp.swapaxes(_blockdiag_extract(dci, pst, hch), 1, 2))

    dq, dkv, g["q_norm_g"], g["k_norm_g"], g["attn_sinks"] = _attn_bwd(
        z, dya, p["q_norm_g"], p["k_norm_g"], p["attn_sinks"], n_q=dims["n_q"], n_kv=dims["n_kv"], name=prefix + "_attn_bwd")

    dz = jnp.concatenate([dq, dkv, du, dza, dzs], axis=1)
    deps = before_in(dq, g) if before_in else ()
    dh = _mm_nt(dz, p["w_in"], out_dtype=F32, tm=512, tn=1664, tko=512, name=prefix + "_mm_dh", deps=deps)
    g["w_in"] = _mm_tn(sv["h"], dz, p["w_in"], into=gbuf.get("w_in"), tm=4096, tn=1664, tko=512,
                       name=prefix + "_mm_dw_in")
    dx, dxb, g["norm_mix_g"] = _rowmap(
        norm_bwd, [(sv["x"], "row", d, 0), (p["norm_mix_g"], "vec", d, 0), (dh, "row", d, 0), (dx1, "row", d, 0)],
        [(d, F32, "row", d), (d, BF16, "row", d), (d, F32, "acc", d)],
        rows=t, tm=_row_tile(22 * d, t), name=prefix + "_norm_mix_bwd")
    return dx, dxb, g


def _loss_and_grad(y, target):
    t, d = y.shape

    def fn(yy, tt):
        e = yy - tt
        dy = e * (1.0 / d)
        return dy, dy, jnp.sum(e * e, keepdims=True).reshape(1, 1)

    dy, dyb, sq = _rowmap(fn, [(y, "row", d, 0), (target, "row", d, 0)],
                          [(d, F32, "row", d), (d, BF16, "row", d), (1, F32, "acc", 1)],
                          rows=t, tm=_row_tile(14 * d, t), name="loss")
    return sq, dy, dyb


def _local_step(x, target, n_layers, weights_of, dims, fwd_tick=None, bwd_tick=None):
    saved, params = [], []
    h = x
    for l in range(n_layers):
        p, deps = weights_of(l, h)
        params.append(p)
        tick = (lambda point, arr, l=l: fwd_tick(l, point, arr)) if fwd_tick else None
        h, sv = _layer_fwd(h, p, dims, "l%d" % l, deps, tick)
        saved.append(sv)
    sq, dy, dyb = _loss_and_grad(h, target)
    grads = [None] * n_layers
    deps = ()
    for l in reversed(range(n_layers)):
        if bwd_tick:
            ffn_done = lambda arr, g, l=l: bwd_tick(l, "ffn", arr, {n: g[n] for n in ("w_ffn_in", "w_ffn_out")})
            mixer_done = lambda arr, g, l=l: bwd_tick(l, "mixer", arr, {n: g[n] for n in MIXER_WEIGHTS})
        else:
            ffn_done = mixer_done = None
        dy, dyb, grads[l] = _layer_bwd(dy, dyb, saved[l], params[l], dims, "l%d" % l, {}, deps=deps,
                                       before_mixer=ffn_done, before_in=mixer_done)
        if bwd_tick:
            deps = bwd_tick(l, "in", dy, {"w_in": grads[l]["w_in"]})
    return sq, dy, grads


COL_SHARDED = ("w_in", "w_attn_branch", "w_ssm_branch", "w_ffn_in")
ROW_SHARDED = ("ssm_glu_w", "w_out", "w_ffn_out")
BIG_WEIGHTS = COL_SHARDED + ROW_SHARDED
WEIGHT_NAMES = ("norm_mix_g", "w_in", "gate_bias", "q_norm_g", "k_norm_g", "attn_sinks", "ssm_lambda_re",
                "ssm_lambda_im", "ssm_log_dt", "ssm_b_re", "ssm_b_im", "ssm_c_re", "ssm_c_im", "ssm_d", "ssm_glu_w",
                "ssm_glu_b", "w_attn_branch", "w_ssm_branch", "w_out", "norm_ffn_g", "w_ffn_in", "w_ffn_out")
SMALL_WEIGHTS = tuple(n for n in WEIGHT_NAMES if n not in BIG_WEIGHTS)
MIXER_WEIGHTS = ("w_out", "w_attn_branch", "w_ssm_branch", "ssm_glu_w")
WEIGHT_GROUPS = {"in": ("w_in",), "mixer": MIXER_WEIGHTS, "ffn": ("w_ffn_in", "w_ffn_out")}


def _dims(d, shapes, tc):
    s, _, aw, _ = shapes["w_attn_branch"]
    sw = shapes["w_ssm_branch"][2]
    in_w = shapes["w_in"][3] * s
    kvw = (in_w - aw - sw - 2 * d) // 2
    ff = shapes["w_ffn_out"][2] * s
    return dict(aw=aw, kvw=kvw, sw=sw, ff=ff, n_q=aw // HEAD_DIM, n_kv=kvw // HEAD_DIM, tc=tc)


def _big_params(big):
    p = {n: _Weight(a, 0, "col") for n, a in big.items() if n in COL_SHARDED}
    p.update({n: _Weight(a.reshape(1, 1, -1, a.shape[-1]), 0, "col") for n, a in big.items() if n in ROW_SHARDED})
    return p


def _layer_params(l, small):
    p = {}
    for n in ("norm_mix_g", "gate_bias", "q_norm_g", "k_norm_g", "ssm_d", "ssm_glu_b", "norm_ffn_g"):
        p[n] = small[n][l][None]
    p["attn_sinks"] = small["attn_sinks"][l]
    p["lam_re"] = small["ssm_lambda_re"][l]
    p["lam_im"] = small["ssm_lambda_im"][l]
    p["log_dt"] = small["ssm_log_dt"][l][:, None]
    p["b_re_t"] = jnp.swapaxes(small["ssm_b_re"][l], 1, 2)
    p["b_im_t"] = jnp.swapaxes(small["ssm_b_im"][l], 1, 2)
    p["c_re"] = small["ssm_c_re"][l]
    p["c_im"] = small["ssm_c_im"][l]
    return p


_ANY = pl.BlockSpec(memory_space=pl.ANY)
_MESH_ID = pl.DeviceIdType.MESH


def _coords():
    return lax.axis_index("x"), lax.axis_index("y"), lax.axis_index("c")


def _remote(src, dst, send_sems, recv_sems, k, to):
    return pltpu.make_async_remote_copy(src_ref=src, dst_ref=dst, send_sem=send_sems.at[k], recv_sem=recv_sems.at[k],
                                        device_id=to, device_id_type=_MESH_ID)


def _comm_call(body, ins, out_shapes, n_remote, name, aliases=None, scratch=()):
    return pl.pallas_call(
        body,
        out_shape=tuple(out_shapes),
        in_specs=[_ANY] * len(ins),
        out_specs=tuple([_ANY] * len(out_shapes)),
        scratch_shapes=[pltpu.SemaphoreType.DMA((n_remote,)), pltpu.SemaphoreType.DMA((n_remote,))] + list(scratch),
        input_output_aliases=aliases or {},
        compiler_params=pltpu.CompilerParams(has_side_effects=True),
        name=name,
    )(*ins)


def _my_chip():
    return (2 * lax.axis_index("x") + lax.axis_index("y")).astype(jnp.int32).reshape(1)


def _my_core():
    return lax.axis_index("c").astype(jnp.int32).reshape(1)


def _cast_into_slot(w, layer, *, name):
    _, r, c = w.shape
    tm = _row_tile(12 * c, r)

    def body(me_ref, w_ref, o_ref):
        o_ref[...] = w_ref[...].astype(o_ref.dtype)

    return pl.pallas_call(
        body,
        out_shape=jax.ShapeDtypeStruct((N_CHIPS, 1, r, c), BF16),
        grid_spec=pltpu.PrefetchScalarGridSpec(
            num_scalar_prefetch=1,
            grid=(r // tm,),
            in_specs=[pl.BlockSpec((None, tm, c), lambda i, me: (layer, i, 0))],
            out_specs=pl.BlockSpec((None, None, tm, c), lambda i, me: (me[0], 0, i, 0)),
        ),
        compiler_params=_cparams("parallel"),
        name=name,
    )(_my_chip(), w)


def _allgather_weights(bufs, *, name):
    n = len(bufs)

    def body(*refs):
        outs = refs[n:2 * n]
        send_sems, recv_sems = refs[2 * n:]
        x, y, c = _coords()
        me = 2 * x + y
        chips = [(1 - x, y), (x, 1 - y), (1 - x, 1 - y)]
        sibling = (x, y, 1 - c)

        def half(i, slot, hc):
            rh = bufs[i].shape[2] // 2
            return outs[i].at[slot, :, pl.ds(hc * rh, rh), :]

        first = [_remote(half(i, me, c), half(i, me, c), send_sems, recv_sems, 6 * i + k, (px, py, c))
                 for i in range(n) for k, (px, py) in enumerate(chips)]
        for cp in first:
            cp.start()
        passed = []
        for k, (px, py) in enumerate(chips):
            for i in range(n):
                landed = half(i, 2 * px + py, c)
                _remote(landed, landed, send_sems, recv_sems, 6 * i + k, (px, py, c)).wait_recv()
                fw = _remote(landed, landed, send_sems, recv_sems, 6 * i + 3 + k, sibling)
                fw.start()
                passed.append(fw)
        for k, (px, py) in enumerate(chips):
            for i in range(n):
                other = half(i, 2 * px + py, 1 - c)
                _remote(other, other, send_sems, recv_sems, 6 * i + 3 + k, sibling).wait_recv()
        for cp in first + passed:
            cp.wait_send()

    outs = [jax.ShapeDtypeStruct(b.shape, b.dtype) for b in bufs]
    return _comm_call(body, bufs, outs, 6 * n, name, aliases={i: i for i in range(n)})


_HBM = pl.BlockSpec(memory_space=pltpu.HBM)
_SEM = pl.BlockSpec(memory_space=pltpu.SEMAPHORE)
_DATAFLOW = pltpu.SideEffectType.DATAFLOW_SIDE_EFFECTING


class _SplitExchange:
    def __init__(self, srcs, lands, build, n_copies, name):
        self.build, self.n, self.name = build, n_copies, name
        self.ns, self.nl = len(srcs), len(lands)
        self.bufs = [pltpu.with_memory_space_constraint(a, pltpu.HBM) for a in list(srcs) + list(lands)]

    def _copies(self, refs, send_sems, recv_sems):
        triples = self.build(refs[:self.ns], refs[self.ns:self.ns + self.nl])
        assert len(triples) == self.n
        return [pltpu.make_async_remote_copy(src_ref=s, dst_ref=d, send_sem=send_sems.at[k], recv_sem=recv_sems.at[k],
                                             device_id=to, device_id_type=_MESH_ID) for k, (s, d, to) in enumerate(triples)]

    def start(self, deps=()):
        nb = self.ns + self.nl

        def body(*refs):
            outs = refs[nb + len(deps):]
            for cp in self._copies(refs, outs[0], outs[1]):
                cp.start()
            outs[-1][...] = jnp.zeros_like(outs[-1])

        sems = pltpu.SemaphoreType.DMA((self.n,))
        res = pl.pallas_call(
            body,
            out_shape=(sems, sems, *[pltpu.HBM(b.shape, b.dtype) for b in self.bufs], jax.ShapeDtypeStruct((8, 128), F32)),
            in_specs=[_HBM] * nb + [_ANY] * len(deps),
            out_specs=(_SEM, _SEM, *[_HBM] * nb, pl.BlockSpec(memory_space=pltpu.VMEM)),
            input_output_aliases={i: 2 + i for i in range(nb)},
            compiler_params=pltpu.CompilerParams(has_side_effects=_DATAFLOW),
            name=self.name + "_start",
        )(*self.bufs, *deps)
        self.send_sems, self.recv_sems = res[0], res[1]
        self.bufs = list(res[2:2 + nb])
        return res[-1]

    def wait(self, after=()):
        nb = self.ns + self.nl

        def body(*refs):
            for cp in self._copies(refs, refs[nb], refs[nb + 1]):
                cp.wait_send()
                cp.wait_recv()

        res = pl.pallas_call(
            body,
            out_shape=tuple(pltpu.HBM(b.shape, b.dtype) for b in self.bufs),
            in_specs=[_HBM] * nb + [_SEM, _SEM] + [_ANY] * len(after),
            out_specs=tuple([_HBM] * nb),
            input_output_aliases={i: i for i in range(nb)},
            compiler_params=pltpu.CompilerParams(has_side_effects=_DATAFLOW),
            name=self.name + "_wait",
        )(*self.bufs, self.send_sems, self.recv_sems, *after)
        res = list(res)
        return res[:self.ns], res[self.ns:]


def _other_chips(x, y):
    return [(1 - x, y), (x, 1 - y), (1 - x, 1 - y)]


def _gather_steps(bufs, tag, deps, publish):
    n = len(bufs)
    half = lambda ref, i, slot, hc: ref.at[slot, :, pl.ds(hc * (bufs[i].shape[2] // 2), bufs[i].shape[2] // 2), :]

    def over_ici(srcs, lands):
        x, y, c = _coords()
        me = 2 * x + y
        return [(half(srcs[i], i, me, c), half(srcs[i], i, me, c), (px, py, c))
                for i in range(n) for px, py in _other_chips(x, y)]

    def to_sibling(srcs, lands):
        x, y, c = _coords()
        return [(half(srcs[i], i, 2 * px + py, c), half(srcs[i], i, 2 * px + py, c), (x, y, 1 - c))
                for i in range(n) for px, py in _other_chips(x, y)]

    ex = _SplitExchange(bufs, [], over_ici, 3 * n, tag + "_ici")
    after = yield ex.start(deps)
    bufs, _ = ex.wait((after,))
    ex = _SplitExchange(bufs, [], to_sibling, 3 * n, tag + "_d2d")
    after = yield ex.start()
    bufs, _ = ex.wait((after,))
    publish(bufs)


def _reduce_steps(grads, tag, publish):
    n = len(grads)
    rh = [g.shape[1] // 2 for g in grads]
    theirs = [lax.empty((g.shape[0], g.shape[1] // 2, g.shape[2]), F32) for g in grads]

    def halves(srcs, lands):
        x, y, c = _coords()
        return [(srcs[i].at[:, pl.ds((1 - c) * rh[i], rh[i]), :], lands[i], (x, y, 1 - c)) for i in range(n)]

    def chips(srcs, lands):
        x, y, c = _coords()
        me = 2 * x + y
        return [(srcs[i].at[2 * px + py], lands[i].at[me], (px, py, c)) for i in range(n) for px, py in _other_chips(x, y)]

    def sibling(srcs, lands):
        x, y, c = _coords()
        return [(srcs[i], lands[i], (x, y, 1 - c)) for i in range(n)]

    ex = _SplitExchange(grads, theirs, halves, n, tag + "_halves")
    after = yield ex.start()
    grads, theirs = ex.wait((after,))
    parts = [_add_own_half(g, t, name="%s_add_own_half_%d" % (tag, i)) for i, (g, t) in enumerate(zip(grads, theirs))]
    ex = _SplitExchange(parts, [lax.empty(p.shape, p.dtype) for p in parts], chips, 3 * n, tag + "_chips")
    after = yield ex.start()
    parts, got = ex.wait((after,))
    mine = [_sum_chips(p, g, name="%s_sum_chips_%d" % (tag, i)) for i, (p, g) in enumerate(zip(parts, got))]
    ex = _SplitExchange(mine, [lax.empty(m.shape, m.dtype) for m in mine], sibling, n, tag + "_sibling")
    after = yield ex.start()
    mine, theirs = ex.wait((after,))
    publish(list(zip(mine, theirs)))


def _share_steps(buf, tag, publish):
    def build(srcs, lands):
        x, y, c = _coords()
        flip = lambda v, b: (1 - v) if b else v
        mine = 4 * x + 2 * y + c
        return [(srcs[0].at[mine], srcs[0].at[mine], (flip(x, r & 4), flip(y, r & 2), flip(c, r & 1)))
                for r in range(1, N_DEV)]

    ex = _SplitExchange([buf], [], build, N_DEV - 1, tag)
    after = yield ex.start()
    bufs, _ = ex.wait((after,))
    publish(bufs[0])


class _Exchanges:
    def __init__(self):
        self.running = []

    def launch(self, steps):
        self.running.append(steps)
        return next(steps)

    def advance(self, steps, after):
        try:
            return steps.send(after)
        except StopIteration:
            self.running.remove(steps)
            return None

    def advance_all(self, after):
        tokens = [self.advance(steps, after) for steps in list(self.running)]
        return tuple(t for t in tokens if t is not None)


def _add_own_half(g, theirs, *, name):
    s, r, c = g.shape
    rh = r // 2
    tm = _row_tile(10 * c, rh)
    nb = rh // tm

    def body(core_ref, g_ref, t_ref, o_ref):
        o_ref[...] = (g_ref[...] + t_ref[...]).astype(o_ref.dtype)

    return pl.pallas_call(
        body,
        out_shape=jax.ShapeDtypeStruct((s, rh, c), BF16),
        grid_spec=pltpu.PrefetchScalarGridSpec(
            num_scalar_prefetch=1,
            grid=(s, nb),
            in_specs=[pl.BlockSpec((None, tm, c), lambda k, i, core: (k, core[0] * nb + i, 0)),
                      pl.BlockSpec((None, tm, c), lambda k, i, core: (k, i, 0))],
            out_specs=pl.BlockSpec((None, tm, c), lambda k, i, core: (k, i, 0)),
        ),
        compiler_params=_cparams("parallel", "parallel"),
        name=name,
    )(_my_core(), g, theirs)


def _sum_chips(part, got, *, name):
    s, rh, c = part.shape
    tm = _row_tile(14 * c, rh)

    def body(me_ref, p_ref, a_ref, b_ref, c_ref, o_ref):
        o_ref[...] = ((p_ref[...].astype(F32) + a_ref[...].astype(F32)) + b_ref[...].astype(F32)) + c_ref[...].astype(F32)

    slot = lambda k: (lambda i, me: ((me[0] + k) % s, i, 0))
    return pl.pallas_call(
        body,
        out_shape=jax.ShapeDtypeStruct((rh, c), F32),
        grid_spec=pltpu.PrefetchScalarGridSpec(
            num_scalar_prefetch=1,
            grid=(rh // tm,),
            in_specs=[pl.BlockSpec((None, tm, c), slot(k)) for k in range(s)],
            out_specs=pl.BlockSpec((tm, c), lambda i, me: (i, 0)),
        ),
        compiler_params=_cparams("parallel"),
        name=name,
    )(_my_chip(), part, got, got, got)


def _place_into_slot(buf, *, name):
    r, c = buf.shape
    tm = _row_tile(8 * c, r)
    mine = (4 * lax.axis_index("x") + 2 * lax.axis_index("y") + lax.axis_index("c")).astype(jnp.int32).reshape(1)

    def body(me_ref, i_ref, o_ref):
        o_ref[...] = i_ref[...]

    return pl.pallas_call(
        body,
        out_shape=jax.ShapeDtypeStruct((N_DEV, r, c), buf.dtype),
        grid_spec=pltpu.PrefetchScalarGridSpec(
            num_scalar_prefetch=1,
            grid=(r // tm,),
            in_specs=[pl.BlockSpec((tm, c), lambda i, me: (i, 0))],
            out_specs=pl.BlockSpec((None, tm, c), lambda i, me: (me[0], i, 0)),
        ),
        compiler_params=_cparams("parallel"),
        name=name,
    )(mine, buf)


def _sum_slots(arr, *, name):
    s, r, c = arr.shape
    tm = _row_tile(4 * c * (s + 1), r)

    def body(*refs):
        acc = refs[0][...]
        for ref in refs[1:s]:
            acc = acc + ref[...]
        refs[s][...] = acc

    return pl.pallas_call(
        body,
        out_shape=jax.ShapeDtypeStruct((r, c), arr.dtype),
        grid=(r // tm,),
        in_specs=[pl.BlockSpec((None, tm, c), lambda i, k=k: (k, i, 0)) for k in range(s)],
        out_specs=pl.BlockSpec((tm, c), lambda i: (i, 0)),
        compiler_params=_cparams("parallel"),
        name=name,
    )(*([arr] * s))


def _adamw_fn(w, g, m, v):
    m = ADAM_B1 * m + (1.0 - ADAM_B1) * g
    v = ADAM_B2 * v + (1.0 - ADAM_B2) * jnp.square(g)
    m_hat = m / (1.0 - ADAM_B1 ** ADAM_STEP)
    v_hat = v / (1.0 - ADAM_B2 ** ADAM_STEP)
    delta = -ADAM_LR * (m_hat / (jnp.sqrt(v_hat) + ADAM_EPS) + ADAM_WD * w)
    return delta, m, v


def _adamw(w, g, m, v, *, name):
    rows, cols = w.shape
    ins = [(a, "row", cols, 0) for a in (w, g, m, v)]
    outs = [(cols, F32, "row", cols)] * 3
    return _rowmap(_adamw_fn, ins, outs, rows=rows, tm=_row_tile(56 * cols, rows), name=name)


def _adamw_sharded(w, m, v, g_mine, g_sibling, layer, into, *, name, deps=()):
    nl, r, c = w.shape
    rh = r // 2
    tm = _row_tile(40 * c, rh)
    nb = rh // tm
    n_into = 0 if into is None else 4

    def body(core_ref, w_ref, m_ref, v_ref, a_ref, b_ref, *rest):
        g_ref, d_ref, nm_ref, nv_ref = rest[n_into + len(deps):]
        g = jnp.where(pl.program_id(0) == core_ref[0], a_ref[...], b_ref[...])
        delta, nm, nv = _adamw_fn(w_ref[...], g, m_ref[...], v_ref[...])
        g_ref[...] = g
        d_ref[...] = delta
        nm_ref[...] = nm
        nv_ref[...] = nv

    whole = pl.BlockSpec((None, tm, c), lambda h, i, core: (layer, h * nb + i, 0))
    half = pl.BlockSpec((tm, c), lambda h, i, core: (i, 0))
    shape = jax.ShapeDtypeStruct((nl, r, c), F32)
    return pl.pallas_call(
        body,
        out_shape=(shape, shape, shape, shape),
        grid_spec=pltpu.PrefetchScalarGridSpec(
            num_scalar_prefetch=1,
            grid=(2, nb),
            in_specs=[whole, whole, whole, half, half] + [pl.BlockSpec(memory_space=pl.ANY)] * (n_into + len(deps)),
            out_specs=(whole, whole, whole, whole),
        ),
        input_output_aliases={6 + k: k for k in range(n_into)},
        compiler_params=_cparams("parallel", "parallel"),
        name=name,
    )(_my_core(), w, m, v, g_mine, g_sibling, *(into or ()), *deps)


def _pack(arrays):
    flat = jnp.concatenate([a.reshape(-1) for a in arrays])
    pad = (-flat.shape[0]) % (256 * 128)
    return jnp.pad(flat, (0, pad)).reshape(-1, 128)


def _unpack(buf, shapes):
    flat = buf.reshape(-1)
    out, off = [], 0
    for s in shapes:
        n = math.prod(s)
        out.append(flat[off:off + n].reshape(s))
        off += n
    return out


def kernel(x, norm_mix_g, w_in, gate_bias, q_norm_g, k_norm_g, attn_sinks, ssm_lambda_re, ssm_lambda_im, ssm_log_dt, ssm_b_re, ssm_b_im, ssm_c_re, ssm_c_im, ssm_d, ssm_glu_w, ssm_glu_b, w_attn_branch, w_ssm_branch, w_out, norm_ffn_g, w_ffn_in, w_ffn_out, loss_target, m_norm_mix_g, m_w_in, m_gate_bias, m_q_norm_g, m_k_norm_g, m_attn_sinks, m_ssm_lambda_re, m_ssm_lambda_im, m_ssm_log_dt, m_ssm_b_re, m_ssm_b_im, m_ssm_c_re, m_ssm_c_im, m_ssm_d, m_ssm_glu_w, m_ssm_glu_b, m_w_attn_branch, m_w_ssm_branch, m_w_out, m_norm_ffn_g, m_w_ffn_in, m_w_ffn_out, v_norm_mix_g, v_w_in, v_gate_bias, v_q_norm_g, v_k_norm_g, v_attn_sinks, v_ssm_lambda_re, v_ssm_lambda_im, v_ssm_log_dt, v_ssm_b_re, v_ssm_b_im, v_ssm_c_re, v_ssm_c_im, v_ssm_d, v_ssm_glu_w, v_ssm_glu_b, v_w_attn_branch, v_w_ssm_branch, v_w_out, v_norm_ffn_g, v_w_ffn_in, v_w_ffn_out):
    w = dict(norm_mix_g=norm_mix_g, w_in=w_in, gate_bias=gate_bias, q_norm_g=q_norm_g, k_norm_g=k_norm_g,
             attn_sinks=attn_sinks, ssm_lambda_re=ssm_lambda_re, ssm_lambda_im=ssm_lambda_im, ssm_log_dt=ssm_log_dt,
             ssm_b_re=ssm_b_re, ssm_b_im=ssm_b_im, ssm_c_re=ssm_c_re, ssm_c_im=ssm_c_im, ssm_d=ssm_d,
             ssm_glu_w=ssm_glu_w, ssm_glu_b=ssm_glu_b, w_attn_branch=w_attn_branch, w_ssm_branch=w_ssm_branch,
             w_out=w_out, norm_ffn_g=norm_ffn_g, w_ffn_in=w_ffn_in, w_ffn_out=w_ffn_out)
    m = dict(norm_mix_g=m_norm_mix_g, w_in=m_w_in, gate_bias=m_gate_bias, q_norm_g=m_q_norm_g, k_norm_g=m_k_norm_g,
             attn_sinks=m_attn_sinks, ssm_lambda_re=m_ssm_lambda_re, ssm_lambda_im=m_ssm_lambda_im,
             ssm_log_dt=m_ssm_log_dt, ssm_b_re=m_ssm_b_re, ssm_b_im=m_ssm_b_im, ssm_c_re=m_ssm_c_re,
             ssm_c_im=m_ssm_c_im, ssm_d=m_ssm_d, ssm_glu_w=m_ssm_glu_w, ssm_glu_b=m_ssm_glu_b,
             w_attn_branch=m_w_attn_branch, w_ssm_branch=m_w_ssm_branch, w_out=m_w_out, norm_ffn_g=m_norm_ffn_g,
             w_ffn_in=m_w_ffn_in, w_ffn_out=m_w_ffn_out)
    v = dict(norm_mix_g=v_norm_mix_g, w_in=v_w_in, gate_bias=v_gate_bias, q_norm_g=v_q_norm_g, k_norm_g=v_k_norm_g,
             attn_sinks=v_attn_sinks, ssm_lambda_re=v_ssm_lambda_re, ssm_lambda_im=v_ssm_lambda_im,
             ssm_log_dt=v_ssm_log_dt, ssm_b_re=v_ssm_b_re, ssm_b_im=v_ssm_b_im, ssm_c_re=v_ssm_c_re,
             ssm_c_im=v_ssm_c_im, ssm_d=v_ssm_d, ssm_glu_w=v_ssm_glu_w, ssm_glu_b=v_ssm_glu_b,
             w_attn_branch=v_w_attn_branch, w_ssm_branch=v_w_ssm_branch, w_out=v_w_out, norm_ffn_g=v_norm_ffn_g,
             w_ffn_in=v_w_ffn_in, w_ffn_out=v_w_ffn_out)
    n_layers = norm_mix_g.shape[0]
    d_model = x.shape[-1]
    seq = x.shape[1]

    exchanges = _Exchanges()
    params = [_layer_params(l, w) for l in range(n_layers)]
    gathers = {}

    def gather(l, group, deps=()):
        names = WEIGHT_GROUPS[group]
        bufs = [_cast_into_slot(w[n], l, name="cast%d_%s" % (l, n)) for n in names]
        steps = _gather_steps(bufs, "ag%d_%s" % (l, group), deps,
                              lambda got: params[l].update(_big_params(dict(zip(names, got)))))
        gathers[l, group] = steps
        return exchanges.launch(steps)

    w_in0 = _allgather_weights([_cast_into_slot(w["w_in"], 0, name="cast0_w_in")], name="ag0_in")[0]
    params[0].update(_big_params({"w_in": w_in0}))
    first_deps = (gather(0, "mixer", (w_in0,)), gather(0, "ffn", (w_in0,)))
    sizes = {n: (N_CHIPS, 1) + w[n].shape[1:] for n in BIG_WEIGHTS}
    dims = _dims(d_model, sizes, min(512, seq))

    def weights_of(l, h):
        if l == 0:
            return params[0], first_deps
        for group in WEIGHT_GROUPS:
            exchanges.advance(gathers[l, group], h)
        return params[l], ()

    def fwd_tick(l, point, arr):
        tokens = []
        if l == 0 and point in ("in", "attn"):
            tokens.append(exchanges.advance(gathers[0, "mixer"], arr))
        if l == 0 and point in ("ssm", "out"):
            tokens.append(exchanges.advance(gathers[0, "ffn"], arr))
        if l + 1 < n_layers and point == "attn":
            tokens += [gather(l + 1, group) for group in WEIGHT_GROUPS]
        if l + 1 < n_layers and point == "ffn_in":
            tokens += [exchanges.advance(gathers[l + 1, group], arr) for group in WEIGHT_GROUPS]
        return tuple(t for t in tokens if t is not None)

    reduced, ready = {}, []

    def bwd_tick(l, stage, arr, stage_grads):
        tokens = exchanges.advance_all(arr)
        names = tuple(stage_grads)

        def publish(halves):
            reduced.update({(l, n): h for n, h in zip(names, halves)})
            ready.append((l, names))

        grads4 = [stage_grads[n].reshape(N_CHIPS, -1, stage_grads[n].shape[-1]) for n in names]
        return tokens + (exchanges.launch(_reduce_steps(grads4, "rs%d_%s" % (l, stage), publish)),)

    sq, dx, grads = _local_step(x[0], loss_target[0], n_layers, weights_of, dims, fwd_tick, bwd_tick)
    loss = lax.psum(sq[0, 0], MESH_AXES) * (0.5 / d_model)

    small_shapes = [w[n].shape for n in SMALL_WEIGHTS]
    small_local = [jnp.stack([grads[l][n].reshape(w[n].shape[1:]) for l in range(n_layers)]) for n in SMALL_WEIGHTS]
    shared = []
    tokens = (exchanges.launch(_share_steps(_place_into_slot(_pack(small_local), name="place_small_grads"),
                                            "share_small_grads", shared.append)),)

    adam = {n: None for n in BIG_WEIGHTS}
    grad, delta, new_m, new_v = {}, {}, {}, {}
    after = dx
    while exchanges.running or ready or shared:
        for l, names in ready:
            for n in names:
                mine, sibling = reduced[l, n]
                adam[n] = _adamw_sharded(w[n], m[n], v[n], mine, sibling, l, adam[n], name="adamw%d_%s" % (l, n), deps=tokens)
                after = adam[n][1]
        del ready[:]
        if shared:
            small_sum = _sum_slots(shared.pop(), name="sum_small_grads")
            grad.update(zip(SMALL_WEIGHTS, _unpack(small_sum, small_shapes)))
            res = _adamw(_pack([w[n] for n in SMALL_WEIGHTS]), small_sum, _pack([m[n] for n in SMALL_WEIGHTS]),
                         _pack([v[n] for n in SMALL_WEIGHTS]), name="adamw_small")
            for out, packed in zip((delta, new_m, new_v), res):
                out.update(zip(SMALL_WEIGHTS, _unpack(packed, small_shapes)))
            after = res[0]
        tokens = exchanges.advance_all(after)
    for n in BIG_WEIGHTS:
        grad[n], delta[n], new_m[n], new_v[n] = adam[n]

    return (loss, dx[None], *[grad[n] for n in WEIGHT_NAMES], *[delta[n] for n in WEIGHT_NAMES],
            *[new_m[n] for n in WEIGHT_NAMES], *[new_v[n] for n in WEIGHT_NAMES])
```

```python
import functools
import math

import jax
import jax.numpy as jnp
from jax import lax
from jax.experimental import pallas as pl
from jax.experimental.pallas import tpu as pltpu

HEAD_DIM = 64
WINDOW = 128
SSM_GROUP_CH = 16
SSM_LANE_GROUPS = 8
RMS_EPS = 1e-6
ADAM_LR = 0.001
ADAM_B1 = 0.9
ADAM_B2 = 0.999
ADAM_EPS = 1e-08
ADAM_WD = 0.01
ADAM_STEP = 10
NEG_BIG = -1e30
MESH_AXES = ("x", "y", "c")
N_CHIPS = 4
N_DEV = 8
VMEM_LIMIT_BYTES = 56 * 1024 * 1024
BF16 = jnp.bfloat16
F32 = jnp.float32


def _cparams(*semantics):
    return pltpu.CompilerParams(dimension_semantics=semantics, vmem_limit_bytes=VMEM_LIMIT_BYTES)


def _pick(n, target, mult):
    if n <= target:
        return n
    best = None
    for d in range(mult, target + 1, mult):
        if n % d == 0:
            best = d
    assert best is not None, (n, target, mult)
    return best


def _rowmap(fn, ins, outs, *, rows, tm, ncol=1, name, deps=()):
    n_in = len(ins)
    nrow = rows // tm
    assert nrow * tm == rows

    in_specs = []
    for arr, kind, width, coloff in ins:
        if kind == "row":
            in_specs.append(pl.BlockSpec((tm, width), lambda j, i, o=coloff: (i, o + j)))
        elif kind == "vec":
            in_specs.append(pl.BlockSpec((1, width), lambda j, i, o=coloff: (0, o + j)))
        else:
            nd = arr.ndim
            in_specs.append(pl.BlockSpec(arr.shape, lambda j, i, nd=nd: (0,) * nd))
    out_specs, out_shapes = [], []
    for cols, dtype, kind, width in outs:
        if kind == "row":
            out_specs.append(pl.BlockSpec((tm, width), lambda j, i: (i, j)))
            out_shapes.append(jax.ShapeDtypeStruct((rows, cols), dtype))
        else:
            out_specs.append(pl.BlockSpec((1, width), lambda j, i: (0, j)))
            out_shapes.append(jax.ShapeDtypeStruct((1, cols), dtype))

    in_specs += [pl.BlockSpec(memory_space=pl.ANY)] * len(deps)

    def body(*refs):
        i = pl.program_id(1)
        res = fn(*[r[...].astype(F32) for r in refs[:n_in]])
        if not isinstance(res, (tuple, list)):
            res = (res,)
        for (cols, dtype, kind, width), ref, val in zip(outs, refs[n_in + len(deps):], res):
            if kind == "row":
                ref[...] = val.astype(ref.dtype)
            else:
                @pl.when(i == 0)
                def _():
                    ref[...] = jnp.zeros_like(ref)
                ref[...] += val.astype(ref.dtype)

    res = pl.pallas_call(
        body,
        out_shape=tuple(out_shapes),
        grid=(ncol, nrow),
        in_specs=in_specs,
        out_specs=tuple(out_specs),
        compiler_params=_cparams("parallel", "arbitrary"),
        name=name,
    )(*[a[0] for a in ins], *deps)
    return res


def _mm_body(dims, nk, has_add, unused_in=0):
    def body(*refs):
        if has_add:
            a_ref, b_ref, add_ref = refs[:3]
            o_ref = refs[3 + unused_in]
            rest = refs[4 + unused_in:]
        else:
            a_ref, b_ref = refs[:2]
            o_ref = refs[2 + unused_in]
            add_ref = None
            rest = refs[3 + unused_in:]
        part = lax.dot_general(a_ref[...], b_ref[...], (dims, ((), ())), preferred_element_type=F32)
        if nk == 1:
            if add_ref is not None:
                part = part + add_ref[...]
            o_ref[...] = part.astype(o_ref.dtype)
        else:
            acc_ref = rest[0]
            k = pl.program_id(2)

            @pl.when(k == 0)
            def _():
                acc_ref[...] = part

            @pl.when(k > 0)
            def _():
                acc_ref[...] += part

            @pl.when(k == nk - 1)
            def _():
                r = acc_ref[...]
                if add_ref is not None:
                    r = r + add_ref[...]
                o_ref[...] = r.astype(o_ref.dtype)
    return body


class _Weight:
    def __init__(self, arr, layer, kind):
        self.arr, self.layer, self.kind = arr, layer, kind
        self.s, _, self.r, self.c = arr.shape
        self.rows = self.r * (self.s if kind == "row" else 1)
        self.cols = self.c * (self.s if kind == "col" else 1)

    def tiles(self, tr, tc):
        return _pick(self.r, tr, 128), _pick(self.c, tc, 128)

    def index(self, tr, tc):
        layer = self.layer
        if self.kind == "col":
            per = self.c // tc
            return lambda rb, cb: (cb // per, layer, rb, cb % per)
        per = self.r // tr
        return lambda rb, cb: (rb // per, layer, rb % per, cb)


def _shard_index(kind, r, c, tr, tc):
    if kind == "col":
        per = c // tc
        return lambda rb, cb: (cb // per, rb, cb % per)
    per = r // tr
    return lambda rb, cb: (rb // per, rb % per, cb)


def _mm_nn(a, w, *, out_dtype, tm, tn, tk, name, add=None, deps=()):
    m, k = a.shape
    assert k == w.rows
    tm = _pick(m, tm, 16)
    tk, tn = w.tiles(tk, tn)
    nk = k // tk
    widx = w.index(tk, tn)
    in_specs = [pl.BlockSpec((tm, tk), lambda n, i, kk: (i, kk)),
                pl.BlockSpec((None, None, tk, tn), lambda n, i, kk: widx(kk, n))]
    args = [a, w.arr]
    if add is not None:
        in_specs.append(pl.BlockSpec((tm, tn), lambda n, i, kk: (i, n)))
        args.append(add)
    in_specs += [pl.BlockSpec(memory_space=pl.ANY)] * len(deps)
    args += list(deps)
    return pl.pallas_call(
        _mm_body(((1,), (0,)), nk, add is not None, unused_in=len(deps)),
        out_shape=jax.ShapeDtypeStruct((m, w.cols), out_dtype),
        grid=(w.cols // tn, m // tm, nk),
        in_specs=in_specs,
        out_specs=pl.BlockSpec((tm, tn), lambda n, i, kk: (i, n)),
        scratch_shapes=[pltpu.VMEM((tm, tn), F32)] if nk > 1 else [],
        compiler_params=_cparams("parallel", "parallel", "arbitrary"),
        name=name,
    )(*args)


def _mm_nt(a, w, *, out_dtype, tm, tn, tko, name, deps=()):
    m, n = a.shape
    assert n == w.cols
    tm = _pick(m, tm, 16)
    if w.kind == "col" and w.s > 1:
        tko = _pick(w.r, tko, 128)
        layer, nsh, width = w.layer, w.s, w.c

        def body(a_ref, w_ref, *rest):
            o_ref = rest[len(deps)]
            acc = None
            for s in range(nsh):
                part = lax.dot_general(a_ref[:, s * width:(s + 1) * width], w_ref[s], (((1,), (1,)), ((), ())),
                                       preferred_element_type=F32)
                acc = part if acc is None else acc + part
            o_ref[...] = acc.astype(o_ref.dtype)

        return pl.pallas_call(
            body,
            out_shape=jax.ShapeDtypeStruct((m, w.rows), out_dtype),
            grid=(w.rows // tko, m // tm),
            in_specs=[pl.BlockSpec((tm, n), lambda ko, i: (i, 0)),
                      pl.BlockSpec((nsh, None, tko, width), lambda ko, i: (0, layer, ko, 0))]
            + [pl.BlockSpec(memory_space=pl.ANY)] * len(deps),
            out_specs=pl.BlockSpec((tm, tko), lambda ko, i: (i, ko)),
            compiler_params=_cparams("parallel", "parallel"),
            name=name,
        )(a, w.arr, *deps)
    tko, tn = w.tiles(tko, tn)
    nk = n // tn
    widx = w.index(tko, tn)
    return pl.pallas_call(
        _mm_body(((1,), (1,)), nk, False, unused_in=len(deps)),
        out_shape=jax.ShapeDtypeStruct((m, w.rows), out_dtype),
        grid=(w.rows // tko, m // tm, nk),
        in_specs=[pl.BlockSpec((tm, tn), lambda ko, i, nn: (i, nn)),
                  pl.BlockSpec((None, None, tko, tn), lambda ko, i, nn: widx(ko, nn))]
        + [pl.BlockSpec(memory_space=pl.ANY)] * len(deps),
        out_specs=pl.BlockSpec((tm, tko), lambda ko, i, nn: (i, ko)),
        scratch_shapes=[pltpu.VMEM((tm, tko), F32)] if nk > 1 else [],
        compiler_params=_cparams("parallel", "parallel", "arbitrary"),
        name=name,
    )(a, w.arr, *deps)


def _mm_tn(a, c, w, *, into, tm, tn, tko, name):
    m, k = a.shape
    mc, n = c.shape
    assert mc == m and k == w.rows and n == w.cols
    tm = _pick(m, tm, 16)
    tko, tn = w.tiles(tko, tn)
    nk = m // tm
    oidx = _shard_index(w.kind, w.r, w.c, tko, tn)
    layer = w.layer
    in_specs = [pl.BlockSpec((tm, tko), lambda ko, nn, mm: (mm, ko)),
                pl.BlockSpec((tm, tn), lambda ko, nn, mm: (mm, nn))]
    args = [a, c]
    if into is not None:
        in_specs.append(pl.BlockSpec(memory_space=pl.ANY))
        args.append(into)
    return pl.pallas_call(
        _mm_body(((0,), (0,)), nk, False, unused_in=len(args) - 2),
        out_shape=jax.ShapeDtypeStruct((w.arr.shape[1], w.s, w.r, w.c), F32),
        grid=(k // tko, n // tn, nk),
        in_specs=in_specs,
        out_specs=pl.BlockSpec((None, None, tko, tn), lambda ko, nn, mm: (layer,) + oidx(ko, nn)),
        scratch_shapes=[pltpu.VMEM((tko, tn), F32)] if nk > 1 else [],
        input_output_aliases={2: 0} if into is not None else {},
        compiler_params=_cparams("parallel", "parallel", "arbitrary"),
        name=name,
    )(*args)


def _rms(x, g):
    r = lax.rsqrt(jnp.mean(x * x, axis=-1, keepdims=True) + RMS_EPS)
    return x * r * g, r


def _rms_bwd(x, r, g, dy):
    dg = jnp.sum(dy * x * r, axis=0, keepdims=True)
    t = dy * g
    dx = r * t - x * (r * r * r) * jnp.mean(t * x, axis=-1, keepdims=True)
    return dx, dg


def _attn_consts(n_q, n_kv, sinks):
    group = n_q // n_kv
    t = jnp.arange(WINDOW, dtype=jnp.int32)[:, None]
    s = jnp.arange(2 * WINDOW, dtype=jnp.int32)[None, :] - WINDOW
    dist = (t - s).astype(F32)
    valid = (dist >= 0) & (dist < WINDOW)
    slopes = jnp.exp2(-8.0 * jnp.arange(1, n_q + 1, dtype=F32) / n_q)
    bias = jnp.where(valid[None], -slopes[:, None, None] * dist[None], NEG_BIG)
    bias = bias.reshape(n_kv, group * WINDOW, 2 * WINDOW)
    sink = jnp.broadcast_to(sinks.astype(F32).reshape(n_kv, group, 1, 1), (n_kv, group, WINDOW, 1))
    return bias, sink.reshape(n_kv, group * WINDOW, 1)


def _attn_probs(q_ref, kc_ref, kp_ref, vc_ref, vp_ref, qg, kg, sink, bias, first_mask, kv, group):
    sl = slice(kv * HEAD_DIM, (kv + 1) * HEAD_DIM)
    k2 = jnp.concatenate([kp_ref[:, sl], kc_ref[:, sl]], axis=0)
    v2 = jnp.concatenate([vp_ref[:, sl], vc_ref[:, sl]], axis=0)
    k2n, rk = _rms(k2, kg)
    qx, qn, rq = [], [], []
    for g in range(group):
        h = kv * group + g
        x = q_ref[:, h * HEAD_DIM:(h + 1) * HEAD_DIM]
        y, r = _rms(x, qg)
        qx.append(x); qn.append(y); rq.append(r)
    qs = jnp.concatenate(qn, axis=0).astype(BF16)
    k2b = k2n.astype(BF16)
    s = lax.dot_general(qs, k2b, (((1,), (1,)), ((), ())), preferred_element_type=F32) * (HEAD_DIM ** -0.5)
    s = jnp.where(first_mask, NEG_BIG, s + bias)
    m = jnp.maximum(jnp.max(s, axis=-1, keepdims=True), sink)
    p = jnp.exp(s - m)
    esink = jnp.exp(sink - m)
    denom = jnp.sum(p, axis=-1, keepdims=True) + esink
    pn = p / denom
    return dict(k2=k2, rk=rk, k2b=k2b, v2b=v2.astype(BF16), qx=qx, rq=rq, qs=qs, pn=pn, psink=esink / denom)


def _attn_specs(n_q, n_kv):
    aw, kvw = n_q * HEAD_DIM, n_kv * HEAD_DIM
    group = n_q // n_kv
    kblk, vblk = aw // kvw, aw // kvw + 1

    def specs(nb):
        cur = lambda n: jnp.minimum(n, nb - 1)
        prev = lambda n: jnp.maximum(jnp.minimum(n, nb - 1) - 1, 0)
        return [
            pl.BlockSpec((WINDOW, aw), lambda n: (cur(n), 0)),
            pl.BlockSpec((WINDOW, kvw), lambda n: (cur(n), kblk)),
            pl.BlockSpec((WINDOW, kvw), lambda n: (prev(n), kblk)),
            pl.BlockSpec((WINDOW, kvw), lambda n: (cur(n), vblk)),
            pl.BlockSpec((WINDOW, kvw), lambda n: (prev(n), vblk)),
        ]
    const_specs = [
        pl.BlockSpec((1, HEAD_DIM), lambda n: (0, 0)),
        pl.BlockSpec((1, HEAD_DIM), lambda n: (0, 0)),
        pl.BlockSpec((n_kv, group * WINDOW, 1), lambda n: (0, 0, 0)),
        pl.BlockSpec((n_kv, group * WINDOW, 2 * WINDOW), lambda n: (0, 0, 0)),
    ]
    return specs, const_specs


def _attn_fwd(z, qg, kg, sinks, *, n_q, n_kv, name, deps=()):
    L = z.shape[0]
    nb = L // WINDOW
    aw = n_q * HEAD_DIM
    group = n_q // n_kv
    bias, sink = _attn_consts(n_q, n_kv, sinks)
    specs, const_specs = _attn_specs(n_q, n_kv)

    def body(q_ref, kc_ref, kp_ref, vc_ref, vp_ref, qg_ref, kg_ref, sink_ref, bias_ref, *rest):
        o_ref = rest[-1]
        n = pl.program_id(0)
        col = lax.broadcasted_iota(jnp.int32, (group * WINDOW, 2 * WINDOW), 1)
        first_mask = jnp.logical_and(n == 0, col < WINDOW)
        for kv in range(n_kv):
            a = _attn_probs(q_ref, kc_ref, kp_ref, vc_ref, vp_ref, qg_ref[...], kg_ref[...],
                            sink_ref[kv], bias_ref[kv], first_mask, kv, group)
            o = jnp.dot(a["pn"].astype(BF16), a["v2b"], preferred_element_type=F32)
            for g in range(group):
                h = kv * group + g
                o_ref[:, h * HEAD_DIM:(h + 1) * HEAD_DIM] = o[g * WINDOW:(g + 1) * WINDOW].astype(o_ref.dtype)

    return pl.pallas_call(
        body,
        out_shape=jax.ShapeDtypeStruct((L, aw), BF16),
        grid=(nb,),
        in_specs=specs(nb) + const_specs + [pl.BlockSpec(memory_space=pl.ANY)] * len(deps),
        out_specs=pl.BlockSpec((WINDOW, aw), lambda n: (n, 0)),
        compiler_params=_cparams("parallel"),
        name=name,
    )(z, z, z, z, z, qg, kg, sink, bias, *deps)


def _attn_bwd(z, do, qg, kg, sinks, *, n_q, n_kv, name):
    L = z.shape[0]
    nb = L // WINDOW
    aw, kvw = n_q * HEAD_DIM, n_kv * HEAD_DIM
    group = n_q // n_kv
    bias, sink = _attn_consts(n_q, n_kv, sinks)
    specs, const_specs = _attn_specs(n_q, n_kv)
    scale = HEAD_DIM ** -0.5

    def body(q_ref, kc_ref, kp_ref, vc_ref, vp_ref, do_ref, qg_ref, kg_ref, sink_ref, bias_ref,
             dq_ref, dkv_ref, dqg_ref, dkg_ref, dsink_ref, carry_ref):
        n = pl.program_id(0)

        @pl.when(n == 0)
        def _():
            dqg_ref[...] = jnp.zeros_like(dqg_ref)
            dkg_ref[...] = jnp.zeros_like(dkg_ref)
            dsink_ref[...] = jnp.zeros_like(dsink_ref)
            carry_ref[...] = jnp.zeros_like(carry_ref)

        @pl.when(n < nb)
        def _():
            col = lax.broadcasted_iota(jnp.int32, (group * WINDOW, 2 * WINDOW), 1)
            first_mask = jnp.logical_and(n == 0, col < WINDOW)
            head_lane = lax.broadcasted_iota(jnp.int32, (1, n_q), 1)
            qg, kg = qg_ref[...], kg_ref[...]
            dqg = jnp.zeros((1, HEAD_DIM), F32)
            dkg = jnp.zeros((1, HEAD_DIM), F32)
            dsink = jnp.zeros((1, n_q), F32)
            for kv in range(n_kv):
                a = _attn_probs(q_ref, kc_ref, kp_ref, vc_ref, vp_ref, qg, kg,
                                sink_ref[kv], bias_ref[kv], first_mask, kv, group)
                pn = a["pn"]
                dos = jnp.concatenate(
                    [do_ref[:, (kv * group + g) * HEAD_DIM:(kv * group + g + 1) * HEAD_DIM] for g in range(group)],
                    axis=0).astype(BF16)
                dpn = lax.dot_general(dos, a["v2b"], (((1,), (1,)), ((), ())), preferred_element_type=F32)
                dv2 = lax.dot_general(pn.astype(BF16), dos, (((0,), (0,)), ((), ())), preferred_element_type=F32)
                delta = jnp.sum(pn * dpn, axis=-1, keepdims=True)
                ds = (pn * (dpn - delta)).astype(BF16)
                dsk = -a["psink"] * delta
                dqn = lax.dot_general(ds, a["k2b"], (((1,), (0,)), ((), ())), preferred_element_type=F32) * scale
                dk2n = lax.dot_general(ds, a["qs"], (((0,), (0,)), ((), ())), preferred_element_type=F32) * scale
                for g in range(group):
                    h = kv * group + g
                    rows = slice(g * WINDOW, (g + 1) * WINDOW)
                    dx, dgq = _rms_bwd(a["qx"][g], a["rq"][g], qg, dqn[rows])
                    dq_ref[:, h * HEAD_DIM:(h + 1) * HEAD_DIM] = dx.astype(dq_ref.dtype)
                    dqg = dqg + dgq
                    dsink = dsink + jnp.where(head_lane == h, jnp.sum(dsk[rows], axis=0, keepdims=True), 0.0)
                dk2, dgk = _rms_bwd(a["k2"], a["rk"], kg, dk2n)
                dkg = dkg + dgk
                ksl = slice(kv * HEAD_DIM, (kv + 1) * HEAD_DIM)
                vsl = slice(kvw + kv * HEAD_DIM, kvw + (kv + 1) * HEAD_DIM)
                dkv_ref[:, ksl] = (carry_ref[:, ksl] + dk2[:WINDOW]).astype(dkv_ref.dtype)
                dkv_ref[:, vsl] = (carry_ref[:, vsl] + dv2[:WINDOW]).astype(dkv_ref.dtype)
                carry_ref[:, ksl] = dk2[WINDOW:]
                carry_ref[:, vsl] = dv2[WINDOW:]
            dqg_ref[...] += dqg
            dkg_ref[...] += dkg
            dsink_ref[...] += dsink

        @pl.when(n == nb)
        def _():
            dkv_ref[...] = carry_ref[...].astype(dkv_ref.dtype)

    in_specs = specs(nb) + [pl.BlockSpec((WINDOW, aw), lambda n: (jnp.minimum(n, nb - 1), 0))] + const_specs
    return pl.pallas_call(
        body,
        out_shape=(jax.ShapeDtypeStruct((L, aw), BF16), jax.ShapeDtypeStruct((L, 2 * kvw), BF16),
                   jax.ShapeDtypeStruct((1, HEAD_DIM), F32), jax.ShapeDtypeStruct((1, HEAD_DIM), F32),
                   jax.ShapeDtypeStruct((1, n_q), F32)),
        grid=(nb + 1,),
        in_specs=in_specs,
        out_specs=(pl.BlockSpec((WINDOW, aw), lambda n: (jnp.minimum(n, nb - 1), 0)),
                   pl.BlockSpec((WINDOW, 2 * kvw), lambda n: (jnp.maximum(n - 1, 0), 0)),
                   pl.BlockSpec((1, HEAD_DIM), lambda n: (0, 0)),
                   pl.BlockSpec((1, HEAD_DIM), lambda n: (0, 0)),
                   pl.BlockSpec((1, n_q), lambda n: (0, 0))),
        scratch_shapes=[pltpu.VMEM((WINDOW, 2 * kvw), F32)],
        compiler_params=_cparams("arbitrary"),
        name=name,
    )(z, z, z, z, z, do, qg, kg, sink, bias)


def _cmul(ar, ai, br, bi):
    return ar * br - ai * bi, ar * bi + ai * br


def _time_permutation(tc):
    r = jnp.arange(tc)
    src = (r % 8) * (tc // 8) + r // 8
    p = (src[:, None] == jnp.arange(tc)[None, :]).astype(BF16)
    return p, p.T


def _unpermute(pt, x):
    hi = x.astype(BF16)
    lo = (x - hi.astype(F32)).astype(BF16)
    return jnp.dot(pt, hi, preferred_element_type=F32) + jnp.dot(pt, lo, preferred_element_type=F32)


def _segment_scan(xr_ref, xi_ref, ar, ai, cr, ci, ng, reverse):
    n = ar.shape[-1]
    row = lax.broadcasted_iota(jnp.int32, (8, n), 0)
    seeded = 7 if reverse else 0
    a8r = jnp.broadcast_to(ar, (8, n))
    a8i = jnp.broadcast_to(ai, (8, n))
    rows_of = lambda g: pl.ds(pl.multiple_of(((ng - 1 - g) if reverse else g) * 8, 8), 8)

    def recur(g, s):
        rows = rows_of(g)
        sr = a8r * s[0] - a8i * s[1] + xr_ref[rows, :]
        si = a8r * s[1] + a8i * s[0] + xi_ref[rows, :]
        xr_ref[rows, :] = sr
        xi_ref[rows, :] = si
        return sr, si

    fr, fi = lax.fori_loop(0, ng, recur, (jnp.where(row == seeded, cr, 0.0), jnp.where(row == seeded, ci, 0.0)))
    pr, pi = ar, ai
    for _ in range(ng.bit_length() - 1):
        pr, pi = _cmul(pr, pi, pr, pi)
    for k in (1, 2, 4):
        keep = (row < 8 - k) if reverse else (row >= k)
        shift = (8 - k) if reverse else k
        mr, mi = jnp.where(keep, pr, 0.0), jnp.where(keep, pi, 0.0)
        tr, ti = pltpu.roll(fr, shift, 0), pltpu.roll(fi, shift, 0)
        fr, fi = fr + mr * tr - mi * ti, fi + mr * ti + mi * tr
        pr, pi = _cmul(pr, pi, pr, pi)
    shift = 7 if reverse else 1
    before_r, before_i = pltpu.roll(fr, shift, 0), pltpu.roll(fi, shift, 0)

    def inherit(g, d):
        rows = rows_of(g)
        dr = a8r * d[0] - a8i * d[1]
        di = a8r * d[1] + a8i * d[0]
        xr_ref[rows, :] = xr_ref[rows, :] + dr
        xi_ref[rows, :] = xi_ref[rows, :] + di
        return dr, di

    lax.fori_loop(0, ng, inherit, (jnp.where(row == seeded, 0.0, before_r), jnp.where(row == seeded, 0.0, before_i)))
    out = 0 if reverse else 7
    return (fr[out:out + 1], fi[out:out + 1],
            jnp.where(row == seeded, cr, before_r), jnp.where(row == seeded, ci, before_i))


def _blockdiag(x):
    g, a, b = x.shape
    j = g // SSM_LANE_GROUPS
    eye = jnp.eye(SSM_LANE_GROUPS, dtype=x.dtype)
    y = x.reshape(j, SSM_LANE_GROUPS, a, 1, b) * eye[None, :, None, :, None]
    return y.reshape(j, SSM_LANE_GROUPS * a, SSM_LANE_GROUPS * b)


def _blockdiag_extract(y, a, b):
    j = y.shape[0]
    y = y.reshape(j, SSM_LANE_GROUPS, a, SSM_LANE_GROUPS, b)
    return jnp.einsum("jgahb,gh->jgab", y, jnp.eye(SSM_LANE_GROUPS, dtype=y.dtype)).reshape(j * SSM_LANE_GROUPS, a, b)


def _ssm_disc(lr, li, ldt, brt, bit):
    dt = jnp.exp(ldt)
    mag = jnp.exp(lr * dt)
    ar = mag * jnp.cos(li * dt)
    ai = mag * jnp.sin(li * dt)
    den = lr * lr + li * li
    fr = ((ar - 1.0) * lr + ai * li) / den
    fi = (ai * lr - (ar - 1.0) * li) / den
    bbr = fr[:, None, :] * brt - fi[:, None, :] * bit
    bbi = fr[:, None, :] * bit + fi[:, None, :] * brt
    return ar, ai, bbr, bbi


def _ssm_prep(lr, li, ldt, brt, bit, *, name):
    g, h, p = brt.shape

    def body(lr_ref, li_ref, ldt_ref, brt_ref, bit_ref, ar_ref, ai_ref, bbr_ref, bbi_ref):
        ar, ai, bbr, bbi = _ssm_disc(lr_ref[...], li_ref[...], ldt_ref[...], brt_ref[...], bit_ref[...])
        ar_ref[...] = ar
        ai_ref[...] = ai
        bbr_ref[...] = bbr
        bbi_ref[...] = bbi

    gp = jax.ShapeDtypeStruct((g, p), F32)
    ghp = jax.ShapeDtypeStruct((g, h, p), F32)
    return pl.pallas_call(body, out_shape=(gp, gp, ghp, ghp), name=name)(lr, li, ldt, brt, bit)


def _ssm_prep_bwd(lr, li, ldt, brt, bit, dar, dai, dbbr, dbbi, *, name):
    g, h, p = brt.shape

    def body(lr_ref, li_ref, ldt_ref, brt_ref, bit_ref, dar_ref, dai_ref, dbbr_ref, dbbi_ref,
             dlr_ref, dli_ref, dldt_ref, dbrt_ref, dbit_ref):
        _, vjp = jax.vjp(_ssm_disc, lr_ref[...], li_ref[...], ldt_ref[...], brt_ref[...], bit_ref[...])
        dlr, dli, dldt, dbrt, dbit = vjp((dar_ref[...], dai_ref[...], dbbr_ref[...], dbbi_ref[...]))
        dlr_ref[...] = dlr
        dli_ref[...] = dli
        dldt_ref[...] = dldt
        dbrt_ref[...] = dbrt
        dbit_ref[...] = dbit

    gp = jax.ShapeDtypeStruct((g, p), F32)
    ghp = jax.ShapeDtypeStruct((g, h, p), F32)
    return pl.pallas_call(body, out_shape=(gp, gp, jax.ShapeDtypeStruct((g, 1), F32), ghp, ghp), name=name)(
        lr, li, ldt, brt, bit, dar, dai, dbbr, dbbi)


def _ssm_specs(tc, nlanes, nch, u_colblk, chunk_of):
    return [
        pl.BlockSpec((tc, nch), lambda j, c: (chunk_of(c), u_colblk + j)),
        pl.BlockSpec((1, nlanes), lambda j, c: (0, j)),
        pl.BlockSpec((1, nlanes), lambda j, c: (0, j)),
        pl.BlockSpec((None, nch, nlanes), lambda j, c: (j, 0, 0)),
        pl.BlockSpec((None, nch, nlanes), lambda j, c: (j, 0, 0)),
        pl.BlockSpec((None, nlanes, nch), lambda j, c: (j, 0, 0)),
        pl.BlockSpec((None, nlanes, nch), lambda j, c: (j, 0, 0)),
        pl.BlockSpec((1, nch), lambda j, c: (0, j)),
        pl.BlockSpec((tc, tc), lambda j, c: (0, 0)),
        pl.BlockSpec((tc, tc), lambda j, c: (0, 0)),
    ]


def _ssm_fwd(z, ar, ai, bblk_r, bblk_i, cblk_r, cblk_i, d, *, u_col, tc, name, deps=()):
    L = z.shape[0]
    nj, nch, nlanes = bblk_r.shape
    w = nj * nch
    nc = L // tc
    ng = tc // 8

    assert ng & (ng - 1) == 0
    perm, perm_t = _time_permutation(tc)

    def body(u_ref, ar_ref, ai_ref, br_ref, bi_ref, cr_ref, ci_ref, d_ref, p_ref, pt_ref, *rest):
        y_ref, s0r_ref, s0i_ref, xr_ref, xi_ref, carr_ref, cari_ref = rest[len(deps):]
        c = pl.program_id(1)

        @pl.when(c == 0)
        def _():
            carr_ref[...] = jnp.zeros_like(carr_ref)
            cari_ref[...] = jnp.zeros_like(cari_ref)

        s0r_ref[...] = carr_ref[...]
        s0i_ref[...] = cari_ref[...]
        u = u_ref[...]
        ub = jnp.dot(p_ref[...], u.astype(BF16), preferred_element_type=F32).astype(BF16)
        xr_ref[...] = jnp.dot(ub, br_ref[...].astype(BF16), preferred_element_type=F32)
        xi_ref[...] = jnp.dot(ub, bi_ref[...].astype(BF16), preferred_element_type=F32)
        cr, ci, _, _ = _segment_scan(xr_ref, xi_ref, ar_ref[...], ai_ref[...], carr_ref[...], cari_ref[...], ng, False)
        carr_ref[...] = cr
        cari_ref[...] = ci
        y = (jnp.dot(xr_ref[...].astype(BF16), cr_ref[...].astype(BF16), preferred_element_type=F32)
             - jnp.dot(xi_ref[...].astype(BF16), ci_ref[...].astype(BF16), preferred_element_type=F32))
        y_ref[...] = _unpermute(pt_ref[...], y) + d_ref[...] * u

    state = jax.ShapeDtypeStruct((nc, 1, nj * nlanes), F32)
    state_spec = pl.BlockSpec((None, 1, nlanes), lambda j, c: (c, 0, j))
    return pl.pallas_call(
        body,
        out_shape=(jax.ShapeDtypeStruct((L, w), F32), state, state),
        grid=(nj, nc),
        in_specs=_ssm_specs(tc, nlanes, nch, u_col // nch, lambda c: c) + [pl.BlockSpec(memory_space=pl.ANY)] * len(deps),
        out_specs=(pl.BlockSpec((tc, nch), lambda j, c: (c, j)), state_spec, state_spec),
        scratch_shapes=[pltpu.VMEM((tc, nlanes), F32), pltpu.VMEM((tc, nlanes), F32),
                        pltpu.VMEM((1, nlanes), F32), pltpu.VMEM((1, nlanes), F32)],
        compiler_params=_cparams("parallel", "arbitrary"),
        name=name,
    )(z, ar, ai, bblk_r, bblk_i, cblk_r, cblk_i, d, perm, perm_t, *deps)


def _ssm_bwd(z, dy, s0r, s0i, ar, ai, bblk_r, bblk_i, cblk_r, cblk_i, d, *, u_col, tc, name):
    L = z.shape[0]
    nj, nch, nlanes = bblk_r.shape
    w = nj * nch
    nc = L // tc
    ng = tc // 8
    chunk_of = lambda c: nc - 1 - c
    assert ng & (ng - 1) == 0
    perm, perm_t = _time_permutation(tc)

    def body(u_ref, ar_ref, ai_ref, br_ref, bi_ref, cr_ref, ci_ref, d_ref, p_ref, pt_ref, dy_ref, s0r_ref, s0i_ref,
             du_ref, dbr_ref, dbi_ref, dcr_ref, dci_ref, dar_ref, dai_ref, dd_ref,
             sr_ref, si_ref, lr_ref, li_ref, carr_ref, cari_ref):
        c = pl.program_id(1)

        @pl.when(c == 0)
        def _():
            for ref in (dbr_ref, dbi_ref, dcr_ref, dci_ref, dar_ref, dai_ref, dd_ref, carr_ref, cari_ref):
                ref[...] = jnp.zeros_like(ref)

        u = u_ref[...]
        dyv = dy_ref[...]
        both = jnp.dot(p_ref[...], jnp.concatenate([u.astype(BF16), dyv.astype(BF16)], axis=1), preferred_element_type=F32)
        ub = both[:, :nch].astype(BF16)
        dyb = both[:, nch:].astype(BF16)
        brb = br_ref[...].astype(BF16)
        bib = bi_ref[...].astype(BF16)
        crb = cr_ref[...].astype(BF16)
        cib = ci_ref[...].astype(BF16)
        a_r, a_i = ar_ref[...], ai_ref[...]

        sr_ref[...] = jnp.dot(ub, brb, preferred_element_type=F32)
        si_ref[...] = jnp.dot(ub, bib, preferred_element_type=F32)
        _, _, start_r, start_i = _segment_scan(sr_ref, si_ref, a_r, a_i, s0r_ref[...], s0i_ref[...], ng, False)

        nt = (((1,), (1,)), ((), ()))
        lr_ref[...] = lax.dot_general(dyb, crb, nt, preferred_element_type=F32)
        li_ref[...] = -lax.dot_general(dyb, cib, nt, preferred_element_type=F32)
        cr, ci, _, _ = _segment_scan(lr_ref, li_ref, a_r, -a_i, carr_ref[...], cari_ref[...], ng, True)
        carr_ref[...] = cr
        cari_ref[...] = ci

        def accumulate(g, carry):
            pr, pi, acc_r, acc_i = carry
            rows = pl.ds(pl.multiple_of(g * 8, 8), 8)
            lr, li = lr_ref[rows, :], li_ref[rows, :]
            return sr_ref[rows, :], si_ref[rows, :], acc_r + lr * pr + li * pi, acc_i + li * pr - lr * pi

        zero8 = jnp.zeros((8, nlanes), F32)
        _, _, acc_r, acc_i = lax.fori_loop(0, ng, accumulate, (start_r, start_i, zero8, zero8))
        dar_ref[...] += jnp.sum(acc_r, axis=0, keepdims=True)
        dai_ref[...] += jnp.sum(acc_i, axis=0, keepdims=True)

        tn = (((0,), (0,)), ((), ()))
        lrb = lr_ref[...].astype(BF16)
        lib = li_ref[...].astype(BF16)
        dcr_ref[...] += lax.dot_general(sr_ref[...].astype(BF16), dyb, tn, preferred_element_type=F32)
        dci_ref[...] -= lax.dot_general(si_ref[...].astype(BF16), dyb, tn, preferred_element_type=F32)
        dbr_ref[...] += lax.dot_general(ub, lrb, tn, preferred_element_type=F32)
        dbi_ref[...] += lax.dot_general(ub, lib, tn, preferred_element_type=F32)
        du = (lax.dot_general(lrb, brb, nt, preferred_element_type=F32)
              + lax.dot_general(lib, bib, nt, preferred_element_type=F32))
        du_ref[...] = (_unpermute(pt_ref[...], du) + d_ref[...] * dyv).astype(du_ref.dtype)
        dd_ref[...] += jnp.sum(dyv * u, axis=0, keepdims=True)

    state_spec = pl.BlockSpec((None, 1, nlanes), lambda j, c: (chunk_of(c), 0, j))
    bshape = jax.ShapeDtypeStruct((nj, nch, nlanes), F32)
    cshape = jax.ShapeDtypeStruct((nj, nlanes, nch), F32)
    ashape = jax.ShapeDtypeStruct((1, nj * nlanes), F32)
    bspec = pl.BlockSpec((None, nch, nlanes), lambda j, c: (j, 0, 0))
    cspec = pl.BlockSpec((None, nlanes, nch), lambda j, c: (j, 0, 0))
    aspec = pl.BlockSpec((1, nlanes), lambda j, c: (0, j))
    big = pltpu.VMEM((tc, nlanes), F32)
    return pl.pallas_call(
        body,
        out_shape=(jax.ShapeDtypeStruct((L, w), BF16), bshape, bshape, cshape, cshape, ashape, ashape,
                   jax.ShapeDtypeStruct((1, w), F32)),
        grid=(nj, nc),
        in_specs=_ssm_specs(tc, nlanes, nch, u_col // nch, chunk_of)
        + [pl.BlockSpec((tc, nch), lambda j, c: (chunk_of(c), j)), state_spec, state_spec],
        out_specs=(pl.BlockSpec((tc, nch), lambda j, c: (chunk_of(c), j)), bspec, bspec, cspec, cspec, aspec, aspec,
                   pl.BlockSpec((1, nch), lambda j, c: (0, j))),
        scratch_shapes=[big, big, big, big, pltpu.VMEM((1, nlanes), F32), pltpu.VMEM((1, nlanes), F32)],
        compiler_params=_cparams("parallel", "arbitrary"),
        name=name,
    )(z, ar, ai, bblk_r, bblk_i, cblk_r, cblk_i, d, perm, perm_t, dy, s0r, s0i)


def _rmsnorm_rows(x, g):
    return x * lax.rsqrt(jnp.mean(x * x, axis=-1, keepdims=True) + RMS_EPS) * g


def _glu_out(y_raw, pre, b):
    yg = jax.nn.gelu(y_raw)
    return yg * jax.nn.sigmoid(pre + b)


def _gate_merge(za, zs, ba, bs, a, bm):
    return jax.nn.sigmoid(za + ba) * a + jax.nn.sigmoid(zs + bs) * bm


def _swiglu(g, u):
    return jax.nn.silu(g) * u


def _row_tile(width_bytes_per_row, rows):
    budget = VMEM_LIMIT_BYTES // 3
    t = max(8, min(1024, budget // (2 * max(width_bytes_per_row, 1))))
    return _pick(rows, t, 16)


def _ssm_params(p, prefix):
    g, pst = p["lam_re"].shape
    ar, ai, bbr, bbi = _ssm_prep(p["lam_re"], p["lam_im"], p["log_dt"], p["b_re_t"], p["b_im_t"], name=prefix + "_ssm_prep")
    return dict(ar=ar.reshape(1, g * pst), ai=ai.reshape(1, g * pst),
                bblk_r=_blockdiag(bbr), bblk_i=_blockdiag(bbi),
                cblk_r=_blockdiag(jnp.swapaxes(p["c_re"], 1, 2)), cblk_i=_blockdiag(jnp.swapaxes(p["c_im"], 1, 2)))


def _layer_fwd(x, p, dims, prefix, deps=(), tick=None):
    tick = tick or (lambda point, arr: ())
    t, d = x.shape
    aw, kvw, sw, ff = dims["aw"], dims["kvw"], dims["sw"], dims["ff"]
    off_u = aw + 2 * kvw
    off_g = off_u + sw
    gblk = _pick(d, 512, 128)
    assert off_g % gblk == 0 and off_u % (SSM_LANE_GROUPS * SSM_GROUP_CH) == 0
    sv = {"x": x}

    h, = _rowmap(_rmsnorm_rows, [(x, "row", d, 0), (p["norm_mix_g"], "vec", d, 0)], [(d, BF16, "row", d)],
                 rows=t, tm=_row_tile(6 * d, t), name=prefix + "_norm_mix", deps=deps)
    z = _mm_nn(h, p["w_in"], out_dtype=F32, tm=1024, tn=1664, tk=2048, name=prefix + "_mm_in")
    ya = _attn_fwd(z, p["q_norm_g"], p["k_norm_g"], p["attn_sinks"], n_q=dims["n_q"], n_kv=dims["n_kv"],
                   name=prefix + "_attn_fwd", deps=tick("in", z))
    sp = _ssm_params(p, prefix)
    y_raw, s0r, s0i = _ssm_fwd(z, sp["ar"], sp["ai"], sp["bblk_r"], sp["bblk_i"], sp["cblk_r"], sp["cblk_i"], p["ssm_d"],
                               u_col=off_u, tc=dims["tc"], name=prefix + "_ssm_fwd", deps=tick("attn", ya))
    yg, = _rowmap(jax.nn.gelu, [(y_raw, "row", sw, 0)], [(sw, BF16, "row", sw)],
                  rows=t, tm=_row_tile(6 * sw, t), name=prefix + "_gelu", deps=tick("ssm", y_raw))
    pre = _mm_nn(yg, p["ssm_glu_w"], out_dtype=F32, tm=1024, tn=1024, tk=1024, name=prefix + "_mm_glu")
    y2, = _rowmap(_glu_out, [(y_raw, "row", sw, 0), (pre, "row", sw, 0), (p["ssm_glu_b"], "vec", sw, 0)],
                  [(sw, BF16, "row", sw)], rows=t, tm=_row_tile(10 * sw, t), name=prefix + "_glu_out")
    a = _mm_nn(ya, p["w_attn_branch"], out_dtype=BF16, tm=1024, tn=512, tk=1024, name=prefix + "_mm_ab")
    bm = _mm_nn(y2, p["w_ssm_branch"], out_dtype=BF16, tm=1024, tn=512, tk=1024, name=prefix + "_mm_sb")
    ngb = d // gblk
    merged, = _rowmap(
        _gate_merge,
        [(z, "row", gblk, off_g // gblk), (z, "row", gblk, off_g // gblk + ngb),
         (p["gate_bias"], "vec", gblk, 0), (p["gate_bias"], "vec", gblk, ngb),
         (a, "row", gblk, 0), (bm, "row", gblk, 0)],
        [(d, BF16, "row", gblk)], rows=t, tm=_row_tile(18 * gblk, t), ncol=ngb, name=prefix + "_gate")
    x1 = _mm_nn(merged, p["w_out"], out_dtype=F32, tm=512, tn=1024, tk=2048, name=prefix + "_mm_out", add=x)
    h2, = _rowmap(_rmsnorm_rows, [(x1, "row", d, 0), (p["norm_ffn_g"], "vec", d, 0)], [(d, BF16, "row", d)],
                  rows=t, tm=_row_tile(6 * d, t), name=prefix + "_norm_ffn", deps=tick("out", x1))
    gu = _mm_nn(h2, p["w_ffn_in"], out_dtype=BF16, tm=1024, tn=1408, tk=2048, name=prefix + "_mm_ffn_in")
    fblk = _pick(ff, 1408, 128)
    nfb = ff // fblk
    act, = _rowmap(_swiglu, [(gu, "row", fblk, 0), (gu, "row", fblk, nfb)], [(ff, BF16, "row", fblk)],
                   rows=t, tm=_row_tile(10 * fblk, t), ncol=nfb, name=prefix + "_swiglu", deps=tick("ffn_in", gu))
    x2 = _mm_nn(act, p["w_ffn_out"], out_dtype=F32, tm=512, tn=512, tk=5632, name=prefix + "_mm_ffn_out", add=x1)
    sv.update(h=h, z=z, ya=ya, sp=sp, y_raw=y_raw, s0r=s0r, s0i=s0i, yg=yg, pre=pre, y2=y2, a=a, bm=bm,
              merged=merged, x1=x1, h2=h2, gu=gu, act=act)
    return x2, sv


def _layer_bwd(dx2, dx2b, sv, p, dims, prefix, gbuf, deps=(), before_mixer=None, before_in=None):
    t, d = dx2.shape
    aw, kvw, sw, ff = dims["aw"], dims["kvw"], dims["sw"], dims["ff"]
    off_u = aw + 2 * kvw
    off_g = off_u + sw
    gblk = _pick(d, 512, 128)
    ngb = d // gblk
    fblk = _pick(ff, 1408, 128)
    nfb = ff // fblk
    g = {}

    dact = _mm_nt(dx2b, p["w_ffn_out"], out_dtype=BF16, tm=512, tn=2048, tko=1408, name=prefix + "_mm_dact", deps=deps)
    g["w_ffn_out"] = _mm_tn(sv["act"], dx2b, p["w_ffn_out"], into=gbuf.get("w_ffn_out"), tm=4096, tn=1024, tko=512,
                            name=prefix + "_mm_dw_ffn_out")

    def swiglu_bwd(gg, uu, da):
        _, vjp = jax.vjp(_swiglu, gg, uu)
        return vjp(da)

    dgu_g, dgu_u = _rowmap(swiglu_bwd, [(sv["gu"], "row", fblk, 0), (sv["gu"], "row", fblk, nfb), (dact, "row", fblk, 0)],
                           [(ff, BF16, "row", fblk), (ff, BF16, "row", fblk)],
                           rows=t, tm=_row_tile(16 * fblk, t), ncol=nfb, name=prefix + "_swiglu_bwd")
    dgu = jnp.concatenate([dgu_g, dgu_u], axis=1)
    dh2 = _mm_nt(dgu, p["w_ffn_in"], out_dtype=F32, tm=256, tn=1408, tko=512, name=prefix + "_mm_dh2")
    g["w_ffn_in"] = _mm_tn(sv["h2"], dgu, p["w_ffn_in"], into=gbuf.get("w_ffn_in"), tm=4096, tn=1408, tko=512,
                           name=prefix + "_mm_dw_ffn_in")

    def norm_bwd(xx, gg, dh, dres):
        _, vjp = jax.vjp(_rmsnorm_rows, xx, gg)
        dxx, dgg = vjp(dh)
        dxx = dxx + dres
        return dxx, dxx, dgg

    dx1, dx1b, g["norm_ffn_g"] = _rowmap(
        norm_bwd, [(sv["x1"], "row", d, 0), (p["norm_ffn_g"], "vec", d, 0), (dh2, "row", d, 0), (dx2, "row", d, 0)],
        [(d, F32, "row", d), (d, BF16, "row", d), (d, F32, "acc", d)],
        rows=t, tm=_row_tile(22 * d, t), name=prefix + "_norm_ffn_bwd")

    deps = before_mixer(dx1, g) if before_mixer else ()
    dmerged = _mm_nt(dx1b, p["w_out"], out_dtype=BF16, tm=1024, tn=2048, tko=1024, name=prefix + "_mm_dmerged", deps=deps)
    g["w_out"] = _mm_tn(sv["merged"], dx1b, p["w_out"], into=gbuf.get("w_out"), tm=4096, tn=1024, tko=512,
                        name=prefix + "_mm_dw_out")

    def gate_bwd(za, zs, ba, bs, aa, bb, dm):
        _, vjp = jax.vjp(_gate_merge, za, zs, ba, bs, aa, bb)
        dza, dzs, dba, dbs, daa, dbb = vjp(dm)
        return daa, dbb, dza, dzs, dba, dbs

    z = sv["z"]
    da, dbm, dza, dzs, dba, dbs = _rowmap(
        gate_bwd,
        [(z, "row", gblk, off_g // gblk), (z, "row", gblk, off_g // gblk + ngb),
         (p["gate_bias"], "vec", gblk, 0), (p["gate_bias"], "vec", gblk, ngb),
         (sv["a"], "row", gblk, 0), (sv["bm"], "row", gblk, 0), (dmerged, "row", gblk, 0)],
        [(d, BF16, "row", gblk), (d, BF16, "row", gblk), (d, BF16, "row", gblk), (d, BF16, "row", gblk),
         (d, F32, "acc", gblk), (d, F32, "acc", gblk)],
        rows=t, tm=_row_tile(32 * gblk, t), ncol=ngb, name=prefix + "_gate_bwd")
    g["gate_bias"] = jnp.concatenate([dba, dbs], axis=1)
    dya = _mm_nt(da, p["w_attn_branch"], out_dtype=BF16, tm=1024, tn=512, tko=1024, name=prefix + "_mm_dya")
    g["w_attn_branch"] = _mm_tn(sv["ya"], da, p["w_attn_branch"], into=gbuf.get("w_attn_branch"), tm=4096, tn=512,
                                tko=512, name=prefix + "_mm_dw_ab")
    dy2 = _mm_nt(dbm, p["w_ssm_branch"], out_dtype=BF16, tm=1024, tn=512, tko=1024, name=prefix + "_mm_dy2")
    g["w_ssm_branch"] = _mm_tn(sv["y2"], dbm, p["w_ssm_branch"], into=gbuf.get("w_ssm_branch"), tm=4096, tn=512,
                               tko=512, name=prefix + "_mm_dw_sb")

    def glu_bwd(y_raw, pre, b, dy):
        yg = jax.nn.gelu(y_raw)
        _, vjp = jax.vjp(lambda a_, b_, c_: a_ * jax.nn.sigmoid(b_ + c_), yg, pre, b)
        dyg, dpre, db = vjp(dy)
        return dyg, dpre, db

    dyg_direct, dpre, g["ssm_glu_b"] = _rowmap(
        glu_bwd, [(sv["y_raw"], "row", sw, 0), (sv["pre"], "row", sw, 0), (p["ssm_glu_b"], "vec", sw, 0), (dy2, "row", sw, 0)],
        [(sw, F32, "row", sw), (sw, BF16, "row", sw), (sw, F32, "acc", sw)],
        rows=t, tm=_row_tile(24 * sw, t), name=prefix + "_glu_bwd")
    dyg2 = _mm_nt(dpre, p["ssm_glu_w"], out_dtype=F32, tm=1024, tn=1024, tko=1024, name=prefix + "_mm_dyg")
    g["ssm_glu_w"] = _mm_tn(sv["yg"], dpre, p["ssm_glu_w"], into=gbuf.get("ssm_glu_w"), tm=4096, tn=1024, tko=512,
                            name=prefix + "_mm_dw_glu")

    def gelu_bwd(y_raw, d1, d2):
        _, vjp = jax.vjp(jax.nn.gelu, y_raw)
        return vjp(d1 + d2)[0]

    dy_raw, = _rowmap(gelu_bwd, [(sv["y_raw"], "row", sw, 0), (dyg_direct, "row", sw, 0), (dyg2, "row", sw, 0)],
                      [(sw, F32, "row", sw)], rows=t, tm=_row_tile(20 * sw, t), name=prefix + "_gelu_bwd")
    sp = sv["sp"]
    du, dbr, dbi, dcr, dci, dar, dai, g["ssm_d"] = _ssm_bwd(
        z, dy_raw, sv["s0r"], sv["s0i"], sp["ar"], sp["ai"], sp["bblk_r"], sp["bblk_i"], sp["cblk_r"], sp["cblk_i"],
        p["ssm_d"], u_col=off_u, tc=dims["tc"], name=prefix + "_ssm_bwd")
    ngr, pst = p["lam_re"].shape
    hch = SSM_GROUP_CH
    dlr, dli, dldt, dbrt, dbit = _ssm_prep_bwd(
        p["lam_re"], p["lam_im"], p["log_dt"], p["b_re_t"], p["b_im_t"],
        dar.reshape(ngr, pst), dai.reshape(ngr, pst), _blockdiag_extract(dbr, hch, pst), _blockdiag_extract(dbi, hch, pst),
        name=prefix + "_ssm_prep_bwd")
    g.update(ssm_lambda_re=dlr, ssm_lambda_im=dli, ssm_log_dt=dldt.reshape(ngr),
             ssm_b_re=jnp.swapaxes(dbrt, 1, 2), ssm_b_im=jnp.swapaxes(dbit, 1, 2),
             ssm_c_re=jnp.swapaxes(_blockdiag_extract(dcr, pst, hch), 1, 2),
             ssm_c_im=jnp.swapaxes(_blockdiag_extract(dci, pst, hch), 1, 2))

    dq, dkv, g["q_norm_g"], g["k_norm_g"], g["attn_sinks"] = _attn_bwd(
        z, dya, p["q_norm_g"], p["k_norm_g"], p["attn_sinks"], n_q=dims["n_q"], n_kv=dims["n_kv"], name=prefix + "_attn_bwd")

    dz = jnp.concatenate([dq, dkv, du, dza, dzs], axis=1)
    deps = before_in(dq, g) if before_in else ()
    dh = _mm_nt(dz, p["w_in"], out_dtype=F32, tm=512, tn=1664, tko=512, name=prefix + "_mm_dh", deps=deps)
    g["w_in"] = _mm_tn(sv["h"], dz, p["w_in"], into=gbuf.get("w_in"), tm=4096, tn=1664, tko=512,
                       name=prefix + "_mm_dw_in")
    dx, dxb, g["norm_mix_g"] = _rowmap(
        norm_bwd, [(sv["x"], "row", d, 0), (p["norm_mix_g"], "vec", d, 0), (dh, "row", d, 0), (dx1, "row", d, 0)],
        [(d, F32, "row", d), (d, BF16, "row", d), (d, F32, "acc", d)],
        rows=t, tm=_row_tile(22 * d, t), name=prefix + "_norm_mix_bwd")
    return dx, dxb, g


def _loss_and_grad(y, target):
    t, d = y.shape

    def fn(yy, tt):
        e = yy - tt
        dy = e * (1.0 / d)
        return dy, dy, jnp.sum(e * e, keepdims=True).reshape(1, 1)

    dy, dyb, sq = _rowmap(fn, [(y, "row", d, 0), (target, "row", d, 0)],
                          [(d, F32, "row", d), (d, BF16, "row", d), (1, F32, "acc", 1)],
                          rows=t, tm=_row_tile(14 * d, t), name="loss")
    return sq, dy, dyb


def _local_step(x, target, n_layers, weights_of, dims, fwd_tick=None, bwd_tick=None):
    saved, params = [], []
    h = x
    for l in range(n_layers):
        p, deps = weights_of(l, h)
        params.append(p)
        tick = (lambda point, arr, l=l: fwd_tick(l, point, arr)) if fwd_tick else None
        h, sv = _layer_fwd(h, p, dims, "l%d" % l, deps, tick)
        saved.append(sv)
    sq, dy, dyb = _loss_and_grad(h, target)
    grads = [None] * n_layers
    deps = ()
    for l in reversed(range(n_layers)):
        if bwd_tick:
            ffn_done = lambda arr, g, l=l: bwd_tick(l, "ffn", arr, {n: g[n] for n in ("w_ffn_in", "w_ffn_out")})
            mixer_done = lambda arr, g, l=l: bwd_tick(l, "mixer", arr, {n: g[n] for n in MIXER_WEIGHTS})
        else:
            ffn_done = mixer_done = None
        dy, dyb, grads[l] = _layer_bwd(dy, dyb, saved[l], params[l], dims, "l%d" % l, {}, deps=deps,
                                       before_mixer=ffn_done, before_in=mixer_done)
        if bwd_tick:
            deps = bwd_tick(l, "in", dy, {"w_in": grads[l]["w_in"]})
    return sq, dy, grads


COL_SHARDED = ("w_in", "w_attn_branch", "w_ssm_branch", "w_ffn_in")
ROW_SHARDED = ("ssm_glu_w", "w_out", "w_ffn_out")
BIG_WEIGHTS = COL_SHARDED + ROW_SHARDED
WEIGHT_NAMES = ("norm_mix_g", "w_in", "gate_bias", "q_norm_g", "k_norm_g", "attn_sinks", "ssm_lambda_re",
                "ssm_lambda_im", "ssm_log_dt", "ssm_b_re", "ssm_b_im", "ssm_c_re", "ssm_c_im", "ssm_d", "ssm_glu_w",
                "ssm_glu_b", "w_attn_branch", "w_ssm_branch", "w_out", "norm_ffn_g", "w_ffn_in", "w_ffn_out")
SMALL_WEIGHTS = tuple(n for n in WEIGHT_NAMES if n not in BIG_WEIGHTS)
MIXER_WEIGHTS = ("w_out", "w_attn_branch", "w_ssm_branch", "ssm_glu_w")
WEIGHT_GROUPS = {"in": ("w_in",), "mixer": MIXER_WEIGHTS, "ffn": ("w_ffn_in", "w_ffn_out")}


def _dims(d, shapes, tc):
    s, _, aw, _ = shapes["w_attn_branch"]
    sw = shapes["w_ssm_branch"][2]
    in_w = shapes["w_in"][3] * s
    kvw = (in_w - aw - sw - 2 * d) // 2
    ff = shapes["w_ffn_out"][2] * s
    return dict(aw=aw, kvw=kvw, sw=sw, ff=ff, n_q=aw // HEAD_DIM, n_kv=kvw // HEAD_DIM, tc=tc)


def _big_params(big):
    p = {n: _Weight(a, 0, "col") for n, a in big.items() if n in COL_SHARDED}
    p.update({n: _Weight(a.reshape(1, 1, -1, a.shape[-1]), 0, "col") for n, a in big.items() if n in ROW_SHARDED})
    return p


def _layer_params(l, small):
    p = {}
    for n in ("norm_mix_g", "gate_bias", "q_norm_g", "k_norm_g", "ssm_d", "ssm_glu_b", "norm_ffn_g"):
        p[n] = small[n][l][None]
    p["attn_sinks"] = small["attn_sinks"][l]
    p["lam_re"] = small["ssm_lambda_re"][l]
    p["lam_im"] = small["ssm_lambda_im"][l]
    p["log_dt"] = small["ssm_log_dt"][l][:, None]
    p["b_re_t"] = jnp.swapaxes(small["ssm_b_re"][l], 1, 2)
    p["b_im_t"] = jnp.swapaxes(small["ssm_b_im"][l], 1, 2)
    p["c_re"] = small["ssm_c_re"][l]
    p["c_im"] = small["ssm_c_im"][l]
    return p


_ANY = pl.BlockSpec(memory_space=pl.ANY)
_MESH_ID = pl.DeviceIdType.MESH


def _coords():
    return lax.axis_index("x"), lax.axis_index("y"), lax.axis_index("c")


def _remote(src, dst, send_sems, recv_sems, k, to):
    return pltpu.make_async_remote_copy(src_ref=src, dst_ref=dst, send_sem=send_sems.at[k], recv_sem=recv_sems.at[k],
                                        device_id=to, device_id_type=_MESH_ID)


def _comm_call(body, ins, out_shapes, n_remote, name, aliases=None, scratch=()):
    return pl.pallas_call(
        body,
        out_shape=tuple(out_shapes),
        in_specs=[_ANY] * len(ins),
        out_specs=tuple([_ANY] * len(out_shapes)),
        scratch_shapes=[pltpu.SemaphoreType.DMA((n_remote,)), pltpu.SemaphoreType.DMA((n_remote,))] + list(scratch),
        input_output_aliases=aliases or {},
        compiler_params=pltpu.CompilerParams(has_side_effects=True),
        name=name,
    )(*ins)


def _my_chip():
    return (2 * lax.axis_index("x") + lax.axis_index("y")).astype(jnp.int32).reshape(1)


def _my_core():
    return lax.axis_index("c").astype(jnp.int32).reshape(1)


def _cast_into_slot(w, layer, *, name):
    _, r, c = w.shape
    tm = _row_tile(12 * c, r)

    def body(me_ref, w_ref, o_ref):
        o_ref[...] = w_ref[...].astype(o_ref.dtype)

    return pl.pallas_call(
        body,
        out_shape=jax.ShapeDtypeStruct((N_CHIPS, 1, r, c), BF16),
        grid_spec=pltpu.PrefetchScalarGridSpec(
            num_scalar_prefetch=1,
            grid=(r // tm,),
            in_specs=[pl.BlockSpec((None, tm, c), lambda i, me: (layer, i, 0))],
            out_specs=pl.BlockSpec((None, None, tm, c), lambda i, me: (me[0], 0, i, 0)),
        ),
        compiler_params=_cparams("parallel"),
        name=name,
    )(_my_chip(), w)


def _allgather_weights(bufs, *, name):
    n = len(bufs)

    def body(*refs):
        outs = refs[n:2 * n]
        send_sems, recv_sems = refs[2 * n:]
        x, y, c = _coords()
        me = 2 * x + y
        chips = [(1 - x, y), (x, 1 - y), (1 - x, 1 - y)]
        sibling = (x, y, 1 - c)

        def half(i, slot, hc):
            rh = bufs[i].shape[2] // 2
            return outs[i].at[slot, :, pl.ds(hc * rh, rh), :]

        first = [_remote(half(i, me, c), half(i, me, c), send_sems, recv_sems, 6 * i + k, (px, py, c))
                 for i in range(n) for k, (px, py) in enumerate(chips)]
        for cp in first:
            cp.start()
        passed = []
        for k, (px, py) in enumerate(chips):
            for i in range(n):
                landed = half(i, 2 * px + py, c)
                _remote(landed, landed, send_sems, recv_sems, 6 * i + k, (px, py, c)).wait_recv()
                fw = _remote(landed, landed, send_sems, recv_sems, 6 * i + 3 + k, sibling)
                fw.start()
                passed.append(fw)
        for k, (px, py) in enumerate(chips):
            for i in range(n):
                other = half(i, 2 * px + py, 1 - c)
                _remote(other, other, send_sems, recv_sems, 6 * i + 3 + k, sibling).wait_recv()
        for cp in first + passed:
            cp.wait_send()

    outs = [jax.ShapeDtypeStruct(b.shape, b.dtype) for b in bufs]
    return _comm_call(body, bufs, outs, 6 * n, name, aliases={i: i for i in range(n)})


_HBM = pl.BlockSpec(memory_space=pltpu.HBM)
_SEM = pl.BlockSpec(memory_space=pltpu.SEMAPHORE)
_DATAFLOW = pltpu.SideEffectType.DATAFLOW_SIDE_EFFECTING


class _SplitExchange:
    def __init__(self, srcs, lands, build, n_copies, name):
        self.build, self.n, self.name = build, n_copies, name
        self.ns, self.nl = len(srcs), len(lands)
        self.bufs = [pltpu.with_memory_space_constraint(a, pltpu.HBM) for a in list(srcs) + list(lands)]

    def _copies(self, refs, send_sems, recv_sems):
        triples = self.build(refs[:self.ns], refs[self.ns:self.ns + self.nl])
        assert len(triples) == self.n
        return [pltpu.make_async_remote_copy(src_ref=s, dst_ref=d, send_sem=send_sems.at[k], recv_sem=recv_sems.at[k],
                                             device_id=to, device_id_type=_MESH_ID) for k, (s, d, to) in enumerate(triples)]

    def start(self, deps=()):
        nb = self.ns + self.nl

        def body(*refs):
            outs = refs[nb + len(deps):]
            for cp in self._copies(refs, outs[0], outs[1]):
                cp.start()
            outs[-1][...] = jnp.zeros_like(outs[-1])

        sems = pltpu.SemaphoreType.DMA((self.n,))
        res = pl.pallas_call(
            body,
            out_shape=(sems, sems, *[pltpu.HBM(b.shape, b.dtype) for b in self.bufs], jax.ShapeDtypeStruct((8, 128), F32)),
            in_specs=[_HBM] * nb + [_ANY] * len(deps),
            out_specs=(_SEM, _SEM, *[_HBM] * nb, pl.BlockSpec(memory_space=pltpu.VMEM)),
            input_output_aliases={i: 2 + i for i in range(nb)},
            compiler_params=pltpu.CompilerParams(has_side_effects=_DATAFLOW),
            name=self.name + "_start",
        )(*self.bufs, *deps)
        self.send_sems, self.recv_sems = res[0], res[1]
        self.bufs = list(res[2:2 + nb])
        return res[-1]

    def wait(self, after=()):
        nb = self.ns + self.nl

        def body(*refs):
            for cp in self._copies(refs, refs[nb], refs[nb + 1]):
                cp.wait_send()
                cp.wait_recv()

        res = pl.pallas_call(
            body,
            out_shape=tuple(pltpu.HBM(b.shape, b.dtype) for b in self.bufs),
            in_specs=[_HBM] * nb + [_SEM, _SEM] + [_ANY] * len(after),
            out_specs=tuple([_HBM] * nb),
            input_output_aliases={i: i for i in range(nb)},
            compiler_params=pltpu.CompilerParams(has_side_effects=_DATAFLOW),
            name=self.name + "_wait",
        )(*self.bufs, self.send_sems, self.recv_sems, *after)
        res = list(res)
        return res[:self.ns], res[self.ns:]


def _other_chips(x, y):
    return [(1 - x, y), (x, 1 - y), (1 - x, 1 - y)]


def _gather_steps(bufs, tag, deps, publish):
    n = len(bufs)
    half = lambda ref, i, slot, hc: ref.at[slot, :, pl.ds(hc * (bufs[i].shape[2] // 2), bufs[i].shape[2] // 2), :]

    def over_ici(srcs, lands):
        x, y, c = _coords()
        me = 2 * x + y
        return [(half(srcs[i], i, me, c), half(srcs[i], i, me, c), (px, py, c))
                for i in range(n) for px, py in _other_chips(x, y)]

    def to_sibling(srcs, lands):
        x, y, c = _coords()
        return [(half(srcs[i], i, 2 * px + py, c), half(srcs[i], i, 2 * px + py, c), (x, y, 1 - c))
                for i in range(n) for px, py in _other_chips(x, y)]

    ex = _SplitExchange(bufs, [], over_ici, 3 * n, tag + "_ici")
    after = yield ex.start(deps)
    bufs, _ = ex.wait((after,))
    ex = _SplitExchange(bufs, [], to_sibling, 3 * n, tag + "_d2d")
    after = yield ex.start()
    bufs, _ = ex.wait((after,))
    publish(bufs)


def _reduce_steps(grads, tag, publish):
    n = len(grads)
    rh = [g.shape[1] // 2 for g in grads]
    theirs = [lax.empty((g.shape[0], g.shape[1] // 2, g.shape[2]), F32) for g in grads]

    def halves(srcs, lands):
        x, y, c = _coords()
        return [(srcs[i].at[:, pl.ds((1 - c) * rh[i], rh[i]), :], lands[i], (x, y, 1 - c)) for i in range(n)]

    def chips(srcs, lands):
        x, y, c = _coords()
        me = 2 * x + y
        return [(srcs[i].at[2 * px + py], lands[i].at[me], (px, py, c)) for i in range(n) for px, py in _other_chips(x, y)]

    def sibling(srcs, lands):
        x, y, c = _coords()
        return [(srcs[i], lands[i], (x, y, 1 - c)) for i in range(n)]

    ex = _SplitExchange(grads, theirs, halves, n, tag + "_halves")
    after = yield ex.start()
    grads, theirs = ex.wait((after,))
    parts = [_add_own_half(g, t, name="%s_add_own_half_%d" % (tag, i)) for i, (g, t) in enumerate(zip(grads, theirs))]
    ex = _SplitExchange(parts, [lax.empty(p.shape, p.dtype) for p in parts], chips, 3 * n, tag + "_chips")
    after = yield ex.start()
    parts, got = ex.wait((after,))
    mine = [_sum_chips(p, g, name="%s_sum_chips_%d" % (tag, i)) for i, (p, g) in enumerate(zip(parts, got))]
    ex = _SplitExchange(mine, [lax.empty(m.shape, m.dtype) for m in mine], sibling, n, tag + "_sibling")
    after = yield ex.start()
    mine, theirs = ex.wait((after,))
    publish(list(zip(mine, theirs)))


def _allreduce_steps(buf, tag, publish):
    r, c = buf.shape

    def to_sibling(srcs, lands):
        x, y, cc = _coords()
        return [(srcs[0], lands[0], (x, y, 1 - cc))]

    def over_ici(srcs, lands):
        x, y, cc = _coords()
        me = 2 * x + y
        return [(srcs[0].at[me], srcs[0].at[me], (px, py, cc)) for px, py in _other_chips(x, y)]

    def halves(srcs, lands):
        x, y, cc = _coords()
        return [(srcs[0].at[cc], srcs[0].at[cc], (x, y, 1 - cc))]

    ex = _SplitExchange([buf], [lax.empty(buf.shape, buf.dtype)], to_sibling, 1, tag + "_cores")
    after = yield ex.start()
    (mine,), (theirs,) = ex.wait((after,))
    ex = _SplitExchange([_add_half_into_slot(mine, theirs, name=tag + "_chip_sum")], [], over_ici, N_CHIPS - 1, tag + "_chips")
    after = yield ex.start()
    (parts,), _ = ex.wait((after,))
    total_half = _sum_slots(parts, name=tag + "_sum_chips")
    ex = _SplitExchange([_place_into_slot(total_half, 2, _my_core(), name=tag + "_place_half")], [], halves, 1, tag + "_halves")
    after = yield ex.start()
    (both,), _ = ex.wait((after,))
    publish(both.reshape(r, c))


class _Exchanges:
    def __init__(self):
        self.running = []

    def launch(self, steps):
        self.running.append(steps)
        return next(steps)

    def advance(self, steps, after):
        try:
            return steps.send(after)
        except StopIteration:
            self.running.remove(steps)
            return None

    def advance_all(self, after):
        tokens = [self.advance(steps, after) for steps in list(self.running)]
        return tuple(t for t in tokens if t is not None)


def _add_own_half(g, theirs, *, name):
    s, r, c = g.shape
    rh = r // 2
    tm = _row_tile(10 * c, rh)
    nb = rh // tm

    def body(core_ref, g_ref, t_ref, o_ref):
        o_ref[...] = (g_ref[...] + t_ref[...]).astype(o_ref.dtype)

    return pl.pallas_call(
        body,
        out_shape=jax.ShapeDtypeStruct((s, rh, c), BF16),
        grid_spec=pltpu.PrefetchScalarGridSpec(
            num_scalar_prefetch=1,
            grid=(s, nb),
            in_specs=[pl.BlockSpec((None, tm, c), lambda k, i, core: (k, core[0] * nb + i, 0)),
                      pl.BlockSpec((None, tm, c), lambda k, i, core: (k, i, 0))],
            out_specs=pl.BlockSpec((None, tm, c), lambda k, i, core: (k, i, 0)),
        ),
        compiler_params=_cparams("parallel", "parallel"),
        name=name,
    )(_my_core(), g, theirs)


def _sum_chips(part, got, *, name):
    s, rh, c = part.shape
    tm = _row_tile(14 * c, rh)

    def body(me_ref, p_ref, a_ref, b_ref, c_ref, o_ref):
        o_ref[...] = ((p_ref[...].astype(F32) + a_ref[...].astype(F32)) + b_ref[...].astype(F32)) + c_ref[...].astype(F32)

    slot = lambda k: (lambda i, me: ((me[0] + k) % s, i, 0))
    return pl.pallas_call(
        body,
        out_shape=jax.ShapeDtypeStruct((rh, c), F32),
        grid_spec=pltpu.PrefetchScalarGridSpec(
            num_scalar_prefetch=1,
            grid=(rh // tm,),
            in_specs=[pl.BlockSpec((None, tm, c), slot(k)) for k in range(s)],
            out_specs=pl.BlockSpec((tm, c), lambda i, me: (i, 0)),
        ),
        compiler_params=_cparams("parallel"),
        name=name,
    )(_my_chip(), part, got, got, got)


def _place_into_slot(buf, n_slots, slot, *, name):
    r, c = buf.shape
    tm = _row_tile(8 * c, r)

    def body(slot_ref, i_ref, o_ref):
        o_ref[...] = i_ref[...]

    return pl.pallas_call(
        body,
        out_shape=jax.ShapeDtypeStruct((n_slots, r, c), buf.dtype),
        grid_spec=pltpu.PrefetchScalarGridSpec(
            num_scalar_prefetch=1,
            grid=(r // tm,),
            in_specs=[pl.BlockSpec((tm, c), lambda i, s: (i, 0))],
            out_specs=pl.BlockSpec((None, tm, c), lambda i, s: (s[0], i, 0)),
        ),
        compiler_params=_cparams("parallel"),
        name=name,
    )(slot, buf)


def _add_half_into_slot(mine, theirs, *, name):
    r, c = mine.shape
    rh = r // 2
    tm = _row_tile(12 * c, rh)
    nb = rh // tm
    where = jnp.concatenate([_my_chip(), _my_core()])

    def body(where_ref, a_ref, b_ref, o_ref):
        o_ref[...] = a_ref[...] + b_ref[...]

    half = pl.BlockSpec((tm, c), lambda i, w: (w[1] * nb + i, 0))
    return pl.pallas_call(
        body,
        out_shape=jax.ShapeDtypeStruct((N_CHIPS, rh, c), mine.dtype),
        grid_spec=pltpu.PrefetchScalarGridSpec(
            num_scalar_prefetch=1,
            grid=(nb,),
            in_specs=[half, half],
            out_specs=pl.BlockSpec((None, tm, c), lambda i, w: (w[0], i, 0)),
        ),
        compiler_params=_cparams("parallel"),
        name=name,
    )(where, mine, theirs)


def _sum_slots(arr, *, name):
    s, r, c = arr.shape
    tm = _row_tile(4 * c * (s + 1), r)

    def body(*refs):
        acc = refs[0][...]
        for ref in refs[1:s]:
            acc = acc + ref[...]
        refs[s][...] = acc

    return pl.pallas_call(
        body,
        out_shape=jax.ShapeDtypeStruct((r, c), arr.dtype),
        grid=(r // tm,),
        in_specs=[pl.BlockSpec((None, tm, c), lambda i, k=k: (k, i, 0)) for k in range(s)],
        out_specs=pl.BlockSpec((tm, c), lambda i: (i, 0)),
        compiler_params=_cparams("parallel"),
        name=name,
    )(*([arr] * s))


def _adamw_fn(w, g, m, v):
    m = ADAM_B1 * m + (1.0 - ADAM_B1) * g
    v = ADAM_B2 * v + (1.0 - ADAM_B2) * jnp.square(g)
    m_hat = m / (1.0 - ADAM_B1 ** ADAM_STEP)
    v_hat = v / (1.0 - ADAM_B2 ** ADAM_STEP)
    delta = -ADAM_LR * (m_hat / (jnp.sqrt(v_hat) + ADAM_EPS) + ADAM_WD * w)
    return delta, m, v


def _adamw(w, g, m, v, *, name):
    rows, cols = w.shape
    ins = [(a, "row", cols, 0) for a in (w, g, m, v)]
    outs = [(cols, F32, "row", cols)] * 3
    return _rowmap(_adamw_fn, ins, outs, rows=rows, tm=_row_tile(56 * cols, rows), name=name)


def _adamw_sharded(w, m, v, g_mine, g_sibling, layer, into, *, name, deps=()):
    nl, r, c = w.shape
    rh = r // 2
    tm = _row_tile(40 * c, rh)
    nb = rh // tm
    n_into = 0 if into is None else 4

    def body(core_ref, w_ref, m_ref, v_ref, a_ref, b_ref, *rest):
        g_ref, d_ref, nm_ref, nv_ref = rest[n_into + len(deps):]
        g = jnp.where(pl.program_id(0) == core_ref[0], a_ref[...], b_ref[...])
        delta, nm, nv = _adamw_fn(w_ref[...], g, m_ref[...], v_ref[...])
        g_ref[...] = g
        d_ref[...] = delta
        nm_ref[...] = nm
        nv_ref[...] = nv

    whole = pl.BlockSpec((None, tm, c), lambda h, i, core: (layer, h * nb + i, 0))
    half = pl.BlockSpec((tm, c), lambda h, i, core: (i, 0))
    shape = jax.ShapeDtypeStruct((nl, r, c), F32)
    return pl.pallas_call(
        body,
        out_shape=(shape, shape, shape, shape),
        grid_spec=pltpu.PrefetchScalarGridSpec(
            num_scalar_prefetch=1,
            grid=(2, nb),
            in_specs=[whole, whole, whole, half, half] + [pl.BlockSpec(memory_space=pl.ANY)] * (n_into + len(deps)),
            out_specs=(whole, whole, whole, whole),
        ),
        input_output_aliases={6 + k: k for k in range(n_into)},
        compiler_params=_cparams("parallel", "parallel"),
        name=name,
    )(_my_core(), w, m, v, g_mine, g_sibling, *(into or ()), *deps)


def _pack(arrays):
    flat = jnp.concatenate([a.reshape(-1) for a in arrays])
    pad = (-flat.shape[0]) % (256 * 128)
    return jnp.pad(flat, (0, pad)).reshape(-1, 128)


def _unpack(buf, shapes):
    flat = buf.reshape(-1)
    out, off = [], 0
    for s in shapes:
        n = math.prod(s)
        out.append(flat[off:off + n].reshape(s))
        off += n
    return out


def kernel(x, norm_mix_g, w_in, gate_bias, q_norm_g, k_norm_g, attn_sinks, ssm_lambda_re, ssm_lambda_im, ssm_log_dt, ssm_b_re, ssm_b_im, ssm_c_re, ssm_c_im, ssm_d, ssm_glu_w, ssm_glu_b, w_attn_branch, w_ssm_branch, w_out, norm_ffn_g, w_ffn_in, w_ffn_out, loss_target, m_norm_mix_g, m_w_in, m_gate_bias, m_q_norm_g, m_k_norm_g, m_attn_sinks, m_ssm_lambda_re, m_ssm_lambda_im, m_ssm_log_dt, m_ssm_b_re, m_ssm_b_im, m_ssm_c_re, m_ssm_c_im, m_ssm_d, m_ssm_glu_w, m_ssm_glu_b, m_w_attn_branch, m_w_ssm_branch, m_w_out, m_norm_ffn_g, m_w_ffn_in, m_w_ffn_out, v_norm_mix_g, v_w_in, v_gate_bias, v_q_norm_g, v_k_norm_g, v_attn_sinks, v_ssm_lambda_re, v_ssm_lambda_im, v_ssm_log_dt, v_ssm_b_re, v_ssm_b_im, v_ssm_c_re, v_ssm_c_im, v_ssm_d, v_ssm_glu_w, v_ssm_glu_b, v_w_attn_branch, v_w_ssm_branch, v_w_out, v_norm_ffn_g, v_w_ffn_in, v_w_ffn_out):
    w = dict(norm_mix_g=norm_mix_g, w_in=w_in, gate_bias=gate_bias, q_norm_g=q_norm_g, k_norm_g=k_norm_g,
             attn_sinks=attn_sinks, ssm_lambda_re=ssm_lambda_re, ssm_lambda_im=ssm_lambda_im, ssm_log_dt=ssm_log_dt,
             ssm_b_re=ssm_b_re, ssm_b_im=ssm_b_im, ssm_c_re=ssm_c_re, ssm_c_im=ssm_c_im, ssm_d=ssm_d,
             ssm_glu_w=ssm_glu_w, ssm_glu_b=ssm_glu_b, w_attn_branch=w_attn_branch, w_ssm_branch=w_ssm_branch,
             w_out=w_out, norm_ffn_g=norm_ffn_g, w_ffn_in=w_ffn_in, w_ffn_out=w_ffn_out)
    m = dict(norm_mix_g=m_norm_mix_g, w_in=m_w_in, gate_bias=m_gate_bias, q_norm_g=m_q_norm_g, k_norm_g=m_k_norm_g,
             attn_sinks=m_attn_sinks, ssm_lambda_re=m_ssm_lambda_re, ssm_lambda_im=m_ssm_lambda_im,
             ssm_log_dt=m_ssm_log_dt, ssm_b_re=m_ssm_b_re, ssm_b_im=m_ssm_b_im, ssm_c_re=m_ssm_c_re,
             ssm_c_im=m_ssm_c_im, ssm_d=m_ssm_d, ssm_glu_w=m_ssm_glu_w, ssm_glu_b=m_ssm_glu_b,
             w_attn_branch=m_w_attn_branch, w_ssm_branch=m_w_ssm_branch, w_out=m_w_out, norm_ffn_g=m_norm_ffn_g,
             w_ffn_in=m_w_ffn_in, w_ffn_out=m_w_ffn_out)
    v = dict(norm_mix_g=v_norm_mix_g, w_in=v_w_in, gate_bias=v_gate_bias, q_norm_g=v_q_norm_g, k_norm_g=v_k_norm_g,
             attn_sinks=v_attn_sinks, ssm_lambda_re=v_ssm_lambda_re, ssm_lambda_im=v_ssm_lambda_im,
             ssm_log_dt=v_ssm_log_dt, ssm_b_re=v_ssm_b_re, ssm_b_im=v_ssm_b_im, ssm_c_re=v_ssm_c_re,
             ssm_c_im=v_ssm_c_im, ssm_d=v_ssm_d, ssm_glu_w=v_ssm_glu_w, ssm_glu_b=v_ssm_glu_b,
             w_attn_branch=v_w_attn_branch, w_ssm_branch=v_w_ssm_branch, w_out=v_w_out, norm_ffn_g=v_norm_ffn_g,
             w_ffn_in=v_w_ffn_in, w_ffn_out=v_w_ffn_out)
    n_layers = norm_mix_g.shape[0]
    d_model = x.shape[-1]
    seq = x.shape[1]

    exchanges = _Exchanges()
    params = [_layer_params(l, w) for l in range(n_layers)]
    gathers = {}

    def gather(l, group, deps=()):
        names = WEIGHT_GROUPS[group]
        bufs = [_cast_into_slot(w[n], l, name="cast%d_%s" % (l, n)) for n in names]
        steps = _gather_steps(bufs, "ag%d_%s" % (l, group), deps,
                              lambda got: params[l].update(_big_params(dict(zip(names, got)))))
        gathers[l, group] = steps
        return exchanges.launch(steps)

    w_in0 = _allgather_weights([_cast_into_slot(w["w_in"], 0, name="cast0_w_in")], name="ag0_in")[0]
    params[0].update(_big_params({"w_in": w_in0}))
    first_deps = (gather(0, "mixer", (w_in0,)), gather(0, "ffn", (w_in0,)))
    sizes = {n: (N_CHIPS, 1) + w[n].shape[1:] for n in BIG_WEIGHTS}
    dims = _dims(d_model, sizes, min(512, seq))

    def weights_of(l, h):
        if l == 0:
            return params[0], first_deps
        for group in WEIGHT_GROUPS:
            exchanges.advance(gathers[l, group], h)
        return params[l], ()

    def fwd_tick(l, point, arr):
        tokens = []
        if l == 0 and point in ("in", "attn"):
            tokens.append(exchanges.advance(gathers[0, "mixer"], arr))
        if l == 0 and point in ("ssm", "out"):
            tokens.append(exchanges.advance(gathers[0, "ffn"], arr))
        if l + 1 < n_layers and point == "attn":
            tokens += [gather(l + 1, group) for group in WEIGHT_GROUPS]
        if l + 1 < n_layers and point == "ffn_in":
            tokens += [exchanges.advance(gathers[l + 1, group], arr) for group in WEIGHT_GROUPS]
        return tuple(t for t in tokens if t is not None)

    reduced, ready = {}, []

    def bwd_tick(l, stage, arr, stage_grads):
        tokens = exchanges.advance_all(arr)
        names = tuple(stage_grads)

        def publish(halves):
            reduced.update({(l, n): h for n, h in zip(names, halves)})
            ready.append((l, names))

        grads4 = [stage_grads[n].reshape(N_CHIPS, -1, stage_grads[n].shape[-1]) for n in names]
        return tokens + (exchanges.launch(_reduce_steps(grads4, "rs%d_%s" % (l, stage), publish)),)

    sq, dx, grads = _local_step(x[0], loss_target[0], n_layers, weights_of, dims, fwd_tick, bwd_tick)
    loss = lax.psum(sq[0, 0], MESH_AXES) * (0.5 / d_model)

    small_shapes = [w[n].shape for n in SMALL_WEIGHTS]
    small_local = [jnp.stack([grads[l][n].reshape(w[n].shape[1:]) for l in range(n_layers)]) for n in SMALL_WEIGHTS]
    shared = []
    tokens = (exchanges.launch(_allreduce_steps(_pack(small_local), "small_grads", shared.append)),)

    adam = {n: None for n in BIG_WEIGHTS}
    grad, delta, new_m, new_v = {}, {}, {}, {}
    after = dx
    while exchanges.running or ready or shared:
        for l, names in ready:
            for n in names:
                mine, sibling = reduced[l, n]
                adam[n] = _adamw_sharded(w[n], m[n], v[n], mine, sibling, l, adam[n], name="adamw%d_%s" % (l, n), deps=tokens)
                after = adam[n][1]
        del ready[:]
        if shared:
            small_sum = shared.pop()
            grad.update(zip(SMALL_WEIGHTS, _unpack(small_sum, small_shapes)))
            res = _adamw(_pack([w[n] for n in SMALL_WEIGHTS]), small_sum, _pack([m[n] for n in SMALL_WEIGHTS]),
                         _pack([v[n] for n in SMALL_WEIGHTS]), name="adamw_small")
            for out, packed in zip((delta, new_m, new_v), res):
                out.update(zip(SMALL_WEIGHTS, _unpack(packed, small_shapes)))
            after = res[0]
        tokens = exchanges.advance_all(after)
    for n in BIG_WEIGHTS:
        grad[n], delta[n], new_m[n], new_v[n] = adam[n]

    return (loss, dx[None], *[grad[n] for n in WEIGHT_NAMES], *[delta[n] for n in WEIGHT_NAMES],
            *[new_m[n] for n in WEIGHT_NAMES], *[new_v[n] for n in WEIGHT_NAMES])
```

```python
import functools
import math

import jax
import jax.numpy as jnp
from jax import lax
from jax.experimental import pallas as pl
from jax.experimental.pallas import tpu as pltpu

HEAD_DIM = 64
WINDOW = 128
SSM_GROUP_CH = 16
SSM_LANE_GROUPS = 8
RMS_EPS = 1e-6
ADAM_LR = 0.001
ADAM_B1 = 0.9
ADAM_B2 = 0.999
ADAM_EPS = 1e-08
ADAM_WD = 0.01
ADAM_STEP = 10
NEG_BIG = -1e30
MESH_AXES = ("x", "y", "c")
N_CHIPS = 4
N_DEV = 8
VMEM_LIMIT_BYTES = 56 * 1024 * 1024
BF16 = jnp.bfloat16
F32 = jnp.float32


def _cparams(*semantics):
    return pltpu.CompilerParams(dimension_semantics=semantics, vmem_limit_bytes=VMEM_LIMIT_BYTES)


def _pick(n, target, mult):
    if n <= target:
        return n
    best = None
    for d in range(mult, target + 1, mult):
        if n % d == 0:
            best = d
    assert best is not None, (n, target, mult)
    return best


def _rowmap(fn, ins, outs, *, rows, tm, ncol=1, name, deps=()):
    n_in = len(ins)
    nrow = rows // tm
    assert nrow * tm == rows

    in_specs = []
    for arr, kind, width, coloff in ins:
        if kind == "row":
            in_specs.append(pl.BlockSpec((tm, width), lambda j, i, o=coloff: (i, o + j)))
        elif kind == "vec":
            in_specs.append(pl.BlockSpec((1, width), lambda j, i, o=coloff: (0, o + j)))
        else:
            nd = arr.ndim
            in_specs.append(pl.BlockSpec(arr.shape, lambda j, i, nd=nd: (0,) * nd))
    out_specs, out_shapes = [], []
    for cols, dtype, kind, width in outs:
        if kind == "row":
            out_specs.append(pl.BlockSpec((tm, width), lambda j, i: (i, j)))
            out_shapes.append(jax.ShapeDtypeStruct((rows, cols), dtype))
        else:
            out_specs.append(pl.BlockSpec((1, width), lambda j, i: (0, j)))
            out_shapes.append(jax.ShapeDtypeStruct((1, cols), dtype))

    in_specs += [pl.BlockSpec(memory_space=pl.ANY)] * len(deps)

    def body(*refs):
        i = pl.program_id(1)
        res = fn(*[r[...].astype(F32) for r in refs[:n_in]])
        if not isinstance(res, (tuple, list)):
            res = (res,)
        for (cols, dtype, kind, width), ref, val in zip(outs, refs[n_in + len(deps):], res):
            if kind == "row":
                ref[...] = val.astype(ref.dtype)
            else:
                @pl.when(i == 0)
                def _():
                    ref[...] = jnp.zeros_like(ref)
                ref[...] += val.astype(ref.dtype)

    res = pl.pallas_call(
        body,
        out_shape=tuple(out_shapes),
        grid=(ncol, nrow),
        in_specs=in_specs,
        out_specs=tuple(out_specs),
        compiler_params=_cparams("parallel", "arbitrary"),
        name=name,
    )(*[a[0] for a in ins], *deps)
    return res


def _mm_body(dims, nk, has_add, unused_in=0):
    def body(*refs):
        if has_add:
            a_ref, b_ref, add_ref = refs[:3]
            o_ref = refs[3 + unused_in]
            rest = refs[4 + unused_in:]
        else:
            a_ref, b_ref = refs[:2]
            o_ref = refs[2 + unused_in]
            add_ref = None
            rest = refs[3 + unused_in:]
        part = lax.dot_general(a_ref[...], b_ref[...], (dims, ((), ())), preferred_element_type=F32)
        if nk == 1:
            if add_ref is not None:
                part = part + add_ref[...]
            o_ref[...] = part.astype(o_ref.dtype)
        else:
            acc_ref = rest[0]
            k = pl.program_id(2)

            @pl.when(k == 0)
            def _():
                acc_ref[...] = part

            @pl.when(k > 0)
            def _():
                acc_ref[...] += part

            @pl.when(k == nk - 1)
            def _():
                r = acc_ref[...]
                if add_ref is not None:
                    r = r + add_ref[...]
                o_ref[...] = r.astype(o_ref.dtype)
    return body


class _Weight:
    def __init__(self, arr, layer, kind):
        self.arr, self.layer, self.kind = arr, layer, kind
        self.s, _, self.r, self.c = arr.shape
        self.rows = self.r * (self.s if kind == "row" else 1)
        self.cols = self.c * (self.s if kind == "col" else 1)

    def tiles(self, tr, tc):
        return _pick(self.r, tr, 128), _pick(self.c, tc, 128)

    def index(self, tr, tc):
        layer = self.layer
        if self.kind == "col":
            per = self.c // tc
            return lambda rb, cb: (cb // per, layer, rb, cb % per)
        per = self.r // tr
        return lambda rb, cb: (rb // per, layer, rb % per, cb)


def _shard_index(kind, r, c, tr, tc):
    if kind == "col":
        per = c // tc
        return lambda rb, cb: (cb // per, rb, cb % per)
    per = r // tr
    return lambda rb, cb: (rb // per, rb % per, cb)


def _mm_nn(a, w, *, out_dtype, tm, tn, tk, name, add=None, deps=()):
    m, k = a.shape
    assert k == w.rows
    tm = _pick(m, tm, 16)
    tk, tn = w.tiles(tk, tn)
    nk = k // tk
    widx = w.index(tk, tn)
    in_specs = [pl.BlockSpec((tm, tk), lambda n, i, kk: (i, kk)),
                pl.BlockSpec((None, None, tk, tn), lambda n, i, kk: widx(kk, n))]
    args = [a, w.arr]
    if add is not None:
        in_specs.append(pl.BlockSpec((tm, tn), lambda n, i, kk: (i, n)))
        args.append(add)
    in_specs += [pl.BlockSpec(memory_space=pl.ANY)] * len(deps)
    args += list(deps)
    return pl.pallas_call(
        _mm_body(((1,), (0,)), nk, add is not None, unused_in=len(deps)),
        out_shape=jax.ShapeDtypeStruct((m, w.cols), out_dtype),
        grid=(w.cols // tn, m // tm, nk),
        in_specs=in_specs,
        out_specs=pl.BlockSpec((tm, tn), lambda n, i, kk: (i, n)),
        scratch_shapes=[pltpu.VMEM((tm, tn), F32)] if nk > 1 else [],
        compiler_params=_cparams("parallel", "parallel", "arbitrary"),
        name=name,
    )(*args)


def _mm_nt(a, w, *, out_dtype, tm, tn, tko, name, deps=()):
    m, n = a.shape
    assert n == w.cols
    tm = _pick(m, tm, 16)
    if w.kind == "col" and w.s > 1:
        tko = _pick(w.r, tko, 128)
        layer, nsh, width = w.layer, w.s, w.c

        def body(a_ref, w_ref, *rest):
            o_ref = rest[len(deps)]
            acc = None
            for s in range(nsh):
                part = lax.dot_general(a_ref[:, s * width:(s + 1) * width], w_ref[s], (((1,), (1,)), ((), ())),
                                       preferred_element_type=F32)
                acc = part if acc is None else acc + part
            o_ref[...] = acc.astype(o_ref.dtype)

        return pl.pallas_call(
            body,
            out_shape=jax.ShapeDtypeStruct((m, w.rows), out_dtype),
            grid=(w.rows // tko, m // tm),
            in_specs=[pl.BlockSpec((tm, n), lambda ko, i: (i, 0)),
                      pl.BlockSpec((nsh, None, tko, width), lambda ko, i: (0, layer, ko, 0))]
            + [pl.BlockSpec(memory_space=pl.ANY)] * len(deps),
            out_specs=pl.BlockSpec((tm, tko), lambda ko, i: (i, ko)),
            compiler_params=_cparams("parallel", "parallel"),
            name=name,
        )(a, w.arr, *deps)
    tko, tn = w.tiles(tko, tn)
    nk = n // tn
    widx = w.index(tko, tn)
    return pl.pallas_call(
        _mm_body(((1,), (1,)), nk, False, unused_in=len(deps)),
        out_shape=jax.ShapeDtypeStruct((m, w.rows), out_dtype),
        grid=(w.rows // tko, m // tm, nk),
        in_specs=[pl.BlockSpec((tm, tn), lambda ko, i, nn: (i, nn)),
                  pl.BlockSpec((None, None, tko, tn), lambda ko, i, nn: widx(ko, nn))]
        + [pl.BlockSpec(memory_space=pl.ANY)] * len(deps),
        out_specs=pl.BlockSpec((tm, tko), lambda ko, i, nn: (i, ko)),
        scratch_shapes=[pltpu.VMEM((tm, tko), F32)] if nk > 1 else [],
        compiler_params=_cparams("parallel", "parallel", "arbitrary"),
        name=name,
    )(a, w.arr, *deps)


def _mm_tn(a, c, w, *, into, tm, tn, tko, name):
    m, k = a.shape
    tm = _pick(m, tm, 16)
    layer = w.layer
    mc, n = c.shape
    assert mc == m and k == w.rows and n == w.cols
    tko, tn = w.tiles(tko, tn)
    nk = m // tm
    oidx = _shard_index(w.kind, w.r, w.c, tko, tn)
    in_specs = [pl.BlockSpec((tm, tko), lambda ko, nn, mm: (mm, ko)),
                pl.BlockSpec((tm, tn), lambda ko, nn, mm: (mm, nn))]
    args = [a, c]
    if into is not None:
        in_specs.append(pl.BlockSpec(memory_space=pl.ANY))
        args.append(into)
    return pl.pallas_call(
        _mm_body(((0,), (0,)), nk, False, unused_in=len(args) - 2),
        out_shape=jax.ShapeDtypeStruct((w.arr.shape[1], w.s, w.r, w.c), F32),
        grid=(k // tko, n // tn, nk),
        in_specs=in_specs,
        out_specs=pl.BlockSpec((None, None, tko, tn), lambda ko, nn, mm: (layer,) + oidx(ko, nn)),
        scratch_shapes=[pltpu.VMEM((tko, tn), F32)] if nk > 1 else [],
        input_output_aliases={2: 0} if into is not None else {},
        compiler_params=_cparams("parallel", "parallel", "arbitrary"),
        name=name,
    )(*args)


def _mxu_sum(x, ones):
    hi = x.astype(BF16)
    lo = (x - hi.astype(F32)).astype(BF16)
    return jnp.dot(hi, ones, preferred_element_type=F32) + jnp.dot(lo, ones, preferred_element_type=F32)


def _head_rms(x, gain, head_ones):
    r = lax.rsqrt(_mxu_sum(x * x, head_ones) * (1.0 / HEAD_DIM) + RMS_EPS)
    return x * r * gain, r


def _head_rms_bwd(x, r, gain, dy, head_ones, fold):
    t = dy * gain
    dx = r * t - x * (r * r * r) * (_mxu_sum(t * x, head_ones) * (1.0 / HEAD_DIM))
    dg = jnp.broadcast_to(jnp.sum(dy * x * r, axis=0, keepdims=True), (8, x.shape[1]))
    return dx, _mxu_sum(dg, fold)[0:1, :HEAD_DIM]


def _attn_consts(n_q, n_kv, sinks, qg, kg):
    group = n_q // n_kv
    t = jnp.arange(WINDOW, dtype=jnp.int32)[:, None]
    s = jnp.arange(2 * WINDOW, dtype=jnp.int32)[None, :] - WINDOW
    dist = (t - s).astype(F32)
    valid = (dist >= 0) & (dist < WINDOW)
    slopes = jnp.exp2(-8.0 * jnp.arange(1, n_q + 1, dtype=F32) / n_q)
    bias = jnp.where(valid[None], -slopes[:, None, None] * dist[None], NEG_BIG)
    sink = jnp.broadcast_to(sinks.astype(F32).reshape(n_kv, group, 1, 1), (n_kv, group, WINDOW, 128))
    head_ones = lambda h: jnp.kron(jnp.eye(h, dtype=F32), jnp.ones((HEAD_DIM, HEAD_DIM), F32)).astype(BF16)
    fold = lambda h: jnp.tile(jnp.eye(HEAD_DIM, 128, dtype=F32), (h, 1)).astype(BF16)
    return dict(
        bias=bias.reshape(n_kv, group * WINDOW, 2 * WINDOW),
        sink=sink.reshape(n_kv, group * WINDOW, 128),
        qg=jnp.tile(qg, (1, n_q)), kg=jnp.tile(kg, (1, n_kv)),
        q_ones=head_ones(n_q), k_ones=head_ones(n_kv),
        q_fold=fold(n_q), k_fold=fold(n_kv),
        key_ones=jnp.ones((2 * WINDOW, 128), BF16))


_ATTN_CONST_ORDER = ("qg", "kg", "sink", "bias", "q_ones", "k_ones", "q_fold", "k_fold", "key_ones")


def _attn_inputs(q_ref, kc_ref, kp_ref, vc_ref, vp_ref, c):
    q = q_ref[...]
    k2 = jnp.concatenate([kp_ref[...], kc_ref[...]], axis=0)
    v2 = jnp.concatenate([vp_ref[...], vc_ref[...]], axis=0)
    qn, rq = _head_rms(q, c["qg"][...], c["q_ones"][...])
    kn, rk = _head_rms(k2, c["kg"][...], c["k_ones"][...])
    return dict(q=q, rq=rq, qn=qn.astype(BF16), k2=k2, rk=rk, kn=kn.astype(BF16), v2=v2.astype(BF16))


def _attn_probs(x, c, first_mask, kv, group):
    sl = slice(kv * HEAD_DIM, (kv + 1) * HEAD_DIM)
    k2b, v2b = x["kn"][:, sl], x["v2"][:, sl]
    qs = jnp.concatenate([x["qn"][:, (kv * group + g) * HEAD_DIM:(kv * group + g + 1) * HEAD_DIM]
                          for g in range(group)], axis=0)
    s = lax.dot_general(qs, k2b, (((1,), (1,)), ((), ())), preferred_element_type=F32) * (HEAD_DIM ** -0.5)
    s = jnp.where(first_mask, NEG_BIG, s + c["bias"][kv])
    sink = c["sink"][kv]
    m = jnp.maximum(jnp.max(s, axis=-1, keepdims=True), sink)
    twice = lambda a: jnp.concatenate([a, a], axis=1)
    p = jnp.exp(s - twice(m))
    esink = jnp.exp(sink - m)
    inv = 1.0 / (_mxu_sum(p, c["key_ones"][...]) + esink)
    return dict(k2b=k2b, v2b=v2b, qs=qs, pn=p * twice(inv), psink=esink * inv, twice=twice)


def _attn_specs(n_q, n_kv):
    aw, kvw = n_q * HEAD_DIM, n_kv * HEAD_DIM
    group = n_q // n_kv
    kblk, vblk = aw // kvw, aw // kvw + 1

    def specs(nb):
        cur = lambda n: jnp.minimum(n, nb - 1)
        prev = lambda n: jnp.maximum(jnp.minimum(n, nb - 1) - 1, 0)
        return [
            pl.BlockSpec((WINDOW, aw), lambda n: (cur(n), 0)),
            pl.BlockSpec((WINDOW, kvw), lambda n: (cur(n), kblk)),
            pl.BlockSpec((WINDOW, kvw), lambda n: (prev(n), kblk)),
            pl.BlockSpec((WINDOW, kvw), lambda n: (cur(n), vblk)),
            pl.BlockSpec((WINDOW, kvw), lambda n: (prev(n), vblk)),
        ]
    whole = lambda shape: pl.BlockSpec(shape, lambda n: (0,) * len(shape))
    return specs, whole


def _attn_fwd(z, qg, kg, sinks, *, n_q, n_kv, name, deps=()):
    L = z.shape[0]
    nb = L // WINDOW
    aw = n_q * HEAD_DIM
    group = n_q // n_kv
    consts = _attn_consts(n_q, n_kv, sinks, qg, kg)
    specs, whole = _attn_specs(n_q, n_kv)
    nc = len(_ATTN_CONST_ORDER)

    def body(q_ref, kc_ref, kp_ref, vc_ref, vp_ref, *rest):
        c = dict(zip(_ATTN_CONST_ORDER, rest[:nc]))
        o_ref = rest[-1]
        n = pl.program_id(0)
        col = lax.broadcasted_iota(jnp.int32, (group * WINDOW, 2 * WINDOW), 1)
        first_mask = jnp.logical_and(n == 0, col < WINDOW)
        x = _attn_inputs(q_ref, kc_ref, kp_ref, vc_ref, vp_ref, c)
        for kv in range(n_kv):
            a = _attn_probs(x, c, first_mask, kv, group)
            o = jnp.dot(a["pn"].astype(BF16), a["v2b"], preferred_element_type=F32)
            for g in range(group):
                h = kv * group + g
                o_ref[:, h * HEAD_DIM:(h + 1) * HEAD_DIM] = o[g * WINDOW:(g + 1) * WINDOW].astype(o_ref.dtype)

    return pl.pallas_call(
        body,
        out_shape=jax.ShapeDtypeStruct((L, aw), BF16),
        grid=(nb,),
        in_specs=specs(nb) + [whole(consts[k].shape) for k in _ATTN_CONST_ORDER]
        + [pl.BlockSpec(memory_space=pl.ANY)] * len(deps),
        out_specs=pl.BlockSpec((WINDOW, aw), lambda n: (n, 0)),
        compiler_params=_cparams("parallel"),
        name=name,
    )(z, z, z, z, z, *[consts[k] for k in _ATTN_CONST_ORDER], *deps)


def _attn_bwd(z, do, qg, kg, sinks, *, n_q, n_kv, name):
    L = z.shape[0]
    nb = L // WINDOW
    aw, kvw = n_q * HEAD_DIM, n_kv * HEAD_DIM
    group = n_q // n_kv
    consts = _attn_consts(n_q, n_kv, sinks, qg, kg)
    specs, whole = _attn_specs(n_q, n_kv)
    scale = HEAD_DIM ** -0.5
    nc = len(_ATTN_CONST_ORDER)

    def body(q_ref, kc_ref, kp_ref, vc_ref, vp_ref, do_ref, *rest):
        c = dict(zip(_ATTN_CONST_ORDER, rest[:nc]))
        dq_ref, dkv_ref, dqg_ref, dkg_ref, dsink_ref, carry_ref, dqn_ref, dkn_ref, dv_ref = rest[nc:]
        n = pl.program_id(0)

        @pl.when(n == 0)
        def _():
            dqg_ref[...] = jnp.zeros_like(dqg_ref)
            dkg_ref[...] = jnp.zeros_like(dkg_ref)
            dsink_ref[...] = jnp.zeros_like(dsink_ref)
            carry_ref[...] = jnp.zeros_like(carry_ref)

        @pl.when(n < nb)
        def _():
            col = lax.broadcasted_iota(jnp.int32, (group * WINDOW, 2 * WINDOW), 1)
            first_mask = jnp.logical_and(n == 0, col < WINDOW)
            head_lane = lax.broadcasted_iota(jnp.int32, (1, n_q), 1)
            x = _attn_inputs(q_ref, kc_ref, kp_ref, vc_ref, vp_ref, c)
            dsink = jnp.zeros((1, n_q), F32)
            for kv in range(n_kv):
                a = _attn_probs(x, c, first_mask, kv, group)
                pn = a["pn"]
                dos = jnp.concatenate(
                    [do_ref[:, (kv * group + g) * HEAD_DIM:(kv * group + g + 1) * HEAD_DIM] for g in range(group)],
                    axis=0).astype(BF16)
                dpn = lax.dot_general(dos, a["v2b"], (((1,), (1,)), ((), ())), preferred_element_type=F32)
                ksl = slice(kv * HEAD_DIM, (kv + 1) * HEAD_DIM)
                dv_ref[:, ksl] = lax.dot_general(pn.astype(BF16), dos, (((0,), (0,)), ((), ())), preferred_element_type=F32)
                delta = _mxu_sum(pn * dpn, c["key_ones"][...])
                ds = (pn * (dpn - a["twice"](delta))).astype(BF16)
                dsk = -a["psink"] * delta
                dqn = lax.dot_general(ds, a["k2b"], (((1,), (0,)), ((), ())), preferred_element_type=F32) * scale
                dkn_ref[:, ksl] = lax.dot_general(ds, a["qs"], (((0,), (0,)), ((), ())), preferred_element_type=F32) * scale
                for g in range(group):
                    h = kv * group + g
                    rows = slice(g * WINDOW, (g + 1) * WINDOW)
                    dqn_ref[:, h * HEAD_DIM:(h + 1) * HEAD_DIM] = dqn[rows]
                    dsink = dsink + jnp.where(head_lane == h, jnp.sum(dsk[rows], axis=0, keepdims=True)[:, :n_q], 0.0)
            dq, dqg = _head_rms_bwd(x["q"], x["rq"], c["qg"][...], dqn_ref[...], c["q_ones"][...], c["q_fold"][...])
            dk2, dkg = _head_rms_bwd(x["k2"], x["rk"], c["kg"][...], dkn_ref[...], c["k_ones"][...], c["k_fold"][...])
            dq_ref[...] = dq.astype(dq_ref.dtype)
            dkv_ref[:, :kvw] = (carry_ref[:, :kvw] + dk2[:WINDOW]).astype(dkv_ref.dtype)
            dkv_ref[:, kvw:] = (carry_ref[:, kvw:] + dv_ref[:WINDOW, :]).astype(dkv_ref.dtype)
            carry_ref[:, :kvw] = dk2[WINDOW:]
            carry_ref[:, kvw:] = dv_ref[WINDOW:, :]
            dqg_ref[...] += dqg
            dkg_ref[...] += dkg
            dsink_ref[...] += dsink

        @pl.when(n == nb)
        def _():
            dkv_ref[...] = carry_ref[...].astype(dkv_ref.dtype)

    in_specs = (specs(nb) + [pl.BlockSpec((WINDOW, aw), lambda n: (jnp.minimum(n, nb - 1), 0))]
                + [whole(consts[k].shape) for k in _ATTN_CONST_ORDER])
    return pl.pallas_call(
        body,
        out_shape=(jax.ShapeDtypeStruct((L, aw), BF16), jax.ShapeDtypeStruct((L, 2 * kvw), BF16),
                   jax.ShapeDtypeStruct((1, HEAD_DIM), F32), jax.ShapeDtypeStruct((1, HEAD_DIM), F32),
                   jax.ShapeDtypeStruct((1, n_q), F32)),
        grid=(nb + 1,),
        in_specs=in_specs,
        out_specs=(pl.BlockSpec((WINDOW, aw), lambda n: (jnp.minimum(n, nb - 1), 0)),
                   pl.BlockSpec((WINDOW, 2 * kvw), lambda n: (jnp.maximum(n - 1, 0), 0)),
                   pl.BlockSpec((1, HEAD_DIM), lambda n: (0, 0)),
                   pl.BlockSpec((1, HEAD_DIM), lambda n: (0, 0)),
                   pl.BlockSpec((1, n_q), lambda n: (0, 0))),
        scratch_shapes=[pltpu.VMEM((WINDOW, 2 * kvw), F32), pltpu.VMEM((WINDOW, aw), F32),
                        pltpu.VMEM((2 * WINDOW, kvw), F32), pltpu.VMEM((2 * WINDOW, kvw), F32)],
        compiler_params=_cparams("arbitrary"),
        name=name,
    )(z, z, z, z, z, do, *[consts[k] for k in _ATTN_CONST_ORDER])


def _cmul(ar, ai, br, bi):
    return ar * br - ai * bi, ar * bi + ai * br


def _time_permutation(tc):
    r = jnp.arange(tc)
    src = (r % 8) * (tc // 8) + r // 8
    p = (src[:, None] == jnp.arange(tc)[None, :]).astype(BF16)
    return p, p.T


def _unpermute(pt, x):
    hi = x.astype(BF16)
    lo = (x - hi.astype(F32)).astype(BF16)
    return jnp.dot(pt, hi, preferred_element_type=F32) + jnp.dot(pt, lo, preferred_element_type=F32)


def _segment_scan(xr_ref, xi_ref, ar, ai, cr, ci, ng, reverse):
    n = ar.shape[-1]
    row = lax.broadcasted_iota(jnp.int32, (8, n), 0)
    seeded = 7 if reverse else 0
    a8r = jnp.broadcast_to(ar, (8, n))
    a8i = jnp.broadcast_to(ai, (8, n))
    rows_of = lambda g: pl.ds(pl.multiple_of(((ng - 1 - g) if reverse else g) * 8, 8), 8)

    def recur(g, s):
        rows = rows_of(g)
        sr = a8r * s[0] - a8i * s[1] + xr_ref[rows, :]
        si = a8r * s[1] + a8i * s[0] + xi_ref[rows, :]
        xr_ref[rows, :] = sr
        xi_ref[rows, :] = si
        return sr, si

    fr, fi = lax.fori_loop(0, ng, recur, (jnp.where(row == seeded, cr, 0.0), jnp.where(row == seeded, ci, 0.0)))
    pr, pi = ar, ai
    for _ in range(ng.bit_length() - 1):
        pr, pi = _cmul(pr, pi, pr, pi)
    for k in (1, 2, 4):
        keep = (row < 8 - k) if reverse else (row >= k)
        shift = (8 - k) if reverse else k
        mr, mi = jnp.where(keep, pr, 0.0), jnp.where(keep, pi, 0.0)
        tr, ti = pltpu.roll(fr, shift, 0), pltpu.roll(fi, shift, 0)
        fr, fi = fr + mr * tr - mi * ti, fi + mr * ti + mi * tr
        pr, pi = _cmul(pr, pi, pr, pi)
    shift = 7 if reverse else 1
    before_r, before_i = pltpu.roll(fr, shift, 0), pltpu.roll(fi, shift, 0)

    def inherit(g, d):
        rows = rows_of(g)
        dr = a8r * d[0] - a8i * d[1]
        di = a8r * d[1] + a8i * d[0]
        xr_ref[rows, :] = xr_ref[rows, :] + dr
        xi_ref[rows, :] = xi_ref[rows, :] + di
        return dr, di

    lax.fori_loop(0, ng, inherit, (jnp.where(row == seeded, 0.0, before_r), jnp.where(row == seeded, 0.0, before_i)))
    out = 0 if reverse else 7
    return (fr[out:out + 1], fi[out:out + 1],
            jnp.where(row == seeded, cr, before_r), jnp.where(row == seeded, ci, before_i))


def _blockdiag(x):
    g, a, b = x.shape
    j = g // SSM_LANE_GROUPS
    eye = jnp.eye(SSM_LANE_GROUPS, dtype=x.dtype)
    y = x.reshape(j, SSM_LANE_GROUPS, a, 1, b) * eye[None, :, None, :, None]
    return y.reshape(j, SSM_LANE_GROUPS * a, SSM_LANE_GROUPS * b)


def _blockdiag_extract(y, a, b):
    j = y.shape[0]
    y = y.reshape(j, SSM_LANE_GROUPS, a, SSM_LANE_GROUPS, b)
    return jnp.einsum("jgahb,gh->jgab", y, jnp.eye(SSM_LANE_GROUPS, dtype=y.dtype)).reshape(j * SSM_LANE_GROUPS, a, b)


def _ssm_disc(lr, li, ldt, brt, bit):
    dt = jnp.exp(ldt)
    mag = jnp.exp(lr * dt)
    ar = mag * jnp.cos(li * dt)
    ai = mag * jnp.sin(li * dt)
    den = lr * lr + li * li
    fr = ((ar - 1.0) * lr + ai * li) / den
    fi = (ai * lr - (ar - 1.0) * li) / den
    bbr = fr[:, None, :] * brt - fi[:, None, :] * bit
    bbi = fr[:, None, :] * bit + fi[:, None, :] * brt
    return ar, ai, bbr, bbi


def _ssm_prep(lr, li, ldt, brt, bit, *, name):
    g, h, p = brt.shape

    def body(lr_ref, li_ref, ldt_ref, brt_ref, bit_ref, ar_ref, ai_ref, bbr_ref, bbi_ref):
        ar, ai, bbr, bbi = _ssm_disc(lr_ref[...], li_ref[...], ldt_ref[...], brt_ref[...], bit_ref[...])
        ar_ref[...] = ar
        ai_ref[...] = ai
        bbr_ref[...] = bbr
        bbi_ref[...] = bbi

    gp = jax.ShapeDtypeStruct((g, p), F32)
    ghp = jax.ShapeDtypeStruct((g, h, p), F32)
    return pl.pallas_call(body, out_shape=(gp, gp, ghp, ghp), name=name)(lr, li, ldt, brt, bit)


def _ssm_prep_bwd(lr, li, ldt, brt, bit, dar, dai, dbbr, dbbi, *, name):
    g, h, p = brt.shape

    def body(lr_ref, li_ref, ldt_ref, brt_ref, bit_ref, dar_ref, dai_ref, dbbr_ref, dbbi_ref,
             dlr_ref, dli_ref, dldt_ref, dbrt_ref, dbit_ref):
        _, vjp = jax.vjp(_ssm_disc, lr_ref[...], li_ref[...], ldt_ref[...], brt_ref[...], bit_ref[...])
        dlr, dli, dldt, dbrt, dbit = vjp((dar_ref[...], dai_ref[...], dbbr_ref[...], dbbi_ref[...]))
        dlr_ref[...] = dlr
        dli_ref[...] = dli
        dldt_ref[...] = dldt
        dbrt_ref[...] = dbrt
        dbit_ref[...] = dbit

    gp = jax.ShapeDtypeStruct((g, p), F32)
    ghp = jax.ShapeDtypeStruct((g, h, p), F32)
    return pl.pallas_call(body, out_shape=(gp, gp, jax.ShapeDtypeStruct((g, 1), F32), ghp, ghp), name=name)(
        lr, li, ldt, brt, bit, dar, dai, dbbr, dbbi)


def _ssm_specs(tc, nlanes, nch, u_colblk, chunk_of):
    return [
        pl.BlockSpec((tc, nch), lambda j, c: (chunk_of(c), u_colblk + j)),
        pl.BlockSpec((1, nlanes), lambda j, c: (0, j)),
        pl.BlockSpec((1, nlanes), lambda j, c: (0, j)),
        pl.BlockSpec((None, nch, nlanes), lambda j, c: (j, 0, 0)),
        pl.BlockSpec((None, nch, nlanes), lambda j, c: (j, 0, 0)),
        pl.BlockSpec((None, nlanes, nch), lambda j, c: (j, 0, 0)),
        pl.BlockSpec((None, nlanes, nch), lambda j, c: (j, 0, 0)),
        pl.BlockSpec((1, nch), lambda j, c: (0, j)),
        pl.BlockSpec((tc, tc), lambda j, c: (0, 0)),
        pl.BlockSpec((tc, tc), lambda j, c: (0, 0)),
    ]


def _ssm_fwd(z, ar, ai, bblk_r, bblk_i, cblk_r, cblk_i, d, *, u_col, tc, name, deps=()):
    L = z.shape[0]
    nj, nch, nlanes = bblk_r.shape
    w = nj * nch
    nc = L // tc
    ng = tc // 8

    assert ng & (ng - 1) == 0
    perm, perm_t = _time_permutation(tc)

    def body(u_ref, ar_ref, ai_ref, br_ref, bi_ref, cr_ref, ci_ref, d_ref, p_ref, pt_ref, *rest):
        y_ref, s0r_ref, s0i_ref, xr_ref, xi_ref, carr_ref, cari_ref = rest[len(deps):]
        c = pl.program_id(1)

        @pl.when(c == 0)
        def _():
            carr_ref[...] = jnp.zeros_like(carr_ref)
            cari_ref[...] = jnp.zeros_like(cari_ref)

        s0r_ref[...] = carr_ref[...]
        s0i_ref[...] = cari_ref[...]
        u = u_ref[...]
        ub = jnp.dot(p_ref[...], u.astype(BF16), preferred_element_type=F32).astype(BF16)
        xr_ref[...] = jnp.dot(ub, br_ref[...].astype(BF16), preferred_element_type=F32)
        xi_ref[...] = jnp.dot(ub, bi_ref[...].astype(BF16), preferred_element_type=F32)
        cr, ci, _, _ = _segment_scan(xr_ref, xi_ref, ar_ref[...], ai_ref[...], carr_ref[...], cari_ref[...], ng, False)
        carr_ref[...] = cr
        cari_ref[...] = ci
        y = (jnp.dot(xr_ref[...].astype(BF16), cr_ref[...].astype(BF16), preferred_element_type=F32)
             - jnp.dot(xi_ref[...].astype(BF16), ci_ref[...].astype(BF16), preferred_element_type=F32))
        y_ref[...] = _unpermute(pt_ref[...], y) + d_ref[...] * u

    state = jax.ShapeDtypeStruct((nc, 1, nj * nlanes), F32)
    state_spec = pl.BlockSpec((None, 1, nlanes), lambda j, c: (c, 0, j))
    return pl.pallas_call(
        body,
        out_shape=(jax.ShapeDtypeStruct((L, w), F32), state, state),
        grid=(nj, nc),
        in_specs=_ssm_specs(tc, nlanes, nch, u_col // nch, lambda c: c) + [pl.BlockSpec(memory_space=pl.ANY)] * len(deps),
        out_specs=(pl.BlockSpec((tc, nch), lambda j, c: (c, j)), state_spec, state_spec),
        scratch_shapes=[pltpu.VMEM((tc, nlanes), F32), pltpu.VMEM((tc, nlanes), F32),
                        pltpu.VMEM((1, nlanes), F32), pltpu.VMEM((1, nlanes), F32)],
        compiler_params=_cparams("parallel", "arbitrary"),
        name=name,
    )(z, ar, ai, bblk_r, bblk_i, cblk_r, cblk_i, d, perm, perm_t, *deps)


def _ssm_bwd(z, dy, s0r, s0i, ar, ai, bblk_r, bblk_i, cblk_r, cblk_i, d, *, u_col, tc, name):
    L = z.shape[0]
    nj, nch, nlanes = bblk_r.shape
    w = nj * nch
    nc = L // tc
    ng = tc // 8
    chunk_of = lambda c: nc - 1 - c
    assert ng & (ng - 1) == 0
    perm, perm_t = _time_permutation(tc)

    def body(u_ref, ar_ref, ai_ref, br_ref, bi_ref, cr_ref, ci_ref, d_ref, p_ref, pt_ref, dy_ref, s0r_ref, s0i_ref,
             du_ref, dbr_ref, dbi_ref, dcr_ref, dci_ref, dar_ref, dai_ref, dd_ref,
             sr_ref, si_ref, lr_ref, li_ref, carr_ref, cari_ref):
        c = pl.program_id(1)

        @pl.when(c == 0)
        def _():
            for ref in (dbr_ref, dbi_ref, dcr_ref, dci_ref, dar_ref, dai_ref, dd_ref, carr_ref, cari_ref):
                ref[...] = jnp.zeros_like(ref)

        u = u_ref[...]
        dyv = dy_ref[...]
        both = jnp.dot(p_ref[...], jnp.concatenate([u.astype(BF16), dyv.astype(BF16)], axis=1), preferred_element_type=F32)
        ub = both[:, :nch].astype(BF16)
        dyb = both[:, nch:].astype(BF16)
        brb = br_ref[...].astype(BF16)
        bib = bi_ref[...].astype(BF16)
        crb = cr_ref[...].astype(BF16)
        cib = ci_ref[...].astype(BF16)
        a_r, a_i = ar_ref[...], ai_ref[...]

        sr_ref[...] = jnp.dot(ub, brb, preferred_element_type=F32)
        si_ref[...] = jnp.dot(ub, bib, preferred_element_type=F32)
        _, _, start_r, start_i = _segment_scan(sr_ref, si_ref, a_r, a_i, s0r_ref[...], s0i_ref[...], ng, False)

        nt = (((1,), (1,)), ((), ()))
        lr_ref[...] = lax.dot_general(dyb, crb, nt, preferred_element_type=F32)
        li_ref[...] = -lax.dot_general(dyb, cib, nt, preferred_element_type=F32)
        cr, ci, _, _ = _segment_scan(lr_ref, li_ref, a_r, -a_i, carr_ref[...], cari_ref[...], ng, True)
        carr_ref[...] = cr
        cari_ref[...] = ci

        def accumulate(g, carry):
            pr, pi, acc_r, acc_i = carry
            rows = pl.ds(pl.multiple_of(g * 8, 8), 8)
            lr, li = lr_ref[rows, :], li_ref[rows, :]
            return sr_ref[rows, :], si_ref[rows, :], acc_r + lr * pr + li * pi, acc_i + li * pr - lr * pi

        zero8 = jnp.zeros((8, nlanes), F32)
        _, _, acc_r, acc_i = lax.fori_loop(0, ng, accumulate, (start_r, start_i, zero8, zero8))
        dar_ref[...] += jnp.sum(acc_r, axis=0, keepdims=True)
        dai_ref[...] += jnp.sum(acc_i, axis=0, keepdims=True)

        tn = (((0,), (0,)), ((), ()))
        lrb = lr_ref[...].astype(BF16)
        lib = li_ref[...].astype(BF16)
        dcr_ref[...] += lax.dot_general(sr_ref[...].astype(BF16), dyb, tn, preferred_element_type=F32)
        dci_ref[...] -= lax.dot_general(si_ref[...].astype(BF16), dyb, tn, preferred_element_type=F32)
        dbr_ref[...] += lax.dot_general(ub, lrb, tn, preferred_element_type=F32)
        dbi_ref[...] += lax.dot_general(ub, lib, tn, preferred_element_type=F32)
        du = (lax.dot_general(lrb, brb, nt, preferred_element_type=F32)
              + lax.dot_general(lib, bib, nt, preferred_element_type=F32))
        du_ref[...] = (_unpermute(pt_ref[...], du) + d_ref[...] * dyv).astype(du_ref.dtype)
        dd_ref[...] += jnp.sum(dyv * u, axis=0, keepdims=True)

    state_spec = pl.BlockSpec((None, 1, nlanes), lambda j, c: (chunk_of(c), 0, j))
    bshape = jax.ShapeDtypeStruct((nj, nch, nlanes), F32)
    cshape = jax.ShapeDtypeStruct((nj, nlanes, nch), F32)
    ashape = jax.ShapeDtypeStruct((1, nj * nlanes), F32)
    bspec = pl.BlockSpec((None, nch, nlanes), lambda j, c: (j, 0, 0))
    cspec = pl.BlockSpec((None, nlanes, nch), lambda j, c: (j, 0, 0))
    aspec = pl.BlockSpec((1, nlanes), lambda j, c: (0, j))
    big = pltpu.VMEM((tc, nlanes), F32)
    return pl.pallas_call(
        body,
        out_shape=(jax.ShapeDtypeStruct((L, w), BF16), bshape, bshape, cshape, cshape, ashape, ashape,
                   jax.ShapeDtypeStruct((1, w), F32)),
        grid=(nj, nc),
        in_specs=_ssm_specs(tc, nlanes, nch, u_col // nch, chunk_of)
        + [pl.BlockSpec((tc, nch), lambda j, c: (chunk_of(c), j)), state_spec, state_spec],
        out_specs=(pl.BlockSpec((tc, nch), lambda j, c: (chunk_of(c), j)), bspec, bspec, cspec, cspec, aspec, aspec,
                   pl.BlockSpec((1, nch), lambda j, c: (0, j))),
        scratch_shapes=[big, big, big, big, pltpu.VMEM((1, nlanes), F32), pltpu.VMEM((1, nlanes), F32)],
        compiler_params=_cparams("parallel", "arbitrary"),
        name=name,
    )(z, ar, ai, bblk_r, bblk_i, cblk_r, cblk_i, d, perm, perm_t, dy, s0r, s0i)


def _rmsnorm_rows(x, g):
    return x * lax.rsqrt(jnp.mean(x * x, axis=-1, keepdims=True) + RMS_EPS) * g


def _glu_out(y_raw, pre, b):
    yg = jax.nn.gelu(y_raw)
    return yg * jax.nn.sigmoid(pre + b)


def _gate_merge(za, zs, ba, bs, a, bm):
    return jax.nn.sigmoid(za + ba) * a + jax.nn.sigmoid(zs + bs) * bm


def _swiglu(g, u):
    return jax.nn.silu(g) * u


def _row_tile(width_bytes_per_row, rows):
    budget = VMEM_LIMIT_BYTES // 3
    t = max(8, min(1024, budget // (2 * max(width_bytes_per_row, 1))))
    return _pick(rows, t, 16)


def _ssm_params(p, prefix):
    g, pst = p["lam_re"].shape
    ar, ai, bbr, bbi = _ssm_prep(p["lam_re"], p["lam_im"], p["log_dt"], p["b_re_t"], p["b_im_t"], name=prefix + "_ssm_prep")
    return dict(ar=ar.reshape(1, g * pst), ai=ai.reshape(1, g * pst),
                bblk_r=_blockdiag(bbr), bblk_i=_blockdiag(bbi),
                cblk_r=_blockdiag(jnp.swapaxes(p["c_re"], 1, 2)), cblk_i=_blockdiag(jnp.swapaxes(p["c_im"], 1, 2)))


def _layer_fwd(x, p, dims, prefix, deps=(), tick=None):
    tick = tick or (lambda point, arr: ())
    t, d = x.shape
    aw, kvw, sw, ff = dims["aw"], dims["kvw"], dims["sw"], dims["ff"]
    off_u = aw + 2 * kvw
    off_g = off_u + sw
    gblk = _pick(d, 512, 128)
    assert off_g % gblk == 0 and off_u % (SSM_LANE_GROUPS * SSM_GROUP_CH) == 0
    sv = {"x": x}

    h, = _rowmap(_rmsnorm_rows, [(x, "row", d, 0), (p["norm_mix_g"], "vec", d, 0)], [(d, BF16, "row", d)],
                 rows=t, tm=_row_tile(6 * d, t), name=prefix + "_norm_mix", deps=deps)
    z = _mm_nn(h, p["w_in"], out_dtype=F32, tm=1024, tn=1664, tk=2048, name=prefix + "_mm_in")
    ya = _attn_fwd(z, p["q_norm_g"], p["k_norm_g"], p["attn_sinks"], n_q=dims["n_q"], n_kv=dims["n_kv"],
                   name=prefix + "_attn_fwd", deps=tick("in", z))
    sp = _ssm_params(p, prefix)
    y_raw, s0r, s0i = _ssm_fwd(z, sp["ar"], sp["ai"], sp["bblk_r"], sp["bblk_i"], sp["cblk_r"], sp["cblk_i"], p["ssm_d"],
                               u_col=off_u, tc=dims["tc"], name=prefix + "_ssm_fwd", deps=tick("attn", ya))
    yg, = _rowmap(jax.nn.gelu, [(y_raw, "row", sw, 0)], [(sw, BF16, "row", sw)],
                  rows=t, tm=_row_tile(6 * sw, t), name=prefix + "_gelu", deps=tick("ssm", y_raw))
    pre = _mm_nn(yg, p["ssm_glu_w"], out_dtype=F32, tm=1024, tn=1024, tk=1024, name=prefix + "_mm_glu")
    y2, = _rowmap(_glu_out, [(y_raw, "row", sw, 0), (pre, "row", sw, 0), (p["ssm_glu_b"], "vec", sw, 0)],
                  [(sw, BF16, "row", sw)], rows=t, tm=_row_tile(10 * sw, t), name=prefix + "_glu_out")
    a = _mm_nn(ya, p["w_attn_branch"], out_dtype=BF16, tm=1024, tn=512, tk=1024, name=prefix + "_mm_ab")
    bm = _mm_nn(y2, p["w_ssm_branch"], out_dtype=BF16, tm=1024, tn=512, tk=1024, name=prefix + "_mm_sb")
    ngb = d // gblk
    merged, = _rowmap(
        _gate_merge,
        [(z, "row", gblk, off_g // gblk), (z, "row", gblk, off_g // gblk + ngb),
         (p["gate_bias"], "vec", gblk, 0), (p["gate_bias"], "vec", gblk, ngb),
         (a, "row", gblk, 0), (bm, "row", gblk, 0)],
        [(d, BF16, "row", gblk)], rows=t, tm=_row_tile(18 * gblk, t), ncol=ngb, name=prefix + "_gate")
    x1 = _mm_nn(merged, p["w_out"], out_dtype=F32, tm=512, tn=1024, tk=2048, name=prefix + "_mm_out", add=x)
    h2, = _rowmap(_rmsnorm_rows, [(x1, "row", d, 0), (p["norm_ffn_g"], "vec", d, 0)], [(d, BF16, "row", d)],
                  rows=t, tm=_row_tile(6 * d, t), name=prefix + "_norm_ffn", deps=tick("out", x1))
    gu = _mm_nn(h2, p["w_ffn_in"], out_dtype=BF16, tm=1024, tn=1408, tk=2048, name=prefix + "_mm_ffn_in")
    fblk = _pick(ff, 1408, 128)
    nfb = ff // fblk
    act, = _rowmap(_swiglu, [(gu, "row", fblk, 0), (gu, "row", fblk, nfb)], [(ff, BF16, "row", fblk)],
                   rows=t, tm=_row_tile(10 * fblk, t), ncol=nfb, name=prefix + "_swiglu", deps=tick("ffn_in", gu))
    x2 = _mm_nn(act, p["w_ffn_out"], out_dtype=F32, tm=512, tn=512, tk=5632, name=prefix + "_mm_ffn_out", add=x1)
    sv.update(h=h, z=z, ya=ya, sp=sp, y_raw=y_raw, s0r=s0r, s0i=s0i, yg=yg, pre=pre, y2=y2, a=a, bm=bm,
              merged=merged, x1=x1, h2=h2, gu=gu, act=act)
    return x2, sv


def _layer_bwd(dx2, dx2b, sv, p, dims, prefix, gbuf, deps=(), before_mixer=None, before_in=None):
    t, d = dx2.shape
    aw, kvw, sw, ff = dims["aw"], dims["kvw"], dims["sw"], dims["ff"]
    off_u = aw + 2 * kvw
    off_g = off_u + sw
    gblk = _pick(d, 512, 128)
    ngb = d // gblk
    fblk = _pick(ff, 1408, 128)
    nfb = ff // fblk
    g = {}

    dact = _mm_nt(dx2b, p["w_ffn_out"], out_dtype=BF16, tm=512, tn=2048, tko=1408, name=prefix + "_mm_dact", deps=deps)
    g["w_ffn_out"] = _mm_tn(sv["act"], dx2b, p["w_ffn_out"], into=gbuf.get("w_ffn_out"), tm=4096, tn=1024, tko=512,
                            name=prefix + "_mm_dw_ffn_out")

    def swiglu_bwd(gg, uu, da):
        _, vjp = jax.vjp(_swiglu, gg, uu)
        return vjp(da)

    dgu_g, dgu_u = _rowmap(swiglu_bwd, [(sv["gu"], "row", fblk, 0), (sv["gu"], "row", fblk, nfb), (dact, "row", fblk, 0)],
                           [(ff, BF16, "row", fblk), (ff, BF16, "row", fblk)],
                           rows=t, tm=_row_tile(16 * fblk, t), ncol=nfb, name=prefix + "_swiglu_bwd")
    dgu = jnp.concatenate([dgu_g, dgu_u], axis=1)
    dh2 = _mm_nt(dgu, p["w_ffn_in"], out_dtype=F32, tm=256, tn=1408, tko=512, name=prefix + "_mm_dh2")
    g["w_ffn_in"] = _mm_tn(sv["h2"], dgu, p["w_ffn_in"], into=gbuf.get("w_ffn_in"), tm=4096, tn=1408, tko=512,
                           name=prefix + "_mm_dw_ffn_in")

    def norm_bwd(xx, gg, dh, dres):
        _, vjp = jax.vjp(_rmsnorm_rows, xx, gg)
        dxx, dgg = vjp(dh)
        dxx = dxx + dres
        return dxx, dxx, dgg

    dx1, dx1b, g["norm_ffn_g"] = _rowmap(
        norm_bwd, [(sv["x1"], "row", d, 0), (p["norm_ffn_g"], "vec", d, 0), (dh2, "row", d, 0), (dx2, "row", d, 0)],
        [(d, F32, "row", d), (d, BF16, "row", d), (d, F32, "acc", d)],
        rows=t, tm=_row_tile(22 * d, t), name=prefix + "_norm_ffn_bwd")

    deps = before_mixer(dx1, g) if before_mixer else ()
    dmerged = _mm_nt(dx1b, p["w_out"], out_dtype=BF16, tm=1024, tn=2048, tko=1024, name=prefix + "_mm_dmerged", deps=deps)
    g["w_out"] = _mm_tn(sv["merged"], dx1b, p["w_out"], into=gbuf.get("w_out"), tm=4096, tn=1024, tko=512,
                        name=prefix + "_mm_dw_out")

    def gate_bwd(za, zs, ba, bs, aa, bb, dm):
        _, vjp = jax.vjp(_gate_merge, za, zs, ba, bs, aa, bb)
        dza, dzs, dba, dbs, daa, dbb = vjp(dm)
        return daa, dbb, dza, dzs, dba, dbs

    z = sv["z"]
    da, dbm, dza, dzs, dba, dbs = _rowmap(
        gate_bwd,
        [(z, "row", gblk, off_g // gblk), (z, "row", gblk, off_g // gblk + ngb),
         (p["gate_bias"], "vec", gblk, 0), (p["gate_bias"], "vec", gblk, ngb),
         (sv["a"], "row", gblk, 0), (sv["bm"], "row", gblk, 0), (dmerged, "row", gblk, 0)],
        [(d, BF16, "row", gblk), (d, BF16, "row", gblk), (d, BF16, "row", gblk), (d, BF16, "row", gblk),
         (d, F32, "acc", gblk), (d, F32, "acc", gblk)],
        rows=t, tm=_row_tile(32 * gblk, t), ncol=ngb, name=prefix + "_gate_bwd")
    g["gate_bias"] = jnp.concatenate([dba, dbs], axis=1)
    dya = _mm_nt(da, p["w_attn_branch"], out_dtype=BF16, tm=1024, tn=512, tko=1024, name=prefix + "_mm_dya")
    g["w_attn_branch"] = _mm_tn(sv["ya"], da, p["w_attn_branch"], into=gbuf.get("w_attn_branch"), tm=4096, tn=512,
                                tko=512, name=prefix + "_mm_dw_ab")
    dy2 = _mm_nt(dbm, p["w_ssm_branch"], out_dtype=BF16, tm=1024, tn=512, tko=1024, name=prefix + "_mm_dy2")
    g["w_ssm_branch"] = _mm_tn(sv["y2"], dbm, p["w_ssm_branch"], into=gbuf.get("w_ssm_branch"), tm=4096, tn=512,
                               tko=512, name=prefix + "_mm_dw_sb")

    def glu_bwd(y_raw, pre, b, dy):
        yg = jax.nn.gelu(y_raw)
        _, vjp = jax.vjp(lambda a_, b_, c_: a_ * jax.nn.sigmoid(b_ + c_), yg, pre, b)
        dyg, dpre, db = vjp(dy)
        return dyg, dpre, db

    dyg_direct, dpre, g["ssm_glu_b"] = _rowmap(
        glu_bwd, [(sv["y_raw"], "row", sw, 0), (sv["pre"], "row", sw, 0), (p["ssm_glu_b"], "vec", sw, 0), (dy2, "row", sw, 0)],
        [(sw, F32, "row", sw), (sw, BF16, "row", sw), (sw, F32, "acc", sw)],
        rows=t, tm=_row_tile(24 * sw, t), name=prefix + "_glu_bwd")
    dyg2 = _mm_nt(dpre, p["ssm_glu_w"], out_dtype=F32, tm=1024, tn=1024, tko=1024, name=prefix + "_mm_dyg")
    g["ssm_glu_w"] = _mm_tn(sv["yg"], dpre, p["ssm_glu_w"], into=gbuf.get("ssm_glu_w"), tm=4096, tn=1024, tko=512,
                            name=prefix + "_mm_dw_glu")

    def gelu_bwd(y_raw, d1, d2):
        _, vjp = jax.vjp(jax.nn.gelu, y_raw)
        return vjp(d1 + d2)[0]

    dy_raw, = _rowmap(gelu_bwd, [(sv["y_raw"], "row", sw, 0), (dyg_direct, "row", sw, 0), (dyg2, "row", sw, 0)],
                      [(sw, F32, "row", sw)], rows=t, tm=_row_tile(20 * sw, t), name=prefix + "_gelu_bwd")
    sp = sv["sp"]
    du, dbr, dbi, dcr, dci, dar, dai, g["ssm_d"] = _ssm_bwd(
        z, dy_raw, sv["s0r"], sv["s0i"], sp["ar"], sp["ai"], sp["bblk_r"], sp["bblk_i"], sp["cblk_r"], sp["cblk_i"],
        p["ssm_d"], u_col=off_u, tc=dims["tc"], name=prefix + "_ssm_bwd")
    ngr, pst = p["lam_re"].shape
    hch = SSM_GROUP_CH
    dlr, dli, dldt, dbrt, dbit = _ssm_prep_bwd(
        p["lam_re"], p["lam_im"], p["log_dt"], p["b_re_t"], p["b_im_t"],
        dar.reshape(ngr, pst), dai.reshape(ngr, pst), _blockdiag_extract(dbr, hch, pst), _blockdiag_extract(dbi, hch, pst),
        name=prefix + "_ssm_prep_bwd")
    g.update(ssm_lambda_re=dlr, ssm_lambda_im=dli, ssm_log_dt=dldt.reshape(ngr),
             ssm_b_re=jnp.swapaxes(dbrt, 1, 2), ssm_b_im=jnp.swapaxes(dbit, 1, 2),
             ssm_c_re=jnp.swapaxes(_blockdiag_extract(dcr, pst, hch), 1, 2),
             ssm_c_im=jnp.swapaxes(_blockdiag_extract(dci, pst, hch), 1, 2))

    dq, dkv, g["q_norm_g"], g["k_norm_g"], g["attn_sinks"] = _attn_bwd(
        z, dya, p["q_norm_g"], p["k_norm_g"], p["attn_sinks"], n_q=dims["n_q"], n_kv=dims["n_kv"], name=prefix + "_attn_bwd")

    dz = jnp.concatenate([dq, dkv, du, dza, dzs], axis=1)
    deps = before_in(dq, g) if before_in else ()
    dh = _mm_nt(dz, p["w_in"], out_dtype=F32, tm=512, tn=1664, tko=512, name=prefix + "_mm_dh", deps=deps)
    g["w_in"] = _mm_tn(sv["h"], dz, p["w_in"], into=gbuf.get("w_in"), tm=4096, tn=1664, tko=512,
                       name=prefix + "_mm_dw_in")
    dx, dxb, g["norm_mix_g"] = _rowmap(
        norm_bwd, [(sv["x"], "row", d, 0), (p["norm_mix_g"], "vec", d, 0), (dh, "row", d, 0), (dx1, "row", d, 0)],
        [(d, F32, "row", d), (d, BF16, "row", d), (d, F32, "acc", d)],
        rows=t, tm=_row_tile(22 * d, t), name=prefix + "_norm_mix_bwd")
    return dx, dxb, g


def _loss_and_grad(y, target):
    t, d = y.shape

    def fn(yy, tt):
        e = yy - tt
        dy = e * (1.0 / d)
        return dy, dy, jnp.sum(e * e, keepdims=True).reshape(1, 1)

    dy, dyb, sq = _rowmap(fn, [(y, "row", d, 0), (target, "row", d, 0)],
                          [(d, F32, "row", d), (d, BF16, "row", d), (1, F32, "acc", 1)],
                          rows=t, tm=_row_tile(14 * d, t), name="loss")
    return sq, dy, dyb


def _local_step(x, target, n_layers, weights_of, dims, fwd_tick=None, bwd_tick=None):
    saved, params = [], []
    h = x
    for l in range(n_layers):
        p, deps = weights_of(l, h)
        params.append(p)
        tick = (lambda point, arr, l=l: fwd_tick(l, point, arr)) if fwd_tick else None
        h, sv = _layer_fwd(h, p, dims, "l%d" % l, deps, tick)
        saved.append(sv)
    sq, dy, dyb = _loss_and_grad(h, target)
    grads = [None] * n_layers
    deps = ()
    for l in reversed(range(n_layers)):
        if bwd_tick:
            ffn_done = lambda arr, g, l=l: bwd_tick(l, "ffn", arr, {n: g[n] for n in ("w_ffn_in", "w_ffn_out")})
            mixer_done = lambda arr, g, l=l: bwd_tick(l, "mixer", arr, {n: g[n] for n in MIXER_WEIGHTS})
        else:
            ffn_done = mixer_done = None
        dy, dyb, grads[l] = _layer_bwd(dy, dyb, saved[l], params[l], dims, "l%d" % l, {}, deps=deps,
                                       before_mixer=ffn_done, before_in=mixer_done)
        if bwd_tick:
            deps = bwd_tick(l, "in", dy, {"w_in": grads[l]["w_in"]})
    return sq, dy, grads


COL_SHARDED = ("w_in", "w_attn_branch", "w_ssm_branch", "w_ffn_in")
ROW_SHARDED = ("ssm_glu_w", "w_out", "w_ffn_out")
BIG_WEIGHTS = COL_SHARDED + ROW_SHARDED
WEIGHT_NAMES = ("norm_mix_g", "w_in", "gate_bias", "q_norm_g", "k_norm_g", "attn_sinks", "ssm_lambda_re",
                "ssm_lambda_im", "ssm_log_dt", "ssm_b_re", "ssm_b_im", "ssm_c_re", "ssm_c_im", "ssm_d", "ssm_glu_w",
                "ssm_glu_b", "w_attn_branch", "w_ssm_branch", "w_out", "norm_ffn_g", "w_ffn_in", "w_ffn_out")
SMALL_WEIGHTS = tuple(n for n in WEIGHT_NAMES if n not in BIG_WEIGHTS)
MIXER_WEIGHTS = ("w_out", "w_attn_branch", "w_ssm_branch", "ssm_glu_w")
WEIGHT_GROUPS = {"in": ("w_in",), "mixer": MIXER_WEIGHTS, "ffn": ("w_ffn_in", "w_ffn_out")}


def _dims(d, shapes, tc):
    s, _, aw, _ = shapes["w_attn_branch"]
    sw = shapes["w_ssm_branch"][2]
    in_w = shapes["w_in"][3] * s
    kvw = (in_w - aw - sw - 2 * d) // 2
    ff = shapes["w_ffn_out"][2] * s
    return dict(aw=aw, kvw=kvw, sw=sw, ff=ff, n_q=aw // HEAD_DIM, n_kv=kvw // HEAD_DIM, tc=tc)


def _big_params(big):
    p = {n: _Weight(a, 0, "col") for n, a in big.items() if n in COL_SHARDED}
    p.update({n: _Weight(a.reshape(1, 1, -1, a.shape[-1]), 0, "col") for n, a in big.items() if n in ROW_SHARDED})
    return p


def _layer_params(l, small):
    p = {}
    for n in ("norm_mix_g", "gate_bias", "q_norm_g", "k_norm_g", "ssm_d", "ssm_glu_b", "norm_ffn_g"):
        p[n] = small[n][l][None]
    p["attn_sinks"] = small["attn_sinks"][l]
    p["lam_re"] = small["ssm_lambda_re"][l]
    p["lam_im"] = small["ssm_lambda_im"][l]
    p["log_dt"] = small["ssm_log_dt"][l][:, None]
    p["b_re_t"] = jnp.swapaxes(small["ssm_b_re"][l], 1, 2)
    p["b_im_t"] = jnp.swapaxes(small["ssm_b_im"][l], 1, 2)
    p["c_re"] = small["ssm_c_re"][l]
    p["c_im"] = small["ssm_c_im"][l]
    return p


_ANY = pl.BlockSpec(memory_space=pl.ANY)
_MESH_ID = pl.DeviceIdType.MESH


def _coords():
    return lax.axis_index("x"), lax.axis_index("y"), lax.axis_index("c")


def _remote(src, dst, send_sems, recv_sems, k, to):
    return pltpu.make_async_remote_copy(src_ref=src, dst_ref=dst, send_sem=send_sems.at[k], recv_sem=recv_sems.at[k],
                                        device_id=to, device_id_type=_MESH_ID)


def _comm_call(body, ins, out_shapes, n_remote, name, aliases=None, scratch=()):
    return pl.pallas_call(
        body,
        out_shape=tuple(out_shapes),
        in_specs=[_ANY] * len(ins),
        out_specs=tuple([_ANY] * len(out_shapes)),
        scratch_shapes=[pltpu.SemaphoreType.DMA((n_remote,)), pltpu.SemaphoreType.DMA((n_remote,))] + list(scratch),
        input_output_aliases=aliases or {},
        compiler_params=pltpu.CompilerParams(has_side_effects=True),
        name=name,
    )(*ins)


def _my_chip():
    return (2 * lax.axis_index("x") + lax.axis_index("y")).astype(jnp.int32).reshape(1)


def _my_core():
    return lax.axis_index("c").astype(jnp.int32).reshape(1)


def _cast_into_slot(w, layer, *, name):
    _, r, c = w.shape
    tm = _row_tile(12 * c, r)

    def body(me_ref, w_ref, o_ref):
        o_ref[...] = w_ref[...].astype(o_ref.dtype)

    return pl.pallas_call(
        body,
        out_shape=jax.ShapeDtypeStruct((N_CHIPS, 1, r, c), BF16),
        grid_spec=pltpu.PrefetchScalarGridSpec(
            num_scalar_prefetch=1,
            grid=(r // tm,),
            in_specs=[pl.BlockSpec((None, tm, c), lambda i, me: (layer, i, 0))],
            out_specs=pl.BlockSpec((None, None, tm, c), lambda i, me: (me[0], 0, i, 0)),
        ),
        compiler_params=_cparams("parallel"),
        name=name,
    )(_my_chip(), w)


def _allgather_weights(bufs, *, name):
    n = len(bufs)

    def body(*refs):
        outs = refs[n:2 * n]
        send_sems, recv_sems = refs[2 * n:]
        x, y, c = _coords()
        me = 2 * x + y
        chips = [(1 - x, y), (x, 1 - y), (1 - x, 1 - y)]
        sibling = (x, y, 1 - c)

        def half(i, slot, hc):
            rh = bufs[i].shape[2] // 2
            return outs[i].at[slot, :, pl.ds(hc * rh, rh), :]

        first = [_remote(half(i, me, c), half(i, me, c), send_sems, recv_sems, 6 * i + k, (px, py, c))
                 for i in range(n) for k, (px, py) in enumerate(chips)]
        for cp in first:
            cp.start()
        passed = []
        for k, (px, py) in enumerate(chips):
            for i in range(n):
                landed = half(i, 2 * px + py, c)
                _remote(landed, landed, send_sems, recv_sems, 6 * i + k, (px, py, c)).wait_recv()
                fw = _remote(landed, landed, send_sems, recv_sems, 6 * i + 3 + k, sibling)
                fw.start()
                passed.append(fw)
        for k, (px, py) in enumerate(chips):
            for i in range(n):
                other = half(i, 2 * px + py, 1 - c)
                _remote(other, other, send_sems, recv_sems, 6 * i + 3 + k, sibling).wait_recv()
        for cp in first + passed:
            cp.wait_send()

    outs = [jax.ShapeDtypeStruct(b.shape, b.dtype) for b in bufs]
    return _comm_call(body, bufs, outs, 6 * n, name, aliases={i: i for i in range(n)})


_HBM = pl.BlockSpec(memory_space=pltpu.HBM)
_SEM = pl.BlockSpec(memory_space=pltpu.SEMAPHORE)
_DATAFLOW = pltpu.SideEffectType.DATAFLOW_SIDE_EFFECTING


class _SplitExchange:
    def __init__(self, srcs, lands, build, n_copies, name):
        self.build, self.n, self.name = build, n_copies, name
        self.ns, self.nl = len(srcs), len(lands)
        self.bufs = [pltpu.with_memory_space_constraint(a, pltpu.HBM) for a in list(srcs) + list(lands)]

    def _copies(self, refs, send_sems, recv_sems):
        triples = self.build(refs[:self.ns], refs[self.ns:self.ns + self.nl])
        assert len(triples) == self.n
        return [pltpu.make_async_remote_copy(src_ref=s, dst_ref=d, send_sem=send_sems.at[k], recv_sem=recv_sems.at[k],
                                             device_id=to, device_id_type=_MESH_ID) for k, (s, d, to) in enumerate(triples)]

    def start(self, deps=()):
        nb = self.ns + self.nl

        def body(*refs):
            outs = refs[nb + len(deps):]
            for cp in self._copies(refs, outs[0], outs[1]):
                cp.start()
            outs[-1][...] = jnp.zeros_like(outs[-1])

        sems = pltpu.SemaphoreType.DMA((self.n,))
        res = pl.pallas_call(
            body,
            out_shape=(sems, sems, *[pltpu.HBM(b.shape, b.dtype) for b in self.bufs], jax.ShapeDtypeStruct((8, 128), F32)),
            in_specs=[_HBM] * nb + [_ANY] * len(deps),
            out_specs=(_SEM, _SEM, *[_HBM] * nb, pl.BlockSpec(memory_space=pltpu.VMEM)),
            input_output_aliases={i: 2 + i for i in range(nb)},
            compiler_params=pltpu.CompilerParams(has_side_effects=_DATAFLOW),
            name=self.name + "_start",
        )(*self.bufs, *deps)
        self.send_sems, self.recv_sems = res[0], res[1]
        self.bufs = list(res[2:2 + nb])
        return res[-1]

    def wait(self, after=()):
        nb = self.ns + self.nl

        def body(*refs):
            for cp in self._copies(refs, refs[nb], refs[nb + 1]):
                cp.wait_send()
                cp.wait_recv()

        res = pl.pallas_call(
            body,
            out_shape=tuple(pltpu.HBM(b.shape, b.dtype) for b in self.bufs),
            in_specs=[_HBM] * nb + [_SEM, _SEM] + [_ANY] * len(after),
            out_specs=tuple([_HBM] * nb),
            input_output_aliases={i: i for i in range(nb)},
            compiler_params=pltpu.CompilerParams(has_side_effects=_DATAFLOW),
            name=self.name + "_wait",
        )(*self.bufs, self.send_sems, self.recv_sems, *after)
        res = list(res)
        return res[:self.ns], res[self.ns:]


def _other_chips(x, y):
    return [(1 - x, y), (x, 1 - y), (1 - x, 1 - y)]


def _gather_steps(bufs, tag, deps, publish):
    n = len(bufs)
    half = lambda ref, i, slot, hc: ref.at[slot, :, pl.ds(hc * (bufs[i].shape[2] // 2), bufs[i].shape[2] // 2), :]

    def over_ici(srcs, lands):
        x, y, c = _coords()
        me = 2 * x + y
        return [(half(srcs[i], i, me, c), half(srcs[i], i, me, c), (px, py, c))
                for i in range(n) for px, py in _other_chips(x, y)]

    def to_sibling(srcs, lands):
        x, y, c = _coords()
        return [(half(srcs[i], i, 2 * px + py, c), half(srcs[i], i, 2 * px + py, c), (x, y, 1 - c))
                for i in range(n) for px, py in _other_chips(x, y)]

    ex = _SplitExchange(bufs, [], over_ici, 3 * n, tag + "_ici")
    after = yield ex.start(deps)
    bufs, _ = ex.wait((after,))
    ex = _SplitExchange(bufs, [], to_sibling, 3 * n, tag + "_d2d")
    after = yield ex.start()
    bufs, _ = ex.wait((after,))
    publish(bufs)


def _reduce_steps(grads, tag, publish):
    n = len(grads)
    rh = [g.shape[1] // 2 for g in grads]
    theirs = [lax.empty((g.shape[0], g.shape[1] // 2, g.shape[2]), F32) for g in grads]

    def halves(srcs, lands):
        x, y, c = _coords()
        return [(srcs[i].at[:, pl.ds((1 - c) * rh[i], rh[i]), :], lands[i], (x, y, 1 - c)) for i in range(n)]

    def chips(srcs, lands):
        x, y, c = _coords()
        me = 2 * x + y
        return [(srcs[i].at[2 * px + py], lands[i].at[me], (px, py, c)) for i in range(n) for px, py in _other_chips(x, y)]

    def sibling(srcs, lands):
        x, y, c = _coords()
        return [(srcs[i], lands[i], (x, y, 1 - c)) for i in range(n)]

    ex = _SplitExchange(grads, theirs, halves, n, tag + "_halves")
    after = yield ex.start()
    grads, theirs = ex.wait((after,))
    parts = [_add_own_half(g, t, name="%s_add_own_half_%d" % (tag, i)) for i, (g, t) in enumerate(zip(grads, theirs))]
    ex = _SplitExchange(parts, [lax.empty(p.shape, p.dtype) for p in parts], chips, 3 * n, tag + "_chips")
    after = yield ex.start()
    parts, got = ex.wait((after,))
    mine = [_sum_chips(p, g, name="%s_sum_chips_%d" % (tag, i)) for i, (p, g) in enumerate(zip(parts, got))]
    ex = _SplitExchange(mine, [lax.empty(m.shape, m.dtype) for m in mine], sibling, n, tag + "_sibling")
    after = yield ex.start()
    mine, theirs = ex.wait((after,))
    publish(list(zip(mine, theirs)))


def _allreduce_steps(buf, tag, publish):
    r, c = buf.shape

    def to_sibling(srcs, lands):
        x, y, cc = _coords()
        return [(srcs[0], lands[0], (x, y, 1 - cc))]

    def over_ici(srcs, lands):
        x, y, cc = _coords()
        me = 2 * x + y
        return [(srcs[0].at[me], srcs[0].at[me], (px, py, cc)) for px, py in _other_chips(x, y)]

    def halves(srcs, lands):
        x, y, cc = _coords()
        return [(srcs[0].at[cc], srcs[0].at[cc], (x, y, 1 - cc))]

    ex = _SplitExchange([buf], [lax.empty(buf.shape, buf.dtype)], to_sibling, 1, tag + "_cores")
    after = yield ex.start()
    (mine,), (theirs,) = ex.wait((after,))
    ex = _SplitExchange([_add_half_into_slot(mine, theirs, name=tag + "_chip_sum")], [], over_ici, N_CHIPS - 1, tag + "_chips")
    after = yield ex.start()
    (parts,), _ = ex.wait((after,))
    total_half = _sum_slots(parts, name=tag + "_sum_chips")
    ex = _SplitExchange([_place_into_slot(total_half, 2, _my_core(), name=tag + "_place_half")], [], halves, 1, tag + "_halves")
    after = yield ex.start()
    (both,), _ = ex.wait((after,))
    publish(both.reshape(r, c))


class _Exchanges:
    def __init__(self):
        self.running = []

    def launch(self, steps):
        self.running.append(steps)
        return next(steps)

    def advance(self, steps, after):
        try:
            return steps.send(after)
        except StopIteration:
            self.running.remove(steps)
            return None

    def advance_all(self, after):
        tokens = [self.advance(steps, after) for steps in list(self.running)]
        return tuple(t for t in tokens if t is not None)


def _add_own_half(g, theirs, *, name):
    s, r, c = g.shape
    rh = r // 2
    tm = _row_tile(10 * c, rh)
    nb = rh // tm

    def body(core_ref, g_ref, t_ref, o_ref):
        o_ref[...] = (g_ref[...] + t_ref[...]).astype(o_ref.dtype)

    return pl.pallas_call(
        body,
        out_shape=jax.ShapeDtypeStruct((s, rh, c), BF16),
        grid_spec=pltpu.PrefetchScalarGridSpec(
            num_scalar_prefetch=1,
            grid=(s, nb),
            in_specs=[pl.BlockSpec((None, tm, c), lambda k, i, core: (k, core[0] * nb + i, 0)),
                      pl.BlockSpec((None, tm, c), lambda k, i, core: (k, i, 0))],
            out_specs=pl.BlockSpec((None, tm, c), lambda k, i, core: (k, i, 0)),
        ),
        compiler_params=_cparams("parallel", "parallel"),
        name=name,
    )(_my_core(), g, theirs)


def _sum_chips(part, got, *, name):
    s, rh, c = part.shape
    tm = _row_tile(14 * c, rh)

    def body(me_ref, p_ref, a_ref, b_ref, c_ref, o_ref):
        o_ref[...] = ((p_ref[...].astype(F32) + a_ref[...].astype(F32)) + b_ref[...].astype(F32)) + c_ref[...].astype(F32)

    slot = lambda k: (lambda i, me: ((me[0] + k) % s, i, 0))
    return pl.pallas_call(
        body,
        out_shape=jax.ShapeDtypeStruct((rh, c), F32),
        grid_spec=pltpu.PrefetchScalarGridSpec(
            num_scalar_prefetch=1,
            grid=(rh // tm,),
            in_specs=[pl.BlockSpec((None, tm, c), slot(k)) for k in range(s)],
            out_specs=pl.BlockSpec((tm, c), lambda i, me: (i, 0)),
        ),
        compiler_params=_cparams("parallel"),
        name=name,
    )(_my_chip(), part, got, got, got)


def _place_into_slot(buf, n_slots, slot, *, name):
    r, c = buf.shape
    tm = _row_tile(8 * c, r)

    def body(slot_ref, i_ref, o_ref):
        o_ref[...] = i_ref[...]

    return pl.pallas_call(
        body,
        out_shape=jax.ShapeDtypeStruct((n_slots, r, c), buf.dtype),
        grid_spec=pltpu.PrefetchScalarGridSpec(
            num_scalar_prefetch=1,
            grid=(r // tm,),
            in_specs=[pl.BlockSpec((tm, c), lambda i, s: (i, 0))],
            out_specs=pl.BlockSpec((None, tm, c), lambda i, s: (s[0], i, 0)),
        ),
        compiler_params=_cparams("parallel"),
        name=name,
    )(slot, buf)


def _add_half_into_slot(mine, theirs, *, name):
    r, c = mine.shape
    rh = r // 2
    tm = _row_tile(12 * c, rh)
    nb = rh // tm
    where = jnp.concatenate([_my_chip(), _my_core()])

    def body(where_ref, a_ref, b_ref, o_ref):
        o_ref[...] = a_ref[...] + b_ref[...]

    half = pl.BlockSpec((tm, c), lambda i, w: (w[1] * nb + i, 0))
    return pl.pallas_call(
        body,
        out_shape=jax.ShapeDtypeStruct((N_CHIPS, rh, c), mine.dtype),
        grid_spec=pltpu.PrefetchScalarGridSpec(
            num_scalar_prefetch=1,
            grid=(nb,),
            in_specs=[half, half],
            out_specs=pl.BlockSpec((None, tm, c), lambda i, w: (w[0], i, 0)),
        ),
        compiler_params=_cparams("parallel"),
        name=name,
    )(where, mine, theirs)


def _sum_slots(arr, *, name):
    s, r, c = arr.shape
    tm = _row_tile(4 * c * (s + 1), r)

    def body(*refs):
        acc = refs[0][...]
        for ref in refs[1:s]:
            acc = acc + ref[...]
        refs[s][...] = acc

    return pl.pallas_call(
        body,
        out_shape=jax.ShapeDtypeStruct((r, c), arr.dtype),
        grid=(r // tm,),
        in_specs=[pl.BlockSpec((None, tm, c), lambda i, k=k: (k, i, 0)) for k in range(s)],
        out_specs=pl.BlockSpec((tm, c), lambda i: (i, 0)),
        compiler_params=_cparams("parallel"),
        name=name,
    )(*([arr] * s))


def _adamw_fn(w, g, m, v):
    m = ADAM_B1 * m + (1.0 - ADAM_B1) * g
    v = ADAM_B2 * v + (1.0 - ADAM_B2) * jnp.square(g)
    m_hat = m / (1.0 - ADAM_B1 ** ADAM_STEP)
    v_hat = v / (1.0 - ADAM_B2 ** ADAM_STEP)
    delta = -ADAM_LR * (m_hat / (jnp.sqrt(v_hat) + ADAM_EPS) + ADAM_WD * w)
    return delta, m, v


def _adamw(w, g, m, v, *, name):
    rows, cols = w.shape
    ins = [(a, "row", cols, 0) for a in (w, g, m, v)]
    outs = [(cols, F32, "row", cols)] * 3
    return _rowmap(_adamw_fn, ins, outs, rows=rows, tm=_row_tile(56 * cols, rows), name=name)


def _adamw_sharded(w, m, v, g_mine, g_sibling, layer, into, *, name, deps=()):
    nl, r, c = w.shape
    rh = r // 2
    tm = _row_tile(40 * c, rh)
    nb = rh // tm
    n_into = 0 if into is None else 4

    def body(core_ref, w_ref, m_ref, v_ref, a_ref, b_ref, *rest):
        g_ref, d_ref, nm_ref, nv_ref = rest[n_into + len(deps):]
        g = jnp.where(pl.program_id(0) == core_ref[0], a_ref[...], b_ref[...])
        delta, nm, nv = _adamw_fn(w_ref[...], g, m_ref[...], v_ref[...])
        g_ref[...] = g
        d_ref[...] = delta
        nm_ref[...] = nm
        nv_ref[...] = nv

    whole = pl.BlockSpec((None, tm, c), lambda h, i, core: (layer, h * nb + i, 0))
    half = pl.BlockSpec((tm, c), lambda h, i, core: (i, 0))
    shape = jax.ShapeDtypeStruct((nl, r, c), F32)
    return pl.pallas_call(
        body,
        out_shape=(shape, shape, shape, shape),
        grid_spec=pltpu.PrefetchScalarGridSpec(
            num_scalar_prefetch=1,
            grid=(2, nb),
            in_specs=[whole, whole, whole, half, half] + [pl.BlockSpec(memory_space=pl.ANY)] * (n_into + len(deps)),
            out_specs=(whole, whole, whole, whole),
        ),
        input_output_aliases={6 + k: k for k in range(n_into)},
        compiler_params=_cparams("parallel", "parallel"),
        name=name,
    )(_my_core(), w, m, v, g_mine, g_sibling, *(into or ()), *deps)


def _pack(arrays):
    flat = jnp.concatenate([a.reshape(-1) for a in arrays])
    pad = (-flat.shape[0]) % (256 * 128)
    return jnp.pad(flat, (0, pad)).reshape(-1, 128)


def _unpack(buf, shapes):
    flat = buf.reshape(-1)
    out, off = [], 0
    for s in shapes:
        n = math.prod(s)
        out.append(flat[off:off + n].reshape(s))
        off += n
    return out


def kernel(x, norm_mix_g, w_in, gate_bias, q_norm_g, k_norm_g, attn_sinks, ssm_lambda_re, ssm_lambda_im, ssm_log_dt, ssm_b_re, ssm_b_im, ssm_c_re, ssm_c_im, ssm_d, ssm_glu_w, ssm_glu_b, w_attn_branch, w_ssm_branch, w_out, norm_ffn_g, w_ffn_in, w_ffn_out, loss_target, m_norm_mix_g, m_w_in, m_gate_bias, m_q_norm_g, m_k_norm_g, m_attn_sinks, m_ssm_lambda_re, m_ssm_lambda_im, m_ssm_log_dt, m_ssm_b_re, m_ssm_b_im, m_ssm_c_re, m_ssm_c_im, m_ssm_d, m_ssm_glu_w, m_ssm_glu_b, m_w_attn_branch, m_w_ssm_branch, m_w_out, m_norm_ffn_g, m_w_ffn_in, m_w_ffn_out, v_norm_mix_g, v_w_in, v_gate_bias, v_q_norm_g, v_k_norm_g, v_attn_sinks, v_ssm_lambda_re, v_ssm_lambda_im, v_ssm_log_dt, v_ssm_b_re, v_ssm_b_im, v_ssm_c_re, v_ssm_c_im, v_ssm_d, v_ssm_glu_w, v_ssm_glu_b, v_w_attn_branch, v_w_ssm_branch, v_w_out, v_norm_ffn_g, v_w_ffn_in, v_w_ffn_out):
    w = dict(norm_mix_g=norm_mix_g, w_in=w_in, gate_bias=gate_bias, q_norm_g=q_norm_g, k_norm_g=k_norm_g,
             attn_sinks=attn_sinks, ssm_lambda_re=ssm_lambda_re, ssm_lambda_im=ssm_lambda_im, ssm_log_dt=ssm_log_dt,
             ssm_b_re=ssm_b_re, ssm_b_im=ssm_b_im, ssm_c_re=ssm_c_re, ssm_c_im=ssm_c_im, ssm_d=ssm_d,
             ssm_glu_w=ssm_glu_w, ssm_glu_b=ssm_glu_b, w_attn_branch=w_attn_branch, w_ssm_branch=w_ssm_branch,
             w_out=w_out, norm_ffn_g=norm_ffn_g, w_ffn_in=w_ffn_in, w_ffn_out=w_ffn_out)
    m = dict(norm_mix_g=m_norm_mix_g, w_in=m_w_in, gate_bias=m_gate_bias, q_norm_g=m_q_norm_g, k_norm_g=m_k_norm_g,
             attn_sinks=m_attn_sinks, ssm_lambda_re=m_ssm_lambda_re, ssm_lambda_im=m_ssm_lambda_im,
             ssm_log_dt=m_ssm_log_dt, ssm_b_re=m_ssm_b_re, ssm_b_im=m_ssm_b_im, ssm_c_re=m_ssm_c_re,
             ssm_c_im=m_ssm_c_im, ssm_d=m_ssm_d, ssm_glu_w=m_ssm_glu_w, ssm_glu_b=m_ssm_glu_b,
             w_attn_branch=m_w_attn_branch, w_ssm_branch=m_w_ssm_branch, w_out=m_w_out, norm_ffn_g=m_norm_ffn_g,
             w_ffn_in=m_w_ffn_in, w_ffn_out=m_w_ffn_out)
    v = dict(norm_mix_g=v_norm_mix_g, w_in=v_w_in, gate_bias=v_gate_bias, q_norm_g=v_q_norm_g, k_norm_g=v_k_norm_g,
             attn_sinks=v_attn_sinks, ssm_lambda_re=v_ssm_lambda_re, ssm_lambda_im=v_ssm_lambda_im,
             ssm_log_dt=v_ssm_log_dt, ssm_b_re=v_ssm_b_re, ssm_b_im=v_ssm_b_im, ssm_c_re=v_ssm_c_re,
             ssm_c_im=v_ssm_c_im, ssm_d=v_ssm_d, ssm_glu_w=v_ssm_glu_w, ssm_glu_b=v_ssm_glu_b,
             w_attn_branch=v_w_attn_branch, w_ssm_branch=v_w_ssm_branch, w_out=v_w_out, norm_ffn_g=v_norm_ffn_g,
             w_ffn_in=v_w_ffn_in, w_ffn_out=v_w_ffn_out)
    n_layers = norm_mix_g.shape[0]
    d_model = x.shape[-1]
    seq = x.shape[1]

    exchanges = _Exchanges()
    params = [_layer_params(l, w) for l in range(n_layers)]
    gathers = {}

    def gather(l, group, deps=()):
        names = WEIGHT_GROUPS[group]
        bufs = [_cast_into_slot(w[n], l, name="cast%d_%s" % (l, n)) for n in names]
        steps = _gather_steps(bufs, "ag%d_%s" % (l, group), deps,
                              lambda got: params[l].update(_big_params(dict(zip(names, got)))))
        gathers[l, group] = steps
        return exchanges.launch(steps)

    w_in0 = _allgather_weights([_cast_into_slot(w["w_in"], 0, name="cast0_w_in")], name="ag0_in")[0]
    params[0].update(_big_params({"w_in": w_in0}))
    first_deps = (gather(0, "mixer", (w_in0,)), gather(0, "ffn", (w_in0,)))
    sizes = {n: (N_CHIPS, 1) + w[n].shape[1:] for n in BIG_WEIGHTS}
    dims = _dims(d_model, sizes, min(512, seq))

    def weights_of(l, h):
        if l == 0:
            return params[0], first_deps
        for group in WEIGHT_GROUPS:
            exchanges.advance(gathers[l, group], h)
        return params[l], ()

    def fwd_tick(l, point, arr):
        tokens = []
        if l == 0 and point in ("in", "attn"):
            tokens.append(exchanges.advance(gathers[0, "mixer"], arr))
        if l == 0 and point in ("ssm", "out"):
            tokens.append(exchanges.advance(gathers[0, "ffn"], arr))
        if l + 1 < n_layers and point == "attn":
            tokens += [gather(l + 1, group) for group in WEIGHT_GROUPS]
        if l + 1 < n_layers and point == "ffn_in":
            tokens += [exchanges.advance(gathers[l + 1, group], arr) for group in WEIGHT_GROUPS]
        return tuple(t for t in tokens if t is not None)

    reduced, ready = {}, []

    def bwd_tick(l, stage, arr, stage_grads):
        tokens = exchanges.advance_all(arr)
        names = tuple(stage_grads)

        def publish(halves):
            reduced.update({(l, n): h for n, h in zip(names, halves)})
            ready.append((l, names))

        grads4 = [stage_grads[n].reshape(N_CHIPS, -1, stage_grads[n].shape[-1]) for n in names]
        return tokens + (exchanges.launch(_reduce_steps(grads4, "rs%d_%s" % (l, stage), publish)),)

    sq, dx, grads = _local_step(x[0], loss_target[0], n_layers, weights_of, dims, fwd_tick, bwd_tick)
    loss = lax.psum(sq[0, 0], MESH_AXES) * (0.5 / d_model)

    small_shapes = [w[n].shape for n in SMALL_WEIGHTS]
    small_local = [jnp.stack([grads[l][n].reshape(w[n].shape[1:]) for l in range(n_layers)]) for n in SMALL_WEIGHTS]
    shared = []
    tokens = (exchanges.launch(_allreduce_steps(_pack(small_local), "small_grads", shared.append)),)

    adam = {n: None for n in BIG_WEIGHTS}
    grad, delta, new_m, new_v = {}, {}, {}, {}
    after = dx
    while exchanges.running or ready or shared:
        for l, names in ready:
            for n in names:
                mine, sibling = reduced[l, n]
                adam[n] = _adamw_sharded(w[n], m[n], v[n], mine, sibling, l, adam[n], name="adamw%d_%s" % (l, n), deps=tokens)
                after = adam[n][1]
        del ready[:]
        if shared:
            grad.update(zip(SMALL_WEIGHTS, _unpack(shared.pop(), small_shapes)))
            for n in SMALL_WEIGHTS:
                flat = lambda a: a.reshape(-1, a.shape[-1])
                res = _adamw(flat(w[n]), flat(grad[n]), flat(m[n]), flat(v[n]), name="adamw_" + n)
                delta[n], new_m[n], new_v[n] = [r.reshape(w[n].shape) for r in res]
            after = delta[SMALL_WEIGHTS[-1]]
        tokens = exchanges.advance_all(after)
    for n in BIG_WEIGHTS:
        grad[n], delta[n], new_m[n], new_v[n] = adam[n]

    return (loss, dx[None], *[grad[n] for n in WEIGHT_NAMES], *[delta[n] for n in WEIGHT_NAMES],
            *[new_m[n] for n in WEIGHT_NAMES], *[new_v[n] for n in WEIGHT_NAMES])
```

```python
import functools
import math

import jax
import jax.numpy as jnp
from jax import lax
from jax.experimental import pallas as pl
from jax.experimental.pallas import tpu as pltpu

HEAD_DIM = 64
WINDOW = 128
SSM_GROUP_CH = 16
SSM_LANE_GROUPS = 8
RMS_EPS = 1e-6
ADAM_LR = 0.001
ADAM_B1 = 0.9
ADAM_B2 = 0.999
ADAM_EPS = 1e-08
ADAM_WD = 0.01
ADAM_STEP = 10
NEG_BIG = -1e30
MESH_AXES = ("x", "y", "c")
N_CHIPS = 4
N_DEV = 8
VMEM_LIMIT_BYTES = 56 * 1024 * 1024
BF16 = jnp.bfloat16
F32 = jnp.float32


def _cparams(*semantics):
    return pltpu.CompilerParams(dimension_semantics=semantics, vmem_limit_bytes=VMEM_LIMIT_BYTES)


def _pick(n, target, mult):
    if n <= target:
        return n
    best = None
    for d in range(mult, target + 1, mult):
        if n % d == 0:
            best = d
    assert best is not None, (n, target, mult)
    return best


def _rowmap(fn, ins, outs, *, rows, tm, ncol=1, name, deps=()):
    n_in = len(ins)
    nrow = rows // tm
    assert nrow * tm == rows

    in_specs = []
    for arr, kind, width, coloff in ins:
        if kind == "row":
            in_specs.append(pl.BlockSpec((tm, width), lambda j, i, o=coloff: (i, o + j)))
        elif kind == "vec":
            in_specs.append(pl.BlockSpec((1, width), lambda j, i, o=coloff: (0, o + j)))
        else:
            nd = arr.ndim
            in_specs.append(pl.BlockSpec(arr.shape, lambda j, i, nd=nd: (0,) * nd))
    out_specs, out_shapes = [], []
    for cols, dtype, kind, width in outs:
        if kind == "row":
            out_specs.append(pl.BlockSpec((tm, width), lambda j, i: (i, j)))
            out_shapes.append(jax.ShapeDtypeStruct((rows, cols), dtype))
        else:
            out_specs.append(pl.BlockSpec((1, width), lambda j, i: (0, j)))
            out_shapes.append(jax.ShapeDtypeStruct((1, cols), dtype))

    in_specs += [pl.BlockSpec(memory_space=pl.ANY)] * len(deps)

    def body(*refs):
        i = pl.program_id(1)
        res = fn(*[r[...].astype(F32) for r in refs[:n_in]])
        if not isinstance(res, (tuple, list)):
            res = (res,)
        for (cols, dtype, kind, width), ref, val in zip(outs, refs[n_in + len(deps):], res):
            if kind == "row":
                ref[...] = val.astype(ref.dtype)
            else:
                @pl.when(i == 0)
                def _():
                    ref[...] = jnp.zeros_like(ref)
                ref[...] += val.astype(ref.dtype)

    res = pl.pallas_call(
        body,
        out_shape=tuple(out_shapes),
        grid=(ncol, nrow),
        in_specs=in_specs,
        out_specs=tuple(out_specs),
        compiler_params=_cparams("parallel", "arbitrary"),
        name=name,
    )(*[a[0] for a in ins], *deps)
    return res


def _mm_body(dims, nk, has_add, unused_in=0):
    def body(*refs):
        if has_add:
            a_ref, b_ref, add_ref = refs[:3]
            o_ref = refs[3 + unused_in]
            rest = refs[4 + unused_in:]
        else:
            a_ref, b_ref = refs[:2]
            o_ref = refs[2 + unused_in]
            add_ref = None
            rest = refs[3 + unused_in:]
        part = lax.dot_general(a_ref[...], b_ref[...], (dims, ((), ())), preferred_element_type=F32)
        if nk == 1:
            if add_ref is not None:
                part = part + add_ref[...]
            o_ref[...] = part.astype(o_ref.dtype)
        else:
            acc_ref = rest[0]
            k = pl.program_id(2)

            @pl.when(k == 0)
            def _():
                acc_ref[...] = part

            @pl.when(k > 0)
            def _():
                acc_ref[...] += part

            @pl.when(k == nk - 1)
            def _():
                r = acc_ref[...]
                if add_ref is not None:
                    r = r + add_ref[...]
                o_ref[...] = r.astype(o_ref.dtype)
    return body


class _Weight:
    def __init__(self, arr, layer, kind):
        self.arr, self.layer, self.kind = arr, layer, kind
        self.s, _, self.r, self.c = arr.shape
        self.rows = self.r * (self.s if kind == "row" else 1)
        self.cols = self.c * (self.s if kind == "col" else 1)

    def tiles(self, tr, tc):
        return _pick(self.r, tr, 128), _pick(self.c, tc, 128)

    def index(self, tr, tc):
        layer = self.layer
        if self.kind == "col":
            per = self.c // tc
            return lambda rb, cb: (cb // per, layer, rb, cb % per)
        per = self.r // tr
        return lambda rb, cb: (rb // per, layer, rb % per, cb)


def _shard_index(kind, r, c, tr, tc):
    if kind == "col":
        per = c // tc
        return lambda rb, cb: (cb // per, rb, cb % per)
    per = r // tr
    return lambda rb, cb: (rb // per, rb % per, cb)


def _mm_nn(a, w, *, out_dtype, tm, tn, tk, name, add=None, deps=()):
    m, k = a.shape
    assert k == w.rows
    tm = _pick(m, tm, 16)
    tk, tn = w.tiles(tk, tn)
    nk = k // tk
    widx = w.index(tk, tn)
    in_specs = [pl.BlockSpec((tm, tk), lambda n, i, kk: (i, kk)),
                pl.BlockSpec((None, None, tk, tn), lambda n, i, kk: widx(kk, n))]
    args = [a, w.arr]
    if add is not None:
        in_specs.append(pl.BlockSpec((tm, tn), lambda n, i, kk: (i, n)))
        args.append(add)
    in_specs += [pl.BlockSpec(memory_space=pl.ANY)] * len(deps)
    args += list(deps)
    return pl.pallas_call(
        _mm_body(((1,), (0,)), nk, add is not None, unused_in=len(deps)),
        out_shape=jax.ShapeDtypeStruct((m, w.cols), out_dtype),
        grid=(w.cols // tn, m // tm, nk),
        in_specs=in_specs,
        out_specs=pl.BlockSpec((tm, tn), lambda n, i, kk: (i, n)),
        scratch_shapes=[pltpu.VMEM((tm, tn), F32)] if nk > 1 else [],
        compiler_params=_cparams("parallel", "parallel", "arbitrary"),
        name=name,
    )(*args)


def _mm_nt(a, w, *, out_dtype, tm, tn, tko, name, deps=()):
    m, n = a.shape
    assert n == w.cols
    tm = _pick(m, tm, 16)
    if w.kind == "col" and w.s > 1:
        tko = _pick(w.r, tko, 128)
        layer, nsh, width = w.layer, w.s, w.c

        def body(a_ref, w_ref, *rest):
            o_ref = rest[len(deps)]
            acc = None
            for s in range(nsh):
                part = lax.dot_general(a_ref[:, s * width:(s + 1) * width], w_ref[s], (((1,), (1,)), ((), ())),
                                       preferred_element_type=F32)
                acc = part if acc is None else acc + part
            o_ref[...] = acc.astype(o_ref.dtype)

        return pl.pallas_call(
            body,
            out_shape=jax.ShapeDtypeStruct((m, w.rows), out_dtype),
            grid=(w.rows // tko, m // tm),
            in_specs=[pl.BlockSpec((tm, n), lambda ko, i: (i, 0)),
                      pl.BlockSpec((nsh, None, tko, width), lambda ko, i: (0, layer, ko, 0))]
            + [pl.BlockSpec(memory_space=pl.ANY)] * len(deps),
            out_specs=pl.BlockSpec((tm, tko), lambda ko, i: (i, ko)),
            compiler_params=_cparams("parallel", "parallel"),
            name=name,
        )(a, w.arr, *deps)
    tko, tn = w.tiles(tko, tn)
    nk = n // tn
    widx = w.index(tko, tn)
    return pl.pallas_call(
        _mm_body(((1,), (1,)), nk, False, unused_in=len(deps)),
        out_shape=jax.ShapeDtypeStruct((m, w.rows), out_dtype),
        grid=(w.rows // tko, m // tm, nk),
        in_specs=[pl.BlockSpec((tm, tn), lambda ko, i, nn: (i, nn)),
                  pl.BlockSpec((None, None, tko, tn), lambda ko, i, nn: widx(ko, nn))]
        + [pl.BlockSpec(memory_space=pl.ANY)] * len(deps),
        out_specs=pl.BlockSpec((tm, tko), lambda ko, i, nn: (i, ko)),
        scratch_shapes=[pltpu.VMEM((tm, tko), F32)] if nk > 1 else [],
        compiler_params=_cparams("parallel", "parallel", "arbitrary"),
        name=name,
    )(a, w.arr, *deps)


def _mm_tn(a, c, w, *, into, tm, tn, tko, name):
    m, k = a.shape
    tm = _pick(m, tm, 16)
    layer = w.layer
    mc, n = c.shape
    assert mc == m and k == w.rows and n == w.cols
    tko, tn = w.tiles(tko, tn)
    nk = m // tm
    oidx = _shard_index(w.kind, w.r, w.c, tko, tn)
    in_specs = [pl.BlockSpec((tm, tko), lambda ko, nn, mm: (mm, ko)),
                pl.BlockSpec((tm, tn), lambda ko, nn, mm: (mm, nn))]
    args = [a, c]
    if into is not None:
        in_specs.append(pl.BlockSpec(memory_space=pl.ANY))
        args.append(into)
    return pl.pallas_call(
        _mm_body(((0,), (0,)), nk, False, unused_in=len(args) - 2),
        out_shape=jax.ShapeDtypeStruct((w.arr.shape[1], w.s, w.r, w.c), F32),
        grid=(k // tko, n // tn, nk),
        in_specs=in_specs,
        out_specs=pl.BlockSpec((None, None, tko, tn), lambda ko, nn, mm: (layer,) + oidx(ko, nn)),
        scratch_shapes=[pltpu.VMEM((tko, tn), F32)] if nk > 1 else [],
        input_output_aliases={2: 0} if into is not None else {},
        compiler_params=_cparams("parallel", "parallel", "arbitrary"),
        name=name,
    )(*args)


def _mxu_sum(x, ones):
    hi = x.astype(BF16)
    lo = (x - hi.astype(F32)).astype(BF16)
    return jnp.dot(hi, ones, preferred_element_type=F32) + jnp.dot(lo, ones, preferred_element_type=F32)


def _head_rms(x, gain, head_ones):
    r = lax.rsqrt(_mxu_sum(x * x, head_ones) * (1.0 / HEAD_DIM) + RMS_EPS)
    return x * r * gain, r


def _head_rms_bwd(x, r, gain, dy, head_ones, fold):
    t = dy * gain
    dx = r * t - x * (r * r * r) * (_mxu_sum(t * x, head_ones) * (1.0 / HEAD_DIM))
    dg = jnp.broadcast_to(jnp.sum(dy * x * r, axis=0, keepdims=True), (8, x.shape[1]))
    return dx, _mxu_sum(dg, fold)[0:1, :HEAD_DIM]


def _attn_consts(n_q, n_kv, sinks, qg, kg):
    group = n_q // n_kv
    t = jnp.arange(WINDOW, dtype=jnp.int32)[:, None]
    s = jnp.arange(2 * WINDOW, dtype=jnp.int32)[None, :] - WINDOW
    dist = (t - s).astype(F32)
    valid = (dist >= 0) & (dist < WINDOW)
    slopes = jnp.exp2(-8.0 * jnp.arange(1, n_q + 1, dtype=F32) / n_q)
    bias = jnp.where(valid[None], -slopes[:, None, None] * dist[None], NEG_BIG)
    sink = jnp.broadcast_to(sinks.astype(F32).reshape(n_kv, group, 1, 1), (n_kv, group, WINDOW, 128))
    head_ones = lambda h: jnp.kron(jnp.eye(h, dtype=F32), jnp.ones((HEAD_DIM, HEAD_DIM), F32)).astype(BF16)
    fold = lambda h: jnp.tile(jnp.eye(HEAD_DIM, 128, dtype=F32), (h, 1)).astype(BF16)
    return dict(
        bias=bias.reshape(n_kv, group * WINDOW, 2 * WINDOW),
        sink=sink.reshape(n_kv, group * WINDOW, 128),
        qg=jnp.tile(qg, (1, n_q)), kg=jnp.tile(kg, (1, n_kv)),
        q_ones=head_ones(n_q), k_ones=head_ones(n_kv),
        q_fold=fold(n_q), k_fold=fold(n_kv),
        key_ones=jnp.ones((2 * WINDOW, 128), BF16))


_ATTN_CONST_ORDER = ("qg", "kg", "sink", "bias", "q_ones", "k_ones", "q_fold", "k_fold", "key_ones")


def _attn_inputs(q_ref, kc_ref, kp_ref, vc_ref, vp_ref, c):
    q = q_ref[...]
    k2 = jnp.concatenate([kp_ref[...], kc_ref[...]], axis=0)
    v2 = jnp.concatenate([vp_ref[...], vc_ref[...]], axis=0)
    qn, rq = _head_rms(q, c["qg"][...], c["q_ones"][...])
    kn, rk = _head_rms(k2, c["kg"][...], c["k_ones"][...])
    return dict(q=q, rq=rq, qn=qn.astype(BF16), k2=k2, rk=rk, kn=kn.astype(BF16), v2=v2.astype(BF16))


def _attn_probs(x, c, first_mask, kv, group):
    sl = slice(kv * HEAD_DIM, (kv + 1) * HEAD_DIM)
    k2b, v2b = x["kn"][:, sl], x["v2"][:, sl]
    qs = jnp.concatenate([x["qn"][:, (kv * group + g) * HEAD_DIM:(kv * group + g + 1) * HEAD_DIM]
                          for g in range(group)], axis=0)
    s = lax.dot_general(qs, k2b, (((1,), (1,)), ((), ())), preferred_element_type=F32) * (HEAD_DIM ** -0.5)
    s = jnp.where(first_mask, NEG_BIG, s + c["bias"][kv])
    sink = c["sink"][kv]
    m = jnp.maximum(jnp.max(s, axis=-1, keepdims=True), sink)
    twice = lambda a: jnp.concatenate([a, a], axis=1)
    p = jnp.exp(s - twice(m))
    esink = jnp.exp(sink - m)
    inv = 1.0 / (_mxu_sum(p, c["key_ones"][...]) + esink)
    return dict(k2b=k2b, v2b=v2b, qs=qs, pn=p * twice(inv), psink=esink * inv, twice=twice)


def _attn_specs(n_q, n_kv):
    aw, kvw = n_q * HEAD_DIM, n_kv * HEAD_DIM
    group = n_q // n_kv
    kblk, vblk = aw // kvw, aw // kvw + 1

    def specs(nb):
        cur = lambda n: jnp.minimum(n, nb - 1)
        prev = lambda n: jnp.maximum(jnp.minimum(n, nb - 1) - 1, 0)
        return [
            pl.BlockSpec((WINDOW, aw), lambda n: (cur(n), 0)),
            pl.BlockSpec((WINDOW, kvw), lambda n: (cur(n), kblk)),
            pl.BlockSpec((WINDOW, kvw), lambda n: (prev(n), kblk)),
            pl.BlockSpec((WINDOW, kvw), lambda n: (cur(n), vblk)),
            pl.BlockSpec((WINDOW, kvw), lambda n: (prev(n), vblk)),
        ]
    whole = lambda shape: pl.BlockSpec(shape, lambda n: (0,) * len(shape))
    return specs, whole


def _attn_fwd(z, qg, kg, sinks, *, n_q, n_kv, name, deps=()):
    L = z.shape[0]
    nb = L // WINDOW
    aw = n_q * HEAD_DIM
    group = n_q // n_kv
    consts = _attn_consts(n_q, n_kv, sinks, qg, kg)
    specs, whole = _attn_specs(n_q, n_kv)
    nc = len(_ATTN_CONST_ORDER)

    def body(q_ref, kc_ref, kp_ref, vc_ref, vp_ref, *rest):
        c = dict(zip(_ATTN_CONST_ORDER, rest[:nc]))
        o_ref = rest[-1]
        n = pl.program_id(0)
        col = lax.broadcasted_iota(jnp.int32, (group * WINDOW, 2 * WINDOW), 1)
        first_mask = jnp.logical_and(n == 0, col < WINDOW)
        x = _attn_inputs(q_ref, kc_ref, kp_ref, vc_ref, vp_ref, c)
        for kv in range(n_kv):
            a = _attn_probs(x, c, first_mask, kv, group)
            o = jnp.dot(a["pn"].astype(BF16), a["v2b"], preferred_element_type=F32)
            for g in range(group):
                h = kv * group + g
                o_ref[:, h * HEAD_DIM:(h + 1) * HEAD_DIM] = o[g * WINDOW:(g + 1) * WINDOW].astype(o_ref.dtype)

    return pl.pallas_call(
        body,
        out_shape=jax.ShapeDtypeStruct((L, aw), BF16),
        grid=(nb,),
        in_specs=specs(nb) + [whole(consts[k].shape) for k in _ATTN_CONST_ORDER]
        + [pl.BlockSpec(memory_space=pl.ANY)] * len(deps),
        out_specs=pl.BlockSpec((WINDOW, aw), lambda n: (n, 0)),
        compiler_params=_cparams("parallel"),
        name=name,
    )(z, z, z, z, z, *[consts[k] for k in _ATTN_CONST_ORDER], *deps)


def _attn_bwd(z, do, qg, kg, sinks, *, n_q, n_kv, name):
    L = z.shape[0]
    nb = L // WINDOW
    aw, kvw = n_q * HEAD_DIM, n_kv * HEAD_DIM
    group = n_q // n_kv
    consts = _attn_consts(n_q, n_kv, sinks, qg, kg)
    specs, whole = _attn_specs(n_q, n_kv)
    scale = HEAD_DIM ** -0.5
    nc = len(_ATTN_CONST_ORDER)

    def body(q_ref, kc_ref, kp_ref, vc_ref, vp_ref, do_ref, *rest):
        c = dict(zip(_ATTN_CONST_ORDER, rest[:nc]))
        dq_ref, dkv_ref, dqg_ref, dkg_ref, dsink_ref, carry_ref, dqn_ref, dkn_ref, dv_ref = rest[nc:]
        n = pl.program_id(0)

        @pl.when(n == 0)
        def _():
            dqg_ref[...] = jnp.zeros_like(dqg_ref)
            dkg_ref[...] = jnp.zeros_like(dkg_ref)
            dsink_ref[...] = jnp.zeros_like(dsink_ref)
            carry_ref[...] = jnp.zeros_like(carry_ref)

        @pl.when(n < nb)
        def _():
            col = lax.broadcasted_iota(jnp.int32, (group * WINDOW, 2 * WINDOW), 1)
            first_mask = jnp.logical_and(n == 0, col < WINDOW)
            head_lane = lax.broadcasted_iota(jnp.int32, (1, n_q), 1)
            x = _attn_inputs(q_ref, kc_ref, kp_ref, vc_ref, vp_ref, c)
            dsink = jnp.zeros((1, n_q), F32)
            for kv in range(n_kv):
                a = _attn_probs(x, c, first_mask, kv, group)
                pn = a["pn"]
                dos = jnp.concatenate(
                    [do_ref[:, (kv * group + g) * HEAD_DIM:(kv * group + g + 1) * HEAD_DIM] for g in range(group)],
                    axis=0).astype(BF16)
                dpn = lax.dot_general(dos, a["v2b"], (((1,), (1,)), ((), ())), preferred_element_type=F32)
                ksl = slice(kv * HEAD_DIM, (kv + 1) * HEAD_DIM)
                dv_ref[:, ksl] = lax.dot_general(pn.astype(BF16), dos, (((0,), (0,)), ((), ())), preferred_element_type=F32)
                delta = _mxu_sum(pn * dpn, c["key_ones"][...])
                ds = (pn * (dpn - a["twice"](delta))).astype(BF16)
                dsk = -a["psink"] * delta
                dqn = lax.dot_general(ds, a["k2b"], (((1,), (0,)), ((), ())), preferred_element_type=F32) * scale
                dkn_ref[:, ksl] = lax.dot_general(ds, a["qs"], (((0,), (0,)), ((), ())), preferred_element_type=F32) * scale
                for g in range(group):
                    h = kv * group + g
                    rows = slice(g * WINDOW, (g + 1) * WINDOW)
                    dqn_ref[:, h * HEAD_DIM:(h + 1) * HEAD_DIM] = dqn[rows]
                    dsink = dsink + jnp.where(head_lane == h, jnp.sum(dsk[rows], axis=0, keepdims=True)[:, :n_q], 0.0)
            dq, dqg = _head_rms_bwd(x["q"], x["rq"], c["qg"][...], dqn_ref[...], c["q_ones"][...], c["q_fold"][...])
            dk2, dkg = _head_rms_bwd(x["k2"], x["rk"], c["kg"][...], dkn_ref[...], c["k_ones"][...], c["k_fold"][...])
            dq_ref[...] = dq.astype(dq_ref.dtype)
            dkv_ref[:, :kvw] = (carry_ref[:, :kvw] + dk2[:WINDOW]).astype(dkv_ref.dtype)
            dkv_ref[:, kvw:] = (carry_ref[:, kvw:] + dv_ref[:WINDOW, :]).astype(dkv_ref.dtype)
            carry_ref[:, :kvw] = dk2[WINDOW:]
            carry_ref[:, kvw:] = dv_ref[WINDOW:, :]
            dqg_ref[...] += dqg
            dkg_ref[...] += dkg
            dsink_ref[...] += dsink

        @pl.when(n == nb)
        def _():
            dkv_ref[...] = carry_ref[...].astype(dkv_ref.dtype)

    in_specs = (specs(nb) + [pl.BlockSpec((WINDOW, aw), lambda n: (jnp.minimum(n, nb - 1), 0))]
                + [whole(consts[k].shape) for k in _ATTN_CONST_ORDER])
    return pl.pallas_call(
        body,
        out_shape=(jax.ShapeDtypeStruct((L, aw), BF16), jax.ShapeDtypeStruct((L, 2 * kvw), BF16),
                   jax.ShapeDtypeStruct((1, HEAD_DIM), F32), jax.ShapeDtypeStruct((1, HEAD_DIM), F32),
                   jax.ShapeDtypeStruct((1, n_q), F32)),
        grid=(nb + 1,),
        in_specs=in_specs,
        out_specs=(pl.BlockSpec((WINDOW, aw), lambda n: (jnp.minimum(n, nb - 1), 0)),
                   pl.BlockSpec((WINDOW, 2 * kvw), lambda n: (jnp.maximum(n - 1, 0), 0)),
                   pl.BlockSpec((1, HEAD_DIM), lambda n: (0, 0)),
                   pl.BlockSpec((1, HEAD_DIM), lambda n: (0, 0)),
                   pl.BlockSpec((1, n_q), lambda n: (0, 0))),
        scratch_shapes=[pltpu.VMEM((WINDOW, 2 * kvw), F32), pltpu.VMEM((WINDOW, aw), F32),
                        pltpu.VMEM((2 * WINDOW, kvw), F32), pltpu.VMEM((2 * WINDOW, kvw), F32)],
        compiler_params=_cparams("arbitrary"),
        name=name,
    )(z, z, z, z, z, do, *[consts[k] for k in _ATTN_CONST_ORDER])


def _cmul(ar, ai, br, bi):
    return ar * br - ai * bi, ar * bi + ai * br


def _time_permutation(tc):
    r = jnp.arange(tc)
    src = (r % 8) * (tc // 8) + r // 8
    p = (src[:, None] == jnp.arange(tc)[None, :]).astype(BF16)
    return p, p.T


def _unpermute(pt, x):
    hi = x.astype(BF16)
    lo = (x - hi.astype(F32)).astype(BF16)
    return jnp.dot(pt, hi, preferred_element_type=F32) + jnp.dot(pt, lo, preferred_element_type=F32)


def _segment_scan(xr_ref, xi_ref, ar, ai, cr, ci, ng, reverse):
    n = ar.shape[-1]
    row = lax.broadcasted_iota(jnp.int32, (8, n), 0)
    seeded = 7 if reverse else 0
    a8r = jnp.broadcast_to(ar, (8, n))
    a8i = jnp.broadcast_to(ai, (8, n))
    rows_of = lambda g: pl.ds(pl.multiple_of(((ng - 1 - g) if reverse else g) * 8, 8), 8)

    def recur(g, s):
        rows = rows_of(g)
        sr = a8r * s[0] - a8i * s[1] + xr_ref[rows, :]
        si = a8r * s[1] + a8i * s[0] + xi_ref[rows, :]
        xr_ref[rows, :] = sr
        xi_ref[rows, :] = si
        return sr, si

    fr, fi = lax.fori_loop(0, ng, recur, (jnp.where(row == seeded, cr, 0.0), jnp.where(row == seeded, ci, 0.0)))
    pr, pi = ar, ai
    for _ in range(ng.bit_length() - 1):
        pr, pi = _cmul(pr, pi, pr, pi)
    for k in (1, 2, 4):
        keep = (row < 8 - k) if reverse else (row >= k)
        shift = (8 - k) if reverse else k
        mr, mi = jnp.where(keep, pr, 0.0), jnp.where(keep, pi, 0.0)
        tr, ti = pltpu.roll(fr, shift, 0), pltpu.roll(fi, shift, 0)
        fr, fi = fr + mr * tr - mi * ti, fi + mr * ti + mi * tr
        pr, pi = _cmul(pr, pi, pr, pi)
    shift = 7 if reverse else 1
    before_r, before_i = pltpu.roll(fr, shift, 0), pltpu.roll(fi, shift, 0)

    def inherit(g, d):
        rows = rows_of(g)
        dr = a8r * d[0] - a8i * d[1]
        di = a8r * d[1] + a8i * d[0]
        xr_ref[rows, :] = xr_ref[rows, :] + dr
        xi_ref[rows, :] = xi_ref[rows, :] + di
        return dr, di

    lax.fori_loop(0, ng, inherit, (jnp.where(row == seeded, 0.0, before_r), jnp.where(row == seeded, 0.0, before_i)))
    out = 0 if reverse else 7
    return (fr[out:out + 1], fi[out:out + 1],
            jnp.where(row == seeded, cr, before_r), jnp.where(row == seeded, ci, before_i))


def _blockdiag(x):
    g, a, b = x.shape
    j = g // SSM_LANE_GROUPS
    eye = jnp.eye(SSM_LANE_GROUPS, dtype=x.dtype)
    y = x.reshape(j, SSM_LANE_GROUPS, a, 1, b) * eye[None, :, None, :, None]
    return y.reshape(j, SSM_LANE_GROUPS * a, SSM_LANE_GROUPS * b)


def _blockdiag_extract(y, a, b):
    j = y.shape[0]
    y = y.reshape(j, SSM_LANE_GROUPS, a, SSM_LANE_GROUPS, b)
    return jnp.einsum("jgahb,gh->jgab", y, jnp.eye(SSM_LANE_GROUPS, dtype=y.dtype)).reshape(j * SSM_LANE_GROUPS, a, b)


def _ssm_disc(lr, li, ldt, brt, bit):
    dt = jnp.exp(ldt)
    mag = jnp.exp(lr * dt)
    ar = mag * jnp.cos(li * dt)
    ai = mag * jnp.sin(li * dt)
    den = lr * lr + li * li
    fr = ((ar - 1.0) * lr + ai * li) / den
    fi = (ai * lr - (ar - 1.0) * li) / den
    bbr = fr[:, None, :] * brt - fi[:, None, :] * bit
    bbi = fr[:, None, :] * bit + fi[:, None, :] * brt
    return ar, ai, bbr, bbi


def _ssm_prep(lr, li, ldt, brt, bit, *, name):
    g, h, p = brt.shape

    def body(lr_ref, li_ref, ldt_ref, brt_ref, bit_ref, ar_ref, ai_ref, bbr_ref, bbi_ref):
        ar, ai, bbr, bbi = _ssm_disc(lr_ref[...], li_ref[...], ldt_ref[...], brt_ref[...], bit_ref[...])
        ar_ref[...] = ar
        ai_ref[...] = ai
        bbr_ref[...] = bbr
        bbi_ref[...] = bbi

    gp = jax.ShapeDtypeStruct((g, p), F32)
    ghp = jax.ShapeDtypeStruct((g, h, p), F32)
    return pl.pallas_call(body, out_shape=(gp, gp, ghp, ghp), name=name)(lr, li, ldt, brt, bit)


def _ssm_prep_bwd(lr, li, ldt, brt, bit, dar, dai, dbbr, dbbi, *, name):
    g, h, p = brt.shape

    def body(lr_ref, li_ref, ldt_ref, brt_ref, bit_ref, dar_ref, dai_ref, dbbr_ref, dbbi_ref,
             dlr_ref, dli_ref, dldt_ref, dbrt_ref, dbit_ref):
        _, vjp = jax.vjp(_ssm_disc, lr_ref[...], li_ref[...], ldt_ref[...], brt_ref[...], bit_ref[...])
        dlr, dli, dldt, dbrt, dbit = vjp((dar_ref[...], dai_ref[...], dbbr_ref[...], dbbi_ref[...]))
        dlr_ref[...] = dlr
        dli_ref[...] = dli
        dldt_ref[...] = dldt
        dbrt_ref[...] = dbrt
        dbit_ref[...] = dbit

    gp = jax.ShapeDtypeStruct((g, p), F32)
    ghp = jax.ShapeDtypeStruct((g, h, p), F32)
    return pl.pallas_call(body, out_shape=(gp, gp, jax.ShapeDtypeStruct((g, 1), F32), ghp, ghp), name=name)(
        lr, li, ldt, brt, bit, dar, dai, dbbr, dbbi)


def _ssm_specs(tc, nlanes, nch, u_colblk, chunk_of):
    return [
        pl.BlockSpec((tc, nch), lambda j, c: (chunk_of(c), u_colblk + j)),
        pl.BlockSpec((1, nlanes), lambda j, c: (0, j)),
        pl.BlockSpec((1, nlanes), lambda j, c: (0, j)),
        pl.BlockSpec((None, nch, nlanes), lambda j, c: (j, 0, 0)),
        pl.BlockSpec((None, nch, nlanes), lambda j, c: (j, 0, 0)),
        pl.BlockSpec((None, nlanes, nch), lambda j, c: (j, 0, 0)),
        pl.BlockSpec((None, nlanes, nch), lambda j, c: (j, 0, 0)),
        pl.BlockSpec((1, nch), lambda j, c: (0, j)),
        pl.BlockSpec((tc, tc), lambda j, c: (0, 0)),
        pl.BlockSpec((tc, tc), lambda j, c: (0, 0)),
    ]


def _ssm_fwd(z, ar, ai, bblk_r, bblk_i, cblk_r, cblk_i, d, *, u_col, tc, name, deps=()):
    L = z.shape[0]
    nj, nch, nlanes = bblk_r.shape
    w = nj * nch
    nc = L // tc
    ng = tc // 8

    assert ng & (ng - 1) == 0
    perm, perm_t = _time_permutation(tc)

    def body(u_ref, ar_ref, ai_ref, br_ref, bi_ref, cr_ref, ci_ref, d_ref, p_ref, pt_ref, *rest):
        y_ref, s0r_ref, s0i_ref, xr_ref, xi_ref, carr_ref, cari_ref = rest[len(deps):]
        c = pl.program_id(1)

        @pl.when(c == 0)
        def _():
            carr_ref[...] = jnp.zeros_like(carr_ref)
            cari_ref[...] = jnp.zeros_like(cari_ref)

        s0r_ref[...] = carr_ref[...]
        s0i_ref[...] = cari_ref[...]
        u = u_ref[...]
        ub = jnp.dot(p_ref[...], u.astype(BF16), preferred_element_type=F32).astype(BF16)
        xr_ref[...] = jnp.dot(ub, br_ref[...].astype(BF16), preferred_element_type=F32)
        xi_ref[...] = jnp.dot(ub, bi_ref[...].astype(BF16), preferred_element_type=F32)
        cr, ci, _, _ = _segment_scan(xr_ref, xi_ref, ar_ref[...], ai_ref[...], carr_ref[...], cari_ref[...], ng, False)
        carr_ref[...] = cr
        cari_ref[...] = ci
        y = (jnp.dot(xr_ref[...].astype(BF16), cr_ref[...].astype(BF16), preferred_element_type=F32)
             - jnp.dot(xi_ref[...].astype(BF16), ci_ref[...].astype(BF16), preferred_element_type=F32))
        y_ref[...] = _unpermute(pt_ref[...], y) + d_ref[...] * u

    state = jax.ShapeDtypeStruct((nc, 1, nj * nlanes), F32)
    state_spec = pl.BlockSpec((None, 1, nlanes), lambda j, c: (c, 0, j))
    return pl.pallas_call(
        body,
        out_shape=(jax.ShapeDtypeStruct((L, w), F32), state, state),
        grid=(nj, nc),
        in_specs=_ssm_specs(tc, nlanes, nch, u_col // nch, lambda c: c) + [pl.BlockSpec(memory_space=pl.ANY)] * len(deps),
        out_specs=(pl.BlockSpec((tc, nch), lambda j, c: (c, j)), state_spec, state_spec),
        scratch_shapes=[pltpu.VMEM((tc, nlanes), F32), pltpu.VMEM((tc, nlanes), F32),
                        pltpu.VMEM((1, nlanes), F32), pltpu.VMEM((1, nlanes), F32)],
        compiler_params=_cparams("parallel", "arbitrary"),
        name=name,
    )(z, ar, ai, bblk_r, bblk_i, cblk_r, cblk_i, d, perm, perm_t, *deps)


def _ssm_bwd(z, dy, s0r, s0i, ar, ai, bblk_r, bblk_i, cblk_r, cblk_i, d, *, u_col, tc, name):
    L = z.shape[0]
    nj, nch, nlanes = bblk_r.shape
    w = nj * nch
    nc = L // tc
    ng = tc // 8
    chunk_of = lambda c: nc - 1 - c
    assert ng & (ng - 1) == 0
    perm, perm_t = _time_permutation(tc)

    def body(u_ref, ar_ref, ai_ref, br_ref, bi_ref, cr_ref, ci_ref, d_ref, p_ref, pt_ref, dy_ref, s0r_ref, s0i_ref,
             du_ref, dbr_ref, dbi_ref, dcr_ref, dci_ref, dar_ref, dai_ref, dd_ref,
             sr_ref, si_ref, lr_ref, li_ref, carr_ref, cari_ref):
        c = pl.program_id(1)

        @pl.when(c == 0)
        def _():
            for ref in (dbr_ref, dbi_ref, dcr_ref, dci_ref, dar_ref, dai_ref, dd_ref, carr_ref, cari_ref):
                ref[...] = jnp.zeros_like(ref)

        u = u_ref[...]
        dyv = dy_ref[...]
        both = jnp.dot(p_ref[...], jnp.concatenate([u.astype(BF16), dyv.astype(BF16)], axis=1), preferred_element_type=F32)
        ub = both[:, :nch].astype(BF16)
        dyb = both[:, nch:].astype(BF16)
        brb = br_ref[...].astype(BF16)
        bib = bi_ref[...].astype(BF16)
        crb = cr_ref[...].astype(BF16)
        cib = ci_ref[...].astype(BF16)
        a_r, a_i = ar_ref[...], ai_ref[...]

        sr_ref[...] = jnp.dot(ub, brb, preferred_element_type=F32)
        si_ref[...] = jnp.dot(ub, bib, preferred_element_type=F32)
        _, _, start_r, start_i = _segment_scan(sr_ref, si_ref, a_r, a_i, s0r_ref[...], s0i_ref[...], ng, False)

        nt = (((1,), (1,)), ((), ()))
        lr_ref[...] = lax.dot_general(dyb, crb, nt, preferred_element_type=F32)
        li_ref[...] = -lax.dot_general(dyb, cib, nt, preferred_element_type=F32)
        cr, ci, _, _ = _segment_scan(lr_ref, li_ref, a_r, -a_i, carr_ref[...], cari_ref[...], ng, True)
        carr_ref[...] = cr
        cari_ref[...] = ci

        def accumulate(g, carry):
            pr, pi, acc_r, acc_i = carry
            rows = pl.ds(pl.multiple_of(g * 8, 8), 8)
            lr, li = lr_ref[rows, :], li_ref[rows, :]
            return sr_ref[rows, :], si_ref[rows, :], acc_r + lr * pr + li * pi, acc_i + li * pr - lr * pi

        zero8 = jnp.zeros((8, nlanes), F32)
        _, _, acc_r, acc_i = lax.fori_loop(0, ng, accumulate, (start_r, start_i, zero8, zero8))
        dar_ref[...] += jnp.sum(acc_r, axis=0, keepdims=True)
        dai_ref[...] += jnp.sum(acc_i, axis=0, keepdims=True)

        tn = (((0,), (0,)), ((), ()))
        lrb = lr_ref[...].astype(BF16)
        lib = li_ref[...].astype(BF16)
        dcr_ref[...] += lax.dot_general(sr_ref[...].astype(BF16), dyb, tn, preferred_element_type=F32)
        dci_ref[...] -= lax.dot_general(si_ref[...].astype(BF16), dyb, tn, preferred_element_type=F32)
        dbr_ref[...] += lax.dot_general(ub, lrb, tn, preferred_element_type=F32)
        dbi_ref[...] += lax.dot_general(ub, lib, tn, preferred_element_type=F32)
        du = (lax.dot_general(lrb, brb, nt, preferred_element_type=F32)
              + lax.dot_general(lib, bib, nt, preferred_element_type=F32))
        du_ref[...] = (_unpermute(pt_ref[...], du) + d_ref[...] * dyv).astype(du_ref.dtype)
        dd_ref[...] += jnp.sum(dyv * u, axis=0, keepdims=True)

    state_spec = pl.BlockSpec((None, 1, nlanes), lambda j, c: (chunk_of(c), 0, j))
    bshape = jax.ShapeDtypeStruct((nj, nch, nlanes), F32)
    cshape = jax.ShapeDtypeStruct((nj, nlanes, nch), F32)
    ashape = jax.ShapeDtypeStruct((1, nj * nlanes), F32)
    bspec = pl.BlockSpec((None, nch, nlanes), lambda j, c: (j, 0, 0))
    cspec = pl.BlockSpec((None, nlanes, nch), lambda j, c: (j, 0, 0))
    aspec = pl.BlockSpec((1, nlanes), lambda j, c: (0, j))
    big = pltpu.VMEM((tc, nlanes), F32)
    return pl.pallas_call(
        body,
        out_shape=(jax.ShapeDtypeStruct((L, w), BF16), bshape, bshape, cshape, cshape, ashape, ashape,
                   jax.ShapeDtypeStruct((1, w), F32)),
        grid=(nj, nc),
        in_specs=_ssm_specs(tc, nlanes, nch, u_col // nch, chunk_of)
        + [pl.BlockSpec((tc, nch), lambda j, c: (chunk_of(c), j)), state_spec, state_spec],
        out_specs=(pl.BlockSpec((tc, nch), lambda j, c: (chunk_of(c), j)), bspec, bspec, cspec, cspec, aspec, aspec,
                   pl.BlockSpec((1, nch), lambda j, c: (0, j))),
        scratch_shapes=[big, big, big, big, pltpu.VMEM((1, nlanes), F32), pltpu.VMEM((1, nlanes), F32)],
        compiler_params=_cparams("parallel", "arbitrary"),
        name=name,
    )(z, ar, ai, bblk_r, bblk_i, cblk_r, cblk_i, d, perm, perm_t, dy, s0r, s0i)


def _rmsnorm_rows(x, g):
    return x * lax.rsqrt(jnp.mean(x * x, axis=-1, keepdims=True) + RMS_EPS) * g


def _glu_out(y_raw, pre, b):
    yg = jax.nn.gelu(y_raw)
    return yg * jax.nn.sigmoid(pre + b)


def _gate_merge(za, zs, ba, bs, a, bm):
    return jax.nn.sigmoid(za + ba) * a + jax.nn.sigmoid(zs + bs) * bm


def _swiglu(g, u):
    return jax.nn.silu(g) * u


def _row_tile(width_bytes_per_row, rows):
    budget = VMEM_LIMIT_BYTES // 3
    t = max(8, min(1024, budget // (2 * max(width_bytes_per_row, 1))))
    return _pick(rows, t, 16)


def _ssm_params(p, prefix):
    g, pst = p["lam_re"].shape
    ar, ai, bbr, bbi = _ssm_prep(p["lam_re"], p["lam_im"], p["log_dt"], p["b_re_t"], p["b_im_t"], name=prefix + "_ssm_prep")
    return dict(ar=ar.reshape(1, g * pst), ai=ai.reshape(1, g * pst),
                bblk_r=_blockdiag(bbr), bblk_i=_blockdiag(bbi),
                cblk_r=_blockdiag(jnp.swapaxes(p["c_re"], 1, 2)), cblk_i=_blockdiag(jnp.swapaxes(p["c_im"], 1, 2)))


def _layer_fwd(x, p, dims, prefix, deps=(), tick=None):
    tick = tick or (lambda point, arr: ())
    t, d = x.shape
    aw, kvw, sw, ff = dims["aw"], dims["kvw"], dims["sw"], dims["ff"]
    off_u = aw + 2 * kvw
    off_g = off_u + sw
    gblk = _pick(d, 512, 128)
    assert off_g % gblk == 0 and off_u % (SSM_LANE_GROUPS * SSM_GROUP_CH) == 0
    sv = {"x": x}

    h, = _rowmap(_rmsnorm_rows, [(x, "row", d, 0), (p["norm_mix_g"], "vec", d, 0)], [(d, BF16, "row", d)],
                 rows=t, tm=_row_tile(6 * d, t), name=prefix + "_norm_mix", deps=deps)
    deps = tick("norm", h)
    z = _mm_nn(h, p["w_in"], out_dtype=F32, tm=1024, tn=1664, tk=2048, name=prefix + "_mm_in", deps=deps)
    ya = _attn_fwd(z, p["q_norm_g"], p["k_norm_g"], p["attn_sinks"], n_q=dims["n_q"], n_kv=dims["n_kv"],
                   name=prefix + "_attn_fwd", deps=tick("in", z))
    sp = _ssm_params(p, prefix)
    y_raw, s0r, s0i = _ssm_fwd(z, sp["ar"], sp["ai"], sp["bblk_r"], sp["bblk_i"], sp["cblk_r"], sp["cblk_i"], p["ssm_d"],
                               u_col=off_u, tc=dims["tc"], name=prefix + "_ssm_fwd", deps=tick("attn", ya))
    yg, = _rowmap(jax.nn.gelu, [(y_raw, "row", sw, 0)], [(sw, BF16, "row", sw)],
                  rows=t, tm=_row_tile(6 * sw, t), name=prefix + "_gelu", deps=tick("ssm", y_raw))
    pre = _mm_nn(yg, p["ssm_glu_w"], out_dtype=F32, tm=1024, tn=1024, tk=1024, name=prefix + "_mm_glu")
    y2, = _rowmap(_glu_out, [(y_raw, "row", sw, 0), (pre, "row", sw, 0), (p["ssm_glu_b"], "vec", sw, 0)],
                  [(sw, BF16, "row", sw)], rows=t, tm=_row_tile(10 * sw, t), name=prefix + "_glu_out")
    a = _mm_nn(ya, p["w_attn_branch"], out_dtype=BF16, tm=1024, tn=512, tk=1024, name=prefix + "_mm_ab")
    bm = _mm_nn(y2, p["w_ssm_branch"], out_dtype=BF16, tm=1024, tn=512, tk=1024, name=prefix + "_mm_sb")
    ngb = d // gblk
    merged, = _rowmap(
        _gate_merge,
        [(z, "row", gblk, off_g // gblk), (z, "row", gblk, off_g // gblk + ngb),
         (p["gate_bias"], "vec", gblk, 0), (p["gate_bias"], "vec", gblk, ngb),
         (a, "row", gblk, 0), (bm, "row", gblk, 0)],
        [(d, BF16, "row", gblk)], rows=t, tm=_row_tile(18 * gblk, t), ncol=ngb, name=prefix + "_gate")
    x1 = _mm_nn(merged, p["w_out"], out_dtype=F32, tm=512, tn=1024, tk=2048, name=prefix + "_mm_out", add=x)
    h2, = _rowmap(_rmsnorm_rows, [(x1, "row", d, 0), (p["norm_ffn_g"], "vec", d, 0)], [(d, BF16, "row", d)],
                  rows=t, tm=_row_tile(6 * d, t), name=prefix + "_norm_ffn", deps=tick("out", x1))
    gu = _mm_nn(h2, p["w_ffn_in"], out_dtype=BF16, tm=1024, tn=1408, tk=2048, name=prefix + "_mm_ffn_in")
    fblk = _pick(ff, 1408, 128)
    nfb = ff // fblk
    act, = _rowmap(_swiglu, [(gu, "row", fblk, 0), (gu, "row", fblk, nfb)], [(ff, BF16, "row", fblk)],
                   rows=t, tm=_row_tile(10 * fblk, t), ncol=nfb, name=prefix + "_swiglu", deps=tick("ffn_in", gu))
    x2 = _mm_nn(act, p["w_ffn_out"], out_dtype=F32, tm=512, tn=512, tk=5632, name=prefix + "_mm_ffn_out", add=x1,
                deps=tick("act", act))
    sv.update(h=h, z=z, ya=ya, sp=sp, y_raw=y_raw, s0r=s0r, s0i=s0i, yg=yg, pre=pre, y2=y2, a=a, bm=bm,
              merged=merged, x1=x1, h2=h2, gu=gu, act=act)
    return x2, sv


def _layer_bwd(dx2, dx2b, sv, p, dims, prefix, gbuf, deps=(), before_mixer=None, before_in=None):
    t, d = dx2.shape
    aw, kvw, sw, ff = dims["aw"], dims["kvw"], dims["sw"], dims["ff"]
    off_u = aw + 2 * kvw
    off_g = off_u + sw
    gblk = _pick(d, 512, 128)
    ngb = d // gblk
    fblk = _pick(ff, 1408, 128)
    nfb = ff // fblk
    g = {}

    dact = _mm_nt(dx2b, p["w_ffn_out"], out_dtype=BF16, tm=512, tn=2048, tko=1408, name=prefix + "_mm_dact", deps=deps)
    g["w_ffn_out"] = _mm_tn(sv["act"], dx2b, p["w_ffn_out"], into=gbuf.get("w_ffn_out"), tm=4096, tn=1024, tko=512,
                            name=prefix + "_mm_dw_ffn_out")

    def swiglu_bwd(gg, uu, da):
        _, vjp = jax.vjp(_swiglu, gg, uu)
        return vjp(da)

    dgu_g, dgu_u = _rowmap(swiglu_bwd, [(sv["gu"], "row", fblk, 0), (sv["gu"], "row", fblk, nfb), (dact, "row", fblk, 0)],
                           [(ff, BF16, "row", fblk), (ff, BF16, "row", fblk)],
                           rows=t, tm=_row_tile(16 * fblk, t), ncol=nfb, name=prefix + "_swiglu_bwd")
    dgu = jnp.concatenate([dgu_g, dgu_u], axis=1)
    dh2 = _mm_nt(dgu, p["w_ffn_in"], out_dtype=F32, tm=256, tn=1408, tko=512, name=prefix + "_mm_dh2")
    g["w_ffn_in"] = _mm_tn(sv["h2"], dgu, p["w_ffn_in"], into=gbuf.get("w_ffn_in"), tm=4096, tn=1408, tko=512,
                           name=prefix + "_mm_dw_ffn_in")

    def norm_bwd(xx, gg, dh, dres):
        _, vjp = jax.vjp(_rmsnorm_rows, xx, gg)
        dxx, dgg = vjp(dh)
        dxx = dxx + dres
        return dxx, dxx, dgg

    dx1, dx1b, g["norm_ffn_g"] = _rowmap(
        norm_bwd, [(sv["x1"], "row", d, 0), (p["norm_ffn_g"], "vec", d, 0), (dh2, "row", d, 0), (dx2, "row", d, 0)],
        [(d, F32, "row", d), (d, BF16, "row", d), (d, F32, "acc", d)],
        rows=t, tm=_row_tile(22 * d, t), name=prefix + "_norm_ffn_bwd")

    deps = before_mixer(dx1, g) if before_mixer else ()
    dmerged = _mm_nt(dx1b, p["w_out"], out_dtype=BF16, tm=1024, tn=2048, tko=1024, name=prefix + "_mm_dmerged", deps=deps)
    g["w_out"] = _mm_tn(sv["merged"], dx1b, p["w_out"], into=gbuf.get("w_out"), tm=4096, tn=1024, tko=512,
                        name=prefix + "_mm_dw_out")

    def gate_bwd(za, zs, ba, bs, aa, bb, dm):
        _, vjp = jax.vjp(_gate_merge, za, zs, ba, bs, aa, bb)
        dza, dzs, dba, dbs, daa, dbb = vjp(dm)
        return daa, dbb, dza, dzs, dba, dbs

    z = sv["z"]
    da, dbm, dza, dzs, dba, dbs = _rowmap(
        gate_bwd,
        [(z, "row", gblk, off_g // gblk), (z, "row", gblk, off_g // gblk + ngb),
         (p["gate_bias"], "vec", gblk, 0), (p["gate_bias"], "vec", gblk, ngb),
         (sv["a"], "row", gblk, 0), (sv["bm"], "row", gblk, 0), (dmerged, "row", gblk, 0)],
        [(d, BF16, "row", gblk), (d, BF16, "row", gblk), (d, BF16, "row", gblk), (d, BF16, "row", gblk),
         (d, F32, "acc", gblk), (d, F32, "acc", gblk)],
        rows=t, tm=_row_tile(32 * gblk, t), ncol=ngb, name=prefix + "_gate_bwd")
    g["gate_bias"] = jnp.concatenate([dba, dbs], axis=1)
    dya = _mm_nt(da, p["w_attn_branch"], out_dtype=BF16, tm=1024, tn=512, tko=1024, name=prefix + "_mm_dya")
    g["w_attn_branch"] = _mm_tn(sv["ya"], da, p["w_attn_branch"], into=gbuf.get("w_attn_branch"), tm=4096, tn=512,
                                tko=512, name=prefix + "_mm_dw_ab")
    dy2 = _mm_nt(dbm, p["w_ssm_branch"], out_dtype=BF16, tm=1024, tn=512, tko=1024, name=prefix + "_mm_dy2")
    g["w_ssm_branch"] = _mm_tn(sv["y2"], dbm, p["w_ssm_branch"], into=gbuf.get("w_ssm_branch"), tm=4096, tn=512,
                               tko=512, name=prefix + "_mm_dw_sb")

    def glu_bwd(y_raw, pre, b, dy):
        yg = jax.nn.gelu(y_raw)
        _, vjp = jax.vjp(lambda a_, b_, c_: a_ * jax.nn.sigmoid(b_ + c_), yg, pre, b)
        dyg, dpre, db = vjp(dy)
        return dyg, dpre, db

    dyg_direct, dpre, g["ssm_glu_b"] = _rowmap(
        glu_bwd, [(sv["y_raw"], "row", sw, 0), (sv["pre"], "row", sw, 0), (p["ssm_glu_b"], "vec", sw, 0), (dy2, "row", sw, 0)],
        [(sw, F32, "row", sw), (sw, BF16, "row", sw), (sw, F32, "acc", sw)],
        rows=t, tm=_row_tile(24 * sw, t), name=prefix + "_glu_bwd")
    dyg2 = _mm_nt(dpre, p["ssm_glu_w"], out_dtype=F32, tm=1024, tn=1024, tko=1024, name=prefix + "_mm_dyg")
    g["ssm_glu_w"] = _mm_tn(sv["yg"], dpre, p["ssm_glu_w"], into=gbuf.get("ssm_glu_w"), tm=4096, tn=1024, tko=512,
                            name=prefix + "_mm_dw_glu")

    def gelu_bwd(y_raw, d1, d2):
        _, vjp = jax.vjp(jax.nn.gelu, y_raw)
        return vjp(d1 + d2)[0]

    dy_raw, = _rowmap(gelu_bwd, [(sv["y_raw"], "row", sw, 0), (dyg_direct, "row", sw, 0), (dyg2, "row", sw, 0)],
                      [(sw, F32, "row", sw)], rows=t, tm=_row_tile(20 * sw, t), name=prefix + "_gelu_bwd")
    sp = sv["sp"]
    du, dbr, dbi, dcr, dci, dar, dai, g["ssm_d"] = _ssm_bwd(
        z, dy_raw, sv["s0r"], sv["s0i"], sp["ar"], sp["ai"], sp["bblk_r"], sp["bblk_i"], sp["cblk_r"], sp["cblk_i"],
        p["ssm_d"], u_col=off_u, tc=dims["tc"], name=prefix + "_ssm_bwd")
    ngr, pst = p["lam_re"].shape
    hch = SSM_GROUP_CH
    dlr, dli, dldt, dbrt, dbit = _ssm_prep_bwd(
        p["lam_re"], p["lam_im"], p["log_dt"], p["b_re_t"], p["b_im_t"],
        dar.reshape(ngr, pst), dai.reshape(ngr, pst), _blockdiag_extract(dbr, hch, pst), _blockdiag_extract(dbi, hch, pst),
        name=prefix + "_ssm_prep_bwd")
    g.update(ssm_lambda_re=dlr, ssm_lambda_im=dli, ssm_log_dt=dldt.reshape(ngr),
             ssm_b_re=jnp.swapaxes(dbrt, 1, 2), ssm_b_im=jnp.swapaxes(dbit, 1, 2),
             ssm_c_re=jnp.swapaxes(_blockdiag_extract(dcr, pst, hch), 1, 2),
             ssm_c_im=jnp.swapaxes(_blockdiag_extract(dci, pst, hch), 1, 2))

    dq, dkv, g["q_norm_g"], g["k_norm_g"], g["attn_sinks"] = _attn_bwd(
        z, dya, p["q_norm_g"], p["k_norm_g"], p["attn_sinks"], n_q=dims["n_q"], n_kv=dims["n_kv"], name=prefix + "_attn_bwd")

    dz = jnp.concatenate([dq, dkv, du, dza, dzs], axis=1)
    deps = before_in(dq, g) if before_in else ()
    dh = _mm_nt(dz, p["w_in"], out_dtype=F32, tm=512, tn=1664, tko=512, name=prefix + "_mm_dh", deps=deps)
    g["w_in"] = _mm_tn(sv["h"], dz, p["w_in"], into=gbuf.get("w_in"), tm=4096, tn=1664, tko=512,
                       name=prefix + "_mm_dw_in")
    dx, dxb, g["norm_mix_g"] = _rowmap(
        norm_bwd, [(sv["x"], "row", d, 0), (p["norm_mix_g"], "vec", d, 0), (dh, "row", d, 0), (dx1, "row", d, 0)],
        [(d, F32, "row", d), (d, BF16, "row", d), (d, F32, "acc", d)],
        rows=t, tm=_row_tile(22 * d, t), name=prefix + "_norm_mix_bwd")
    return dx, dxb, g


def _loss_and_grad(y, target):
    t, d = y.shape

    def fn(yy, tt):
        e = yy - tt
        dy = e * (1.0 / d)
        return dy, dy, jnp.sum(e * e, keepdims=True).reshape(1, 1)

    dy, dyb, sq = _rowmap(fn, [(y, "row", d, 0), (target, "row", d, 0)],
                          [(d, F32, "row", d), (d, BF16, "row", d), (1, F32, "acc", 1)],
                          rows=t, tm=_row_tile(14 * d, t), name="loss")
    return sq, dy, dyb


def _local_step(x, target, n_layers, weights_of, dims, fwd_tick=None, bwd_tick=None):
    saved, params = [], []
    h = x
    for l in range(n_layers):
        p, deps = weights_of(l, h)
        params.append(p)
        tick = (lambda point, arr, l=l: fwd_tick(l, point, arr)) if fwd_tick else None
        h, sv = _layer_fwd(h, p, dims, "l%d" % l, deps, tick)
        saved.append(sv)
    sq, dy, dyb = _loss_and_grad(h, target)
    grads = [None] * n_layers
    deps = ()
    for l in reversed(range(n_layers)):
        if bwd_tick:
            ffn_done = lambda arr, g, l=l: bwd_tick(l, "ffn", (arr, g["w_ffn_in"], g["w_ffn_out"]),
                                                    {n: g[n] for n in ("w_ffn_in", "w_ffn_out")})
            mixer_done = lambda arr, g, l=l: bwd_tick(l, "mixer", (arr, g["ssm_d"]) + tuple(g[n] for n in MIXER_WEIGHTS),
                                                      {n: g[n] for n in MIXER_WEIGHTS})
        else:
            ffn_done = mixer_done = None
        dy, dyb, grads[l] = _layer_bwd(dy, dyb, saved[l], params[l], dims, "l%d" % l, {}, deps=deps,
                                       before_mixer=ffn_done, before_in=mixer_done)
        if bwd_tick:
            deps = bwd_tick(l, "in", (dy, grads[l]["w_in"]), {"w_in": grads[l]["w_in"]})
    return sq, dy, grads, deps


COL_SHARDED = ("w_in", "w_attn_branch", "w_ssm_branch", "w_ffn_in")
ROW_SHARDED = ("ssm_glu_w", "w_out", "w_ffn_out")
BIG_WEIGHTS = COL_SHARDED + ROW_SHARDED
WEIGHT_NAMES = ("norm_mix_g", "w_in", "gate_bias", "q_norm_g", "k_norm_g", "attn_sinks", "ssm_lambda_re",
                "ssm_lambda_im", "ssm_log_dt", "ssm_b_re", "ssm_b_im", "ssm_c_re", "ssm_c_im", "ssm_d", "ssm_glu_w",
                "ssm_glu_b", "w_attn_branch", "w_ssm_branch", "w_out", "norm_ffn_g", "w_ffn_in", "w_ffn_out")
SMALL_WEIGHTS = tuple(n for n in WEIGHT_NAMES if n not in BIG_WEIGHTS)
MIXER_WEIGHTS = ("w_out", "w_attn_branch", "w_ssm_branch", "ssm_glu_w")
WEIGHT_GROUPS = {"in": ("w_in",), "mixer": MIXER_WEIGHTS, "ffn": ("w_ffn_in", "w_ffn_out")}


def _dims(d, shapes, tc):
    s, _, aw, _ = shapes["w_attn_branch"]
    sw = shapes["w_ssm_branch"][2]
    in_w = shapes["w_in"][3] * s
    kvw = (in_w - aw - sw - 2 * d) // 2
    ff = shapes["w_ffn_out"][2] * s
    return dict(aw=aw, kvw=kvw, sw=sw, ff=ff, n_q=aw // HEAD_DIM, n_kv=kvw // HEAD_DIM, tc=tc)


def _big_params(big):
    p = {n: _Weight(a, 0, "col") for n, a in big.items() if n in COL_SHARDED}
    p.update({n: _Weight(a.reshape(1, 1, -1, a.shape[-1]), 0, "col") for n, a in big.items() if n in ROW_SHARDED})
    return p


def _layer_params(l, small):
    p = {}
    for n in ("norm_mix_g", "gate_bias", "q_norm_g", "k_norm_g", "ssm_d", "ssm_glu_b", "norm_ffn_g"):
        p[n] = small[n][l][None]
    p["attn_sinks"] = small["attn_sinks"][l]
    p["lam_re"] = small["ssm_lambda_re"][l]
    p["lam_im"] = small["ssm_lambda_im"][l]
    p["log_dt"] = small["ssm_log_dt"][l][:, None]
    p["b_re_t"] = jnp.swapaxes(small["ssm_b_re"][l], 1, 2)
    p["b_im_t"] = jnp.swapaxes(small["ssm_b_im"][l], 1, 2)
    p["c_re"] = small["ssm_c_re"][l]
    p["c_im"] = small["ssm_c_im"][l]
    return p


_ANY = pl.BlockSpec(memory_space=pl.ANY)
_MESH_ID = pl.DeviceIdType.MESH


def _coords():
    return lax.axis_index("x"), lax.axis_index("y"), lax.axis_index("c")


def _my_chip():
    return (2 * lax.axis_index("x") + lax.axis_index("y")).astype(jnp.int32).reshape(1)


def _my_core():
    return lax.axis_index("c").astype(jnp.int32).reshape(1)


def _cast_into_slot(w, layer, *, name, deps=()):
    _, r, c = w.shape
    tm = _row_tile(12 * c, r)

    def body(me_ref, w_ref, *rest):
        rest[-1][...] = w_ref[...].astype(rest[-1].dtype)

    return pl.pallas_call(
        body,
        out_shape=jax.ShapeDtypeStruct((N_CHIPS, 1, r, c), BF16),
        grid_spec=pltpu.PrefetchScalarGridSpec(
            num_scalar_prefetch=1,
            grid=(r // tm,),
            in_specs=[pl.BlockSpec((None, tm, c), lambda i, me: (layer, i, 0))]
            + [pl.BlockSpec(memory_space=pl.ANY)] * len(deps),
            out_specs=pl.BlockSpec((None, None, tm, c), lambda i, me: (me[0], 0, i, 0)),
        ),
        compiler_params=_cparams("parallel"),
        name=name,
    )(_my_chip(), w, *deps)


_HBM = pl.BlockSpec(memory_space=pltpu.HBM)
_SEM = pl.BlockSpec(memory_space=pltpu.SEMAPHORE)
_DATAFLOW = pltpu.SideEffectType.DATAFLOW_SIDE_EFFECTING


class _SplitExchange:
    def __init__(self, srcs, lands, build, n_copies, name):
        self.build, self.n, self.name = build, n_copies, name
        self.ns, self.nl = len(srcs), len(lands)
        self.bufs = [pltpu.with_memory_space_constraint(a, pltpu.HBM) for a in list(srcs) + list(lands)]

    def _copies(self, refs, send_sems, recv_sems):
        triples = self.build(refs[:self.ns], refs[self.ns:self.ns + self.nl])
        assert len(triples) == self.n
        return [pltpu.make_async_remote_copy(src_ref=s, dst_ref=d, send_sem=send_sems.at[k], recv_sem=recv_sems.at[k],
                                             device_id=to, device_id_type=_MESH_ID) for k, (s, d, to) in enumerate(triples)]

    def start(self, deps=()):
        nb = self.ns + self.nl

        def body(*refs):
            outs = refs[nb + len(deps):]
            for cp in self._copies(refs, outs[0], outs[1]):
                cp.start()
            outs[-1][...] = jnp.zeros_like(outs[-1])

        sems = pltpu.SemaphoreType.DMA((self.n,))
        res = pl.pallas_call(
            body,
            out_shape=(sems, sems, *[pltpu.HBM(b.shape, b.dtype) for b in self.bufs], jax.ShapeDtypeStruct((8, 128), F32)),
            in_specs=[_HBM] * nb + [_ANY] * len(deps),
            out_specs=(_SEM, _SEM, *[_HBM] * nb, pl.BlockSpec(memory_space=pltpu.VMEM)),
            input_output_aliases={i: 2 + i for i in range(nb)},
            compiler_params=pltpu.CompilerParams(has_side_effects=_DATAFLOW),
            name=self.name + "_start",
        )(*self.bufs, *deps)
        self.send_sems, self.recv_sems = res[0], res[1]
        self.bufs = list(res[2:2 + nb])
        return res[-1]

    def wait(self, after=()):
        nb = self.ns + self.nl
        after = tuple(after) if isinstance(after, (tuple, list)) else (after,)

        def body(*refs):
            for cp in self._copies(refs, refs[nb], refs[nb + 1]):
                cp.wait_send()
                cp.wait_recv()

        res = pl.pallas_call(
            body,
            out_shape=tuple(pltpu.HBM(b.shape, b.dtype) for b in self.bufs),
            in_specs=[_HBM] * nb + [_SEM, _SEM] + [_ANY] * len(after),
            out_specs=tuple([_HBM] * nb),
            input_output_aliases={i: i for i in range(nb)},
            compiler_params=pltpu.CompilerParams(has_side_effects=_DATAFLOW),
            name=self.name + "_wait",
        )(*self.bufs, self.send_sems, self.recv_sems, *after)
        res = list(res)
        return res[:self.ns], res[self.ns:]


def _other_chips(x, y):
    return [(1 - x, y), (x, 1 - y), (1 - x, 1 - y)]


def _gather_steps(bufs, tag, deps, publish):
    n = len(bufs)
    half = lambda ref, i, slot, hc: ref.at[slot, :, pl.ds(hc * (bufs[i].shape[2] // 2), bufs[i].shape[2] // 2), :]

    def over_ici(srcs, lands):
        x, y, c = _coords()
        me = 2 * x + y
        return [(half(srcs[i], i, me, c), half(srcs[i], i, me, c), (px, py, c))
                for i in range(n) for px, py in _other_chips(x, y)]

    def to_sibling(srcs, lands):
        x, y, c = _coords()
        return [(half(srcs[i], i, 2 * px + py, c), half(srcs[i], i, 2 * px + py, c), (x, y, 1 - c))
                for i in range(n) for px, py in _other_chips(x, y)]

    ex = _SplitExchange(bufs, [], over_ici, 3 * n, tag + "_ici")
    after = yield ex.start(deps)
    bufs, _ = ex.wait(after)
    ex = _SplitExchange(bufs, [], to_sibling, 3 * n, tag + "_d2d")
    after = yield ex.start()
    bufs, _ = ex.wait(after)
    publish(bufs)


def _reduce_steps(grads, tag, publish):
    n = len(grads)
    rh = [g.shape[1] // 2 for g in grads]
    theirs = [lax.empty((g.shape[0], g.shape[1] // 2, g.shape[2]), F32) for g in grads]

    def halves(srcs, lands):
        x, y, c = _coords()
        return [(srcs[i].at[:, pl.ds((1 - c) * rh[i], rh[i]), :], lands[i], (x, y, 1 - c)) for i in range(n)]

    def chips(srcs, lands):
        x, y, c = _coords()
        me = 2 * x + y
        return [(srcs[i].at[2 * px + py], lands[i].at[me], (px, py, c)) for i in range(n) for px, py in _other_chips(x, y)]

    def sibling(srcs, lands):
        x, y, c = _coords()
        return [(srcs[i], lands[i], (x, y, 1 - c)) for i in range(n)]

    ex = _SplitExchange(grads, theirs, halves, n, tag + "_halves")
    after = yield ex.start()
    grads, theirs = ex.wait(after)
    parts = [_add_own_half(g, t, name="%s_add_own_half_%d" % (tag, i)) for i, (g, t) in enumerate(zip(grads, theirs))]
    ex = _SplitExchange(parts, [lax.empty(p.shape, p.dtype) for p in parts], chips, 3 * n, tag + "_chips")
    after = yield ex.start()
    parts, got = ex.wait(after)
    mine = [_sum_chips(p, g, name="%s_sum_chips_%d" % (tag, i)) for i, (p, g) in enumerate(zip(parts, got))]
    ex = _SplitExchange(mine, [lax.empty(m.shape, m.dtype) for m in mine], sibling, n, tag + "_sibling")
    after = yield ex.start()
    mine, theirs = ex.wait(after)
    publish(list(zip(mine, theirs)))


def _allreduce_steps(buf, tag, publish):
    r, c = buf.shape

    def to_sibling(srcs, lands):
        x, y, cc = _coords()
        return [(srcs[0], lands[0], (x, y, 1 - cc))]

    def over_ici(srcs, lands):
        x, y, cc = _coords()
        me = 2 * x + y
        return [(srcs[0].at[me], srcs[0].at[me], (px, py, cc)) for px, py in _other_chips(x, y)]

    def halves(srcs, lands):
        x, y, cc = _coords()
        return [(srcs[0].at[cc], srcs[0].at[cc], (x, y, 1 - cc))]

    ex = _SplitExchange([buf], [lax.empty(buf.shape, buf.dtype)], to_sibling, 1, tag + "_cores")
    after = yield ex.start()
    (mine,), (theirs,) = ex.wait(after)
    ex = _SplitExchange([_add_half_into_slot(mine, theirs, name=tag + "_chip_sum")], [], over_ici, N_CHIPS - 1, tag + "_chips")
    after = yield ex.start()
    (parts,), _ = ex.wait(after)
    total_half = _sum_slots(parts, name=tag + "_sum_chips")
    ex = _SplitExchange([_place_into_slot(total_half, 2, _my_core(), name=tag + "_place_half")], [], halves, 1, tag + "_halves")
    after = yield ex.start()
    (both,), _ = ex.wait(after)
    publish(both.reshape(r, c))


class _Exchanges:
    def __init__(self):
        self.running = []

    def launch(self, steps):
        self.running.append(steps)
        return next(steps)

    def advance(self, steps, after):
        try:
            return steps.send(after)
        except StopIteration:
            self.running.remove(steps)
            return None

    def advance_all(self, after):
        tokens = [self.advance(steps, after) for steps in list(self.running)]
        return tuple(t for t in tokens if t is not None)


def _add_own_half(g, theirs, *, name):
    s, r, c = g.shape
    rh = r // 2
    tm = _row_tile(10 * c, rh)
    nb = rh // tm

    def body(core_ref, g_ref, t_ref, o_ref):
        o_ref[...] = (g_ref[...] + t_ref[...]).astype(o_ref.dtype)

    return pl.pallas_call(
        body,
        out_shape=jax.ShapeDtypeStruct((s, rh, c), BF16),
        grid_spec=pltpu.PrefetchScalarGridSpec(
            num_scalar_prefetch=1,
            grid=(s, nb),
            in_specs=[pl.BlockSpec((None, tm, c), lambda k, i, core: (k, core[0] * nb + i, 0)),
                      pl.BlockSpec((None, tm, c), lambda k, i, core: (k, i, 0))],
            out_specs=pl.BlockSpec((None, tm, c), lambda k, i, core: (k, i, 0)),
        ),
        compiler_params=_cparams("parallel", "parallel"),
        name=name,
    )(_my_core(), g, theirs)


def _sum_chips(part, got, *, name):
    s, rh, c = part.shape
    tm = _row_tile(14 * c, rh)

    def body(me_ref, p_ref, a_ref, b_ref, c_ref, o_ref):
        o_ref[...] = ((p_ref[...].astype(F32) + a_ref[...].astype(F32)) + b_ref[...].astype(F32)) + c_ref[...].astype(F32)

    slot = lambda k: (lambda i, me: ((me[0] + k) % s, i, 0))
    return pl.pallas_call(
        body,
        out_shape=jax.ShapeDtypeStruct((rh, c), F32),
        grid_spec=pltpu.PrefetchScalarGridSpec(
            num_scalar_prefetch=1,
            grid=(rh // tm,),
            in_specs=[pl.BlockSpec((None, tm, c), slot(k)) for k in range(s)],
            out_specs=pl.BlockSpec((tm, c), lambda i, me: (i, 0)),
        ),
        compiler_params=_cparams("parallel"),
        name=name,
    )(_my_chip(), part, got, got, got)


def _place_into_slot(buf, n_slots, slot, *, name):
    r, c = buf.shape
    tm = _row_tile(8 * c, r)

    def body(slot_ref, i_ref, o_ref):
        o_ref[...] = i_ref[...]

    return pl.pallas_call(
        body,
        out_shape=jax.ShapeDtypeStruct((n_slots, r, c), buf.dtype),
        grid_spec=pltpu.PrefetchScalarGridSpec(
            num_scalar_prefetch=1,
            grid=(r // tm,),
            in_specs=[pl.BlockSpec((tm, c), lambda i, s: (i, 0))],
            out_specs=pl.BlockSpec((None, tm, c), lambda i, s: (s[0], i, 0)),
        ),
        compiler_params=_cparams("parallel"),
        name=name,
    )(slot, buf)


def _add_half_into_slot(mine, theirs, *, name):
    r, c = mine.shape
    rh = r // 2
    tm = _row_tile(12 * c, rh)
    nb = rh // tm
    where = jnp.concatenate([_my_chip(), _my_core()])

    def body(where_ref, a_ref, b_ref, o_ref):
        o_ref[...] = a_ref[...] + b_ref[...]

    half = pl.BlockSpec((tm, c), lambda i, w: (w[1] * nb + i, 0))
    return pl.pallas_call(
        body,
        out_shape=jax.ShapeDtypeStruct((N_CHIPS, rh, c), mine.dtype),
        grid_spec=pltpu.PrefetchScalarGridSpec(
            num_scalar_prefetch=1,
            grid=(nb,),
            in_specs=[half, half],
            out_specs=pl.BlockSpec((None, tm, c), lambda i, w: (w[0], i, 0)),
        ),
        compiler_params=_cparams("parallel"),
        name=name,
    )(where, mine, theirs)


def _sum_slots(arr, *, name):
    s, r, c = arr.shape
    tm = _row_tile(4 * c * (s + 1), r)

    def body(*refs):
        acc = refs[0][...]
        for ref in refs[1:s]:
            acc = acc + ref[...]
        refs[s][...] = acc

    return pl.pallas_call(
        body,
        out_shape=jax.ShapeDtypeStruct((r, c), arr.dtype),
        grid=(r // tm,),
        in_specs=[pl.BlockSpec((None, tm, c), lambda i, k=k: (k, i, 0)) for k in range(s)],
        out_specs=pl.BlockSpec((tm, c), lambda i: (i, 0)),
        compiler_params=_cparams("parallel"),
        name=name,
    )(*([arr] * s))


def _adamw_fn(w, g, m, v):
    m = ADAM_B1 * m + (1.0 - ADAM_B1) * g
    v = ADAM_B2 * v + (1.0 - ADAM_B2) * jnp.square(g)
    m_hat = m / (1.0 - ADAM_B1 ** ADAM_STEP)
    v_hat = v / (1.0 - ADAM_B2 ** ADAM_STEP)
    delta = -ADAM_LR * (m_hat / (jnp.sqrt(v_hat) + ADAM_EPS) + ADAM_WD * w)
    return delta, m, v


def _adamw(w, g, m, v, *, name):
    rows, cols = w.shape
    ins = [(a, "row", cols, 0) for a in (w, g, m, v)]
    outs = [(cols, F32, "row", cols)] * 3
    return _rowmap(_adamw_fn, ins, outs, rows=rows, tm=_row_tile(56 * cols, rows), name=name)


def _adamw_sharded(w, m, v, g_mine, g_sibling, layer, into, *, name, deps=()):
    nl, r, c = w.shape
    rh = r // 2
    tm = _row_tile(40 * c, rh)
    nb = rh // tm
    n_into = 0 if into is None else 4

    def body(core_ref, w_ref, m_ref, v_ref, a_ref, b_ref, *rest):
        g_ref, d_ref, nm_ref, nv_ref = rest[n_into + len(deps):]
        g = jnp.where(pl.program_id(0) == core_ref[0], a_ref[...], b_ref[...])
        delta, nm, nv = _adamw_fn(w_ref[...], g, m_ref[...], v_ref[...])
        g_ref[...] = g
        d_ref[...] = delta
        nm_ref[...] = nm
        nv_ref[...] = nv

    whole = pl.BlockSpec((None, tm, c), lambda h, i, core: (layer, h * nb + i, 0))
    half = pl.BlockSpec((tm, c), lambda h, i, core: (i, 0))
    shape = jax.ShapeDtypeStruct((nl, r, c), F32)
    return pl.pallas_call(
        body,
        out_shape=(shape, shape, shape, shape),
        grid_spec=pltpu.PrefetchScalarGridSpec(
            num_scalar_prefetch=1,
            grid=(2, nb),
            in_specs=[whole, whole, whole, half, half] + [pl.BlockSpec(memory_space=pl.ANY)] * (n_into + len(deps)),
            out_specs=(whole, whole, whole, whole),
        ),
        input_output_aliases={6 + k: k for k in range(n_into)},
        compiler_params=_cparams("parallel", "parallel"),
        name=name,
    )(_my_core(), w, m, v, g_mine, g_sibling, *(into or ()), *deps)


def _pack(arrays):
    flat = jnp.concatenate([a.reshape(-1) for a in arrays])
    pad = (-flat.shape[0]) % (256 * 128)
    return jnp.pad(flat, (0, pad)).reshape(-1, 128)


def _unpack(buf, shapes):
    flat = buf.reshape(-1)
    out, off = [], 0
    for s in shapes:
        n = math.prod(s)
        out.append(flat[off:off + n].reshape(s))
        off += n
    return out


def kernel(x, norm_mix_g, w_in, gate_bias, q_norm_g, k_norm_g, attn_sinks, ssm_lambda_re, ssm_lambda_im, ssm_log_dt, ssm_b_re, ssm_b_im, ssm_c_re, ssm_c_im, ssm_d, ssm_glu_w, ssm_glu_b, w_attn_branch, w_ssm_branch, w_out, norm_ffn_g, w_ffn_in, w_ffn_out, loss_target, m_norm_mix_g, m_w_in, m_gate_bias, m_q_norm_g, m_k_norm_g, m_attn_sinks, m_ssm_lambda_re, m_ssm_lambda_im, m_ssm_log_dt, m_ssm_b_re, m_ssm_b_im, m_ssm_c_re, m_ssm_c_im, m_ssm_d, m_ssm_glu_w, m_ssm_glu_b, m_w_attn_branch, m_w_ssm_branch, m_w_out, m_norm_ffn_g, m_w_ffn_in, m_w_ffn_out, v_norm_mix_g, v_w_in, v_gate_bias, v_q_norm_g, v_k_norm_g, v_attn_sinks, v_ssm_lambda_re, v_ssm_lambda_im, v_ssm_log_dt, v_ssm_b_re, v_ssm_b_im, v_ssm_c_re, v_ssm_c_im, v_ssm_d, v_ssm_glu_w, v_ssm_glu_b, v_w_attn_branch, v_w_ssm_branch, v_w_out, v_norm_ffn_g, v_w_ffn_in, v_w_ffn_out):
    w = dict(norm_mix_g=norm_mix_g, w_in=w_in, gate_bias=gate_bias, q_norm_g=q_norm_g, k_norm_g=k_norm_g,
             attn_sinks=attn_sinks, ssm_lambda_re=ssm_lambda_re, ssm_lambda_im=ssm_lambda_im, ssm_log_dt=ssm_log_dt,
             ssm_b_re=ssm_b_re, ssm_b_im=ssm_b_im, ssm_c_re=ssm_c_re, ssm_c_im=ssm_c_im, ssm_d=ssm_d,
             ssm_glu_w=ssm_glu_w, ssm_glu_b=ssm_glu_b, w_attn_branch=w_attn_branch, w_ssm_branch=w_ssm_branch,
             w_out=w_out, norm_ffn_g=norm_ffn_g, w_ffn_in=w_ffn_in, w_ffn_out=w_ffn_out)
    m = dict(norm_mix_g=m_norm_mix_g, w_in=m_w_in, gate_bias=m_gate_bias, q_norm_g=m_q_norm_g, k_norm_g=m_k_norm_g,
             attn_sinks=m_attn_sinks, ssm_lambda_re=m_ssm_lambda_re, ssm_lambda_im=m_ssm_lambda_im,
             ssm_log_dt=m_ssm_log_dt, ssm_b_re=m_ssm_b_re, ssm_b_im=m_ssm_b_im, ssm_c_re=m_ssm_c_re,
             ssm_c_im=m_ssm_c_im, ssm_d=m_ssm_d, ssm_glu_w=m_ssm_glu_w, ssm_glu_b=m_ssm_glu_b,
             w_attn_branch=m_w_attn_branch, w_ssm_branch=m_w_ssm_branch, w_out=m_w_out, norm_ffn_g=m_norm_ffn_g,
             w_ffn_in=m_w_ffn_in, w_ffn_out=m_w_ffn_out)
    v = dict(norm_mix_g=v_norm_mix_g, w_in=v_w_in, gate_bias=v_gate_bias, q_norm_g=v_q_norm_g, k_norm_g=v_k_norm_g,
             attn_sinks=v_attn_sinks, ssm_lambda_re=v_ssm_lambda_re, ssm_lambda_im=v_ssm_lambda_im,
             ssm_log_dt=v_ssm_log_dt, ssm_b_re=v_ssm_b_re, ssm_b_im=v_ssm_b_im, ssm_c_re=v_ssm_c_re,
             ssm_c_im=v_ssm_c_im, ssm_d=v_ssm_d, ssm_glu_w=v_ssm_glu_w, ssm_glu_b=v_ssm_glu_b,
             w_attn_branch=v_w_attn_branch, w_ssm_branch=v_w_ssm_branch, w_out=v_w_out, norm_ffn_g=v_norm_ffn_g,
             w_ffn_in=v_w_ffn_in, w_ffn_out=v_w_ffn_out)
    n_layers = norm_mix_g.shape[0]
    d_model = x.shape[-1]
    seq = x.shape[1]

    exchanges = _Exchanges()
    params = [_layer_params(l, w) for l in range(n_layers)]
    gathers = {}

    def gather(l, group, bufs, deps=()):
        names = WEIGHT_GROUPS[group]
        steps = _gather_steps(bufs, "ag%d_%s" % (l, group), deps,
                              lambda got: params[l].update(_big_params(dict(zip(names, got)))))
        gathers[l, group] = steps
        return exchanges.launch(steps)

    started = gather(0, "in", [_cast_into_slot(w["w_in"], 0, name="cast0_w_in")])
    casts = {(l, group): [_cast_into_slot(w[n], l, name="cast%d_%s" % (l, n), deps=(started,)) for n in WEIGHT_GROUPS[group]]
             for l in range(n_layers) for group in WEIGHT_GROUPS if (l, group) != (0, "in")}
    forwarded = exchanges.advance(gathers[0, "in"], tuple(b for bufs in casts.values() for b in bufs))
    first_deps = (forwarded, gather(0, "mixer", casts[0, "mixer"], (forwarded,)), gather(0, "ffn", casts[0, "ffn"], (forwarded,)))
    sizes = {n: (N_CHIPS, 1) + w[n].shape[1:] for n in BIG_WEIGHTS}
    dims = _dims(d_model, sizes, min(512, seq))

    def weights_of(l, h):
        if l == 0:
            return params[0], first_deps
        for group in WEIGHT_GROUPS:
            exchanges.advance(gathers[l, group], h)
        return params[l], ()

    def fwd_tick(l, point, arr):
        tokens = []
        if l == 0 and point == "norm":
            tokens.append(exchanges.advance(gathers[0, "in"], arr))
        if l == 0 and point in ("in", "attn"):
            tokens.append(exchanges.advance(gathers[0, "mixer"], arr))
        if l == 0 and point in ("ssm", "out"):
            tokens.append(exchanges.advance(gathers[0, "ffn"], arr))
        if l + 1 < n_layers and point == "attn":
            tokens += [gather(l + 1, group, casts[l + 1, group]) for group in WEIGHT_GROUPS]
        if l + 1 < n_layers and point == "act":
            tokens += [exchanges.advance(gathers[l + 1, group], arr) for group in WEIGHT_GROUPS]
        return tuple(t for t in tokens if t is not None)

    reduced, ready = {}, []

    def bwd_tick(l, stage, arr, stage_grads):
        tokens = exchanges.advance_all(arr)
        names = tuple(stage_grads)

        def publish(halves):
            reduced.update({(l, n): h for n, h in zip(names, halves)})
            ready.append((l, names))

        grads4 = [stage_grads[n].reshape(N_CHIPS, -1, stage_grads[n].shape[-1]) for n in names]
        return tokens + (exchanges.launch(_reduce_steps(grads4, "rs%d_%s" % (l, stage), publish)),)

    sq, dx, grads, last_tokens = _local_step(x[0], loss_target[0], n_layers, weights_of, dims, fwd_tick, bwd_tick)
    loss = lax.psum(sq[0, 0], MESH_AXES) * (0.5 / d_model)

    small_shapes = [w[n].shape for n in SMALL_WEIGHTS]
    small_local = [jnp.stack([grads[l][n].reshape(w[n].shape[1:]) for l in range(n_layers)]) for n in SMALL_WEIGHTS]
    shared = []
    tokens = last_tokens + (exchanges.launch(_allreduce_steps(_pack(small_local), "small_grads", shared.append)),)

    adam = {n: None for n in BIG_WEIGHTS}
    grad, delta, new_m, new_v = {}, {}, {}, {}
    while exchanges.running or ready or shared:
        after = [dx]
        for l, names in ready[:2]:
            for n in names:
                mine, sibling = reduced[l, n]
                adam[n] = _adamw_sharded(w[n], m[n], v[n], mine, sibling, l, adam[n], name="adamw%d_%s" % (l, n), deps=tokens)
                after.append(adam[n][1])
        del ready[:2]
        if shared:
            grad.update(zip(SMALL_WEIGHTS, _unpack(shared.pop(), small_shapes)))
            for n in SMALL_WEIGHTS:
                flat = lambda a: a.reshape(-1, a.shape[-1])
                res = _adamw(flat(w[n]), flat(grad[n]), flat(m[n]), flat(v[n]), name="adamw_" + n)
                delta[n], new_m[n], new_v[n] = [r.reshape(w[n].shape) for r in res]
            after += [delta[n] for n in SMALL_WEIGHTS]
        tokens = exchanges.advance_all(tuple(after))
    for n in BIG_WEIGHTS:
        grad[n], delta[n], new_m[n], new_v[n] = adam[n]

    return (loss, dx[None], *[grad[n] for n in WEIGHT_NAMES], *[delta[n] for n in WEIGHT_NAMES],
            *[new_m[n] for n in WEIGHT_NAMES], *[new_v[n] for n in WEIGHT_NAMES])
```

```python
import functools
import math

import jax
import jax.numpy as jnp
from jax import lax
from jax.experimental import pallas as pl
from jax.experimental.pallas import tpu as pltpu

HEAD_DIM = 64
WINDOW = 128
SSM_GROUP_CH = 16
SSM_LANE_GROUPS = 8
RMS_EPS = 1e-6
ADAM_LR = 0.001
ADAM_B1 = 0.9
ADAM_B2 = 0.999
ADAM_EPS = 1e-08
ADAM_WD = 0.01
ADAM_STEP = 10
NEG_BIG = -1e30
MESH_AXES = ("x", "y", "c")
N_CHIPS = 4
N_DEV = 8
VMEM_LIMIT_BYTES = 56 * 1024 * 1024
BF16 = jnp.bfloat16
F32 = jnp.float32


def _cparams(*semantics):
    return pltpu.CompilerParams(dimension_semantics=semantics, vmem_limit_bytes=VMEM_LIMIT_BYTES)


def _pick(n, target, mult):
    if n <= target:
        return n
    best = None
    for d in range(mult, target + 1, mult):
        if n % d == 0:
            best = d
    assert best is not None, (n, target, mult)
    return best


def _rowmap(fn, ins, outs, *, rows, tm, ncol=1, name, deps=()):
    n_in = len(ins)
    nrow = rows // tm
    assert nrow * tm == rows

    in_specs = []
    for arr, kind, width, coloff in ins:
        if kind == "row":
            in_specs.append(pl.BlockSpec((tm, width), lambda j, i, o=coloff: (i, o + j)))
        elif kind == "vec":
            in_specs.append(pl.BlockSpec((1, width), lambda j, i, o=coloff: (0, o + j)))
        else:
            nd = arr.ndim
            in_specs.append(pl.BlockSpec(arr.shape, lambda j, i, nd=nd: (0,) * nd))
    out_specs, out_shapes = [], []
    for cols, dtype, kind, width in outs:
        if kind == "row":
            out_specs.append(pl.BlockSpec((tm, width), lambda j, i: (i, j)))
            out_shapes.append(jax.ShapeDtypeStruct((rows, cols), dtype))
        else:
            out_specs.append(pl.BlockSpec((1, width), lambda j, i: (0, j)))
            out_shapes.append(jax.ShapeDtypeStruct((1, cols), dtype))

    in_specs += [pl.BlockSpec(memory_space=pl.ANY)] * len(deps)

    def body(*refs):
        i = pl.program_id(1)
        res = fn(*[r[...].astype(F32) for r in refs[:n_in]])
        if not isinstance(res, (tuple, list)):
            res = (res,)
        for (cols, dtype, kind, width), ref, val in zip(outs, refs[n_in + len(deps):], res):
            if kind == "row":
                ref[...] = val.astype(ref.dtype)
            else:
                @pl.when(i == 0)
                def _():
                    ref[...] = jnp.zeros_like(ref)
                ref[...] += val.astype(ref.dtype)

    res = pl.pallas_call(
        body,
        out_shape=tuple(out_shapes),
        grid=(ncol, nrow),
        in_specs=in_specs,
        out_specs=tuple(out_specs),
        compiler_params=_cparams("parallel", "arbitrary"),
        name=name,
    )(*[a[0] for a in ins], *deps)
    return res


def _mm_body(dims, nk, has_add, unused_in=0):
    def body(*refs):
        if has_add:
            a_ref, b_ref, add_ref = refs[:3]
            o_ref = refs[3 + unused_in]
            rest = refs[4 + unused_in:]
        else:
            a_ref, b_ref = refs[:2]
            o_ref = refs[2 + unused_in]
            add_ref = None
            rest = refs[3 + unused_in:]
        part = lax.dot_general(a_ref[...], b_ref[...], (dims, ((), ())), preferred_element_type=F32)
        if nk == 1:
            if add_ref is not None:
                part = part + add_ref[...]
            o_ref[...] = part.astype(o_ref.dtype)
        else:
            acc_ref = rest[0]
            k = pl.program_id(2)

            @pl.when(k == 0)
            def _():
                acc_ref[...] = part

            @pl.when(k > 0)
            def _():
                acc_ref[...] += part

            @pl.when(k == nk - 1)
            def _():
                r = acc_ref[...]
                if add_ref is not None:
                    r = r + add_ref[...]
                o_ref[...] = r.astype(o_ref.dtype)
    return body


class _Weight:
    def __init__(self, arr, layer, kind):
        self.arr, self.layer, self.kind = arr, layer, kind
        self.s, _, self.r, self.c = arr.shape
        self.rows = self.r * (self.s if kind == "row" else 1)
        self.cols = self.c * (self.s if kind == "col" else 1)

    def tiles(self, tr, tc):
        return _pick(self.r, tr, 128), _pick(self.c, tc, 128)

    def index(self, tr, tc):
        layer = self.layer
        if self.kind == "col":
            per = self.c // tc
            return lambda rb, cb: (cb // per, layer, rb, cb % per)
        per = self.r // tr
        return lambda rb, cb: (rb // per, layer, rb % per, cb)


def _shard_index(kind, r, c, tr, tc):
    if kind == "col":
        per = c // tc
        return lambda rb, cb: (cb // per, rb, cb % per)
    per = r // tr
    return lambda rb, cb: (rb // per, rb % per, cb)


def _mm_nn(a, w, *, out_dtype, tm, tn, tk, name, add=None, deps=()):
    m, k = a.shape
    assert k == w.rows
    tm = _pick(m, tm, 16)
    tk, tn = w.tiles(tk, tn)
    nk = k // tk
    widx = w.index(tk, tn)
    in_specs = [pl.BlockSpec((tm, tk), lambda n, i, kk: (i, kk)),
                pl.BlockSpec((None, None, tk, tn), lambda n, i, kk: widx(kk, n))]
    args = [a, w.arr]
    if add is not None:
        in_specs.append(pl.BlockSpec((tm, tn), lambda n, i, kk: (i, n)))
        args.append(add)
    in_specs += [pl.BlockSpec(memory_space=pl.ANY)] * len(deps)
    args += list(deps)
    return pl.pallas_call(
        _mm_body(((1,), (0,)), nk, add is not None, unused_in=len(deps)),
        out_shape=jax.ShapeDtypeStruct((m, w.cols), out_dtype),
        grid=(w.cols // tn, m // tm, nk),
        in_specs=in_specs,
        out_specs=pl.BlockSpec((tm, tn), lambda n, i, kk: (i, n)),
        scratch_shapes=[pltpu.VMEM((tm, tn), F32)] if nk > 1 else [],
        compiler_params=_cparams("parallel", "parallel", "arbitrary"),
        name=name,
    )(*args)


def _interleaved_block(k, n_blocks):
    half = n_blocks // 2
    if isinstance(k, int):
        return 2 * k if k < half else 2 * (k - half) + 1
    return jnp.where(k < half, 2 * k, 2 * (k - half) + 1)


def _mm_nt(a, w, *, out_dtype, tm, tn, tko, name, deps=(), interleaved=0):
    m, n = a.shape
    assert n == w.cols
    tm = _pick(m, tm, 16)
    if w.kind == "col" and w.s > 1:
        tko = _pick(w.r, tko, 128)
        layer, nsh, width = w.layer, w.s, w.c
        blk = interleaved or width
        per = width // blk

        def body(a_ref, w_ref, *rest):
            o_ref = rest[len(deps)]
            acc = None
            for k in range(nsh * per):
                s, j = divmod(k, per)
                at = (_interleaved_block(k, nsh * per) if interleaved else k) * blk
                part = lax.dot_general(a_ref[:, at:at + blk], w_ref[s, :, j * blk:(j + 1) * blk], (((1,), (1,)), ((), ())),
                                       preferred_element_type=F32)
                acc = part if acc is None else acc + part
            o_ref[...] = acc.astype(o_ref.dtype)

        return pl.pallas_call(
            body,
            out_shape=jax.ShapeDtypeStruct((m, w.rows), out_dtype),
            grid=(w.rows // tko, m // tm),
            in_specs=[pl.BlockSpec((tm, n), lambda ko, i: (i, 0)),
                      pl.BlockSpec((nsh, None, tko, width), lambda ko, i: (0, layer, ko, 0))]
            + [pl.BlockSpec(memory_space=pl.ANY)] * len(deps),
            out_specs=pl.BlockSpec((tm, tko), lambda ko, i: (i, ko)),
            compiler_params=_cparams("parallel", "parallel"),
            name=name,
        )(a, w.arr, *deps)
    tko, tn = w.tiles(tko, tn)
    nk = n // tn
    widx = w.index(tko, tn)
    return pl.pallas_call(
        _mm_body(((1,), (1,)), nk, False, unused_in=len(deps)),
        out_shape=jax.ShapeDtypeStruct((m, w.rows), out_dtype),
        grid=(w.rows // tko, m // tm, nk),
        in_specs=[pl.BlockSpec((tm, tn), lambda ko, i, nn: (i, nn)),
                  pl.BlockSpec((None, None, tko, tn), lambda ko, i, nn: widx(ko, nn))]
        + [pl.BlockSpec(memory_space=pl.ANY)] * len(deps),
        out_specs=pl.BlockSpec((tm, tko), lambda ko, i, nn: (i, ko)),
        scratch_shapes=[pltpu.VMEM((tm, tko), F32)] if nk > 1 else [],
        compiler_params=_cparams("parallel", "parallel", "arbitrary"),
        name=name,
    )(a, w.arr, *deps)


def _mm_tn(a, c, w, *, into, tm, tn, tko, name, interleaved=False):
    m, k = a.shape
    tm = _pick(m, tm, 16)
    layer = w.layer
    mc, n = c.shape
    assert mc == m and k == w.rows and n == w.cols
    tko, tn = w.tiles(tko, tn)
    nk = m // tm
    oidx = _shard_index(w.kind, w.r, w.c, tko, tn)
    n_blocks = n // tn
    c_block = (lambda nn: _interleaved_block(nn, n_blocks)) if interleaved else (lambda nn: nn)
    in_specs = [pl.BlockSpec((tm, tko), lambda ko, nn, mm: (mm, ko)),
                pl.BlockSpec((tm, tn), lambda ko, nn, mm: (mm, c_block(nn)))]
    args = [a, c]
    if into is not None:
        in_specs.append(pl.BlockSpec(memory_space=pl.ANY))
        args.append(into)
    return pl.pallas_call(
        _mm_body(((0,), (0,)), nk, False, unused_in=len(args) - 2),
        out_shape=jax.ShapeDtypeStruct((w.arr.shape[1], w.s, w.r, w.c), F32),
        grid=(k // tko, n // tn, nk),
        in_specs=in_specs,
        out_specs=pl.BlockSpec((None, None, tko, tn), lambda ko, nn, mm: (layer,) + oidx(ko, nn)),
        scratch_shapes=[pltpu.VMEM((tko, tn), F32)] if nk > 1 else [],
        input_output_aliases={2: 0} if into is not None else {},
        compiler_params=_cparams("parallel", "parallel", "arbitrary"),
        name=name,
    )(*args)


def _mxu_sum(x, ones):
    hi = x.astype(BF16)
    lo = (x - hi.astype(F32)).astype(BF16)
    return jnp.dot(hi, ones, preferred_element_type=F32) + jnp.dot(lo, ones, preferred_element_type=F32)


def _head_rms(x, gain, head_ones):
    r = lax.rsqrt(_mxu_sum(x * x, head_ones) * (1.0 / HEAD_DIM) + RMS_EPS)
    return x * r * gain, r


def _head_rms_bwd(x, r, gain, dy, head_ones, fold):
    t = dy * gain
    dx = r * t - x * (r * r * r) * (_mxu_sum(t * x, head_ones) * (1.0 / HEAD_DIM))
    dg = jnp.broadcast_to(jnp.sum(dy * x * r, axis=0, keepdims=True), (8, x.shape[1]))
    return dx, _mxu_sum(dg, fold)[0:1, :HEAD_DIM]


def _attn_consts(n_q, n_kv, sinks, qg, kg):
    group = n_q // n_kv
    t = jnp.arange(WINDOW, dtype=jnp.int32)[:, None]
    s = jnp.arange(2 * WINDOW, dtype=jnp.int32)[None, :] - WINDOW
    dist = (t - s).astype(F32)
    valid = (dist >= 0) & (dist < WINDOW)
    slopes = jnp.exp2(-8.0 * jnp.arange(1, n_q + 1, dtype=F32) / n_q)
    bias = jnp.where(valid[None], -slopes[:, None, None] * dist[None], NEG_BIG)
    sink = jnp.broadcast_to(sinks.astype(F32).reshape(n_kv, group, 1, 1), (n_kv, group, WINDOW, 128))
    head_ones = lambda h: jnp.kron(jnp.eye(h, dtype=F32), jnp.ones((HEAD_DIM, HEAD_DIM), F32)).astype(BF16)
    fold = lambda h: jnp.tile(jnp.eye(HEAD_DIM, 128, dtype=F32), (h, 1)).astype(BF16)
    return dict(
        bias=bias.reshape(n_kv, group * WINDOW, 2 * WINDOW),
        sink=sink.reshape(n_kv, group * WINDOW, 128),
        qg=jnp.tile(qg, (1, n_q)), kg=jnp.tile(kg, (1, n_kv)),
        q_ones=head_ones(n_q), k_ones=head_ones(n_kv),
        q_fold=fold(n_q), k_fold=fold(n_kv),
        key_ones=jnp.ones((2 * WINDOW, 128), BF16))


_ATTN_CONST_ORDER = ("qg", "kg", "sink", "bias", "q_ones", "k_ones", "q_fold", "k_fold", "key_ones")


def _attn_inputs(q_ref, kc_ref, kp_ref, vc_ref, vp_ref, c):
    q = q_ref[...]
    k2 = jnp.concatenate([kp_ref[...], kc_ref[...]], axis=0)
    v2 = jnp.concatenate([vp_ref[...], vc_ref[...]], axis=0)
    qn, rq = _head_rms(q, c["qg"][...], c["q_ones"][...])
    kn, rk = _head_rms(k2, c["kg"][...], c["k_ones"][...])
    return dict(q=q, rq=rq, qn=qn.astype(BF16), k2=k2, rk=rk, kn=kn.astype(BF16), v2=v2.astype(BF16))


def _attn_probs(x, c, first_mask, kv, group):
    sl = slice(kv * HEAD_DIM, (kv + 1) * HEAD_DIM)
    k2b, v2b = x["kn"][:, sl], x["v2"][:, sl]
    qs = jnp.concatenate([x["qn"][:, (kv * group + g) * HEAD_DIM:(kv * group + g + 1) * HEAD_DIM]
                          for g in range(group)], axis=0)
    s = lax.dot_general(qs, k2b, (((1,), (1,)), ((), ())), preferred_element_type=F32) * (HEAD_DIM ** -0.5)
    s = jnp.where(first_mask, NEG_BIG, s + c["bias"][kv])
    sink = c["sink"][kv]
    m = jnp.maximum(jnp.max(s, axis=-1, keepdims=True), sink)
    twice = lambda a: jnp.concatenate([a, a], axis=1)
    p = jnp.exp(s - twice(m))
    esink = jnp.exp(sink - m)
    inv = 1.0 / (_mxu_sum(p, c["key_ones"][...]) + esink)
    return dict(k2b=k2b, v2b=v2b, qs=qs, pn=p * twice(inv), psink=esink * inv, twice=twice)


def _attn_specs(n_q, n_kv):
    aw, kvw = n_q * HEAD_DIM, n_kv * HEAD_DIM
    group = n_q // n_kv
    kblk, vblk = aw // kvw, aw // kvw + 1

    def specs(nb):
        cur = lambda n: jnp.minimum(n, nb - 1)
        prev = lambda n: jnp.maximum(jnp.minimum(n, nb - 1) - 1, 0)
        return [
            pl.BlockSpec((WINDOW, aw), lambda n: (cur(n), 0)),
            pl.BlockSpec((WINDOW, kvw), lambda n: (cur(n), kblk)),
            pl.BlockSpec((WINDOW, kvw), lambda n: (prev(n), kblk)),
            pl.BlockSpec((WINDOW, kvw), lambda n: (cur(n), vblk)),
            pl.BlockSpec((WINDOW, kvw), lambda n: (prev(n), vblk)),
        ]
    whole = lambda shape: pl.BlockSpec(shape, lambda n: (0,) * len(shape))
    return specs, whole


def _attn_fwd(z, qg, kg, sinks, *, n_q, n_kv, name, deps=()):
    L = z.shape[0]
    nb = L // WINDOW
    aw = n_q * HEAD_DIM
    group = n_q // n_kv
    consts = _attn_consts(n_q, n_kv, sinks, qg, kg)
    specs, whole = _attn_specs(n_q, n_kv)
    nc = len(_ATTN_CONST_ORDER)

    def body(q_ref, kc_ref, kp_ref, vc_ref, vp_ref, *rest):
        c = dict(zip(_ATTN_CONST_ORDER, rest[:nc]))
        o_ref = rest[-1]
        n = pl.program_id(0)
        col = lax.broadcasted_iota(jnp.int32, (group * WINDOW, 2 * WINDOW), 1)
        first_mask = jnp.logical_and(n == 0, col < WINDOW)
        x = _attn_inputs(q_ref, kc_ref, kp_ref, vc_ref, vp_ref, c)
        for kv in range(n_kv):
            a = _attn_probs(x, c, first_mask, kv, group)
            o = jnp.dot(a["pn"].astype(BF16), a["v2b"], preferred_element_type=F32)
            for g in range(group):
                h = kv * group + g
                o_ref[:, h * HEAD_DIM:(h + 1) * HEAD_DIM] = o[g * WINDOW:(g + 1) * WINDOW].astype(o_ref.dtype)

    return pl.pallas_call(
        body,
        out_shape=jax.ShapeDtypeStruct((L, aw), BF16),
        grid=(nb,),
        in_specs=specs(nb) + [whole(consts[k].shape) for k in _ATTN_CONST_ORDER]
        + [pl.BlockSpec(memory_space=pl.ANY)] * len(deps),
        out_specs=pl.BlockSpec((WINDOW, aw), lambda n: (n, 0)),
        compiler_params=_cparams("parallel"),
        name=name,
    )(z, z, z, z, z, *[consts[k] for k in _ATTN_CONST_ORDER], *deps)


def _attn_bwd(z, do, qg, kg, sinks, *, n_q, n_kv, name):
    L = z.shape[0]
    nb = L // WINDOW
    aw, kvw = n_q * HEAD_DIM, n_kv * HEAD_DIM
    group = n_q // n_kv
    consts = _attn_consts(n_q, n_kv, sinks, qg, kg)
    specs, whole = _attn_specs(n_q, n_kv)
    scale = HEAD_DIM ** -0.5
    nc = len(_ATTN_CONST_ORDER)

    def body(q_ref, kc_ref, kp_ref, vc_ref, vp_ref, do_ref, *rest):
        c = dict(zip(_ATTN_CONST_ORDER, rest[:nc]))
        dq_ref, dkv_ref, dqg_ref, dkg_ref, dsink_ref, carry_ref, dqn_ref, dkn_ref, dv_ref = rest[nc:]
        n = pl.program_id(0)

        @pl.when(n == 0)
        def _():
            dqg_ref[...] = jnp.zeros_like(dqg_ref)
            dkg_ref[...] = jnp.zeros_like(dkg_ref)
            dsink_ref[...] = jnp.zeros_like(dsink_ref)
            carry_ref[...] = jnp.zeros_like(carry_ref)

        @pl.when(n < nb)
        def _():
            col = lax.broadcasted_iota(jnp.int32, (group * WINDOW, 2 * WINDOW), 1)
            first_mask = jnp.logical_and(n == 0, col < WINDOW)
            head_lane = lax.broadcasted_iota(jnp.int32, (1, n_q), 1)
            x = _attn_inputs(q_ref, kc_ref, kp_ref, vc_ref, vp_ref, c)
            dsink = jnp.zeros((1, n_q), F32)
            for kv in range(n_kv):
                a = _attn_probs(x, c, first_mask, kv, group)
                pn = a["pn"]
                dos = jnp.concatenate(
                    [do_ref[:, (kv * group + g) * HEAD_DIM:(kv * group + g + 1) * HEAD_DIM] for g in range(group)],
                    axis=0).astype(BF16)
                dpn = lax.dot_general(dos, a["v2b"], (((1,), (1,)), ((), ())), preferred_element_type=F32)
                ksl = slice(kv * HEAD_DIM, (kv + 1) * HEAD_DIM)
                dv_ref[:, ksl] = lax.dot_general(pn.astype(BF16), dos, (((0,), (0,)), ((), ())), preferred_element_type=F32)
                delta = _mxu_sum(pn * dpn, c["key_ones"][...])
                ds = (pn * (dpn - a["twice"](delta))).astype(BF16)
                dsk = -a["psink"] * delta
                dqn = lax.dot_general(ds, a["k2b"], (((1,), (0,)), ((), ())), preferred_element_type=F32) * scale
                dkn_ref[:, ksl] = lax.dot_general(ds, a["qs"], (((0,), (0,)), ((), ())), preferred_element_type=F32) * scale
                for g in range(group):
                    h = kv * group + g
                    rows = slice(g * WINDOW, (g + 1) * WINDOW)
                    dqn_ref[:, h * HEAD_DIM:(h + 1) * HEAD_DIM] = dqn[rows]
                    dsink = dsink + jnp.where(head_lane == h, jnp.sum(dsk[rows], axis=0, keepdims=True)[:, :n_q], 0.0)
            dq, dqg = _head_rms_bwd(x["q"], x["rq"], c["qg"][...], dqn_ref[...], c["q_ones"][...], c["q_fold"][...])
            dk2, dkg = _head_rms_bwd(x["k2"], x["rk"], c["kg"][...], dkn_ref[...], c["k_ones"][...], c["k_fold"][...])
            dq_ref[...] = dq.astype(dq_ref.dtype)
            dkv_ref[:, :kvw] = (carry_ref[:, :kvw] + dk2[:WINDOW]).astype(dkv_ref.dtype)
            dkv_ref[:, kvw:] = (carry_ref[:, kvw:] + dv_ref[:WINDOW, :]).astype(dkv_ref.dtype)
            carry_ref[:, :kvw] = dk2[WINDOW:]
            carry_ref[:, kvw:] = dv_ref[WINDOW:, :]
            dqg_ref[...] += dqg
            dkg_ref[...] += dkg
            dsink_ref[...] += dsink

        @pl.when(n == nb)
        def _():
            dkv_ref[...] = carry_ref[...].astype(dkv_ref.dtype)

    in_specs = (specs(nb) + [pl.BlockSpec((WINDOW, aw), lambda n: (jnp.minimum(n, nb - 1), 0))]
                + [whole(consts[k].shape) for k in _ATTN_CONST_ORDER])
    return pl.pallas_call(
        body,
        out_shape=(jax.ShapeDtypeStruct((L, aw), BF16), jax.ShapeDtypeStruct((L, 2 * kvw), BF16),
                   jax.ShapeDtypeStruct((1, HEAD_DIM), F32), jax.ShapeDtypeStruct((1, HEAD_DIM), F32),
                   jax.ShapeDtypeStruct((1, n_q), F32)),
        grid=(nb + 1,),
        in_specs=in_specs,
        out_specs=(pl.BlockSpec((WINDOW, aw), lambda n: (jnp.minimum(n, nb - 1), 0)),
                   pl.BlockSpec((WINDOW, 2 * kvw), lambda n: (jnp.maximum(n - 1, 0), 0)),
                   pl.BlockSpec((1, HEAD_DIM), lambda n: (0, 0)),
                   pl.BlockSpec((1, HEAD_DIM), lambda n: (0, 0)),
                   pl.BlockSpec((1, n_q), lambda n: (0, 0))),
        scratch_shapes=[pltpu.VMEM((WINDOW, 2 * kvw), F32), pltpu.VMEM((WINDOW, aw), F32),
                        pltpu.VMEM((2 * WINDOW, kvw), F32), pltpu.VMEM((2 * WINDOW, kvw), F32)],
        compiler_params=_cparams("arbitrary"),
        name=name,
    )(z, z, z, z, z, do, *[consts[k] for k in _ATTN_CONST_ORDER])


def _cmul(ar, ai, br, bi):
    return ar * br - ai * bi, ar * bi + ai * br


def _time_permutation(tc):
    r = jnp.arange(tc)
    src = (r % 8) * (tc // 8) + r // 8
    p = (src[:, None] == jnp.arange(tc)[None, :]).astype(BF16)
    return p, p.T


def _unpermute(pt, x):
    hi = x.astype(BF16)
    lo = (x - hi.astype(F32)).astype(BF16)
    return jnp.dot(pt, hi, preferred_element_type=F32) + jnp.dot(pt, lo, preferred_element_type=F32)


def _segment_scan(xr_ref, xi_ref, ar, ai, cr, ci, ng, reverse):
    n = ar.shape[-1]
    row = lax.broadcasted_iota(jnp.int32, (8, n), 0)
    seeded = 7 if reverse else 0
    a8r = jnp.broadcast_to(ar, (8, n))
    a8i = jnp.broadcast_to(ai, (8, n))
    rows_of = lambda g: pl.ds(pl.multiple_of(((ng - 1 - g) if reverse else g) * 8, 8), 8)

    def recur(g, s):
        rows = rows_of(g)
        sr = a8r * s[0] - a8i * s[1] + xr_ref[rows, :]
        si = a8r * s[1] + a8i * s[0] + xi_ref[rows, :]
        xr_ref[rows, :] = sr
        xi_ref[rows, :] = si
        return sr, si

    fr, fi = lax.fori_loop(0, ng, recur, (jnp.where(row == seeded, cr, 0.0), jnp.where(row == seeded, ci, 0.0)))
    pr, pi = ar, ai
    for _ in range(ng.bit_length() - 1):
        pr, pi = _cmul(pr, pi, pr, pi)
    for k in (1, 2, 4):
        keep = (row < 8 - k) if reverse else (row >= k)
        shift = (8 - k) if reverse else k
        mr, mi = jnp.where(keep, pr, 0.0), jnp.where(keep, pi, 0.0)
        tr, ti = pltpu.roll(fr, shift, 0), pltpu.roll(fi, shift, 0)
        fr, fi = fr + mr * tr - mi * ti, fi + mr * ti + mi * tr
        pr, pi = _cmul(pr, pi, pr, pi)
    shift = 7 if reverse else 1
    before_r, before_i = pltpu.roll(fr, shift, 0), pltpu.roll(fi, shift, 0)

    def inherit(g, d):
        rows = rows_of(g)
        dr = a8r * d[0] - a8i * d[1]
        di = a8r * d[1] + a8i * d[0]
        xr_ref[rows, :] = xr_ref[rows, :] + dr
        xi_ref[rows, :] = xi_ref[rows, :] + di
        return dr, di

    lax.fori_loop(0, ng, inherit, (jnp.where(row == seeded, 0.0, before_r), jnp.where(row == seeded, 0.0, before_i)))
    out = 0 if reverse else 7
    return (fr[out:out + 1], fi[out:out + 1],
            jnp.where(row == seeded, cr, before_r), jnp.where(row == seeded, ci, before_i))


def _blockdiag(x):
    g, a, b = x.shape
    j = g // SSM_LANE_GROUPS
    eye = jnp.eye(SSM_LANE_GROUPS, dtype=x.dtype)
    y = x.reshape(j, SSM_LANE_GROUPS, a, 1, b) * eye[None, :, None, :, None]
    return y.reshape(j, SSM_LANE_GROUPS * a, SSM_LANE_GROUPS * b)


def _blockdiag_extract(y, a, b):
    j = y.shape[0]
    y = y.reshape(j, SSM_LANE_GROUPS, a, SSM_LANE_GROUPS, b)
    return jnp.einsum("jgahb,gh->jgab", y, jnp.eye(SSM_LANE_GROUPS, dtype=y.dtype)).reshape(j * SSM_LANE_GROUPS, a, b)


def _ssm_disc(lr, li, ldt, brt, bit):
    dt = jnp.exp(ldt)
    mag = jnp.exp(lr * dt)
    ar = mag * jnp.cos(li * dt)
    ai = mag * jnp.sin(li * dt)
    den = lr * lr + li * li
    fr = ((ar - 1.0) * lr + ai * li) / den
    fi = (ai * lr - (ar - 1.0) * li) / den
    bbr = fr[:, None, :] * brt - fi[:, None, :] * bit
    bbi = fr[:, None, :] * bit + fi[:, None, :] * brt
    return ar, ai, bbr, bbi


def _ssm_prep(lr, li, ldt, brt, bit, *, name):
    g, h, p = brt.shape

    def body(lr_ref, li_ref, ldt_ref, brt_ref, bit_ref, ar_ref, ai_ref, bbr_ref, bbi_ref):
        ar, ai, bbr, bbi = _ssm_disc(lr_ref[...], li_ref[...], ldt_ref[...], brt_ref[...], bit_ref[...])
        ar_ref[...] = ar
        ai_ref[...] = ai
        bbr_ref[...] = bbr
        bbi_ref[...] = bbi

    gp = jax.ShapeDtypeStruct((g, p), F32)
    ghp = jax.ShapeDtypeStruct((g, h, p), F32)
    return pl.pallas_call(body, out_shape=(gp, gp, ghp, ghp), name=name)(lr, li, ldt, brt, bit)


def _ssm_prep_bwd(lr, li, ldt, brt, bit, dar, dai, dbbr, dbbi, *, name):
    g, h, p = brt.shape

    def body(lr_ref, li_ref, ldt_ref, brt_ref, bit_ref, dar_ref, dai_ref, dbbr_ref, dbbi_ref,
             dlr_ref, dli_ref, dldt_ref, dbrt_ref, dbit_ref):
        _, vjp = jax.vjp(_ssm_disc, lr_ref[...], li_ref[...], ldt_ref[...], brt_ref[...], bit_ref[...])
        dlr, dli, dldt, dbrt, dbit = vjp((dar_ref[...], dai_ref[...], dbbr_ref[...], dbbi_ref[...]))
        dlr_ref[...] = dlr
        dli_ref[...] = dli
        dldt_ref[...] = dldt
        dbrt_ref[...] = dbrt
        dbit_ref[...] = dbit

    gp = jax.ShapeDtypeStruct((g, p), F32)
    ghp = jax.ShapeDtypeStruct((g, h, p), F32)
    return pl.pallas_call(body, out_shape=(gp, gp, jax.ShapeDtypeStruct((g, 1), F32), ghp, ghp), name=name)(
        lr, li, ldt, brt, bit, dar, dai, dbbr, dbbi)


def _ssm_specs(tc, nlanes, nch, u_colblk, chunk_of):
    return [
        pl.BlockSpec((tc, nch), lambda j, c: (chunk_of(c), u_colblk + j)),
        pl.BlockSpec((1, nlanes), lambda j, c: (0, j)),
        pl.BlockSpec((1, nlanes), lambda j, c: (0, j)),
        pl.BlockSpec((None, nch, nlanes), lambda j, c: (j, 0, 0)),
        pl.BlockSpec((None, nch, nlanes), lambda j, c: (j, 0, 0)),
        pl.BlockSpec((None, nlanes, nch), lambda j, c: (j, 0, 0)),
        pl.BlockSpec((None, nlanes, nch), lambda j, c: (j, 0, 0)),
        pl.BlockSpec((1, nch), lambda j, c: (0, j)),
        pl.BlockSpec((tc, tc), lambda j, c: (0, 0)),
        pl.BlockSpec((tc, tc), lambda j, c: (0, 0)),
    ]


def _ssm_fwd(z, ar, ai, bblk_r, bblk_i, cblk_r, cblk_i, d, *, u_col, tc, name, deps=()):
    L = z.shape[0]
    nj, nch, nlanes = bblk_r.shape
    w = nj * nch
    nc = L // tc
    ng = tc // 8

    assert ng & (ng - 1) == 0
    perm, perm_t = _time_permutation(tc)

    def body(u_ref, ar_ref, ai_ref, br_ref, bi_ref, cr_ref, ci_ref, d_ref, p_ref, pt_ref, *rest):
        y_ref, s0r_ref, s0i_ref, xr_ref, xi_ref, carr_ref, cari_ref = rest[len(deps):]
        c = pl.program_id(1)

        @pl.when(c == 0)
        def _():
            carr_ref[...] = jnp.zeros_like(carr_ref)
            cari_ref[...] = jnp.zeros_like(cari_ref)

        s0r_ref[...] = carr_ref[...]
        s0i_ref[...] = cari_ref[...]
        u = u_ref[...]
        ub = jnp.dot(p_ref[...], u.astype(BF16), preferred_element_type=F32).astype(BF16)
        xr_ref[...] = jnp.dot(ub, br_ref[...].astype(BF16), preferred_element_type=F32)
        xi_ref[...] = jnp.dot(ub, bi_ref[...].astype(BF16), preferred_element_type=F32)
        cr, ci, _, _ = _segment_scan(xr_ref, xi_ref, ar_ref[...], ai_ref[...], carr_ref[...], cari_ref[...], ng, False)
        carr_ref[...] = cr
        cari_ref[...] = ci
        y = (jnp.dot(xr_ref[...].astype(BF16), cr_ref[...].astype(BF16), preferred_element_type=F32)
             - jnp.dot(xi_ref[...].astype(BF16), ci_ref[...].astype(BF16), preferred_element_type=F32))
        y_ref[...] = _unpermute(pt_ref[...], y) + d_ref[...] * u

    state = jax.ShapeDtypeStruct((nc, 1, nj * nlanes), F32)
    state_spec = pl.BlockSpec((None, 1, nlanes), lambda j, c: (c, 0, j))
    return pl.pallas_call(
        body,
        out_shape=(jax.ShapeDtypeStruct((L, w), F32), state, state),
        grid=(nj, nc),
        in_specs=_ssm_specs(tc, nlanes, nch, u_col // nch, lambda c: c) + [pl.BlockSpec(memory_space=pl.ANY)] * len(deps),
        out_specs=(pl.BlockSpec((tc, nch), lambda j, c: (c, j)), state_spec, state_spec),
        scratch_shapes=[pltpu.VMEM((tc, nlanes), F32), pltpu.VMEM((tc, nlanes), F32),
                        pltpu.VMEM((1, nlanes), F32), pltpu.VMEM((1, nlanes), F32)],
        compiler_params=_cparams("parallel", "arbitrary"),
        name=name,
    )(z, ar, ai, bblk_r, bblk_i, cblk_r, cblk_i, d, perm, perm_t, *deps)


def _ssm_bwd(z, dy, s0r, s0i, ar, ai, bblk_r, bblk_i, cblk_r, cblk_i, d, *, u_col, tc, name):
    L = z.shape[0]
    nj, nch, nlanes = bblk_r.shape
    w = nj * nch
    nc = L // tc
    ng = tc // 8
    chunk_of = lambda c: nc - 1 - c
    assert ng & (ng - 1) == 0
    perm, perm_t = _time_permutation(tc)

    def body(u_ref, ar_ref, ai_ref, br_ref, bi_ref, cr_ref, ci_ref, d_ref, p_ref, pt_ref, dy_ref, s0r_ref, s0i_ref,
             du_ref, dbr_ref, dbi_ref, dcr_ref, dci_ref, dar_ref, dai_ref, dd_ref,
             sr_ref, si_ref, lr_ref, li_ref, carr_ref, cari_ref):
        c = pl.program_id(1)

        @pl.when(c == 0)
        def _():
            for ref in (dbr_ref, dbi_ref, dcr_ref, dci_ref, dar_ref, dai_ref, dd_ref, carr_ref, cari_ref):
                ref[...] = jnp.zeros_like(ref)

        u = u_ref[...]
        dyv = dy_ref[...]
        both = jnp.dot(p_ref[...], jnp.concatenate([u.astype(BF16), dyv.astype(BF16)], axis=1), preferred_element_type=F32)
        ub = both[:, :nch].astype(BF16)
        dyb = both[:, nch:].astype(BF16)
        brb = br_ref[...].astype(BF16)
        bib = bi_ref[...].astype(BF16)
        crb = cr_ref[...].astype(BF16)
        cib = ci_ref[...].astype(BF16)
        a_r, a_i = ar_ref[...], ai_ref[...]

        sr_ref[...] = jnp.dot(ub, brb, preferred_element_type=F32)
        si_ref[...] = jnp.dot(ub, bib, preferred_element_type=F32)
        _, _, start_r, start_i = _segment_scan(sr_ref, si_ref, a_r, a_i, s0r_ref[...], s0i_ref[...], ng, False)

        nt = (((1,), (1,)), ((), ()))
        lr_ref[...] = lax.dot_general(dyb, crb, nt, preferred_element_type=F32)
        li_ref[...] = -lax.dot_general(dyb, cib, nt, preferred_element_type=F32)
        cr, ci, _, _ = _segment_scan(lr_ref, li_ref, a_r, -a_i, carr_ref[...], cari_ref[...], ng, True)
        carr_ref[...] = cr
        cari_ref[...] = ci

        def accumulate(g, carry):
            pr, pi, acc_r, acc_i = carry
            rows = pl.ds(pl.multiple_of(g * 8, 8), 8)
            lr, li = lr_ref[rows, :], li_ref[rows, :]
            return sr_ref[rows, :], si_ref[rows, :], acc_r + lr * pr + li * pi, acc_i + li * pr - lr * pi

        zero8 = jnp.zeros((8, nlanes), F32)
        _, _, acc_r, acc_i = lax.fori_loop(0, ng, accumulate, (start_r, start_i, zero8, zero8))
        dar_ref[...] += jnp.sum(acc_r, axis=0, keepdims=True)
        dai_ref[...] += jnp.sum(acc_i, axis=0, keepdims=True)

        tn = (((0,), (0,)), ((), ()))
        lrb = lr_ref[...].astype(BF16)
        lib = li_ref[...].astype(BF16)
        dcr_ref[...] += lax.dot_general(sr_ref[...].astype(BF16), dyb, tn, preferred_element_type=F32)
        dci_ref[...] -= lax.dot_general(si_ref[...].astype(BF16), dyb, tn, preferred_element_type=F32)
        dbr_ref[...] += lax.dot_general(ub, lrb, tn, preferred_element_type=F32)
        dbi_ref[...] += lax.dot_general(ub, lib, tn, preferred_element_type=F32)
        du = (lax.dot_general(lrb, brb, nt, preferred_element_type=F32)
              + lax.dot_general(lib, bib, nt, preferred_element_type=F32))
        du_ref[...] = (_unpermute(pt_ref[...], du) + d_ref[...] * dyv).astype(du_ref.dtype)
        dd_ref[...] += jnp.sum(dyv * u, axis=0, keepdims=True)

    state_spec = pl.BlockSpec((None, 1, nlanes), lambda j, c: (chunk_of(c), 0, j))
    bshape = jax.ShapeDtypeStruct((nj, nch, nlanes), F32)
    cshape = jax.ShapeDtypeStruct((nj, nlanes, nch), F32)
    ashape = jax.ShapeDtypeStruct((1, nj * nlanes), F32)
    bspec = pl.BlockSpec((None, nch, nlanes), lambda j, c: (j, 0, 0))
    cspec = pl.BlockSpec((None, nlanes, nch), lambda j, c: (j, 0, 0))
    aspec = pl.BlockSpec((1, nlanes), lambda j, c: (0, j))
    big = pltpu.VMEM((tc, nlanes), F32)
    return pl.pallas_call(
        body,
        out_shape=(jax.ShapeDtypeStruct((L, w), BF16), bshape, bshape, cshape, cshape, ashape, ashape,
                   jax.ShapeDtypeStruct((1, w), F32)),
        grid=(nj, nc),
        in_specs=_ssm_specs(tc, nlanes, nch, u_col // nch, chunk_of)
        + [pl.BlockSpec((tc, nch), lambda j, c: (chunk_of(c), j)), state_spec, state_spec],
        out_specs=(pl.BlockSpec((tc, nch), lambda j, c: (chunk_of(c), j)), bspec, bspec, cspec, cspec, aspec, aspec,
                   pl.BlockSpec((1, nch), lambda j, c: (0, j))),
        scratch_shapes=[big, big, big, big, pltpu.VMEM((1, nlanes), F32), pltpu.VMEM((1, nlanes), F32)],
        compiler_params=_cparams("parallel", "arbitrary"),
        name=name,
    )(z, ar, ai, bblk_r, bblk_i, cblk_r, cblk_i, d, perm, perm_t, dy, s0r, s0i)


def _rmsnorm_rows(x, g):
    return x * lax.rsqrt(jnp.mean(x * x, axis=-1, keepdims=True) + RMS_EPS) * g


def _glu_out(y_raw, pre, b):
    yg = jax.nn.gelu(y_raw)
    return yg * jax.nn.sigmoid(pre + b)


def _gate_merge(za, zs, ba, bs, a, bm):
    return jax.nn.sigmoid(za + ba) * a + jax.nn.sigmoid(zs + bs) * bm


def _swiglu(g, u):
    return jax.nn.silu(g) * u


def _row_tile(width_bytes_per_row, rows):
    budget = VMEM_LIMIT_BYTES // 3
    t = max(8, min(1024, budget // (2 * max(width_bytes_per_row, 1))))
    return _pick(rows, t, 16)


def _ssm_params(p, prefix):
    g, pst = p["lam_re"].shape
    ar, ai, bbr, bbi = _ssm_prep(p["lam_re"], p["lam_im"], p["log_dt"], p["b_re_t"], p["b_im_t"], name=prefix + "_ssm_prep")
    return dict(ar=ar.reshape(1, g * pst), ai=ai.reshape(1, g * pst),
                bblk_r=_blockdiag(bbr), bblk_i=_blockdiag(bbi),
                cblk_r=_blockdiag(jnp.swapaxes(p["c_re"], 1, 2)), cblk_i=_blockdiag(jnp.swapaxes(p["c_im"], 1, 2)))


def _layer_fwd(x, p, dims, prefix, deps=(), tick=None):
    tick = tick or (lambda point, arr: ())
    t, d = x.shape
    aw, kvw, sw, ff = dims["aw"], dims["kvw"], dims["sw"], dims["ff"]
    off_u = aw + 2 * kvw
    off_g = off_u + sw
    gblk = _pick(d, 512, 128)
    assert off_g % gblk == 0 and off_u % (SSM_LANE_GROUPS * SSM_GROUP_CH) == 0
    sv = {"x": x}

    h, = _rowmap(_rmsnorm_rows, [(x, "row", d, 0), (p["norm_mix_g"], "vec", d, 0)], [(d, BF16, "row", d)],
                 rows=t, tm=_row_tile(6 * d, t), name=prefix + "_norm_mix", deps=deps)
    deps = tick("norm", h)
    z = _mm_nn(h, p["w_in"], out_dtype=F32, tm=1024, tn=1664, tk=2048, name=prefix + "_mm_in", deps=deps)
    ya = _attn_fwd(z, p["q_norm_g"], p["k_norm_g"], p["attn_sinks"], n_q=dims["n_q"], n_kv=dims["n_kv"],
                   name=prefix + "_attn_fwd", deps=tick("in", z))
    sp = _ssm_params(p, prefix)
    y_raw, s0r, s0i = _ssm_fwd(z, sp["ar"], sp["ai"], sp["bblk_r"], sp["bblk_i"], sp["cblk_r"], sp["cblk_i"], p["ssm_d"],
                               u_col=off_u, tc=dims["tc"], name=prefix + "_ssm_fwd", deps=tick("attn", ya))
    yg, = _rowmap(jax.nn.gelu, [(y_raw, "row", sw, 0)], [(sw, BF16, "row", sw)],
                  rows=t, tm=_row_tile(6 * sw, t), name=prefix + "_gelu", deps=tick("ssm", y_raw))
    pre = _mm_nn(yg, p["ssm_glu_w"], out_dtype=F32, tm=1024, tn=1024, tk=1024, name=prefix + "_mm_glu")
    y2, = _rowmap(_glu_out, [(y_raw, "row", sw, 0), (pre, "row", sw, 0), (p["ssm_glu_b"], "vec", sw, 0)],
                  [(sw, BF16, "row", sw)], rows=t, tm=_row_tile(10 * sw, t), name=prefix + "_glu_out")
    a = _mm_nn(ya, p["w_attn_branch"], out_dtype=BF16, tm=1024, tn=512, tk=1024, name=prefix + "_mm_ab")
    bm = _mm_nn(y2, p["w_ssm_branch"], out_dtype=BF16, tm=1024, tn=512, tk=1024, name=prefix + "_mm_sb")
    ngb = d // gblk
    merged, = _rowmap(
        _gate_merge,
        [(z, "row", gblk, off_g // gblk), (z, "row", gblk, off_g // gblk + ngb),
         (p["gate_bias"], "vec", gblk, 0), (p["gate_bias"], "vec", gblk, ngb),
         (a, "row", gblk, 0), (bm, "row", gblk, 0)],
        [(d, BF16, "row", gblk)], rows=t, tm=_row_tile(18 * gblk, t), ncol=ngb, name=prefix + "_gate")
    x1 = _mm_nn(merged, p["w_out"], out_dtype=F32, tm=512, tn=1024, tk=2048, name=prefix + "_mm_out", add=x)
    h2, = _rowmap(_rmsnorm_rows, [(x1, "row", d, 0), (p["norm_ffn_g"], "vec", d, 0)], [(d, BF16, "row", d)],
                  rows=t, tm=_row_tile(6 * d, t), name=prefix + "_norm_ffn", deps=tick("out", x1))
    gu = _mm_nn(h2, p["w_ffn_in"], out_dtype=BF16, tm=1024, tn=1408, tk=2048, name=prefix + "_mm_ffn_in")
    fblk = _pick(ff, 1408, 128)
    nfb = ff // fblk
    act, = _rowmap(_swiglu, [(gu, "row", fblk, 0), (gu, "row", fblk, nfb)], [(ff, BF16, "row", fblk)],
                   rows=t, tm=_row_tile(10 * fblk, t), ncol=nfb, name=prefix + "_swiglu", deps=tick("ffn_in", gu))
    x2 = _mm_nn(act, p["w_ffn_out"], out_dtype=F32, tm=512, tn=512, tk=5632, name=prefix + "_mm_ffn_out", add=x1,
                deps=tick("act", act))
    sv.update(h=h, z=z, ya=ya, sp=sp, y_raw=y_raw, s0r=s0r, s0i=s0i, yg=yg, pre=pre, y2=y2, a=a, bm=bm,
              merged=merged, x1=x1, h2=h2, gu=gu, act=act)
    return x2, sv


def _layer_bwd(dx2, dx2b, sv, p, dims, prefix, gbuf, deps=(), before_mixer=None, before_in=None):
    t, d = dx2.shape
    aw, kvw, sw, ff = dims["aw"], dims["kvw"], dims["sw"], dims["ff"]
    off_u = aw + 2 * kvw
    off_g = off_u + sw
    gblk = _pick(d, 512, 128)
    ngb = d // gblk
    fblk = _pick(ff, 1408, 128)
    nfb = ff // fblk
    g = {}

    dact = _mm_nt(dx2b, p["w_ffn_out"], out_dtype=BF16, tm=512, tn=2048, tko=1408, name=prefix + "_mm_dact", deps=deps)
    g["w_ffn_out"] = _mm_tn(sv["act"], dx2b, p["w_ffn_out"], into=gbuf.get("w_ffn_out"), tm=4096, tn=1024, tko=512,
                            name=prefix + "_mm_dw_ffn_out")

    def swiglu_bwd(gg, uu, da):
        s = jax.nn.sigmoid(gg)
        gs = gg * s
        return jnp.concatenate([da * uu * (s + gs * (1.0 - s)), da * gs], axis=1)

    dgu, = _rowmap(swiglu_bwd, [(sv["gu"], "row", fblk, 0), (sv["gu"], "row", fblk, nfb), (dact, "row", fblk, 0)],
                   [(2 * ff, BF16, "row", 2 * fblk)],
                   rows=t, tm=_row_tile(16 * fblk, t), ncol=nfb, name=prefix + "_swiglu_bwd")
    dh2 = _mm_nt(dgu, p["w_ffn_in"], out_dtype=F32, tm=256, tn=1408, tko=512, name=prefix + "_mm_dh2", interleaved=fblk)
    g["w_ffn_in"] = _mm_tn(sv["h2"], dgu, p["w_ffn_in"], into=gbuf.get("w_ffn_in"), tm=4096, tn=fblk, tko=512,
                           name=prefix + "_mm_dw_ffn_in", interleaved=True)

    def norm_bwd(xx, gg, dh, dres):
        _, vjp = jax.vjp(_rmsnorm_rows, xx, gg)
        dxx, dgg = vjp(dh)
        dxx = dxx + dres
        return dxx, dxx, dgg

    dx1, dx1b, g["norm_ffn_g"] = _rowmap(
        norm_bwd, [(sv["x1"], "row", d, 0), (p["norm_ffn_g"], "vec", d, 0), (dh2, "row", d, 0), (dx2, "row", d, 0)],
        [(d, F32, "row", d), (d, BF16, "row", d), (d, F32, "acc", d)],
        rows=t, tm=_row_tile(22 * d, t), name=prefix + "_norm_ffn_bwd")

    deps = before_mixer(dx1, g) if before_mixer else ()
    dmerged = _mm_nt(dx1b, p["w_out"], out_dtype=BF16, tm=1024, tn=2048, tko=1024, name=prefix + "_mm_dmerged", deps=deps)
    g["w_out"] = _mm_tn(sv["merged"], dx1b, p["w_out"], into=gbuf.get("w_out"), tm=4096, tn=1024, tko=512,
                        name=prefix + "_mm_dw_out")

    def gate_bwd(za, zs, ba, bs, aa, bb, dm):
        sa = jax.nn.sigmoid(za + ba)
        ss = jax.nn.sigmoid(zs + bs)
        daa, dbb = dm * sa, dm * ss
        dza, dzs = daa * aa * (1.0 - sa), dbb * bb * (1.0 - ss)
        return daa, dbb, dza, dzs, jnp.sum(dza, axis=0, keepdims=True), jnp.sum(dzs, axis=0, keepdims=True)

    z = sv["z"]
    da, dbm, dza, dzs, dba, dbs = _rowmap(
        gate_bwd,
        [(z, "row", gblk, off_g // gblk), (z, "row", gblk, off_g // gblk + ngb),
         (p["gate_bias"], "vec", gblk, 0), (p["gate_bias"], "vec", gblk, ngb),
         (sv["a"], "row", gblk, 0), (sv["bm"], "row", gblk, 0), (dmerged, "row", gblk, 0)],
        [(d, BF16, "row", gblk), (d, BF16, "row", gblk), (d, BF16, "row", gblk), (d, BF16, "row", gblk),
         (d, F32, "acc", gblk), (d, F32, "acc", gblk)],
        rows=t, tm=_row_tile(32 * gblk, t), ncol=ngb, name=prefix + "_gate_bwd")
    g["gate_bias"] = jnp.concatenate([dba, dbs], axis=1)
    dya = _mm_nt(da, p["w_attn_branch"], out_dtype=BF16, tm=1024, tn=512, tko=1024, name=prefix + "_mm_dya")
    g["w_attn_branch"] = _mm_tn(sv["ya"], da, p["w_attn_branch"], into=gbuf.get("w_attn_branch"), tm=4096, tn=512,
                                tko=512, name=prefix + "_mm_dw_ab")
    dy2 = _mm_nt(dbm, p["w_ssm_branch"], out_dtype=BF16, tm=1024, tn=512, tko=1024, name=prefix + "_mm_dy2")
    g["w_ssm_branch"] = _mm_tn(sv["y2"], dbm, p["w_ssm_branch"], into=gbuf.get("w_ssm_branch"), tm=4096, tn=512,
                               tko=512, name=prefix + "_mm_dw_sb")

    def glu_bwd(y_raw, pre, b, dy):
        yg = jax.nn.gelu(y_raw)
        _, vjp = jax.vjp(lambda a_, b_, c_: a_ * jax.nn.sigmoid(b_ + c_), yg, pre, b)
        dyg, dpre, db = vjp(dy)
        return dyg, dpre, db

    dyg_direct, dpre, g["ssm_glu_b"] = _rowmap(
        glu_bwd, [(sv["y_raw"], "row", sw, 0), (sv["pre"], "row", sw, 0), (p["ssm_glu_b"], "vec", sw, 0), (dy2, "row", sw, 0)],
        [(sw, F32, "row", sw), (sw, BF16, "row", sw), (sw, F32, "acc", sw)],
        rows=t, tm=_row_tile(24 * sw, t), name=prefix + "_glu_bwd")
    dyg2 = _mm_nt(dpre, p["ssm_glu_w"], out_dtype=F32, tm=1024, tn=1024, tko=1024, name=prefix + "_mm_dyg")
    g["ssm_glu_w"] = _mm_tn(sv["yg"], dpre, p["ssm_glu_w"], into=gbuf.get("ssm_glu_w"), tm=4096, tn=1024, tko=512,
                            name=prefix + "_mm_dw_glu")

    def gelu_bwd(y_raw, d1, d2):
        _, vjp = jax.vjp(jax.nn.gelu, y_raw)
        return vjp(d1 + d2)[0]

    dy_raw, = _rowmap(gelu_bwd, [(sv["y_raw"], "row", sw, 0), (dyg_direct, "row", sw, 0), (dyg2, "row", sw, 0)],
                      [(sw, F32, "row", sw)], rows=t, tm=_row_tile(20 * sw, t), name=prefix + "_gelu_bwd")
    sp = sv["sp"]
    du, dbr, dbi, dcr, dci, dar, dai, g["ssm_d"] = _ssm_bwd(
        z, dy_raw, sv["s0r"], sv["s0i"], sp["ar"], sp["ai"], sp["bblk_r"], sp["bblk_i"], sp["cblk_r"], sp["cblk_i"],
        p["ssm_d"], u_col=off_u, tc=dims["tc"], name=prefix + "_ssm_bwd")
    ngr, pst = p["lam_re"].shape
    hch = SSM_GROUP_CH
    dlr, dli, dldt, dbrt, dbit = _ssm_prep_bwd(
        p["lam_re"], p["lam_im"], p["log_dt"], p["b_re_t"], p["b_im_t"],
        dar.reshape(ngr, pst), dai.reshape(ngr, pst), _blockdiag_extract(dbr, hch, pst), _blockdiag_extract(dbi, hch, pst),
        name=prefix + "_ssm_prep_bwd")
    g.update(ssm_lambda_re=dlr, ssm_lambda_im=dli, ssm_log_dt=dldt.reshape(ngr),
             ssm_b_re=jnp.swapaxes(dbrt, 1, 2), ssm_b_im=jnp.swapaxes(dbit, 1, 2),
             ssm_c_re=jnp.swapaxes(_blockdiag_extract(dcr, pst, hch), 1, 2),
             ssm_c_im=jnp.swapaxes(_blockdiag_extract(dci, pst, hch), 1, 2))

    dq, dkv, g["q_norm_g"], g["k_norm_g"], g["attn_sinks"] = _attn_bwd(
        z, dya, p["q_norm_g"], p["k_norm_g"], p["attn_sinks"], n_q=dims["n_q"], n_kv=dims["n_kv"], name=prefix + "_attn_bwd")

    dz = jnp.concatenate([dq, dkv, du, dza, dzs], axis=1)
    deps = before_in(dq, g) if before_in else ()
    dh = _mm_nt(dz, p["w_in"], out_dtype=F32, tm=512, tn=1664, tko=512, name=prefix + "_mm_dh", deps=deps)
    g["w_in"] = _mm_tn(sv["h"], dz, p["w_in"], into=gbuf.get("w_in"), tm=4096, tn=1664, tko=512,
                       name=prefix + "_mm_dw_in")
    dx, dxb, g["norm_mix_g"] = _rowmap(
        norm_bwd, [(sv["x"], "row", d, 0), (p["norm_mix_g"], "vec", d, 0), (dh, "row", d, 0), (dx1, "row", d, 0)],
        [(d, F32, "row", d), (d, BF16, "row", d), (d, F32, "acc", d)],
        rows=t, tm=_row_tile(22 * d, t), name=prefix + "_norm_mix_bwd")
    return dx, dxb, g


def _loss_and_grad(y, target):
    t, d = y.shape

    def fn(yy, tt):
        e = yy - tt
        dy = e * (1.0 / d)
        return dy, dy, jnp.sum(e * e, keepdims=True).reshape(1, 1)

    dy, dyb, sq = _rowmap(fn, [(y, "row", d, 0), (target, "row", d, 0)],
                          [(d, F32, "row", d), (d, BF16, "row", d), (1, F32, "acc", 1)],
                          rows=t, tm=_row_tile(14 * d, t), name="loss")
    return sq, dy, dyb


def _local_step(x, target, n_layers, weights_of, dims, fwd_tick=None, bwd_tick=None):
    saved, params = [], []
    h = x
    for l in range(n_layers):
        p, deps = weights_of(l, h)
        params.append(p)
        tick = (lambda point, arr, l=l: fwd_tick(l, point, arr)) if fwd_tick else None
        h, sv = _layer_fwd(h, p, dims, "l%d" % l, deps, tick)
        saved.append(sv)
    sq, dy, dyb = _loss_and_grad(h, target)
    grads = [None] * n_layers
    deps = ()
    for l in reversed(range(n_layers)):
        if bwd_tick:
            ffn_done = lambda arr, g, l=l: bwd_tick(l, "ffn", (arr, g["w_ffn_in"], g["w_ffn_out"]),
                                                    {n: g[n] for n in ("w_ffn_in", "w_ffn_out")})
            mixer_done = lambda arr, g, l=l: bwd_tick(l, "mixer", (arr, g["ssm_d"]) + tuple(g[n] for n in MIXER_WEIGHTS),
                                                      {n: g[n] for n in MIXER_WEIGHTS})
        else:
            ffn_done = mixer_done = None
        dy, dyb, grads[l] = _layer_bwd(dy, dyb, saved[l], params[l], dims, "l%d" % l, {}, deps=deps,
                                       before_mixer=ffn_done, before_in=mixer_done)
        if bwd_tick:
            deps = bwd_tick(l, "in", (dy, grads[l]["w_in"]), {"w_in": grads[l]["w_in"]})
    return sq, dy, grads, deps


COL_SHARDED = ("w_in", "w_attn_branch", "w_ssm_branch", "w_ffn_in")
ROW_SHARDED = ("ssm_glu_w", "w_out", "w_ffn_out")
BIG_WEIGHTS = COL_SHARDED + ROW_SHARDED
WEIGHT_NAMES = ("norm_mix_g", "w_in", "gate_bias", "q_norm_g", "k_norm_g", "attn_sinks", "ssm_lambda_re",
                "ssm_lambda_im", "ssm_log_dt", "ssm_b_re", "ssm_b_im", "ssm_c_re", "ssm_c_im", "ssm_d", "ssm_glu_w",
                "ssm_glu_b", "w_attn_branch", "w_ssm_branch", "w_out", "norm_ffn_g", "w_ffn_in", "w_ffn_out")
SMALL_WEIGHTS = tuple(n for n in WEIGHT_NAMES if n not in BIG_WEIGHTS)
MIXER_WEIGHTS = ("w_out", "w_attn_branch", "w_ssm_branch", "ssm_glu_w")
WEIGHT_GROUPS = {"in": ("w_in",), "mixer": MIXER_WEIGHTS, "ffn": ("w_ffn_in", "w_ffn_out")}


def _dims(d, shapes, tc):
    s, _, aw, _ = shapes["w_attn_branch"]
    sw = shapes["w_ssm_branch"][2]
    in_w = shapes["w_in"][3] * s
    kvw = (in_w - aw - sw - 2 * d) // 2
    ff = shapes["w_ffn_out"][2] * s
    return dict(aw=aw, kvw=kvw, sw=sw, ff=ff, n_q=aw // HEAD_DIM, n_kv=kvw // HEAD_DIM, tc=tc)


def _big_params(big):
    p = {n: _Weight(a, 0, "col") for n, a in big.items() if n in COL_SHARDED}
    p.update({n: _Weight(a.reshape(1, 1, -1, a.shape[-1]), 0, "col") for n, a in big.items() if n in ROW_SHARDED})
    return p


def _layer_params(l, small):
    p = {}
    for n in ("norm_mix_g", "gate_bias", "q_norm_g", "k_norm_g", "ssm_d", "ssm_glu_b", "norm_ffn_g"):
        p[n] = small[n][l][None]
    p["attn_sinks"] = small["attn_sinks"][l]
    p["lam_re"] = small["ssm_lambda_re"][l]
    p["lam_im"] = small["ssm_lambda_im"][l]
    p["log_dt"] = small["ssm_log_dt"][l][:, None]
    p["b_re_t"] = jnp.swapaxes(small["ssm_b_re"][l], 1, 2)
    p["b_im_t"] = jnp.swapaxes(small["ssm_b_im"][l], 1, 2)
    p["c_re"] = small["ssm_c_re"][l]
    p["c_im"] = small["ssm_c_im"][l]
    return p


_ANY = pl.BlockSpec(memory_space=pl.ANY)
_MESH_ID = pl.DeviceIdType.MESH


def _coords():
    return lax.axis_index("x"), lax.axis_index("y"), lax.axis_index("c")


def _my_chip():
    return (2 * lax.axis_index("x") + lax.axis_index("y")).astype(jnp.int32).reshape(1)


def _my_core():
    return lax.axis_index("c").astype(jnp.int32).reshape(1)


def _cast_into_slot(w, layer, *, name, deps=()):
    _, r, c = w.shape
    tm = _row_tile(12 * c, r)

    def body(me_ref, w_ref, *rest):
        rest[-1][...] = w_ref[...].astype(rest[-1].dtype)

    return pl.pallas_call(
        body,
        out_shape=jax.ShapeDtypeStruct((N_CHIPS, 1, r, c), BF16),
        grid_spec=pltpu.PrefetchScalarGridSpec(
            num_scalar_prefetch=1,
            grid=(r // tm,),
            in_specs=[pl.BlockSpec((None, tm, c), lambda i, me: (layer, i, 0))]
            + [pl.BlockSpec(memory_space=pl.ANY)] * len(deps),
            out_specs=pl.BlockSpec((None, None, tm, c), lambda i, me: (me[0], 0, i, 0)),
        ),
        compiler_params=_cparams("parallel"),
        name=name,
    )(_my_chip(), w, *deps)


_HBM = pl.BlockSpec(memory_space=pltpu.HBM)
_SEM = pl.BlockSpec(memory_space=pltpu.SEMAPHORE)
_DATAFLOW = pltpu.SideEffectType.DATAFLOW_SIDE_EFFECTING


class _SplitExchange:
    def __init__(self, srcs, lands, build, n_copies, name):
        self.build, self.n, self.name = build, n_copies, name
        self.ns, self.nl = len(srcs), len(lands)
        self.bufs = [pltpu.with_memory_space_constraint(a, pltpu.HBM) for a in list(srcs) + list(lands)]

    def _copies(self, refs, send_sems, recv_sems):
        triples = self.build(refs[:self.ns], refs[self.ns:self.ns + self.nl])
        assert len(triples) == self.n
        return [pltpu.make_async_remote_copy(src_ref=s, dst_ref=d, send_sem=send_sems.at[k], recv_sem=recv_sems.at[k],
                                             device_id=to, device_id_type=_MESH_ID) for k, (s, d, to) in enumerate(triples)]

    def start(self, deps=()):
        nb = self.ns + self.nl

        def body(*refs):
            outs = refs[nb + len(deps):]
            for cp in self._copies(refs, outs[0], outs[1]):
                cp.start()
            outs[-1][...] = jnp.zeros_like(outs[-1])

        sems = pltpu.SemaphoreType.DMA((self.n,))
        res = pl.pallas_call(
            body,
            out_shape=(sems, sems, *[pltpu.HBM(b.shape, b.dtype) for b in self.bufs], jax.ShapeDtypeStruct((8, 128), F32)),
            in_specs=[_HBM] * nb + [_ANY] * len(deps),
            out_specs=(_SEM, _SEM, *[_HBM] * nb, pl.BlockSpec(memory_space=pltpu.VMEM)),
            input_output_aliases={i: 2 + i for i in range(nb)},
            compiler_params=pltpu.CompilerParams(has_side_effects=_DATAFLOW),
            name=self.name + "_start",
        )(*self.bufs, *deps)
        self.send_sems, self.recv_sems = res[0], res[1]
        self.bufs = list(res[2:2 + nb])
        return res[-1]

    def wait(self, after=()):
        nb = self.ns + self.nl
        after = tuple(after) if isinstance(after, (tuple, list)) else (after,)

        def body(*refs):
            for cp in self._copies(refs, refs[nb], refs[nb + 1]):
                cp.wait_send()
                cp.wait_recv()

        res = pl.pallas_call(
            body,
            out_shape=tuple(pltpu.HBM(b.shape, b.dtype) for b in self.bufs),
            in_specs=[_HBM] * nb + [_SEM, _SEM] + [_ANY] * len(after),
            out_specs=tuple([_HBM] * nb),
            input_output_aliases={i: i for i in range(nb)},
            compiler_params=pltpu.CompilerParams(has_side_effects=_DATAFLOW),
            name=self.name + "_wait",
        )(*self.bufs, self.send_sems, self.recv_sems, *after)
        res = list(res)
        return res[:self.ns], res[self.ns:]


def _other_chips(x, y):
    return [(1 - x, y), (x, 1 - y), (1 - x, 1 - y)]


def _gather_steps(bufs, tag, deps, publish):
    n = len(bufs)
    half = lambda ref, i, slot, hc: ref.at[slot, :, pl.ds(hc * (bufs[i].shape[2] // 2), bufs[i].shape[2] // 2), :]

    def over_ici(srcs, lands):
        x, y, c = _coords()
        me = 2 * x + y
        return [(half(srcs[i], i, me, c), half(srcs[i], i, me, c), (px, py, c))
                for i in range(n) for px, py in _other_chips(x, y)]

    def to_sibling(srcs, lands):
        x, y, c = _coords()
        return [(half(srcs[i], i, 2 * px + py, c), half(srcs[i], i, 2 * px + py, c), (x, y, 1 - c))
                for i in range(n) for px, py in _other_chips(x, y)]

    ex = _SplitExchange(bufs, [], over_ici, 3 * n, tag + "_ici")
    after = yield ex.start(deps)
    bufs, _ = ex.wait(after)
    ex = _SplitExchange(bufs, [], to_sibling, 3 * n, tag + "_d2d")
    after = yield ex.start()
    bufs, _ = ex.wait(after)
    publish(bufs)


def _reduce_steps(grads, tag, publish):
    n = len(grads)
    rh = [g.shape[1] // 2 for g in grads]
    theirs = [lax.empty((g.shape[0], g.shape[1] // 2, g.shape[2]), F32) for g in grads]

    def halves(srcs, lands):
        x, y, c = _coords()
        return [(srcs[i].at[:, pl.ds((1 - c) * rh[i], rh[i]), :], lands[i], (x, y, 1 - c)) for i in range(n)]

    def chips(srcs, lands):
        x, y, c = _coords()
        me = 2 * x + y
        return [(srcs[i].at[2 * px + py], lands[i].at[me], (px, py, c)) for i in range(n) for px, py in _other_chips(x, y)]

    def sibling(srcs, lands):
        x, y, c = _coords()
        return [(srcs[i], lands[i], (x, y, 1 - c)) for i in range(n)]

    ex = _SplitExchange(grads, theirs, halves, n, tag + "_halves")
    after = yield ex.start()
    grads, theirs = ex.wait(after)
    parts = [_add_own_half(g, t, name="%s_add_own_half_%d" % (tag, i)) for i, (g, t) in enumerate(zip(grads, theirs))]
    ex = _SplitExchange(parts, [lax.empty(p.shape, p.dtype) for p in parts], chips, 3 * n, tag + "_chips")
    after = yield ex.start()
    parts, got = ex.wait(after)
    mine = [_sum_chips(p, g, name="%s_sum_chips_%d" % (tag, i)) for i, (p, g) in enumerate(zip(parts, got))]
    ex = _SplitExchange(mine, [lax.empty(m.shape, m.dtype) for m in mine], sibling, n, tag + "_sibling")
    after = yield ex.start()
    mine, theirs = ex.wait(after)
    publish(list(zip(mine, theirs)))


def _allreduce_steps(buf, tag, publish):
    r, c = buf.shape

    def to_sibling(srcs, lands):
        x, y, cc = _coords()
        return [(srcs[0], lands[0], (x, y, 1 - cc))]

    def over_ici(srcs, lands):
        x, y, cc = _coords()
        me = 2 * x + y
        return [(srcs[0].at[me], srcs[0].at[me], (px, py, cc)) for px, py in _other_chips(x, y)]

    def halves(srcs, lands):
        x, y, cc = _coords()
        return [(srcs[0].at[cc], srcs[0].at[cc], (x, y, 1 - cc))]

    ex = _SplitExchange([buf], [lax.empty(buf.shape, buf.dtype)], to_sibling, 1, tag + "_cores")
    after = yield ex.start()
    (mine,), (theirs,) = ex.wait(after)
    ex = _SplitExchange([_add_half_into_slot(mine, theirs, name=tag + "_chip_sum")], [], over_ici, N_CHIPS - 1, tag + "_chips")
    after = yield ex.start()
    (parts,), _ = ex.wait(after)
    total_half = _sum_slots(parts, name=tag + "_sum_chips")
    ex = _SplitExchange([_place_into_slot(total_half, 2, _my_core(), name=tag + "_place_half")], [], halves, 1, tag + "_halves")
    after = yield ex.start()
    (both,), _ = ex.wait(after)
    publish(both.reshape(r, c))


class _Exchanges:
    def __init__(self):
        self.running = []

    def launch(self, steps):
        self.running.append(steps)
        return next(steps)

    def advance(self, steps, after):
        try:
            return steps.send(after)
        except StopIteration:
            self.running.remove(steps)
            return None

    def advance_all(self, after):
        tokens = [self.advance(steps, after) for steps in list(self.running)]
        return tuple(t for t in tokens if t is not None)


def _add_own_half(g, theirs, *, name):
    s, r, c = g.shape
    rh = r // 2
    tm = _row_tile(10 * c, rh)
    nb = rh // tm

    def body(core_ref, g_ref, t_ref, o_ref):
        o_ref[...] = (g_ref[...] + t_ref[...]).astype(o_ref.dtype)

    return pl.pallas_call(
        body,
        out_shape=jax.ShapeDtypeStruct((s, rh, c), BF16),
        grid_spec=pltpu.PrefetchScalarGridSpec(
            num_scalar_prefetch=1,
            grid=(s, nb),
            in_specs=[pl.BlockSpec((None, tm, c), lambda k, i, core: (k, core[0] * nb + i, 0)),
                      pl.BlockSpec((None, tm, c), lambda k, i, core: (k, i, 0))],
            out_specs=pl.BlockSpec((None, tm, c), lambda k, i, core: (k, i, 0)),
        ),
        compiler_params=_cparams("parallel", "parallel"),
        name=name,
    )(_my_core(), g, theirs)


def _sum_chips(part, got, *, name):
    s, rh, c = part.shape
    tm = _row_tile(14 * c, rh)

    def body(me_ref, p_ref, a_ref, b_ref, c_ref, o_ref):
        o_ref[...] = ((p_ref[...].astype(F32) + a_ref[...].astype(F32)) + b_ref[...].astype(F32)) + c_ref[...].astype(F32)

    slot = lambda k: (lambda i, me: ((me[0] + k) % s, i, 0))
    return pl.pallas_call(
        body,
        out_shape=jax.ShapeDtypeStruct((rh, c), F32),
        grid_spec=pltpu.PrefetchScalarGridSpec(
            num_scalar_prefetch=1,
            grid=(rh // tm,),
            in_specs=[pl.BlockSpec((None, tm, c), slot(k)) for k in range(s)],
            out_specs=pl.BlockSpec((tm, c), lambda i, me: (i, 0)),
        ),
        compiler_params=_cparams("parallel"),
        name=name,
    )(_my_chip(), part, got, got, got)


def _place_into_slot(buf, n_slots, slot, *, name):
    r, c = buf.shape
    tm = _row_tile(8 * c, r)

    def body(slot_ref, i_ref, o_ref):
        o_ref[...] = i_ref[...]

    return pl.pallas_call(
        body,
        out_shape=jax.ShapeDtypeStruct((n_slots, r, c), buf.dtype),
        grid_spec=pltpu.PrefetchScalarGridSpec(
            num_scalar_prefetch=1,
            grid=(r // tm,),
            in_specs=[pl.BlockSpec((tm, c), lambda i, s: (i, 0))],
            out_specs=pl.BlockSpec((None, tm, c), lambda i, s: (s[0], i, 0)),
        ),
        compiler_params=_cparams("parallel"),
        name=name,
    )(slot, buf)


def _add_half_into_slot(mine, theirs, *, name):
    r, c = mine.shape
    rh = r // 2
    tm = _row_tile(12 * c, rh)
    nb = rh // tm
    where = jnp.concatenate([_my_chip(), _my_core()])

    def body(where_ref, a_ref, b_ref, o_ref):
        o_ref[...] = a_ref[...] + b_ref[...]

    half = pl.BlockSpec((tm, c), lambda i, w: (w[1] * nb + i, 0))
    return pl.pallas_call(
        body,
        out_shape=jax.ShapeDtypeStruct((N_CHIPS, rh, c), mine.dtype),
        grid_spec=pltpu.PrefetchScalarGridSpec(
            num_scalar_prefetch=1,
            grid=(nb,),
            in_specs=[half, half],
            out_specs=pl.BlockSpec((None, tm, c), lambda i, w: (w[0], i, 0)),
        ),
        compiler_params=_cparams("parallel"),
        name=name,
    )(where, mine, theirs)


def _sum_slots(arr, *, name):
    s, r, c = arr.shape
    tm = _row_tile(4 * c * (s + 1), r)

    def body(*refs):
        acc = refs[0][...]
        for ref in refs[1:s]:
            acc = acc + ref[...]
        refs[s][...] = acc

    return pl.pallas_call(
        body,
        out_shape=jax.ShapeDtypeStruct((r, c), arr.dtype),
        grid=(r // tm,),
        in_specs=[pl.BlockSpec((None, tm, c), lambda i, k=k: (k, i, 0)) for k in range(s)],
        out_specs=pl.BlockSpec((tm, c), lambda i: (i, 0)),
        compiler_params=_cparams("parallel"),
        name=name,
    )(*([arr] * s))


def _adamw_fn(w, g, m, v):
    m = ADAM_B1 * m + (1.0 - ADAM_B1) * g
    v = ADAM_B2 * v + (1.0 - ADAM_B2) * jnp.square(g)
    m_hat = m / (1.0 - ADAM_B1 ** ADAM_STEP)
    v_hat = v / (1.0 - ADAM_B2 ** ADAM_STEP)
    delta = -ADAM_LR * (m_hat / (jnp.sqrt(v_hat) + ADAM_EPS) + ADAM_WD * w)
    return delta, m, v


def _adamw(w, g, m, v, *, name):
    rows, cols = w.shape
    ins = [(a, "row", cols, 0) for a in (w, g, m, v)]
    outs = [(cols, F32, "row", cols)] * 3
    return _rowmap(_adamw_fn, ins, outs, rows=rows, tm=_row_tile(56 * cols, rows), name=name)


def _adamw_sharded(w, m, v, g_mine, g_sibling, layer, into, *, name, deps=()):
    nl, r, c = w.shape
    rh = r // 2
    tm = _row_tile(40 * c, rh)
    nb = rh // tm
    n_into = 0 if into is None else 4

    def body(core_ref, w_ref, m_ref, v_ref, a_ref, b_ref, *rest):
        g_ref, d_ref, nm_ref, nv_ref = rest[n_into + len(deps):]
        g = jnp.where(pl.program_id(0) == core_ref[0], a_ref[...], b_ref[...])
        delta, nm, nv = _adamw_fn(w_ref[...], g, m_ref[...], v_ref[...])
        g_ref[...] = g
        d_ref[...] = delta
        nm_ref[...] = nm
        nv_ref[...] = nv

    whole = pl.BlockSpec((None, tm, c), lambda h, i, core: (layer, h * nb + i, 0))
    half = pl.BlockSpec((tm, c), lambda h, i, core: (i, 0))
    shape = jax.ShapeDtypeStruct((nl, r, c), F32)
    return pl.pallas_call(
        body,
        out_shape=(shape, shape, shape, shape),
        grid_spec=pltpu.PrefetchScalarGridSpec(
            num_scalar_prefetch=1,
            grid=(2, nb),
            in_specs=[whole, whole, whole, half, half] + [pl.BlockSpec(memory_space=pl.ANY)] * (n_into + len(deps)),
            out_specs=(whole, whole, whole, whole),
        ),
        input_output_aliases={6 + k: k for k in range(n_into)},
        compiler_params=_cparams("parallel", "parallel"),
        name=name,
    )(_my_core(), w, m, v, g_mine, g_sibling, *(into or ()), *deps)


def _pack(arrays):
    flat = jnp.concatenate([a.reshape(-1) for a in arrays])
    pad = (-flat.shape[0]) % (256 * 128)
    return jnp.pad(flat, (0, pad)).reshape(-1, 128)


def _unpack(buf, shapes):
    flat = buf.reshape(-1)
    out, off = [], 0
    for s in shapes:
        n = math.prod(s)
        out.append(flat[off:off + n].reshape(s))
        off += n
    return out


def kernel(x, norm_mix_g, w_in, gate_bias, q_norm_g, k_norm_g, attn_sinks, ssm_lambda_re, ssm_lambda_im, ssm_log_dt, ssm_b_re, ssm_b_im, ssm_c_re, ssm_c_im, ssm_d, ssm_glu_w, ssm_glu_b, w_attn_branch, w_ssm_branch, w_out, norm_ffn_g, w_ffn_in, w_ffn_out, loss_target, m_norm_mix_g, m_w_in, m_gate_bias, m_q_norm_g, m_k_norm_g, m_attn_sinks, m_ssm_lambda_re, m_ssm_lambda_im, m_ssm_log_dt, m_ssm_b_re, m_ssm_b_im, m_ssm_c_re, m_ssm_c_im, m_ssm_d, m_ssm_glu_w, m_ssm_glu_b, m_w_attn_branch, m_w_ssm_branch, m_w_out, m_norm_ffn_g, m_w_ffn_in, m_w_ffn_out, v_norm_mix_g, v_w_in, v_gate_bias, v_q_norm_g, v_k_norm_g, v_attn_sinks, v_ssm_lambda_re, v_ssm_lambda_im, v_ssm_log_dt, v_ssm_b_re, v_ssm_b_im, v_ssm_c_re, v_ssm_c_im, v_ssm_d, v_ssm_glu_w, v_ssm_glu_b, v_w_attn_branch, v_w_ssm_branch, v_w_out, v_norm_ffn_g, v_w_ffn_in, v_w_ffn_out):
    w = dict(norm_mix_g=norm_mix_g, w_in=w_in, gate_bias=gate_bias, q_norm_g=q_norm_g, k_norm_g=k_norm_g,
             attn_sinks=attn_sinks, ssm_lambda_re=ssm_lambda_re, ssm_lambda_im=ssm_lambda_im, ssm_log_dt=ssm_log_dt,
             ssm_b_re=ssm_b_re, ssm_b_im=ssm_b_im, ssm_c_re=ssm_c_re, ssm_c_im=ssm_c_im, ssm_d=ssm_d,
             ssm_glu_w=ssm_glu_w, ssm_glu_b=ssm_glu_b, w_attn_branch=w_attn_branch, w_ssm_branch=w_ssm_branch,
             w_out=w_out, norm_ffn_g=norm_ffn_g, w_ffn_in=w_ffn_in, w_ffn_out=w_ffn_out)
    m = dict(norm_mix_g=m_norm_mix_g, w_in=m_w_in, gate_bias=m_gate_bias, q_norm_g=m_q_norm_g, k_norm_g=m_k_norm_g,
             attn_sinks=m_attn_sinks, ssm_lambda_re=m_ssm_lambda_re, ssm_lambda_im=m_ssm_lambda_im,
             ssm_log_dt=m_ssm_log_dt, ssm_b_re=m_ssm_b_re, ssm_b_im=m_ssm_b_im, ssm_c_re=m_ssm_c_re,
             ssm_c_im=m_ssm_c_im, ssm_d=m_ssm_d, ssm_glu_w=m_ssm_glu_w, ssm_glu_b=m_ssm_glu_b,
             w_attn_branch=m_w_attn_branch, w_ssm_branch=m_w_ssm_branch, w_out=m_w_out, norm_ffn_g=m_norm_ffn_g,
             w_ffn_in=m_w_ffn_in, w_ffn_out=m_w_ffn_out)
    v = dict(norm_mix_g=v_norm_mix_g, w_in=v_w_in, gate_bias=v_gate_bias, q_norm_g=v_q_norm_g, k_norm_g=v_k_norm_g,
             attn_sinks=v_attn_sinks, ssm_lambda_re=v_ssm_lambda_re, ssm_lambda_im=v_ssm_lambda_im,
             ssm_log_dt=v_ssm_log_dt, ssm_b_re=v_ssm_b_re, ssm_b_im=v_ssm_b_im, ssm_c_re=v_ssm_c_re,
             ssm_c_im=v_ssm_c_im, ssm_d=v_ssm_d, ssm_glu_w=v_ssm_glu_w, ssm_glu_b=v_ssm_glu_b,
             w_attn_branch=v_w_attn_branch, w_ssm_branch=v_w_ssm_branch, w_out=v_w_out, norm_ffn_g=v_norm_ffn_g,
             w_ffn_in=v_w_ffn_in, w_ffn_out=v_w_ffn_out)
    n_layers = norm_mix_g.shape[0]
    d_model = x.shape[-1]
    seq = x.shape[1]

    exchanges = _Exchanges()
    params = [_layer_params(l, w) for l in range(n_layers)]
    gathers = {}

    def gather(l, group, bufs, deps=()):
        names = WEIGHT_GROUPS[group]
        steps = _gather_steps(bufs, "ag%d_%s" % (l, group), deps,
                              lambda got: params[l].update(_big_params(dict(zip(names, got)))))
        gathers[l, group] = steps
        return exchanges.launch(steps)

    started = gather(0, "in", [_cast_into_slot(w["w_in"], 0, name="cast0_w_in")])
    casts = {(l, group): [_cast_into_slot(w[n], l, name="cast%d_%s" % (l, n), deps=(started,)) for n in WEIGHT_GROUPS[group]]
             for l in range(n_layers) for group in WEIGHT_GROUPS if (l, group) != (0, "in")}
    forwarded = exchanges.advance(gathers[0, "in"], tuple(b for bufs in casts.values() for b in bufs))
    first_deps = (forwarded, gather(0, "mixer", casts[0, "mixer"], (forwarded,)), gather(0, "ffn", casts[0, "ffn"], (forwarded,)))
    sizes = {n: (N_CHIPS, 1) + w[n].shape[1:] for n in BIG_WEIGHTS}
    dims = _dims(d_model, sizes, min(512, seq))

    def weights_of(l, h):
        if l == 0:
            return params[0], first_deps
        for group in WEIGHT_GROUPS:
            exchanges.advance(gathers[l, group], h)
        return params[l], ()

    def fwd_tick(l, point, arr):
        tokens = []
        if l == 0 and point == "norm":
            tokens.append(exchanges.advance(gathers[0, "in"], arr))
        if l == 0 and point in ("in", "attn"):
            tokens.append(exchanges.advance(gathers[0, "mixer"], arr))
        if l == 0 and point in ("ssm", "out"):
            tokens.append(exchanges.advance(gathers[0, "ffn"], arr))
        if l + 1 < n_layers and point == "attn":
            tokens += [gather(l + 1, group, casts[l + 1, group]) for group in WEIGHT_GROUPS]
        if l + 1 < n_layers and point == "act":
            tokens += [exchanges.advance(gathers[l + 1, group], arr) for group in WEIGHT_GROUPS]
        return tuple(t for t in tokens if t is not None)

    reduced, ready = {}, []

    def bwd_tick(l, stage, arr, stage_grads):
        tokens = exchanges.advance_all(arr)
        names = tuple(stage_grads)

        def publish(halves):
            reduced.update({(l, n): h for n, h in zip(names, halves)})
            ready.append((l, names))

        grads4 = [stage_grads[n].reshape(N_CHIPS, -1, stage_grads[n].shape[-1]) for n in names]
        return tokens + (exchanges.launch(_reduce_steps(grads4, "rs%d_%s" % (l, stage), publish)),)

    sq, dx, grads, last_tokens = _local_step(x[0], loss_target[0], n_layers, weights_of, dims, fwd_tick, bwd_tick)
    loss = lax.psum(sq[0, 0], MESH_AXES) * (0.5 / d_model)

    small_shapes = [w[n].shape for n in SMALL_WEIGHTS]
    small_local = [jnp.stack([grads[l][n].reshape(w[n].shape[1:]) for l in range(n_layers)]) for n in SMALL_WEIGHTS]
    shared = []
    tokens = last_tokens + (exchanges.launch(_allreduce_steps(_pack(small_local), "small_grads", shared.append)),)

    adam = {n: None for n in BIG_WEIGHTS}
    grad, delta, new_m, new_v = {}, {}, {}, {}
    while exchanges.running or ready or shared:
        after = [dx]
        for l, names in ready[:2]:
            for n in names:
                mine, sibling = reduced[l, n]
                adam[n] = _adamw_sharded(w[n], m[n], v[n], mine, sibling, l, adam[n], name="adamw%d_%s" % (l, n), deps=tokens)
                after.append(adam[n][1])
        del ready[:2]
        if shared:
            grad.update(zip(SMALL_WEIGHTS, _unpack(shared.pop(), small_shapes)))
            for n in SMALL_WEIGHTS:
                flat = lambda a: a.reshape(-1, a.shape[-1])
                res = _adamw(flat(w[n]), flat(grad[n]), flat(m[n]), flat(v[n]), name="adamw_" + n)
                delta[n], new_m[n], new_v[n] = [r.reshape(w[n].shape) for r in res]
            after += [delta[n] for n in SMALL_WEIGHTS]
        tokens = exchanges.advance_all(tuple(after))
    for n in BIG_WEIGHTS:
        grad[n], delta[n], new_m[n], new_v[n] = adam[n]

    return (loss, dx[None], *[grad[n] for n in WEIGHT_NAMES], *[delta[n] for n in WEIGHT_NAMES],
            *[new_m[n] for n in WEIGHT_NAMES], *[new_v[n] for n in WEIGHT_NAMES])
```

```python
import functools
import math

import jax
import jax.numpy as jnp
from jax import lax
from jax.experimental import pallas as pl
from jax.experimental.pallas import tpu as pltpu

HEAD_DIM = 64
WINDOW = 128
SSM_GROUP_CH = 16
SSM_LANE_GROUPS = 8
RMS_EPS = 1e-6
ADAM_LR = 0.001
ADAM_B1 = 0.9
ADAM_B2 = 0.999
ADAM_EPS = 1e-08
ADAM_WD = 0.01
ADAM_STEP = 10
NEG_BIG = -1e30
MESH_AXES = ("x", "y", "c")
N_CHIPS = 4
N_DEV = 8
VMEM_LIMIT_BYTES = 56 * 1024 * 1024
BF16 = jnp.bfloat16
F32 = jnp.float32


def _cparams(*semantics):
    return pltpu.CompilerParams(dimension_semantics=semantics, vmem_limit_bytes=VMEM_LIMIT_BYTES)


def _pick(n, target, mult):
    if n <= target:
        return n
    best = None
    for d in range(mult, target + 1, mult):
        if n % d == 0:
            best = d
    assert best is not None, (n, target, mult)
    return best


def _rowmap(fn, ins, outs, *, rows, tm, ncol=1, name, deps=()):
    n_in = len(ins)
    nrow = rows // tm
    assert nrow * tm == rows

    in_specs = []
    for arr, kind, width, coloff in ins:
        if kind == "row":
            in_specs.append(pl.BlockSpec((tm, width), lambda j, i, o=coloff: (i, o + j)))
        elif kind == "vec":
            in_specs.append(pl.BlockSpec((1, width), lambda j, i, o=coloff: (0, o + j)))
        else:
            nd = arr.ndim
            in_specs.append(pl.BlockSpec(arr.shape, lambda j, i, nd=nd: (0,) * nd))
    out_specs, out_shapes = [], []
    for cols, dtype, kind, width in outs:
        if kind == "row":
            out_specs.append(pl.BlockSpec((tm, width), lambda j, i: (i, j)))
            out_shapes.append(jax.ShapeDtypeStruct((rows, cols), dtype))
        else:
            out_specs.append(pl.BlockSpec((1, width), lambda j, i: (0, j)))
            out_shapes.append(jax.ShapeDtypeStruct((1, cols), dtype))

    in_specs += [pl.BlockSpec(memory_space=pl.ANY)] * len(deps)

    def body(*refs):
        i = pl.program_id(1)
        res = fn(*[r[...].astype(F32) for r in refs[:n_in]])
        if not isinstance(res, (tuple, list)):
            res = (res,)
        for (cols, dtype, kind, width), ref, val in zip(outs, refs[n_in + len(deps):], res):
            if kind == "row":
                ref[...] = val.astype(ref.dtype)
            else:
                @pl.when(i == 0)
                def _():
                    ref[...] = jnp.zeros_like(ref)
                ref[...] += val.astype(ref.dtype)

    res = pl.pallas_call(
        body,
        out_shape=tuple(out_shapes),
        grid=(ncol, nrow),
        in_specs=in_specs,
        out_specs=tuple(out_specs),
        compiler_params=_cparams("parallel", "arbitrary"),
        name=name,
    )(*[a[0] for a in ins], *deps)
    return res


def _mm_body(dims, nk, has_add, unused_in=0):
    def body(*refs):
        if has_add:
            a_ref, b_ref, add_ref = refs[:3]
            o_ref = refs[3 + unused_in]
            rest = refs[4 + unused_in:]
        else:
            a_ref, b_ref = refs[:2]
            o_ref = refs[2 + unused_in]
            add_ref = None
            rest = refs[3 + unused_in:]
        part = lax.dot_general(a_ref[...], b_ref[...], (dims, ((), ())), preferred_element_type=F32)
        if nk == 1:
            if add_ref is not None:
                part = part + add_ref[...]
            o_ref[...] = part.astype(o_ref.dtype)
        else:
            acc_ref = rest[0]
            k = pl.program_id(2)

            @pl.when(k == 0)
            def _():
                acc_ref[...] = part

            @pl.when(k > 0)
            def _():
                acc_ref[...] += part

            @pl.when(k == nk - 1)
            def _():
                r = acc_ref[...]
                if add_ref is not None:
                    r = r + add_ref[...]
                o_ref[...] = r.astype(o_ref.dtype)
    return body


class _Weight:
    def __init__(self, arr, layer, kind):
        self.arr, self.layer, self.kind = arr, layer, kind
        self.s, _, self.r, self.c = arr.shape
        self.rows = self.r * (self.s if kind == "row" else 1)
        self.cols = self.c * (self.s if kind == "col" else 1)

    def tiles(self, tr, tc):
        return _pick(self.r, tr, 128), _pick(self.c, tc, 128)

    def index(self, tr, tc):
        layer = self.layer
        if self.kind == "col":
            per = self.c // tc
            return lambda rb, cb: (cb // per, layer, rb, cb % per)
        per = self.r // tr
        return lambda rb, cb: (rb // per, layer, rb % per, cb)


def _shard_index(kind, r, c, tr, tc):
    if kind == "col":
        per = c // tc
        return lambda rb, cb: (cb // per, rb, cb % per)
    per = r // tr
    return lambda rb, cb: (rb // per, rb % per, cb)


def _mm_nn(a, w, *, out_dtype, tm, tn, tk, name, add=None, deps=()):
    m, k = a.shape
    assert k == w.rows
    tm = _pick(m, tm, 16)
    tk, tn = w.tiles(tk, tn)
    nk = k // tk
    widx = w.index(tk, tn)
    in_specs = [pl.BlockSpec((tm, tk), lambda n, i, kk: (i, kk)),
                pl.BlockSpec((None, None, tk, tn), lambda n, i, kk: widx(kk, n))]
    args = [a, w.arr]
    if add is not None:
        in_specs.append(pl.BlockSpec((tm, tn), lambda n, i, kk: (i, n)))
        args.append(add)
    in_specs += [pl.BlockSpec(memory_space=pl.ANY)] * len(deps)
    args += list(deps)
    return pl.pallas_call(
        _mm_body(((1,), (0,)), nk, add is not None, unused_in=len(deps)),
        out_shape=jax.ShapeDtypeStruct((m, w.cols), out_dtype),
        grid=(w.cols // tn, m // tm, nk),
        in_specs=in_specs,
        out_specs=pl.BlockSpec((tm, tn), lambda n, i, kk: (i, n)),
        scratch_shapes=[pltpu.VMEM((tm, tn), F32)] if nk > 1 else [],
        compiler_params=_cparams("parallel", "parallel", "arbitrary"),
        name=name,
    )(*args)


def _interleaved_block(k, n_blocks):
    half = n_blocks // 2
    if isinstance(k, int):
        return 2 * k if k < half else 2 * (k - half) + 1
    return jnp.where(k < half, 2 * k, 2 * (k - half) + 1)


def _mm_nt(a, w, *, out_dtype, tm, tn, tko, name, deps=(), interleaved=0):
    m, n = a.shape
    assert n == w.cols
    tm = _pick(m, tm, 16)
    if w.kind == "col" and w.s > 1:
        tko = _pick(w.r, tko, 128)
        layer, nsh, width = w.layer, w.s, w.c
        blk = interleaved or width
        per = width // blk

        def body(a_ref, w_ref, *rest):
            o_ref = rest[len(deps)]
            acc = None
            for k in range(nsh * per):
                s, j = divmod(k, per)
                at = (_interleaved_block(k, nsh * per) if interleaved else k) * blk
                part = lax.dot_general(a_ref[:, at:at + blk], w_ref[s, :, j * blk:(j + 1) * blk], (((1,), (1,)), ((), ())),
                                       preferred_element_type=F32)
                acc = part if acc is None else acc + part
            o_ref[...] = acc.astype(o_ref.dtype)

        return pl.pallas_call(
            body,
            out_shape=jax.ShapeDtypeStruct((m, w.rows), out_dtype),
            grid=(w.rows // tko, m // tm),
            in_specs=[pl.BlockSpec((tm, n), lambda ko, i: (i, 0)),
                      pl.BlockSpec((nsh, None, tko, width), lambda ko, i: (0, layer, ko, 0))]
            + [pl.BlockSpec(memory_space=pl.ANY)] * len(deps),
            out_specs=pl.BlockSpec((tm, tko), lambda ko, i: (i, ko)),
            compiler_params=_cparams("parallel", "parallel"),
            name=name,
        )(a, w.arr, *deps)
    tko, tn = w.tiles(tko, tn)
    nk = n // tn
    widx = w.index(tko, tn)
    return pl.pallas_call(
        _mm_body(((1,), (1,)), nk, False, unused_in=len(deps)),
        out_shape=jax.ShapeDtypeStruct((m, w.rows), out_dtype),
        grid=(w.rows // tko, m // tm, nk),
        in_specs=[pl.BlockSpec((tm, tn), lambda ko, i, nn: (i, nn)),
                  pl.BlockSpec((None, None, tko, tn), lambda ko, i, nn: widx(ko, nn))]
        + [pl.BlockSpec(memory_space=pl.ANY)] * len(deps),
        out_specs=pl.BlockSpec((tm, tko), lambda ko, i, nn: (i, ko)),
        scratch_shapes=[pltpu.VMEM((tm, tko), F32)] if nk > 1 else [],
        compiler_params=_cparams("parallel", "parallel", "arbitrary"),
        name=name,
    )(a, w.arr, *deps)


def _mm_tn(a, c, w, *, into, tm, tn, tko, name, interleaved=False):
    m, k = a.shape
    tm = _pick(m, tm, 16)
    layer = w.layer
    mc, n = c.shape
    assert mc == m and k == w.rows and n == w.cols
    tko, tn = w.tiles(tko, tn)
    nk = m // tm
    oidx = _shard_index(w.kind, w.r, w.c, tko, tn)
    n_blocks = n // tn
    c_block = (lambda nn: _interleaved_block(nn, n_blocks)) if interleaved else (lambda nn: nn)
    in_specs = [pl.BlockSpec((tm, tko), lambda ko, nn, mm: (mm, ko)),
                pl.BlockSpec((tm, tn), lambda ko, nn, mm: (mm, c_block(nn)))]
    args = [a, c]
    if into is not None:
        in_specs.append(pl.BlockSpec(memory_space=pl.ANY))
        args.append(into)
    return pl.pallas_call(
        _mm_body(((0,), (0,)), nk, False, unused_in=len(args) - 2),
        out_shape=jax.ShapeDtypeStruct((w.arr.shape[1], w.s, w.r, w.c), F32),
        grid=(k // tko, n // tn, nk),
        in_specs=in_specs,
        out_specs=pl.BlockSpec((None, None, tko, tn), lambda ko, nn, mm: (layer,) + oidx(ko, nn)),
        scratch_shapes=[pltpu.VMEM((tko, tn), F32)] if nk > 1 else [],
        input_output_aliases={2: 0} if into is not None else {},
        compiler_params=_cparams("parallel", "parallel", "arbitrary"),
        name=name,
    )(*args)


def _mxu_sum(x, ones):
    hi = x.astype(BF16)
    lo = (x - hi.astype(F32)).astype(BF16)
    return jnp.dot(hi, ones, preferred_element_type=F32) + jnp.dot(lo, ones, preferred_element_type=F32)


def _head_rms(x, gain, head_ones):
    r = lax.rsqrt(_mxu_sum(x * x, head_ones) * (1.0 / HEAD_DIM) + RMS_EPS)
    return x * r * gain, r


def _head_rms_bwd(x, r, gain, dy, head_ones, fold):
    t = dy * gain
    dx = r * t - x * (r * r * r) * (_mxu_sum(t * x, head_ones) * (1.0 / HEAD_DIM))
    dg = jnp.broadcast_to(jnp.sum(dy * x * r, axis=0, keepdims=True), (8, x.shape[1]))
    return dx, _mxu_sum(dg, fold)[0:1, :HEAD_DIM]


def _attn_consts(n_q, n_kv, sinks, qg, kg):
    group = n_q // n_kv
    t = jnp.arange(WINDOW, dtype=jnp.int32)[:, None]
    s = jnp.arange(2 * WINDOW, dtype=jnp.int32)[None, :] - WINDOW
    dist = (t - s).astype(F32)
    valid = (dist >= 0) & (dist < WINDOW)
    slopes = jnp.exp2(-8.0 * jnp.arange(1, n_q + 1, dtype=F32) / n_q)
    bias = jnp.where(valid[None], -slopes[:, None, None] * dist[None], NEG_BIG)
    sink = jnp.broadcast_to(sinks.astype(F32).reshape(n_kv, group, 1, 1), (n_kv, group, WINDOW, 128))
    head_ones = lambda h: jnp.kron(jnp.eye(h, dtype=F32), jnp.ones((HEAD_DIM, HEAD_DIM), F32)).astype(BF16)
    fold = lambda h: jnp.tile(jnp.eye(HEAD_DIM, 128, dtype=F32), (h, 1)).astype(BF16)
    return dict(
        bias=bias.reshape(n_kv, group * WINDOW, 2 * WINDOW),
        sink=sink.reshape(n_kv, group * WINDOW, 128),
        qg=jnp.tile(qg, (1, n_q)), kg=jnp.tile(kg, (1, n_kv)),
        q_ones=head_ones(n_q), k_ones=head_ones(n_kv),
        q_fold=fold(n_q), k_fold=fold(n_kv),
        key_ones=jnp.ones((2 * WINDOW, 128), BF16))


_ATTN_CONST_ORDER = ("qg", "kg", "sink", "bias", "q_ones", "k_ones", "q_fold", "k_fold", "key_ones")


def _attn_inputs(q_ref, kc_ref, kp_ref, vc_ref, vp_ref, c):
    q = q_ref[...]
    k2 = jnp.concatenate([kp_ref[...], kc_ref[...]], axis=0)
    v2 = jnp.concatenate([vp_ref[...], vc_ref[...]], axis=0)
    qn, rq = _head_rms(q, c["qg"][...], c["q_ones"][...])
    kn, rk = _head_rms(k2, c["kg"][...], c["k_ones"][...])
    return dict(q=q, rq=rq, qn=qn.astype(BF16), k2=k2, rk=rk, kn=kn.astype(BF16), v2=v2.astype(BF16))


def _attn_probs(x, c, first_mask, kv, group):
    sl = slice(kv * HEAD_DIM, (kv + 1) * HEAD_DIM)
    k2b, v2b = x["kn"][:, sl], x["v2"][:, sl]
    qs = jnp.concatenate([x["qn"][:, (kv * group + g) * HEAD_DIM:(kv * group + g + 1) * HEAD_DIM]
                          for g in range(group)], axis=0)
    s = lax.dot_general(qs, k2b, (((1,), (1,)), ((), ())), preferred_element_type=F32) * (HEAD_DIM ** -0.5)
    s = jnp.where(first_mask, NEG_BIG, s + c["bias"][kv])
    sink = c["sink"][kv]
    m = jnp.maximum(jnp.max(s, axis=-1, keepdims=True), sink)
    twice = lambda a: jnp.concatenate([a, a], axis=1)
    p = jnp.exp(s - twice(m))
    esink = jnp.exp(sink - m)
    inv = 1.0 / (_mxu_sum(p, c["key_ones"][...]) + esink)
    return dict(k2b=k2b, v2b=v2b, qs=qs, pn=p * twice(inv), psink=esink * inv, twice=twice)


def _attn_specs(n_q, n_kv):
    aw, kvw = n_q * HEAD_DIM, n_kv * HEAD_DIM
    group = n_q // n_kv
    kblk, vblk = aw // kvw, aw // kvw + 1

    def specs(nb):
        cur = lambda n: jnp.minimum(n, nb - 1)
        prev = lambda n: jnp.maximum(jnp.minimum(n, nb - 1) - 1, 0)
        return [
            pl.BlockSpec((WINDOW, aw), lambda n: (cur(n), 0)),
            pl.BlockSpec((WINDOW, kvw), lambda n: (cur(n), kblk)),
            pl.BlockSpec((WINDOW, kvw), lambda n: (prev(n), kblk)),
            pl.BlockSpec((WINDOW, kvw), lambda n: (cur(n), vblk)),
            pl.BlockSpec((WINDOW, kvw), lambda n: (prev(n), vblk)),
        ]
    whole = lambda shape: pl.BlockSpec(shape, lambda n: (0,) * len(shape))
    return specs, whole


def _attn_fwd(z, qg, kg, sinks, *, n_q, n_kv, name, deps=()):
    L = z.shape[0]
    nb = L // WINDOW
    aw = n_q * HEAD_DIM
    group = n_q // n_kv
    consts = _attn_consts(n_q, n_kv, sinks, qg, kg)
    specs, whole = _attn_specs(n_q, n_kv)
    nc = len(_ATTN_CONST_ORDER)

    def body(q_ref, kc_ref, kp_ref, vc_ref, vp_ref, *rest):
        c = dict(zip(_ATTN_CONST_ORDER, rest[:nc]))
        o_ref = rest[-1]
        n = pl.program_id(0)
        col = lax.broadcasted_iota(jnp.int32, (group * WINDOW, 2 * WINDOW), 1)
        first_mask = jnp.logical_and(n == 0, col < WINDOW)
        x = _attn_inputs(q_ref, kc_ref, kp_ref, vc_ref, vp_ref, c)
        for kv in range(n_kv):
            a = _attn_probs(x, c, first_mask, kv, group)
            o = jnp.dot(a["pn"].astype(BF16), a["v2b"], preferred_element_type=F32)
            for g in range(group):
                h = kv * group + g
                o_ref[:, h * HEAD_DIM:(h + 1) * HEAD_DIM] = o[g * WINDOW:(g + 1) * WINDOW].astype(o_ref.dtype)

    return pl.pallas_call(
        body,
        out_shape=jax.ShapeDtypeStruct((L, aw), BF16),
        grid=(nb,),
        in_specs=specs(nb) + [whole(consts[k].shape) for k in _ATTN_CONST_ORDER]
        + [pl.BlockSpec(memory_space=pl.ANY)] * len(deps),
        out_specs=pl.BlockSpec((WINDOW, aw), lambda n: (n, 0)),
        compiler_params=_cparams("parallel"),
        name=name,
    )(z, z, z, z, z, *[consts[k] for k in _ATTN_CONST_ORDER], *deps)


def _attn_bwd(z, do, qg, kg, sinks, *, n_q, n_kv, name):
    L = z.shape[0]
    nb = L // WINDOW
    aw, kvw = n_q * HEAD_DIM, n_kv * HEAD_DIM
    group = n_q // n_kv
    consts = _attn_consts(n_q, n_kv, sinks, qg, kg)
    specs, whole = _attn_specs(n_q, n_kv)
    scale = HEAD_DIM ** -0.5
    nc = len(_ATTN_CONST_ORDER)

    def body(q_ref, kc_ref, kp_ref, vc_ref, vp_ref, do_ref, *rest):
        c = dict(zip(_ATTN_CONST_ORDER, rest[:nc]))
        dq_ref, dkv_ref, dqg_ref, dkg_ref, dsink_ref, carry_ref, dqn_ref, dkn_ref, dv_ref = rest[nc:]
        n = pl.program_id(0)

        @pl.when(n == 0)
        def _():
            dqg_ref[...] = jnp.zeros_like(dqg_ref)
            dkg_ref[...] = jnp.zeros_like(dkg_ref)
            dsink_ref[...] = jnp.zeros_like(dsink_ref)
            carry_ref[...] = jnp.zeros_like(carry_ref)

        @pl.when(n < nb)
        def _():
            col = lax.broadcasted_iota(jnp.int32, (group * WINDOW, 2 * WINDOW), 1)
            first_mask = jnp.logical_and(n == 0, col < WINDOW)
            head_lane = lax.broadcasted_iota(jnp.int32, (1, n_q), 1)
            x = _attn_inputs(q_ref, kc_ref, kp_ref, vc_ref, vp_ref, c)
            dsink = jnp.zeros((1, n_q), F32)
            for kv in range(n_kv):
                a = _attn_probs(x, c, first_mask, kv, group)
                pn = a["pn"]
                dos = jnp.concatenate(
                    [do_ref[:, (kv * group + g) * HEAD_DIM:(kv * group + g + 1) * HEAD_DIM] for g in range(group)],
                    axis=0).astype(BF16)
                dpn = lax.dot_general(dos, a["v2b"], (((1,), (1,)), ((), ())), preferred_element_type=F32)
                ksl = slice(kv * HEAD_DIM, (kv + 1) * HEAD_DIM)
                dv_ref[:, ksl] = lax.dot_general(pn.astype(BF16), dos, (((0,), (0,)), ((), ())), preferred_element_type=F32)
                delta = _mxu_sum(pn * dpn, c["key_ones"][...])
                ds = (pn * (dpn - a["twice"](delta))).astype(BF16)
                dsk = -a["psink"] * delta
                dqn = lax.dot_general(ds, a["k2b"], (((1,), (0,)), ((), ())), preferred_element_type=F32) * scale
                dkn_ref[:, ksl] = lax.dot_general(ds, a["qs"], (((0,), (0,)), ((), ())), preferred_element_type=F32) * scale
                for g in range(group):
                    h = kv * group + g
                    rows = slice(g * WINDOW, (g + 1) * WINDOW)
                    dqn_ref[:, h * HEAD_DIM:(h + 1) * HEAD_DIM] = dqn[rows]
                    dsink = dsink + jnp.where(head_lane == h, jnp.sum(dsk[rows], axis=0, keepdims=True)[:, :n_q], 0.0)
            dq, dqg = _head_rms_bwd(x["q"], x["rq"], c["qg"][...], dqn_ref[...], c["q_ones"][...], c["q_fold"][...])
            dk2, dkg = _head_rms_bwd(x["k2"], x["rk"], c["kg"][...], dkn_ref[...], c["k_ones"][...], c["k_fold"][...])
            dq_ref[...] = dq.astype(dq_ref.dtype)
            dkv_ref[:, :kvw] = (carry_ref[:, :kvw] + dk2[:WINDOW]).astype(dkv_ref.dtype)
            dkv_ref[:, kvw:] = (carry_ref[:, kvw:] + dv_ref[:WINDOW, :]).astype(dkv_ref.dtype)
            carry_ref[:, :kvw] = dk2[WINDOW:]
            carry_ref[:, kvw:] = dv_ref[WINDOW:, :]
            dqg_ref[...] += dqg
            dkg_ref[...] += dkg
            dsink_ref[...] += dsink

        @pl.when(n == nb)
        def _():
            dkv_ref[...] = carry_ref[...].astype(dkv_ref.dtype)

    in_specs = (specs(nb) + [pl.BlockSpec((WINDOW, aw), lambda n: (jnp.minimum(n, nb - 1), 0))]
                + [whole(consts[k].shape) for k in _ATTN_CONST_ORDER])
    return pl.pallas_call(
        body,
        out_shape=(jax.ShapeDtypeStruct((L, aw), BF16), jax.ShapeDtypeStruct((L, 2 * kvw), BF16),
                   jax.ShapeDtypeStruct((1, HEAD_DIM), F32), jax.ShapeDtypeStruct((1, HEAD_DIM), F32),
                   jax.ShapeDtypeStruct((1, n_q), F32)),
        grid=(nb + 1,),
        in_specs=in_specs,
        out_specs=(pl.BlockSpec((WINDOW, aw), lambda n: (jnp.minimum(n, nb - 1), 0)),
                   pl.BlockSpec((WINDOW, 2 * kvw), lambda n: (jnp.maximum(n - 1, 0), 0)),
                   pl.BlockSpec((1, HEAD_DIM), lambda n: (0, 0)),
                   pl.BlockSpec((1, HEAD_DIM), lambda n: (0, 0)),
                   pl.BlockSpec((1, n_q), lambda n: (0, 0))),
        scratch_shapes=[pltpu.VMEM((WINDOW, 2 * kvw), F32), pltpu.VMEM((WINDOW, aw), F32),
                        pltpu.VMEM((2 * WINDOW, kvw), F32), pltpu.VMEM((2 * WINDOW, kvw), F32)],
        compiler_params=_cparams("arbitrary"),
        name=name,
    )(z, z, z, z, z, do, *[consts[k] for k in _ATTN_CONST_ORDER])


def _cmul(ar, ai, br, bi):
    return ar * br - ai * bi, ar * bi + ai * br


def _time_permutation(tc):
    r = jnp.arange(tc)
    src = (r % 8) * (tc // 8) + r // 8
    p = (src[:, None] == jnp.arange(tc)[None, :]).astype(BF16)
    return p, p.T


def _unpermute(pt, x):
    hi = x.astype(BF16)
    lo = (x - hi.astype(F32)).astype(BF16)
    moved = jnp.dot(pt, jnp.concatenate([hi, lo], axis=1), preferred_element_type=F32)
    return moved[:, :x.shape[1]] + moved[:, x.shape[1]:]


def _segment_scan(xr_ref, xi_ref, ar, ai, cr, ci, ng, reverse):
    n = ar.shape[-1]
    row = lax.broadcasted_iota(jnp.int32, (8, n), 0)
    seeded = 7 if reverse else 0
    a8r = jnp.broadcast_to(ar, (8, n))
    a8i = jnp.broadcast_to(ai, (8, n))
    rows_of = lambda g: pl.ds(pl.multiple_of(((ng - 1 - g) if reverse else g) * 8, 8), 8)

    def recur(g, s):
        rows = rows_of(g)
        sr = a8r * s[0] - a8i * s[1] + xr_ref[rows, :]
        si = a8r * s[1] + a8i * s[0] + xi_ref[rows, :]
        xr_ref[rows, :] = sr
        xi_ref[rows, :] = si
        return sr, si

    fr, fi = lax.fori_loop(0, ng, recur, (jnp.where(row == seeded, cr, 0.0), jnp.where(row == seeded, ci, 0.0)))
    pr, pi = ar, ai
    for _ in range(ng.bit_length() - 1):
        pr, pi = _cmul(pr, pi, pr, pi)
    for k in (1, 2, 4):
        keep = (row < 8 - k) if reverse else (row >= k)
        shift = (8 - k) if reverse else k
        mr, mi = jnp.where(keep, pr, 0.0), jnp.where(keep, pi, 0.0)
        tr, ti = pltpu.roll(fr, shift, 0), pltpu.roll(fi, shift, 0)
        fr, fi = fr + mr * tr - mi * ti, fi + mr * ti + mi * tr
        pr, pi = _cmul(pr, pi, pr, pi)
    shift = 7 if reverse else 1
    before_r, before_i = pltpu.roll(fr, shift, 0), pltpu.roll(fi, shift, 0)

    def inherit(g, d):
        rows = rows_of(g)
        dr = a8r * d[0] - a8i * d[1]
        di = a8r * d[1] + a8i * d[0]
        xr_ref[rows, :] = xr_ref[rows, :] + dr
        xi_ref[rows, :] = xi_ref[rows, :] + di
        return dr, di

    lax.fori_loop(0, ng, inherit, (jnp.where(row == seeded, 0.0, before_r), jnp.where(row == seeded, 0.0, before_i)))
    out = 0 if reverse else 7
    return (fr[out:out + 1], fi[out:out + 1],
            jnp.where(row == seeded, cr, before_r), jnp.where(row == seeded, ci, before_i))


def _blockdiag(x):
    g, a, b = x.shape
    j = g // SSM_LANE_GROUPS
    eye = jnp.eye(SSM_LANE_GROUPS, dtype=x.dtype)
    y = x.reshape(j, SSM_LANE_GROUPS, a, 1, b) * eye[None, :, None, :, None]
    return y.reshape(j, SSM_LANE_GROUPS * a, SSM_LANE_GROUPS * b)


def _blockdiag_extract(y, a, b):
    j = y.shape[0]
    y = y.reshape(j, SSM_LANE_GROUPS, a, SSM_LANE_GROUPS, b)
    return jnp.einsum("jgahb,gh->jgab", y, jnp.eye(SSM_LANE_GROUPS, dtype=y.dtype)).reshape(j * SSM_LANE_GROUPS, a, b)


def _ssm_disc(lr, li, ldt, brt, bit):
    dt = jnp.exp(ldt)
    mag = jnp.exp(lr * dt)
    ar = mag * jnp.cos(li * dt)
    ai = mag * jnp.sin(li * dt)
    den = lr * lr + li * li
    fr = ((ar - 1.0) * lr + ai * li) / den
    fi = (ai * lr - (ar - 1.0) * li) / den
    bbr = fr[:, None, :] * brt - fi[:, None, :] * bit
    bbi = fr[:, None, :] * bit + fi[:, None, :] * brt
    return ar, ai, bbr, bbi


def _ssm_prep(lr, li, ldt, brt, bit, *, name):
    g, h, p = brt.shape

    def body(lr_ref, li_ref, ldt_ref, brt_ref, bit_ref, ar_ref, ai_ref, bbr_ref, bbi_ref):
        ar, ai, bbr, bbi = _ssm_disc(lr_ref[...], li_ref[...], ldt_ref[...], brt_ref[...], bit_ref[...])
        ar_ref[...] = ar
        ai_ref[...] = ai
        bbr_ref[...] = bbr
        bbi_ref[...] = bbi

    gp = jax.ShapeDtypeStruct((g, p), F32)
    ghp = jax.ShapeDtypeStruct((g, h, p), F32)
    return pl.pallas_call(body, out_shape=(gp, gp, ghp, ghp), name=name)(lr, li, ldt, brt, bit)


def _ssm_prep_bwd(lr, li, ldt, brt, bit, dar, dai, dbbr, dbbi, *, name):
    g, h, p = brt.shape

    def body(lr_ref, li_ref, ldt_ref, brt_ref, bit_ref, dar_ref, dai_ref, dbbr_ref, dbbi_ref,
             dlr_ref, dli_ref, dldt_ref, dbrt_ref, dbit_ref):
        _, vjp = jax.vjp(_ssm_disc, lr_ref[...], li_ref[...], ldt_ref[...], brt_ref[...], bit_ref[...])
        dlr, dli, dldt, dbrt, dbit = vjp((dar_ref[...], dai_ref[...], dbbr_ref[...], dbbi_ref[...]))
        dlr_ref[...] = dlr
        dli_ref[...] = dli
        dldt_ref[...] = dldt
        dbrt_ref[...] = dbrt
        dbit_ref[...] = dbit

    gp = jax.ShapeDtypeStruct((g, p), F32)
    ghp = jax.ShapeDtypeStruct((g, h, p), F32)
    return pl.pallas_call(body, out_shape=(gp, gp, jax.ShapeDtypeStruct((g, 1), F32), ghp, ghp), name=name)(
        lr, li, ldt, brt, bit, dar, dai, dbbr, dbbi)


def _ssm_specs(tc, nlanes, nch, u_colblk, chunk_of):
    return [
        pl.BlockSpec((tc, nch), lambda j, c: (chunk_of(c), u_colblk + j)),
        pl.BlockSpec((1, nlanes), lambda j, c: (0, j)),
        pl.BlockSpec((1, nlanes), lambda j, c: (0, j)),
        pl.BlockSpec((None, nch, nlanes), lambda j, c: (j, 0, 0)),
        pl.BlockSpec((None, nch, nlanes), lambda j, c: (j, 0, 0)),
        pl.BlockSpec((None, nlanes, nch), lambda j, c: (j, 0, 0)),
        pl.BlockSpec((None, nlanes, nch), lambda j, c: (j, 0, 0)),
        pl.BlockSpec((1, nch), lambda j, c: (0, j)),
        pl.BlockSpec((tc, tc), lambda j, c: (0, 0)),
        pl.BlockSpec((tc, tc), lambda j, c: (0, 0)),
    ]


def _ssm_fwd(z, ar, ai, bblk_r, bblk_i, cblk_r, cblk_i, d, *, u_col, tc, name, deps=()):
    L = z.shape[0]
    nj, nch, nlanes = bblk_r.shape
    w = nj * nch
    nc = L // tc
    ng = tc // 8

    assert ng & (ng - 1) == 0
    perm, perm_t = _time_permutation(tc)

    def body(u_ref, ar_ref, ai_ref, br_ref, bi_ref, cr_ref, ci_ref, d_ref, p_ref, pt_ref, *rest):
        y_ref, s0r_ref, s0i_ref, xr_ref, xi_ref, carr_ref, cari_ref = rest[len(deps):]
        c = pl.program_id(1)

        @pl.when(c == 0)
        def _():
            carr_ref[...] = jnp.zeros_like(carr_ref)
            cari_ref[...] = jnp.zeros_like(cari_ref)

        s0r_ref[...] = carr_ref[...]
        s0i_ref[...] = cari_ref[...]
        u = u_ref[...]
        ub = jnp.dot(p_ref[...], u.astype(BF16), preferred_element_type=F32).astype(BF16)
        xr_ref[...] = jnp.dot(ub, br_ref[...].astype(BF16), preferred_element_type=F32)
        xi_ref[...] = jnp.dot(ub, bi_ref[...].astype(BF16), preferred_element_type=F32)
        cr, ci, _, _ = _segment_scan(xr_ref, xi_ref, ar_ref[...], ai_ref[...], carr_ref[...], cari_ref[...], ng, False)
        carr_ref[...] = cr
        cari_ref[...] = ci
        y = (jnp.dot(xr_ref[...].astype(BF16), cr_ref[...].astype(BF16), preferred_element_type=F32)
             - jnp.dot(xi_ref[...].astype(BF16), ci_ref[...].astype(BF16), preferred_element_type=F32))
        y_ref[...] = _unpermute(pt_ref[...], y) + d_ref[...] * u

    state = jax.ShapeDtypeStruct((nc, 1, nj * nlanes), F32)
    state_spec = pl.BlockSpec((None, 1, nlanes), lambda j, c: (c, 0, j))
    return pl.pallas_call(
        body,
        out_shape=(jax.ShapeDtypeStruct((L, w), F32), state, state),
        grid=(nj, nc),
        in_specs=_ssm_specs(tc, nlanes, nch, u_col // nch, lambda c: c) + [pl.BlockSpec(memory_space=pl.ANY)] * len(deps),
        out_specs=(pl.BlockSpec((tc, nch), lambda j, c: (c, j)), state_spec, state_spec),
        scratch_shapes=[pltpu.VMEM((tc, nlanes), F32), pltpu.VMEM((tc, nlanes), F32),
                        pltpu.VMEM((1, nlanes), F32), pltpu.VMEM((1, nlanes), F32)],
        compiler_params=_cparams("parallel", "arbitrary"),
        name=name,
    )(z, ar, ai, bblk_r, bblk_i, cblk_r, cblk_i, d, perm, perm_t, *deps)


def _ssm_bwd(z, dy, s0r, s0i, ar, ai, bblk_r, bblk_i, cblk_r, cblk_i, d, *, u_col, tc, name):
    L = z.shape[0]
    nj, nch, nlanes = bblk_r.shape
    w = nj * nch
    nc = L // tc
    ng = tc // 8
    chunk_of = lambda c: nc - 1 - c
    assert ng & (ng - 1) == 0
    perm, perm_t = _time_permutation(tc)

    def body(u_ref, ar_ref, ai_ref, br_ref, bi_ref, cr_ref, ci_ref, d_ref, p_ref, pt_ref, dy_ref, s0r_ref, s0i_ref,
             du_ref, dbr_ref, dbi_ref, dcr_ref, dci_ref, dar_ref, dai_ref, dd_ref,
             sr_ref, si_ref, lr_ref, li_ref, carr_ref, cari_ref):
        c = pl.program_id(1)

        @pl.when(c == 0)
        def _():
            for ref in (dbr_ref, dbi_ref, dcr_ref, dci_ref, dar_ref, dai_ref, dd_ref, carr_ref, cari_ref):
                ref[...] = jnp.zeros_like(ref)

        u = u_ref[...]
        dyv = dy_ref[...]
        both = jnp.dot(p_ref[...], jnp.concatenate([u.astype(BF16), dyv.astype(BF16)], axis=1), preferred_element_type=F32)
        ub = both[:, :nch].astype(BF16)
        dyb = both[:, nch:].astype(BF16)
        brb = br_ref[...].astype(BF16)
        bib = bi_ref[...].astype(BF16)
        crb = cr_ref[...].astype(BF16)
        cib = ci_ref[...].astype(BF16)
        a_r, a_i = ar_ref[...], ai_ref[...]

        sr_ref[...] = jnp.dot(ub, brb, preferred_element_type=F32)
        si_ref[...] = jnp.dot(ub, bib, preferred_element_type=F32)
        _, _, start_r, start_i = _segment_scan(sr_ref, si_ref, a_r, a_i, s0r_ref[...], s0i_ref[...], ng, False)

        nt = (((1,), (1,)), ((), ()))
        lr_ref[...] = lax.dot_general(dyb, crb, nt, preferred_element_type=F32)
        li_ref[...] = -lax.dot_general(dyb, cib, nt, preferred_element_type=F32)
        cr, ci, _, _ = _segment_scan(lr_ref, li_ref, a_r, -a_i, carr_ref[...], cari_ref[...], ng, True)
        carr_ref[...] = cr
        cari_ref[...] = ci

        def accumulate(g, carry):
            pr, pi, acc_r, acc_i = carry
            rows = pl.ds(pl.multiple_of(g * 8, 8), 8)
            lr, li = lr_ref[rows, :], li_ref[rows, :]
            return sr_ref[rows, :], si_ref[rows, :], acc_r + lr * pr + li * pi, acc_i + li * pr - lr * pi

        zero8 = jnp.zeros((8, nlanes), F32)
        _, _, acc_r, acc_i = lax.fori_loop(0, ng, accumulate, (start_r, start_i, zero8, zero8))
        dar_ref[...] += jnp.sum(acc_r, axis=0, keepdims=True)
        dai_ref[...] += jnp.sum(acc_i, axis=0, keepdims=True)

        tn = (((0,), (0,)), ((), ()))
        lrb = lr_ref[...].astype(BF16)
        lib = li_ref[...].astype(BF16)
        dcr_ref[...] += lax.dot_general(dyb, sr_ref[...].astype(BF16), tn, preferred_element_type=F32)
        dci_ref[...] -= lax.dot_general(dyb, si_ref[...].astype(BF16), tn, preferred_element_type=F32)
        dbr_ref[...] += lax.dot_general(ub, lrb, tn, preferred_element_type=F32)
        dbi_ref[...] += lax.dot_general(ub, lib, tn, preferred_element_type=F32)
        du = (lax.dot_general(lrb, brb, nt, preferred_element_type=F32)
              + lax.dot_general(lib, bib, nt, preferred_element_type=F32))
        du_ref[...] = (_unpermute(pt_ref[...], du) + d_ref[...] * dyv).astype(du_ref.dtype)
        dd_ref[...] += jnp.sum(dyv * u, axis=0, keepdims=True)

    state_spec = pl.BlockSpec((None, 1, nlanes), lambda j, c: (chunk_of(c), 0, j))
    bshape = jax.ShapeDtypeStruct((nj, nch, nlanes), F32)
    ashape = jax.ShapeDtypeStruct((1, nj * nlanes), F32)
    bspec = pl.BlockSpec((None, nch, nlanes), lambda j, c: (j, 0, 0))
    aspec = pl.BlockSpec((1, nlanes), lambda j, c: (0, j))
    big = pltpu.VMEM((tc, nlanes), F32)
    return pl.pallas_call(
        body,
        out_shape=(jax.ShapeDtypeStruct((L, w), BF16), bshape, bshape, bshape, bshape, ashape, ashape,
                   jax.ShapeDtypeStruct((1, w), F32)),
        grid=(nj, nc),
        in_specs=_ssm_specs(tc, nlanes, nch, u_col // nch, chunk_of)
        + [pl.BlockSpec((tc, nch), lambda j, c: (chunk_of(c), j)), state_spec, state_spec],
        out_specs=(pl.BlockSpec((tc, nch), lambda j, c: (chunk_of(c), j)), bspec, bspec, bspec, bspec, aspec, aspec,
                   pl.BlockSpec((1, nch), lambda j, c: (0, j))),
        scratch_shapes=[big, big, big, big, pltpu.VMEM((1, nlanes), F32), pltpu.VMEM((1, nlanes), F32)],
        compiler_params=_cparams("parallel", "arbitrary"),
        name=name,
    )(z, ar, ai, bblk_r, bblk_i, cblk_r, cblk_i, d, perm, perm_t, dy, s0r, s0i)


def _rmsnorm_rows(x, g):
    return x * lax.rsqrt(jnp.mean(x * x, axis=-1, keepdims=True) + RMS_EPS) * g


def _glu_out(y_raw, pre, b):
    yg = jax.nn.gelu(y_raw)
    return yg * jax.nn.sigmoid(pre + b)


def _gate_merge(za, zs, ba, bs, a, bm):
    return jax.nn.sigmoid(za + ba) * a + jax.nn.sigmoid(zs + bs) * bm


def _swiglu(g, u):
    return jax.nn.silu(g) * u


def _row_tile(width_bytes_per_row, rows):
    budget = VMEM_LIMIT_BYTES // 3
    t = max(8, min(1024, budget // (2 * max(width_bytes_per_row, 1))))
    return _pick(rows, t, 16)


def _ssm_params(p, prefix):
    g, pst = p["lam_re"].shape
    ar, ai, bbr, bbi = _ssm_prep(p["lam_re"], p["lam_im"], p["log_dt"], p["b_re_t"], p["b_im_t"], name=prefix + "_ssm_prep")
    return dict(ar=ar.reshape(1, g * pst), ai=ai.reshape(1, g * pst),
                bblk_r=_blockdiag(bbr), bblk_i=_blockdiag(bbi),
                cblk_r=_blockdiag(jnp.swapaxes(p["c_re"], 1, 2)), cblk_i=_blockdiag(jnp.swapaxes(p["c_im"], 1, 2)))


def _layer_fwd(x, p, dims, prefix, deps=(), tick=None):
    tick = tick or (lambda point, arr: ())
    t, d = x.shape
    aw, kvw, sw, ff = dims["aw"], dims["kvw"], dims["sw"], dims["ff"]
    off_u = aw + 2 * kvw
    off_g = off_u + sw
    gblk = _pick(d, 512, 128)
    assert off_g % gblk == 0 and off_u % (SSM_LANE_GROUPS * SSM_GROUP_CH) == 0
    sv = {"x": x}

    h, = _rowmap(_rmsnorm_rows, [(x, "row", d, 0), (p["norm_mix_g"], "vec", d, 0)], [(d, BF16, "row", d)],
                 rows=t, tm=_row_tile(6 * d, t), name=prefix + "_norm_mix", deps=deps)
    deps = tick("norm", h)
    z = _mm_nn(h, p["w_in"], out_dtype=F32, tm=1024, tn=1664, tk=2048, name=prefix + "_mm_in", deps=deps)
    ya = _attn_fwd(z, p["q_norm_g"], p["k_norm_g"], p["attn_sinks"], n_q=dims["n_q"], n_kv=dims["n_kv"],
                   name=prefix + "_attn_fwd", deps=tick("in", z))
    sp = _ssm_params(p, prefix)
    y_raw, s0r, s0i = _ssm_fwd(z, sp["ar"], sp["ai"], sp["bblk_r"], sp["bblk_i"], sp["cblk_r"], sp["cblk_i"], p["ssm_d"],
                               u_col=off_u, tc=dims["tc"], name=prefix + "_ssm_fwd", deps=tick("attn", ya))
    yg, = _rowmap(jax.nn.gelu, [(y_raw, "row", sw, 0)], [(sw, BF16, "row", sw)],
                  rows=t, tm=_row_tile(6 * sw, t), name=prefix + "_gelu", deps=tick("ssm", y_raw))
    pre = _mm_nn(yg, p["ssm_glu_w"], out_dtype=F32, tm=1024, tn=1024, tk=1024, name=prefix + "_mm_glu")
    y2, = _rowmap(_glu_out, [(y_raw, "row", sw, 0), (pre, "row", sw, 0), (p["ssm_glu_b"], "vec", sw, 0)],
                  [(sw, BF16, "row", sw)], rows=t, tm=_row_tile(10 * sw, t), name=prefix + "_glu_out")
    a = _mm_nn(ya, p["w_attn_branch"], out_dtype=BF16, tm=1024, tn=512, tk=1024, name=prefix + "_mm_ab")
    bm = _mm_nn(y2, p["w_ssm_branch"], out_dtype=BF16, tm=1024, tn=512, tk=1024, name=prefix + "_mm_sb")
    ngb = d // gblk
    merged, = _rowmap(
        _gate_merge,
        [(z, "row", gblk, off_g // gblk), (z, "row", gblk, off_g // gblk + ngb),
         (p["gate_bias"], "vec", gblk, 0), (p["gate_bias"], "vec", gblk, ngb),
         (a, "row", gblk, 0), (bm, "row", gblk, 0)],
        [(d, BF16, "row", gblk)], rows=t, tm=_row_tile(18 * gblk, t), ncol=ngb, name=prefix + "_gate")
    x1 = _mm_nn(merged, p["w_out"], out_dtype=F32, tm=512, tn=1024, tk=2048, name=prefix + "_mm_out", add=x)
    h2, = _rowmap(_rmsnorm_rows, [(x1, "row", d, 0), (p["norm_ffn_g"], "vec", d, 0)], [(d, BF16, "row", d)],
                  rows=t, tm=_row_tile(6 * d, t), name=prefix + "_norm_ffn", deps=tick("out", x1))
    gu = _mm_nn(h2, p["w_ffn_in"], out_dtype=BF16, tm=1024, tn=1408, tk=2048, name=prefix + "_mm_ffn_in")
    fblk = _pick(ff, 1408, 128)
    nfb = ff // fblk
    act, = _rowmap(_swiglu, [(gu, "row", fblk, 0), (gu, "row", fblk, nfb)], [(ff, BF16, "row", fblk)],
                   rows=t, tm=_row_tile(10 * fblk, t), ncol=nfb, name=prefix + "_swiglu", deps=tick("ffn_in", gu))
    x2 = _mm_nn(act, p["w_ffn_out"], out_dtype=F32, tm=512, tn=512, tk=5632, name=prefix + "_mm_ffn_out", add=x1,
                deps=tick("act", act))
    sv.update(h=h, z=z, ya=ya, sp=sp, y_raw=y_raw, s0r=s0r, s0i=s0i, yg=yg, pre=pre, y2=y2, a=a, bm=bm,
              merged=merged, x1=x1, h2=h2, gu=gu, act=act)
    return x2, sv


def _layer_bwd(dx2, dx2b, sv, p, dims, prefix, gbuf, deps=(), before_mixer=None, before_in=None):
    t, d = dx2.shape
    aw, kvw, sw, ff = dims["aw"], dims["kvw"], dims["sw"], dims["ff"]
    off_u = aw + 2 * kvw
    off_g = off_u + sw
    gblk = _pick(d, 512, 128)
    ngb = d // gblk
    fblk = _pick(ff, 1408, 128)
    nfb = ff // fblk
    g = {}

    dact = _mm_nt(dx2b, p["w_ffn_out"], out_dtype=BF16, tm=512, tn=2048, tko=1408, name=prefix + "_mm_dact", deps=deps)
    g["w_ffn_out"] = _mm_tn(sv["act"], dx2b, p["w_ffn_out"], into=gbuf.get("w_ffn_out"), tm=4096, tn=1024, tko=512,
                            name=prefix + "_mm_dw_ffn_out")

    def swiglu_bwd(gg, uu, da):
        s = jax.nn.sigmoid(gg)
        gs = gg * s
        return jnp.concatenate([da * uu * (s + gs * (1.0 - s)), da * gs], axis=1)

    dgu, = _rowmap(swiglu_bwd, [(sv["gu"], "row", fblk, 0), (sv["gu"], "row", fblk, nfb), (dact, "row", fblk, 0)],
                   [(2 * ff, BF16, "row", 2 * fblk)],
                   rows=t, tm=_row_tile(16 * fblk, t), ncol=nfb, name=prefix + "_swiglu_bwd")
    dh2 = _mm_nt(dgu, p["w_ffn_in"], out_dtype=F32, tm=256, tn=1408, tko=512, name=prefix + "_mm_dh2", interleaved=fblk)
    g["w_ffn_in"] = _mm_tn(sv["h2"], dgu, p["w_ffn_in"], into=gbuf.get("w_ffn_in"), tm=4096, tn=fblk, tko=512,
                           name=prefix + "_mm_dw_ffn_in", interleaved=True)

    def norm_bwd(xx, gg, dh, dres):
        _, vjp = jax.vjp(_rmsnorm_rows, xx, gg)
        dxx, dgg = vjp(dh)
        dxx = dxx + dres
        return dxx, dxx, dgg

    dx1, dx1b, g["norm_ffn_g"] = _rowmap(
        norm_bwd, [(sv["x1"], "row", d, 0), (p["norm_ffn_g"], "vec", d, 0), (dh2, "row", d, 0), (dx2, "row", d, 0)],
        [(d, F32, "row", d), (d, BF16, "row", d), (d, F32, "acc", d)],
        rows=t, tm=_row_tile(22 * d, t), name=prefix + "_norm_ffn_bwd")

    deps = before_mixer(dx1, g) if before_mixer else ()
    dmerged = _mm_nt(dx1b, p["w_out"], out_dtype=BF16, tm=1024, tn=2048, tko=1024, name=prefix + "_mm_dmerged", deps=deps)
    g["w_out"] = _mm_tn(sv["merged"], dx1b, p["w_out"], into=gbuf.get("w_out"), tm=4096, tn=1024, tko=512,
                        name=prefix + "_mm_dw_out")

    def gate_bwd(za, zs, ba, bs, aa, bb, dm):
        sa = jax.nn.sigmoid(za + ba)
        ss = jax.nn.sigmoid(zs + bs)
        daa, dbb = dm * sa, dm * ss
        dza, dzs = daa * aa * (1.0 - sa), dbb * bb * (1.0 - ss)
        return daa, dbb, dza, dzs, jnp.sum(dza, axis=0, keepdims=True), jnp.sum(dzs, axis=0, keepdims=True)

    z = sv["z"]
    da, dbm, dza, dzs, dba, dbs = _rowmap(
        gate_bwd,
        [(z, "row", gblk, off_g // gblk), (z, "row", gblk, off_g // gblk + ngb),
         (p["gate_bias"], "vec", gblk, 0), (p["gate_bias"], "vec", gblk, ngb),
         (sv["a"], "row", gblk, 0), (sv["bm"], "row", gblk, 0), (dmerged, "row", gblk, 0)],
        [(d, BF16, "row", gblk), (d, BF16, "row", gblk), (d, BF16, "row", gblk), (d, BF16, "row", gblk),
         (d, F32, "acc", gblk), (d, F32, "acc", gblk)],
        rows=t, tm=_row_tile(32 * gblk, t), ncol=ngb, name=prefix + "_gate_bwd")
    g["gate_bias"] = jnp.concatenate([dba, dbs], axis=1)
    dya = _mm_nt(da, p["w_attn_branch"], out_dtype=BF16, tm=1024, tn=512, tko=1024, name=prefix + "_mm_dya")
    g["w_attn_branch"] = _mm_tn(sv["ya"], da, p["w_attn_branch"], into=gbuf.get("w_attn_branch"), tm=4096, tn=512,
                                tko=512, name=prefix + "_mm_dw_ab")
    dy2 = _mm_nt(dbm, p["w_ssm_branch"], out_dtype=BF16, tm=1024, tn=512, tko=1024, name=prefix + "_mm_dy2")
    g["w_ssm_branch"] = _mm_tn(sv["y2"], dbm, p["w_ssm_branch"], into=gbuf.get("w_ssm_branch"), tm=4096, tn=512,
                               tko=512, name=prefix + "_mm_dw_sb")

    def glu_bwd(y_raw, pre, b, dy):
        yg = jax.nn.gelu(y_raw)
        _, vjp = jax.vjp(lambda a_, b_, c_: a_ * jax.nn.sigmoid(b_ + c_), yg, pre, b)
        dyg, dpre, db = vjp(dy)
        return dyg, dpre, db

    dyg_direct, dpre, g["ssm_glu_b"] = _rowmap(
        glu_bwd, [(sv["y_raw"], "row", sw, 0), (sv["pre"], "row", sw, 0), (p["ssm_glu_b"], "vec", sw, 0), (dy2, "row", sw, 0)],
        [(sw, F32, "row", sw), (sw, BF16, "row", sw), (sw, F32, "acc", sw)],
        rows=t, tm=_row_tile(24 * sw, t), name=prefix + "_glu_bwd")
    dyg2 = _mm_nt(dpre, p["ssm_glu_w"], out_dtype=F32, tm=1024, tn=1024, tko=1024, name=prefix + "_mm_dyg")
    g["ssm_glu_w"] = _mm_tn(sv["yg"], dpre, p["ssm_glu_w"], into=gbuf.get("ssm_glu_w"), tm=4096, tn=1024, tko=512,
                            name=prefix + "_mm_dw_glu")

    def gelu_bwd(y_raw, d1, d2):
        _, vjp = jax.vjp(jax.nn.gelu, y_raw)
        return vjp(d1 + d2)[0]

    dy_raw, = _rowmap(gelu_bwd, [(sv["y_raw"], "row", sw, 0), (dyg_direct, "row", sw, 0), (dyg2, "row", sw, 0)],
                      [(sw, F32, "row", sw)], rows=t, tm=_row_tile(20 * sw, t), name=prefix + "_gelu_bwd")
    sp = sv["sp"]
    du, dbr, dbi, dcr, dci, dar, dai, g["ssm_d"] = _ssm_bwd(
        z, dy_raw, sv["s0r"], sv["s0i"], sp["ar"], sp["ai"], sp["bblk_r"], sp["bblk_i"], sp["cblk_r"], sp["cblk_i"],
        p["ssm_d"], u_col=off_u, tc=dims["tc"], name=prefix + "_ssm_bwd")
    ngr, pst = p["lam_re"].shape
    hch = SSM_GROUP_CH
    dlr, dli, dldt, dbrt, dbit = _ssm_prep_bwd(
        p["lam_re"], p["lam_im"], p["log_dt"], p["b_re_t"], p["b_im_t"],
        dar.reshape(ngr, pst), dai.reshape(ngr, pst), _blockdiag_extract(dbr, hch, pst), _blockdiag_extract(dbi, hch, pst),
        name=prefix + "_ssm_prep_bwd")
    g.update(ssm_lambda_re=dlr, ssm_lambda_im=dli, ssm_log_dt=dldt.reshape(ngr),
             ssm_b_re=jnp.swapaxes(dbrt, 1, 2), ssm_b_im=jnp.swapaxes(dbit, 1, 2),
             ssm_c_re=_blockdiag_extract(dcr, hch, pst), ssm_c_im=_blockdiag_extract(dci, hch, pst))

    dq, dkv, g["q_norm_g"], g["k_norm_g"], g["attn_sinks"] = _attn_bwd(
        z, dya, p["q_norm_g"], p["k_norm_g"], p["attn_sinks"], n_q=dims["n_q"], n_kv=dims["n_kv"], name=prefix + "_attn_bwd")

    dz = jnp.concatenate([dq, dkv, du, dza, dzs], axis=1)
    deps = before_in(dq, g) if before_in else ()
    dh = _mm_nt(dz, p["w_in"], out_dtype=F32, tm=512, tn=1664, tko=512, name=prefix + "_mm_dh", deps=deps)
    g["w_in"] = _mm_tn(sv["h"], dz, p["w_in"], into=gbuf.get("w_in"), tm=4096, tn=1664, tko=512,
                       name=prefix + "_mm_dw_in")
    dx, dxb, g["norm_mix_g"] = _rowmap(
        norm_bwd, [(sv["x"], "row", d, 0), (p["norm_mix_g"], "vec", d, 0), (dh, "row", d, 0), (dx1, "row", d, 0)],
        [(d, F32, "row", d), (d, BF16, "row", d), (d, F32, "acc", d)],
        rows=t, tm=_row_tile(22 * d, t), name=prefix + "_norm_mix_bwd")
    return dx, dxb, g


def _loss_and_grad(y, target):
    t, d = y.shape

    def fn(yy, tt):
        e = yy - tt
        dy = e * (1.0 / d)
        return dy, dy, jnp.sum(e * e, keepdims=True).reshape(1, 1)

    dy, dyb, sq = _rowmap(fn, [(y, "row", d, 0), (target, "row", d, 0)],
                          [(d, F32, "row", d), (d, BF16, "row", d), (1, F32, "acc", 1)],
                          rows=t, tm=_row_tile(14 * d, t), name="loss")
    return sq, dy, dyb


def _local_step(x, target, n_layers, weights_of, dims, fwd_tick=None, bwd_tick=None):
    saved, params = [], []
    h = x
    for l in range(n_layers):
        p, deps = weights_of(l, h)
        params.append(p)
        tick = (lambda point, arr, l=l: fwd_tick(l, point, arr)) if fwd_tick else None
        h, sv = _layer_fwd(h, p, dims, "l%d" % l, deps, tick)
        saved.append(sv)
    sq, dy, dyb = _loss_and_grad(h, target)
    grads = [None] * n_layers
    deps = ()
    for l in reversed(range(n_layers)):
        if bwd_tick:
            ffn_done = lambda arr, g, l=l: bwd_tick(l, "ffn", (arr, g["w_ffn_in"], g["w_ffn_out"]),
                                                    {n: g[n] for n in ("w_ffn_in", "w_ffn_out")})
            mixer_done = lambda arr, g, l=l: bwd_tick(l, "mixer", (arr, g["ssm_d"]) + tuple(g[n] for n in MIXER_WEIGHTS),
                                                      {n: g[n] for n in MIXER_WEIGHTS})
        else:
            ffn_done = mixer_done = None
        dy, dyb, grads[l] = _layer_bwd(dy, dyb, saved[l], params[l], dims, "l%d" % l, {}, deps=deps,
                                       before_mixer=ffn_done, before_in=mixer_done)
        if bwd_tick:
            deps = bwd_tick(l, "in", (dy, grads[l]["w_in"]), {"w_in": grads[l]["w_in"]})
    return sq, dy, grads, deps


COL_SHARDED = ("w_in", "w_attn_branch", "w_ssm_branch", "w_ffn_in")
ROW_SHARDED = ("ssm_glu_w", "w_out", "w_ffn_out")
BIG_WEIGHTS = COL_SHARDED + ROW_SHARDED
WEIGHT_NAMES = ("norm_mix_g", "w_in", "gate_bias", "q_norm_g", "k_norm_g", "attn_sinks", "ssm_lambda_re",
                "ssm_lambda_im", "ssm_log_dt", "ssm_b_re", "ssm_b_im", "ssm_c_re", "ssm_c_im", "ssm_d", "ssm_glu_w",
                "ssm_glu_b", "w_attn_branch", "w_ssm_branch", "w_out", "norm_ffn_g", "w_ffn_in", "w_ffn_out")
SMALL_WEIGHTS = tuple(n for n in WEIGHT_NAMES if n not in BIG_WEIGHTS)
MIXER_WEIGHTS = ("w_out", "w_attn_branch", "w_ssm_branch", "ssm_glu_w")
WEIGHT_GROUPS = {"in": ("w_in",), "mixer": MIXER_WEIGHTS, "ffn": ("w_ffn_in", "w_ffn_out")}


def _dims(d, shapes, tc):
    s, _, aw, _ = shapes["w_attn_branch"]
    sw = shapes["w_ssm_branch"][2]
    in_w = shapes["w_in"][3] * s
    kvw = (in_w - aw - sw - 2 * d) // 2
    ff = shapes["w_ffn_out"][2] * s
    return dict(aw=aw, kvw=kvw, sw=sw, ff=ff, n_q=aw // HEAD_DIM, n_kv=kvw // HEAD_DIM, tc=tc)


def _big_params(big):
    p = {n: _Weight(a, 0, "col") for n, a in big.items() if n in COL_SHARDED}
    p.update({n: _Weight(a.reshape(1, 1, -1, a.shape[-1]), 0, "col") for n, a in big.items() if n in ROW_SHARDED})
    return p


def _layer_params(l, small):
    p = {}
    for n in ("norm_mix_g", "gate_bias", "q_norm_g", "k_norm_g", "ssm_d", "ssm_glu_b", "norm_ffn_g"):
        p[n] = small[n][l][None]
    p["attn_sinks"] = small["attn_sinks"][l]
    p["lam_re"] = small["ssm_lambda_re"][l]
    p["lam_im"] = small["ssm_lambda_im"][l]
    p["log_dt"] = small["ssm_log_dt"][l][:, None]
    p["b_re_t"] = jnp.swapaxes(small["ssm_b_re"][l], 1, 2)
    p["b_im_t"] = jnp.swapaxes(small["ssm_b_im"][l], 1, 2)
    p["c_re"] = small["ssm_c_re"][l]
    p["c_im"] = small["ssm_c_im"][l]
    return p


_ANY = pl.BlockSpec(memory_space=pl.ANY)
_MESH_ID = pl.DeviceIdType.MESH


def _coords():
    return lax.axis_index("x"), lax.axis_index("y"), lax.axis_index("c")


def _my_chip():
    return (2 * lax.axis_index("x") + lax.axis_index("y")).astype(jnp.int32).reshape(1)


def _my_core():
    return lax.axis_index("c").astype(jnp.int32).reshape(1)


def _cast_into_slot(w, layer, *, name, deps=()):
    _, r, c = w.shape
    tm = _row_tile(12 * c, r)

    def body(me_ref, w_ref, *rest):
        rest[-1][...] = w_ref[...].astype(rest[-1].dtype)

    return pl.pallas_call(
        body,
        out_shape=jax.ShapeDtypeStruct((N_CHIPS, 1, r, c), BF16),
        grid_spec=pltpu.PrefetchScalarGridSpec(
            num_scalar_prefetch=1,
            grid=(r // tm,),
            in_specs=[pl.BlockSpec((None, tm, c), lambda i, me: (layer, i, 0))]
            + [pl.BlockSpec(memory_space=pl.ANY)] * len(deps),
            out_specs=pl.BlockSpec((None, None, tm, c), lambda i, me: (me[0], 0, i, 0)),
        ),
        compiler_params=_cparams("parallel"),
        name=name,
    )(_my_chip(), w, *deps)


_HBM = pl.BlockSpec(memory_space=pltpu.HBM)
_SEM = pl.BlockSpec(memory_space=pltpu.SEMAPHORE)
_DATAFLOW = pltpu.SideEffectType.DATAFLOW_SIDE_EFFECTING


class _SplitExchange:
    def __init__(self, srcs, lands, build, n_copies, name):
        self.build, self.n, self.name = build, n_copies, name
        self.ns, self.nl = len(srcs), len(lands)
        self.bufs = [pltpu.with_memory_space_constraint(a, pltpu.HBM) for a in list(srcs) + list(lands)]

    def _copies(self, refs, send_sems, recv_sems):
        triples = self.build(refs[:self.ns], refs[self.ns:self.ns + self.nl])
        assert len(triples) == self.n
        return [pltpu.make_async_remote_copy(src_ref=s, dst_ref=d, send_sem=send_sems.at[k], recv_sem=recv_sems.at[k],
                                             device_id=to, device_id_type=_MESH_ID) for k, (s, d, to) in enumerate(triples)]

    def start(self, deps=()):
        nb = self.ns + self.nl

        def body(*refs):
            outs = refs[nb + len(deps):]
            for cp in self._copies(refs, outs[0], outs[1]):
                cp.start()
            outs[-1][...] = jnp.zeros_like(outs[-1])

        sems = pltpu.SemaphoreType.DMA((self.n,))
        res = pl.pallas_call(
            body,
            out_shape=(sems, sems, *[pltpu.HBM(b.shape, b.dtype) for b in self.bufs], jax.ShapeDtypeStruct((8, 128), F32)),
            in_specs=[_HBM] * nb + [_ANY] * len(deps),
            out_specs=(_SEM, _SEM, *[_HBM] * nb, pl.BlockSpec(memory_space=pltpu.VMEM)),
            input_output_aliases={i: 2 + i for i in range(nb)},
            compiler_params=pltpu.CompilerParams(has_side_effects=_DATAFLOW),
            name=self.name + "_start",
        )(*self.bufs, *deps)
        self.send_sems, self.recv_sems = res[0], res[1]
        self.bufs = list(res[2:2 + nb])
        return res[-1]

    def wait(self, after=()):
        nb = self.ns + self.nl
        after = tuple(after) if isinstance(after, (tuple, list)) else (after,)

        def body(*refs):
            for cp in self._copies(refs, refs[nb], refs[nb + 1]):
                cp.wait_send()
                cp.wait_recv()

        res = pl.pallas_call(
            body,
            out_shape=tuple(pltpu.HBM(b.shape, b.dtype) for b in self.bufs),
            in_specs=[_HBM] * nb + [_SEM, _SEM] + [_ANY] * len(after),
            out_specs=tuple([_HBM] * nb),
            input_output_aliases={i: i for i in range(nb)},
            compiler_params=pltpu.CompilerParams(has_side_effects=_DATAFLOW),
            name=self.name + "_wait",
        )(*self.bufs, self.send_sems, self.recv_sems, *after)
        res = list(res)
        return res[:self.ns], res[self.ns:]


def _other_chips(x, y):
    return [(1 - x, y), (x, 1 - y), (1 - x, 1 - y)]


def _gather_steps(bufs, tag, deps, publish):
    n = len(bufs)
    half = lambda ref, i, slot, hc: ref.at[slot, :, pl.ds(hc * (bufs[i].shape[2] // 2), bufs[i].shape[2] // 2), :]

    def over_ici(srcs, lands):
        x, y, c = _coords()
        me = 2 * x + y
        return [(half(srcs[i], i, me, c), half(srcs[i], i, me, c), (px, py, c))
                for i in range(n) for px, py in _other_chips(x, y)]

    def to_sibling(srcs, lands):
        x, y, c = _coords()
        return [(half(srcs[i], i, 2 * px + py, c), half(srcs[i], i, 2 * px + py, c), (x, y, 1 - c))
                for i in range(n) for px, py in _other_chips(x, y)]

    ex = _SplitExchange(bufs, [], over_ici, 3 * n, tag + "_ici")
    after = yield ex.start(deps)
    bufs, _ = ex.wait(after)
    ex = _SplitExchange(bufs, [], to_sibling, 3 * n, tag + "_d2d")
    after = yield ex.start()
    bufs, _ = ex.wait(after)
    publish(bufs)


def _reduce_steps(grads, tag, publish):
    n = len(grads)
    rh = [g.shape[1] // 2 for g in grads]
    theirs = [lax.empty((g.shape[0], g.shape[1] // 2, g.shape[2]), F32) for g in grads]

    def halves(srcs, lands):
        x, y, c = _coords()
        return [(srcs[i].at[:, pl.ds((1 - c) * rh[i], rh[i]), :], lands[i], (x, y, 1 - c)) for i in range(n)]

    def chips(srcs, lands):
        x, y, c = _coords()
        me = 2 * x + y
        return [(srcs[i].at[2 * px + py], lands[i].at[me], (px, py, c)) for i in range(n) for px, py in _other_chips(x, y)]

    def sibling(srcs, lands):
        x, y, c = _coords()
        return [(srcs[i], lands[i], (x, y, 1 - c)) for i in range(n)]

    ex = _SplitExchange(grads, theirs, halves, n, tag + "_halves")
    after = yield ex.start()
    grads, theirs = ex.wait(after)
    parts = [_add_own_half(g, t, name="%s_add_own_half_%d" % (tag, i)) for i, (g, t) in enumerate(zip(grads, theirs))]
    ex = _SplitExchange(parts, [lax.empty(p.shape, p.dtype) for p in parts], chips, 3 * n, tag + "_chips")
    after = yield ex.start()
    parts, got = ex.wait(after)
    mine = [_sum_chips(p, g, name="%s_sum_chips_%d" % (tag, i)) for i, (p, g) in enumerate(zip(parts, got))]
    ex = _SplitExchange(mine, [lax.empty(m.shape, m.dtype) for m in mine], sibling, n, tag + "_sibling")
    after = yield ex.start()
    mine, theirs = ex.wait(after)
    publish(list(zip(mine, theirs)))


def _allreduce_steps(buf, tag, publish):
    r, c = buf.shape

    def to_sibling(srcs, lands):
        x, y, cc = _coords()
        return [(srcs[0], lands[0], (x, y, 1 - cc))]

    def over_ici(srcs, lands):
        x, y, cc = _coords()
        me = 2 * x + y
        return [(srcs[0].at[me], srcs[0].at[me], (px, py, cc)) for px, py in _other_chips(x, y)]

    def halves(srcs, lands):
        x, y, cc = _coords()
        return [(srcs[0].at[cc], srcs[0].at[cc], (x, y, 1 - cc))]

    ex = _SplitExchange([buf], [lax.empty(buf.shape, buf.dtype)], to_sibling, 1, tag + "_cores")
    after = yield ex.start()
    (mine,), (theirs,) = ex.wait(after)
    ex = _SplitExchange([_add_half_into_slot(mine, theirs, name=tag + "_chip_sum")], [], over_ici, N_CHIPS - 1, tag + "_chips")
    after = yield ex.start()
    (parts,), _ = ex.wait(after)
    total_half = _sum_slots(parts, name=tag + "_sum_chips")
    ex = _SplitExchange([_place_into_slot(total_half, 2, _my_core(), name=tag + "_place_half")], [], halves, 1, tag + "_halves")
    after = yield ex.start()
    (both,), _ = ex.wait(after)
    publish(both.reshape(r, c))


class _Exchanges:
    def __init__(self):
        self.running = []

    def launch(self, steps):
        self.running.append(steps)
        return next(steps)

    def advance(self, steps, after):
        try:
            return steps.send(after)
        except StopIteration:
            self.running.remove(steps)
            return None

    def advance_all(self, after):
        tokens = [self.advance(steps, after) for steps in list(self.running)]
        return tuple(t for t in tokens if t is not None)


def _add_own_half(g, theirs, *, name):
    s, r, c = g.shape
    rh = r // 2
    tm = _row_tile(10 * c, rh)
    nb = rh // tm

    def body(core_ref, g_ref, t_ref, o_ref):
        o_ref[...] = (g_ref[...] + t_ref[...]).astype(o_ref.dtype)

    return pl.pallas_call(
        body,
        out_shape=jax.ShapeDtypeStruct((s, rh, c), BF16),
        grid_spec=pltpu.PrefetchScalarGridSpec(
            num_scalar_prefetch=1,
            grid=(s, nb),
            in_specs=[pl.BlockSpec((None, tm, c), lambda k, i, core: (k, core[0] * nb + i, 0)),
                      pl.BlockSpec((None, tm, c), lambda k, i, core: (k, i, 0))],
            out_specs=pl.BlockSpec((None, tm, c), lambda k, i, core: (k, i, 0)),
        ),
        compiler_params=_cparams("parallel", "parallel"),
        name=name,
    )(_my_core(), g, theirs)


def _sum_chips(part, got, *, name):
    s, rh, c = part.shape
    tm = _row_tile(14 * c, rh)

    def body(me_ref, p_ref, a_ref, b_ref, c_ref, o_ref):
        o_ref[...] = ((p_ref[...].astype(F32) + a_ref[...].astype(F32)) + b_ref[...].astype(F32)) + c_ref[...].astype(F32)

    slot = lambda k: (lambda i, me: ((me[0] + k) % s, i, 0))
    return pl.pallas_call(
        body,
        out_shape=jax.ShapeDtypeStruct((rh, c), F32),
        grid_spec=pltpu.PrefetchScalarGridSpec(
            num_scalar_prefetch=1,
            grid=(rh // tm,),
            in_specs=[pl.BlockSpec((None, tm, c), slot(k)) for k in range(s)],
            out_specs=pl.BlockSpec((tm, c), lambda i, me: (i, 0)),
        ),
        compiler_params=_cparams("parallel"),
        name=name,
    )(_my_chip(), part, got, got, got)


def _place_into_slot(buf, n_slots, slot, *, name):
    r, c = buf.shape
    tm = _row_tile(8 * c, r)

    def body(slot_ref, i_ref, o_ref):
        o_ref[...] = i_ref[...]

    return pl.pallas_call(
        body,
        out_shape=jax.ShapeDtypeStruct((n_slots, r, c), buf.dtype),
        grid_spec=pltpu.PrefetchScalarGridSpec(
            num_scalar_prefetch=1,
            grid=(r // tm,),
            in_specs=[pl.BlockSpec((tm, c), lambda i, s: (i, 0))],
            out_specs=pl.BlockSpec((None, tm, c), lambda i, s: (s[0], i, 0)),
        ),
        compiler_params=_cparams("parallel"),
        name=name,
    )(slot, buf)


def _add_half_into_slot(mine, theirs, *, name):
    r, c = mine.shape
    rh = r // 2
    tm = _row_tile(12 * c, rh)
    nb = rh // tm
    where = jnp.concatenate([_my_chip(), _my_core()])

    def body(where_ref, a_ref, b_ref, o_ref):
        o_ref[...] = a_ref[...] + b_ref[...]

    half = pl.BlockSpec((tm, c), lambda i, w: (w[1] * nb + i, 0))
    return pl.pallas_call(
        body,
        out_shape=jax.ShapeDtypeStruct((N_CHIPS, rh, c), mine.dtype),
        grid_spec=pltpu.PrefetchScalarGridSpec(
            num_scalar_prefetch=1,
            grid=(nb,),
            in_specs=[half, half],
            out_specs=pl.BlockSpec((None, tm, c), lambda i, w: (w[0], i, 0)),
        ),
        compiler_params=_cparams("parallel"),
        name=name,
    )(where, mine, theirs)


def _sum_slots(arr, *, name):
    s, r, c = arr.shape
    tm = _row_tile(4 * c * (s + 1), r)

    def body(*refs):
        acc = refs[0][...]
        for ref in refs[1:s]:
            acc = acc + ref[...]
        refs[s][...] = acc

    return pl.pallas_call(
        body,
        out_shape=jax.ShapeDtypeStruct((r, c), arr.dtype),
        grid=(r // tm,),
        in_specs=[pl.BlockSpec((None, tm, c), lambda i, k=k: (k, i, 0)) for k in range(s)],
        out_specs=pl.BlockSpec((tm, c), lambda i: (i, 0)),
        compiler_params=_cparams("parallel"),
        name=name,
    )(*([arr] * s))


def _adamw_fn(w, g, m, v):
    m = ADAM_B1 * m + (1.0 - ADAM_B1) * g
    v = ADAM_B2 * v + (1.0 - ADAM_B2) * jnp.square(g)
    m_hat = m / (1.0 - ADAM_B1 ** ADAM_STEP)
    v_hat = v / (1.0 - ADAM_B2 ** ADAM_STEP)
    delta = -ADAM_LR * (m_hat / (jnp.sqrt(v_hat) + ADAM_EPS) + ADAM_WD * w)
    return delta, m, v


def _adamw(w, g, m, v, *, name):
    rows, cols = w.shape
    ins = [(a, "row", cols, 0) for a in (w, g, m, v)]
    outs = [(cols, F32, "row", cols)] * 3
    return _rowmap(_adamw_fn, ins, outs, rows=rows, tm=_row_tile(56 * cols, rows), name=name)


def _adamw_sharded(w, m, v, g_mine, g_sibling, layer, into, *, name, deps=()):
    nl, r, c = w.shape
    rh = r // 2
    tm = _row_tile(40 * c, rh)
    nb = rh // tm
    n_into = 0 if into is None else 4

    def body(core_ref, w_ref, m_ref, v_ref, a_ref, b_ref, *rest):
        g_ref, d_ref, nm_ref, nv_ref = rest[n_into + len(deps):]
        g = jnp.where(pl.program_id(0) == core_ref[0], a_ref[...], b_ref[...])
        delta, nm, nv = _adamw_fn(w_ref[...], g, m_ref[...], v_ref[...])
        g_ref[...] = g
        d_ref[...] = delta
        nm_ref[...] = nm
        nv_ref[...] = nv

    whole = pl.BlockSpec((None, tm, c), lambda h, i, core: (layer, h * nb + i, 0))
    half = pl.BlockSpec((tm, c), lambda h, i, core: (i, 0))
    shape = jax.ShapeDtypeStruct((nl, r, c), F32)
    return pl.pallas_call(
        body,
        out_shape=(shape, shape, shape, shape),
        grid_spec=pltpu.PrefetchScalarGridSpec(
            num_scalar_prefetch=1,
            grid=(2, nb),
            in_specs=[whole, whole, whole, half, half] + [pl.BlockSpec(memory_space=pl.ANY)] * (n_into + len(deps)),
            out_specs=(whole, whole, whole, whole),
        ),
        input_output_aliases={6 + k: k for k in range(n_into)},
        compiler_params=_cparams("parallel", "parallel"),
        name=name,
    )(_my_core(), w, m, v, g_mine, g_sibling, *(into or ()), *deps)


def _pack(arrays):
    flat = jnp.concatenate([a.reshape(-1) for a in arrays])
    pad = (-flat.shape[0]) % (256 * 128)
    return jnp.pad(flat, (0, pad)).reshape(-1, 128)


def _unpack(buf, shapes):
    flat = buf.reshape(-1)
    out, off = [], 0
    for s in shapes:
        n = math.prod(s)
        out.append(flat[off:off + n].reshape(s))
        off += n
    return out


def kernel(x, norm_mix_g, w_in, gate_bias, q_norm_g, k_norm_g, attn_sinks, ssm_lambda_re, ssm_lambda_im, ssm_log_dt, ssm_b_re, ssm_b_im, ssm_c_re, ssm_c_im, ssm_d, ssm_glu_w, ssm_glu_b, w_attn_branch, w_ssm_branch, w_out, norm_ffn_g, w_ffn_in, w_ffn_out, loss_target, m_norm_mix_g, m_w_in, m_gate_bias, m_q_norm_g, m_k_norm_g, m_attn_sinks, m_ssm_lambda_re, m_ssm_lambda_im, m_ssm_log_dt, m_ssm_b_re, m_ssm_b_im, m_ssm_c_re, m_ssm_c_im, m_ssm_d, m_ssm_glu_w, m_ssm_glu_b, m_w_attn_branch, m_w_ssm_branch, m_w_out, m_norm_ffn_g, m_w_ffn_in, m_w_ffn_out, v_norm_mix_g, v_w_in, v_gate_bias, v_q_norm_g, v_k_norm_g, v_attn_sinks, v_ssm_lambda_re, v_ssm_lambda_im, v_ssm_log_dt, v_ssm_b_re, v_ssm_b_im, v_ssm_c_re, v_ssm_c_im, v_ssm_d, v_ssm_glu_w, v_ssm_glu_b, v_w_attn_branch, v_w_ssm_branch, v_w_out, v_norm_ffn_g, v_w_ffn_in, v_w_ffn_out):
    w = dict(norm_mix_g=norm_mix_g, w_in=w_in, gate_bias=gate_bias, q_norm_g=q_norm_g, k_norm_g=k_norm_g,
             attn_sinks=attn_sinks, ssm_lambda_re=ssm_lambda_re, ssm_lambda_im=ssm_lambda_im, ssm_log_dt=ssm_log_dt,
             ssm_b_re=ssm_b_re, ssm_b_im=ssm_b_im, ssm_c_re=ssm_c_re, ssm_c_im=ssm_c_im, ssm_d=ssm_d,
             ssm_glu_w=ssm_glu_w, ssm_glu_b=ssm_glu_b, w_attn_branch=w_attn_branch, w_ssm_branch=w_ssm_branch,
             w_out=w_out, norm_ffn_g=norm_ffn_g, w_ffn_in=w_ffn_in, w_ffn_out=w_ffn_out)
    m = dict(norm_mix_g=m_norm_mix_g, w_in=m_w_in, gate_bias=m_gate_bias, q_norm_g=m_q_norm_g, k_norm_g=m_k_norm_g,
             attn_sinks=m_attn_sinks, ssm_lambda_re=m_ssm_lambda_re, ssm_lambda_im=m_ssm_lambda_im,
             ssm_log_dt=m_ssm_log_dt, ssm_b_re=m_ssm_b_re, ssm_b_im=m_ssm_b_im, ssm_c_re=m_ssm_c_re,
             ssm_c_im=m_ssm_c_im, ssm_d=m_ssm_d, ssm_glu_w=m_ssm_glu_w, ssm_glu_b=m_ssm_glu_b,
             w_attn_branch=m_w_attn_branch, w_ssm_branch=m_w_ssm_branch, w_out=m_w_out, norm_ffn_g=m_norm_ffn_g,
             w_ffn_in=m_w_ffn_in, w_ffn_out=m_w_ffn_out)
    v = dict(norm_mix_g=v_norm_mix_g, w_in=v_w_in, gate_bias=v_gate_bias, q_norm_g=v_q_norm_g, k_norm_g=v_k_norm_g,
             attn_sinks=v_attn_sinks, ssm_lambda_re=v_ssm_lambda_re, ssm_lambda_im=v_ssm_lambda_im,
             ssm_log_dt=v_ssm_log_dt, ssm_b_re=v_ssm_b_re, ssm_b_im=v_ssm_b_im, ssm_c_re=v_ssm_c_re,
             ssm_c_im=v_ssm_c_im, ssm_d=v_ssm_d, ssm_glu_w=v_ssm_glu_w, ssm_glu_b=v_ssm_glu_b,
             w_attn_branch=v_w_attn_branch, w_ssm_branch=v_w_ssm_branch, w_out=v_w_out, norm_ffn_g=v_norm_ffn_g,
             w_ffn_in=v_w_ffn_in, w_ffn_out=v_w_ffn_out)
    n_layers = norm_mix_g.shape[0]
    d_model = x.shape[-1]
    seq = x.shape[1]

    exchanges = _Exchanges()
    params = [_layer_params(l, w) for l in range(n_layers)]
    gathers = {}

    def gather(l, group, bufs, deps=()):
        names = WEIGHT_GROUPS[group]
        steps = _gather_steps(bufs, "ag%d_%s" % (l, group), deps,
                              lambda got: params[l].update(_big_params(dict(zip(names, got)))))
        gathers[l, group] = steps
        return exchanges.launch(steps)

    started = gather(0, "in", [_cast_into_slot(w["w_in"], 0, name="cast0_w_in")])
    casts = {(l, group): [_cast_into_slot(w[n], l, name="cast%d_%s" % (l, n), deps=(started,)) for n in WEIGHT_GROUPS[group]]
             for l in range(n_layers) for group in WEIGHT_GROUPS if (l, group) != (0, "in")}
    forwarded = exchanges.advance(gathers[0, "in"], tuple(b for bufs in casts.values() for b in bufs))
    first_deps = (forwarded, gather(0, "mixer", casts[0, "mixer"], (forwarded,)), gather(0, "ffn", casts[0, "ffn"], (forwarded,)))
    sizes = {n: (N_CHIPS, 1) + w[n].shape[1:] for n in BIG_WEIGHTS}
    dims = _dims(d_model, sizes, min(512, seq))

    def weights_of(l, h):
        if l == 0:
            return params[0], first_deps
        for group in WEIGHT_GROUPS:
            exchanges.advance(gathers[l, group], h)
        return params[l], ()

    def fwd_tick(l, point, arr):
        tokens = []
        if l == 0 and point == "norm":
            tokens.append(exchanges.advance(gathers[0, "in"], arr))
        if l == 0 and point in ("in", "attn"):
            tokens.append(exchanges.advance(gathers[0, "mixer"], arr))
        if l == 0 and point in ("ssm", "out"):
            tokens.append(exchanges.advance(gathers[0, "ffn"], arr))
        if l + 1 < n_layers and point == "attn":
            tokens += [gather(l + 1, group, casts[l + 1, group]) for group in WEIGHT_GROUPS]
        if l + 1 < n_layers and point == "act":
            tokens += [exchanges.advance(gathers[l + 1, group], arr) for group in WEIGHT_GROUPS]
        return tuple(t for t in tokens if t is not None)

    reduced, ready = {}, []

    def bwd_tick(l, stage, arr, stage_grads):
        tokens = exchanges.advance_all(arr)
        names = tuple(stage_grads)

        def publish(halves):
            reduced.update({(l, n): h for n, h in zip(names, halves)})
            ready.append((l, names))

        grads4 = [stage_grads[n].reshape(N_CHIPS, -1, stage_grads[n].shape[-1]) for n in names]
        return tokens + (exchanges.launch(_reduce_steps(grads4, "rs%d_%s" % (l, stage), publish)),)

    sq, dx, grads, last_tokens = _local_step(x[0], loss_target[0], n_layers, weights_of, dims, fwd_tick, bwd_tick)
    loss = lax.psum(sq[0, 0], MESH_AXES) * (0.5 / d_model)

    small_shapes = [w[n].shape for n in SMALL_WEIGHTS]
    small_local = [jnp.stack([grads[l][n].reshape(w[n].shape[1:]) for l in range(n_layers)]) for n in SMALL_WEIGHTS]
    shared = []
    tokens = last_tokens + (exchanges.launch(_allreduce_steps(_pack(small_local), "small_grads", shared.append)),)

    adam = {n: None for n in BIG_WEIGHTS}
    grad, delta, new_m, new_v = {}, {}, {}, {}
    while exchanges.running or ready or shared:
        after = [dx]
        for l, names in ready[:2]:
            for n in names:
                mine, sibling = reduced[l, n]
                adam[n] = _adamw_sharded(w[n], m[n], v[n], mine, sibling, l, adam[n], name="adamw%d_%s" % (l, n), deps=tokens)
                after.append(adam[n][1])
        del ready[:2]
        if shared:
            grad.update(zip(SMALL_WEIGHTS, _unpack(shared.pop(), small_shapes)))
            for n in SMALL_WEIGHTS:
                flat = lambda a: a.reshape(-1, a.shape[-1])
                res = _adamw(flat(w[n]), flat(grad[n]), flat(m[n]), flat(v[n]), name="adamw_" + n)
                delta[n], new_m[n], new_v[n] = [r.reshape(w[n].shape) for r in res]
            after += [delta[n] for n in SMALL_WEIGHTS]
        tokens = exchanges.advance_all(tuple(after))
    for n in BIG_WEIGHTS:
        grad[n], delta[n], new_m[n], new_v[n] = adam[n]

    return (loss, dx[None], *[grad[n] for n in WEIGHT_NAMES], *[delta[n] for n in WEIGHT_NAMES],
            *[new_m[n] for n in WEIGHT_NAMES], *[new_v[n] for n in WEIGHT_NAMES])
```

```python
import functools
import math

import jax
import jax.numpy as jnp
from jax import lax
from jax.experimental import pallas as pl
from jax.experimental.pallas import tpu as pltpu

HEAD_DIM = 64
WINDOW = 128
SSM_GROUP_CH = 16
SSM_LANE_GROUPS = 8
RMS_EPS = 1e-6
ADAM_LR = 0.001
ADAM_B1 = 0.9
ADAM_B2 = 0.999
ADAM_EPS = 1e-08
ADAM_WD = 0.01
ADAM_STEP = 10
NEG_BIG = -1e30
MESH_AXES = ("x", "y", "c")
N_CHIPS = 4
N_DEV = 8
VMEM_LIMIT_BYTES = 56 * 1024 * 1024
BF16 = jnp.bfloat16
F32 = jnp.float32


def _cparams(*semantics):
    return pltpu.CompilerParams(dimension_semantics=semantics, vmem_limit_bytes=VMEM_LIMIT_BYTES)


def _pick(n, target, mult):
    if n <= target:
        return n
    best = None
    for d in range(mult, target + 1, mult):
        if n % d == 0:
            best = d
    assert best is not None, (n, target, mult)
    return best


def _rowmap(fn, ins, outs, *, rows, tm, ncol=1, name, deps=()):
    n_in = len(ins)
    nrow = rows // tm
    assert nrow * tm == rows

    in_specs = []
    for arr, kind, width, coloff in ins:
        if kind == "row":
            in_specs.append(pl.BlockSpec((tm, width), lambda j, i, o=coloff: (i, o + j)))
        elif kind == "vec":
            in_specs.append(pl.BlockSpec((1, width), lambda j, i, o=coloff: (0, o + j)))
        else:
            nd = arr.ndim
            in_specs.append(pl.BlockSpec(arr.shape, lambda j, i, nd=nd: (0,) * nd))
    out_specs, out_shapes = [], []
    for cols, dtype, kind, width in outs:
        if kind == "row":
            out_specs.append(pl.BlockSpec((tm, width), lambda j, i: (i, j)))
            out_shapes.append(jax.ShapeDtypeStruct((rows, cols), dtype))
        else:
            out_specs.append(pl.BlockSpec((1, width), lambda j, i: (0, j)))
            out_shapes.append(jax.ShapeDtypeStruct((1, cols), dtype))

    in_specs += [pl.BlockSpec(memory_space=pl.ANY)] * len(deps)

    def body(*refs):
        i = pl.program_id(1)
        res = fn(*[r[...].astype(F32) for r in refs[:n_in]])
        if not isinstance(res, (tuple, list)):
            res = (res,)
        for (cols, dtype, kind, width), ref, val in zip(outs, refs[n_in + len(deps):], res):
            if kind == "row":
                ref[...] = val.astype(ref.dtype)
            else:
                @pl.when(i == 0)
                def _():
                    ref[...] = jnp.zeros_like(ref)
                ref[...] += val.astype(ref.dtype)

    res = pl.pallas_call(
        body,
        out_shape=tuple(out_shapes),
        grid=(ncol, nrow),
        in_specs=in_specs,
        out_specs=tuple(out_specs),
        compiler_params=_cparams("parallel", "arbitrary"),
        name=name,
    )(*[a[0] for a in ins], *deps)
    return res


def _mm_body(dims, nk, has_add, unused_in=0):
    def body(*refs):
        if has_add:
            a_ref, b_ref, add_ref = refs[:3]
            o_ref = refs[3 + unused_in]
            rest = refs[4 + unused_in:]
        else:
            a_ref, b_ref = refs[:2]
            o_ref = refs[2 + unused_in]
            add_ref = None
            rest = refs[3 + unused_in:]
        part = lax.dot_general(a_ref[...], b_ref[...], (dims, ((), ())), preferred_element_type=F32)
        if nk == 1:
            if add_ref is not None:
                part = part + add_ref[...]
            o_ref[...] = part.astype(o_ref.dtype)
        else:
            acc_ref = rest[0]
            k = pl.program_id(2)

            @pl.when(k == 0)
            def _():
                acc_ref[...] = part

            @pl.when(k > 0)
            def _():
                acc_ref[...] += part

            @pl.when(k == nk - 1)
            def _():
                r = acc_ref[...]
                if add_ref is not None:
                    r = r + add_ref[...]
                o_ref[...] = r.astype(o_ref.dtype)
    return body


class _Weight:
    def __init__(self, arr, layer, kind):
        self.arr, self.layer, self.kind = arr, layer, kind
        self.s, _, self.r, self.c = arr.shape
        self.rows = self.r * (self.s if kind == "row" else 1)
        self.cols = self.c * (self.s if kind == "col" else 1)

    def tiles(self, tr, tc):
        return _pick(self.r, tr, 128), _pick(self.c, tc, 128)

    def index(self, tr, tc):
        layer = self.layer
        if self.kind == "col":
            per = self.c // tc
            return lambda rb, cb: (cb // per, layer, rb, cb % per)
        per = self.r // tr
        return lambda rb, cb: (rb // per, layer, rb % per, cb)


def _shard_index(kind, r, c, tr, tc):
    if kind == "col":
        per = c // tc
        return lambda rb, cb: (cb // per, rb, cb % per)
    per = r // tr
    return lambda rb, cb: (rb // per, rb % per, cb)


def _mm_nn(a, w, *, out_dtype, tm, tn, tk, name, add=None, deps=()):
    m, k = a.shape
    assert k == w.rows
    tm = _pick(m, tm, 16)
    tk, tn = w.tiles(tk, tn)
    nk = k // tk
    widx = w.index(tk, tn)
    in_specs = [pl.BlockSpec((tm, tk), lambda n, i, kk: (i, kk)),
                pl.BlockSpec((None, None, tk, tn), lambda n, i, kk: widx(kk, n))]
    args = [a, w.arr]
    if add is not None:
        in_specs.append(pl.BlockSpec((tm, tn), lambda n, i, kk: (i, n)))
        args.append(add)
    in_specs += [pl.BlockSpec(memory_space=pl.ANY)] * len(deps)
    args += list(deps)
    return pl.pallas_call(
        _mm_body(((1,), (0,)), nk, add is not None, unused_in=len(deps)),
        out_shape=jax.ShapeDtypeStruct((m, w.cols), out_dtype),
        grid=(w.cols // tn, m // tm, nk),
        in_specs=in_specs,
        out_specs=pl.BlockSpec((tm, tn), lambda n, i, kk: (i, n)),
        scratch_shapes=[pltpu.VMEM((tm, tn), F32)] if nk > 1 else [],
        compiler_params=_cparams("parallel", "parallel", "arbitrary"),
        name=name,
    )(*args)


def _interleaved_block(k, n_blocks):
    half = n_blocks // 2
    if isinstance(k, int):
        return 2 * k if k < half else 2 * (k - half) + 1
    return jnp.where(k < half, 2 * k, 2 * (k - half) + 1)


def _mm_nt(a, w, *, out_dtype, tm, tn, tko, name, deps=(), interleaved=0):
    m, n = a.shape
    assert n == w.cols
    tm = _pick(m, tm, 16)
    if w.kind == "col" and w.s > 1:
        tko = _pick(w.r, tko, 128)
        layer, nsh, width = w.layer, w.s, w.c
        blk = interleaved or width
        per = width // blk

        def body(a_ref, w_ref, *rest):
            o_ref = rest[len(deps)]
            acc = None
            for k in range(nsh * per):
                s, j = divmod(k, per)
                at = (_interleaved_block(k, nsh * per) if interleaved else k) * blk
                part = lax.dot_general(a_ref[:, at:at + blk], w_ref[s, :, j * blk:(j + 1) * blk], (((1,), (1,)), ((), ())),
                                       preferred_element_type=F32)
                acc = part if acc is None else acc + part
            o_ref[...] = acc.astype(o_ref.dtype)

        return pl.pallas_call(
            body,
            out_shape=jax.ShapeDtypeStruct((m, w.rows), out_dtype),
            grid=(w.rows // tko, m // tm),
            in_specs=[pl.BlockSpec((tm, n), lambda ko, i: (i, 0)),
                      pl.BlockSpec((nsh, None, tko, width), lambda ko, i: (0, layer, ko, 0))]
            + [pl.BlockSpec(memory_space=pl.ANY)] * len(deps),
            out_specs=pl.BlockSpec((tm, tko), lambda ko, i: (i, ko)),
            compiler_params=_cparams("parallel", "parallel"),
            name=name,
        )(a, w.arr, *deps)
    tko, tn = w.tiles(tko, tn)
    nk = n // tn
    widx = w.index(tko, tn)
    return pl.pallas_call(
        _mm_body(((1,), (1,)), nk, False, unused_in=len(deps)),
        out_shape=jax.ShapeDtypeStruct((m, w.rows), out_dtype),
        grid=(w.rows // tko, m // tm, nk),
        in_specs=[pl.BlockSpec((tm, tn), lambda ko, i, nn: (i, nn)),
                  pl.BlockSpec((None, None, tko, tn), lambda ko, i, nn: widx(ko, nn))]
        + [pl.BlockSpec(memory_space=pl.ANY)] * len(deps),
        out_specs=pl.BlockSpec((tm, tko), lambda ko, i, nn: (i, ko)),
        scratch_shapes=[pltpu.VMEM((tm, tko), F32)] if nk > 1 else [],
        compiler_params=_cparams("parallel", "parallel", "arbitrary"),
        name=name,
    )(a, w.arr, *deps)


def _mm_tn(a, c, w, *, into, tm, tn, tko, name, interleaved=False):
    m, k = a.shape
    tm = _pick(m, tm, 16)
    layer = w.layer
    mc, n = c.shape
    assert mc == m and k == w.rows and n == w.cols
    tko, tn = w.tiles(tko, tn)
    nk = m // tm
    oidx = _shard_index(w.kind, w.r, w.c, tko, tn)
    n_blocks = n // tn
    c_block = (lambda nn: _interleaved_block(nn, n_blocks)) if interleaved else (lambda nn: nn)
    in_specs = [pl.BlockSpec((tm, tko), lambda ko, nn, mm: (mm, ko)),
                pl.BlockSpec((tm, tn), lambda ko, nn, mm: (mm, c_block(nn)))]
    args = [a, c]
    if into is not None:
        in_specs.append(pl.BlockSpec(memory_space=pl.ANY))
        args.append(into)
    return pl.pallas_call(
        _mm_body(((0,), (0,)), nk, False, unused_in=len(args) - 2),
        out_shape=jax.ShapeDtypeStruct((w.arr.shape[1], w.s, w.r, w.c), F32),
        grid=(k // tko, n // tn, nk),
        in_specs=in_specs,
        out_specs=pl.BlockSpec((None, None, tko, tn), lambda ko, nn, mm: (layer,) + oidx(ko, nn)),
        scratch_shapes=[pltpu.VMEM((tko, tn), F32)] if nk > 1 else [],
        input_output_aliases={2: 0} if into is not None else {},
        compiler_params=_cparams("parallel", "parallel", "arbitrary"),
        name=name,
    )(*args)


def _mxu_sum(x, ones):
    hi = x.astype(BF16)
    lo = (x - hi.astype(F32)).astype(BF16)
    return jnp.dot(hi, ones, preferred_element_type=F32) + jnp.dot(lo, ones, preferred_element_type=F32)


def _head_rms(x, gain, sums):
    r = lax.rsqrt(_mxu_sum(x * x, sums[0]) * (1.0 / HEAD_DIM) + RMS_EPS)
    if len(sums) == 2:
        r = _mxu_sum(r, sums[1])
    return x * r * gain, r


def _head_rms_bwd(x, r, gain, dy, sums, fold):
    t = dy * gain
    mean = _mxu_sum(t * x, sums[0]) * (1.0 / HEAD_DIM)
    dx = r * t - x * (r * r * r) * (mean if len(sums) == 1 else _mxu_sum(mean, sums[1]))
    dg = jnp.broadcast_to(jnp.sum(dy * x * r, axis=0, keepdims=True), (8, x.shape[1]))
    return dx, _mxu_sum(dg, fold)[0:1, :HEAD_DIM]


def _attn_consts(n_q, n_kv, sinks, qg, kg):
    group = n_q // n_kv
    t = jnp.arange(WINDOW, dtype=jnp.int32)[:, None]
    s = jnp.arange(2 * WINDOW, dtype=jnp.int32)[None, :] - WINDOW
    dist = (t - s).astype(F32)
    valid = (dist >= 0) & (dist < WINDOW)
    slopes = jnp.exp2(-8.0 * jnp.arange(1, n_q + 1, dtype=F32) / n_q)
    bias = jnp.where(valid[None], -slopes[:, None, None] * dist[None], NEG_BIG)
    sink = jnp.broadcast_to(sinks.astype(F32).reshape(n_kv, group, 1, 1), (n_kv, group, WINDOW, 128))
    head_ones = lambda h: jnp.kron(jnp.eye(h, dtype=F32), jnp.ones((HEAD_DIM, HEAD_DIM), F32)).astype(BF16)
    fold = lambda h: jnp.tile(jnp.eye(HEAD_DIM, 128, dtype=F32), (h, 1)).astype(BF16)
    q_heads = jnp.kron(jnp.eye(n_q, 128, dtype=F32), jnp.ones((HEAD_DIM, 1), F32)).astype(BF16)
    return dict(
        bias=bias.reshape(n_kv, group * WINDOW, 2 * WINDOW),
        sink=sink.reshape(n_kv, group * WINDOW, 128),
        qg=jnp.tile(qg, (1, n_q)), kg=jnp.tile(kg, (1, n_kv)),
        q_heads=q_heads, q_spread=q_heads.T,
        k_ones=head_ones(n_kv),
        q_fold=fold(n_q), k_fold=fold(n_kv),
        key_ones=jnp.ones((2 * WINDOW, 128), BF16))


_ATTN_CONST_ORDER = ("qg", "kg", "sink", "bias", "q_heads", "q_spread", "k_ones", "q_fold", "k_fold", "key_ones")


def _attn_inputs(q_ref, kc_ref, kp_ref, vc_ref, vp_ref, c):
    q = q_ref[...]
    k2 = jnp.concatenate([kp_ref[...], kc_ref[...]], axis=0)
    v2 = jnp.concatenate([vp_ref[...], vc_ref[...]], axis=0)
    qn, rq = _head_rms(q, c["qg"][...], (c["q_heads"][...], c["q_spread"][...]))
    kn, rk = _head_rms(k2, c["kg"][...], (c["k_ones"][...],))
    return dict(q=q, rq=rq, qn=qn.astype(BF16), k2=k2, rk=rk, kn=kn.astype(BF16), v2=v2.astype(BF16))


def _attn_probs(x, c, first_mask, kv, group):
    sl = slice(kv * HEAD_DIM, (kv + 1) * HEAD_DIM)
    k2b, v2b = x["kn"][:, sl], x["v2"][:, sl]
    qs = jnp.concatenate([x["qn"][:, (kv * group + g) * HEAD_DIM:(kv * group + g + 1) * HEAD_DIM]
                          for g in range(group)], axis=0)
    s = lax.dot_general(qs, k2b, (((1,), (1,)), ((), ())), preferred_element_type=F32) * (HEAD_DIM ** -0.5)
    s = jnp.where(first_mask, NEG_BIG, s + c["bias"][kv])
    sink = c["sink"][kv]
    m = jnp.maximum(jnp.max(s, axis=-1, keepdims=True), sink)
    twice = lambda a: jnp.concatenate([a, a], axis=1)
    p = jnp.exp(s - twice(m))
    esink = jnp.exp(sink - m)
    inv = 1.0 / (_mxu_sum(p, c["key_ones"][...]) + esink)
    return dict(k2b=k2b, v2b=v2b, qs=qs, pn=p * twice(inv), psink=esink * inv, twice=twice)


def _attn_specs(n_q, n_kv):
    aw, kvw = n_q * HEAD_DIM, n_kv * HEAD_DIM
    group = n_q // n_kv
    kblk, vblk = aw // kvw, aw // kvw + 1

    def specs(nb):
        cur = lambda n: jnp.minimum(n, nb - 1)
        prev = lambda n: jnp.maximum(jnp.minimum(n, nb - 1) - 1, 0)
        return [
            pl.BlockSpec((WINDOW, aw), lambda n: (cur(n), 0)),
            pl.BlockSpec((WINDOW, kvw), lambda n: (cur(n), kblk)),
            pl.BlockSpec((WINDOW, kvw), lambda n: (prev(n), kblk)),
            pl.BlockSpec((WINDOW, kvw), lambda n: (cur(n), vblk)),
            pl.BlockSpec((WINDOW, kvw), lambda n: (prev(n), vblk)),
        ]
    whole = lambda shape: pl.BlockSpec(shape, lambda n: (0,) * len(shape))
    return specs, whole


def _attn_fwd(z, qg, kg, sinks, *, n_q, n_kv, name, deps=()):
    L = z.shape[0]
    nb = L // WINDOW
    aw = n_q * HEAD_DIM
    group = n_q // n_kv
    consts = _attn_consts(n_q, n_kv, sinks, qg, kg)
    specs, whole = _attn_specs(n_q, n_kv)
    nc = len(_ATTN_CONST_ORDER)

    def body(q_ref, kc_ref, kp_ref, vc_ref, vp_ref, *rest):
        c = dict(zip(_ATTN_CONST_ORDER, rest[:nc]))
        o_ref = rest[-1]
        n = pl.program_id(0)
        col = lax.broadcasted_iota(jnp.int32, (group * WINDOW, 2 * WINDOW), 1)
        first_mask = jnp.logical_and(n == 0, col < WINDOW)
        x = _attn_inputs(q_ref, kc_ref, kp_ref, vc_ref, vp_ref, c)
        for kv in range(n_kv):
            a = _attn_probs(x, c, first_mask, kv, group)
            o = jnp.dot(a["pn"].astype(BF16), a["v2b"], preferred_element_type=F32)
            for g in range(group):
                h = kv * group + g
                o_ref[:, h * HEAD_DIM:(h + 1) * HEAD_DIM] = o[g * WINDOW:(g + 1) * WINDOW].astype(o_ref.dtype)

    return pl.pallas_call(
        body,
        out_shape=jax.ShapeDtypeStruct((L, aw), BF16),
        grid=(nb,),
        in_specs=specs(nb) + [whole(consts[k].shape) for k in _ATTN_CONST_ORDER]
        + [pl.BlockSpec(memory_space=pl.ANY)] * len(deps),
        out_specs=pl.BlockSpec((WINDOW, aw), lambda n: (n, 0)),
        compiler_params=_cparams("parallel"),
        name=name,
    )(z, z, z, z, z, *[consts[k] for k in _ATTN_CONST_ORDER], *deps)


def _attn_bwd(z, do, qg, kg, sinks, *, n_q, n_kv, name):
    L = z.shape[0]
    nb = L // WINDOW
    aw, kvw = n_q * HEAD_DIM, n_kv * HEAD_DIM
    group = n_q // n_kv
    consts = _attn_consts(n_q, n_kv, sinks, qg, kg)
    specs, whole = _attn_specs(n_q, n_kv)
    scale = HEAD_DIM ** -0.5
    nc = len(_ATTN_CONST_ORDER)

    def body(q_ref, kc_ref, kp_ref, vc_ref, vp_ref, do_ref, *rest):
        c = dict(zip(_ATTN_CONST_ORDER, rest[:nc]))
        dq_ref, dkv_ref, dqg_ref, dkg_ref, dsink_ref, carry_ref, dqn_ref, dkn_ref, dv_ref = rest[nc:]
        n = pl.program_id(0)

        @pl.when(n == 0)
        def _():
            dqg_ref[...] = jnp.zeros_like(dqg_ref)
            dkg_ref[...] = jnp.zeros_like(dkg_ref)
            dsink_ref[...] = jnp.zeros_like(dsink_ref)
            carry_ref[...] = jnp.zeros_like(carry_ref)

        @pl.when(n < nb)
        def _():
            col = lax.broadcasted_iota(jnp.int32, (group * WINDOW, 2 * WINDOW), 1)
            first_mask = jnp.logical_and(n == 0, col < WINDOW)
            head_lane = lax.broadcasted_iota(jnp.int32, (1, n_q), 1)
            x = _attn_inputs(q_ref, kc_ref, kp_ref, vc_ref, vp_ref, c)
            dsink = jnp.zeros((1, n_q), F32)
            for kv in range(n_kv):
                a = _attn_probs(x, c, first_mask, kv, group)
                pn = a["pn"]
                dos = jnp.concatenate(
                    [do_ref[:, (kv * group + g) * HEAD_DIM:(kv * group + g + 1) * HEAD_DIM] for g in range(group)],
                    axis=0).astype(BF16)
                dpn = lax.dot_general(dos, a["v2b"], (((1,), (1,)), ((), ())), preferred_element_type=F32)
                ksl = slice(kv * HEAD_DIM, (kv + 1) * HEAD_DIM)
                dv_ref[:, ksl] = lax.dot_general(pn.astype(BF16), dos, (((0,), (0,)), ((), ())), preferred_element_type=F32)
                delta = _mxu_sum(pn * dpn, c["key_ones"][...])
                ds = (pn * (dpn - a["twice"](delta))).astype(BF16)
                dsk = -a["psink"] * delta
                dqn = lax.dot_general(ds, a["k2b"], (((1,), (0,)), ((), ())), preferred_element_type=F32) * scale
                dkn_ref[:, ksl] = lax.dot_general(ds, a["qs"], (((0,), (0,)), ((), ())), preferred_element_type=F32) * scale
                for g in range(group):
                    h = kv * group + g
                    rows = slice(g * WINDOW, (g + 1) * WINDOW)
                    dqn_ref[:, h * HEAD_DIM:(h + 1) * HEAD_DIM] = dqn[rows]
                    dsink = dsink + jnp.where(head_lane == h, jnp.sum(dsk[rows], axis=0, keepdims=True)[:, :n_q], 0.0)
            dq, dqg = _head_rms_bwd(x["q"], x["rq"], c["qg"][...], dqn_ref[...],
                                    (c["q_heads"][...], c["q_spread"][...]), c["q_fold"][...])
            dk2, dkg = _head_rms_bwd(x["k2"], x["rk"], c["kg"][...], dkn_ref[...], (c["k_ones"][...],), c["k_fold"][...])
            dq_ref[...] = dq.astype(dq_ref.dtype)
            dkv_ref[:, :kvw] = (carry_ref[:, :kvw] + dk2[:WINDOW]).astype(dkv_ref.dtype)
            dkv_ref[:, kvw:] = (carry_ref[:, kvw:] + dv_ref[:WINDOW, :]).astype(dkv_ref.dtype)
            carry_ref[:, :kvw] = dk2[WINDOW:]
            carry_ref[:, kvw:] = dv_ref[WINDOW:, :]
            dqg_ref[...] += dqg
            dkg_ref[...] += dkg
            dsink_ref[...] += dsink

        @pl.when(n == nb)
        def _():
            dkv_ref[...] = carry_ref[...].astype(dkv_ref.dtype)

    in_specs = (specs(nb) + [pl.BlockSpec((WINDOW, aw), lambda n: (jnp.minimum(n, nb - 1), 0))]
                + [whole(consts[k].shape) for k in _ATTN_CONST_ORDER])
    return pl.pallas_call(
        body,
        out_shape=(jax.ShapeDtypeStruct((L, aw), BF16), jax.ShapeDtypeStruct((L, 2 * kvw), BF16),
                   jax.ShapeDtypeStruct((1, HEAD_DIM), F32), jax.ShapeDtypeStruct((1, HEAD_DIM), F32),
                   jax.ShapeDtypeStruct((1, n_q), F32)),
        grid=(nb + 1,),
        in_specs=in_specs,
        out_specs=(pl.BlockSpec((WINDOW, aw), lambda n: (jnp.minimum(n, nb - 1), 0)),
                   pl.BlockSpec((WINDOW, 2 * kvw), lambda n: (jnp.maximum(n - 1, 0), 0)),
                   pl.BlockSpec((1, HEAD_DIM), lambda n: (0, 0)),
                   pl.BlockSpec((1, HEAD_DIM), lambda n: (0, 0)),
                   pl.BlockSpec((1, n_q), lambda n: (0, 0))),
        scratch_shapes=[pltpu.VMEM((WINDOW, 2 * kvw), F32), pltpu.VMEM((WINDOW, aw), F32),
                        pltpu.VMEM((2 * WINDOW, kvw), F32), pltpu.VMEM((2 * WINDOW, kvw), F32)],
        compiler_params=_cparams("arbitrary"),
        name=name,
    )(z, z, z, z, z, do, *[consts[k] for k in _ATTN_CONST_ORDER])


def _cmul(ar, ai, br, bi):
    return ar * br - ai * bi, ar * bi + ai * br


def _time_permutation(tc):
    r = jnp.arange(tc)
    src = (r % 8) * (tc // 8) + r // 8
    p = (src[:, None] == jnp.arange(tc)[None, :]).astype(BF16)
    return p, p.T


def _unpermute(pt, x):
    hi = x.astype(BF16)
    lo = (x - hi.astype(F32)).astype(BF16)
    moved = jnp.dot(pt, jnp.concatenate([hi, lo], axis=1), preferred_element_type=F32)
    return moved[:, :x.shape[1]] + moved[:, x.shape[1]:]


def _segment_scan(xr_ref, xi_ref, ar, ai, cr, ci, ng, reverse):
    n = ar.shape[-1]
    row = lax.broadcasted_iota(jnp.int32, (8, n), 0)
    seeded = 7 if reverse else 0
    a8r = jnp.broadcast_to(ar, (8, n))
    a8i = jnp.broadcast_to(ai, (8, n))
    rows_of = lambda g: pl.ds(pl.multiple_of(((ng - 1 - g) if reverse else g) * 8, 8), 8)

    def recur(g, s):
        rows = rows_of(g)
        sr = a8r * s[0] - a8i * s[1] + xr_ref[rows, :]
        si = a8r * s[1] + a8i * s[0] + xi_ref[rows, :]
        xr_ref[rows, :] = sr
        xi_ref[rows, :] = si
        return sr, si

    fr, fi = lax.fori_loop(0, ng, recur, (jnp.where(row == seeded, cr, 0.0), jnp.where(row == seeded, ci, 0.0)))
    pr, pi = ar, ai
    for _ in range(ng.bit_length() - 1):
        pr, pi = _cmul(pr, pi, pr, pi)
    for k in (1, 2, 4):
        keep = (row < 8 - k) if reverse else (row >= k)
        shift = (8 - k) if reverse else k
        mr, mi = jnp.where(keep, pr, 0.0), jnp.where(keep, pi, 0.0)
        tr, ti = pltpu.roll(fr, shift, 0), pltpu.roll(fi, shift, 0)
        fr, fi = fr + mr * tr - mi * ti, fi + mr * ti + mi * tr
        pr, pi = _cmul(pr, pi, pr, pi)
    shift = 7 if reverse else 1
    before_r, before_i = pltpu.roll(fr, shift, 0), pltpu.roll(fi, shift, 0)

    def inherit(g, d):
        rows = rows_of(g)
        dr = a8r * d[0] - a8i * d[1]
        di = a8r * d[1] + a8i * d[0]
        xr_ref[rows, :] = xr_ref[rows, :] + dr
        xi_ref[rows, :] = xi_ref[rows, :] + di
        return dr, di

    lax.fori_loop(0, ng, inherit, (jnp.where(row == seeded, 0.0, before_r), jnp.where(row == seeded, 0.0, before_i)))
    out = 0 if reverse else 7
    return (fr[out:out + 1], fi[out:out + 1],
            jnp.where(row == seeded, cr, before_r), jnp.where(row == seeded, ci, before_i))


def _blockdiag(x):
    g, a, b = x.shape
    j = g // SSM_LANE_GROUPS
    eye = jnp.eye(SSM_LANE_GROUPS, dtype=x.dtype)
    y = x.reshape(j, SSM_LANE_GROUPS, a, 1, b) * eye[None, :, None, :, None]
    return y.reshape(j, SSM_LANE_GROUPS * a, SSM_LANE_GROUPS * b)


def _blockdiag_extract(y, a, b):
    j = y.shape[0]
    y = y.reshape(j, SSM_LANE_GROUPS, a, SSM_LANE_GROUPS, b)
    return jnp.einsum("jgahb,gh->jgab", y, jnp.eye(SSM_LANE_GROUPS, dtype=y.dtype)).reshape(j * SSM_LANE_GROUPS, a, b)


def _ssm_disc(lr, li, ldt, brt, bit):
    dt = jnp.exp(ldt)
    mag = jnp.exp(lr * dt)
    ar = mag * jnp.cos(li * dt)
    ai = mag * jnp.sin(li * dt)
    den = lr * lr + li * li
    fr = ((ar - 1.0) * lr + ai * li) / den
    fi = (ai * lr - (ar - 1.0) * li) / den
    bbr = fr[:, None, :] * brt - fi[:, None, :] * bit
    bbi = fr[:, None, :] * bit + fi[:, None, :] * brt
    return ar, ai, bbr, bbi


def _ssm_prep(lr, li, ldt, brt, bit, *, name):
    g, h, p = brt.shape

    def body(lr_ref, li_ref, ldt_ref, brt_ref, bit_ref, ar_ref, ai_ref, bbr_ref, bbi_ref):
        ar, ai, bbr, bbi = _ssm_disc(lr_ref[...], li_ref[...], ldt_ref[...], brt_ref[...], bit_ref[...])
        ar_ref[...] = ar
        ai_ref[...] = ai
        bbr_ref[...] = bbr
        bbi_ref[...] = bbi

    gp = jax.ShapeDtypeStruct((g, p), F32)
    ghp = jax.ShapeDtypeStruct((g, h, p), F32)
    return pl.pallas_call(body, out_shape=(gp, gp, ghp, ghp), name=name)(lr, li, ldt, brt, bit)


def _ssm_prep_bwd(lr, li, ldt, brt, bit, dar, dai, dbbr, dbbi, *, name):
    g, h, p = brt.shape

    def body(lr_ref, li_ref, ldt_ref, brt_ref, bit_ref, dar_ref, dai_ref, dbbr_ref, dbbi_ref,
             dlr_ref, dli_ref, dldt_ref, dbrt_ref, dbit_ref):
        _, vjp = jax.vjp(_ssm_disc, lr_ref[...], li_ref[...], ldt_ref[...], brt_ref[...], bit_ref[...])
        dlr, dli, dldt, dbrt, dbit = vjp((dar_ref[...], dai_ref[...], dbbr_ref[...], dbbi_ref[...]))
        dlr_ref[...] = dlr
        dli_ref[...] = dli
        dldt_ref[...] = dldt
        dbrt_ref[...] = dbrt
        dbit_ref[...] = dbit

    gp = jax.ShapeDtypeStruct((g, p), F32)
    ghp = jax.ShapeDtypeStruct((g, h, p), F32)
    return pl.pallas_call(body, out_shape=(gp, gp, jax.ShapeDtypeStruct((g, 1), F32), ghp, ghp), name=name)(
        lr, li, ldt, brt, bit, dar, dai, dbbr, dbbi)


def _ssm_specs(tc, nlanes, nch, u_colblk, chunk_of):
    return [
        pl.BlockSpec((tc, nch), lambda j, c: (chunk_of(c), u_colblk + j)),
        pl.BlockSpec((1, nlanes), lambda j, c: (0, j)),
        pl.BlockSpec((1, nlanes), lambda j, c: (0, j)),
        pl.BlockSpec((None, nch, nlanes), lambda j, c: (j, 0, 0)),
        pl.BlockSpec((None, nch, nlanes), lambda j, c: (j, 0, 0)),
        pl.BlockSpec((None, nlanes, nch), lambda j, c: (j, 0, 0)),
        pl.BlockSpec((None, nlanes, nch), lambda j, c: (j, 0, 0)),
        pl.BlockSpec((1, nch), lambda j, c: (0, j)),
        pl.BlockSpec((tc, tc), lambda j, c: (0, 0)),
        pl.BlockSpec((tc, tc), lambda j, c: (0, 0)),
    ]


def _ssm_fwd(z, ar, ai, bblk_r, bblk_i, cblk_r, cblk_i, d, *, u_col, tc, name, deps=()):
    L = z.shape[0]
    nj, nch, nlanes = bblk_r.shape
    w = nj * nch
    nc = L // tc
    ng = tc // 8

    assert ng & (ng - 1) == 0
    perm, perm_t = _time_permutation(tc)

    def body(u_ref, ar_ref, ai_ref, br_ref, bi_ref, cr_ref, ci_ref, d_ref, p_ref, pt_ref, *rest):
        y_ref, s0r_ref, s0i_ref, xr_ref, xi_ref, carr_ref, cari_ref = rest[len(deps):]
        c = pl.program_id(1)

        @pl.when(c == 0)
        def _():
            carr_ref[...] = jnp.zeros_like(carr_ref)
            cari_ref[...] = jnp.zeros_like(cari_ref)

        s0r_ref[...] = carr_ref[...]
        s0i_ref[...] = cari_ref[...]
        u = u_ref[...]
        ub = jnp.dot(p_ref[...], u.astype(BF16), preferred_element_type=F32).astype(BF16)
        xr_ref[...] = jnp.dot(ub, br_ref[...].astype(BF16), preferred_element_type=F32)
        xi_ref[...] = jnp.dot(ub, bi_ref[...].astype(BF16), preferred_element_type=F32)
        cr, ci, _, _ = _segment_scan(xr_ref, xi_ref, ar_ref[...], ai_ref[...], carr_ref[...], cari_ref[...], ng, False)
        carr_ref[...] = cr
        cari_ref[...] = ci
        y = (jnp.dot(xr_ref[...].astype(BF16), cr_ref[...].astype(BF16), preferred_element_type=F32)
             - jnp.dot(xi_ref[...].astype(BF16), ci_ref[...].astype(BF16), preferred_element_type=F32))
        y_ref[...] = _unpermute(pt_ref[...], y) + d_ref[...] * u

    state = jax.ShapeDtypeStruct((nc, 1, nj * nlanes), F32)
    state_spec = pl.BlockSpec((None, 1, nlanes), lambda j, c: (c, 0, j))
    return pl.pallas_call(
        body,
        out_shape=(jax.ShapeDtypeStruct((L, w), F32), state, state),
        grid=(nj, nc),
        in_specs=_ssm_specs(tc, nlanes, nch, u_col // nch, lambda c: c) + [pl.BlockSpec(memory_space=pl.ANY)] * len(deps),
        out_specs=(pl.BlockSpec((tc, nch), lambda j, c: (c, j)), state_spec, state_spec),
        scratch_shapes=[pltpu.VMEM((tc, nlanes), F32), pltpu.VMEM((tc, nlanes), F32),
                        pltpu.VMEM((1, nlanes), F32), pltpu.VMEM((1, nlanes), F32)],
        compiler_params=_cparams("parallel", "arbitrary"),
        name=name,
    )(z, ar, ai, bblk_r, bblk_i, cblk_r, cblk_i, d, perm, perm_t, *deps)


def _ssm_bwd(z, dy, s0r, s0i, ar, ai, bblk_r, bblk_i, cblk_r, cblk_i, d, *, u_col, tc, name):
    L = z.shape[0]
    nj, nch, nlanes = bblk_r.shape
    w = nj * nch
    nc = L // tc
    ng = tc // 8
    chunk_of = lambda c: nc - 1 - c
    assert ng & (ng - 1) == 0
    perm, perm_t = _time_permutation(tc)

    def body(u_ref, ar_ref, ai_ref, br_ref, bi_ref, cr_ref, ci_ref, d_ref, p_ref, pt_ref, dy_ref, s0r_ref, s0i_ref,
             du_ref, dbr_ref, dbi_ref, dcr_ref, dci_ref, dar_ref, dai_ref, dd_ref,
             sr_ref, si_ref, lr_ref, li_ref, carr_ref, cari_ref):
        c = pl.program_id(1)

        @pl.when(c == 0)
        def _():
            for ref in (dbr_ref, dbi_ref, dcr_ref, dci_ref, dar_ref, dai_ref, dd_ref, carr_ref, cari_ref):
                ref[...] = jnp.zeros_like(ref)

        u = u_ref[...]
        dyv = dy_ref[...]
        both = jnp.dot(p_ref[...], jnp.concatenate([u.astype(BF16), dyv.astype(BF16)], axis=1), preferred_element_type=F32)
        ub = both[:, :nch].astype(BF16)
        dyb = both[:, nch:].astype(BF16)
        brb = br_ref[...].astype(BF16)
        bib = bi_ref[...].astype(BF16)
        crb = cr_ref[...].astype(BF16)
        cib = ci_ref[...].astype(BF16)
        a_r, a_i = ar_ref[...], ai_ref[...]

        sr_ref[...] = jnp.dot(ub, brb, preferred_element_type=F32)
        si_ref[...] = jnp.dot(ub, bib, preferred_element_type=F32)
        _, _, start_r, start_i = _segment_scan(sr_ref, si_ref, a_r, a_i, s0r_ref[...], s0i_ref[...], ng, False)

        nt = (((1,), (1,)), ((), ()))
        lr_ref[...] = lax.dot_general(dyb, crb, nt, preferred_element_type=F32)
        li_ref[...] = -lax.dot_general(dyb, cib, nt, preferred_element_type=F32)
        cr, ci, _, _ = _segment_scan(lr_ref, li_ref, a_r, -a_i, carr_ref[...], cari_ref[...], ng, True)
        carr_ref[...] = cr
        cari_ref[...] = ci

        def accumulate(g, carry):
            pr, pi, acc_r, acc_i = carry
            rows = pl.ds(pl.multiple_of(g * 8, 8), 8)
            lr, li = lr_ref[rows, :], li_ref[rows, :]
            return sr_ref[rows, :], si_ref[rows, :], acc_r + lr * pr + li * pi, acc_i + li * pr - lr * pi

        zero8 = jnp.zeros((8, nlanes), F32)
        _, _, acc_r, acc_i = lax.fori_loop(0, ng, accumulate, (start_r, start_i, zero8, zero8))
        dar_ref[...] += jnp.sum(acc_r, axis=0, keepdims=True)
        dai_ref[...] += jnp.sum(acc_i, axis=0, keepdims=True)

        tn = (((0,), (0,)), ((), ()))
        lrb = lr_ref[...].astype(BF16)
        lib = li_ref[...].astype(BF16)
        dcr_ref[...] += lax.dot_general(dyb, sr_ref[...].astype(BF16), tn, preferred_element_type=F32)
        dci_ref[...] -= lax.dot_general(dyb, si_ref[...].astype(BF16), tn, preferred_element_type=F32)
        dbr_ref[...] += lax.dot_general(ub, lrb, tn, preferred_element_type=F32)
        dbi_ref[...] += lax.dot_general(ub, lib, tn, preferred_element_type=F32)
        du = (lax.dot_general(lrb, brb, nt, preferred_element_type=F32)
              + lax.dot_general(lib, bib, nt, preferred_element_type=F32))
        du_ref[...] = (_unpermute(pt_ref[...], du) + d_ref[...] * dyv).astype(du_ref.dtype)
        dd_ref[...] += jnp.sum(dyv * u, axis=0, keepdims=True)

    state_spec = pl.BlockSpec((None, 1, nlanes), lambda j, c: (chunk_of(c), 0, j))
    bshape = jax.ShapeDtypeStruct((nj, nch, nlanes), F32)
    ashape = jax.ShapeDtypeStruct((1, nj * nlanes), F32)
    bspec = pl.BlockSpec((None, nch, nlanes), lambda j, c: (j, 0, 0))
    aspec = pl.BlockSpec((1, nlanes), lambda j, c: (0, j))
    big = pltpu.VMEM((tc, nlanes), F32)
    return pl.pallas_call(
        body,
        out_shape=(jax.ShapeDtypeStruct((L, w), BF16), bshape, bshape, bshape, bshape, ashape, ashape,
                   jax.ShapeDtypeStruct((1, w), F32)),
        grid=(nj, nc),
        in_specs=_ssm_specs(tc, nlanes, nch, u_col // nch, chunk_of)
        + [pl.BlockSpec((tc, nch), lambda j, c: (chunk_of(c), j)), state_spec, state_spec],
        out_specs=(pl.BlockSpec((tc, nch), lambda j, c: (chunk_of(c), j)), bspec, bspec, bspec, bspec, aspec, aspec,
                   pl.BlockSpec((1, nch), lambda j, c: (0, j))),
        scratch_shapes=[big, big, big, big, pltpu.VMEM((1, nlanes), F32), pltpu.VMEM((1, nlanes), F32)],
        compiler_params=_cparams("parallel", "arbitrary"),
        name=name,
    )(z, ar, ai, bblk_r, bblk_i, cblk_r, cblk_i, d, perm, perm_t, dy, s0r, s0i)


def _rmsnorm_rows(x, g):
    return x * lax.rsqrt(jnp.mean(x * x, axis=-1, keepdims=True) + RMS_EPS) * g


def _glu_out(y_raw, pre, b):
    yg = jax.nn.gelu(y_raw)
    return yg * jax.nn.sigmoid(pre + b)


def _gate_merge(za, zs, ba, bs, a, bm):
    return jax.nn.sigmoid(za + ba) * a + jax.nn.sigmoid(zs + bs) * bm


def _swiglu(g, u):
    return jax.nn.silu(g) * u


def _row_tile(width_bytes_per_row, rows):
    budget = VMEM_LIMIT_BYTES // 3
    t = max(8, min(1024, budget // (2 * max(width_bytes_per_row, 1))))
    return _pick(rows, t, 16)


def _ssm_params(p, prefix):
    g, pst = p["lam_re"].shape
    ar, ai, bbr, bbi = _ssm_prep(p["lam_re"], p["lam_im"], p["log_dt"], p["b_re_t"], p["b_im_t"], name=prefix + "_ssm_prep")
    return dict(ar=ar.reshape(1, g * pst), ai=ai.reshape(1, g * pst),
                bblk_r=_blockdiag(bbr), bblk_i=_blockdiag(bbi),
                cblk_r=_blockdiag(jnp.swapaxes(p["c_re"], 1, 2)), cblk_i=_blockdiag(jnp.swapaxes(p["c_im"], 1, 2)))


def _layer_fwd(x, p, dims, prefix, deps=(), tick=None):
    tick = tick or (lambda point, arr: ())
    t, d = x.shape
    aw, kvw, sw, ff = dims["aw"], dims["kvw"], dims["sw"], dims["ff"]
    off_u = aw + 2 * kvw
    off_g = off_u + sw
    gblk = _pick(d, 512, 128)
    assert off_g % gblk == 0 and off_u % (SSM_LANE_GROUPS * SSM_GROUP_CH) == 0
    sv = {"x": x}

    h, = _rowmap(_rmsnorm_rows, [(x, "row", d, 0), (p["norm_mix_g"], "vec", d, 0)], [(d, BF16, "row", d)],
                 rows=t, tm=_row_tile(6 * d, t), name=prefix + "_norm_mix", deps=deps)
    deps = tick("norm", h)
    z = _mm_nn(h, p["w_in"], out_dtype=F32, tm=1024, tn=1664, tk=2048, name=prefix + "_mm_in", deps=deps)
    ya = _attn_fwd(z, p["q_norm_g"], p["k_norm_g"], p["attn_sinks"], n_q=dims["n_q"], n_kv=dims["n_kv"],
                   name=prefix + "_attn_fwd", deps=tick("in", z))
    sp = _ssm_params(p, prefix)
    y_raw, s0r, s0i = _ssm_fwd(z, sp["ar"], sp["ai"], sp["bblk_r"], sp["bblk_i"], sp["cblk_r"], sp["cblk_i"], p["ssm_d"],
                               u_col=off_u, tc=dims["tc"], name=prefix + "_ssm_fwd", deps=tick("attn", ya))
    yg, = _rowmap(jax.nn.gelu, [(y_raw, "row", sw, 0)], [(sw, BF16, "row", sw)],
                  rows=t, tm=_row_tile(6 * sw, t), name=prefix + "_gelu", deps=tick("ssm", y_raw))
    pre = _mm_nn(yg, p["ssm_glu_w"], out_dtype=F32, tm=1024, tn=1024, tk=1024, name=prefix + "_mm_glu")
    y2, = _rowmap(_glu_out, [(y_raw, "row", sw, 0), (pre, "row", sw, 0), (p["ssm_glu_b"], "vec", sw, 0)],
                  [(sw, BF16, "row", sw)], rows=t, tm=_row_tile(10 * sw, t), name=prefix + "_glu_out")
    a = _mm_nn(ya, p["w_attn_branch"], out_dtype=BF16, tm=1024, tn=512, tk=1024, name=prefix + "_mm_ab")
    bm = _mm_nn(y2, p["w_ssm_branch"], out_dtype=BF16, tm=1024, tn=512, tk=1024, name=prefix + "_mm_sb")
    ngb = d // gblk
    merged, = _rowmap(
        _gate_merge,
        [(z, "row", gblk, off_g // gblk), (z, "row", gblk, off_g // gblk + ngb),
         (p["gate_bias"], "vec", gblk, 0), (p["gate_bias"], "vec", gblk, ngb),
         (a, "row", gblk, 0), (bm, "row", gblk, 0)],
        [(d, BF16, "row", gblk)], rows=t, tm=_row_tile(18 * gblk, t), ncol=ngb, name=prefix + "_gate")
    x1 = _mm_nn(merged, p["w_out"], out_dtype=F32, tm=512, tn=1024, tk=2048, name=prefix + "_mm_out", add=x)
    h2, = _rowmap(_rmsnorm_rows, [(x1, "row", d, 0), (p["norm_ffn_g"], "vec", d, 0)], [(d, BF16, "row", d)],
                  rows=t, tm=_row_tile(6 * d, t), name=prefix + "_norm_ffn", deps=tick("out", x1))
    gu = _mm_nn(h2, p["w_ffn_in"], out_dtype=BF16, tm=1024, tn=1408, tk=2048, name=prefix + "_mm_ffn_in")
    fblk = _pick(ff, 1408, 128)
    nfb = ff // fblk
    act, = _rowmap(_swiglu, [(gu, "row", fblk, 0), (gu, "row", fblk, nfb)], [(ff, BF16, "row", fblk)],
                   rows=t, tm=_row_tile(10 * fblk, t), ncol=nfb, name=prefix + "_swiglu", deps=tick("ffn_in", gu))
    x2 = _mm_nn(act, p["w_ffn_out"], out_dtype=F32, tm=512, tn=512, tk=5632, name=prefix + "_mm_ffn_out", add=x1,
                deps=tick("act", act))
    sv.update(h=h, z=z, ya=ya, sp=sp, y_raw=y_raw, s0r=s0r, s0i=s0i, yg=yg, pre=pre, y2=y2, a=a, bm=bm,
              merged=merged, x1=x1, h2=h2, gu=gu, act=act)
    return x2, sv


def _layer_bwd(dx2, dx2b, sv, p, dims, prefix, gbuf, deps=(), before_mixer=None, before_in=None):
    t, d = dx2.shape
    aw, kvw, sw, ff = dims["aw"], dims["kvw"], dims["sw"], dims["ff"]
    off_u = aw + 2 * kvw
    off_g = off_u + sw
    gblk = _pick(d, 512, 128)
    ngb = d // gblk
    fblk = _pick(ff, 1408, 128)
    nfb = ff // fblk
    g = {}

    dact = _mm_nt(dx2b, p["w_ffn_out"], out_dtype=BF16, tm=512, tn=2048, tko=1408, name=prefix + "_mm_dact", deps=deps)
    g["w_ffn_out"] = _mm_tn(sv["act"], dx2b, p["w_ffn_out"], into=gbuf.get("w_ffn_out"), tm=4096, tn=1024, tko=512,
                            name=prefix + "_mm_dw_ffn_out")

    def swiglu_bwd(gg, uu, da):
        s = jax.nn.sigmoid(gg)
        gs = gg * s
        return jnp.concatenate([da * uu * (s + gs * (1.0 - s)), da * gs], axis=1)

    dgu, = _rowmap(swiglu_bwd, [(sv["gu"], "row", fblk, 0), (sv["gu"], "row", fblk, nfb), (dact, "row", fblk, 0)],
                   [(2 * ff, BF16, "row", 2 * fblk)],
                   rows=t, tm=_row_tile(16 * fblk, t), ncol=nfb, name=prefix + "_swiglu_bwd")
    dh2 = _mm_nt(dgu, p["w_ffn_in"], out_dtype=F32, tm=256, tn=1408, tko=512, name=prefix + "_mm_dh2", interleaved=fblk)
    g["w_ffn_in"] = _mm_tn(sv["h2"], dgu, p["w_ffn_in"], into=gbuf.get("w_ffn_in"), tm=4096, tn=fblk, tko=512,
                           name=prefix + "_mm_dw_ffn_in", interleaved=True)

    def norm_bwd(xx, gg, dh, dres):
        _, vjp = jax.vjp(_rmsnorm_rows, xx, gg)
        dxx, dgg = vjp(dh)
        dxx = dxx + dres
        return dxx, dxx, dgg

    dx1, dx1b, g["norm_ffn_g"] = _rowmap(
        norm_bwd, [(sv["x1"], "row", d, 0), (p["norm_ffn_g"], "vec", d, 0), (dh2, "row", d, 0), (dx2, "row", d, 0)],
        [(d, F32, "row", d), (d, BF16, "row", d), (d, F32, "acc", d)],
        rows=t, tm=_row_tile(22 * d, t), name=prefix + "_norm_ffn_bwd")

    deps = before_mixer(dx1, g) if before_mixer else ()
    dmerged = _mm_nt(dx1b, p["w_out"], out_dtype=BF16, tm=1024, tn=2048, tko=1024, name=prefix + "_mm_dmerged", deps=deps)
    g["w_out"] = _mm_tn(sv["merged"], dx1b, p["w_out"], into=gbuf.get("w_out"), tm=4096, tn=1024, tko=512,
                        name=prefix + "_mm_dw_out")

    def gate_bwd(za, zs, ba, bs, aa, bb, dm):
        sa = jax.nn.sigmoid(za + ba)
        ss = jax.nn.sigmoid(zs + bs)
        daa, dbb = dm * sa, dm * ss
        dza, dzs = daa * aa * (1.0 - sa), dbb * bb * (1.0 - ss)
        return daa, dbb, dza, dzs, jnp.sum(dza, axis=0, keepdims=True), jnp.sum(dzs, axis=0, keepdims=True)

    z = sv["z"]
    da, dbm, dza, dzs, dba, dbs = _rowmap(
        gate_bwd,
        [(z, "row", gblk, off_g // gblk), (z, "row", gblk, off_g // gblk + ngb),
         (p["gate_bias"], "vec", gblk, 0), (p["gate_bias"], "vec", gblk, ngb),
         (sv["a"], "row", gblk, 0), (sv["bm"], "row", gblk, 0), (dmerged, "row", gblk, 0)],
        [(d, BF16, "row", gblk), (d, BF16, "row", gblk), (d, BF16, "row", gblk), (d, BF16, "row", gblk),
         (d, F32, "acc", gblk), (d, F32, "acc", gblk)],
        rows=t, tm=_row_tile(32 * gblk, t), ncol=ngb, name=prefix + "_gate_bwd")
    g["gate_bias"] = jnp.concatenate([dba, dbs], axis=1)
    dya = _mm_nt(da, p["w_attn_branch"], out_dtype=BF16, tm=1024, tn=512, tko=1024, name=prefix + "_mm_dya")
    g["w_attn_branch"] = _mm_tn(sv["ya"], da, p["w_attn_branch"], into=gbuf.get("w_attn_branch"), tm=4096, tn=512,
                                tko=512, name=prefix + "_mm_dw_ab")
    dy2 = _mm_nt(dbm, p["w_ssm_branch"], out_dtype=BF16, tm=1024, tn=512, tko=1024, name=prefix + "_mm_dy2")
    g["w_ssm_branch"] = _mm_tn(sv["y2"], dbm, p["w_ssm_branch"], into=gbuf.get("w_ssm_branch"), tm=4096, tn=512,
                               tko=512, name=prefix + "_mm_dw_sb")

    def glu_bwd(y_raw, pre, b, dy):
        yg = jax.nn.gelu(y_raw)
        _, vjp = jax.vjp(lambda a_, b_, c_: a_ * jax.nn.sigmoid(b_ + c_), yg, pre, b)
        dyg, dpre, db = vjp(dy)
        return dyg, dpre, db

    dyg_direct, dpre, g["ssm_glu_b"] = _rowmap(
        glu_bwd, [(sv["y_raw"], "row", sw, 0), (sv["pre"], "row", sw, 0), (p["ssm_glu_b"], "vec", sw, 0), (dy2, "row", sw, 0)],
        [(sw, F32, "row", sw), (sw, BF16, "row", sw), (sw, F32, "acc", sw)],
        rows=t, tm=_row_tile(24 * sw, t), name=prefix + "_glu_bwd")
    dyg2 = _mm_nt(dpre, p["ssm_glu_w"], out_dtype=F32, tm=1024, tn=1024, tko=1024, name=prefix + "_mm_dyg")
    g["ssm_glu_w"] = _mm_tn(sv["yg"], dpre, p["ssm_glu_w"], into=gbuf.get("ssm_glu_w"), tm=4096, tn=1024, tko=512,
                            name=prefix + "_mm_dw_glu")

    def gelu_bwd(y_raw, d1, d2):
        _, vjp = jax.vjp(jax.nn.gelu, y_raw)
        return vjp(d1 + d2)[0]

    dy_raw, = _rowmap(gelu_bwd, [(sv["y_raw"], "row", sw, 0), (dyg_direct, "row", sw, 0), (dyg2, "row", sw, 0)],
                      [(sw, F32, "row", sw)], rows=t, tm=_row_tile(20 * sw, t), name=prefix + "_gelu_bwd")
    sp = sv["sp"]
    du, dbr, dbi, dcr, dci, dar, dai, g["ssm_d"] = _ssm_bwd(
        z, dy_raw, sv["s0r"], sv["s0i"], sp["ar"], sp["ai"], sp["bblk_r"], sp["bblk_i"], sp["cblk_r"], sp["cblk_i"],
        p["ssm_d"], u_col=off_u, tc=dims["tc"], name=prefix + "_ssm_bwd")
    ngr, pst = p["lam_re"].shape
    hch = SSM_GROUP_CH
    dlr, dli, dldt, dbrt, dbit = _ssm_prep_bwd(
        p["lam_re"], p["lam_im"], p["log_dt"], p["b_re_t"], p["b_im_t"],
        dar.reshape(ngr, pst), dai.reshape(ngr, pst), _blockdiag_extract(dbr, hch, pst), _blockdiag_extract(dbi, hch, pst),
        name=prefix + "_ssm_prep_bwd")
    g.update(ssm_lambda_re=dlr, ssm_lambda_im=dli, ssm_log_dt=dldt.reshape(ngr),
             ssm_b_re=jnp.swapaxes(dbrt, 1, 2), ssm_b_im=jnp.swapaxes(dbit, 1, 2),
             ssm_c_re=_blockdiag_extract(dcr, hch, pst), ssm_c_im=_blockdiag_extract(dci, hch, pst))

    dq, dkv, g["q_norm_g"], g["k_norm_g"], g["attn_sinks"] = _attn_bwd(
        z, dya, p["q_norm_g"], p["k_norm_g"], p["attn_sinks"], n_q=dims["n_q"], n_kv=dims["n_kv"], name=prefix + "_attn_bwd")

    dz = jnp.concatenate([dq, dkv, du, dza, dzs], axis=1)
    deps = before_in(dq, g) if before_in else ()
    dh = _mm_nt(dz, p["w_in"], out_dtype=F32, tm=512, tn=1664, tko=512, name=prefix + "_mm_dh", deps=deps)
    g["w_in"] = _mm_tn(sv["h"], dz, p["w_in"], into=gbuf.get("w_in"), tm=4096, tn=1664, tko=512,
                       name=prefix + "_mm_dw_in")
    dx, dxb, g["norm_mix_g"] = _rowmap(
        norm_bwd, [(sv["x"], "row", d, 0), (p["norm_mix_g"], "vec", d, 0), (dh, "row", d, 0), (dx1, "row", d, 0)],
        [(d, F32, "row", d), (d, BF16, "row", d), (d, F32, "acc", d)],
        rows=t, tm=_row_tile(22 * d, t), name=prefix + "_norm_mix_bwd")
    return dx, dxb, g


def _loss_and_grad(y, target):
    t, d = y.shape

    def fn(yy, tt):
        e = yy - tt
        dy = e * (1.0 / d)
        return dy, dy, jnp.sum(e * e, keepdims=True).reshape(1, 1)

    dy, dyb, sq = _rowmap(fn, [(y, "row", d, 0), (target, "row", d, 0)],
                          [(d, F32, "row", d), (d, BF16, "row", d), (1, F32, "acc", 1)],
                          rows=t, tm=_row_tile(14 * d, t), name="loss")
    return sq, dy, dyb


def _local_step(x, target, n_layers, weights_of, dims, fwd_tick=None, bwd_tick=None):
    saved, params = [], []
    h = x
    for l in range(n_layers):
        p, deps = weights_of(l, h)
        params.append(p)
        tick = (lambda point, arr, l=l: fwd_tick(l, point, arr)) if fwd_tick else None
        h, sv = _layer_fwd(h, p, dims, "l%d" % l, deps, tick)
        saved.append(sv)
    sq, dy, dyb = _loss_and_grad(h, target)
    grads = [None] * n_layers
    deps = ()
    for l in reversed(range(n_layers)):
        if bwd_tick:
            ffn_done = lambda arr, g, l=l: bwd_tick(l, "ffn", (arr, g["w_ffn_in"], g["w_ffn_out"]),
                                                    {n: g[n] for n in ("w_ffn_in", "w_ffn_out")})
            mixer_done = lambda arr, g, l=l: bwd_tick(l, "mixer", (arr, g["ssm_d"]) + tuple(g[n] for n in MIXER_WEIGHTS),
                                                      {n: g[n] for n in MIXER_WEIGHTS})
        else:
            ffn_done = mixer_done = None
        dy, dyb, grads[l] = _layer_bwd(dy, dyb, saved[l], params[l], dims, "l%d" % l, {}, deps=deps,
                                       before_mixer=ffn_done, before_in=mixer_done)
        if bwd_tick:
            deps = bwd_tick(l, "in", (dy, grads[l]["w_in"]), {"w_in": grads[l]["w_in"]})
    return sq, dy, grads, deps


COL_SHARDED = ("w_in", "w_attn_branch", "w_ssm_branch", "w_ffn_in")
ROW_SHARDED = ("ssm_glu_w", "w_out", "w_ffn_out")
BIG_WEIGHTS = COL_SHARDED + ROW_SHARDED
WEIGHT_NAMES = ("norm_mix_g", "w_in", "gate_bias", "q_norm_g", "k_norm_g", "attn_sinks", "ssm_lambda_re",
                "ssm_lambda_im", "ssm_log_dt", "ssm_b_re", "ssm_b_im", "ssm_c_re", "ssm_c_im", "ssm_d", "ssm_glu_w",
                "ssm_glu_b", "w_attn_branch", "w_ssm_branch", "w_out", "norm_ffn_g", "w_ffn_in", "w_ffn_out")
SMALL_WEIGHTS = tuple(n for n in WEIGHT_NAMES if n not in BIG_WEIGHTS)
MIXER_WEIGHTS = ("w_out", "w_attn_branch", "w_ssm_branch", "ssm_glu_w")
WEIGHT_GROUPS = {"in": ("w_in",), "mixer": MIXER_WEIGHTS, "ffn": ("w_ffn_in", "w_ffn_out")}


def _dims(d, shapes, tc):
    s, _, aw, _ = shapes["w_attn_branch"]
    sw = shapes["w_ssm_branch"][2]
    in_w = shapes["w_in"][3] * s
    kvw = (in_w - aw - sw - 2 * d) // 2
    ff = shapes["w_ffn_out"][2] * s
    return dict(aw=aw, kvw=kvw, sw=sw, ff=ff, n_q=aw // HEAD_DIM, n_kv=kvw // HEAD_DIM, tc=tc)


def _big_params(big):
    p = {n: _Weight(a, 0, "col") for n, a in big.items() if n in COL_SHARDED}
    p.update({n: _Weight(a.reshape(1, 1, -1, a.shape[-1]), 0, "col") for n, a in big.items() if n in ROW_SHARDED})
    return p


def _layer_params(l, small):
    p = {}
    for n in ("norm_mix_g", "gate_bias", "q_norm_g", "k_norm_g", "ssm_d", "ssm_glu_b", "norm_ffn_g"):
        p[n] = small[n][l][None]
    p["attn_sinks"] = small["attn_sinks"][l]
    p["lam_re"] = small["ssm_lambda_re"][l]
    p["lam_im"] = small["ssm_lambda_im"][l]
    p["log_dt"] = small["ssm_log_dt"][l][:, None]
    p["b_re_t"] = jnp.swapaxes(small["ssm_b_re"][l], 1, 2)
    p["b_im_t"] = jnp.swapaxes(small["ssm_b_im"][l], 1, 2)
    p["c_re"] = small["ssm_c_re"][l]
    p["c_im"] = small["ssm_c_im"][l]
    return p


_ANY = pl.BlockSpec(memory_space=pl.ANY)
_MESH_ID = pl.DeviceIdType.MESH


def _coords():
    return lax.axis_index("x"), lax.axis_index("y"), lax.axis_index("c")


def _my_chip():
    return (2 * lax.axis_index("x") + lax.axis_index("y")).astype(jnp.int32).reshape(1)


def _my_core():
    return lax.axis_index("c").astype(jnp.int32).reshape(1)


def _cast_into_slot(w, layer, *, name, deps=()):
    _, r, c = w.shape
    tm = _row_tile(12 * c, r)

    def body(me_ref, w_ref, *rest):
        rest[-1][...] = w_ref[...].astype(rest[-1].dtype)

    return pl.pallas_call(
        body,
        out_shape=jax.ShapeDtypeStruct((N_CHIPS, 1, r, c), BF16),
        grid_spec=pltpu.PrefetchScalarGridSpec(
            num_scalar_prefetch=1,
            grid=(r // tm,),
            in_specs=[pl.BlockSpec((None, tm, c), lambda i, me: (layer, i, 0))]
            + [pl.BlockSpec(memory_space=pl.ANY)] * len(deps),
            out_specs=pl.BlockSpec((None, None, tm, c), lambda i, me: (me[0], 0, i, 0)),
        ),
        compiler_params=_cparams("parallel"),
        name=name,
    )(_my_chip(), w, *deps)


_HBM = pl.BlockSpec(memory_space=pltpu.HBM)
_SEM = pl.BlockSpec(memory_space=pltpu.SEMAPHORE)
_DATAFLOW = pltpu.SideEffectType.DATAFLOW_SIDE_EFFECTING


class _SplitExchange:
    def __init__(self, srcs, lands, build, n_copies, name):
        self.build, self.n, self.name = build, n_copies, name
        self.ns, self.nl = len(srcs), len(lands)
        self.bufs = [pltpu.with_memory_space_constraint(a, pltpu.HBM) for a in list(srcs) + list(lands)]

    def _copies(self, refs, send_sems, recv_sems):
        triples = self.build(refs[:self.ns], refs[self.ns:self.ns + self.nl])
        assert len(triples) == self.n
        return [pltpu.make_async_remote_copy(src_ref=s, dst_ref=d, send_sem=send_sems.at[k], recv_sem=recv_sems.at[k],
                                             device_id=to, device_id_type=_MESH_ID) for k, (s, d, to) in enumerate(triples)]

    def start(self, deps=()):
        nb = self.ns + self.nl

        def body(*refs):
            outs = refs[nb + len(deps):]
            for cp in self._copies(refs, outs[0], outs[1]):
                cp.start()
            outs[-1][...] = jnp.zeros_like(outs[-1])

        sems = pltpu.SemaphoreType.DMA((self.n,))
        res = pl.pallas_call(
            body,
            out_shape=(sems, sems, *[pltpu.HBM(b.shape, b.dtype) for b in self.bufs], jax.ShapeDtypeStruct((8, 128), F32)),
            in_specs=[_HBM] * nb + [_ANY] * len(deps),
            out_specs=(_SEM, _SEM, *[_HBM] * nb, pl.BlockSpec(memory_space=pltpu.VMEM)),
            input_output_aliases={i: 2 + i for i in range(nb)},
            compiler_params=pltpu.CompilerParams(has_side_effects=_DATAFLOW),
            name=self.name + "_start",
        )(*self.bufs, *deps)
        self.send_sems, self.recv_sems = res[0], res[1]
        self.bufs = list(res[2:2 + nb])
        return res[-1]

    def wait(self, after=()):
        nb = self.ns + self.nl
        after = tuple(after) if isinstance(after, (tuple, list)) else (after,)

        def body(*refs):
            for cp in self._copies(refs, refs[nb], refs[nb + 1]):
                cp.wait_send()
                cp.wait_recv()

        res = pl.pallas_call(
            body,
            out_shape=tuple(pltpu.HBM(b.shape, b.dtype) for b in self.bufs),
            in_specs=[_HBM] * nb + [_SEM, _SEM] + [_ANY] * len(after),
            out_specs=tuple([_HBM] * nb),
            input_output_aliases={i: i for i in range(nb)},
            compiler_params=pltpu.CompilerParams(has_side_effects=_DATAFLOW),
            name=self.name + "_wait",
        )(*self.bufs, self.send_sems, self.recv_sems, *after)
        res = list(res)
        return res[:self.ns], res[self.ns:]


def _other_chips(x, y):
    return [(1 - x, y), (x, 1 - y), (1 - x, 1 - y)]


def _gather_steps(bufs, tag, deps, publish):
    n = len(bufs)
    half = lambda ref, i, slot, hc: ref.at[slot, :, pl.ds(hc * (bufs[i].shape[2] // 2), bufs[i].shape[2] // 2), :]

    def over_ici(srcs, lands):
        x, y, c = _coords()
        me = 2 * x + y
        return [(half(srcs[i], i, me, c), half(srcs[i], i, me, c), (px, py, c))
                for i in range(n) for px, py in _other_chips(x, y)]

    def to_sibling(srcs, lands):
        x, y, c = _coords()
        return [(half(srcs[i], i, 2 * px + py, c), half(srcs[i], i, 2 * px + py, c), (x, y, 1 - c))
                for i in range(n) for px, py in _other_chips(x, y)]

    ex = _SplitExchange(bufs, [], over_ici, 3 * n, tag + "_ici")
    after = yield ex.start(deps)
    bufs, _ = ex.wait(after)
    ex = _SplitExchange(bufs, [], to_sibling, 3 * n, tag + "_d2d")
    after = yield ex.start()
    bufs, _ = ex.wait(after)
    publish(bufs)


def _reduce_steps(grads, tag, publish):
    n = len(grads)
    rh = [g.shape[1] // 2 for g in grads]
    theirs = [lax.empty((g.shape[0], g.shape[1] // 2, g.shape[2]), F32) for g in grads]

    def halves(srcs, lands):
        x, y, c = _coords()
        return [(srcs[i].at[:, pl.ds((1 - c) * rh[i], rh[i]), :], lands[i], (x, y, 1 - c)) for i in range(n)]

    def chips(srcs, lands):
        x, y, c = _coords()
        me = 2 * x + y
        return [(srcs[i].at[2 * px + py], lands[i].at[me], (px, py, c)) for i in range(n) for px, py in _other_chips(x, y)]

    def sibling(srcs, lands):
        x, y, c = _coords()
        return [(srcs[i], lands[i], (x, y, 1 - c)) for i in range(n)]

    ex = _SplitExchange(grads, theirs, halves, n, tag + "_halves")
    after = yield ex.start()
    grads, theirs = ex.wait(after)
    parts = [_add_own_half(g, t, name="%s_add_own_half_%d" % (tag, i)) for i, (g, t) in enumerate(zip(grads, theirs))]
    ex = _SplitExchange(parts, [lax.empty(p.shape, p.dtype) for p in parts], chips, 3 * n, tag + "_chips")
    after = yield ex.start()
    parts, got = ex.wait(after)
    mine = [_sum_chips(p, g, name="%s_sum_chips_%d" % (tag, i)) for i, (p, g) in enumerate(zip(parts, got))]
    ex = _SplitExchange(mine, [lax.empty(m.shape, m.dtype) for m in mine], sibling, n, tag + "_sibling")
    after = yield ex.start()
    mine, theirs = ex.wait(after)
    publish(list(zip(mine, theirs)))


def _allreduce_steps(buf, tag, publish):
    r, c = buf.shape

    def to_sibling(srcs, lands):
        x, y, cc = _coords()
        return [(srcs[0], lands[0], (x, y, 1 - cc))]

    def over_ici(srcs, lands):
        x, y, cc = _coords()
        me = 2 * x + y
        return [(srcs[0].at[me], srcs[0].at[me], (px, py, cc)) for px, py in _other_chips(x, y)]

    def halves(srcs, lands):
        x, y, cc = _coords()
        return [(srcs[0].at[cc], srcs[0].at[cc], (x, y, 1 - cc))]

    ex = _SplitExchange([buf], [lax.empty(buf.shape, buf.dtype)], to_sibling, 1, tag + "_cores")
    after = yield ex.start()
    (mine,), (theirs,) = ex.wait(after)
    ex = _SplitExchange([_add_half_into_slot(mine, theirs, name=tag + "_chip_sum")], [], over_ici, N_CHIPS - 1, tag + "_chips")
    after = yield ex.start()
    (parts,), _ = ex.wait(after)
    total_half = _sum_slots(parts, name=tag + "_sum_chips")
    ex = _SplitExchange([_place_into_slot(total_half, 2, _my_core(), name=tag + "_place_half")], [], halves, 1, tag + "_halves")
    after = yield ex.start()
    (both,), _ = ex.wait(after)
    publish(both.reshape(r, c))


class _Exchanges:
    def __init__(self):
        self.running = []

    def launch(self, steps):
        self.running.append(steps)
        return next(steps)

    def advance(self, steps, after):
        try:
            return steps.send(after)
        except StopIteration:
            self.running.remove(steps)
            return None

    def advance_all(self, after):
        tokens = [self.advance(steps, after) for steps in list(self.running)]
        return tuple(t for t in tokens if t is not None)


def _add_own_half(g, theirs, *, name):
    s, r, c = g.shape
    rh = r // 2
    tm = _row_tile(10 * c, rh)
    nb = rh // tm

    def body(core_ref, g_ref, t_ref, o_ref):
        o_ref[...] = (g_ref[...] + t_ref[...]).astype(o_ref.dtype)

    return pl.pallas_call(
        body,
        out_shape=jax.ShapeDtypeStruct((s, rh, c), BF16),
        grid_spec=pltpu.PrefetchScalarGridSpec(
            num_scalar_prefetch=1,
            grid=(s, nb),
            in_specs=[pl.BlockSpec((None, tm, c), lambda k, i, core: (k, core[0] * nb + i, 0)),
                      pl.BlockSpec((None, tm, c), lambda k, i, core: (k, i, 0))],
            out_specs=pl.BlockSpec((None, tm, c), lambda k, i, core: (k, i, 0)),
        ),
        compiler_params=_cparams("parallel", "parallel"),
        name=name,
    )(_my_core(), g, theirs)


def _sum_chips(part, got, *, name):
    s, rh, c = part.shape
    tm = _row_tile(14 * c, rh)

    def body(me_ref, p_ref, a_ref, b_ref, c_ref, o_ref):
        o_ref[...] = ((p_ref[...].astype(F32) + a_ref[...].astype(F32)) + b_ref[...].astype(F32)) + c_ref[...].astype(F32)

    slot = lambda k: (lambda i, me: ((me[0] + k) % s, i, 0))
    return pl.pallas_call(
        body,
        out_shape=jax.ShapeDtypeStruct((rh, c), F32),
        grid_spec=pltpu.PrefetchScalarGridSpec(
            num_scalar_prefetch=1,
            grid=(rh // tm,),
            in_specs=[pl.BlockSpec((None, tm, c), slot(k)) for k in range(s)],
            out_specs=pl.BlockSpec((tm, c), lambda i, me: (i, 0)),
        ),
        compiler_params=_cparams("parallel"),
        name=name,
    )(_my_chip(), part, got, got, got)


def _place_into_slot(buf, n_slots, slot, *, name):
    r, c = buf.shape
    tm = _row_tile(8 * c, r)

    def body(slot_ref, i_ref, o_ref):
        o_ref[...] = i_ref[...]

    return pl.pallas_call(
        body,
        out_shape=jax.ShapeDtypeStruct((n_slots, r, c), buf.dtype),
        grid_spec=pltpu.PrefetchScalarGridSpec(
            num_scalar_prefetch=1,
            grid=(r // tm,),
            in_specs=[pl.BlockSpec((tm, c), lambda i, s: (i, 0))],
            out_specs=pl.BlockSpec((None, tm, c), lambda i, s: (s[0], i, 0)),
        ),
        compiler_params=_cparams("parallel"),
        name=name,
    )(slot, buf)


def _add_half_into_slot(mine, theirs, *, name):
    r, c = mine.shape
    rh = r // 2
    tm = _row_tile(12 * c, rh)
    nb = rh // tm
    where = jnp.concatenate([_my_chip(), _my_core()])

    def body(where_ref, a_ref, b_ref, o_ref):
        o_ref[...] = a_ref[...] + b_ref[...]

    half = pl.BlockSpec((tm, c), lambda i, w: (w[1] * nb + i, 0))
    return pl.pallas_call(
        body,
        out_shape=jax.ShapeDtypeStruct((N_CHIPS, rh, c), mine.dtype),
        grid_spec=pltpu.PrefetchScalarGridSpec(
            num_scalar_prefetch=1,
            grid=(nb,),
            in_specs=[half, half],
            out_specs=pl.BlockSpec((None, tm, c), lambda i, w: (w[0], i, 0)),
        ),
        compiler_params=_cparams("parallel"),
        name=name,
    )(where, mine, theirs)


def _sum_slots(arr, *, name):
    s, r, c = arr.shape
    tm = _row_tile(4 * c * (s + 1), r)

    def body(*refs):
        acc = refs[0][...]
        for ref in refs[1:s]:
            acc = acc + ref[...]
        refs[s][...] = acc

    return pl.pallas_call(
        body,
        out_shape=jax.ShapeDtypeStruct((r, c), arr.dtype),
        grid=(r // tm,),
        in_specs=[pl.BlockSpec((None, tm, c), lambda i, k=k: (k, i, 0)) for k in range(s)],
        out_specs=pl.BlockSpec((tm, c), lambda i: (i, 0)),
        compiler_params=_cparams("parallel"),
        name=name,
    )(*([arr] * s))


def _adamw_fn(w, g, m, v):
    m = ADAM_B1 * m + (1.0 - ADAM_B1) * g
    v = ADAM_B2 * v + (1.0 - ADAM_B2) * jnp.square(g)
    m_hat = m / (1.0 - ADAM_B1 ** ADAM_STEP)
    v_hat = v / (1.0 - ADAM_B2 ** ADAM_STEP)
    delta = -ADAM_LR * (m_hat / (jnp.sqrt(v_hat) + ADAM_EPS) + ADAM_WD * w)
    return delta, m, v


def _adamw(w, g, m, v, *, name):
    rows, cols = w.shape
    ins = [(a, "row", cols, 0) for a in (w, g, m, v)]
    outs = [(cols, F32, "row", cols)] * 3
    return _rowmap(_adamw_fn, ins, outs, rows=rows, tm=_row_tile(56 * cols, rows), name=name)


def _adamw_sharded(w, m, v, g_mine, g_sibling, layer, into, *, name, deps=()):
    nl, r, c = w.shape
    rh = r // 2
    tm = _row_tile(40 * c, rh)
    nb = rh // tm
    n_into = 0 if into is None else 4

    def body(core_ref, w_ref, m_ref, v_ref, a_ref, b_ref, *rest):
        g_ref, d_ref, nm_ref, nv_ref = rest[n_into + len(deps):]
        g = jnp.where(pl.program_id(0) == core_ref[0], a_ref[...], b_ref[...])
        delta, nm, nv = _adamw_fn(w_ref[...], g, m_ref[...], v_ref[...])
        g_ref[...] = g
        d_ref[...] = delta
        nm_ref[...] = nm
        nv_ref[...] = nv

    whole = pl.BlockSpec((None, tm, c), lambda h, i, core: (layer, h * nb + i, 0))
    half = pl.BlockSpec((tm, c), lambda h, i, core: (i, 0))
    shape = jax.ShapeDtypeStruct((nl, r, c), F32)
    return pl.pallas_call(
        body,
        out_shape=(shape, shape, shape, shape),
        grid_spec=pltpu.PrefetchScalarGridSpec(
            num_scalar_prefetch=1,
            grid=(2, nb),
            in_specs=[whole, whole, whole, half, half] + [pl.BlockSpec(memory_space=pl.ANY)] * (n_into + len(deps)),
            out_specs=(whole, whole, whole, whole),
        ),
        input_output_aliases={6 + k: k for k in range(n_into)},
        compiler_params=_cparams("parallel", "parallel"),
        name=name,
    )(_my_core(), w, m, v, g_mine, g_sibling, *(into or ()), *deps)


def _pack(arrays):
    flat = jnp.concatenate([a.reshape(-1) for a in arrays])
    pad = (-flat.shape[0]) % (256 * 128)
    return jnp.pad(flat, (0, pad)).reshape(-1, 128)


def _unpack(buf, shapes):
    flat = buf.reshape(-1)
    out, off = [], 0
    for s in shapes:
        n = math.prod(s)
        out.append(flat[off:off + n].reshape(s))
        off += n
    return out


def kernel(x, norm_mix_g, w_in, gate_bias, q_norm_g, k_norm_g, attn_sinks, ssm_lambda_re, ssm_lambda_im, ssm_log_dt, ssm_b_re, ssm_b_im, ssm_c_re, ssm_c_im, ssm_d, ssm_glu_w, ssm_glu_b, w_attn_branch, w_ssm_branch, w_out, norm_ffn_g, w_ffn_in, w_ffn_out, loss_target, m_norm_mix_g, m_w_in, m_gate_bias, m_q_norm_g, m_k_norm_g, m_attn_sinks, m_ssm_lambda_re, m_ssm_lambda_im, m_ssm_log_dt, m_ssm_b_re, m_ssm_b_im, m_ssm_c_re, m_ssm_c_im, m_ssm_d, m_ssm_glu_w, m_ssm_glu_b, m_w_attn_branch, m_w_ssm_branch, m_w_out, m_norm_ffn_g, m_w_ffn_in, m_w_ffn_out, v_norm_mix_g, v_w_in, v_gate_bias, v_q_norm_g, v_k_norm_g, v_attn_sinks, v_ssm_lambda_re, v_ssm_lambda_im, v_ssm_log_dt, v_ssm_b_re, v_ssm_b_im, v_ssm_c_re, v_ssm_c_im, v_ssm_d, v_ssm_glu_w, v_ssm_glu_b, v_w_attn_branch, v_w_ssm_branch, v_w_out, v_norm_ffn_g, v_w_ffn_in, v_w_ffn_out):
    w = dict(norm_mix_g=norm_mix_g, w_in=w_in, gate_bias=gate_bias, q_norm_g=q_norm_g, k_norm_g=k_norm_g,
             attn_sinks=attn_sinks, ssm_lambda_re=ssm_lambda_re, ssm_lambda_im=ssm_lambda_im, ssm_log_dt=ssm_log_dt,
             ssm_b_re=ssm_b_re, ssm_b_im=ssm_b_im, ssm_c_re=ssm_c_re, ssm_c_im=ssm_c_im, ssm_d=ssm_d,
             ssm_glu_w=ssm_glu_w, ssm_glu_b=ssm_glu_b, w_attn_branch=w_attn_branch, w_ssm_branch=w_ssm_branch,
             w_out=w_out, norm_ffn_g=norm_ffn_g, w_ffn_in=w_ffn_in, w_ffn_out=w_ffn_out)
    m = dict(norm_mix_g=m_norm_mix_g, w_in=m_w_in, gate_bias=m_gate_bias, q_norm_g=m_q_norm_g, k_norm_g=m_k_norm_g,
             attn_sinks=m_attn_sinks, ssm_lambda_re=m_ssm_lambda_re, ssm_lambda_im=m_ssm_lambda_im,
             ssm_log_dt=m_ssm_log_dt, ssm_b_re=m_ssm_b_re, ssm_b_im=m_ssm_b_im, ssm_c_re=m_ssm_c_re,
             ssm_c_im=m_ssm_c_im, ssm_d=m_ssm_d, ssm_glu_w=m_ssm_glu_w, ssm_glu_b=m_ssm_glu_b,
             w_attn_branch=m_w_attn_branch, w_ssm_branch=m_w_ssm_branch, w_out=m_w_out, norm_ffn_g=m_norm_ffn_g,
             w_ffn_in=m_w_ffn_in, w_ffn_out=m_w_ffn_out)
    v = dict(norm_mix_g=v_norm_mix_g, w_in=v_w_in, gate_bias=v_gate_bias, q_norm_g=v_q_norm_g, k_norm_g=v_k_norm_g,
             attn_sinks=v_attn_sinks, ssm_lambda_re=v_ssm_lambda_re, ssm_lambda_im=v_ssm_lambda_im,
             ssm_log_dt=v_ssm_log_dt, ssm_b_re=v_ssm_b_re, ssm_b_im=v_ssm_b_im, ssm_c_re=v_ssm_c_re,
             ssm_c_im=v_ssm_c_im, ssm_d=v_ssm_d, ssm_glu_w=v_ssm_glu_w, ssm_glu_b=v_ssm_glu_b,
             w_attn_branch=v_w_attn_branch, w_ssm_branch=v_w_ssm_branch, w_out=v_w_out, norm_ffn_g=v_norm_ffn_g,
             w_ffn_in=v_w_ffn_in, w_ffn_out=v_w_ffn_out)
    n_layers = norm_mix_g.shape[0]
    d_model = x.shape[-1]
    seq = x.shape[1]

    exchanges = _Exchanges()
    params = [_layer_params(l, w) for l in range(n_layers)]
    gathers = {}

    def gather(l, group, bufs, deps=()):
        names = WEIGHT_GROUPS[group]
        steps = _gather_steps(bufs, "ag%d_%s" % (l, group), deps,
                              lambda got: params[l].update(_big_params(dict(zip(names, got)))))
        gathers[l, group] = steps
        return exchanges.launch(steps)

    started = gather(0, "in", [_cast_into_slot(w["w_in"], 0, name="cast0_w_in")])
    casts = {(l, group): [_cast_into_slot(w[n], l, name="cast%d_%s" % (l, n), deps=(started,)) for n in WEIGHT_GROUPS[group]]
             for l in range(n_layers) for group in WEIGHT_GROUPS if (l, group) != (0, "in")}
    forwarded = exchanges.advance(gathers[0, "in"], tuple(b for bufs in casts.values() for b in bufs))
    first_deps = (forwarded, gather(0, "mixer", casts[0, "mixer"], (forwarded,)), gather(0, "ffn", casts[0, "ffn"], (forwarded,)))
    sizes = {n: (N_CHIPS, 1) + w[n].shape[1:] for n in BIG_WEIGHTS}
    dims = _dims(d_model, sizes, min(512, seq))

    def weights_of(l, h):
        if l == 0:
            return params[0], first_deps
        for group in WEIGHT_GROUPS:
            exchanges.advance(gathers[l, group], h)
        return params[l], ()

    def fwd_tick(l, point, arr):
        tokens = []
        if l == 0 and point == "norm":
            tokens.append(exchanges.advance(gathers[0, "in"], arr))
        if l == 0 and point in ("in", "attn"):
            tokens.append(exchanges.advance(gathers[0, "mixer"], arr))
        if l == 0 and point in ("ssm", "out"):
            tokens.append(exchanges.advance(gathers[0, "ffn"], arr))
        if l + 1 < n_layers and point == "attn":
            tokens += [gather(l + 1, group, casts[l + 1, group]) for group in WEIGHT_GROUPS]
        if l + 1 < n_layers and point == "act":
            tokens += [exchanges.advance(gathers[l + 1, group], arr) for group in WEIGHT_GROUPS]
        return tuple(t for t in tokens if t is not None)

    reduced, ready = {}, []

    def bwd_tick(l, stage, arr, stage_grads):
        tokens = exchanges.advance_all(arr)
        names = tuple(stage_grads)

        def publish(halves):
            reduced.update({(l, n): h for n, h in zip(names, halves)})
            ready.append((l, names))

        grads4 = [stage_grads[n].reshape(N_CHIPS, -1, stage_grads[n].shape[-1]) for n in names]
        return tokens + (exchanges.launch(_reduce_steps(grads4, "rs%d_%s" % (l, stage), publish)),)

    sq, dx, grads, last_tokens = _local_step(x[0], loss_target[0], n_layers, weights_of, dims, fwd_tick, bwd_tick)
    loss = lax.psum(sq[0, 0], MESH_AXES) * (0.5 / d_model)

    small_shapes = [w[n].shape for n in SMALL_WEIGHTS]
    small_local = [jnp.stack([grads[l][n].reshape(w[n].shape[1:]) for l in range(n_layers)]) for n in SMALL_WEIGHTS]
    shared = []
    tokens = last_tokens + (exchanges.launch(_allreduce_steps(_pack(small_local), "small_grads", shared.append)),)

    adam = {n: None for n in BIG_WEIGHTS}
    grad, delta, new_m, new_v = {}, {}, {}, {}
    while exchanges.running or ready or shared:
        after = [dx]
        for l, names in ready[:2]:
            for n in names:
                mine, sibling = reduced[l, n]
                adam[n] = _adamw_sharded(w[n], m[n], v[n], mine, sibling, l, adam[n], name="adamw%d_%s" % (l, n), deps=tokens)
                after.append(adam[n][1])
        del ready[:2]
        if shared:
            grad.update(zip(SMALL_WEIGHTS, _unpack(shared.pop(), small_shapes)))
            for n in SMALL_WEIGHTS:
                flat = lambda a: a.reshape(-1, a.shape[-1])
                res = _adamw(flat(w[n]), flat(grad[n]), flat(m[n]), flat(v[n]), name="adamw_" + n)
                delta[n], new_m[n], new_v[n] = [r.reshape(w[n].shape) for r in res]
            after += [delta[n] for n in SMALL_WEIGHTS]
        tokens = exchanges.advance_all(tuple(after))
    for n in BIG_WEIGHTS:
        grad[n], delta[n], new_m[n], new_v[n] = adam[n]

    return (loss, dx[None], *[grad[n] for n in WEIGHT_NAMES], *[delta[n] for n in WEIGHT_NAMES],
            *[new_m[n] for n in WEIGHT_NAMES], *[new_v[n] for n in WEIGHT_NAMES])
```

```python
import math

import jax
import jax.numpy as jnp
from jax import lax
from jax.experimental import pallas as pl
from jax.experimental.pallas import tpu as pltpu

HEAD_DIM = 64
WINDOW = 128
SSM_GROUP_CH = 16
SSM_LANE_GROUPS = 8
RMS_EPS = 1e-6
ADAM_LR = 0.001
ADAM_B1 = 0.9
ADAM_B2 = 0.999
ADAM_EPS = 1e-08
ADAM_WD = 0.01
ADAM_STEP = 10
NEG_BIG = -1e30
MESH_AXES = ("x", "y", "c")
N_CHIPS = 4
N_DEV = 8
VMEM_LIMIT_BYTES = 56 * 1024 * 1024
BF16 = jnp.bfloat16
F32 = jnp.float32


def _cparams(*semantics):
    return pltpu.CompilerParams(dimension_semantics=semantics, vmem_limit_bytes=VMEM_LIMIT_BYTES)


def _pick(n, target, mult):
    if n <= target:
        return n
    best = None
    for d in range(mult, target + 1, mult):
        if n % d == 0:
            best = d
    assert best is not None, (n, target, mult)
    return best


def _rowmap(fn, ins, outs, *, rows, tm, ncol=1, name, deps=()):
    n_in = len(ins)
    nrow = rows // tm
    assert nrow * tm == rows

    in_specs = []
    for arr, kind, width, coloff in ins:
        if kind == "row":
            in_specs.append(pl.BlockSpec((tm, width), lambda j, i, o=coloff: (i, o + j)))
        elif kind == "vec":
            in_specs.append(pl.BlockSpec((1, width), lambda j, i, o=coloff: (0, o + j)))
        else:
            nd = arr.ndim
            in_specs.append(pl.BlockSpec(arr.shape, lambda j, i, nd=nd: (0,) * nd))
    out_specs, out_shapes = [], []
    for cols, dtype, kind, width in outs:
        if kind == "row":
            out_specs.append(pl.BlockSpec((tm, width), lambda j, i: (i, j)))
            out_shapes.append(jax.ShapeDtypeStruct((rows, cols), dtype))
        else:
            out_specs.append(pl.BlockSpec((1, width), lambda j, i: (0, j)))
            out_shapes.append(jax.ShapeDtypeStruct((1, cols), dtype))

    in_specs += [pl.BlockSpec(memory_space=pl.ANY)] * len(deps)

    def body(*refs):
        i = pl.program_id(1)
        res = fn(*[r[...].astype(F32) for r in refs[:n_in]])
        if not isinstance(res, (tuple, list)):
            res = (res,)
        for (cols, dtype, kind, width), ref, val in zip(outs, refs[n_in + len(deps):], res):
            if kind == "row":
                ref[...] = val.astype(ref.dtype)
            else:
                @pl.when(i == 0)
                def _():
                    ref[...] = jnp.zeros_like(ref)
                ref[...] += val.astype(ref.dtype)

    res = pl.pallas_call(
        body,
        out_shape=tuple(out_shapes),
        grid=(ncol, nrow),
        in_specs=in_specs,
        out_specs=tuple(out_specs),
        compiler_params=_cparams("parallel", "arbitrary"),
        name=name,
    )(*[a[0] for a in ins], *deps)
    return res


def _mm_body(dims, nk, has_add, unused_in=0):
    def body(*refs):
        if has_add:
            a_ref, b_ref, add_ref = refs[:3]
            o_ref = refs[3 + unused_in]
            rest = refs[4 + unused_in:]
        else:
            a_ref, b_ref = refs[:2]
            o_ref = refs[2 + unused_in]
            add_ref = None
            rest = refs[3 + unused_in:]
        part = lax.dot_general(a_ref[...], b_ref[...], (dims, ((), ())), preferred_element_type=F32)
        if nk == 1:
            if add_ref is not None:
                part = part + add_ref[...]
            o_ref[...] = part.astype(o_ref.dtype)
        else:
            acc_ref = rest[0]
            k = pl.program_id(2)

            @pl.when(k == 0)
            def _():
                acc_ref[...] = part

            @pl.when(k > 0)
            def _():
                acc_ref[...] += part

            @pl.when(k == nk - 1)
            def _():
                r = acc_ref[...]
                if add_ref is not None:
                    r = r + add_ref[...]
                o_ref[...] = r.astype(o_ref.dtype)
    return body


class _Weight:
    def __init__(self, arr, layer, kind):
        self.arr, self.layer, self.kind = arr, layer, kind
        self.s, _, self.r, self.c = arr.shape
        self.rows = self.r * (self.s if kind == "row" else 1)
        self.cols = self.c * (self.s if kind == "col" else 1)

    def tiles(self, tr, tc):
        return _pick(self.r, tr, 128), _pick(self.c, tc, 128)

    def index(self, tr, tc):
        layer = self.layer
        if self.kind == "col":
            per = self.c // tc
            return lambda rb, cb: (cb // per, layer, rb, cb % per)
        per = self.r // tr
        return lambda rb, cb: (rb // per, layer, rb % per, cb)


def _shard_index(kind, r, c, tr, tc):
    if kind == "col":
        per = c // tc
        return lambda rb, cb: (cb // per, rb, cb % per)
    per = r // tr
    return lambda rb, cb: (rb // per, rb % per, cb)


def _mm_nn(a, w, *, out_dtype, tm, tn, tk, name, add=None, deps=()):
    m, k = a.shape
    assert k == w.rows
    tm = _pick(m, tm, 16)
    tk, tn = w.tiles(tk, tn)
    nk = k // tk
    widx = w.index(tk, tn)
    in_specs = [pl.BlockSpec((tm, tk), lambda n, i, kk: (i, kk)),
                pl.BlockSpec((None, None, tk, tn), lambda n, i, kk: widx(kk, n))]
    args = [a, w.arr]
    if add is not None:
        in_specs.append(pl.BlockSpec((tm, tn), lambda n, i, kk: (i, n)))
        args.append(add)
    in_specs += [pl.BlockSpec(memory_space=pl.ANY)] * len(deps)
    args += list(deps)
    return pl.pallas_call(
        _mm_body(((1,), (0,)), nk, add is not None, unused_in=len(deps)),
        out_shape=jax.ShapeDtypeStruct((m, w.cols), out_dtype),
        grid=(w.cols // tn, m // tm, nk),
        in_specs=in_specs,
        out_specs=pl.BlockSpec((tm, tn), lambda n, i, kk: (i, n)),
        scratch_shapes=[pltpu.VMEM((tm, tn), F32)] if nk > 1 else [],
        compiler_params=_cparams("parallel", "parallel", "arbitrary"),
        name=name,
    )(*args)


def _interleaved_block(k, n_blocks):
    half = n_blocks // 2
    if isinstance(k, int):
        return 2 * k if k < half else 2 * (k - half) + 1
    return jnp.where(k < half, 2 * k, 2 * (k - half) + 1)


def _mm_nt(a, w, *, out_dtype, tm, tn, tko, name, deps=(), interleaved=0):
    m, n = a.shape
    assert n == w.cols
    tm = _pick(m, tm, 16)
    if w.kind == "col" and w.s > 1:
        tko = _pick(w.r, tko, 128)
        layer, nsh, width = w.layer, w.s, w.c
        blk = interleaved or width
        per = width // blk

        def body(a_ref, w_ref, *rest):
            o_ref = rest[len(deps)]
            acc = None
            for k in range(nsh * per):
                s, j = divmod(k, per)
                at = (_interleaved_block(k, nsh * per) if interleaved else k) * blk
                part = lax.dot_general(a_ref[:, at:at + blk], w_ref[s, :, j * blk:(j + 1) * blk], (((1,), (1,)), ((), ())),
                                       preferred_element_type=F32)
                acc = part if acc is None else acc + part
            o_ref[...] = acc.astype(o_ref.dtype)

        return pl.pallas_call(
            body,
            out_shape=jax.ShapeDtypeStruct((m, w.rows), out_dtype),
            grid=(w.rows // tko, m // tm),
            in_specs=[pl.BlockSpec((tm, n), lambda ko, i: (i, 0)),
                      pl.BlockSpec((nsh, None, tko, width), lambda ko, i: (0, layer, ko, 0))]
            + [pl.BlockSpec(memory_space=pl.ANY)] * len(deps),
            out_specs=pl.BlockSpec((tm, tko), lambda ko, i: (i, ko)),
            compiler_params=_cparams("parallel", "parallel"),
            name=name,
        )(a, w.arr, *deps)
    tko, tn = w.tiles(tko, tn)
    nk = n // tn
    widx = w.index(tko, tn)
    return pl.pallas_call(
        _mm_body(((1,), (1,)), nk, False, unused_in=len(deps)),
        out_shape=jax.ShapeDtypeStruct((m, w.rows), out_dtype),
        grid=(w.rows // tko, m // tm, nk),
        in_specs=[pl.BlockSpec((tm, tn), lambda ko, i, nn: (i, nn)),
                  pl.BlockSpec((None, None, tko, tn), lambda ko, i, nn: widx(ko, nn))]
        + [pl.BlockSpec(memory_space=pl.ANY)] * len(deps),
        out_specs=pl.BlockSpec((tm, tko), lambda ko, i, nn: (i, ko)),
        scratch_shapes=[pltpu.VMEM((tm, tko), F32)] if nk > 1 else [],
        compiler_params=_cparams("parallel", "parallel", "arbitrary"),
        name=name,
    )(a, w.arr, *deps)


def _mm_tn(a, c, w, *, into, tm, tn, tko, name, interleaved=False):
    m, k = a.shape
    tm = _pick(m, tm, 16)
    layer = w.layer
    mc, n = c.shape
    assert mc == m and k == w.rows and n == w.cols
    tko, tn = w.tiles(tko, tn)
    nk = m // tm
    oidx = _shard_index(w.kind, w.r, w.c, tko, tn)
    n_blocks = n // tn
    c_block = (lambda nn: _interleaved_block(nn, n_blocks)) if interleaved else (lambda nn: nn)
    in_specs = [pl.BlockSpec((tm, tko), lambda ko, nn, mm: (mm, ko)),
                pl.BlockSpec((tm, tn), lambda ko, nn, mm: (mm, c_block(nn)))]
    args = [a, c]
    if into is not None:
        in_specs.append(pl.BlockSpec(memory_space=pl.ANY))
        args.append(into)
    return pl.pallas_call(
        _mm_body(((0,), (0,)), nk, False, unused_in=len(args) - 2),
        out_shape=jax.ShapeDtypeStruct((w.arr.shape[1], w.s, w.r, w.c), F32),
        grid=(k // tko, n // tn, nk),
        in_specs=in_specs,
        out_specs=pl.BlockSpec((None, None, tko, tn), lambda ko, nn, mm: (layer,) + oidx(ko, nn)),
        scratch_shapes=[pltpu.VMEM((tko, tn), F32)] if nk > 1 else [],
        input_output_aliases={2: 0} if into is not None else {},
        compiler_params=_cparams("parallel", "parallel", "arbitrary"),
        name=name,
    )(*args)


def _mxu_sum(x, ones):
    hi = x.astype(BF16)
    lo = (x - hi.astype(F32)).astype(BF16)
    return jnp.dot(hi, ones, preferred_element_type=F32) + jnp.dot(lo, ones, preferred_element_type=F32)


def _head_rms(x, gain, sums):
    r = lax.rsqrt(_mxu_sum(x * x, sums[0]) * (1.0 / HEAD_DIM) + RMS_EPS)
    if len(sums) == 2:
        r = _mxu_sum(r, sums[1])
    return x * r * gain, r


def _head_rms_bwd(x, r, gain, dy, sums, fold):
    t = dy * gain
    mean = _mxu_sum(t * x, sums[0]) * (1.0 / HEAD_DIM)
    dx = r * t - x * (r * r * r) * (mean if len(sums) == 1 else _mxu_sum(mean, sums[1]))
    dg = jnp.broadcast_to(jnp.sum(dy * x * r, axis=0, keepdims=True), (8, x.shape[1]))
    return dx, _mxu_sum(dg, fold)[0:1, :HEAD_DIM]


def _attn_consts(n_q, n_kv, sinks, qg, kg):
    group = n_q // n_kv
    t = jnp.arange(WINDOW, dtype=jnp.int32)[:, None]
    s = jnp.arange(2 * WINDOW, dtype=jnp.int32)[None, :] - WINDOW
    dist = (t - s).astype(F32)
    valid = (dist >= 0) & (dist < WINDOW)
    slopes = jnp.exp2(-8.0 * jnp.arange(1, n_q + 1, dtype=F32) / n_q)
    bias = jnp.where(valid[None], -slopes[:, None, None] * dist[None], NEG_BIG)
    sink = jnp.broadcast_to(sinks.astype(F32).reshape(n_kv, group, 1, 1), (n_kv, group, WINDOW, 128))
    head_ones = lambda h: jnp.kron(jnp.eye(h, dtype=F32), jnp.ones((HEAD_DIM, HEAD_DIM), F32)).astype(BF16)
    fold = lambda h: jnp.tile(jnp.eye(HEAD_DIM, 128, dtype=F32), (h, 1)).astype(BF16)
    q_heads = jnp.kron(jnp.eye(n_q, 128, dtype=F32), jnp.ones((HEAD_DIM, 1), F32)).astype(BF16)
    return dict(
        bias=bias.reshape(n_kv, group * WINDOW, 2 * WINDOW),
        sink=sink.reshape(n_kv, group * WINDOW, 128),
        qg=jnp.tile(qg, (1, n_q)), kg=jnp.tile(kg, (1, n_kv)),
        q_heads=q_heads, q_spread=q_heads.T,
        k_ones=head_ones(n_kv),
        q_fold=fold(n_q), k_fold=fold(n_kv),
        key_ones=jnp.ones((2 * WINDOW, 128), BF16))


_ATTN_CONST_ORDER = ("qg", "kg", "sink", "bias", "q_heads", "q_spread", "k_ones", "q_fold", "k_fold", "key_ones")


def _attn_inputs(q_ref, kc_ref, kp_ref, vc_ref, vp_ref, c):
    q = q_ref[...]
    k2 = jnp.concatenate([kp_ref[...], kc_ref[...]], axis=0)
    v2 = jnp.concatenate([vp_ref[...], vc_ref[...]], axis=0)
    qn, rq = _head_rms(q, c["qg"][...], (c["q_heads"][...], c["q_spread"][...]))
    kn, rk = _head_rms(k2, c["kg"][...], (c["k_ones"][...],))
    return dict(q=q, rq=rq, qn=qn.astype(BF16), k2=k2, rk=rk, kn=kn.astype(BF16), v2=v2.astype(BF16))


def _attn_probs(x, c, first_mask, kv, group):
    sl = slice(kv * HEAD_DIM, (kv + 1) * HEAD_DIM)
    k2b, v2b = x["kn"][:, sl], x["v2"][:, sl]
    qs = jnp.concatenate([x["qn"][:, (kv * group + g) * HEAD_DIM:(kv * group + g + 1) * HEAD_DIM]
                          for g in range(group)], axis=0)
    s = lax.dot_general(qs, k2b, (((1,), (1,)), ((), ())), preferred_element_type=F32) * (HEAD_DIM ** -0.5)
    s = jnp.where(first_mask, NEG_BIG, s + c["bias"][kv])
    sink = c["sink"][kv]
    m = jnp.maximum(jnp.max(s, axis=-1, keepdims=True), sink)
    twice = lambda a: jnp.concatenate([a, a], axis=1)
    p = jnp.exp(s - twice(m))
    esink = jnp.exp(sink - m)
    inv = 1.0 / (_mxu_sum(p, c["key_ones"][...]) + esink)
    return dict(k2b=k2b, v2b=v2b, qs=qs, pn=p * twice(inv), psink=esink * inv, twice=twice)


def _attn_specs(n_q, n_kv):
    aw, kvw = n_q * HEAD_DIM, n_kv * HEAD_DIM
    group = n_q // n_kv
    kblk, vblk = aw // kvw, aw // kvw + 1

    def specs(nb):
        cur = lambda n: jnp.minimum(n, nb - 1)
        prev = lambda n: jnp.maximum(jnp.minimum(n, nb - 1) - 1, 0)
        return [
            pl.BlockSpec((WINDOW, aw), lambda n: (cur(n), 0)),
            pl.BlockSpec((WINDOW, kvw), lambda n: (cur(n), kblk)),
            pl.BlockSpec((WINDOW, kvw), lambda n: (prev(n), kblk)),
            pl.BlockSpec((WINDOW, kvw), lambda n: (cur(n), vblk)),
            pl.BlockSpec((WINDOW, kvw), lambda n: (prev(n), vblk)),
        ]
    whole = lambda shape: pl.BlockSpec(shape, lambda n: (0,) * len(shape))
    return specs, whole


def _attn_fwd(z, qg, kg, sinks, *, n_q, n_kv, name, deps=()):
    L = z.shape[0]
    nb = L // WINDOW
    aw = n_q * HEAD_DIM
    group = n_q // n_kv
    consts = _attn_consts(n_q, n_kv, sinks, qg, kg)
    specs, whole = _attn_specs(n_q, n_kv)
    nc = len(_ATTN_CONST_ORDER)

    def body(q_ref, kc_ref, kp_ref, vc_ref, vp_ref, *rest):
        c = dict(zip(_ATTN_CONST_ORDER, rest[:nc]))
        o_ref = rest[-1]
        n = pl.program_id(0)
        col = lax.broadcasted_iota(jnp.int32, (group * WINDOW, 2 * WINDOW), 1)
        first_mask = jnp.logical_and(n == 0, col < WINDOW)
        x = _attn_inputs(q_ref, kc_ref, kp_ref, vc_ref, vp_ref, c)
        for kv in range(n_kv):
            a = _attn_probs(x, c, first_mask, kv, group)
            o = jnp.dot(a["pn"].astype(BF16), a["v2b"], preferred_element_type=F32)
            for g in range(group):
                h = kv * group + g
                o_ref[:, h * HEAD_DIM:(h + 1) * HEAD_DIM] = o[g * WINDOW:(g + 1) * WINDOW].astype(o_ref.dtype)

    return pl.pallas_call(
        body,
        out_shape=jax.ShapeDtypeStruct((L, aw), BF16),
        grid=(nb,),
        in_specs=specs(nb) + [whole(consts[k].shape) for k in _ATTN_CONST_ORDER]
        + [pl.BlockSpec(memory_space=pl.ANY)] * len(deps),
        out_specs=pl.BlockSpec((WINDOW, aw), lambda n: (n, 0)),
        compiler_params=_cparams("parallel"),
        name=name,
    )(z, z, z, z, z, *[consts[k] for k in _ATTN_CONST_ORDER], *deps)


def _attn_bwd(z, do, qg, kg, sinks, *, n_q, n_kv, name):
    L = z.shape[0]
    nb = L // WINDOW
    aw, kvw = n_q * HEAD_DIM, n_kv * HEAD_DIM
    group = n_q // n_kv
    consts = _attn_consts(n_q, n_kv, sinks, qg, kg)
    specs, whole = _attn_specs(n_q, n_kv)
    scale = HEAD_DIM ** -0.5
    nc = len(_ATTN_CONST_ORDER)

    def body(q_ref, kc_ref, kp_ref, vc_ref, vp_ref, do_ref, *rest):
        c = dict(zip(_ATTN_CONST_ORDER, rest[:nc]))
        dq_ref, dkv_ref, dqg_ref, dkg_ref, dsink_ref, carry_ref, dqn_ref, dkn_ref, dv_ref = rest[nc:]
        n = pl.program_id(0)

        @pl.when(n == 0)
        def _():
            dqg_ref[...] = jnp.zeros_like(dqg_ref)
            dkg_ref[...] = jnp.zeros_like(dkg_ref)
            dsink_ref[...] = jnp.zeros_like(dsink_ref)
            carry_ref[...] = jnp.zeros_like(carry_ref)

        @pl.when(n < nb)
        def _():
            col = lax.broadcasted_iota(jnp.int32, (group * WINDOW, 2 * WINDOW), 1)
            first_mask = jnp.logical_and(n == 0, col < WINDOW)
            head_lane = lax.broadcasted_iota(jnp.int32, (1, n_q), 1)
            x = _attn_inputs(q_ref, kc_ref, kp_ref, vc_ref, vp_ref, c)
            dsink = jnp.zeros((1, n_q), F32)
            for kv in range(n_kv):
                a = _attn_probs(x, c, first_mask, kv, group)
                pn = a["pn"]
                dos = jnp.concatenate(
                    [do_ref[:, (kv * group + g) * HEAD_DIM:(kv * group + g + 1) * HEAD_DIM] for g in range(group)],
                    axis=0).astype(BF16)
                dpn = lax.dot_general(dos, a["v2b"], (((1,), (1,)), ((), ())), preferred_element_type=F32)
                ksl = slice(kv * HEAD_DIM, (kv + 1) * HEAD_DIM)
                dv_ref[:, ksl] = lax.dot_general(pn.astype(BF16), dos, (((0,), (0,)), ((), ())), preferred_element_type=F32)
                delta = _mxu_sum(pn * dpn, c["key_ones"][...])
                ds = (pn * (dpn - a["twice"](delta))).astype(BF16)
                dsk = -a["psink"] * delta
                dqn = lax.dot_general(ds, a["k2b"], (((1,), (0,)), ((), ())), preferred_element_type=F32) * scale
                dkn_ref[:, ksl] = lax.dot_general(ds, a["qs"], (((0,), (0,)), ((), ())), preferred_element_type=F32) * scale
                for g in range(group):
                    h = kv * group + g
                    rows = slice(g * WINDOW, (g + 1) * WINDOW)
                    dqn_ref[:, h * HEAD_DIM:(h + 1) * HEAD_DIM] = dqn[rows]
                    dsink = dsink + jnp.where(head_lane == h, jnp.sum(dsk[rows], axis=0, keepdims=True)[:, :n_q], 0.0)
            dq, dqg = _head_rms_bwd(x["q"], x["rq"], c["qg"][...], dqn_ref[...],
                                    (c["q_heads"][...], c["q_spread"][...]), c["q_fold"][...])
            dk2, dkg = _head_rms_bwd(x["k2"], x["rk"], c["kg"][...], dkn_ref[...], (c["k_ones"][...],), c["k_fold"][...])
            dq_ref[...] = dq.astype(dq_ref.dtype)
            dkv_ref[:, :kvw] = (carry_ref[:, :kvw] + dk2[:WINDOW]).astype(dkv_ref.dtype)
            dkv_ref[:, kvw:] = (carry_ref[:, kvw:] + dv_ref[:WINDOW, :]).astype(dkv_ref.dtype)
            carry_ref[:, :kvw] = dk2[WINDOW:]
            carry_ref[:, kvw:] = dv_ref[WINDOW:, :]
            dqg_ref[...] += dqg
            dkg_ref[...] += dkg
            dsink_ref[...] += dsink

        @pl.when(n == nb)
        def _():
            dkv_ref[...] = carry_ref[...].astype(dkv_ref.dtype)

    in_specs = (specs(nb) + [pl.BlockSpec((WINDOW, aw), lambda n: (jnp.minimum(n, nb - 1), 0))]
                + [whole(consts[k].shape) for k in _ATTN_CONST_ORDER])
    return pl.pallas_call(
        body,
        out_shape=(jax.ShapeDtypeStruct((L, aw), BF16), jax.ShapeDtypeStruct((L, 2 * kvw), BF16),
                   jax.ShapeDtypeStruct((1, HEAD_DIM), F32), jax.ShapeDtypeStruct((1, HEAD_DIM), F32),
                   jax.ShapeDtypeStruct((1, n_q), F32)),
        grid=(nb + 1,),
        in_specs=in_specs,
        out_specs=(pl.BlockSpec((WINDOW, aw), lambda n: (jnp.minimum(n, nb - 1), 0)),
                   pl.BlockSpec((WINDOW, 2 * kvw), lambda n: (jnp.maximum(n - 1, 0), 0)),
                   pl.BlockSpec((1, HEAD_DIM), lambda n: (0, 0)),
                   pl.BlockSpec((1, HEAD_DIM), lambda n: (0, 0)),
                   pl.BlockSpec((1, n_q), lambda n: (0, 0))),
        scratch_shapes=[pltpu.VMEM((WINDOW, 2 * kvw), F32), pltpu.VMEM((WINDOW, aw), F32),
                        pltpu.VMEM((2 * WINDOW, kvw), F32), pltpu.VMEM((2 * WINDOW, kvw), F32)],
        compiler_params=_cparams("arbitrary"),
        name=name,
    )(z, z, z, z, z, do, *[consts[k] for k in _ATTN_CONST_ORDER])


def _cmul(ar, ai, br, bi):
    return ar * br - ai * bi, ar * bi + ai * br


def _time_permutation(tc):
    r = jnp.arange(tc)
    src = (r % 8) * (tc // 8) + r // 8
    p = (src[:, None] == jnp.arange(tc)[None, :]).astype(BF16)
    return p, p.T


def _unpermute(pt, x):
    hi = x.astype(BF16)
    lo = (x - hi.astype(F32)).astype(BF16)
    moved = jnp.dot(pt, jnp.concatenate([hi, lo], axis=1), preferred_element_type=F32)
    return moved[:, :x.shape[1]] + moved[:, x.shape[1]:]


def _segment_scan(xr_ref, xi_ref, ar, ai, cr, ci, ng, reverse):
    n = ar.shape[-1]
    row = lax.broadcasted_iota(jnp.int32, (8, n), 0)
    seeded = 7 if reverse else 0
    a8r = jnp.broadcast_to(ar, (8, n))
    a8i = jnp.broadcast_to(ai, (8, n))
    rows_of = lambda g: pl.ds(pl.multiple_of(((ng - 1 - g) if reverse else g) * 8, 8), 8)

    def recur(g, s):
        rows = rows_of(g)
        sr = a8r * s[0] - a8i * s[1] + xr_ref[rows, :]
        si = a8r * s[1] + a8i * s[0] + xi_ref[rows, :]
        xr_ref[rows, :] = sr
        xi_ref[rows, :] = si
        return sr, si

    fr, fi = lax.fori_loop(0, ng, recur, (jnp.where(row == seeded, cr, 0.0), jnp.where(row == seeded, ci, 0.0)))
    pr, pi = ar, ai
    for _ in range(ng.bit_length() - 1):
        pr, pi = _cmul(pr, pi, pr, pi)
    for k in (1, 2, 4):
        keep = (row < 8 - k) if reverse else (row >= k)
        shift = (8 - k) if reverse else k
        mr, mi = jnp.where(keep, pr, 0.0), jnp.where(keep, pi, 0.0)
        tr, ti = pltpu.roll(fr, shift, 0), pltpu.roll(fi, shift, 0)
        fr, fi = fr + mr * tr - mi * ti, fi + mr * ti + mi * tr
        pr, pi = _cmul(pr, pi, pr, pi)
    shift = 7 if reverse else 1
    before_r, before_i = pltpu.roll(fr, shift, 0), pltpu.roll(fi, shift, 0)

    def inherit(g, d):
        rows = rows_of(g)
        dr = a8r * d[0] - a8i * d[1]
        di = a8r * d[1] + a8i * d[0]
        xr_ref[rows, :] = xr_ref[rows, :] + dr
        xi_ref[rows, :] = xi_ref[rows, :] + di
        return dr, di

    lax.fori_loop(0, ng, inherit, (jnp.where(row == seeded, 0.0, before_r), jnp.where(row == seeded, 0.0, before_i)))
    out = 0 if reverse else 7
    return (fr[out:out + 1], fi[out:out + 1],
            jnp.where(row == seeded, cr, before_r), jnp.where(row == seeded, ci, before_i))


def _blockdiag(x):
    g, a, b = x.shape
    j = g // SSM_LANE_GROUPS
    eye = jnp.eye(SSM_LANE_GROUPS, dtype=x.dtype)
    y = x.reshape(j, SSM_LANE_GROUPS, a, 1, b) * eye[None, :, None, :, None]
    return y.reshape(j, SSM_LANE_GROUPS * a, SSM_LANE_GROUPS * b)


def _blockdiag_extract(y, a, b):
    j = y.shape[0]
    y = y.reshape(j, SSM_LANE_GROUPS, a, SSM_LANE_GROUPS, b)
    return jnp.einsum("jgahb,gh->jgab", y, jnp.eye(SSM_LANE_GROUPS, dtype=y.dtype)).reshape(j * SSM_LANE_GROUPS, a, b)


def _ssm_disc(lr, li, ldt, brt, bit):
    dt = jnp.exp(ldt)
    mag = jnp.exp(lr * dt)
    ar = mag * jnp.cos(li * dt)
    ai = mag * jnp.sin(li * dt)
    den = lr * lr + li * li
    fr = ((ar - 1.0) * lr + ai * li) / den
    fi = (ai * lr - (ar - 1.0) * li) / den
    bbr = fr[:, None, :] * brt - fi[:, None, :] * bit
    bbi = fr[:, None, :] * bit + fi[:, None, :] * brt
    return ar, ai, bbr, bbi


def _ssm_prep(lr, li, ldt, brt, bit, *, name):
    g, h, p = brt.shape

    def body(lr_ref, li_ref, ldt_ref, brt_ref, bit_ref, ar_ref, ai_ref, bbr_ref, bbi_ref):
        ar, ai, bbr, bbi = _ssm_disc(lr_ref[...], li_ref[...], ldt_ref[...], brt_ref[...], bit_ref[...])
        ar_ref[...] = ar
        ai_ref[...] = ai
        bbr_ref[...] = bbr
        bbi_ref[...] = bbi

    gp = jax.ShapeDtypeStruct((g, p), F32)
    ghp = jax.ShapeDtypeStruct((g, h, p), F32)
    return pl.pallas_call(body, out_shape=(gp, gp, ghp, ghp), name=name)(lr, li, ldt, brt, bit)


def _ssm_prep_bwd(lr, li, ldt, brt, bit, dar, dai, dbbr, dbbi, *, name):
    g, h, p = brt.shape

    def body(lr_ref, li_ref, ldt_ref, brt_ref, bit_ref, dar_ref, dai_ref, dbbr_ref, dbbi_ref,
             dlr_ref, dli_ref, dldt_ref, dbrt_ref, dbit_ref):
        _, vjp = jax.vjp(_ssm_disc, lr_ref[...], li_ref[...], ldt_ref[...], brt_ref[...], bit_ref[...])
        dlr, dli, dldt, dbrt, dbit = vjp((dar_ref[...], dai_ref[...], dbbr_ref[...], dbbi_ref[...]))
        dlr_ref[...] = dlr
        dli_ref[...] = dli
        dldt_ref[...] = dldt
        dbrt_ref[...] = dbrt
        dbit_ref[...] = dbit

    gp = jax.ShapeDtypeStruct((g, p), F32)
    ghp = jax.ShapeDtypeStruct((g, h, p), F32)
    return pl.pallas_call(body, out_shape=(gp, gp, jax.ShapeDtypeStruct((g, 1), F32), ghp, ghp), name=name)(
        lr, li, ldt, brt, bit, dar, dai, dbbr, dbbi)


def _ssm_specs(tc, nlanes, nch, u_colblk, chunk_of):
    return [
        pl.BlockSpec((tc, nch), lambda j, c: (chunk_of(c), u_colblk + j)),
        pl.BlockSpec((1, nlanes), lambda j, c: (0, j)),
        pl.BlockSpec((1, nlanes), lambda j, c: (0, j)),
        pl.BlockSpec((None, nch, nlanes), lambda j, c: (j, 0, 0)),
        pl.BlockSpec((None, nch, nlanes), lambda j, c: (j, 0, 0)),
        pl.BlockSpec((None, nlanes, nch), lambda j, c: (j, 0, 0)),
        pl.BlockSpec((None, nlanes, nch), lambda j, c: (j, 0, 0)),
        pl.BlockSpec((1, nch), lambda j, c: (0, j)),
        pl.BlockSpec((tc, tc), lambda j, c: (0, 0)),
        pl.BlockSpec((tc, tc), lambda j, c: (0, 0)),
    ]


def _ssm_fwd(z, ar, ai, bblk_r, bblk_i, cblk_r, cblk_i, d, *, u_col, tc, name, deps=()):
    L = z.shape[0]
    nj, nch, nlanes = bblk_r.shape
    w = nj * nch
    nc = L // tc
    ng = tc // 8

    assert ng & (ng - 1) == 0
    perm, perm_t = _time_permutation(tc)

    def body(u_ref, ar_ref, ai_ref, br_ref, bi_ref, cr_ref, ci_ref, d_ref, p_ref, pt_ref, *rest):
        y_ref, s0r_ref, s0i_ref, xr_ref, xi_ref, carr_ref, cari_ref = rest[len(deps):]
        c = pl.program_id(1)

        @pl.when(c == 0)
        def _():
            carr_ref[...] = jnp.zeros_like(carr_ref)
            cari_ref[...] = jnp.zeros_like(cari_ref)

        s0r_ref[...] = carr_ref[...]
        s0i_ref[...] = cari_ref[...]
        u = u_ref[...]
        ub = jnp.dot(p_ref[...], u.astype(BF16), preferred_element_type=F32).astype(BF16)
        xr_ref[...] = jnp.dot(ub, br_ref[...].astype(BF16), preferred_element_type=F32)
        xi_ref[...] = jnp.dot(ub, bi_ref[...].astype(BF16), preferred_element_type=F32)
        cr, ci, _, _ = _segment_scan(xr_ref, xi_ref, ar_ref[...], ai_ref[...], carr_ref[...], cari_ref[...], ng, False)
        carr_ref[...] = cr
        cari_ref[...] = ci
        y = (jnp.dot(xr_ref[...].astype(BF16), cr_ref[...].astype(BF16), preferred_element_type=F32)
             - jnp.dot(xi_ref[...].astype(BF16), ci_ref[...].astype(BF16), preferred_element_type=F32))
        y_ref[...] = _unpermute(pt_ref[...], y) + d_ref[...] * u

    state = jax.ShapeDtypeStruct((nc, 1, nj * nlanes), F32)
    state_spec = pl.BlockSpec((None, 1, nlanes), lambda j, c: (c, 0, j))
    return pl.pallas_call(
        body,
        out_shape=(jax.ShapeDtypeStruct((L, w), F32), state, state),
        grid=(nj, nc),
        in_specs=_ssm_specs(tc, nlanes, nch, u_col // nch, lambda c: c) + [pl.BlockSpec(memory_space=pl.ANY)] * len(deps),
        out_specs=(pl.BlockSpec((tc, nch), lambda j, c: (c, j)), state_spec, state_spec),
        scratch_shapes=[pltpu.VMEM((tc, nlanes), F32), pltpu.VMEM((tc, nlanes), F32),
                        pltpu.VMEM((1, nlanes), F32), pltpu.VMEM((1, nlanes), F32)],
        compiler_params=_cparams("parallel", "arbitrary"),
        name=name,
    )(z, ar, ai, bblk_r, bblk_i, cblk_r, cblk_i, d, perm, perm_t, *deps)


def _ssm_bwd(z, dy, s0r, s0i, ar, ai, bblk_r, bblk_i, cblk_r, cblk_i, d, *, u_col, tc, name):
    L = z.shape[0]
    nj, nch, nlanes = bblk_r.shape
    w = nj * nch
    nc = L // tc
    ng = tc // 8
    chunk_of = lambda c: nc - 1 - c
    assert ng & (ng - 1) == 0
    perm, perm_t = _time_permutation(tc)

    def body(u_ref, ar_ref, ai_ref, br_ref, bi_ref, cr_ref, ci_ref, d_ref, p_ref, pt_ref, dy_ref, s0r_ref, s0i_ref,
             du_ref, dbr_ref, dbi_ref, dcr_ref, dci_ref, dar_ref, dai_ref, dd_ref,
             sr_ref, si_ref, lr_ref, li_ref, carr_ref, cari_ref):
        c = pl.program_id(1)

        @pl.when(c == 0)
        def _():
            for ref in (dbr_ref, dbi_ref, dcr_ref, dci_ref, dar_ref, dai_ref, dd_ref, carr_ref, cari_ref):
                ref[...] = jnp.zeros_like(ref)

        u = u_ref[...]
        dyv = dy_ref[...]
        both = jnp.dot(p_ref[...], jnp.concatenate([u.astype(BF16), dyv.astype(BF16)], axis=1), preferred_element_type=F32)
        ub = both[:, :nch].astype(BF16)
        dyb = both[:, nch:].astype(BF16)
        brb = br_ref[...].astype(BF16)
        bib = bi_ref[...].astype(BF16)
        crb = cr_ref[...].astype(BF16)
        cib = ci_ref[...].astype(BF16)
        a_r, a_i = ar_ref[...], ai_ref[...]

        sr_ref[...] = jnp.dot(ub, brb, preferred_element_type=F32)
        si_ref[...] = jnp.dot(ub, bib, preferred_element_type=F32)
        _, _, start_r, start_i = _segment_scan(sr_ref, si_ref, a_r, a_i, s0r_ref[...], s0i_ref[...], ng, False)

        nt = (((1,), (1,)), ((), ()))
        lr_ref[...] = lax.dot_general(dyb, crb, nt, preferred_element_type=F32)
        li_ref[...] = -lax.dot_general(dyb, cib, nt, preferred_element_type=F32)
        cr, ci, _, _ = _segment_scan(lr_ref, li_ref, a_r, -a_i, carr_ref[...], cari_ref[...], ng, True)
        carr_ref[...] = cr
        cari_ref[...] = ci

        def accumulate(g, carry):
            pr, pi, acc_r, acc_i = carry
            rows = pl.ds(pl.multiple_of(g * 8, 8), 8)
            lr, li = lr_ref[rows, :], li_ref[rows, :]
            return sr_ref[rows, :], si_ref[rows, :], acc_r + lr * pr + li * pi, acc_i + li * pr - lr * pi

        zero8 = jnp.zeros((8, nlanes), F32)
        _, _, acc_r, acc_i = lax.fori_loop(0, ng, accumulate, (start_r, start_i, zero8, zero8))
        dar_ref[...] += jnp.sum(acc_r, axis=0, keepdims=True)
        dai_ref[...] += jnp.sum(acc_i, axis=0, keepdims=True)

        tn = (((0,), (0,)), ((), ()))
        lrb = lr_ref[...].astype(BF16)
        lib = li_ref[...].astype(BF16)
        dcr_ref[...] += lax.dot_general(dyb, sr_ref[...].astype(BF16), tn, preferred_element_type=F32)
        dci_ref[...] -= lax.dot_general(dyb, si_ref[...].astype(BF16), tn, preferred_element_type=F32)
        dbr_ref[...] += lax.dot_general(ub, lrb, tn, preferred_element_type=F32)
        dbi_ref[...] += lax.dot_general(ub, lib, tn, preferred_element_type=F32)
        du = (lax.dot_general(lrb, brb, nt, preferred_element_type=F32)
              + lax.dot_general(lib, bib, nt, preferred_element_type=F32))
        du_ref[...] = (_unpermute(pt_ref[...], du) + d_ref[...] * dyv).astype(du_ref.dtype)
        dd_ref[...] += jnp.sum(dyv * u, axis=0, keepdims=True)

    state_spec = pl.BlockSpec((None, 1, nlanes), lambda j, c: (chunk_of(c), 0, j))
    bshape = jax.ShapeDtypeStruct((nj, nch, nlanes), F32)
    ashape = jax.ShapeDtypeStruct((1, nj * nlanes), F32)
    bspec = pl.BlockSpec((None, nch, nlanes), lambda j, c: (j, 0, 0))
    aspec = pl.BlockSpec((1, nlanes), lambda j, c: (0, j))
    big = pltpu.VMEM((tc, nlanes), F32)
    return pl.pallas_call(
        body,
        out_shape=(jax.ShapeDtypeStruct((L, w), BF16), bshape, bshape, bshape, bshape, ashape, ashape,
                   jax.ShapeDtypeStruct((1, w), F32)),
        grid=(nj, nc),
        in_specs=_ssm_specs(tc, nlanes, nch, u_col // nch, chunk_of)
        + [pl.BlockSpec((tc, nch), lambda j, c: (chunk_of(c), j)), state_spec, state_spec],
        out_specs=(pl.BlockSpec((tc, nch), lambda j, c: (chunk_of(c), j)), bspec, bspec, bspec, bspec, aspec, aspec,
                   pl.BlockSpec((1, nch), lambda j, c: (0, j))),
        scratch_shapes=[big, big, big, big, pltpu.VMEM((1, nlanes), F32), pltpu.VMEM((1, nlanes), F32)],
        compiler_params=_cparams("parallel", "arbitrary"),
        name=name,
    )(z, ar, ai, bblk_r, bblk_i, cblk_r, cblk_i, d, perm, perm_t, dy, s0r, s0i)


def _rmsnorm_rows(x, g):
    return x * lax.rsqrt(jnp.mean(x * x, axis=-1, keepdims=True) + RMS_EPS) * g


def _glu_out(y_raw, pre, b):
    yg = jax.nn.gelu(y_raw)
    return yg * jax.nn.sigmoid(pre + b)


def _gate_merge(za, zs, ba, bs, a, bm):
    return jax.nn.sigmoid(za + ba) * a + jax.nn.sigmoid(zs + bs) * bm


def _swiglu(g, u):
    return jax.nn.silu(g) * u


def _row_tile(width_bytes_per_row, rows):
    budget = VMEM_LIMIT_BYTES // 3
    t = max(8, min(1024, budget // (2 * max(width_bytes_per_row, 1))))
    return _pick(rows, t, 16)


def _ssm_params(p, prefix):
    g, pst = p["lam_re"].shape
    ar, ai, bbr, bbi = _ssm_prep(p["lam_re"], p["lam_im"], p["log_dt"], p["b_re_t"], p["b_im_t"], name=prefix + "_ssm_prep")
    return dict(ar=ar.reshape(1, g * pst), ai=ai.reshape(1, g * pst),
                bblk_r=_blockdiag(bbr), bblk_i=_blockdiag(bbi),
                cblk_r=_blockdiag(jnp.swapaxes(p["c_re"], 1, 2)), cblk_i=_blockdiag(jnp.swapaxes(p["c_im"], 1, 2)))


def _layer_fwd(x, p, dims, prefix, deps=(), tick=None):
    tick = tick or (lambda point, arr: ())
    t, d = x.shape
    aw, kvw, sw, ff = dims["aw"], dims["kvw"], dims["sw"], dims["ff"]
    off_u = aw + 2 * kvw
    off_g = off_u + sw
    gblk = _pick(d, 512, 128)
    assert off_g % gblk == 0 and off_u % (SSM_LANE_GROUPS * SSM_GROUP_CH) == 0
    sv = {"x": x}

    h, = _rowmap(_rmsnorm_rows, [(x, "row", d, 0), (p["norm_mix_g"], "vec", d, 0)], [(d, BF16, "row", d)],
                 rows=t, tm=_row_tile(6 * d, t), name=prefix + "_norm_mix", deps=deps)
    deps = tick("norm", h)
    z = _mm_nn(h, p["w_in"], out_dtype=F32, tm=1024, tn=1664, tk=2048, name=prefix + "_mm_in", deps=deps)
    ya = _attn_fwd(z, p["q_norm_g"], p["k_norm_g"], p["attn_sinks"], n_q=dims["n_q"], n_kv=dims["n_kv"],
                   name=prefix + "_attn_fwd", deps=tick("in", z))
    sp = _ssm_params(p, prefix)
    y_raw, s0r, s0i = _ssm_fwd(z, sp["ar"], sp["ai"], sp["bblk_r"], sp["bblk_i"], sp["cblk_r"], sp["cblk_i"], p["ssm_d"],
                               u_col=off_u, tc=dims["tc"], name=prefix + "_ssm_fwd", deps=tick("attn", ya))
    yg, = _rowmap(jax.nn.gelu, [(y_raw, "row", sw, 0)], [(sw, BF16, "row", sw)],
                  rows=t, tm=_row_tile(6 * sw, t), name=prefix + "_gelu", deps=tick("ssm", y_raw))
    pre = _mm_nn(yg, p["ssm_glu_w"], out_dtype=F32, tm=1024, tn=1024, tk=1024, name=prefix + "_mm_glu")
    y2, = _rowmap(_glu_out, [(y_raw, "row", sw, 0), (pre, "row", sw, 0), (p["ssm_glu_b"], "vec", sw, 0)],
                  [(sw, BF16, "row", sw)], rows=t, tm=_row_tile(10 * sw, t), name=prefix + "_glu_out")
    a = _mm_nn(ya, p["w_attn_branch"], out_dtype=BF16, tm=1024, tn=512, tk=1024, name=prefix + "_mm_ab")
    bm = _mm_nn(y2, p["w_ssm_branch"], out_dtype=BF16, tm=1024, tn=512, tk=1024, name=prefix + "_mm_sb")
    ngb = d // gblk
    merged, = _rowmap(
        _gate_merge,
        [(z, "row", gblk, off_g // gblk), (z, "row", gblk, off_g // gblk + ngb),
         (p["gate_bias"], "vec", gblk, 0), (p["gate_bias"], "vec", gblk, ngb),
         (a, "row", gblk, 0), (bm, "row", gblk, 0)],
        [(d, BF16, "row", gblk)], rows=t, tm=_row_tile(18 * gblk, t), ncol=ngb, name=prefix + "_gate")
    x1 = _mm_nn(merged, p["w_out"], out_dtype=F32, tm=512, tn=1024, tk=2048, name=prefix + "_mm_out", add=x)
    h2, = _rowmap(_rmsnorm_rows, [(x1, "row", d, 0), (p["norm_ffn_g"], "vec", d, 0)], [(d, BF16, "row", d)],
                  rows=t, tm=_row_tile(6 * d, t), name=prefix + "_norm_ffn", deps=tick("out", x1))
    gu = _mm_nn(h2, p["w_ffn_in"], out_dtype=BF16, tm=1024, tn=1408, tk=2048, name=prefix + "_mm_ffn_in")
    fblk = _pick(ff, 1408, 128)
    nfb = ff // fblk
    act, = _rowmap(_swiglu, [(gu, "row", fblk, 0), (gu, "row", fblk, nfb)], [(ff, BF16, "row", fblk)],
                   rows=t, tm=_row_tile(10 * fblk, t), ncol=nfb, name=prefix + "_swiglu", deps=tick("ffn_in", gu))
    x2 = _mm_nn(act, p["w_ffn_out"], out_dtype=F32, tm=512, tn=512, tk=5632, name=prefix + "_mm_ffn_out", add=x1,
                deps=tick("act", act))
    sv.update(h=h, z=z, ya=ya, sp=sp, y_raw=y_raw, s0r=s0r, s0i=s0i, yg=yg, pre=pre, y2=y2, a=a, bm=bm,
              merged=merged, x1=x1, h2=h2, gu=gu, act=act)
    return x2, sv


def _layer_bwd(dx2, dx2b, sv, p, dims, prefix, gbuf, deps=(), before_mixer=None, before_in=None):
    t, d = dx2.shape
    aw, kvw, sw, ff = dims["aw"], dims["kvw"], dims["sw"], dims["ff"]
    off_u = aw + 2 * kvw
    off_g = off_u + sw
    gblk = _pick(d, 512, 128)
    ngb = d // gblk
    fblk = _pick(ff, 1408, 128)
    nfb = ff // fblk
    g = {}

    dact = _mm_nt(dx2b, p["w_ffn_out"], out_dtype=BF16, tm=1024, tn=2048, tko=1408, name=prefix + "_mm_dact", deps=deps)
    g["w_ffn_out"] = _mm_tn(sv["act"], dx2b, p["w_ffn_out"], into=gbuf.get("w_ffn_out"), tm=4096, tn=1024, tko=512,
                            name=prefix + "_mm_dw_ffn_out")

    def swiglu_bwd(gg, uu, da):
        s = jax.nn.sigmoid(gg)
        gs = gg * s
        return jnp.concatenate([da * uu * (s + gs * (1.0 - s)), da * gs], axis=1)

    dgu, = _rowmap(swiglu_bwd, [(sv["gu"], "row", fblk, 0), (sv["gu"], "row", fblk, nfb), (dact, "row", fblk, 0)],
                   [(2 * ff, BF16, "row", 2 * fblk)],
                   rows=t, tm=_row_tile(16 * fblk, t), ncol=nfb, name=prefix + "_swiglu_bwd")
    dh2 = _mm_nt(dgu, p["w_ffn_in"], out_dtype=F32, tm=512, tn=1408, tko=512, name=prefix + "_mm_dh2", interleaved=fblk)
    g["w_ffn_in"] = _mm_tn(sv["h2"], dgu, p["w_ffn_in"], into=gbuf.get("w_ffn_in"), tm=4096, tn=fblk, tko=512,
                           name=prefix + "_mm_dw_ffn_in", interleaved=True)

    def norm_bwd(xx, gg, dh, dres):
        _, vjp = jax.vjp(_rmsnorm_rows, xx, gg)
        dxx, dgg = vjp(dh)
        dxx = dxx + dres
        return dxx, dxx, dgg

    dx1, dx1b, g["norm_ffn_g"] = _rowmap(
        norm_bwd, [(sv["x1"], "row", d, 0), (p["norm_ffn_g"], "vec", d, 0), (dh2, "row", d, 0), (dx2, "row", d, 0)],
        [(d, F32, "row", d), (d, BF16, "row", d), (d, F32, "acc", d)],
        rows=t, tm=_row_tile(22 * d, t), name=prefix + "_norm_ffn_bwd")

    deps = before_mixer(dx1, g) if before_mixer else ()
    dmerged = _mm_nt(dx1b, p["w_out"], out_dtype=BF16, tm=1024, tn=2048, tko=1024, name=prefix + "_mm_dmerged", deps=deps)
    g["w_out"] = _mm_tn(sv["merged"], dx1b, p["w_out"], into=gbuf.get("w_out"), tm=4096, tn=1024, tko=512,
                        name=prefix + "_mm_dw_out")

    def gate_bwd(za, zs, ba, bs, aa, bb, dm):
        sa = jax.nn.sigmoid(za + ba)
        ss = jax.nn.sigmoid(zs + bs)
        daa, dbb = dm * sa, dm * ss
        dza, dzs = daa * aa * (1.0 - sa), dbb * bb * (1.0 - ss)
        return daa, dbb, dza, dzs, jnp.sum(dza, axis=0, keepdims=True), jnp.sum(dzs, axis=0, keepdims=True)

    z = sv["z"]
    da, dbm, dza, dzs, dba, dbs = _rowmap(
        gate_bwd,
        [(z, "row", gblk, off_g // gblk), (z, "row", gblk, off_g // gblk + ngb),
         (p["gate_bias"], "vec", gblk, 0), (p["gate_bias"], "vec", gblk, ngb),
         (sv["a"], "row", gblk, 0), (sv["bm"], "row", gblk, 0), (dmerged, "row", gblk, 0)],
        [(d, BF16, "row", gblk), (d, BF16, "row", gblk), (d, BF16, "row", gblk), (d, BF16, "row", gblk),
         (d, F32, "acc", gblk), (d, F32, "acc", gblk)],
        rows=t, tm=_row_tile(32 * gblk, t), ncol=ngb, name=prefix + "_gate_bwd")
    g["gate_bias"] = jnp.concatenate([dba, dbs], axis=1)
    dya = _mm_nt(da, p["w_attn_branch"], out_dtype=BF16, tm=1024, tn=512, tko=1024, name=prefix + "_mm_dya")
    g["w_attn_branch"] = _mm_tn(sv["ya"], da, p["w_attn_branch"], into=gbuf.get("w_attn_branch"), tm=4096, tn=512,
                                tko=512, name=prefix + "_mm_dw_ab")
    dy2 = _mm_nt(dbm, p["w_ssm_branch"], out_dtype=BF16, tm=1024, tn=512, tko=1024, name=prefix + "_mm_dy2")
    g["w_ssm_branch"] = _mm_tn(sv["y2"], dbm, p["w_ssm_branch"], into=gbuf.get("w_ssm_branch"), tm=4096, tn=512,
                               tko=512, name=prefix + "_mm_dw_sb")

    def glu_bwd(y_raw, pre, b, dy):
        yg = jax.nn.gelu(y_raw)
        _, vjp = jax.vjp(lambda a_, b_, c_: a_ * jax.nn.sigmoid(b_ + c_), yg, pre, b)
        dyg, dpre, db = vjp(dy)
        return dyg, dpre, db

    dyg_direct, dpre, g["ssm_glu_b"] = _rowmap(
        glu_bwd, [(sv["y_raw"], "row", sw, 0), (sv["pre"], "row", sw, 0), (p["ssm_glu_b"], "vec", sw, 0), (dy2, "row", sw, 0)],
        [(sw, F32, "row", sw), (sw, BF16, "row", sw), (sw, F32, "acc", sw)],
        rows=t, tm=_row_tile(24 * sw, t), name=prefix + "_glu_bwd")
    dyg2 = _mm_nt(dpre, p["ssm_glu_w"], out_dtype=F32, tm=1024, tn=1024, tko=1024, name=prefix + "_mm_dyg")
    g["ssm_glu_w"] = _mm_tn(sv["yg"], dpre, p["ssm_glu_w"], into=gbuf.get("ssm_glu_w"), tm=4096, tn=1024, tko=512,
                            name=prefix + "_mm_dw_glu")

    def gelu_bwd(y_raw, d1, d2):
        _, vjp = jax.vjp(jax.nn.gelu, y_raw)
        return vjp(d1 + d2)[0]

    dy_raw, = _rowmap(gelu_bwd, [(sv["y_raw"], "row", sw, 0), (dyg_direct, "row", sw, 0), (dyg2, "row", sw, 0)],
                      [(sw, F32, "row", sw)], rows=t, tm=_row_tile(20 * sw, t), name=prefix + "_gelu_bwd")
    sp = sv["sp"]
    du, dbr, dbi, dcr, dci, dar, dai, g["ssm_d"] = _ssm_bwd(
        z, dy_raw, sv["s0r"], sv["s0i"], sp["ar"], sp["ai"], sp["bblk_r"], sp["bblk_i"], sp["cblk_r"], sp["cblk_i"],
        p["ssm_d"], u_col=off_u, tc=dims["tc"], name=prefix + "_ssm_bwd")
    ngr, pst = p["lam_re"].shape
    hch = SSM_GROUP_CH
    dlr, dli, dldt, dbrt, dbit = _ssm_prep_bwd(
        p["lam_re"], p["lam_im"], p["log_dt"], p["b_re_t"], p["b_im_t"],
        dar.reshape(ngr, pst), dai.reshape(ngr, pst), _blockdiag_extract(dbr, hch, pst), _blockdiag_extract(dbi, hch, pst),
        name=prefix + "_ssm_prep_bwd")
    g.update(ssm_lambda_re=dlr, ssm_lambda_im=dli, ssm_log_dt=dldt.reshape(ngr),
             ssm_b_re=jnp.swapaxes(dbrt, 1, 2), ssm_b_im=jnp.swapaxes(dbit, 1, 2),
             ssm_c_re=_blockdiag_extract(dcr, hch, pst), ssm_c_im=_blockdiag_extract(dci, hch, pst))

    dq, dkv, g["q_norm_g"], g["k_norm_g"], g["attn_sinks"] = _attn_bwd(
        z, dya, p["q_norm_g"], p["k_norm_g"], p["attn_sinks"], n_q=dims["n_q"], n_kv=dims["n_kv"], name=prefix + "_attn_bwd")

    dz = jnp.concatenate([dq, dkv, du, dza, dzs], axis=1)
    deps = before_in(dq, g) if before_in else ()
    dh = _mm_nt(dz, p["w_in"], out_dtype=F32, tm=1024, tn=1664, tko=512, name=prefix + "_mm_dh", deps=deps)
    g["w_in"] = _mm_tn(sv["h"], dz, p["w_in"], into=gbuf.get("w_in"), tm=4096, tn=1664, tko=512,
                       name=prefix + "_mm_dw_in")
    dx, dxb, g["norm_mix_g"] = _rowmap(
        norm_bwd, [(sv["x"], "row", d, 0), (p["norm_mix_g"], "vec", d, 0), (dh, "row", d, 0), (dx1, "row", d, 0)],
        [(d, F32, "row", d), (d, BF16, "row", d), (d, F32, "acc", d)],
        rows=t, tm=_row_tile(22 * d, t), name=prefix + "_norm_mix_bwd")
    return dx, dxb, g


def _loss_and_grad(y, target):
    t, d = y.shape

    def fn(yy, tt):
        e = yy - tt
        dy = e * (1.0 / d)
        return dy, dy, jnp.sum(e * e, keepdims=True).reshape(1, 1)

    dy, dyb, sq = _rowmap(fn, [(y, "row", d, 0), (target, "row", d, 0)],
                          [(d, F32, "row", d), (d, BF16, "row", d), (1, F32, "acc", 1)],
                          rows=t, tm=_row_tile(14 * d, t), name="loss")
    return sq, dy, dyb


def _local_step(x, target, n_layers, weights_of, dims, fwd_tick=None, bwd_tick=None):
    saved, params = [], []
    h = x
    for l in range(n_layers):
        p, deps = weights_of(l, h)
        params.append(p)
        tick = (lambda point, arr, l=l: fwd_tick(l, point, arr)) if fwd_tick else None
        h, sv = _layer_fwd(h, p, dims, "l%d" % l, deps, tick)
        saved.append(sv)
    sq, dy, dyb = _loss_and_grad(h, target)
    grads = [None] * n_layers
    deps = ()
    for l in reversed(range(n_layers)):
        if bwd_tick:
            ffn_done = lambda arr, g, l=l: bwd_tick(l, "ffn", (arr, g["w_ffn_in"], g["w_ffn_out"]),
                                                    {n: g[n] for n in ("w_ffn_in", "w_ffn_out")})
            mixer_done = lambda arr, g, l=l: bwd_tick(l, "mixer", (arr, g["ssm_d"]) + tuple(g[n] for n in MIXER_WEIGHTS),
                                                      {n: g[n] for n in MIXER_WEIGHTS})
        else:
            ffn_done = mixer_done = None
        dy, dyb, grads[l] = _layer_bwd(dy, dyb, saved[l], params[l], dims, "l%d" % l, {}, deps=deps,
                                       before_mixer=ffn_done, before_in=mixer_done)
        if bwd_tick:
            deps = bwd_tick(l, "in", (dy, grads[l]["w_in"]), {"w_in": grads[l]["w_in"]})
    return sq, dy, grads, deps


COL_SHARDED = ("w_in", "w_attn_branch", "w_ssm_branch", "w_ffn_in")
ROW_SHARDED = ("ssm_glu_w", "w_out", "w_ffn_out")
BIG_WEIGHTS = COL_SHARDED + ROW_SHARDED
WEIGHT_NAMES = ("norm_mix_g", "w_in", "gate_bias", "q_norm_g", "k_norm_g", "attn_sinks", "ssm_lambda_re",
                "ssm_lambda_im", "ssm_log_dt", "ssm_b_re", "ssm_b_im", "ssm_c_re", "ssm_c_im", "ssm_d", "ssm_glu_w",
                "ssm_glu_b", "w_attn_branch", "w_ssm_branch", "w_out", "norm_ffn_g", "w_ffn_in", "w_ffn_out")
SMALL_WEIGHTS = tuple(n for n in WEIGHT_NAMES if n not in BIG_WEIGHTS)
MIXER_WEIGHTS = ("w_out", "w_attn_branch", "w_ssm_branch", "ssm_glu_w")
WEIGHT_GROUPS = {"in": ("w_in",), "mixer": MIXER_WEIGHTS, "ffn": ("w_ffn_in", "w_ffn_out")}


def _dims(d, shapes, tc):
    s, _, aw, _ = shapes["w_attn_branch"]
    sw = shapes["w_ssm_branch"][2]
    in_w = shapes["w_in"][3] * s
    kvw = (in_w - aw - sw - 2 * d) // 2
    ff = shapes["w_ffn_out"][2] * s
    return dict(aw=aw, kvw=kvw, sw=sw, ff=ff, n_q=aw // HEAD_DIM, n_kv=kvw // HEAD_DIM, tc=tc)


def _big_params(big):
    p = {n: _Weight(a, 0, "col") for n, a in big.items() if n in COL_SHARDED}
    p.update({n: _Weight(a.reshape(1, 1, -1, a.shape[-1]), 0, "col") for n, a in big.items() if n in ROW_SHARDED})
    return p


def _layer_params(l, small):
    p = {}
    for n in ("norm_mix_g", "gate_bias", "q_norm_g", "k_norm_g", "ssm_d", "ssm_glu_b", "norm_ffn_g"):
        p[n] = small[n][l][None]
    p["attn_sinks"] = small["attn_sinks"][l]
    p["lam_re"] = small["ssm_lambda_re"][l]
    p["lam_im"] = small["ssm_lambda_im"][l]
    p["log_dt"] = small["ssm_log_dt"][l][:, None]
    p["b_re_t"] = jnp.swapaxes(small["ssm_b_re"][l], 1, 2)
    p["b_im_t"] = jnp.swapaxes(small["ssm_b_im"][l], 1, 2)
    p["c_re"] = small["ssm_c_re"][l]
    p["c_im"] = small["ssm_c_im"][l]
    return p


_ANY = pl.BlockSpec(memory_space=pl.ANY)
_MESH_ID = pl.DeviceIdType.MESH


def _coords():
    return lax.axis_index("x"), lax.axis_index("y"), lax.axis_index("c")


def _my_chip():
    return (2 * lax.axis_index("x") + lax.axis_index("y")).astype(jnp.int32).reshape(1)


def _my_core():
    return lax.axis_index("c").astype(jnp.int32).reshape(1)


def _cast_into_slot(w, layer, *, name, deps=()):
    _, r, c = w.shape
    tm = _row_tile(12 * c, r)

    def body(me_ref, w_ref, *rest):
        rest[-1][...] = w_ref[...].astype(rest[-1].dtype)

    return pl.pallas_call(
        body,
        out_shape=jax.ShapeDtypeStruct((N_CHIPS, 1, r, c), BF16),
        grid_spec=pltpu.PrefetchScalarGridSpec(
            num_scalar_prefetch=1,
            grid=(r // tm,),
            in_specs=[pl.BlockSpec((None, tm, c), lambda i, me: (layer, i, 0))]
            + [pl.BlockSpec(memory_space=pl.ANY)] * len(deps),
            out_specs=pl.BlockSpec((None, None, tm, c), lambda i, me: (me[0], 0, i, 0)),
        ),
        compiler_params=_cparams("parallel"),
        name=name,
    )(_my_chip(), w, *deps)


_HBM = pl.BlockSpec(memory_space=pltpu.HBM)
_SEM = pl.BlockSpec(memory_space=pltpu.SEMAPHORE)
_DATAFLOW = pltpu.SideEffectType.DATAFLOW_SIDE_EFFECTING


class _SplitExchange:
    def __init__(self, srcs, lands, build, n_copies, name):
        self.build, self.n, self.name = build, n_copies, name
        self.ns, self.nl = len(srcs), len(lands)
        self.bufs = [pltpu.with_memory_space_constraint(a, pltpu.HBM) for a in list(srcs) + list(lands)]

    def _copies(self, refs, send_sems, recv_sems):
        triples = self.build(refs[:self.ns], refs[self.ns:self.ns + self.nl])
        assert len(triples) == self.n
        return [pltpu.make_async_remote_copy(src_ref=s, dst_ref=d, send_sem=send_sems.at[k], recv_sem=recv_sems.at[k],
                                             device_id=to, device_id_type=_MESH_ID) for k, (s, d, to) in enumerate(triples)]

    def start(self, deps=()):
        nb = self.ns + self.nl

        def body(*refs):
            outs = refs[nb + len(deps):]
            for cp in self._copies(refs, outs[0], outs[1]):
                cp.start()
            outs[-1][...] = jnp.zeros_like(outs[-1])

        sems = pltpu.SemaphoreType.DMA((self.n,))
        res = pl.pallas_call(
            body,
            out_shape=(sems, sems, *[pltpu.HBM(b.shape, b.dtype) for b in self.bufs], jax.ShapeDtypeStruct((8, 128), F32)),
            in_specs=[_HBM] * nb + [_ANY] * len(deps),
            out_specs=(_SEM, _SEM, *[_HBM] * nb, pl.BlockSpec(memory_space=pltpu.VMEM)),
            input_output_aliases={i: 2 + i for i in range(nb)},
            compiler_params=pltpu.CompilerParams(has_side_effects=_DATAFLOW),
            name=self.name + "_start",
        )(*self.bufs, *deps)
        self.send_sems, self.recv_sems = res[0], res[1]
        self.bufs = list(res[2:2 + nb])
        return res[-1]

    def wait(self, after=()):
        nb = self.ns + self.nl
        after = tuple(after) if isinstance(after, (tuple, list)) else (after,)

        def body(*refs):
            for cp in self._copies(refs, refs[nb], refs[nb + 1]):
                cp.wait_send()
                cp.wait_recv()

        res = pl.pallas_call(
            body,
            out_shape=tuple(pltpu.HBM(b.shape, b.dtype) for b in self.bufs),
            in_specs=[_HBM] * nb + [_SEM, _SEM] + [_ANY] * len(after),
            out_specs=tuple([_HBM] * nb),
            input_output_aliases={i: i for i in range(nb)},
            compiler_params=pltpu.CompilerParams(has_side_effects=_DATAFLOW),
            name=self.name + "_wait",
        )(*self.bufs, self.send_sems, self.recv_sems, *after)
        res = list(res)
        return res[:self.ns], res[self.ns:]


def _other_chips(x, y):
    return [(1 - x, y), (x, 1 - y), (1 - x, 1 - y)]


def _gather_steps(bufs, tag, deps, publish):
    n = len(bufs)
    half = lambda ref, i, slot, hc: ref.at[slot, :, pl.ds(hc * (bufs[i].shape[2] // 2), bufs[i].shape[2] // 2), :]

    def over_ici(srcs, lands):
        x, y, c = _coords()
        me = 2 * x + y
        return [(half(srcs[i], i, me, c), half(srcs[i], i, me, c), (px, py, c))
                for i in range(n) for px, py in _other_chips(x, y)]

    def to_sibling(srcs, lands):
        x, y, c = _coords()
        return [(half(srcs[i], i, 2 * px + py, c), half(srcs[i], i, 2 * px + py, c), (x, y, 1 - c))
                for i in range(n) for px, py in _other_chips(x, y)]

    ex = _SplitExchange(bufs, [], over_ici, 3 * n, tag + "_ici")
    after = yield ex.start(deps)
    bufs, _ = ex.wait(after)
    ex = _SplitExchange(bufs, [], to_sibling, 3 * n, tag + "_d2d")
    after = yield ex.start()
    bufs, _ = ex.wait(after)
    publish(bufs)


def _reduce_steps(grads, tag, publish):
    n = len(grads)
    rh = [g.shape[1] // 2 for g in grads]
    theirs = [lax.empty((g.shape[0], g.shape[1] // 2, g.shape[2]), F32) for g in grads]

    def halves(srcs, lands):
        x, y, c = _coords()
        return [(srcs[i].at[:, pl.ds((1 - c) * rh[i], rh[i]), :], lands[i], (x, y, 1 - c)) for i in range(n)]

    def chips(srcs, lands):
        x, y, c = _coords()
        me = 2 * x + y
        return [(srcs[i].at[2 * px + py], lands[i].at[me], (px, py, c)) for i in range(n) for px, py in _other_chips(x, y)]

    def sibling(srcs, lands):
        x, y, c = _coords()
        return [(srcs[i], lands[i], (x, y, 1 - c)) for i in range(n)]

    ex = _SplitExchange(grads, theirs, halves, n, tag + "_halves")
    after = yield ex.start()
    grads, theirs = ex.wait(after)
    parts = [_add_own_half(g, t, name="%s_add_own_half_%d" % (tag, i)) for i, (g, t) in enumerate(zip(grads, theirs))]
    ex = _SplitExchange(parts, [lax.empty(p.shape, p.dtype) for p in parts], chips, 3 * n, tag + "_chips")
    after = yield ex.start()
    parts, got = ex.wait(after)
    mine = [_sum_chips(p, g, name="%s_sum_chips_%d" % (tag, i)) for i, (p, g) in enumerate(zip(parts, got))]
    ex = _SplitExchange(mine, [lax.empty(m.shape, m.dtype) for m in mine], sibling, n, tag + "_sibling")
    after = yield ex.start()
    mine, theirs = ex.wait(after)
    publish(list(zip(mine, theirs)))


def _allreduce_steps(buf, tag, publish):
    r, c = buf.shape

    def to_sibling(srcs, lands):
        x, y, cc = _coords()
        return [(srcs[0], lands[0], (x, y, 1 - cc))]

    def over_ici(srcs, lands):
        x, y, cc = _coords()
        me = 2 * x + y
        return [(srcs[0].at[me], srcs[0].at[me], (px, py, cc)) for px, py in _other_chips(x, y)]

    def halves(srcs, lands):
        x, y, cc = _coords()
        return [(srcs[0].at[cc], srcs[0].at[cc], (x, y, 1 - cc))]

    ex = _SplitExchange([buf], [lax.empty(buf.shape, buf.dtype)], to_sibling, 1, tag + "_cores")
    after = yield ex.start()
    (mine,), (theirs,) = ex.wait(after)
    ex = _SplitExchange([_add_half_into_slot(mine, theirs, name=tag + "_chip_sum")], [], over_ici, N_CHIPS - 1, tag + "_chips")
    after = yield ex.start()
    (parts,), _ = ex.wait(after)
    total_half = _sum_slots(parts, name=tag + "_sum_chips")
    ex = _SplitExchange([_place_into_slot(total_half, 2, _my_core(), name=tag + "_place_half")], [], halves, 1, tag + "_halves")
    after = yield ex.start()
    (both,), _ = ex.wait(after)
    publish(both.reshape(r, c))


class _Exchanges:
    def __init__(self):
        self.running = []

    def launch(self, steps):
        self.running.append(steps)
        return next(steps)

    def advance(self, steps, after):
        try:
            return steps.send(after)
        except StopIteration:
            self.running.remove(steps)
            return None

    def advance_all(self, after):
        tokens = [self.advance(steps, after) for steps in list(self.running)]
        return tuple(t for t in tokens if t is not None)


def _add_own_half(g, theirs, *, name):
    s, r, c = g.shape
    rh = r // 2
    tm = _row_tile(10 * c, rh)
    nb = rh // tm

    def body(core_ref, g_ref, t_ref, o_ref):
        o_ref[...] = (g_ref[...] + t_ref[...]).astype(o_ref.dtype)

    return pl.pallas_call(
        body,
        out_shape=jax.ShapeDtypeStruct((s, rh, c), BF16),
        grid_spec=pltpu.PrefetchScalarGridSpec(
            num_scalar_prefetch=1,
            grid=(s, nb),
            in_specs=[pl.BlockSpec((None, tm, c), lambda k, i, core: (k, core[0] * nb + i, 0)),
                      pl.BlockSpec((None, tm, c), lambda k, i, core: (k, i, 0))],
            out_specs=pl.BlockSpec((None, tm, c), lambda k, i, core: (k, i, 0)),
        ),
        compiler_params=_cparams("parallel", "parallel"),
        name=name,
    )(_my_core(), g, theirs)


def _sum_chips(part, got, *, name):
    s, rh, c = part.shape
    tm = _row_tile(14 * c, rh)

    def body(me_ref, p_ref, a_ref, b_ref, c_ref, o_ref):
        o_ref[...] = ((p_ref[...].astype(F32) + a_ref[...].astype(F32)) + b_ref[...].astype(F32)) + c_ref[...].astype(F32)

    slot = lambda k: (lambda i, me: ((me[0] + k) % s, i, 0))
    return pl.pallas_call(
        body,
        out_shape=jax.ShapeDtypeStruct((rh, c), F32),
        grid_spec=pltpu.PrefetchScalarGridSpec(
            num_scalar_prefetch=1,
            grid=(rh // tm,),
            in_specs=[pl.BlockSpec((None, tm, c), slot(k)) for k in range(s)],
            out_specs=pl.BlockSpec((tm, c), lambda i, me: (i, 0)),
        ),
        compiler_params=_cparams("parallel"),
        name=name,
    )(_my_chip(), part, got, got, got)


def _place_into_slot(buf, n_slots, slot, *, name):
    r, c = buf.shape
    tm = _row_tile(8 * c, r)

    def body(slot_ref, i_ref, o_ref):
        o_ref[...] = i_ref[...]

    return pl.pallas_call(
        body,
        out_shape=jax.ShapeDtypeStruct((n_slots, r, c), buf.dtype),
        grid_spec=pltpu.PrefetchScalarGridSpec(
            num_scalar_prefetch=1,
            grid=(r // tm,),
            in_specs=[pl.BlockSpec((tm, c), lambda i, s: (i, 0))],
            out_specs=pl.BlockSpec((None, tm, c), lambda i, s: (s[0], i, 0)),
        ),
        compiler_params=_cparams("parallel"),
        name=name,
    )(slot, buf)


def _add_half_into_slot(mine, theirs, *, name):
    r, c = mine.shape
    rh = r // 2
    tm = _row_tile(12 * c, rh)
    nb = rh // tm
    where = jnp.concatenate([_my_chip(), _my_core()])

    def body(where_ref, a_ref, b_ref, o_ref):
        o_ref[...] = a_ref[...] + b_ref[...]

    half = pl.BlockSpec((tm, c), lambda i, w: (w[1] * nb + i, 0))
    return pl.pallas_call(
        body,
        out_shape=jax.ShapeDtypeStruct((N_CHIPS, rh, c), mine.dtype),
        grid_spec=pltpu.PrefetchScalarGridSpec(
            num_scalar_prefetch=1,
            grid=(nb,),
            in_specs=[half, half],
            out_specs=pl.BlockSpec((None, tm, c), lambda i, w: (w[0], i, 0)),
        ),
        compiler_params=_cparams("parallel"),
        name=name,
    )(where, mine, theirs)


def _sum_slots(arr, *, name):
    s, r, c = arr.shape
    tm = _row_tile(4 * c * (s + 1), r)

    def body(*refs):
        acc = refs[0][...]
        for ref in refs[1:s]:
            acc = acc + ref[...]
        refs[s][...] = acc

    return pl.pallas_call(
        body,
        out_shape=jax.ShapeDtypeStruct((r, c), arr.dtype),
        grid=(r // tm,),
        in_specs=[pl.BlockSpec((None, tm, c), lambda i, k=k: (k, i, 0)) for k in range(s)],
        out_specs=pl.BlockSpec((tm, c), lambda i: (i, 0)),
        compiler_params=_cparams("parallel"),
        name=name,
    )(*([arr] * s))


def _adamw_fn(w, g, m, v):
    m = ADAM_B1 * m + (1.0 - ADAM_B1) * g
    v = ADAM_B2 * v + (1.0 - ADAM_B2) * jnp.square(g)
    m_hat = m / (1.0 - ADAM_B1 ** ADAM_STEP)
    v_hat = v / (1.0 - ADAM_B2 ** ADAM_STEP)
    delta = -ADAM_LR * (m_hat / (jnp.sqrt(v_hat) + ADAM_EPS) + ADAM_WD * w)
    return delta, m, v


def _adamw(w, g, m, v, *, name):
    rows, cols = w.shape
    ins = [(a, "row", cols, 0) for a in (w, g, m, v)]
    outs = [(cols, F32, "row", cols)] * 3
    return _rowmap(_adamw_fn, ins, outs, rows=rows, tm=_row_tile(56 * cols, rows), name=name)


def _adamw_sharded(w, m, v, g_mine, g_sibling, layer, into, *, name, deps=()):
    nl, r, c = w.shape
    rh = r // 2
    tm = _row_tile(40 * c, rh)
    nb = rh // tm
    n_into = 0 if into is None else 4

    def body(core_ref, w_ref, m_ref, v_ref, a_ref, b_ref, *rest):
        g_ref, d_ref, nm_ref, nv_ref = rest[n_into + len(deps):]
        g = jnp.where(pl.program_id(0) == core_ref[0], a_ref[...], b_ref[...])
        delta, nm, nv = _adamw_fn(w_ref[...], g, m_ref[...], v_ref[...])
        g_ref[...] = g
        d_ref[...] = delta
        nm_ref[...] = nm
        nv_ref[...] = nv

    whole = pl.BlockSpec((None, tm, c), lambda h, i, core: (layer, h * nb + i, 0))
    half = pl.BlockSpec((tm, c), lambda h, i, core: (i, 0))
    shape = jax.ShapeDtypeStruct((nl, r, c), F32)
    return pl.pallas_call(
        body,
        out_shape=(shape, shape, shape, shape),
        grid_spec=pltpu.PrefetchScalarGridSpec(
            num_scalar_prefetch=1,
            grid=(2, nb),
            in_specs=[whole, whole, whole, half, half] + [pl.BlockSpec(memory_space=pl.ANY)] * (n_into + len(deps)),
            out_specs=(whole, whole, whole, whole),
        ),
        input_output_aliases={6 + k: k for k in range(n_into)},
        compiler_params=_cparams("parallel", "parallel"),
        name=name,
    )(_my_core(), w, m, v, g_mine, g_sibling, *(into or ()), *deps)


def _pack(arrays):
    flat = jnp.concatenate([a.reshape(-1) for a in arrays])
    pad = (-flat.shape[0]) % (256 * 128)
    return jnp.pad(flat, (0, pad)).reshape(-1, 128)


def _unpack(buf, shapes):
    flat = buf.reshape(-1)
    out, off = [], 0
    for s in shapes:
        n = math.prod(s)
        out.append(flat[off:off + n].reshape(s))
        off += n
    return out


def kernel(x, norm_mix_g, w_in, gate_bias, q_norm_g, k_norm_g, attn_sinks, ssm_lambda_re, ssm_lambda_im, ssm_log_dt, ssm_b_re, ssm_b_im, ssm_c_re, ssm_c_im, ssm_d, ssm_glu_w, ssm_glu_b, w_attn_branch, w_ssm_branch, w_out, norm_ffn_g, w_ffn_in, w_ffn_out, loss_target, m_norm_mix_g, m_w_in, m_gate_bias, m_q_norm_g, m_k_norm_g, m_attn_sinks, m_ssm_lambda_re, m_ssm_lambda_im, m_ssm_log_dt, m_ssm_b_re, m_ssm_b_im, m_ssm_c_re, m_ssm_c_im, m_ssm_d, m_ssm_glu_w, m_ssm_glu_b, m_w_attn_branch, m_w_ssm_branch, m_w_out, m_norm_ffn_g, m_w_ffn_in, m_w_ffn_out, v_norm_mix_g, v_w_in, v_gate_bias, v_q_norm_g, v_k_norm_g, v_attn_sinks, v_ssm_lambda_re, v_ssm_lambda_im, v_ssm_log_dt, v_ssm_b_re, v_ssm_b_im, v_ssm_c_re, v_ssm_c_im, v_ssm_d, v_ssm_glu_w, v_ssm_glu_b, v_w_attn_branch, v_w_ssm_branch, v_w_out, v_norm_ffn_g, v_w_ffn_in, v_w_ffn_out):
    w = dict(norm_mix_g=norm_mix_g, w_in=w_in, gate_bias=gate_bias, q_norm_g=q_norm_g, k_norm_g=k_norm_g,
             attn_sinks=attn_sinks, ssm_lambda_re=ssm_lambda_re, ssm_lambda_im=ssm_lambda_im, ssm_log_dt=ssm_log_dt,
             ssm_b_re=ssm_b_re, ssm_b_im=ssm_b_im, ssm_c_re=ssm_c_re, ssm_c_im=ssm_c_im, ssm_d=ssm_d,
             ssm_glu_w=ssm_glu_w, ssm_glu_b=ssm_glu_b, w_attn_branch=w_attn_branch, w_ssm_branch=w_ssm_branch,
             w_out=w_out, norm_ffn_g=norm_ffn_g, w_ffn_in=w_ffn_in, w_ffn_out=w_ffn_out)
    m = dict(norm_mix_g=m_norm_mix_g, w_in=m_w_in, gate_bias=m_gate_bias, q_norm_g=m_q_norm_g, k_norm_g=m_k_norm_g,
             attn_sinks=m_attn_sinks, ssm_lambda_re=m_ssm_lambda_re, ssm_lambda_im=m_ssm_lambda_im,
             ssm_log_dt=m_ssm_log_dt, ssm_b_re=m_ssm_b_re, ssm_b_im=m_ssm_b_im, ssm_c_re=m_ssm_c_re,
             ssm_c_im=m_ssm_c_im, ssm_d=m_ssm_d, ssm_glu_w=m_ssm_glu_w, ssm_glu_b=m_ssm_glu_b,
             w_attn_branch=m_w_attn_branch, w_ssm_branch=m_w_ssm_branch, w_out=m_w_out, norm_ffn_g=m_norm_ffn_g,
             w_ffn_in=m_w_ffn_in, w_ffn_out=m_w_ffn_out)
    v = dict(norm_mix_g=v_norm_mix_g, w_in=v_w_in, gate_bias=v_gate_bias, q_norm_g=v_q_norm_g, k_norm_g=v_k_norm_g,
             attn_sinks=v_attn_sinks, ssm_lambda_re=v_ssm_lambda_re, ssm_lambda_im=v_ssm_lambda_im,
             ssm_log_dt=v_ssm_log_dt, ssm_b_re=v_ssm_b_re, ssm_b_im=v_ssm_b_im, ssm_c_re=v_ssm_c_re,
             ssm_c_im=v_ssm_c_im, ssm_d=v_ssm_d, ssm_glu_w=v_ssm_glu_w, ssm_glu_b=v_ssm_glu_b,
             w_attn_branch=v_w_attn_branch, w_ssm_branch=v_w_ssm_branch, w_out=v_w_out, norm_ffn_g=v_norm_ffn_g,
             w_ffn_in=v_w_ffn_in, w_ffn_out=v_w_ffn_out)
    n_layers = norm_mix_g.shape[0]
    d_model = x.shape[-1]
    seq = x.shape[1]

    exchanges = _Exchanges()
    params = [_layer_params(l, w) for l in range(n_layers)]
    gathers = {}

    def gather(l, group, bufs, deps=()):
        names = WEIGHT_GROUPS[group]
        steps = _gather_steps(bufs, "ag%d_%s" % (l, group), deps,
                              lambda got: params[l].update(_big_params(dict(zip(names, got)))))
        gathers[l, group] = steps
        return exchanges.launch(steps)

    started = gather(0, "in", [_cast_into_slot(w["w_in"], 0, name="cast0_w_in")])
    casts = {(l, group): [_cast_into_slot(w[n], l, name="cast%d_%s" % (l, n), deps=(started,)) for n in WEIGHT_GROUPS[group]]
             for l in range(n_layers) for group in WEIGHT_GROUPS if (l, group) != (0, "in")}
    forwarded = exchanges.advance(gathers[0, "in"], tuple(b for bufs in casts.values() for b in bufs))
    first_deps = (forwarded, gather(0, "mixer", casts[0, "mixer"], (forwarded,)), gather(0, "ffn", casts[0, "ffn"], (forwarded,)))
    sizes = {n: (N_CHIPS, 1) + w[n].shape[1:] for n in BIG_WEIGHTS}
    dims = _dims(d_model, sizes, min(512, seq))

    def weights_of(l, h):
        if l == 0:
            return params[0], first_deps
        for group in WEIGHT_GROUPS:
            exchanges.advance(gathers[l, group], h)
        return params[l], ()

    def fwd_tick(l, point, arr):
        tokens = []
        if l == 0 and point == "norm":
            tokens.append(exchanges.advance(gathers[0, "in"], arr))
        if l == 0 and point in ("in", "attn"):
            tokens.append(exchanges.advance(gathers[0, "mixer"], arr))
        if l == 0 and point in ("ssm", "out"):
            tokens.append(exchanges.advance(gathers[0, "ffn"], arr))
        if l + 1 < n_layers and point == "attn":
            tokens += [gather(l + 1, group, casts[l + 1, group]) for group in WEIGHT_GROUPS]
        if l + 1 < n_layers and point == "act":
            tokens += [exchanges.advance(gathers[l + 1, group], arr) for group in WEIGHT_GROUPS]
        return tuple(t for t in tokens if t is not None)

    reduced, ready = {}, []

    def bwd_tick(l, stage, arr, stage_grads):
        tokens = exchanges.advance_all(arr)
        names = tuple(stage_grads)

        def publish(halves):
            reduced.update({(l, n): h for n, h in zip(names, halves)})
            ready.append((l, names))

        grads4 = [stage_grads[n].reshape(N_CHIPS, -1, stage_grads[n].shape[-1]) for n in names]
        return tokens + (exchanges.launch(_reduce_steps(grads4, "rs%d_%s" % (l, stage), publish)),)

    sq, dx, grads, last_tokens = _local_step(x[0], loss_target[0], n_layers, weights_of, dims, fwd_tick, bwd_tick)
    loss = lax.psum(sq[0, 0], MESH_AXES) * (0.5 / d_model)

    small_shapes = [w[n].shape for n in SMALL_WEIGHTS]
    small_local = [jnp.stack([grads[l][n].reshape(w[n].shape[1:]) for l in range(n_layers)]) for n in SMALL_WEIGHTS]
    shared = []
    tokens = last_tokens + (exchanges.launch(_allreduce_steps(_pack(small_local), "small_grads", shared.append)),)

    adam = {n: None for n in BIG_WEIGHTS}
    grad, delta, new_m, new_v = {}, {}, {}, {}
    while exchanges.running or ready or shared:
        after = [dx]
        for l, names in ready[:2]:
            for n in names:
                mine, sibling = reduced[l, n]
                adam[n] = _adamw_sharded(w[n], m[n], v[n], mine, sibling, l, adam[n], name="adamw%d_%s" % (l, n), deps=tokens)
                after.append(adam[n][1])
        del ready[:2]
        if shared:
            grad.update(zip(SMALL_WEIGHTS, _unpack(shared.pop(), small_shapes)))
            for n in SMALL_WEIGHTS:
                flat = lambda a: a.reshape(-1, a.shape[-1])
                res = _adamw(flat(w[n]), flat(grad[n]), flat(m[n]), flat(v[n]), name="adamw_" + n)
                delta[n], new_m[n], new_v[n] = [r.reshape(w[n].shape) for r in res]
            after += [delta[n] for n in SMALL_WEIGHTS]
        tokens = exchanges.advance_all(tuple(after))
    for n in BIG_WEIGHTS:
        grad[n], delta[n], new_m[n], new_v[n] = adam[n]

    return (loss, dx[None], *[grad[n] for n in WEIGHT_NAMES], *[delta[n] for n in WEIGHT_NAMES],
            *[new_m[n] for n in WEIGHT_NAMES], *[new_v[n] for n in WEIGHT_NAMES])
```

```python
import math

import jax
import jax.numpy as jnp
from jax import lax
from jax.experimental import pallas as pl
from jax.experimental.pallas import tpu as pltpu

HEAD_DIM = 64
WINDOW = 128
SSM_GROUP_CH = 16
SSM_LANE_GROUPS = 8
RMS_EPS = 1e-6
ADAM_LR = 0.001
ADAM_B1 = 0.9
ADAM_B2 = 0.999
ADAM_EPS = 1e-08
ADAM_WD = 0.01
ADAM_STEP = 10
NEG_BIG = -1e30
MESH_AXES = ("x", "y", "c")
N_CHIPS = 4
N_DEV = 8
VMEM_LIMIT_BYTES = 56 * 1024 * 1024
BF16 = jnp.bfloat16
F32 = jnp.float32


def _cparams(*semantics):
    return pltpu.CompilerParams(dimension_semantics=semantics, vmem_limit_bytes=VMEM_LIMIT_BYTES)


def _pick(n, target, mult):
    if n <= target:
        return n
    best = None
    for d in range(mult, target + 1, mult):
        if n % d == 0:
            best = d
    assert best is not None, (n, target, mult)
    return best


def _rowmap(fn, ins, outs, *, rows, tm, ncol=1, name, deps=()):
    n_in = len(ins)
    nrow = rows // tm
    assert nrow * tm == rows

    in_specs = []
    for arr, kind, width, coloff in ins:
        if kind == "row":
            in_specs.append(pl.BlockSpec((tm, width), lambda j, i, o=coloff: (i, o + j)))
        elif kind == "vec":
            in_specs.append(pl.BlockSpec((1, width), lambda j, i, o=coloff: (0, o + j)))
        else:
            nd = arr.ndim
            in_specs.append(pl.BlockSpec(arr.shape, lambda j, i, nd=nd: (0,) * nd))
    out_specs, out_shapes = [], []
    for cols, dtype, kind, width in outs:
        if kind == "row":
            out_specs.append(pl.BlockSpec((tm, width), lambda j, i: (i, j)))
            out_shapes.append(jax.ShapeDtypeStruct((rows, cols), dtype))
        else:
            out_specs.append(pl.BlockSpec((1, width), lambda j, i: (0, j)))
            out_shapes.append(jax.ShapeDtypeStruct((1, cols), dtype))

    in_specs += [pl.BlockSpec(memory_space=pl.ANY)] * len(deps)

    def body(*refs):
        i = pl.program_id(1)
        res = fn(*[r[...].astype(F32) for r in refs[:n_in]])
        if not isinstance(res, (tuple, list)):
            res = (res,)
        for (cols, dtype, kind, width), ref, val in zip(outs, refs[n_in + len(deps):], res):
            if kind == "row":
                ref[...] = val.astype(ref.dtype)
            else:
                @pl.when(i == 0)
                def _():
                    ref[...] = jnp.zeros_like(ref)
                ref[...] += val.astype(ref.dtype)

    res = pl.pallas_call(
        body,
        out_shape=tuple(out_shapes),
        grid=(ncol, nrow),
        in_specs=in_specs,
        out_specs=tuple(out_specs),
        compiler_params=_cparams("parallel", "arbitrary"),
        name=name,
    )(*[a[0] for a in ins], *deps)
    return res


def _mm_body(dims, nk, has_add, unused_in=0):
    def body(*refs):
        if has_add:
            a_ref, b_ref, add_ref = refs[:3]
            o_ref = refs[3 + unused_in]
            rest = refs[4 + unused_in:]
        else:
            a_ref, b_ref = refs[:2]
            o_ref = refs[2 + unused_in]
            add_ref = None
            rest = refs[3 + unused_in:]
        part = lax.dot_general(a_ref[...], b_ref[...], (dims, ((), ())), preferred_element_type=F32)
        if nk == 1:
            if add_ref is not None:
                part = part + add_ref[...]
            o_ref[...] = part.astype(o_ref.dtype)
        else:
            acc_ref = rest[0]
            k = pl.program_id(2)

            @pl.when(k == 0)
            def _():
                acc_ref[...] = part

            @pl.when(k > 0)
            def _():
                acc_ref[...] += part

            @pl.when(k == nk - 1)
            def _():
                r = acc_ref[...]
                if add_ref is not None:
                    r = r + add_ref[...]
                o_ref[...] = r.astype(o_ref.dtype)
    return body


class _Weight:
    def __init__(self, arr, layer, kind):
        self.arr, self.layer, self.kind = arr, layer, kind
        self.s, _, self.r, self.c = arr.shape
        self.rows = self.r * (self.s if kind == "row" else 1)
        self.cols = self.c * (self.s if kind == "col" else 1)

    def tiles(self, tr, tc):
        return _pick(self.r, tr, 128), _pick(self.c, tc, 128)

    def index(self, tr, tc):
        layer = self.layer
        if self.kind == "col":
            per = self.c // tc
            return lambda rb, cb: (cb // per, layer, rb, cb % per)
        per = self.r // tr
        return lambda rb, cb: (rb // per, layer, rb % per, cb)


def _shard_index(kind, r, c, tr, tc):
    if kind == "col":
        per = c // tc
        return lambda rb, cb: (cb // per, rb, cb % per)
    per = r // tr
    return lambda rb, cb: (rb // per, rb % per, cb)


def _mm_nn(a, w, *, out_dtype, tm, tn, tk, name, add=None, deps=()):
    m, k = a.shape
    assert k == w.rows
    tm = _pick(m, tm, 16)
    tk, tn = w.tiles(tk, tn)
    nk = k // tk
    widx = w.index(tk, tn)
    in_specs = [pl.BlockSpec((tm, tk), lambda n, i, kk: (i, kk)),
                pl.BlockSpec((None, None, tk, tn), lambda n, i, kk: widx(kk, n))]
    args = [a, w.arr]
    if add is not None:
        in_specs.append(pl.BlockSpec((tm, tn), lambda n, i, kk: (i, n)))
        args.append(add)
    in_specs += [pl.BlockSpec(memory_space=pl.ANY)] * len(deps)
    args += list(deps)
    return pl.pallas_call(
        _mm_body(((1,), (0,)), nk, add is not None, unused_in=len(deps)),
        out_shape=jax.ShapeDtypeStruct((m, w.cols), out_dtype),
        grid=(w.cols // tn, m // tm, nk),
        in_specs=in_specs,
        out_specs=pl.BlockSpec((tm, tn), lambda n, i, kk: (i, n)),
        scratch_shapes=[pltpu.VMEM((tm, tn), F32)] if nk > 1 else [],
        compiler_params=_cparams("parallel", "parallel", "arbitrary"),
        name=name,
    )(*args)


def _interleaved_block(k, n_blocks):
    half = n_blocks // 2
    if isinstance(k, int):
        return 2 * k if k < half else 2 * (k - half) + 1
    return jnp.where(k < half, 2 * k, 2 * (k - half) + 1)


def _mm_nt(a, w, *, out_dtype, tm, tn, tko, name, deps=(), interleaved=0):
    m, n = a.shape
    assert n == w.cols
    tm = _pick(m, tm, 16)
    if w.kind == "col" and w.s > 1:
        tko = _pick(w.r, tko, 128)
        layer, nsh, width = w.layer, w.s, w.c
        blk = interleaved or width
        per = width // blk

        def body(a_ref, w_ref, *rest):
            o_ref = rest[len(deps)]
            acc = None
            for k in range(nsh * per):
                s, j = divmod(k, per)
                at = (_interleaved_block(k, nsh * per) if interleaved else k) * blk
                part = lax.dot_general(a_ref[:, at:at + blk], w_ref[s, :, j * blk:(j + 1) * blk], (((1,), (1,)), ((), ())),
                                       preferred_element_type=F32)
                acc = part if acc is None else acc + part
            o_ref[...] = acc.astype(o_ref.dtype)

        return pl.pallas_call(
            body,
            out_shape=jax.ShapeDtypeStruct((m, w.rows), out_dtype),
            grid=(w.rows // tko, m // tm),
            in_specs=[pl.BlockSpec((tm, n), lambda ko, i: (i, 0)),
                      pl.BlockSpec((nsh, None, tko, width), lambda ko, i: (0, layer, ko, 0))]
            + [pl.BlockSpec(memory_space=pl.ANY)] * len(deps),
            out_specs=pl.BlockSpec((tm, tko), lambda ko, i: (i, ko)),
            compiler_params=_cparams("parallel", "parallel"),
            name=name,
        )(a, w.arr, *deps)
    tko, tn = w.tiles(tko, tn)
    nk = n // tn
    widx = w.index(tko, tn)
    return pl.pallas_call(
        _mm_body(((1,), (1,)), nk, False, unused_in=len(deps)),
        out_shape=jax.ShapeDtypeStruct((m, w.rows), out_dtype),
        grid=(w.rows // tko, m // tm, nk),
        in_specs=[pl.BlockSpec((tm, tn), lambda ko, i, nn: (i, nn)),
                  pl.BlockSpec((None, None, tko, tn), lambda ko, i, nn: widx(ko, nn))]
        + [pl.BlockSpec(memory_space=pl.ANY)] * len(deps),
        out_specs=pl.BlockSpec((tm, tko), lambda ko, i, nn: (i, ko)),
        scratch_shapes=[pltpu.VMEM((tm, tko), F32)] if nk > 1 else [],
        compiler_params=_cparams("parallel", "parallel", "arbitrary"),
        name=name,
    )(a, w.arr, *deps)


def _mm_tn(a, c, w, *, into, tm, tn, tko, name, interleaved=False):
    m, k = a.shape
    tm = _pick(m, tm, 16)
    layer = w.layer
    mc, n = c.shape
    assert mc == m and k == w.rows and n == w.cols
    tko, tn = w.tiles(tko, tn)
    nk = m // tm
    oidx = _shard_index(w.kind, w.r, w.c, tko, tn)
    n_blocks = n // tn
    c_block = (lambda nn: _interleaved_block(nn, n_blocks)) if interleaved else (lambda nn: nn)
    in_specs = [pl.BlockSpec((tm, tko), lambda ko, nn, mm: (mm, ko)),
                pl.BlockSpec((tm, tn), lambda ko, nn, mm: (mm, c_block(nn)))]
    args = [a, c]
    if into is not None:
        in_specs.append(pl.BlockSpec(memory_space=pl.ANY))
        args.append(into)
    return pl.pallas_call(
        _mm_body(((0,), (0,)), nk, False, unused_in=len(args) - 2),
        out_shape=jax.ShapeDtypeStruct((w.arr.shape[1], w.s, w.r, w.c), F32),
        grid=(k // tko, n // tn, nk),
        in_specs=in_specs,
        out_specs=pl.BlockSpec((None, None, tko, tn), lambda ko, nn, mm: (layer,) + oidx(ko, nn)),
        scratch_shapes=[pltpu.VMEM((tko, tn), F32)] if nk > 1 else [],
        input_output_aliases={2: 0} if into is not None else {},
        compiler_params=_cparams("parallel", "parallel", "arbitrary"),
        name=name,
    )(*args)


def _mxu_sum(x, ones):
    hi = x.astype(BF16)
    lo = (x - hi.astype(F32)).astype(BF16)
    return jnp.dot(hi, ones, preferred_element_type=F32) + jnp.dot(lo, ones, preferred_element_type=F32)


def _head_rms(x, gain, sums):
    r = lax.rsqrt(_mxu_sum(x * x, sums[0]) * (1.0 / HEAD_DIM) + RMS_EPS)
    if len(sums) == 2:
        r = _mxu_sum(r, sums[1])
    return x * r * gain, r


def _head_rms_bwd(x, r, gain, dy, sums, fold):
    t = dy * gain
    mean = _mxu_sum(t * x, sums[0]) * (1.0 / HEAD_DIM)
    dx = r * t - x * (r * r * r) * (mean if len(sums) == 1 else _mxu_sum(mean, sums[1]))
    dg = jnp.broadcast_to(jnp.sum(dy * x * r, axis=0, keepdims=True), (8, x.shape[1]))
    return dx, _mxu_sum(dg, fold)[0:1, :HEAD_DIM]


def _attn_consts(n_q, n_kv, sinks, qg, kg):
    group = n_q // n_kv
    t = jnp.arange(WINDOW, dtype=jnp.int32)[:, None]
    s = jnp.arange(2 * WINDOW, dtype=jnp.int32)[None, :] - WINDOW
    dist = (t - s).astype(F32)
    valid = (dist >= 0) & (dist < WINDOW)
    slopes = jnp.exp2(-8.0 * jnp.arange(1, n_q + 1, dtype=F32) / n_q)
    bias = jnp.where(valid[None], -slopes[:, None, None] * dist[None], NEG_BIG)
    sink = jnp.broadcast_to(sinks.astype(F32).reshape(n_kv, group, 1, 1), (n_kv, group, WINDOW, 128))
    head_ones = lambda h: jnp.kron(jnp.eye(h, dtype=F32), jnp.ones((HEAD_DIM, HEAD_DIM), F32)).astype(BF16)
    fold = lambda h: jnp.tile(jnp.eye(HEAD_DIM, 128, dtype=F32), (h, 1)).astype(BF16)
    q_heads = jnp.kron(jnp.eye(n_q, 128, dtype=F32), jnp.ones((HEAD_DIM, 1), F32)).astype(BF16)
    return dict(
        bias=bias.reshape(n_kv, group * WINDOW, 2 * WINDOW),
        sink=sink.reshape(n_kv, group * WINDOW, 128),
        qg=jnp.tile(qg, (1, n_q)), kg=jnp.tile(kg, (1, n_kv)),
        q_heads=q_heads, q_spread=q_heads.T,
        k_ones=head_ones(n_kv),
        q_fold=fold(n_q), k_fold=fold(n_kv),
        key_ones=jnp.ones((2 * WINDOW, 128), BF16))


_ATTN_CONST_ORDER = ("qg", "kg", "sink", "bias", "q_heads", "q_spread", "k_ones", "q_fold", "k_fold", "key_ones")


def _attn_inputs(q_ref, kc_ref, kp_ref, vc_ref, vp_ref, c):
    q = q_ref[...]
    k2 = jnp.concatenate([kp_ref[...], kc_ref[...]], axis=0)
    v2 = jnp.concatenate([vp_ref[...], vc_ref[...]], axis=0)
    qn, rq = _head_rms(q, c["qg"][...], (c["q_heads"][...], c["q_spread"][...]))
    kn, rk = _head_rms(k2, c["kg"][...], (c["k_ones"][...],))
    return dict(q=q, rq=rq, qn=qn.astype(BF16), k2=k2, rk=rk, kn=kn.astype(BF16), v2=v2.astype(BF16))


def _attn_probs(x, c, first_mask, kv, group):
    sl = slice(kv * HEAD_DIM, (kv + 1) * HEAD_DIM)
    k2b, v2b = x["kn"][:, sl], x["v2"][:, sl]
    qs = jnp.concatenate([x["qn"][:, (kv * group + g) * HEAD_DIM:(kv * group + g + 1) * HEAD_DIM]
                          for g in range(group)], axis=0)
    s = lax.dot_general(qs, k2b, (((1,), (1,)), ((), ())), preferred_element_type=F32) * (HEAD_DIM ** -0.5)
    s = jnp.where(first_mask, NEG_BIG, s + c["bias"][kv])
    sink = c["sink"][kv]
    m = jnp.maximum(jnp.max(s, axis=-1, keepdims=True), sink)
    twice = lambda a: jnp.concatenate([a, a], axis=1)
    p = jnp.exp(s - twice(m))
    esink = jnp.exp(sink - m)
    inv = 1.0 / (_mxu_sum(p, c["key_ones"][...]) + esink)
    return dict(k2b=k2b, v2b=v2b, qs=qs, pn=p * twice(inv), psink=esink * inv, twice=twice)


def _attn_specs(n_q, n_kv):
    aw, kvw = n_q * HEAD_DIM, n_kv * HEAD_DIM
    group = n_q // n_kv
    kblk, vblk = aw // kvw, aw // kvw + 1

    def specs(nb):
        cur = lambda n: jnp.minimum(n, nb - 1)
        prev = lambda n: jnp.maximum(jnp.minimum(n, nb - 1) - 1, 0)
        return [
            pl.BlockSpec((WINDOW, aw), lambda n: (cur(n), 0)),
            pl.BlockSpec((WINDOW, kvw), lambda n: (cur(n), kblk)),
            pl.BlockSpec((WINDOW, kvw), lambda n: (prev(n), kblk)),
            pl.BlockSpec((WINDOW, kvw), lambda n: (cur(n), vblk)),
            pl.BlockSpec((WINDOW, kvw), lambda n: (prev(n), vblk)),
        ]
    whole = lambda shape: pl.BlockSpec(shape, lambda n: (0,) * len(shape))
    return specs, whole


def _attn_fwd(z, qg, kg, sinks, *, n_q, n_kv, name, deps=()):
    L = z.shape[0]
    nb = L // WINDOW
    aw = n_q * HEAD_DIM
    group = n_q // n_kv
    consts = _attn_consts(n_q, n_kv, sinks, qg, kg)
    specs, whole = _attn_specs(n_q, n_kv)
    nc = len(_ATTN_CONST_ORDER)

    def body(q_ref, kc_ref, kp_ref, vc_ref, vp_ref, *rest):
        c = dict(zip(_ATTN_CONST_ORDER, rest[:nc]))
        o_ref = rest[-1]
        n = pl.program_id(0)
        col = lax.broadcasted_iota(jnp.int32, (group * WINDOW, 2 * WINDOW), 1)
        first_mask = jnp.logical_and(n == 0, col < WINDOW)
        x = _attn_inputs(q_ref, kc_ref, kp_ref, vc_ref, vp_ref, c)
        for kv in range(n_kv):
            a = _attn_probs(x, c, first_mask, kv, group)
            o = jnp.dot(a["pn"].astype(BF16), a["v2b"], preferred_element_type=F32)
            for g in range(group):
                h = kv * group + g
                o_ref[:, h * HEAD_DIM:(h + 1) * HEAD_DIM] = o[g * WINDOW:(g + 1) * WINDOW].astype(o_ref.dtype)

    return pl.pallas_call(
        body,
        out_shape=jax.ShapeDtypeStruct((L, aw), BF16),
        grid=(nb,),
        in_specs=specs(nb) + [whole(consts[k].shape) for k in _ATTN_CONST_ORDER]
        + [pl.BlockSpec(memory_space=pl.ANY)] * len(deps),
        out_specs=pl.BlockSpec((WINDOW, aw), lambda n: (n, 0)),
        compiler_params=_cparams("parallel"),
        name=name,
    )(z, z, z, z, z, *[consts[k] for k in _ATTN_CONST_ORDER], *deps)


def _attn_bwd(z, do, qg, kg, sinks, *, n_q, n_kv, name):
    L = z.shape[0]
    nb = L // WINDOW
    aw, kvw = n_q * HEAD_DIM, n_kv * HEAD_DIM
    group = n_q // n_kv
    consts = _attn_consts(n_q, n_kv, sinks, qg, kg)
    specs, whole = _attn_specs(n_q, n_kv)
    scale = HEAD_DIM ** -0.5
    nc = len(_ATTN_CONST_ORDER)

    def body(q_ref, kc_ref, kp_ref, vc_ref, vp_ref, do_ref, *rest):
        c = dict(zip(_ATTN_CONST_ORDER, rest[:nc]))
        dq_ref, dkv_ref, dqg_ref, dkg_ref, dsink_ref, carry_ref, dqn_ref, dkn_ref, dv_ref = rest[nc:]
        n = pl.program_id(0)

        @pl.when(n == 0)
        def _():
            dqg_ref[...] = jnp.zeros_like(dqg_ref)
            dkg_ref[...] = jnp.zeros_like(dkg_ref)
            dsink_ref[...] = jnp.zeros_like(dsink_ref)
            carry_ref[...] = jnp.zeros_like(carry_ref)

        @pl.when(n < nb)
        def _():
            col = lax.broadcasted_iota(jnp.int32, (group * WINDOW, 2 * WINDOW), 1)
            first_mask = jnp.logical_and(n == 0, col < WINDOW)
            head_lane = lax.broadcasted_iota(jnp.int32, (1, n_q), 1)
            x = _attn_inputs(q_ref, kc_ref, kp_ref, vc_ref, vp_ref, c)
            dsink = jnp.zeros((1, n_q), F32)
            for kv in range(n_kv):
                a = _attn_probs(x, c, first_mask, kv, group)
                pn = a["pn"]
                dos = jnp.concatenate(
                    [do_ref[:, (kv * group + g) * HEAD_DIM:(kv * group + g + 1) * HEAD_DIM] for g in range(group)],
                    axis=0).astype(BF16)
                dpn = lax.dot_general(dos, a["v2b"], (((1,), (1,)), ((), ())), preferred_element_type=F32)
                ksl = slice(kv * HEAD_DIM, (kv + 1) * HEAD_DIM)
                dv_ref[:, ksl] = lax.dot_general(pn.astype(BF16), dos, (((0,), (0,)), ((), ())), preferred_element_type=F32)
                delta = _mxu_sum(pn * dpn, c["key_ones"][...])
                ds = (pn * (dpn - a["twice"](delta))).astype(BF16)
                dsk = -a["psink"] * delta
                dqn = lax.dot_general(ds, a["k2b"], (((1,), (0,)), ((), ())), preferred_element_type=F32) * scale
                dkn_ref[:, ksl] = lax.dot_general(ds, a["qs"], (((0,), (0,)), ((), ())), preferred_element_type=F32) * scale
                for g in range(group):
                    h = kv * group + g
                    rows = slice(g * WINDOW, (g + 1) * WINDOW)
                    dqn_ref[:, h * HEAD_DIM:(h + 1) * HEAD_DIM] = dqn[rows]
                    dsink = dsink + jnp.where(head_lane == h, jnp.sum(dsk[rows], axis=0, keepdims=True)[:, :n_q], 0.0)
            dq, dqg = _head_rms_bwd(x["q"], x["rq"], c["qg"][...], dqn_ref[...],
                                    (c["q_heads"][...], c["q_spread"][...]), c["q_fold"][...])
            dk2, dkg = _head_rms_bwd(x["k2"], x["rk"], c["kg"][...], dkn_ref[...], (c["k_ones"][...],), c["k_fold"][...])
            dq_ref[...] = dq.astype(dq_ref.dtype)
            dkv_ref[:, :kvw] = (carry_ref[:, :kvw] + dk2[:WINDOW]).astype(dkv_ref.dtype)
            dkv_ref[:, kvw:] = (carry_ref[:, kvw:] + dv_ref[:WINDOW, :]).astype(dkv_ref.dtype)
            carry_ref[:, :kvw] = dk2[WINDOW:]
            carry_ref[:, kvw:] = dv_ref[WINDOW:, :]
            dqg_ref[...] += dqg
            dkg_ref[...] += dkg
            dsink_ref[...] += dsink

        @pl.when(n == nb)
        def _():
            dkv_ref[...] = carry_ref[...].astype(dkv_ref.dtype)

    in_specs = (specs(nb) + [pl.BlockSpec((WINDOW, aw), lambda n: (jnp.minimum(n, nb - 1), 0))]
                + [whole(consts[k].shape) for k in _ATTN_CONST_ORDER])
    return pl.pallas_call(
        body,
        out_shape=(jax.ShapeDtypeStruct((L, aw), BF16), jax.ShapeDtypeStruct((L, 2 * kvw), BF16),
                   jax.ShapeDtypeStruct((1, HEAD_DIM), F32), jax.ShapeDtypeStruct((1, HEAD_DIM), F32),
                   jax.ShapeDtypeStruct((1, n_q), F32)),
        grid=(nb + 1,),
        in_specs=in_specs,
        out_specs=(pl.BlockSpec((WINDOW, aw), lambda n: (jnp.minimum(n, nb - 1), 0)),
                   pl.BlockSpec((WINDOW, 2 * kvw), lambda n: (jnp.maximum(n - 1, 0), 0)),
                   pl.BlockSpec((1, HEAD_DIM), lambda n: (0, 0)),
                   pl.BlockSpec((1, HEAD_DIM), lambda n: (0, 0)),
                   pl.BlockSpec((1, n_q), lambda n: (0, 0))),
        scratch_shapes=[pltpu.VMEM((WINDOW, 2 * kvw), F32), pltpu.VMEM((WINDOW, aw), F32),
                        pltpu.VMEM((2 * WINDOW, kvw), F32), pltpu.VMEM((2 * WINDOW, kvw), F32)],
        compiler_params=_cparams("arbitrary"),
        name=name,
    )(z, z, z, z, z, do, *[consts[k] for k in _ATTN_CONST_ORDER])


def _cmul(ar, ai, br, bi):
    return ar * br - ai * bi, ar * bi + ai * br


def _time_permutation(tc):
    r = jnp.arange(tc)
    src = (r % 8) * (tc // 8) + r // 8
    p = (src[:, None] == jnp.arange(tc)[None, :]).astype(BF16)
    return p, p.T


def _unpermute(pt, x):
    hi = x.astype(BF16)
    lo = (x - hi.astype(F32)).astype(BF16)
    moved = jnp.dot(pt, jnp.concatenate([hi, lo], axis=1), preferred_element_type=F32)
    return moved[:, :x.shape[1]] + moved[:, x.shape[1]:]


def _segment_scan(xr_ref, xi_ref, ar, ai, cr, ci, ng, reverse):
    n = ar.shape[-1]
    row = lax.broadcasted_iota(jnp.int32, (8, n), 0)
    seeded = 7 if reverse else 0
    a8r = jnp.broadcast_to(ar, (8, n))
    a8i = jnp.broadcast_to(ai, (8, n))
    rows_of = lambda g: pl.ds(pl.multiple_of(((ng - 1 - g) if reverse else g) * 8, 8), 8)

    def recur(g, s):
        rows = rows_of(g)
        sr = a8r * s[0] - a8i * s[1] + xr_ref[rows, :]
        si = a8r * s[1] + a8i * s[0] + xi_ref[rows, :]
        xr_ref[rows, :] = sr
        xi_ref[rows, :] = si
        return sr, si

    fr, fi = lax.fori_loop(0, ng, recur, (jnp.where(row == seeded, cr, 0.0), jnp.where(row == seeded, ci, 0.0)))
    pr, pi = ar, ai
    for _ in range(ng.bit_length() - 1):
        pr, pi = _cmul(pr, pi, pr, pi)
    for k in (1, 2, 4):
        keep = (row < 8 - k) if reverse else (row >= k)
        shift = (8 - k) if reverse else k
        mr, mi = jnp.where(keep, pr, 0.0), jnp.where(keep, pi, 0.0)
        tr, ti = pltpu.roll(fr, shift, 0), pltpu.roll(fi, shift, 0)
        fr, fi = fr + mr * tr - mi * ti, fi + mr * ti + mi * tr
        pr, pi = _cmul(pr, pi, pr, pi)
    shift = 7 if reverse else 1
    before_r, before_i = pltpu.roll(fr, shift, 0), pltpu.roll(fi, shift, 0)

    def inherit(g, d):
        rows = rows_of(g)
        dr = a8r * d[0] - a8i * d[1]
        di = a8r * d[1] + a8i * d[0]
        xr_ref[rows, :] = xr_ref[rows, :] + dr
        xi_ref[rows, :] = xi_ref[rows, :] + di
        return dr, di

    lax.fori_loop(0, ng, inherit, (jnp.where(row == seeded, 0.0, before_r), jnp.where(row == seeded, 0.0, before_i)))
    out = 0 if reverse else 7
    return (fr[out:out + 1], fi[out:out + 1],
            jnp.where(row == seeded, cr, before_r), jnp.where(row == seeded, ci, before_i))


def _blockdiag(x):
    g, a, b = x.shape
    j = g // SSM_LANE_GROUPS
    eye = jnp.eye(SSM_LANE_GROUPS, dtype=x.dtype)
    y = x.reshape(j, SSM_LANE_GROUPS, a, 1, b) * eye[None, :, None, :, None]
    return y.reshape(j, SSM_LANE_GROUPS * a, SSM_LANE_GROUPS * b)


def _blockdiag_extract(y, a, b):
    j = y.shape[0]
    y = y.reshape(j, SSM_LANE_GROUPS, a, SSM_LANE_GROUPS, b)
    return jnp.einsum("jgahb,gh->jgab", y, jnp.eye(SSM_LANE_GROUPS, dtype=y.dtype)).reshape(j * SSM_LANE_GROUPS, a, b)


def _ssm_disc(lr, li, ldt, brt, bit):
    dt = jnp.exp(ldt)
    mag = jnp.exp(lr * dt)
    ar = mag * jnp.cos(li * dt)
    ai = mag * jnp.sin(li * dt)
    den = lr * lr + li * li
    fr = ((ar - 1.0) * lr + ai * li) / den
    fi = (ai * lr - (ar - 1.0) * li) / den
    bbr = fr[:, None, :] * brt - fi[:, None, :] * bit
    bbi = fr[:, None, :] * bit + fi[:, None, :] * brt
    return ar, ai, bbr, bbi


def _ssm_prep(lr, li, ldt, brt, bit, *, name):
    g, h, p = brt.shape

    def body(lr_ref, li_ref, ldt_ref, brt_ref, bit_ref, ar_ref, ai_ref, bbr_ref, bbi_ref):
        ar, ai, bbr, bbi = _ssm_disc(lr_ref[...], li_ref[...], ldt_ref[...], brt_ref[...], bit_ref[...])
        ar_ref[...] = ar
        ai_ref[...] = ai
        bbr_ref[...] = bbr
        bbi_ref[...] = bbi

    gp = jax.ShapeDtypeStruct((g, p), F32)
    ghp = jax.ShapeDtypeStruct((g, h, p), F32)
    return pl.pallas_call(body, out_shape=(gp, gp, ghp, ghp), name=name)(lr, li, ldt, brt, bit)


def _ssm_prep_bwd(lr, li, ldt, brt, bit, dar, dai, dbbr, dbbi, *, name):
    g, h, p = brt.shape

    def body(lr_ref, li_ref, ldt_ref, brt_ref, bit_ref, dar_ref, dai_ref, dbbr_ref, dbbi_ref,
             dlr_ref, dli_ref, dldt_ref, dbrt_ref, dbit_ref):
        _, vjp = jax.vjp(_ssm_disc, lr_ref[...], li_ref[...], ldt_ref[...], brt_ref[...], bit_ref[...])
        dlr, dli, dldt, dbrt, dbit = vjp((dar_ref[...], dai_ref[...], dbbr_ref[...], dbbi_ref[...]))
        dlr_ref[...] = dlr
        dli_ref[...] = dli
        dldt_ref[...] = dldt
        dbrt_ref[...] = dbrt
        dbit_ref[...] = dbit

    gp = jax.ShapeDtypeStruct((g, p), F32)
    ghp = jax.ShapeDtypeStruct((g, h, p), F32)
    return pl.pallas_call(body, out_shape=(gp, gp, jax.ShapeDtypeStruct((g, 1), F32), ghp, ghp), name=name)(
        lr, li, ldt, brt, bit, dar, dai, dbbr, dbbi)


def _ssm_specs(tc, nlanes, nch, u_colblk, chunk_of):
    return [
        pl.BlockSpec((tc, nch), lambda j, c: (chunk_of(c), u_colblk + j)),
        pl.BlockSpec((1, nlanes), lambda j, c: (0, j)),
        pl.BlockSpec((1, nlanes), lambda j, c: (0, j)),
        pl.BlockSpec((None, nch, nlanes), lambda j, c: (j, 0, 0)),
        pl.BlockSpec((None, nch, nlanes), lambda j, c: (j, 0, 0)),
        pl.BlockSpec((None, nlanes, nch), lambda j, c: (j, 0, 0)),
        pl.BlockSpec((None, nlanes, nch), lambda j, c: (j, 0, 0)),
        pl.BlockSpec((1, nch), lambda j, c: (0, j)),
        pl.BlockSpec((tc, tc), lambda j, c: (0, 0)),
        pl.BlockSpec((tc, tc), lambda j, c: (0, 0)),
    ]


def _ssm_fwd(z, ar, ai, bblk_r, bblk_i, cblk_r, cblk_i, d, *, u_col, tc, name, deps=()):
    L = z.shape[0]
    nj, nch, nlanes = bblk_r.shape
    w = nj * nch
    nc = L // tc
    ng = tc // 8

    assert ng & (ng - 1) == 0
    perm, perm_t = _time_permutation(tc)

    def body(u_ref, ar_ref, ai_ref, br_ref, bi_ref, cr_ref, ci_ref, d_ref, p_ref, pt_ref, *rest):
        y_ref, s0r_ref, s0i_ref, xr_ref, xi_ref, carr_ref, cari_ref = rest[len(deps):]
        c = pl.program_id(1)

        @pl.when(c == 0)
        def _():
            carr_ref[...] = jnp.zeros_like(carr_ref)
            cari_ref[...] = jnp.zeros_like(cari_ref)

        s0r_ref[...] = carr_ref[...]
        s0i_ref[...] = cari_ref[...]
        u = u_ref[...]
        ub = jnp.dot(p_ref[...], u.astype(BF16), preferred_element_type=F32).astype(BF16)
        xr_ref[...] = jnp.dot(ub, br_ref[...].astype(BF16), preferred_element_type=F32)
        xi_ref[...] = jnp.dot(ub, bi_ref[...].astype(BF16), preferred_element_type=F32)
        cr, ci, _, _ = _segment_scan(xr_ref, xi_ref, ar_ref[...], ai_ref[...], carr_ref[...], cari_ref[...], ng, False)
        carr_ref[...] = cr
        cari_ref[...] = ci
        y = (jnp.dot(xr_ref[...].astype(BF16), cr_ref[...].astype(BF16), preferred_element_type=F32)
             - jnp.dot(xi_ref[...].astype(BF16), ci_ref[...].astype(BF16), preferred_element_type=F32))
        y_ref[...] = _unpermute(pt_ref[...], y) + d_ref[...] * u

    state = jax.ShapeDtypeStruct((nc, 1, nj * nlanes), F32)
    state_spec = pl.BlockSpec((None, 1, nlanes), lambda j, c: (c, 0, j))
    return pl.pallas_call(
        body,
        out_shape=(jax.ShapeDtypeStruct((L, w), F32), state, state),
        grid=(nj, nc),
        in_specs=_ssm_specs(tc, nlanes, nch, u_col // nch, lambda c: c) + [pl.BlockSpec(memory_space=pl.ANY)] * len(deps),
        out_specs=(pl.BlockSpec((tc, nch), lambda j, c: (c, j)), state_spec, state_spec),
        scratch_shapes=[pltpu.VMEM((tc, nlanes), F32), pltpu.VMEM((tc, nlanes), F32),
                        pltpu.VMEM((1, nlanes), F32), pltpu.VMEM((1, nlanes), F32)],
        compiler_params=_cparams("parallel", "arbitrary"),
        name=name,
    )(z, ar, ai, bblk_r, bblk_i, cblk_r, cblk_i, d, perm, perm_t, *deps)


def _ssm_bwd(z, dy, s0r, s0i, ar, ai, bblk_r, bblk_i, cblk_r, cblk_i, d, *, u_col, tc, name):
    L = z.shape[0]
    nj, nch, nlanes = bblk_r.shape
    w = nj * nch
    nc = L // tc
    ng = tc // 8
    chunk_of = lambda c: nc - 1 - c
    assert ng & (ng - 1) == 0
    perm, perm_t = _time_permutation(tc)

    def body(u_ref, ar_ref, ai_ref, br_ref, bi_ref, cr_ref, ci_ref, d_ref, p_ref, pt_ref, dy_ref, s0r_ref, s0i_ref,
             du_ref, dbr_ref, dbi_ref, dcr_ref, dci_ref, dar_ref, dai_ref, dd_ref,
             sr_ref, si_ref, lr_ref, li_ref, carr_ref, cari_ref):
        c = pl.program_id(1)

        @pl.when(c == 0)
        def _():
            for ref in (dbr_ref, dbi_ref, dcr_ref, dci_ref, dar_ref, dai_ref, dd_ref, carr_ref, cari_ref):
                ref[...] = jnp.zeros_like(ref)

        u = u_ref[...]
        dyv = dy_ref[...]
        both = jnp.dot(p_ref[...], jnp.concatenate([u.astype(BF16), dyv.astype(BF16)], axis=1), preferred_element_type=F32)
        ub = both[:, :nch].astype(BF16)
        dyb = both[:, nch:].astype(BF16)
        brb = br_ref[...].astype(BF16)
        bib = bi_ref[...].astype(BF16)
        crb = cr_ref[...].astype(BF16)
        cib = ci_ref[...].astype(BF16)
        a_r, a_i = ar_ref[...], ai_ref[...]

        sr_ref[...] = jnp.dot(ub, brb, preferred_element_type=F32)
        si_ref[...] = jnp.dot(ub, bib, preferred_element_type=F32)
        _, _, start_r, start_i = _segment_scan(sr_ref, si_ref, a_r, a_i, s0r_ref[...], s0i_ref[...], ng, False)

        nt = (((1,), (1,)), ((), ()))
        lr_ref[...] = lax.dot_general(dyb, crb, nt, preferred_element_type=F32)
        li_ref[...] = -lax.dot_general(dyb, cib, nt, preferred_element_type=F32)
        cr, ci, _, _ = _segment_scan(lr_ref, li_ref, a_r, -a_i, carr_ref[...], cari_ref[...], ng, True)
        carr_ref[...] = cr
        cari_ref[...] = ci

        def accumulate(g, carry):
            pr, pi, acc_r, acc_i = carry
            rows = pl.ds(pl.multiple_of(g * 8, 8), 8)
            lr, li = lr_ref[rows, :], li_ref[rows, :]
            return sr_ref[rows, :], si_ref[rows, :], acc_r + lr * pr + li * pi, acc_i + li * pr - lr * pi

        zero8 = jnp.zeros((8, nlanes), F32)
        _, _, acc_r, acc_i = lax.fori_loop(0, ng, accumulate, (start_r, start_i, zero8, zero8))
        dar_ref[...] += jnp.sum(acc_r, axis=0, keepdims=True)
        dai_ref[...] += jnp.sum(acc_i, axis=0, keepdims=True)

        tn = (((0,), (0,)), ((), ()))
        lrb = lr_ref[...].astype(BF16)
        lib = li_ref[...].astype(BF16)
        dcr_ref[...] += lax.dot_general(dyb, sr_ref[...].astype(BF16), tn, preferred_element_type=F32)
        dci_ref[...] -= lax.dot_general(dyb, si_ref[...].astype(BF16), tn, preferred_element_type=F32)
        dbr_ref[...] += lax.dot_general(ub, lrb, tn, preferred_element_type=F32)
        dbi_ref[...] += lax.dot_general(ub, lib, tn, preferred_element_type=F32)
        du = (lax.dot_general(lrb, brb, nt, preferred_element_type=F32)
              + lax.dot_general(lib, bib, nt, preferred_element_type=F32))
        du_ref[...] = (_unpermute(pt_ref[...], du) + d_ref[...] * dyv).astype(du_ref.dtype)
        dd_ref[...] += jnp.sum(dyv * u, axis=0, keepdims=True)

    state_spec = pl.BlockSpec((None, 1, nlanes), lambda j, c: (chunk_of(c), 0, j))
    bshape = jax.ShapeDtypeStruct((nj, nch, nlanes), F32)
    ashape = jax.ShapeDtypeStruct((1, nj * nlanes), F32)
    bspec = pl.BlockSpec((None, nch, nlanes), lambda j, c: (j, 0, 0))
    aspec = pl.BlockSpec((1, nlanes), lambda j, c: (0, j))
    big = pltpu.VMEM((tc, nlanes), F32)
    return pl.pallas_call(
        body,
        out_shape=(jax.ShapeDtypeStruct((L, w), BF16), bshape, bshape, bshape, bshape, ashape, ashape,
                   jax.ShapeDtypeStruct((1, w), F32)),
        grid=(nj, nc),
        in_specs=_ssm_specs(tc, nlanes, nch, u_col // nch, chunk_of)
        + [pl.BlockSpec((tc, nch), lambda j, c: (chunk_of(c), j)), state_spec, state_spec],
        out_specs=(pl.BlockSpec((tc, nch), lambda j, c: (chunk_of(c), j)), bspec, bspec, bspec, bspec, aspec, aspec,
                   pl.BlockSpec((1, nch), lambda j, c: (0, j))),
        scratch_shapes=[big, big, big, big, pltpu.VMEM((1, nlanes), F32), pltpu.VMEM((1, nlanes), F32)],
        compiler_params=_cparams("parallel", "arbitrary"),
        name=name,
    )(z, ar, ai, bblk_r, bblk_i, cblk_r, cblk_i, d, perm, perm_t, dy, s0r, s0i)


def _rmsnorm_rows(x, g):
    return x * lax.rsqrt(jnp.mean(x * x, axis=-1, keepdims=True) + RMS_EPS) * g


def _glu_out(y_raw, pre, b):
    yg = jax.nn.gelu(y_raw)
    return yg * jax.nn.sigmoid(pre + b)


def _gate_merge(za, zs, ba, bs, a, bm):
    return jax.nn.sigmoid(za + ba) * a + jax.nn.sigmoid(zs + bs) * bm


def _swiglu(g, u):
    return jax.nn.silu(g) * u


def _row_tile(width_bytes_per_row, rows):
    budget = VMEM_LIMIT_BYTES // 3
    t = max(8, min(1024, budget // (2 * max(width_bytes_per_row, 1))))
    return _pick(rows, t, 16)


def _ssm_params(p, prefix):
    g, pst = p["lam_re"].shape
    ar, ai, bbr, bbi = _ssm_prep(p["lam_re"], p["lam_im"], p["log_dt"], p["b_re_t"], p["b_im_t"], name=prefix + "_ssm_prep")
    return dict(ar=ar.reshape(1, g * pst), ai=ai.reshape(1, g * pst),
                bblk_r=_blockdiag(bbr), bblk_i=_blockdiag(bbi),
                cblk_r=_blockdiag(jnp.swapaxes(p["c_re"], 1, 2)), cblk_i=_blockdiag(jnp.swapaxes(p["c_im"], 1, 2)))


def _layer_fwd(x, p, dims, prefix, deps=(), tick=None):
    tick = tick or (lambda point, arr: ())
    t, d = x.shape
    aw, kvw, sw, ff = dims["aw"], dims["kvw"], dims["sw"], dims["ff"]
    off_u = aw + 2 * kvw
    off_g = off_u + sw
    gblk = _pick(d, 512, 128)
    assert off_g % gblk == 0 and off_u % (SSM_LANE_GROUPS * SSM_GROUP_CH) == 0
    sv = {"x": x}

    h, = _rowmap(_rmsnorm_rows, [(x, "row", d, 0), (p["norm_mix_g"], "vec", d, 0)], [(d, BF16, "row", d)],
                 rows=t, tm=_row_tile(6 * d, t), name=prefix + "_norm_mix", deps=deps)
    deps = tick("norm", h)
    z = _mm_nn(h, p["w_in"], out_dtype=F32, tm=1024, tn=1664, tk=2048, name=prefix + "_mm_in", deps=deps)
    ya = _attn_fwd(z, p["q_norm_g"], p["k_norm_g"], p["attn_sinks"], n_q=dims["n_q"], n_kv=dims["n_kv"],
                   name=prefix + "_attn_fwd", deps=tick("in", z))
    sp = _ssm_params(p, prefix)
    y_raw, s0r, s0i = _ssm_fwd(z, sp["ar"], sp["ai"], sp["bblk_r"], sp["bblk_i"], sp["cblk_r"], sp["cblk_i"], p["ssm_d"],
                               u_col=off_u, tc=dims["tc"], name=prefix + "_ssm_fwd", deps=tick("attn", ya))
    yg, = _rowmap(jax.nn.gelu, [(y_raw, "row", sw, 0)], [(sw, BF16, "row", sw)],
                  rows=t, tm=_row_tile(6 * sw, t), name=prefix + "_gelu", deps=tick("ssm", y_raw))
    pre = _mm_nn(yg, p["ssm_glu_w"], out_dtype=F32, tm=1024, tn=1024, tk=1024, name=prefix + "_mm_glu")
    y2, = _rowmap(_glu_out, [(y_raw, "row", sw, 0), (pre, "row", sw, 0), (p["ssm_glu_b"], "vec", sw, 0)],
                  [(sw, BF16, "row", sw)], rows=t, tm=_row_tile(10 * sw, t), name=prefix + "_glu_out")
    a = _mm_nn(ya, p["w_attn_branch"], out_dtype=BF16, tm=1024, tn=512, tk=1024, name=prefix + "_mm_ab")
    bm = _mm_nn(y2, p["w_ssm_branch"], out_dtype=BF16, tm=1024, tn=512, tk=1024, name=prefix + "_mm_sb")
    ngb = d // gblk
    merged, = _rowmap(
        _gate_merge,
        [(z, "row", gblk, off_g // gblk), (z, "row", gblk, off_g // gblk + ngb),
         (p["gate_bias"], "vec", gblk, 0), (p["gate_bias"], "vec", gblk, ngb),
         (a, "row", gblk, 0), (bm, "row", gblk, 0)],
        [(d, BF16, "row", gblk)], rows=t, tm=_row_tile(18 * gblk, t), ncol=ngb, name=prefix + "_gate")
    x1 = _mm_nn(merged, p["w_out"], out_dtype=F32, tm=1024, tn=1024, tk=2048, name=prefix + "_mm_out", add=x)
    h2, = _rowmap(_rmsnorm_rows, [(x1, "row", d, 0), (p["norm_ffn_g"], "vec", d, 0)], [(d, BF16, "row", d)],
                  rows=t, tm=_row_tile(6 * d, t), name=prefix + "_norm_ffn", deps=tick("out", x1))
    gu = _mm_nn(h2, p["w_ffn_in"], out_dtype=BF16, tm=1024, tn=1408, tk=2048, name=prefix + "_mm_ffn_in")
    fblk = _pick(ff, 1408, 128)
    nfb = ff // fblk
    act, = _rowmap(_swiglu, [(gu, "row", fblk, 0), (gu, "row", fblk, nfb)], [(ff, BF16, "row", fblk)],
                   rows=t, tm=_row_tile(10 * fblk, t), ncol=nfb, name=prefix + "_swiglu", deps=tick("ffn_in", gu))
    x2 = _mm_nn(act, p["w_ffn_out"], out_dtype=F32, tm=1024, tn=512, tk=5632, name=prefix + "_mm_ffn_out", add=x1,
                deps=tick("act", act))
    sv.update(h=h, z=z, ya=ya, sp=sp, y_raw=y_raw, s0r=s0r, s0i=s0i, yg=yg, pre=pre, y2=y2, a=a, bm=bm,
              merged=merged, x1=x1, h2=h2, gu=gu, act=act)
    return x2, sv


def _layer_bwd(dx2, dx2b, sv, p, dims, prefix, gbuf, deps=(), before_mixer=None, before_in=None):
    t, d = dx2.shape
    aw, kvw, sw, ff = dims["aw"], dims["kvw"], dims["sw"], dims["ff"]
    off_u = aw + 2 * kvw
    off_g = off_u + sw
    gblk = _pick(d, 512, 128)
    ngb = d // gblk
    fblk = _pick(ff, 1408, 128)
    nfb = ff // fblk
    g = {}

    dact = _mm_nt(dx2b, p["w_ffn_out"], out_dtype=BF16, tm=1024, tn=2048, tko=1408, name=prefix + "_mm_dact", deps=deps)
    g["w_ffn_out"] = _mm_tn(sv["act"], dx2b, p["w_ffn_out"], into=gbuf.get("w_ffn_out"), tm=4096, tn=1024, tko=512,
                            name=prefix + "_mm_dw_ffn_out")

    def swiglu_bwd(gg, uu, da):
        s = jax.nn.sigmoid(gg)
        gs = gg * s
        return jnp.concatenate([da * uu * (s + gs * (1.0 - s)), da * gs], axis=1)

    dgu, = _rowmap(swiglu_bwd, [(sv["gu"], "row", fblk, 0), (sv["gu"], "row", fblk, nfb), (dact, "row", fblk, 0)],
                   [(2 * ff, BF16, "row", 2 * fblk)],
                   rows=t, tm=_row_tile(16 * fblk, t), ncol=nfb, name=prefix + "_swiglu_bwd")
    dh2 = _mm_nt(dgu, p["w_ffn_in"], out_dtype=F32, tm=512, tn=1408, tko=512, name=prefix + "_mm_dh2", interleaved=fblk)
    g["w_ffn_in"] = _mm_tn(sv["h2"], dgu, p["w_ffn_in"], into=gbuf.get("w_ffn_in"), tm=4096, tn=fblk, tko=512,
                           name=prefix + "_mm_dw_ffn_in", interleaved=True)

    def norm_bwd(xx, gg, dh, dres):
        _, vjp = jax.vjp(_rmsnorm_rows, xx, gg)
        dxx, dgg = vjp(dh)
        dxx = dxx + dres
        return dxx, dxx, dgg

    dx1, dx1b, g["norm_ffn_g"] = _rowmap(
        norm_bwd, [(sv["x1"], "row", d, 0), (p["norm_ffn_g"], "vec", d, 0), (dh2, "row", d, 0), (dx2, "row", d, 0)],
        [(d, F32, "row", d), (d, BF16, "row", d), (d, F32, "acc", d)],
        rows=t, tm=_row_tile(22 * d, t), name=prefix + "_norm_ffn_bwd")

    deps = before_mixer(dx1, g) if before_mixer else ()
    dmerged = _mm_nt(dx1b, p["w_out"], out_dtype=BF16, tm=1024, tn=2048, tko=1024, name=prefix + "_mm_dmerged", deps=deps)
    g["w_out"] = _mm_tn(sv["merged"], dx1b, p["w_out"], into=gbuf.get("w_out"), tm=4096, tn=1024, tko=512,
                        name=prefix + "_mm_dw_out")

    def gate_bwd(za, zs, ba, bs, aa, bb, dm):
        sa = jax.nn.sigmoid(za + ba)
        ss = jax.nn.sigmoid(zs + bs)
        daa, dbb = dm * sa, dm * ss
        dza, dzs = daa * aa * (1.0 - sa), dbb * bb * (1.0 - ss)
        return daa, dbb, dza, dzs, jnp.sum(dza, axis=0, keepdims=True), jnp.sum(dzs, axis=0, keepdims=True)

    z = sv["z"]
    da, dbm, dza, dzs, dba, dbs = _rowmap(
        gate_bwd,
        [(z, "row", gblk, off_g // gblk), (z, "row", gblk, off_g // gblk + ngb),
         (p["gate_bias"], "vec", gblk, 0), (p["gate_bias"], "vec", gblk, ngb),
         (sv["a"], "row", gblk, 0), (sv["bm"], "row", gblk, 0), (dmerged, "row", gblk, 0)],
        [(d, BF16, "row", gblk), (d, BF16, "row", gblk), (d, BF16, "row", gblk), (d, BF16, "row", gblk),
         (d, F32, "acc", gblk), (d, F32, "acc", gblk)],
        rows=t, tm=_row_tile(32 * gblk, t), ncol=ngb, name=prefix + "_gate_bwd")
    g["gate_bias"] = jnp.concatenate([dba, dbs], axis=1)
    dya = _mm_nt(da, p["w_attn_branch"], out_dtype=BF16, tm=1024, tn=512, tko=1024, name=prefix + "_mm_dya")
    g["w_attn_branch"] = _mm_tn(sv["ya"], da, p["w_attn_branch"], into=gbuf.get("w_attn_branch"), tm=4096, tn=512,
                                tko=512, name=prefix + "_mm_dw_ab")
    dy2 = _mm_nt(dbm, p["w_ssm_branch"], out_dtype=BF16, tm=1024, tn=512, tko=1024, name=prefix + "_mm_dy2")
    g["w_ssm_branch"] = _mm_tn(sv["y2"], dbm, p["w_ssm_branch"], into=gbuf.get("w_ssm_branch"), tm=4096, tn=512,
                               tko=512, name=prefix + "_mm_dw_sb")

    def glu_bwd(y_raw, pre, b, dy):
        yg = jax.nn.gelu(y_raw)
        _, vjp = jax.vjp(lambda a_, b_, c_: a_ * jax.nn.sigmoid(b_ + c_), yg, pre, b)
        dyg, dpre, db = vjp(dy)
        return dyg, dpre, db

    dyg_direct, dpre, g["ssm_glu_b"] = _rowmap(
        glu_bwd, [(sv["y_raw"], "row", sw, 0), (sv["pre"], "row", sw, 0), (p["ssm_glu_b"], "vec", sw, 0), (dy2, "row", sw, 0)],
        [(sw, F32, "row", sw), (sw, BF16, "row", sw), (sw, F32, "acc", sw)],
        rows=t, tm=_row_tile(24 * sw, t), name=prefix + "_glu_bwd")
    dyg2 = _mm_nt(dpre, p["ssm_glu_w"], out_dtype=F32, tm=1024, tn=1024, tko=1024, name=prefix + "_mm_dyg")
    g["ssm_glu_w"] = _mm_tn(sv["yg"], dpre, p["ssm_glu_w"], into=gbuf.get("ssm_glu_w"), tm=4096, tn=1024, tko=512,
                            name=prefix + "_mm_dw_glu")

    def gelu_bwd(y_raw, d1, d2):
        _, vjp = jax.vjp(jax.nn.gelu, y_raw)
        return vjp(d1 + d2)[0]

    dy_raw, = _rowmap(gelu_bwd, [(sv["y_raw"], "row", sw, 0), (dyg_direct, "row", sw, 0), (dyg2, "row", sw, 0)],
                      [(sw, F32, "row", sw)], rows=t, tm=_row_tile(20 * sw, t), name=prefix + "_gelu_bwd")
    sp = sv["sp"]
    du, dbr, dbi, dcr, dci, dar, dai, g["ssm_d"] = _ssm_bwd(
        z, dy_raw, sv["s0r"], sv["s0i"], sp["ar"], sp["ai"], sp["bblk_r"], sp["bblk_i"], sp["cblk_r"], sp["cblk_i"],
        p["ssm_d"], u_col=off_u, tc=dims["tc"], name=prefix + "_ssm_bwd")
    ngr, pst = p["lam_re"].shape
    hch = SSM_GROUP_CH
    dlr, dli, dldt, dbrt, dbit = _ssm_prep_bwd(
        p["lam_re"], p["lam_im"], p["log_dt"], p["b_re_t"], p["b_im_t"],
        dar.reshape(ngr, pst), dai.reshape(ngr, pst), _blockdiag_extract(dbr, hch, pst), _blockdiag_extract(dbi, hch, pst),
        name=prefix + "_ssm_prep_bwd")
    g.update(ssm_lambda_re=dlr, ssm_lambda_im=dli, ssm_log_dt=dldt.reshape(ngr),
             ssm_b_re=jnp.swapaxes(dbrt, 1, 2), ssm_b_im=jnp.swapaxes(dbit, 1, 2),
             ssm_c_re=_blockdiag_extract(dcr, hch, pst), ssm_c_im=_blockdiag_extract(dci, hch, pst))

    dq, dkv, g["q_norm_g"], g["k_norm_g"], g["attn_sinks"] = _attn_bwd(
        z, dya, p["q_norm_g"], p["k_norm_g"], p["attn_sinks"], n_q=dims["n_q"], n_kv=dims["n_kv"], name=prefix + "_attn_bwd")

    dz = jnp.concatenate([dq, dkv, du, dza, dzs], axis=1)
    deps = before_in(dq, g) if before_in else ()
    dh = _mm_nt(dz, p["w_in"], out_dtype=F32, tm=1024, tn=1664, tko=512, name=prefix + "_mm_dh", deps=deps)
    g["w_in"] = _mm_tn(sv["h"], dz, p["w_in"], into=gbuf.get("w_in"), tm=4096, tn=1664, tko=512,
                       name=prefix + "_mm_dw_in")
    dx, dxb, g["norm_mix_g"] = _rowmap(
        norm_bwd, [(sv["x"], "row", d, 0), (p["norm_mix_g"], "vec", d, 0), (dh, "row", d, 0), (dx1, "row", d, 0)],
        [(d, F32, "row", d), (d, BF16, "row", d), (d, F32, "acc", d)],
        rows=t, tm=_row_tile(22 * d, t), name=prefix + "_norm_mix_bwd")
    return dx, dxb, g


def _loss_and_grad(y, target):
    t, d = y.shape

    def fn(yy, tt):
        e = yy - tt
        dy = e * (1.0 / d)
        return dy, dy, jnp.sum(e * e, keepdims=True).reshape(1, 1)

    dy, dyb, sq = _rowmap(fn, [(y, "row", d, 0), (target, "row", d, 0)],
                          [(d, F32, "row", d), (d, BF16, "row", d), (1, F32, "acc", 1)],
                          rows=t, tm=_row_tile(14 * d, t), name="loss")
    return sq, dy, dyb


def _local_step(x, target, n_layers, weights_of, dims, fwd_tick=None, bwd_tick=None):
    saved, params = [], []
    h = x
    for l in range(n_layers):
        p, deps = weights_of(l, h)
        params.append(p)
        tick = (lambda point, arr, l=l: fwd_tick(l, point, arr)) if fwd_tick else None
        h, sv = _layer_fwd(h, p, dims, "l%d" % l, deps, tick)
        saved.append(sv)
    sq, dy, dyb = _loss_and_grad(h, target)
    grads = [None] * n_layers
    deps = ()
    for l in reversed(range(n_layers)):
        if bwd_tick:
            ffn_done = lambda arr, g, l=l: bwd_tick(l, "ffn", (arr, g["w_ffn_in"], g["w_ffn_out"]),
                                                    {n: g[n] for n in ("w_ffn_in", "w_ffn_out")})
            mixer_done = lambda arr, g, l=l: bwd_tick(l, "mixer", (arr, g["ssm_d"]) + tuple(g[n] for n in MIXER_WEIGHTS),
                                                      {n: g[n] for n in MIXER_WEIGHTS})
        else:
            ffn_done = mixer_done = None
        dy, dyb, grads[l] = _layer_bwd(dy, dyb, saved[l], params[l], dims, "l%d" % l, {}, deps=deps,
                                       before_mixer=ffn_done, before_in=mixer_done)
        if bwd_tick:
            deps = bwd_tick(l, "in", (dy, grads[l]["w_in"]), {"w_in": grads[l]["w_in"]})
    return sq, dy, grads, deps


COL_SHARDED = ("w_in", "w_attn_branch", "w_ssm_branch", "w_ffn_in")
ROW_SHARDED = ("ssm_glu_w", "w_out", "w_ffn_out")
BIG_WEIGHTS = COL_SHARDED + ROW_SHARDED
WEIGHT_NAMES = ("norm_mix_g", "w_in", "gate_bias", "q_norm_g", "k_norm_g", "attn_sinks", "ssm_lambda_re",
                "ssm_lambda_im", "ssm_log_dt", "ssm_b_re", "ssm_b_im", "ssm_c_re", "ssm_c_im", "ssm_d", "ssm_glu_w",
                "ssm_glu_b", "w_attn_branch", "w_ssm_branch", "w_out", "norm_ffn_g", "w_ffn_in", "w_ffn_out")
SMALL_WEIGHTS = tuple(n for n in WEIGHT_NAMES if n not in BIG_WEIGHTS)
MIXER_WEIGHTS = ("w_out", "w_attn_branch", "w_ssm_branch", "ssm_glu_w")
WEIGHT_GROUPS = {"in": ("w_in",), "mixer": MIXER_WEIGHTS, "ffn": ("w_ffn_in", "w_ffn_out")}


def _dims(d, shapes, tc):
    s, _, aw, _ = shapes["w_attn_branch"]
    sw = shapes["w_ssm_branch"][2]
    in_w = shapes["w_in"][3] * s
    kvw = (in_w - aw - sw - 2 * d) // 2
    ff = shapes["w_ffn_out"][2] * s
    return dict(aw=aw, kvw=kvw, sw=sw, ff=ff, n_q=aw // HEAD_DIM, n_kv=kvw // HEAD_DIM, tc=tc)


def _big_params(big):
    p = {n: _Weight(a, 0, "col") for n, a in big.items() if n in COL_SHARDED}
    p.update({n: _Weight(a.reshape(1, 1, -1, a.shape[-1]), 0, "col") for n, a in big.items() if n in ROW_SHARDED})
    return p


def _layer_params(l, small):
    p = {}
    for n in ("norm_mix_g", "gate_bias", "q_norm_g", "k_norm_g", "ssm_d", "ssm_glu_b", "norm_ffn_g"):
        p[n] = small[n][l][None]
    p["attn_sinks"] = small["attn_sinks"][l]
    p["lam_re"] = small["ssm_lambda_re"][l]
    p["lam_im"] = small["ssm_lambda_im"][l]
    p["log_dt"] = small["ssm_log_dt"][l][:, None]
    p["b_re_t"] = jnp.swapaxes(small["ssm_b_re"][l], 1, 2)
    p["b_im_t"] = jnp.swapaxes(small["ssm_b_im"][l], 1, 2)
    p["c_re"] = small["ssm_c_re"][l]
    p["c_im"] = small["ssm_c_im"][l]
    return p


_ANY = pl.BlockSpec(memory_space=pl.ANY)
_MESH_ID = pl.DeviceIdType.MESH


def _coords():
    return lax.axis_index("x"), lax.axis_index("y"), lax.axis_index("c")


def _my_chip():
    return (2 * lax.axis_index("x") + lax.axis_index("y")).astype(jnp.int32).reshape(1)


def _my_core():
    return lax.axis_index("c").astype(jnp.int32).reshape(1)


def _cast_into_slot(w, layer, *, name, deps=()):
    _, r, c = w.shape
    tm = _row_tile(12 * c, r)

    def body(me_ref, w_ref, *rest):
        rest[-1][...] = w_ref[...].astype(rest[-1].dtype)

    return pl.pallas_call(
        body,
        out_shape=jax.ShapeDtypeStruct((N_CHIPS, 1, r, c), BF16),
        grid_spec=pltpu.PrefetchScalarGridSpec(
            num_scalar_prefetch=1,
            grid=(r // tm,),
            in_specs=[pl.BlockSpec((None, tm, c), lambda i, me: (layer, i, 0))]
            + [pl.BlockSpec(memory_space=pl.ANY)] * len(deps),
            out_specs=pl.BlockSpec((None, None, tm, c), lambda i, me: (me[0], 0, i, 0)),
        ),
        compiler_params=_cparams("parallel"),
        name=name,
    )(_my_chip(), w, *deps)


_HBM = pl.BlockSpec(memory_space=pltpu.HBM)
_SEM = pl.BlockSpec(memory_space=pltpu.SEMAPHORE)
_DATAFLOW = pltpu.SideEffectType.DATAFLOW_SIDE_EFFECTING


class _SplitExchange:
    def __init__(self, srcs, lands, build, n_copies, name):
        self.build, self.n, self.name = build, n_copies, name
        self.ns, self.nl = len(srcs), len(lands)
        self.bufs = [pltpu.with_memory_space_constraint(a, pltpu.HBM) for a in list(srcs) + list(lands)]

    def _copies(self, refs, send_sems, recv_sems):
        triples = self.build(refs[:self.ns], refs[self.ns:self.ns + self.nl])
        assert len(triples) == self.n
        return [pltpu.make_async_remote_copy(src_ref=s, dst_ref=d, send_sem=send_sems.at[k], recv_sem=recv_sems.at[k],
                                             device_id=to, device_id_type=_MESH_ID) for k, (s, d, to) in enumerate(triples)]

    def start(self, deps=()):
        nb = self.ns + self.nl

        def body(*refs):
            outs = refs[nb + len(deps):]
            for cp in self._copies(refs, outs[0], outs[1]):
                cp.start()
            outs[-1][...] = jnp.zeros_like(outs[-1])

        sems = pltpu.SemaphoreType.DMA((self.n,))
        res = pl.pallas_call(
            body,
            out_shape=(sems, sems, *[pltpu.HBM(b.shape, b.dtype) for b in self.bufs], jax.ShapeDtypeStruct((8, 128), F32)),
            in_specs=[_HBM] * nb + [_ANY] * len(deps),
            out_specs=(_SEM, _SEM, *[_HBM] * nb, pl.BlockSpec(memory_space=pltpu.VMEM)),
            input_output_aliases={i: 2 + i for i in range(nb)},
            compiler_params=pltpu.CompilerParams(has_side_effects=_DATAFLOW),
            name=self.name + "_start",
        )(*self.bufs, *deps)
        self.send_sems, self.recv_sems = res[0], res[1]
        self.bufs = list(res[2:2 + nb])
        return res[-1]

    def wait(self, after=()):
        nb = self.ns + self.nl
        after = tuple(after) if isinstance(after, (tuple, list)) else (after,)

        def body(*refs):
            for cp in self._copies(refs, refs[nb], refs[nb + 1]):
                cp.wait_send()
                cp.wait_recv()

        res = pl.pallas_call(
            body,
            out_shape=tuple(pltpu.HBM(b.shape, b.dtype) for b in self.bufs),
            in_specs=[_HBM] * nb + [_SEM, _SEM] + [_ANY] * len(after),
            out_specs=tuple([_HBM] * nb),
            input_output_aliases={i: i for i in range(nb)},
            compiler_params=pltpu.CompilerParams(has_side_effects=_DATAFLOW),
            name=self.name + "_wait",
        )(*self.bufs, self.send_sems, self.recv_sems, *after)
        res = list(res)
        return res[:self.ns], res[self.ns:]


def _other_chips(x, y):
    return [(1 - x, y), (x, 1 - y), (1 - x, 1 - y)]


def _gather_steps(bufs, tag, deps, publish):
    n = len(bufs)
    half = lambda ref, i, slot, hc: ref.at[slot, :, pl.ds(hc * (bufs[i].shape[2] // 2), bufs[i].shape[2] // 2), :]

    def over_ici(srcs, lands):
        x, y, c = _coords()
        me = 2 * x + y
        return [(half(srcs[i], i, me, c), half(srcs[i], i, me, c), (px, py, c))
                for i in range(n) for px, py in _other_chips(x, y)]

    def to_sibling(srcs, lands):
        x, y, c = _coords()
        return [(half(srcs[i], i, 2 * px + py, c), half(srcs[i], i, 2 * px + py, c), (x, y, 1 - c))
                for i in range(n) for px, py in _other_chips(x, y)]

    ex = _SplitExchange(bufs, [], over_ici, 3 * n, tag + "_ici")
    after = yield ex.start(deps)
    bufs, _ = ex.wait(after)
    ex = _SplitExchange(bufs, [], to_sibling, 3 * n, tag + "_d2d")
    after = yield ex.start()
    bufs, _ = ex.wait(after)
    publish(bufs)


def _reduce_steps(grads, tag, publish):
    n = len(grads)
    rh = [g.shape[1] // 2 for g in grads]
    theirs = [lax.empty((g.shape[0], g.shape[1] // 2, g.shape[2]), F32) for g in grads]

    def halves(srcs, lands):
        x, y, c = _coords()
        return [(srcs[i].at[:, pl.ds((1 - c) * rh[i], rh[i]), :], lands[i], (x, y, 1 - c)) for i in range(n)]

    def chips(srcs, lands):
        x, y, c = _coords()
        me = 2 * x + y
        return [(srcs[i].at[2 * px + py], lands[i].at[me], (px, py, c)) for i in range(n) for px, py in _other_chips(x, y)]

    def sibling(srcs, lands):
        x, y, c = _coords()
        return [(srcs[i], lands[i], (x, y, 1 - c)) for i in range(n)]

    ex = _SplitExchange(grads, theirs, halves, n, tag + "_halves")
    after = yield ex.start()
    grads, theirs = ex.wait(after)
    parts = [_add_own_half(g, t, name="%s_add_own_half_%d" % (tag, i)) for i, (g, t) in enumerate(zip(grads, theirs))]
    ex = _SplitExchange(parts, [lax.empty(p.shape, p.dtype) for p in parts], chips, 3 * n, tag + "_chips")
    after = yield ex.start()
    parts, got = ex.wait(after)
    mine = [_sum_chips(p, g, name="%s_sum_chips_%d" % (tag, i)) for i, (p, g) in enumerate(zip(parts, got))]
    ex = _SplitExchange(mine, [lax.empty(m.shape, m.dtype) for m in mine], sibling, n, tag + "_sibling")
    after = yield ex.start()
    mine, theirs = ex.wait(after)
    publish(list(zip(mine, theirs)))


def _allreduce_steps(buf, tag, publish):
    r, c = buf.shape

    def to_sibling(srcs, lands):
        x, y, cc = _coords()
        return [(srcs[0], lands[0], (x, y, 1 - cc))]

    def over_ici(srcs, lands):
        x, y, cc = _coords()
        me = 2 * x + y
        return [(srcs[0].at[me], srcs[0].at[me], (px, py, cc)) for px, py in _other_chips(x, y)]

    def halves(srcs, lands):
        x, y, cc = _coords()
        return [(srcs[0].at[cc], srcs[0].at[cc], (x, y, 1 - cc))]

    ex = _SplitExchange([buf], [lax.empty(buf.shape, buf.dtype)], to_sibling, 1, tag + "_cores")
    after = yield ex.start()
    (mine,), (theirs,) = ex.wait(after)
    ex = _SplitExchange([_add_half_into_slot(mine, theirs, name=tag + "_chip_sum")], [], over_ici, N_CHIPS - 1, tag + "_chips")
    after = yield ex.start()
    (parts,), _ = ex.wait(after)
    total_half = _sum_slots(parts, name=tag + "_sum_chips")
    ex = _SplitExchange([_place_into_slot(total_half, 2, _my_core(), name=tag + "_place_half")], [], halves, 1, tag + "_halves")
    after = yield ex.start()
    (both,), _ = ex.wait(after)
    publish(both.reshape(r, c))


class _Exchanges:
    def __init__(self):
        self.running = []

    def launch(self, steps):
        self.running.append(steps)
        return next(steps)

    def advance(self, steps, after):
        try:
            return steps.send(after)
        except StopIteration:
            self.running.remove(steps)
            return None

    def advance_all(self, after):
        tokens = [self.advance(steps, after) for steps in list(self.running)]
        return tuple(t for t in tokens if t is not None)


def _add_own_half(g, theirs, *, name):
    s, r, c = g.shape
    rh = r // 2
    tm = _row_tile(10 * c, rh)
    nb = rh // tm

    def body(core_ref, g_ref, t_ref, o_ref):
        o_ref[...] = (g_ref[...] + t_ref[...]).astype(o_ref.dtype)

    return pl.pallas_call(
        body,
        out_shape=jax.ShapeDtypeStruct((s, rh, c), BF16),
        grid_spec=pltpu.PrefetchScalarGridSpec(
            num_scalar_prefetch=1,
            grid=(s, nb),
            in_specs=[pl.BlockSpec((None, tm, c), lambda k, i, core: (k, core[0] * nb + i, 0)),
                      pl.BlockSpec((None, tm, c), lambda k, i, core: (k, i, 0))],
            out_specs=pl.BlockSpec((None, tm, c), lambda k, i, core: (k, i, 0)),
        ),
        compiler_params=_cparams("parallel", "parallel"),
        name=name,
    )(_my_core(), g, theirs)


def _sum_chips(part, got, *, name):
    s, rh, c = part.shape
    tm = _row_tile(14 * c, rh)

    def body(me_ref, p_ref, a_ref, b_ref, c_ref, o_ref):
        o_ref[...] = ((p_ref[...].astype(F32) + a_ref[...].astype(F32)) + b_ref[...].astype(F32)) + c_ref[...].astype(F32)

    slot = lambda k: (lambda i, me: ((me[0] + k) % s, i, 0))
    return pl.pallas_call(
        body,
        out_shape=jax.ShapeDtypeStruct((rh, c), F32),
        grid_spec=pltpu.PrefetchScalarGridSpec(
            num_scalar_prefetch=1,
            grid=(rh // tm,),
            in_specs=[pl.BlockSpec((None, tm, c), slot(k)) for k in range(s)],
            out_specs=pl.BlockSpec((tm, c), lambda i, me: (i, 0)),
        ),
        compiler_params=_cparams("parallel"),
        name=name,
    )(_my_chip(), part, got, got, got)


def _place_into_slot(buf, n_slots, slot, *, name):
    r, c = buf.shape
    tm = _row_tile(8 * c, r)

    def body(slot_ref, i_ref, o_ref):
        o_ref[...] = i_ref[...]

    return pl.pallas_call(
        body,
        out_shape=jax.ShapeDtypeStruct((n_slots, r, c), buf.dtype),
        grid_spec=pltpu.PrefetchScalarGridSpec(
            num_scalar_prefetch=1,
            grid=(r // tm,),
            in_specs=[pl.BlockSpec((tm, c), lambda i, s: (i, 0))],
            out_specs=pl.BlockSpec((None, tm, c), lambda i, s: (s[0], i, 0)),
        ),
        compiler_params=_cparams("parallel"),
        name=name,
    )(slot, buf)


def _add_half_into_slot(mine, theirs, *, name):
    r, c = mine.shape
    rh = r // 2
    tm = _row_tile(12 * c, rh)
    nb = rh // tm
    where = jnp.concatenate([_my_chip(), _my_core()])

    def body(where_ref, a_ref, b_ref, o_ref):
        o_ref[...] = a_ref[...] + b_ref[...]

    half = pl.BlockSpec((tm, c), lambda i, w: (w[1] * nb + i, 0))
    return pl.pallas_call(
        body,
        out_shape=jax.ShapeDtypeStruct((N_CHIPS, rh, c), mine.dtype),
        grid_spec=pltpu.PrefetchScalarGridSpec(
            num_scalar_prefetch=1,
            grid=(nb,),
            in_specs=[half, half],
            out_specs=pl.BlockSpec((None, tm, c), lambda i, w: (w[0], i, 0)),
        ),
        compiler_params=_cparams("parallel"),
        name=name,
    )(where, mine, theirs)


def _sum_slots(arr, *, name):
    s, r, c = arr.shape
    tm = _row_tile(4 * c * (s + 1), r)

    def body(*refs):
        acc = refs[0][...]
        for ref in refs[1:s]:
            acc = acc + ref[...]
        refs[s][...] = acc

    return pl.pallas_call(
        body,
        out_shape=jax.ShapeDtypeStruct((r, c), arr.dtype),
        grid=(r // tm,),
        in_specs=[pl.BlockSpec((None, tm, c), lambda i, k=k: (k, i, 0)) for k in range(s)],
        out_specs=pl.BlockSpec((tm, c), lambda i: (i, 0)),
        compiler_params=_cparams("parallel"),
        name=name,
    )(*([arr] * s))


def _adamw_fn(w, g, m, v):
    m = ADAM_B1 * m + (1.0 - ADAM_B1) * g
    v = ADAM_B2 * v + (1.0 - ADAM_B2) * jnp.square(g)
    m_hat = m / (1.0 - ADAM_B1 ** ADAM_STEP)
    v_hat = v / (1.0 - ADAM_B2 ** ADAM_STEP)
    delta = -ADAM_LR * (m_hat / (jnp.sqrt(v_hat) + ADAM_EPS) + ADAM_WD * w)
    return delta, m, v


def _adamw(w, g, m, v, *, name):
    rows, cols = w.shape
    ins = [(a, "row", cols, 0) for a in (w, g, m, v)]
    outs = [(cols, F32, "row", cols)] * 3
    return _rowmap(_adamw_fn, ins, outs, rows=rows, tm=_row_tile(56 * cols, rows), name=name)


def _adamw_sharded(w, m, v, g_mine, g_sibling, layer, into, *, name, deps=()):
    nl, r, c = w.shape
    rh = r // 2
    tm = _row_tile(40 * c, rh)
    nb = rh // tm
    n_into = 0 if into is None else 4

    def body(core_ref, w_ref, m_ref, v_ref, a_ref, b_ref, *rest):
        g_ref, d_ref, nm_ref, nv_ref = rest[n_into + len(deps):]
        g = jnp.where(pl.program_id(0) == core_ref[0], a_ref[...], b_ref[...])
        delta, nm, nv = _adamw_fn(w_ref[...], g, m_ref[...], v_ref[...])
        g_ref[...] = g
        d_ref[...] = delta
        nm_ref[...] = nm
        nv_ref[...] = nv

    whole = pl.BlockSpec((None, tm, c), lambda h, i, core: (layer, h * nb + i, 0))
    half = pl.BlockSpec((tm, c), lambda h, i, core: (i, 0))
    shape = jax.ShapeDtypeStruct((nl, r, c), F32)
    return pl.pallas_call(
        body,
        out_shape=(shape, shape, shape, shape),
        grid_spec=pltpu.PrefetchScalarGridSpec(
            num_scalar_prefetch=1,
            grid=(2, nb),
            in_specs=[whole, whole, whole, half, half] + [pl.BlockSpec(memory_space=pl.ANY)] * (n_into + len(deps)),
            out_specs=(whole, whole, whole, whole),
        ),
        input_output_aliases={6 + k: k for k in range(n_into)},
        compiler_params=_cparams("parallel", "parallel"),
        name=name,
    )(_my_core(), w, m, v, g_mine, g_sibling, *(into or ()), *deps)


def _pack(arrays):
    flat = jnp.concatenate([a.reshape(-1) for a in arrays])
    pad = (-flat.shape[0]) % (256 * 128)
    return jnp.pad(flat, (0, pad)).reshape(-1, 128)


def _unpack(buf, shapes):
    flat = buf.reshape(-1)
    out, off = [], 0
    for s in shapes:
        n = math.prod(s)
        out.append(flat[off:off + n].reshape(s))
        off += n
    return out


def kernel(x, norm_mix_g, w_in, gate_bias, q_norm_g, k_norm_g, attn_sinks, ssm_lambda_re, ssm_lambda_im, ssm_log_dt, ssm_b_re, ssm_b_im, ssm_c_re, ssm_c_im, ssm_d, ssm_glu_w, ssm_glu_b, w_attn_branch, w_ssm_branch, w_out, norm_ffn_g, w_ffn_in, w_ffn_out, loss_target, m_norm_mix_g, m_w_in, m_gate_bias, m_q_norm_g, m_k_norm_g, m_attn_sinks, m_ssm_lambda_re, m_ssm_lambda_im, m_ssm_log_dt, m_ssm_b_re, m_ssm_b_im, m_ssm_c_re, m_ssm_c_im, m_ssm_d, m_ssm_glu_w, m_ssm_glu_b, m_w_attn_branch, m_w_ssm_branch, m_w_out, m_norm_ffn_g, m_w_ffn_in, m_w_ffn_out, v_norm_mix_g, v_w_in, v_gate_bias, v_q_norm_g, v_k_norm_g, v_attn_sinks, v_ssm_lambda_re, v_ssm_lambda_im, v_ssm_log_dt, v_ssm_b_re, v_ssm_b_im, v_ssm_c_re, v_ssm_c_im, v_ssm_d, v_ssm_glu_w, v_ssm_glu_b, v_w_attn_branch, v_w_ssm_branch, v_w_out, v_norm_ffn_g, v_w_ffn_in, v_w_ffn_out):
    w = dict(norm_mix_g=norm_mix_g, w_in=w_in, gate_bias=gate_bias, q_norm_g=q_norm_g, k_norm_g=k_norm_g,
             attn_sinks=attn_sinks, ssm_lambda_re=ssm_lambda_re, ssm_lambda_im=ssm_lambda_im, ssm_log_dt=ssm_log_dt,
             ssm_b_re=ssm_b_re, ssm_b_im=ssm_b_im, ssm_c_re=ssm_c_re, ssm_c_im=ssm_c_im, ssm_d=ssm_d,
             ssm_glu_w=ssm_glu_w, ssm_glu_b=ssm_glu_b, w_attn_branch=w_attn_branch, w_ssm_branch=w_ssm_branch,
             w_out=w_out, norm_ffn_g=norm_ffn_g, w_ffn_in=w_ffn_in, w_ffn_out=w_ffn_out)
    m = dict(norm_mix_g=m_norm_mix_g, w_in=m_w_in, gate_bias=m_gate_bias, q_norm_g=m_q_norm_g, k_norm_g=m_k_norm_g,
             attn_sinks=m_attn_sinks, ssm_lambda_re=m_ssm_lambda_re, ssm_lambda_im=m_ssm_lambda_im,
             ssm_log_dt=m_ssm_log_dt, ssm_b_re=m_ssm_b_re, ssm_b_im=m_ssm_b_im, ssm_c_re=m_ssm_c_re,
             ssm_c_im=m_ssm_c_im, ssm_d=m_ssm_d, ssm_glu_w=m_ssm_glu_w, ssm_glu_b=m_ssm_glu_b,
             w_attn_branch=m_w_attn_branch, w_ssm_branch=m_w_ssm_branch, w_out=m_w_out, norm_ffn_g=m_norm_ffn_g,
             w_ffn_in=m_w_ffn_in, w_ffn_out=m_w_ffn_out)
    v = dict(norm_mix_g=v_norm_mix_g, w_in=v_w_in, gate_bias=v_gate_bias, q_norm_g=v_q_norm_g, k_norm_g=v_k_norm_g,
             attn_sinks=v_attn_sinks, ssm_lambda_re=v_ssm_lambda_re, ssm_lambda_im=v_ssm_lambda_im,
             ssm_log_dt=v_ssm_log_dt, ssm_b_re=v_ssm_b_re, ssm_b_im=v_ssm_b_im, ssm_c_re=v_ssm_c_re,
             ssm_c_im=v_ssm_c_im, ssm_d=v_ssm_d, ssm_glu_w=v_ssm_glu_w, ssm_glu_b=v_ssm_glu_b,
             w_attn_branch=v_w_attn_branch, w_ssm_branch=v_w_ssm_branch, w_out=v_w_out, norm_ffn_g=v_norm_ffn_g,
             w_ffn_in=v_w_ffn_in, w_ffn_out=v_w_ffn_out)
    n_layers = norm_mix_g.shape[0]
    d_model = x.shape[-1]
    seq = x.shape[1]

    exchanges = _Exchanges()
    params = [_layer_params(l, w) for l in range(n_layers)]
    gathers = {}

    def gather(l, group, bufs, deps=()):
        names = WEIGHT_GROUPS[group]
        steps = _gather_steps(bufs, "ag%d_%s" % (l, group), deps,
                              lambda got: params[l].update(_big_params(dict(zip(names, got)))))
        gathers[l, group] = steps
        return exchanges.launch(steps)

    started = gather(0, "in", [_cast_into_slot(w["w_in"], 0, name="cast0_w_in")])
    casts = {(l, group): [_cast_into_slot(w[n], l, name="cast%d_%s" % (l, n), deps=(started,)) for n in WEIGHT_GROUPS[group]]
             for l in range(n_layers) for group in WEIGHT_GROUPS if (l, group) != (0, "in")}
    forwarded = exchanges.advance(gathers[0, "in"], tuple(b for bufs in casts.values() for b in bufs))
    first_deps = (forwarded, gather(0, "mixer", casts[0, "mixer"], (forwarded,)), gather(0, "ffn", casts[0, "ffn"], (forwarded,)))
    sizes = {n: (N_CHIPS, 1) + w[n].shape[1:] for n in BIG_WEIGHTS}
    dims = _dims(d_model, sizes, min(512, seq))

    def weights_of(l, h):
        if l == 0:
            return params[0], first_deps
        for group in WEIGHT_GROUPS:
            exchanges.advance(gathers[l, group], h)
        return params[l], ()

    def fwd_tick(l, point, arr):
        tokens = []
        if l == 0 and point == "norm":
            tokens.append(exchanges.advance(gathers[0, "in"], arr))
        if l == 0 and point in ("in", "attn"):
            tokens.append(exchanges.advance(gathers[0, "mixer"], arr))
        if l == 0 and point in ("ssm", "out"):
            tokens.append(exchanges.advance(gathers[0, "ffn"], arr))
        if l + 1 < n_layers and point == "attn":
            tokens += [gather(l + 1, group, casts[l + 1, group]) for group in WEIGHT_GROUPS]
        if l + 1 < n_layers and point == "act":
            tokens += [exchanges.advance(gathers[l + 1, group], arr) for group in WEIGHT_GROUPS]
        return tuple(t for t in tokens if t is not None)

    reduced, ready = {}, []

    def bwd_tick(l, stage, arr, stage_grads):
        tokens = exchanges.advance_all(arr)
        names = tuple(stage_grads)

        def publish(halves):
            reduced.update({(l, n): h for n, h in zip(names, halves)})
            ready.append((l, names))

        grads4 = [stage_grads[n].reshape(N_CHIPS, -1, stage_grads[n].shape[-1]) for n in names]
        return tokens + (exchanges.launch(_reduce_steps(grads4, "rs%d_%s" % (l, stage), publish)),)

    sq, dx, grads, last_tokens = _local_step(x[0], loss_target[0], n_layers, weights_of, dims, fwd_tick, bwd_tick)
    loss = lax.psum(sq[0, 0], MESH_AXES) * (0.5 / d_model)

    small_shapes = [w[n].shape for n in SMALL_WEIGHTS]
    small_local = [jnp.stack([grads[l][n].reshape(w[n].shape[1:]) for l in range(n_layers)]) for n in SMALL_WEIGHTS]
    shared = []
    tokens = last_tokens + (exchanges.launch(_allreduce_steps(_pack(small_local), "small_grads", shared.append)),)

    adam = {n: None for n in BIG_WEIGHTS}
    grad, delta, new_m, new_v = {}, {}, {}, {}
    while exchanges.running or ready or shared:
        after = [dx]
        for l, names in ready[:2]:
            for n in names:
                mine, sibling = reduced[l, n]
                adam[n] = _adamw_sharded(w[n], m[n], v[n], mine, sibling, l, adam[n], name="adamw%d_%s" % (l, n), deps=tokens)
                after.append(adam[n][1])
        del ready[:2]
        if shared:
            grad.update(zip(SMALL_WEIGHTS, _unpack(shared.pop(), small_shapes)))
            for n in SMALL_WEIGHTS:
                flat = lambda a: a.reshape(-1, a.shape[-1])
                res = _adamw(flat(w[n]), flat(grad[n]), flat(m[n]), flat(v[n]), name="adamw_" + n)
                delta[n], new_m[n], new_v[n] = [r.reshape(w[n].shape) for r in res]
            after += [delta[n] for n in SMALL_WEIGHTS]
        tokens = exchanges.advance_all(tuple(after))
    for n in BIG_WEIGHTS:
        grad[n], delta[n], new_m[n], new_v[n] = adam[n]

    return (loss, dx[None], *[grad[n] for n in WEIGHT_NAMES], *[delta[n] for n in WEIGHT_NAMES],
            *[new_m[n] for n in WEIGHT_NAMES], *[new_v[n] for n in WEIGHT_NAMES])
```

```python
import math

import jax
import jax.numpy as jnp
from jax import lax
from jax.experimental import pallas as pl
from jax.experimental.pallas import tpu as pltpu

HEAD_DIM = 64
WINDOW = 128
SSM_GROUP_CH = 16
SSM_LANE_GROUPS = 8
RMS_EPS = 1e-6
ADAM_LR = 0.001
ADAM_B1 = 0.9
ADAM_B2 = 0.999
ADAM_EPS = 1e-08
ADAM_WD = 0.01
ADAM_STEP = 10
NEG_BIG = -1e30
MESH_AXES = ("x", "y", "c")
N_CHIPS = 4
N_DEV = 8
VMEM_LIMIT_BYTES = 56 * 1024 * 1024
BF16 = jnp.bfloat16
F32 = jnp.float32


def _cparams(*semantics):
    return pltpu.CompilerParams(dimension_semantics=semantics, vmem_limit_bytes=VMEM_LIMIT_BYTES)


def _pick(n, target, mult):
    if n <= target:
        return n
    best = None
    for d in range(mult, target + 1, mult):
        if n % d == 0:
            best = d
    assert best is not None, (n, target, mult)
    return best


def _rowmap(fn, ins, outs, *, rows, tm, ncol=1, name, deps=()):
    n_in = len(ins)
    nrow = rows // tm
    assert nrow * tm == rows

    in_specs = []
    for arr, kind, width, coloff in ins:
        if kind == "row":
            in_specs.append(pl.BlockSpec((tm, width), lambda j, i, o=coloff: (i, o + j)))
        elif kind == "vec":
            in_specs.append(pl.BlockSpec((1, width), lambda j, i, o=coloff: (0, o + j)))
        else:
            nd = arr.ndim
            in_specs.append(pl.BlockSpec(arr.shape, lambda j, i, nd=nd: (0,) * nd))
    out_specs, out_shapes = [], []
    for cols, dtype, kind, width in outs:
        if kind == "row":
            out_specs.append(pl.BlockSpec((tm, width), lambda j, i: (i, j)))
            out_shapes.append(jax.ShapeDtypeStruct((rows, cols), dtype))
        else:
            out_specs.append(pl.BlockSpec((1, width), lambda j, i: (0, j)))
            out_shapes.append(jax.ShapeDtypeStruct((1, cols), dtype))

    in_specs += [pl.BlockSpec(memory_space=pl.ANY)] * len(deps)

    def body(*refs):
        i = pl.program_id(1)
        res = fn(*[r[...].astype(F32) for r in refs[:n_in]])
        if not isinstance(res, (tuple, list)):
            res = (res,)
        for (cols, dtype, kind, width), ref, val in zip(outs, refs[n_in + len(deps):], res):
            if kind == "row":
                ref[...] = val.astype(ref.dtype)
            else:
                @pl.when(i == 0)
                def _():
                    ref[...] = jnp.zeros_like(ref)
                ref[...] += val.astype(ref.dtype)

    res = pl.pallas_call(
        body,
        out_shape=tuple(out_shapes),
        grid=(ncol, nrow),
        in_specs=in_specs,
        out_specs=tuple(out_specs),
        compiler_params=_cparams("parallel", "arbitrary"),
        name=name,
    )(*[a[0] for a in ins], *deps)
    return res


def _mm_body(dims, nk, has_add, unused_in=0):
    def body(*refs):
        if has_add:
            a_ref, b_ref, add_ref = refs[:3]
            o_ref = refs[3 + unused_in]
            rest = refs[4 + unused_in:]
        else:
            a_ref, b_ref = refs[:2]
            o_ref = refs[2 + unused_in]
            add_ref = None
            rest = refs[3 + unused_in:]
        part = lax.dot_general(a_ref[...], b_ref[...], (dims, ((), ())), preferred_element_type=F32)
        if nk == 1:
            if add_ref is not None:
                part = part + add_ref[...]
            o_ref[...] = part.astype(o_ref.dtype)
        else:
            acc_ref = rest[0]
            k = pl.program_id(2)

            @pl.when(k == 0)
            def _():
                acc_ref[...] = part

            @pl.when(k > 0)
            def _():
                acc_ref[...] += part

            @pl.when(k == nk - 1)
            def _():
                r = acc_ref[...]
                if add_ref is not None:
                    r = r + add_ref[...]
                o_ref[...] = r.astype(o_ref.dtype)
    return body


class _Weight:
    def __init__(self, arr, layer, kind):
        self.arr, self.layer, self.kind = arr, layer, kind
        self.s, _, self.r, self.c = arr.shape
        self.rows = self.r * (self.s if kind == "row" else 1)
        self.cols = self.c * (self.s if kind == "col" else 1)

    def tiles(self, tr, tc):
        return _pick(self.r, tr, 128), _pick(self.c, tc, 128)

    def index(self, tr, tc):
        layer = self.layer
        if self.kind == "col":
            per = self.c // tc
            return lambda rb, cb: (cb // per, layer, rb, cb % per)
        per = self.r // tr
        return lambda rb, cb: (rb // per, layer, rb % per, cb)


def _shard_index(kind, r, c, tr, tc):
    if kind == "col":
        per = c // tc
        return lambda rb, cb: (cb // per, rb, cb % per)
    per = r // tr
    return lambda rb, cb: (rb // per, rb % per, cb)


def _mm_nn(a, w, *, out_dtype, tm, tn, tk, name, add=None, deps=()):
    m, k = a.shape
    assert k == w.rows
    tm = _pick(m, tm, 16)
    tk, tn = w.tiles(tk, tn)
    nk = k // tk
    widx = w.index(tk, tn)
    in_specs = [pl.BlockSpec((tm, tk), lambda n, i, kk: (i, kk)),
                pl.BlockSpec((None, None, tk, tn), lambda n, i, kk: widx(kk, n))]
    args = [a, w.arr]
    if add is not None:
        in_specs.append(pl.BlockSpec((tm, tn), lambda n, i, kk: (i, n)))
        args.append(add)
    in_specs += [pl.BlockSpec(memory_space=pl.ANY)] * len(deps)
    args += list(deps)
    return pl.pallas_call(
        _mm_body(((1,), (0,)), nk, add is not None, unused_in=len(deps)),
        out_shape=jax.ShapeDtypeStruct((m, w.cols), out_dtype),
        grid=(w.cols // tn, m // tm, nk),
        in_specs=in_specs,
        out_specs=pl.BlockSpec((tm, tn), lambda n, i, kk: (i, n)),
        scratch_shapes=[pltpu.VMEM((tm, tn), F32)] if nk > 1 else [],
        compiler_params=_cparams("parallel", "parallel", "arbitrary"),
        name=name,
    )(*args)


def _interleaved_block(k, n_blocks):
    half = n_blocks // 2
    if isinstance(k, int):
        return 2 * k if k < half else 2 * (k - half) + 1
    return jnp.where(k < half, 2 * k, 2 * (k - half) + 1)


def _mm_nt(a, w, *, out_dtype, tm, tn, tko, name, deps=(), interleaved=0):
    m, n = a.shape
    assert n == w.cols
    tm = _pick(m, tm, 16)
    if w.kind == "col" and w.s > 1:
        tko = _pick(w.r, tko, 128)
        layer, nsh, width = w.layer, w.s, w.c
        blk = interleaved or width
        per = width // blk

        def body(a_ref, w_ref, *rest):
            o_ref = rest[len(deps)]
            acc = None
            for k in range(nsh * per):
                s, j = divmod(k, per)
                at = (_interleaved_block(k, nsh * per) if interleaved else k) * blk
                part = lax.dot_general(a_ref[:, at:at + blk], w_ref[s, :, j * blk:(j + 1) * blk], (((1,), (1,)), ((), ())),
                                       preferred_element_type=F32)
                acc = part if acc is None else acc + part
            o_ref[...] = acc.astype(o_ref.dtype)

        return pl.pallas_call(
            body,
            out_shape=jax.ShapeDtypeStruct((m, w.rows), out_dtype),
            grid=(w.rows // tko, m // tm),
            in_specs=[pl.BlockSpec((tm, n), lambda ko, i: (i, 0)),
                      pl.BlockSpec((nsh, None, tko, width), lambda ko, i: (0, layer, ko, 0))]
            + [pl.BlockSpec(memory_space=pl.ANY)] * len(deps),
            out_specs=pl.BlockSpec((tm, tko), lambda ko, i: (i, ko)),
            compiler_params=_cparams("parallel", "parallel"),
            name=name,
        )(a, w.arr, *deps)
    tko, tn = w.tiles(tko, tn)
    nk = n // tn
    widx = w.index(tko, tn)
    return pl.pallas_call(
        _mm_body(((1,), (1,)), nk, False, unused_in=len(deps)),
        out_shape=jax.ShapeDtypeStruct((m, w.rows), out_dtype),
        grid=(w.rows // tko, m // tm, nk),
        in_specs=[pl.BlockSpec((tm, tn), lambda ko, i, nn: (i, nn)),
                  pl.BlockSpec((None, None, tko, tn), lambda ko, i, nn: widx(ko, nn))]
        + [pl.BlockSpec(memory_space=pl.ANY)] * len(deps),
        out_specs=pl.BlockSpec((tm, tko), lambda ko, i, nn: (i, ko)),
        scratch_shapes=[pltpu.VMEM((tm, tko), F32)] if nk > 1 else [],
        compiler_params=_cparams("parallel", "parallel", "arbitrary"),
        name=name,
    )(a, w.arr, *deps)


def _mm_tn(a, c, w, *, into, tm, tn, tko, name, interleaved=False):
    m, k = a.shape
    tm = _pick(m, tm, 16)
    layer = w.layer
    mc, n = c.shape
    assert mc == m and k == w.rows and n == w.cols
    tko, tn = w.tiles(tko, tn)
    nk = m // tm
    oidx = _shard_index(w.kind, w.r, w.c, tko, tn)
    n_blocks = n // tn
    c_block = (lambda nn: _interleaved_block(nn, n_blocks)) if interleaved else (lambda nn: nn)
    resident = dict(pipeline_mode=pl.Buffered(1)) if nk == 1 else {}
    in_specs = [pl.BlockSpec((tm, tko), lambda ko, nn, mm: (mm, ko), **resident),
                pl.BlockSpec((tm, tn), lambda ko, nn, mm: (mm, c_block(nn)))]
    args = [a, c]
    if into is not None:
        in_specs.append(pl.BlockSpec(memory_space=pl.ANY))
        args.append(into)
    return pl.pallas_call(
        _mm_body(((0,), (0,)), nk, False, unused_in=len(args) - 2),
        out_shape=jax.ShapeDtypeStruct((w.arr.shape[1], w.s, w.r, w.c), F32),
        grid=(k // tko, n // tn, nk),
        in_specs=in_specs,
        out_specs=pl.BlockSpec((None, None, tko, tn), lambda ko, nn, mm: (layer,) + oidx(ko, nn)),
        scratch_shapes=[pltpu.VMEM((tko, tn), F32)] if nk > 1 else [],
        input_output_aliases={2: 0} if into is not None else {},
        compiler_params=_cparams("parallel", "parallel", "arbitrary"),
        name=name,
    )(*args)


def _mxu_sum(x, ones):
    hi = x.astype(BF16)
    lo = (x - hi.astype(F32)).astype(BF16)
    return jnp.dot(hi, ones, preferred_element_type=F32) + jnp.dot(lo, ones, preferred_element_type=F32)


def _head_rms(x, gain, sums):
    r = lax.rsqrt(_mxu_sum(x * x, sums[0]) * (1.0 / HEAD_DIM) + RMS_EPS)
    if len(sums) == 2:
        r = _mxu_sum(r, sums[1])
    return x * r * gain, r


def _head_rms_bwd(x, r, gain, dy, sums, fold):
    t = dy * gain
    mean = _mxu_sum(t * x, sums[0]) * (1.0 / HEAD_DIM)
    dx = r * t - x * (r * r * r) * (mean if len(sums) == 1 else _mxu_sum(mean, sums[1]))
    dg = jnp.broadcast_to(jnp.sum(dy * x * r, axis=0, keepdims=True), (8, x.shape[1]))
    return dx, _mxu_sum(dg, fold)[0:1, :HEAD_DIM]


def _attn_consts(n_q, n_kv, sinks, qg, kg):
    group = n_q // n_kv
    t = jnp.arange(WINDOW, dtype=jnp.int32)[:, None]
    s = jnp.arange(2 * WINDOW, dtype=jnp.int32)[None, :] - WINDOW
    dist = (t - s).astype(F32)
    valid = (dist >= 0) & (dist < WINDOW)
    slopes = jnp.exp2(-8.0 * jnp.arange(1, n_q + 1, dtype=F32) / n_q)
    bias = jnp.where(valid[None], -slopes[:, None, None] * dist[None], NEG_BIG)
    sink = jnp.broadcast_to(sinks.astype(F32).reshape(n_kv, group, 1, 1), (n_kv, group, WINDOW, 128))
    head_ones = lambda h: jnp.kron(jnp.eye(h, dtype=F32), jnp.ones((HEAD_DIM, HEAD_DIM), F32)).astype(BF16)
    fold = lambda h: jnp.tile(jnp.eye(HEAD_DIM, 128, dtype=F32), (h, 1)).astype(BF16)
    q_heads = jnp.kron(jnp.eye(n_q, 128, dtype=F32), jnp.ones((HEAD_DIM, 1), F32)).astype(BF16)
    return dict(
        bias=bias.reshape(n_kv, group * WINDOW, 2 * WINDOW),
        sink=sink.reshape(n_kv, group * WINDOW, 128),
        qg=jnp.tile(qg, (1, n_q)), kg=jnp.tile(kg, (1, n_kv)),
        q_heads=q_heads, q_spread=q_heads.T,
        k_ones=head_ones(n_kv),
        q_fold=fold(n_q), k_fold=fold(n_kv),
        key_ones=jnp.ones((2 * WINDOW, 128), BF16))


_ATTN_CONST_ORDER = ("qg", "kg", "sink", "bias", "q_heads", "q_spread", "k_ones", "q_fold", "k_fold", "key_ones")


def _attn_inputs(q_ref, kc_ref, kp_ref, vc_ref, vp_ref, c):
    q = q_ref[...]
    k2 = jnp.concatenate([kp_ref[...], kc_ref[...]], axis=0)
    v2 = jnp.concatenate([vp_ref[...], vc_ref[...]], axis=0)
    qn, rq = _head_rms(q, c["qg"][...], (c["q_heads"][...], c["q_spread"][...]))
    kn, rk = _head_rms(k2, c["kg"][...], (c["k_ones"][...],))
    return dict(q=q, rq=rq, qn=qn.astype(BF16), k2=k2, rk=rk, kn=kn.astype(BF16), v2=v2.astype(BF16))


def _attn_probs(x, c, first_mask, kv, group):
    sl = slice(kv * HEAD_DIM, (kv + 1) * HEAD_DIM)
    k2b, v2b = x["kn"][:, sl], x["v2"][:, sl]
    qs = jnp.concatenate([x["qn"][:, (kv * group + g) * HEAD_DIM:(kv * group + g + 1) * HEAD_DIM]
                          for g in range(group)], axis=0)
    s = lax.dot_general(qs, k2b, (((1,), (1,)), ((), ())), preferred_element_type=F32) * (HEAD_DIM ** -0.5)
    s = jnp.where(first_mask, NEG_BIG, s + c["bias"][kv])
    sink = c["sink"][kv]
    m = jnp.maximum(jnp.max(s, axis=-1, keepdims=True), sink)
    twice = lambda a: jnp.concatenate([a, a], axis=1)
    p = jnp.exp(s - twice(m))
    esink = jnp.exp(sink - m)
    inv = 1.0 / (_mxu_sum(p, c["key_ones"][...]) + esink)
    return dict(k2b=k2b, v2b=v2b, qs=qs, pn=p * twice(inv), psink=esink * inv, twice=twice)


def _attn_specs(n_q, n_kv):
    aw, kvw = n_q * HEAD_DIM, n_kv * HEAD_DIM
    group = n_q // n_kv
    kblk, vblk = aw // kvw, aw // kvw + 1

    def specs(nb):
        cur = lambda n: jnp.minimum(n, nb - 1)
        prev = lambda n: jnp.maximum(jnp.minimum(n, nb - 1) - 1, 0)
        return [
            pl.BlockSpec((WINDOW, aw), lambda n: (cur(n), 0)),
            pl.BlockSpec((WINDOW, kvw), lambda n: (cur(n), kblk)),
            pl.BlockSpec((WINDOW, kvw), lambda n: (prev(n), kblk)),
            pl.BlockSpec((WINDOW, kvw), lambda n: (cur(n), vblk)),
            pl.BlockSpec((WINDOW, kvw), lambda n: (prev(n), vblk)),
        ]
    whole = lambda shape: pl.BlockSpec(shape, lambda n: (0,) * len(shape))
    return specs, whole


def _attn_fwd(z, qg, kg, sinks, *, n_q, n_kv, name, deps=()):
    L = z.shape[0]
    nb = L // WINDOW
    aw = n_q * HEAD_DIM
    group = n_q // n_kv
    consts = _attn_consts(n_q, n_kv, sinks, qg, kg)
    specs, whole = _attn_specs(n_q, n_kv)
    nc = len(_ATTN_CONST_ORDER)

    def body(q_ref, kc_ref, kp_ref, vc_ref, vp_ref, *rest):
        c = dict(zip(_ATTN_CONST_ORDER, rest[:nc]))
        o_ref = rest[-1]
        n = pl.program_id(0)
        col = lax.broadcasted_iota(jnp.int32, (group * WINDOW, 2 * WINDOW), 1)
        first_mask = jnp.logical_and(n == 0, col < WINDOW)
        x = _attn_inputs(q_ref, kc_ref, kp_ref, vc_ref, vp_ref, c)
        for kv in range(n_kv):
            a = _attn_probs(x, c, first_mask, kv, group)
            o = jnp.dot(a["pn"].astype(BF16), a["v2b"], preferred_element_type=F32)
            for g in range(group):
                h = kv * group + g
                o_ref[:, h * HEAD_DIM:(h + 1) * HEAD_DIM] = o[g * WINDOW:(g + 1) * WINDOW].astype(o_ref.dtype)

    return pl.pallas_call(
        body,
        out_shape=jax.ShapeDtypeStruct((L, aw), BF16),
        grid=(nb,),
        in_specs=specs(nb) + [whole(consts[k].shape) for k in _ATTN_CONST_ORDER]
        + [pl.BlockSpec(memory_space=pl.ANY)] * len(deps),
        out_specs=pl.BlockSpec((WINDOW, aw), lambda n: (n, 0)),
        compiler_params=_cparams("parallel"),
        name=name,
    )(z, z, z, z, z, *[consts[k] for k in _ATTN_CONST_ORDER], *deps)


def _attn_bwd(z, do, qg, kg, sinks, *, n_q, n_kv, name):
    L = z.shape[0]
    nb = L // WINDOW
    aw, kvw = n_q * HEAD_DIM, n_kv * HEAD_DIM
    group = n_q // n_kv
    consts = _attn_consts(n_q, n_kv, sinks, qg, kg)
    specs, whole = _attn_specs(n_q, n_kv)
    scale = HEAD_DIM ** -0.5
    nc = len(_ATTN_CONST_ORDER)

    def body(q_ref, kc_ref, kp_ref, vc_ref, vp_ref, do_ref, *rest):
        c = dict(zip(_ATTN_CONST_ORDER, rest[:nc]))
        dq_ref, dkv_ref, dqg_ref, dkg_ref, dsink_ref, carry_ref, dqn_ref, dkn_ref, dv_ref = rest[nc:]
        n = pl.program_id(0)

        @pl.when(n == 0)
        def _():
            dqg_ref[...] = jnp.zeros_like(dqg_ref)
            dkg_ref[...] = jnp.zeros_like(dkg_ref)
            dsink_ref[...] = jnp.zeros_like(dsink_ref)
            carry_ref[...] = jnp.zeros_like(carry_ref)

        @pl.when(n < nb)
        def _():
            col = lax.broadcasted_iota(jnp.int32, (group * WINDOW, 2 * WINDOW), 1)
            first_mask = jnp.logical_and(n == 0, col < WINDOW)
            head_lane = lax.broadcasted_iota(jnp.int32, (1, n_q), 1)
            x = _attn_inputs(q_ref, kc_ref, kp_ref, vc_ref, vp_ref, c)
            dsink = jnp.zeros((1, n_q), F32)
            for kv in range(n_kv):
                a = _attn_probs(x, c, first_mask, kv, group)
                pn = a["pn"]
                dos = jnp.concatenate(
                    [do_ref[:, (kv * group + g) * HEAD_DIM:(kv * group + g + 1) * HEAD_DIM] for g in range(group)],
                    axis=0).astype(BF16)
                dpn = lax.dot_general(dos, a["v2b"], (((1,), (1,)), ((), ())), preferred_element_type=F32)
                ksl = slice(kv * HEAD_DIM, (kv + 1) * HEAD_DIM)
                dv_ref[:, ksl] = lax.dot_general(pn.astype(BF16), dos, (((0,), (0,)), ((), ())), preferred_element_type=F32)
                delta = _mxu_sum(pn * dpn, c["key_ones"][...])
                ds = (pn * (dpn - a["twice"](delta))).astype(BF16)
                dsk = -a["psink"] * delta
                dqn = lax.dot_general(ds, a["k2b"], (((1,), (0,)), ((), ())), preferred_element_type=F32) * scale
                dkn_ref[:, ksl] = lax.dot_general(ds, a["qs"], (((0,), (0,)), ((), ())), preferred_element_type=F32) * scale
                for g in range(group):
                    h = kv * group + g
                    rows = slice(g * WINDOW, (g + 1) * WINDOW)
                    dqn_ref[:, h * HEAD_DIM:(h + 1) * HEAD_DIM] = dqn[rows]
                    dsink = dsink + jnp.where(head_lane == h, jnp.sum(dsk[rows], axis=0, keepdims=True)[:, :n_q], 0.0)
            dq, dqg = _head_rms_bwd(x["q"], x["rq"], c["qg"][...], dqn_ref[...],
                                    (c["q_heads"][...], c["q_spread"][...]), c["q_fold"][...])
            dk2, dkg = _head_rms_bwd(x["k2"], x["rk"], c["kg"][...], dkn_ref[...], (c["k_ones"][...],), c["k_fold"][...])
            dq_ref[...] = dq.astype(dq_ref.dtype)
            dkv_ref[:, :kvw] = (carry_ref[:, :kvw] + dk2[:WINDOW]).astype(dkv_ref.dtype)
            dkv_ref[:, kvw:] = (carry_ref[:, kvw:] + dv_ref[:WINDOW, :]).astype(dkv_ref.dtype)
            carry_ref[:, :kvw] = dk2[WINDOW:]
            carry_ref[:, kvw:] = dv_ref[WINDOW:, :]
            dqg_ref[...] += dqg
            dkg_ref[...] += dkg
            dsink_ref[...] += dsink

        @pl.when(n == nb)
        def _():
            dkv_ref[...] = carry_ref[...].astype(dkv_ref.dtype)

    in_specs = (specs(nb) + [pl.BlockSpec((WINDOW, aw), lambda n: (jnp.minimum(n, nb - 1), 0))]
                + [whole(consts[k].shape) for k in _ATTN_CONST_ORDER])
    return pl.pallas_call(
        body,
        out_shape=(jax.ShapeDtypeStruct((L, aw), BF16), jax.ShapeDtypeStruct((L, 2 * kvw), BF16),
                   jax.ShapeDtypeStruct((1, HEAD_DIM), F32), jax.ShapeDtypeStruct((1, HEAD_DIM), F32),
                   jax.ShapeDtypeStruct((1, n_q), F32)),
        grid=(nb + 1,),
        in_specs=in_specs,
        out_specs=(pl.BlockSpec((WINDOW, aw), lambda n: (jnp.minimum(n, nb - 1), 0)),
                   pl.BlockSpec((WINDOW, 2 * kvw), lambda n: (jnp.maximum(n - 1, 0), 0)),
                   pl.BlockSpec((1, HEAD_DIM), lambda n: (0, 0)),
                   pl.BlockSpec((1, HEAD_DIM), lambda n: (0, 0)),
                   pl.BlockSpec((1, n_q), lambda n: (0, 0))),
        scratch_shapes=[pltpu.VMEM((WINDOW, 2 * kvw), F32), pltpu.VMEM((WINDOW, aw), F32),
                        pltpu.VMEM((2 * WINDOW, kvw), F32), pltpu.VMEM((2 * WINDOW, kvw), F32)],
        compiler_params=_cparams("arbitrary"),
        name=name,
    )(z, z, z, z, z, do, *[consts[k] for k in _ATTN_CONST_ORDER])


def _cmul(ar, ai, br, bi):
    return ar * br - ai * bi, ar * bi + ai * br


def _time_permutation(tc):
    r = jnp.arange(tc)
    src = (r % 8) * (tc // 8) + r // 8
    p = (src[:, None] == jnp.arange(tc)[None, :]).astype(BF16)
    return p, p.T


def _unpermute(pt, x):
    hi = x.astype(BF16)
    lo = (x - hi.astype(F32)).astype(BF16)
    moved = jnp.dot(pt, jnp.concatenate([hi, lo], axis=1), preferred_element_type=F32)
    return moved[:, :x.shape[1]] + moved[:, x.shape[1]:]


def _segment_scan(xr_ref, xi_ref, ar, ai, cr, ci, ng, reverse):
    n = ar.shape[-1]
    row = lax.broadcasted_iota(jnp.int32, (8, n), 0)
    seeded = 7 if reverse else 0
    a8r = jnp.broadcast_to(ar, (8, n))
    a8i = jnp.broadcast_to(ai, (8, n))
    rows_of = lambda g: pl.ds(pl.multiple_of(((ng - 1 - g) if reverse else g) * 8, 8), 8)

    def recur(g, s):
        rows = rows_of(g)
        sr = a8r * s[0] - a8i * s[1] + xr_ref[rows, :]
        si = a8r * s[1] + a8i * s[0] + xi_ref[rows, :]
        xr_ref[rows, :] = sr
        xi_ref[rows, :] = si
        return sr, si

    fr, fi = lax.fori_loop(0, ng, recur, (jnp.where(row == seeded, cr, 0.0), jnp.where(row == seeded, ci, 0.0)))
    pr, pi = ar, ai
    for _ in range(ng.bit_length() - 1):
        pr, pi = _cmul(pr, pi, pr, pi)
    for k in (1, 2, 4):
        keep = (row < 8 - k) if reverse else (row >= k)
        shift = (8 - k) if reverse else k
        mr, mi = jnp.where(keep, pr, 0.0), jnp.where(keep, pi, 0.0)
        tr, ti = pltpu.roll(fr, shift, 0), pltpu.roll(fi, shift, 0)
        fr, fi = fr + mr * tr - mi * ti, fi + mr * ti + mi * tr
        pr, pi = _cmul(pr, pi, pr, pi)
    shift = 7 if reverse else 1
    before_r, before_i = pltpu.roll(fr, shift, 0), pltpu.roll(fi, shift, 0)

    def inherit(g, d):
        rows = rows_of(g)
        dr = a8r * d[0] - a8i * d[1]
        di = a8r * d[1] + a8i * d[0]
        xr_ref[rows, :] = xr_ref[rows, :] + dr
        xi_ref[rows, :] = xi_ref[rows, :] + di
        return dr, di

    lax.fori_loop(0, ng, inherit, (jnp.where(row == seeded, 0.0, before_r), jnp.where(row == seeded, 0.0, before_i)))
    out = 0 if reverse else 7
    return (fr[out:out + 1], fi[out:out + 1],
            jnp.where(row == seeded, cr, before_r), jnp.where(row == seeded, ci, before_i))


def _blockdiag(x):
    g, a, b = x.shape
    j = g // SSM_LANE_GROUPS
    eye = jnp.eye(SSM_LANE_GROUPS, dtype=x.dtype)
    y = x.reshape(j, SSM_LANE_GROUPS, a, 1, b) * eye[None, :, None, :, None]
    return y.reshape(j, SSM_LANE_GROUPS * a, SSM_LANE_GROUPS * b)


def _blockdiag_extract(y, a, b):
    j = y.shape[0]
    y = y.reshape(j, SSM_LANE_GROUPS, a, SSM_LANE_GROUPS, b)
    return jnp.einsum("jgahb,gh->jgab", y, jnp.eye(SSM_LANE_GROUPS, dtype=y.dtype)).reshape(j * SSM_LANE_GROUPS, a, b)


def _ssm_disc(lr, li, ldt, brt, bit):
    dt = jnp.exp(ldt)
    mag = jnp.exp(lr * dt)
    ar = mag * jnp.cos(li * dt)
    ai = mag * jnp.sin(li * dt)
    den = lr * lr + li * li
    fr = ((ar - 1.0) * lr + ai * li) / den
    fi = (ai * lr - (ar - 1.0) * li) / den
    bbr = fr[:, None, :] * brt - fi[:, None, :] * bit
    bbi = fr[:, None, :] * bit + fi[:, None, :] * brt
    return ar, ai, bbr, bbi


def _ssm_prep(lr, li, ldt, brt, bit, *, name):
    g, h, p = brt.shape

    def body(lr_ref, li_ref, ldt_ref, brt_ref, bit_ref, ar_ref, ai_ref, bbr_ref, bbi_ref):
        ar, ai, bbr, bbi = _ssm_disc(lr_ref[...], li_ref[...], ldt_ref[...], brt_ref[...], bit_ref[...])
        ar_ref[...] = ar
        ai_ref[...] = ai
        bbr_ref[...] = bbr
        bbi_ref[...] = bbi

    gp = jax.ShapeDtypeStruct((g, p), F32)
    ghp = jax.ShapeDtypeStruct((g, h, p), F32)
    return pl.pallas_call(body, out_shape=(gp, gp, ghp, ghp), name=name)(lr, li, ldt, brt, bit)


def _ssm_prep_bwd(lr, li, ldt, brt, bit, dar, dai, dbbr, dbbi, *, name):
    g, h, p = brt.shape

    def body(lr_ref, li_ref, ldt_ref, brt_ref, bit_ref, dar_ref, dai_ref, dbbr_ref, dbbi_ref,
             dlr_ref, dli_ref, dldt_ref, dbrt_ref, dbit_ref):
        _, vjp = jax.vjp(_ssm_disc, lr_ref[...], li_ref[...], ldt_ref[...], brt_ref[...], bit_ref[...])
        dlr, dli, dldt, dbrt, dbit = vjp((dar_ref[...], dai_ref[...], dbbr_ref[...], dbbi_ref[...]))
        dlr_ref[...] = dlr
        dli_ref[...] = dli
        dldt_ref[...] = dldt
        dbrt_ref[...] = dbrt
        dbit_ref[...] = dbit

    gp = jax.ShapeDtypeStruct((g, p), F32)
    ghp = jax.ShapeDtypeStruct((g, h, p), F32)
    return pl.pallas_call(body, out_shape=(gp, gp, jax.ShapeDtypeStruct((g, 1), F32), ghp, ghp), name=name)(
        lr, li, ldt, brt, bit, dar, dai, dbbr, dbbi)


def _ssm_specs(tc, nlanes, nch, u_colblk, chunk_of):
    return [
        pl.BlockSpec((tc, nch), lambda j, c: (chunk_of(c), u_colblk + j)),
        pl.BlockSpec((1, nlanes), lambda j, c: (0, j)),
        pl.BlockSpec((1, nlanes), lambda j, c: (0, j)),
        pl.BlockSpec((None, nch, nlanes), lambda j, c: (j, 0, 0)),
        pl.BlockSpec((None, nch, nlanes), lambda j, c: (j, 0, 0)),
        pl.BlockSpec((None, nlanes, nch), lambda j, c: (j, 0, 0)),
        pl.BlockSpec((None, nlanes, nch), lambda j, c: (j, 0, 0)),
        pl.BlockSpec((1, nch), lambda j, c: (0, j)),
        pl.BlockSpec((tc, tc), lambda j, c: (0, 0)),
        pl.BlockSpec((tc, tc), lambda j, c: (0, 0)),
    ]


def _ssm_fwd(z, ar, ai, bblk_r, bblk_i, cblk_r, cblk_i, d, *, u_col, tc, name, deps=()):
    L = z.shape[0]
    nj, nch, nlanes = bblk_r.shape
    w = nj * nch
    nc = L // tc
    ng = tc // 8

    assert ng & (ng - 1) == 0
    perm, perm_t = _time_permutation(tc)

    def body(u_ref, ar_ref, ai_ref, br_ref, bi_ref, cr_ref, ci_ref, d_ref, p_ref, pt_ref, *rest):
        y_ref, s0r_ref, s0i_ref, xr_ref, xi_ref, carr_ref, cari_ref = rest[len(deps):]
        c = pl.program_id(1)

        @pl.when(c == 0)
        def _():
            carr_ref[...] = jnp.zeros_like(carr_ref)
            cari_ref[...] = jnp.zeros_like(cari_ref)

        s0r_ref[...] = carr_ref[...]
        s0i_ref[...] = cari_ref[...]
        u = u_ref[...]
        ub = jnp.dot(p_ref[...], u.astype(BF16), preferred_element_type=F32).astype(BF16)
        xr_ref[...] = jnp.dot(ub, br_ref[...].astype(BF16), preferred_element_type=F32)
        xi_ref[...] = jnp.dot(ub, bi_ref[...].astype(BF16), preferred_element_type=F32)
        cr, ci, _, _ = _segment_scan(xr_ref, xi_ref, ar_ref[...], ai_ref[...], carr_ref[...], cari_ref[...], ng, False)
        carr_ref[...] = cr
        cari_ref[...] = ci
        y = (jnp.dot(xr_ref[...].astype(BF16), cr_ref[...].astype(BF16), preferred_element_type=F32)
             - jnp.dot(xi_ref[...].astype(BF16), ci_ref[...].astype(BF16), preferred_element_type=F32))
        y_ref[...] = _unpermute(pt_ref[...], y) + d_ref[...] * u

    state = jax.ShapeDtypeStruct((nc, 1, nj * nlanes), F32)
    state_spec = pl.BlockSpec((None, 1, nlanes), lambda j, c: (c, 0, j))
    return pl.pallas_call(
        body,
        out_shape=(jax.ShapeDtypeStruct((L, w), F32), state, state),
        grid=(nj, nc),
        in_specs=_ssm_specs(tc, nlanes, nch, u_col // nch, lambda c: c) + [pl.BlockSpec(memory_space=pl.ANY)] * len(deps),
        out_specs=(pl.BlockSpec((tc, nch), lambda j, c: (c, j)), state_spec, state_spec),
        scratch_shapes=[pltpu.VMEM((tc, nlanes), F32), pltpu.VMEM((tc, nlanes), F32),
                        pltpu.VMEM((1, nlanes), F32), pltpu.VMEM((1, nlanes), F32)],
        compiler_params=_cparams("parallel", "arbitrary"),
        name=name,
    )(z, ar, ai, bblk_r, bblk_i, cblk_r, cblk_i, d, perm, perm_t, *deps)


def _ssm_bwd(z, dy, s0r, s0i, ar, ai, bblk_r, bblk_i, cblk_r, cblk_i, d, *, u_col, tc, name):
    L = z.shape[0]
    nj, nch, nlanes = bblk_r.shape
    w = nj * nch
    nc = L // tc
    ng = tc // 8
    chunk_of = lambda c: nc - 1 - c
    assert ng & (ng - 1) == 0
    perm, perm_t = _time_permutation(tc)

    def body(u_ref, ar_ref, ai_ref, br_ref, bi_ref, cr_ref, ci_ref, d_ref, p_ref, pt_ref, dy_ref, s0r_ref, s0i_ref,
             du_ref, dbr_ref, dbi_ref, dcr_ref, dci_ref, dar_ref, dai_ref, dd_ref,
             sr_ref, si_ref, lr_ref, li_ref, carr_ref, cari_ref):
        c = pl.program_id(1)

        @pl.when(c == 0)
        def _():
            for ref in (dbr_ref, dbi_ref, dcr_ref, dci_ref, dar_ref, dai_ref, dd_ref, carr_ref, cari_ref):
                ref[...] = jnp.zeros_like(ref)

        u = u_ref[...]
        dyv = dy_ref[...]
        both = jnp.dot(p_ref[...], jnp.concatenate([u.astype(BF16), dyv.astype(BF16)], axis=1), preferred_element_type=F32)
        ub = both[:, :nch].astype(BF16)
        dyb = both[:, nch:].astype(BF16)
        brb = br_ref[...].astype(BF16)
        bib = bi_ref[...].astype(BF16)
        crb = cr_ref[...].astype(BF16)
        cib = ci_ref[...].astype(BF16)
        a_r, a_i = ar_ref[...], ai_ref[...]

        sr_ref[...] = jnp.dot(ub, brb, preferred_element_type=F32)
        si_ref[...] = jnp.dot(ub, bib, preferred_element_type=F32)
        _, _, start_r, start_i = _segment_scan(sr_ref, si_ref, a_r, a_i, s0r_ref[...], s0i_ref[...], ng, False)

        nt = (((1,), (1,)), ((), ()))
        lr_ref[...] = lax.dot_general(dyb, crb, nt, preferred_element_type=F32)
        li_ref[...] = -lax.dot_general(dyb, cib, nt, preferred_element_type=F32)
        cr, ci, _, _ = _segment_scan(lr_ref, li_ref, a_r, -a_i, carr_ref[...], cari_ref[...], ng, True)
        carr_ref[...] = cr
        cari_ref[...] = ci

        def accumulate(g, carry):
            pr, pi, acc_r, acc_i = carry
            rows = pl.ds(pl.multiple_of(g * 8, 8), 8)
            lr, li = lr_ref[rows, :], li_ref[rows, :]
            return sr_ref[rows, :], si_ref[rows, :], acc_r + lr * pr + li * pi, acc_i + li * pr - lr * pi

        zero8 = jnp.zeros((8, nlanes), F32)
        _, _, acc_r, acc_i = lax.fori_loop(0, ng, accumulate, (start_r, start_i, zero8, zero8))
        dar_ref[...] += jnp.sum(acc_r, axis=0, keepdims=True)
        dai_ref[...] += jnp.sum(acc_i, axis=0, keepdims=True)

        tn = (((0,), (0,)), ((), ()))
        lrb = lr_ref[...].astype(BF16)
        lib = li_ref[...].astype(BF16)
        dcr_ref[...] += lax.dot_general(dyb, sr_ref[...].astype(BF16), tn, preferred_element_type=F32)
        dci_ref[...] -= lax.dot_general(dyb, si_ref[...].astype(BF16), tn, preferred_element_type=F32)
        dbr_ref[...] += lax.dot_general(ub, lrb, tn, preferred_element_type=F32)
        dbi_ref[...] += lax.dot_general(ub, lib, tn, preferred_element_type=F32)
        du = (lax.dot_general(lrb, brb, nt, preferred_element_type=F32)
              + lax.dot_general(lib, bib, nt, preferred_element_type=F32))
        du_ref[...] = (_unpermute(pt_ref[...], du) + d_ref[...] * dyv).astype(du_ref.dtype)
        dd_ref[...] += jnp.sum(dyv * u, axis=0, keepdims=True)

    state_spec = pl.BlockSpec((None, 1, nlanes), lambda j, c: (chunk_of(c), 0, j))
    bshape = jax.ShapeDtypeStruct((nj, nch, nlanes), F32)
    ashape = jax.ShapeDtypeStruct((1, nj * nlanes), F32)
    bspec = pl.BlockSpec((None, nch, nlanes), lambda j, c: (j, 0, 0))
    aspec = pl.BlockSpec((1, nlanes), lambda j, c: (0, j))
    big = pltpu.VMEM((tc, nlanes), F32)
    return pl.pallas_call(
        body,
        out_shape=(jax.ShapeDtypeStruct((L, w), BF16), bshape, bshape, bshape, bshape, ashape, ashape,
                   jax.ShapeDtypeStruct((1, w), F32)),
        grid=(nj, nc),
        in_specs=_ssm_specs(tc, nlanes, nch, u_col // nch, chunk_of)
        + [pl.BlockSpec((tc, nch), lambda j, c: (chunk_of(c), j)), state_spec, state_spec],
        out_specs=(pl.BlockSpec((tc, nch), lambda j, c: (chunk_of(c), j)), bspec, bspec, bspec, bspec, aspec, aspec,
                   pl.BlockSpec((1, nch), lambda j, c: (0, j))),
        scratch_shapes=[big, big, big, big, pltpu.VMEM((1, nlanes), F32), pltpu.VMEM((1, nlanes), F32)],
        compiler_params=_cparams("parallel", "arbitrary"),
        name=name,
    )(z, ar, ai, bblk_r, bblk_i, cblk_r, cblk_i, d, perm, perm_t, dy, s0r, s0i)


def _rmsnorm_rows(x, g):
    return x * lax.rsqrt(jnp.mean(x * x, axis=-1, keepdims=True) + RMS_EPS) * g


def _glu_out(y_raw, pre, b):
    yg = jax.nn.gelu(y_raw)
    return yg * jax.nn.sigmoid(pre + b)


def _gate_merge(za, zs, ba, bs, a, bm):
    return jax.nn.sigmoid(za + ba) * a + jax.nn.sigmoid(zs + bs) * bm


def _swiglu(g, u):
    return jax.nn.silu(g) * u


def _row_tile(width_bytes_per_row, rows):
    budget = VMEM_LIMIT_BYTES // 3
    t = max(8, min(1024, budget // (2 * max(width_bytes_per_row, 1))))
    return _pick(rows, t, 16)


def _ssm_params(p, prefix):
    g, pst = p["lam_re"].shape
    ar, ai, bbr, bbi = _ssm_prep(p["lam_re"], p["lam_im"], p["log_dt"], p["b_re_t"], p["b_im_t"], name=prefix + "_ssm_prep")
    return dict(ar=ar.reshape(1, g * pst), ai=ai.reshape(1, g * pst),
                bblk_r=_blockdiag(bbr), bblk_i=_blockdiag(bbi),
                cblk_r=_blockdiag(jnp.swapaxes(p["c_re"], 1, 2)), cblk_i=_blockdiag(jnp.swapaxes(p["c_im"], 1, 2)))


def _layer_fwd(x, p, dims, prefix, deps=(), tick=None):
    tick = tick or (lambda point, arr: ())
    t, d = x.shape
    aw, kvw, sw, ff = dims["aw"], dims["kvw"], dims["sw"], dims["ff"]
    off_u = aw + 2 * kvw
    off_g = off_u + sw
    gblk = _pick(d, 512, 128)
    assert off_g % gblk == 0 and off_u % (SSM_LANE_GROUPS * SSM_GROUP_CH) == 0
    sv = {"x": x}

    h, = _rowmap(_rmsnorm_rows, [(x, "row", d, 0), (p["norm_mix_g"], "vec", d, 0)], [(d, BF16, "row", d)],
                 rows=t, tm=_row_tile(6 * d, t), name=prefix + "_norm_mix", deps=deps)
    deps = tick("norm", h)
    z = _mm_nn(h, p["w_in"], out_dtype=F32, tm=1024, tn=1664, tk=2048, name=prefix + "_mm_in", deps=deps)
    ya = _attn_fwd(z, p["q_norm_g"], p["k_norm_g"], p["attn_sinks"], n_q=dims["n_q"], n_kv=dims["n_kv"],
                   name=prefix + "_attn_fwd", deps=tick("in", z))
    sp = _ssm_params(p, prefix)
    y_raw, s0r, s0i = _ssm_fwd(z, sp["ar"], sp["ai"], sp["bblk_r"], sp["bblk_i"], sp["cblk_r"], sp["cblk_i"], p["ssm_d"],
                               u_col=off_u, tc=dims["tc"], name=prefix + "_ssm_fwd", deps=tick("attn", ya))
    yg, = _rowmap(jax.nn.gelu, [(y_raw, "row", sw, 0)], [(sw, BF16, "row", sw)],
                  rows=t, tm=_row_tile(6 * sw, t), name=prefix + "_gelu", deps=tick("ssm", y_raw))
    pre = _mm_nn(yg, p["ssm_glu_w"], out_dtype=F32, tm=1024, tn=1024, tk=1024, name=prefix + "_mm_glu")
    y2, = _rowmap(_glu_out, [(y_raw, "row", sw, 0), (pre, "row", sw, 0), (p["ssm_glu_b"], "vec", sw, 0)],
                  [(sw, BF16, "row", sw)], rows=t, tm=_row_tile(10 * sw, t), name=prefix + "_glu_out")
    a = _mm_nn(ya, p["w_attn_branch"], out_dtype=BF16, tm=1024, tn=512, tk=1024, name=prefix + "_mm_ab")
    bm = _mm_nn(y2, p["w_ssm_branch"], out_dtype=BF16, tm=1024, tn=512, tk=1024, name=prefix + "_mm_sb")
    ngb = d // gblk
    merged, = _rowmap(
        _gate_merge,
        [(z, "row", gblk, off_g // gblk), (z, "row", gblk, off_g // gblk + ngb),
         (p["gate_bias"], "vec", gblk, 0), (p["gate_bias"], "vec", gblk, ngb),
         (a, "row", gblk, 0), (bm, "row", gblk, 0)],
        [(d, BF16, "row", gblk)], rows=t, tm=_row_tile(18 * gblk, t), ncol=ngb, name=prefix + "_gate")
    x1 = _mm_nn(merged, p["w_out"], out_dtype=F32, tm=1024, tn=1024, tk=2048, name=prefix + "_mm_out", add=x)
    h2, = _rowmap(_rmsnorm_rows, [(x1, "row", d, 0), (p["norm_ffn_g"], "vec", d, 0)], [(d, BF16, "row", d)],
                  rows=t, tm=_row_tile(6 * d, t), name=prefix + "_norm_ffn", deps=tick("out", x1))
    gu = _mm_nn(h2, p["w_ffn_in"], out_dtype=BF16, tm=1024, tn=1408, tk=2048, name=prefix + "_mm_ffn_in")
    fblk = _pick(ff, 1408, 128)
    nfb = ff // fblk
    act, = _rowmap(_swiglu, [(gu, "row", fblk, 0), (gu, "row", fblk, nfb)], [(ff, BF16, "row", fblk)],
                   rows=t, tm=_row_tile(10 * fblk, t), ncol=nfb, name=prefix + "_swiglu", deps=tick("ffn_in", gu))
    x2 = _mm_nn(act, p["w_ffn_out"], out_dtype=F32, tm=1024, tn=512, tk=5632, name=prefix + "_mm_ffn_out", add=x1,
                deps=tick("act", act))
    sv.update(h=h, z=z, ya=ya, sp=sp, y_raw=y_raw, s0r=s0r, s0i=s0i, yg=yg, pre=pre, y2=y2, a=a, bm=bm,
              merged=merged, x1=x1, h2=h2, gu=gu, act=act)
    return x2, sv


def _layer_bwd(dx2, dx2b, sv, p, dims, prefix, gbuf, deps=(), before_mixer=None, before_in=None):
    t, d = dx2.shape
    aw, kvw, sw, ff = dims["aw"], dims["kvw"], dims["sw"], dims["ff"]
    off_u = aw + 2 * kvw
    off_g = off_u + sw
    gblk = _pick(d, 512, 128)
    ngb = d // gblk
    fblk = _pick(ff, 1408, 128)
    nfb = ff // fblk
    g = {}

    dact = _mm_nt(dx2b, p["w_ffn_out"], out_dtype=BF16, tm=1024, tn=2048, tko=1408, name=prefix + "_mm_dact", deps=deps)
    g["w_ffn_out"] = _mm_tn(sv["act"], dx2b, p["w_ffn_out"], into=gbuf.get("w_ffn_out"), tm=4096, tn=1024, tko=512,
                            name=prefix + "_mm_dw_ffn_out")

    def swiglu_bwd(gg, uu, da):
        s = jax.nn.sigmoid(gg)
        gs = gg * s
        return jnp.concatenate([da * uu * (s + gs * (1.0 - s)), da * gs], axis=1)

    dgu, = _rowmap(swiglu_bwd, [(sv["gu"], "row", fblk, 0), (sv["gu"], "row", fblk, nfb), (dact, "row", fblk, 0)],
                   [(2 * ff, BF16, "row", 2 * fblk)],
                   rows=t, tm=_row_tile(16 * fblk, t), ncol=nfb, name=prefix + "_swiglu_bwd")
    dh2 = _mm_nt(dgu, p["w_ffn_in"], out_dtype=F32, tm=512, tn=1408, tko=512, name=prefix + "_mm_dh2", interleaved=fblk)
    g["w_ffn_in"] = _mm_tn(sv["h2"], dgu, p["w_ffn_in"], into=gbuf.get("w_ffn_in"), tm=4096, tn=fblk, tko=1024,
                           name=prefix + "_mm_dw_ffn_in", interleaved=True)

    def norm_bwd(xx, gg, dh, dres):
        _, vjp = jax.vjp(_rmsnorm_rows, xx, gg)
        dxx, dgg = vjp(dh)
        dxx = dxx + dres
        return dxx, dxx, dgg

    dx1, dx1b, g["norm_ffn_g"] = _rowmap(
        norm_bwd, [(sv["x1"], "row", d, 0), (p["norm_ffn_g"], "vec", d, 0), (dh2, "row", d, 0), (dx2, "row", d, 0)],
        [(d, F32, "row", d), (d, BF16, "row", d), (d, F32, "acc", d)],
        rows=t, tm=_row_tile(22 * d, t), name=prefix + "_norm_ffn_bwd")

    deps = before_mixer(dx1, g) if before_mixer else ()
    dmerged = _mm_nt(dx1b, p["w_out"], out_dtype=BF16, tm=1024, tn=2048, tko=1024, name=prefix + "_mm_dmerged", deps=deps)
    g["w_out"] = _mm_tn(sv["merged"], dx1b, p["w_out"], into=gbuf.get("w_out"), tm=4096, tn=1024, tko=512,
                        name=prefix + "_mm_dw_out")

    def gate_bwd(za, zs, ba, bs, aa, bb, dm):
        sa = jax.nn.sigmoid(za + ba)
        ss = jax.nn.sigmoid(zs + bs)
        daa, dbb = dm * sa, dm * ss
        dza, dzs = daa * aa * (1.0 - sa), dbb * bb * (1.0 - ss)
        return daa, dbb, dza, dzs, jnp.sum(dza, axis=0, keepdims=True), jnp.sum(dzs, axis=0, keepdims=True)

    z = sv["z"]
    da, dbm, dza, dzs, dba, dbs = _rowmap(
        gate_bwd,
        [(z, "row", gblk, off_g // gblk), (z, "row", gblk, off_g // gblk + ngb),
         (p["gate_bias"], "vec", gblk, 0), (p["gate_bias"], "vec", gblk, ngb),
         (sv["a"], "row", gblk, 0), (sv["bm"], "row", gblk, 0), (dmerged, "row", gblk, 0)],
        [(d, BF16, "row", gblk), (d, BF16, "row", gblk), (d, BF16, "row", gblk), (d, BF16, "row", gblk),
         (d, F32, "acc", gblk), (d, F32, "acc", gblk)],
        rows=t, tm=_row_tile(32 * gblk, t), ncol=ngb, name=prefix + "_gate_bwd")
    g["gate_bias"] = jnp.concatenate([dba, dbs], axis=1)
    dya = _mm_nt(da, p["w_attn_branch"], out_dtype=BF16, tm=1024, tn=512, tko=1024, name=prefix + "_mm_dya")
    g["w_attn_branch"] = _mm_tn(sv["ya"], da, p["w_attn_branch"], into=gbuf.get("w_attn_branch"), tm=4096, tn=512,
                                tko=512, name=prefix + "_mm_dw_ab")
    dy2 = _mm_nt(dbm, p["w_ssm_branch"], out_dtype=BF16, tm=1024, tn=512, tko=1024, name=prefix + "_mm_dy2")
    g["w_ssm_branch"] = _mm_tn(sv["y2"], dbm, p["w_ssm_branch"], into=gbuf.get("w_ssm_branch"), tm=4096, tn=512,
                               tko=512, name=prefix + "_mm_dw_sb")

    def glu_bwd(y_raw, pre, b, dy):
        yg = jax.nn.gelu(y_raw)
        _, vjp = jax.vjp(lambda a_, b_, c_: a_ * jax.nn.sigmoid(b_ + c_), yg, pre, b)
        dyg, dpre, db = vjp(dy)
        return dyg, dpre, db

    dyg_direct, dpre, g["ssm_glu_b"] = _rowmap(
        glu_bwd, [(sv["y_raw"], "row", sw, 0), (sv["pre"], "row", sw, 0), (p["ssm_glu_b"], "vec", sw, 0), (dy2, "row", sw, 0)],
        [(sw, F32, "row", sw), (sw, BF16, "row", sw), (sw, F32, "acc", sw)],
        rows=t, tm=_row_tile(24 * sw, t), name=prefix + "_glu_bwd")
    dyg2 = _mm_nt(dpre, p["ssm_glu_w"], out_dtype=F32, tm=1024, tn=1024, tko=1024, name=prefix + "_mm_dyg")
    g["ssm_glu_w"] = _mm_tn(sv["yg"], dpre, p["ssm_glu_w"], into=gbuf.get("ssm_glu_w"), tm=4096, tn=1024, tko=512,
                            name=prefix + "_mm_dw_glu")

    def gelu_bwd(y_raw, d1, d2):
        _, vjp = jax.vjp(jax.nn.gelu, y_raw)
        return vjp(d1 + d2)[0]

    dy_raw, = _rowmap(gelu_bwd, [(sv["y_raw"], "row", sw, 0), (dyg_direct, "row", sw, 0), (dyg2, "row", sw, 0)],
                      [(sw, F32, "row", sw)], rows=t, tm=_row_tile(20 * sw, t), name=prefix + "_gelu_bwd")
    sp = sv["sp"]
    du, dbr, dbi, dcr, dci, dar, dai, g["ssm_d"] = _ssm_bwd(
        z, dy_raw, sv["s0r"], sv["s0i"], sp["ar"], sp["ai"], sp["bblk_r"], sp["bblk_i"], sp["cblk_r"], sp["cblk_i"],
        p["ssm_d"], u_col=off_u, tc=dims["tc"], name=prefix + "_ssm_bwd")
    ngr, pst = p["lam_re"].shape
    hch = SSM_GROUP_CH
    dlr, dli, dldt, dbrt, dbit = _ssm_prep_bwd(
        p["lam_re"], p["lam_im"], p["log_dt"], p["b_re_t"], p["b_im_t"],
        dar.reshape(ngr, pst), dai.reshape(ngr, pst), _blockdiag_extract(dbr, hch, pst), _blockdiag_extract(dbi, hch, pst),
        name=prefix + "_ssm_prep_bwd")
    g.update(ssm_lambda_re=dlr, ssm_lambda_im=dli, ssm_log_dt=dldt.reshape(ngr),
             ssm_b_re=jnp.swapaxes(dbrt, 1, 2), ssm_b_im=jnp.swapaxes(dbit, 1, 2),
             ssm_c_re=_blockdiag_extract(dcr, hch, pst), ssm_c_im=_blockdiag_extract(dci, hch, pst))

    dq, dkv, g["q_norm_g"], g["k_norm_g"], g["attn_sinks"] = _attn_bwd(
        z, dya, p["q_norm_g"], p["k_norm_g"], p["attn_sinks"], n_q=dims["n_q"], n_kv=dims["n_kv"], name=prefix + "_attn_bwd")

    dz = jnp.concatenate([dq, dkv, du, dza, dzs], axis=1)
    deps = before_in(dq, g) if before_in else ()
    dh = _mm_nt(dz, p["w_in"], out_dtype=F32, tm=1024, tn=1664, tko=512, name=prefix + "_mm_dh", deps=deps)
    g["w_in"] = _mm_tn(sv["h"], dz, p["w_in"], into=gbuf.get("w_in"), tm=4096, tn=1664, tko=1024,
                       name=prefix + "_mm_dw_in")
    dx, dxb, g["norm_mix_g"] = _rowmap(
        norm_bwd, [(sv["x"], "row", d, 0), (p["norm_mix_g"], "vec", d, 0), (dh, "row", d, 0), (dx1, "row", d, 0)],
        [(d, F32, "row", d), (d, BF16, "row", d), (d, F32, "acc", d)],
        rows=t, tm=_row_tile(22 * d, t), name=prefix + "_norm_mix_bwd")
    return dx, dxb, g


def _loss_and_grad(y, target):
    t, d = y.shape

    def fn(yy, tt):
        e = yy - tt
        dy = e * (1.0 / d)
        return dy, dy, jnp.sum(e * e, keepdims=True).reshape(1, 1)

    dy, dyb, sq = _rowmap(fn, [(y, "row", d, 0), (target, "row", d, 0)],
                          [(d, F32, "row", d), (d, BF16, "row", d), (1, F32, "acc", 1)],
                          rows=t, tm=_row_tile(14 * d, t), name="loss")
    return sq, dy, dyb


def _local_step(x, target, n_layers, weights_of, dims, fwd_tick=None, bwd_tick=None):
    saved, params = [], []
    h = x
    for l in range(n_layers):
        p, deps = weights_of(l, h)
        params.append(p)
        tick = (lambda point, arr, l=l: fwd_tick(l, point, arr)) if fwd_tick else None
        h, sv = _layer_fwd(h, p, dims, "l%d" % l, deps, tick)
        saved.append(sv)
    sq, dy, dyb = _loss_and_grad(h, target)
    grads = [None] * n_layers
    deps = ()
    for l in reversed(range(n_layers)):
        if bwd_tick:
            ffn_done = lambda arr, g, l=l: bwd_tick(l, "ffn", (arr, g["w_ffn_in"], g["w_ffn_out"]),
                                                    {n: g[n] for n in ("w_ffn_in", "w_ffn_out")})
            mixer_done = lambda arr, g, l=l: bwd_tick(l, "mixer", (arr, g["ssm_d"]) + tuple(g[n] for n in MIXER_WEIGHTS),
                                                      {n: g[n] for n in MIXER_WEIGHTS})
        else:
            ffn_done = mixer_done = None
        dy, dyb, grads[l] = _layer_bwd(dy, dyb, saved[l], params[l], dims, "l%d" % l, {}, deps=deps,
                                       before_mixer=ffn_done, before_in=mixer_done)
        if bwd_tick:
            deps = bwd_tick(l, "in", (dy, grads[l]["w_in"]), {"w_in": grads[l]["w_in"]})
    return sq, dy, grads, deps


COL_SHARDED = ("w_in", "w_attn_branch", "w_ssm_branch", "w_ffn_in")
ROW_SHARDED = ("ssm_glu_w", "w_out", "w_ffn_out")
BIG_WEIGHTS = COL_SHARDED + ROW_SHARDED
WEIGHT_NAMES = ("norm_mix_g", "w_in", "gate_bias", "q_norm_g", "k_norm_g", "attn_sinks", "ssm_lambda_re",
                "ssm_lambda_im", "ssm_log_dt", "ssm_b_re", "ssm_b_im", "ssm_c_re", "ssm_c_im", "ssm_d", "ssm_glu_w",
                "ssm_glu_b", "w_attn_branch", "w_ssm_branch", "w_out", "norm_ffn_g", "w_ffn_in", "w_ffn_out")
SMALL_WEIGHTS = tuple(n for n in WEIGHT_NAMES if n not in BIG_WEIGHTS)
MIXER_WEIGHTS = ("w_out", "w_attn_branch", "w_ssm_branch", "ssm_glu_w")
WEIGHT_GROUPS = {"in": ("w_in",), "mixer": MIXER_WEIGHTS, "ffn": ("w_ffn_in", "w_ffn_out")}


def _dims(d, shapes, tc):
    s, _, aw, _ = shapes["w_attn_branch"]
    sw = shapes["w_ssm_branch"][2]
    in_w = shapes["w_in"][3] * s
    kvw = (in_w - aw - sw - 2 * d) // 2
    ff = shapes["w_ffn_out"][2] * s
    return dict(aw=aw, kvw=kvw, sw=sw, ff=ff, n_q=aw // HEAD_DIM, n_kv=kvw // HEAD_DIM, tc=tc)


def _big_params(big):
    p = {n: _Weight(a, 0, "col") for n, a in big.items() if n in COL_SHARDED}
    p.update({n: _Weight(a.reshape(1, 1, -1, a.shape[-1]), 0, "col") for n, a in big.items() if n in ROW_SHARDED})
    return p


def _layer_params(l, small):
    p = {}
    for n in ("norm_mix_g", "gate_bias", "q_norm_g", "k_norm_g", "ssm_d", "ssm_glu_b", "norm_ffn_g"):
        p[n] = small[n][l][None]
    p["attn_sinks"] = small["attn_sinks"][l]
    p["lam_re"] = small["ssm_lambda_re"][l]
    p["lam_im"] = small["ssm_lambda_im"][l]
    p["log_dt"] = small["ssm_log_dt"][l][:, None]
    p["b_re_t"] = jnp.swapaxes(small["ssm_b_re"][l], 1, 2)
    p["b_im_t"] = jnp.swapaxes(small["ssm_b_im"][l], 1, 2)
    p["c_re"] = small["ssm_c_re"][l]
    p["c_im"] = small["ssm_c_im"][l]
    return p


_ANY = pl.BlockSpec(memory_space=pl.ANY)
_MESH_ID = pl.DeviceIdType.MESH


def _coords():
    return lax.axis_index("x"), lax.axis_index("y"), lax.axis_index("c")


def _my_chip():
    return (2 * lax.axis_index("x") + lax.axis_index("y")).astype(jnp.int32).reshape(1)


def _my_core():
    return lax.axis_index("c").astype(jnp.int32).reshape(1)


def _cast_into_slot(w, layer, *, name, deps=()):
    _, r, c = w.shape
    tm = _row_tile(12 * c, r)

    def body(me_ref, w_ref, *rest):
        rest[-1][...] = w_ref[...].astype(rest[-1].dtype)

    return pl.pallas_call(
        body,
        out_shape=jax.ShapeDtypeStruct((N_CHIPS, 1, r, c), BF16),
        grid_spec=pltpu.PrefetchScalarGridSpec(
            num_scalar_prefetch=1,
            grid=(r // tm,),
            in_specs=[pl.BlockSpec((None, tm, c), lambda i, me: (layer, i, 0))]
            + [pl.BlockSpec(memory_space=pl.ANY)] * len(deps),
            out_specs=pl.BlockSpec((None, None, tm, c), lambda i, me: (me[0], 0, i, 0)),
        ),
        compiler_params=_cparams("parallel"),
        name=name,
    )(_my_chip(), w, *deps)


_HBM = pl.BlockSpec(memory_space=pltpu.HBM)
_SEM = pl.BlockSpec(memory_space=pltpu.SEMAPHORE)
_DATAFLOW = pltpu.SideEffectType.DATAFLOW_SIDE_EFFECTING


class _SplitExchange:
    def __init__(self, srcs, lands, build, n_copies, name):
        self.build, self.n, self.name = build, n_copies, name
        self.ns, self.nl = len(srcs), len(lands)
        self.bufs = [pltpu.with_memory_space_constraint(a, pltpu.HBM) for a in list(srcs) + list(lands)]

    def _copies(self, refs, send_sems, recv_sems):
        triples = self.build(refs[:self.ns], refs[self.ns:self.ns + self.nl])
        assert len(triples) == self.n
        return [pltpu.make_async_remote_copy(src_ref=s, dst_ref=d, send_sem=send_sems.at[k], recv_sem=recv_sems.at[k],
                                             device_id=to, device_id_type=_MESH_ID) for k, (s, d, to) in enumerate(triples)]

    def start(self, deps=()):
        nb = self.ns + self.nl

        def body(*refs):
            outs = refs[nb + len(deps):]
            for cp in self._copies(refs, outs[0], outs[1]):
                cp.start()
            outs[-1][...] = jnp.zeros_like(outs[-1])

        sems = pltpu.SemaphoreType.DMA((self.n,))
        res = pl.pallas_call(
            body,
            out_shape=(sems, sems, *[pltpu.HBM(b.shape, b.dtype) for b in self.bufs], jax.ShapeDtypeStruct((8, 128), F32)),
            in_specs=[_HBM] * nb + [_ANY] * len(deps),
            out_specs=(_SEM, _SEM, *[_HBM] * nb, pl.BlockSpec(memory_space=pltpu.VMEM)),
            input_output_aliases={i: 2 + i for i in range(nb)},
            compiler_params=pltpu.CompilerParams(has_side_effects=_DATAFLOW),
            name=self.name + "_start",
        )(*self.bufs, *deps)
        self.send_sems, self.recv_sems = res[0], res[1]
        self.bufs = list(res[2:2 + nb])
        return res[-1]

    def wait(self, after=()):
        nb = self.ns + self.nl
        after = tuple(after) if isinstance(after, (tuple, list)) else (after,)

        def body(*refs):
            for cp in self._copies(refs, refs[nb], refs[nb + 1]):
                cp.wait_send()
                cp.wait_recv()

        res = pl.pallas_call(
            body,
            out_shape=tuple(pltpu.HBM(b.shape, b.dtype) for b in self.bufs),
            in_specs=[_HBM] * nb + [_SEM, _SEM] + [_ANY] * len(after),
            out_specs=tuple([_HBM] * nb),
            input_output_aliases={i: i for i in range(nb)},
            compiler_params=pltpu.CompilerParams(has_side_effects=_DATAFLOW),
            name=self.name + "_wait",
        )(*self.bufs, self.send_sems, self.recv_sems, *after)
        res = list(res)
        return res[:self.ns], res[self.ns:]


def _other_chips(x, y):
    return [(1 - x, y), (x, 1 - y), (1 - x, 1 - y)]


def _gather_steps(bufs, tag, deps, publish):
    n = len(bufs)
    half = lambda ref, i, slot, hc: ref.at[slot, :, pl.ds(hc * (bufs[i].shape[2] // 2), bufs[i].shape[2] // 2), :]

    def over_ici(srcs, lands):
        x, y, c = _coords()
        me = 2 * x + y
        return [(half(srcs[i], i, me, c), half(srcs[i], i, me, c), (px, py, c))
                for i in range(n) for px, py in _other_chips(x, y)]

    def to_sibling(srcs, lands):
        x, y, c = _coords()
        return [(half(srcs[i], i, 2 * px + py, c), half(srcs[i], i, 2 * px + py, c), (x, y, 1 - c))
                for i in range(n) for px, py in _other_chips(x, y)]

    ex = _SplitExchange(bufs, [], over_ici, 3 * n, tag + "_ici")
    after = yield ex.start(deps)
    bufs, _ = ex.wait(after)
    ex = _SplitExchange(bufs, [], to_sibling, 3 * n, tag + "_d2d")
    after = yield ex.start()
    bufs, _ = ex.wait(after)
    publish(bufs)


def _reduce_steps(grads, tag, publish):
    n = len(grads)
    rh = [g.shape[1] // 2 for g in grads]
    theirs = [lax.empty((g.shape[0], g.shape[1] // 2, g.shape[2]), F32) for g in grads]

    def halves(srcs, lands):
        x, y, c = _coords()
        return [(srcs[i].at[:, pl.ds((1 - c) * rh[i], rh[i]), :], lands[i], (x, y, 1 - c)) for i in range(n)]

    def chips(srcs, lands):
        x, y, c = _coords()
        me = 2 * x + y
        return [(srcs[i].at[2 * px + py], lands[i].at[me], (px, py, c)) for i in range(n) for px, py in _other_chips(x, y)]

    def sibling(srcs, lands):
        x, y, c = _coords()
        return [(srcs[i], lands[i], (x, y, 1 - c)) for i in range(n)]

    ex = _SplitExchange(grads, theirs, halves, n, tag + "_halves")
    after = yield ex.start()
    grads, theirs = ex.wait(after)
    parts = [_add_own_half(g, t, name="%s_add_own_half_%d" % (tag, i)) for i, (g, t) in enumerate(zip(grads, theirs))]
    ex = _SplitExchange(parts, [lax.empty(p.shape, p.dtype) for p in parts], chips, 3 * n, tag + "_chips")
    after = yield ex.start()
    parts, got = ex.wait(after)
    mine = [_sum_chips(p, g, name="%s_sum_chips_%d" % (tag, i)) for i, (p, g) in enumerate(zip(parts, got))]
    ex = _SplitExchange(mine, [lax.empty(m.shape, m.dtype) for m in mine], sibling, n, tag + "_sibling")
    after = yield ex.start()
    mine, theirs = ex.wait(after)
    publish(list(zip(mine, theirs)))


def _allreduce_steps(buf, tag, publish):
    r, c = buf.shape

    def to_sibling(srcs, lands):
        x, y, cc = _coords()
        return [(srcs[0], lands[0], (x, y, 1 - cc))]

    def over_ici(srcs, lands):
        x, y, cc = _coords()
        me = 2 * x + y
        return [(srcs[0].at[me], srcs[0].at[me], (px, py, cc)) for px, py in _other_chips(x, y)]

    def halves(srcs, lands):
        x, y, cc = _coords()
        return [(srcs[0].at[cc], srcs[0].at[cc], (x, y, 1 - cc))]

    ex = _SplitExchange([buf], [lax.empty(buf.shape, buf.dtype)], to_sibling, 1, tag + "_cores")
    after = yield ex.start()
    (mine,), (theirs,) = ex.wait(after)
    ex = _SplitExchange([_add_half_into_slot(mine, theirs, name=tag + "_chip_sum")], [], over_ici, N_CHIPS - 1, tag + "_chips")
    after = yield ex.start()
    (parts,), _ = ex.wait(after)
    total_half = _sum_slots(parts, name=tag + "_sum_chips")
    ex = _SplitExchange([_place_into_slot(total_half, 2, _my_core(), name=tag + "_place_half")], [], halves, 1, tag + "_halves")
    after = yield ex.start()
    (both,), _ = ex.wait(after)
    publish(both.reshape(r, c))


class _Exchanges:
    def __init__(self):
        self.running = []

    def launch(self, steps):
        self.running.append(steps)
        return next(steps)

    def advance(self, steps, after):
        try:
            return steps.send(after)
        except StopIteration:
            self.running.remove(steps)
            return None

    def advance_all(self, after):
        tokens = [self.advance(steps, after) for steps in list(self.running)]
        return tuple(t for t in tokens if t is not None)


def _add_own_half(g, theirs, *, name):
    s, r, c = g.shape
    rh = r // 2
    tm = _row_tile(10 * c, rh)
    nb = rh // tm

    def body(core_ref, g_ref, t_ref, o_ref):
        o_ref[...] = (g_ref[...] + t_ref[...]).astype(o_ref.dtype)

    return pl.pallas_call(
        body,
        out_shape=jax.ShapeDtypeStruct((s, rh, c), BF16),
        grid_spec=pltpu.PrefetchScalarGridSpec(
            num_scalar_prefetch=1,
            grid=(s, nb),
            in_specs=[pl.BlockSpec((None, tm, c), lambda k, i, core: (k, core[0] * nb + i, 0)),
                      pl.BlockSpec((None, tm, c), lambda k, i, core: (k, i, 0))],
            out_specs=pl.BlockSpec((None, tm, c), lambda k, i, core: (k, i, 0)),
        ),
        compiler_params=_cparams("parallel", "parallel"),
        name=name,
    )(_my_core(), g, theirs)


def _sum_chips(part, got, *, name):
    s, rh, c = part.shape
    tm = _row_tile(14 * c, rh)

    def body(me_ref, p_ref, a_ref, b_ref, c_ref, o_ref):
        o_ref[...] = ((p_ref[...].astype(F32) + a_ref[...].astype(F32)) + b_ref[...].astype(F32)) + c_ref[...].astype(F32)

    slot = lambda k: (lambda i, me: ((me[0] + k) % s, i, 0))
    return pl.pallas_call(
        body,
        out_shape=jax.ShapeDtypeStruct((rh, c), F32),
        grid_spec=pltpu.PrefetchScalarGridSpec(
            num_scalar_prefetch=1,
            grid=(rh // tm,),
            in_specs=[pl.BlockSpec((None, tm, c), slot(k)) for k in range(s)],
            out_specs=pl.BlockSpec((tm, c), lambda i, me: (i, 0)),
        ),
        compiler_params=_cparams("parallel"),
        name=name,
    )(_my_chip(), part, got, got, got)


def _place_into_slot(buf, n_slots, slot, *, name):
    r, c = buf.shape
    tm = _row_tile(8 * c, r)

    def body(slot_ref, i_ref, o_ref):
        o_ref[...] = i_ref[...]

    return pl.pallas_call(
        body,
        out_shape=jax.ShapeDtypeStruct((n_slots, r, c), buf.dtype),
        grid_spec=pltpu.PrefetchScalarGridSpec(
            num_scalar_prefetch=1,
            grid=(r // tm,),
            in_specs=[pl.BlockSpec((tm, c), lambda i, s: (i, 0))],
            out_specs=pl.BlockSpec((None, tm, c), lambda i, s: (s[0], i, 0)),
        ),
        compiler_params=_cparams("parallel"),
        name=name,
    )(slot, buf)


def _add_half_into_slot(mine, theirs, *, name):
    r, c = mine.shape
    rh = r // 2
    tm = _row_tile(12 * c, rh)
    nb = rh // tm
    where = jnp.concatenate([_my_chip(), _my_core()])

    def body(where_ref, a_ref, b_ref, o_ref):
        o_ref[...] = a_ref[...] + b_ref[...]

    half = pl.BlockSpec((tm, c), lambda i, w: (w[1] * nb + i, 0))
    return pl.pallas_call(
        body,
        out_shape=jax.ShapeDtypeStruct((N_CHIPS, rh, c), mine.dtype),
        grid_spec=pltpu.PrefetchScalarGridSpec(
            num_scalar_prefetch=1,
            grid=(nb,),
            in_specs=[half, half],
            out_specs=pl.BlockSpec((None, tm, c), lambda i, w: (w[0], i, 0)),
        ),
        compiler_params=_cparams("parallel"),
        name=name,
    )(where, mine, theirs)


def _sum_slots(arr, *, name):
    s, r, c = arr.shape
    tm = _row_tile(4 * c * (s + 1), r)

    def body(*refs):
        acc = refs[0][...]
        for ref in refs[1:s]:
            acc = acc + ref[...]
        refs[s][...] = acc

    return pl.pallas_call(
        body,
        out_shape=jax.ShapeDtypeStruct((r, c), arr.dtype),
        grid=(r // tm,),
        in_specs=[pl.BlockSpec((None, tm, c), lambda i, k=k: (k, i, 0)) for k in range(s)],
        out_specs=pl.BlockSpec((tm, c), lambda i: (i, 0)),
        compiler_params=_cparams("parallel"),
        name=name,
    )(*([arr] * s))


def _adamw_fn(w, g, m, v):
    m = ADAM_B1 * m + (1.0 - ADAM_B1) * g
    v = ADAM_B2 * v + (1.0 - ADAM_B2) * jnp.square(g)
    m_hat = m / (1.0 - ADAM_B1 ** ADAM_STEP)
    v_hat = v / (1.0 - ADAM_B2 ** ADAM_STEP)
    delta = -ADAM_LR * (m_hat / (jnp.sqrt(v_hat) + ADAM_EPS) + ADAM_WD * w)
    return delta, m, v


def _adamw(w, g, m, v, *, name):
    rows, cols = w.shape
    ins = [(a, "row", cols, 0) for a in (w, g, m, v)]
    outs = [(cols, F32, "row", cols)] * 3
    return _rowmap(_adamw_fn, ins, outs, rows=rows, tm=_row_tile(56 * cols, rows), name=name)


def _adamw_sharded(w, m, v, g_mine, g_sibling, layer, into, *, name, deps=()):
    nl, r, c = w.shape
    rh = r // 2
    tm = _row_tile(40 * c, rh)
    nb = rh // tm
    n_into = 0 if into is None else 4

    def body(core_ref, w_ref, m_ref, v_ref, a_ref, b_ref, *rest):
        g_ref, d_ref, nm_ref, nv_ref = rest[n_into + len(deps):]
        g = jnp.where(pl.program_id(0) == core_ref[0], a_ref[...], b_ref[...])
        delta, nm, nv = _adamw_fn(w_ref[...], g, m_ref[...], v_ref[...])
        g_ref[...] = g
        d_ref[...] = delta
        nm_ref[...] = nm
        nv_ref[...] = nv

    whole = pl.BlockSpec((None, tm, c), lambda h, i, core: (layer, h * nb + i, 0))
    half = pl.BlockSpec((tm, c), lambda h, i, core: (i, 0))
    shape = jax.ShapeDtypeStruct((nl, r, c), F32)
    return pl.pallas_call(
        body,
        out_shape=(shape, shape, shape, shape),
        grid_spec=pltpu.PrefetchScalarGridSpec(
            num_scalar_prefetch=1,
            grid=(2, nb),
            in_specs=[whole, whole, whole, half, half] + [pl.BlockSpec(memory_space=pl.ANY)] * (n_into + len(deps)),
            out_specs=(whole, whole, whole, whole),
        ),
        input_output_aliases={6 + k: k for k in range(n_into)},
        compiler_params=_cparams("parallel", "parallel"),
        name=name,
    )(_my_core(), w, m, v, g_mine, g_sibling, *(into or ()), *deps)


def _pack(arrays):
    flat = jnp.concatenate([a.reshape(-1) for a in arrays])
    pad = (-flat.shape[0]) % (256 * 128)
    return jnp.pad(flat, (0, pad)).reshape(-1, 128)


def _unpack(buf, shapes):
    flat = buf.reshape(-1)
    out, off = [], 0
    for s in shapes:
        n = math.prod(s)
        out.append(flat[off:off + n].reshape(s))
        off += n
    return out


def kernel(x, norm_mix_g, w_in, gate_bias, q_norm_g, k_norm_g, attn_sinks, ssm_lambda_re, ssm_lambda_im, ssm_log_dt, ssm_b_re, ssm_b_im, ssm_c_re, ssm_c_im, ssm_d, ssm_glu_w, ssm_glu_b, w_attn_branch, w_ssm_branch, w_out, norm_ffn_g, w_ffn_in, w_ffn_out, loss_target, m_norm_mix_g, m_w_in, m_gate_bias, m_q_norm_g, m_k_norm_g, m_attn_sinks, m_ssm_lambda_re, m_ssm_lambda_im, m_ssm_log_dt, m_ssm_b_re, m_ssm_b_im, m_ssm_c_re, m_ssm_c_im, m_ssm_d, m_ssm_glu_w, m_ssm_glu_b, m_w_attn_branch, m_w_ssm_branch, m_w_out, m_norm_ffn_g, m_w_ffn_in, m_w_ffn_out, v_norm_mix_g, v_w_in, v_gate_bias, v_q_norm_g, v_k_norm_g, v_attn_sinks, v_ssm_lambda_re, v_ssm_lambda_im, v_ssm_log_dt, v_ssm_b_re, v_ssm_b_im, v_ssm_c_re, v_ssm_c_im, v_ssm_d, v_ssm_glu_w, v_ssm_glu_b, v_w_attn_branch, v_w_ssm_branch, v_w_out, v_norm_ffn_g, v_w_ffn_in, v_w_ffn_out):
    w = dict(norm_mix_g=norm_mix_g, w_in=w_in, gate_bias=gate_bias, q_norm_g=q_norm_g, k_norm_g=k_norm_g,
             attn_sinks=attn_sinks, ssm_lambda_re=ssm_lambda_re, ssm_lambda_im=ssm_lambda_im, ssm_log_dt=ssm_log_dt,
             ssm_b_re=ssm_b_re, ssm_b_im=ssm_b_im, ssm_c_re=ssm_c_re, ssm_c_im=ssm_c_im, ssm_d=ssm_d,
             ssm_glu_w=ssm_glu_w, ssm_glu_b=ssm_glu_b, w_attn_branch=w_attn_branch, w_ssm_branch=w_ssm_branch,
             w_out=w_out, norm_ffn_g=norm_ffn_g, w_ffn_in=w_ffn_in, w_ffn_out=w_ffn_out)
    m = dict(norm_mix_g=m_norm_mix_g, w_in=m_w_in, gate_bias=m_gate_bias, q_norm_g=m_q_norm_g, k_norm_g=m_k_norm_g,
             attn_sinks=m_attn_sinks, ssm_lambda_re=m_ssm_lambda_re, ssm_lambda_im=m_ssm_lambda_im,
             ssm_log_dt=m_ssm_log_dt, ssm_b_re=m_ssm_b_re, ssm_b_im=m_ssm_b_im, ssm_c_re=m_ssm_c_re,
             ssm_c_im=m_ssm_c_im, ssm_d=m_ssm_d, ssm_glu_w=m_ssm_glu_w, ssm_glu_b=m_ssm_glu_b,
             w_attn_branch=m_w_attn_branch, w_ssm_branch=m_w_ssm_branch, w_out=m_w_out, norm_ffn_g=m_norm_ffn_g,
             w_ffn_in=m_w_ffn_in, w_ffn_out=m_w_ffn_out)
    v = dict(norm_mix_g=v_norm_mix_g, w_in=v_w_in, gate_bias=v_gate_bias, q_norm_g=v_q_norm_g, k_norm_g=v_k_norm_g,
             attn_sinks=v_attn_sinks, ssm_lambda_re=v_ssm_lambda_re, ssm_lambda_im=v_ssm_lambda_im,
             ssm_log_dt=v_ssm_log_dt, ssm_b_re=v_ssm_b_re, ssm_b_im=v_ssm_b_im, ssm_c_re=v_ssm_c_re,
             ssm_c_im=v_ssm_c_im, ssm_d=v_ssm_d, ssm_glu_w=v_ssm_glu_w, ssm_glu_b=v_ssm_glu_b,
             w_attn_branch=v_w_attn_branch, w_ssm_branch=v_w_ssm_branch, w_out=v_w_out, norm_ffn_g=v_norm_ffn_g,
             w_ffn_in=v_w_ffn_in, w_ffn_out=v_w_ffn_out)
    n_layers = norm_mix_g.shape[0]
    d_model = x.shape[-1]
    seq = x.shape[1]

    exchanges = _Exchanges()
    params = [_layer_params(l, w) for l in range(n_layers)]
    gathers = {}

    def gather(l, group, bufs, deps=()):
        names = WEIGHT_GROUPS[group]
        steps = _gather_steps(bufs, "ag%d_%s" % (l, group), deps,
                              lambda got: params[l].update(_big_params(dict(zip(names, got)))))
        gathers[l, group] = steps
        return exchanges.launch(steps)

    started = gather(0, "in", [_cast_into_slot(w["w_in"], 0, name="cast0_w_in")])
    casts = {(l, group): [_cast_into_slot(w[n], l, name="cast%d_%s" % (l, n), deps=(started,)) for n in WEIGHT_GROUPS[group]]
             for l in range(n_layers) for group in WEIGHT_GROUPS if (l, group) != (0, "in")}
    forwarded = exchanges.advance(gathers[0, "in"], tuple(b for bufs in casts.values() for b in bufs))
    first_deps = (forwarded, gather(0, "mixer", casts[0, "mixer"], (forwarded,)), gather(0, "ffn", casts[0, "ffn"], (forwarded,)))
    sizes = {n: (N_CHIPS, 1) + w[n].shape[1:] for n in BIG_WEIGHTS}
    dims = _dims(d_model, sizes, min(512, seq))

    def weights_of(l, h):
        if l == 0:
            return params[0], first_deps
        for group in WEIGHT_GROUPS:
            exchanges.advance(gathers[l, group], h)
        return params[l], ()

    def fwd_tick(l, point, arr):
        tokens = []
        if l == 0 and point == "norm":
            tokens.append(exchanges.advance(gathers[0, "in"], arr))
        if l == 0 and point in ("in", "attn"):
            tokens.append(exchanges.advance(gathers[0, "mixer"], arr))
        if l == 0 and point in ("ssm", "out"):
            tokens.append(exchanges.advance(gathers[0, "ffn"], arr))
        if l + 1 < n_layers and point == "attn":
            tokens += [gather(l + 1, group, casts[l + 1, group]) for group in WEIGHT_GROUPS]
        if l + 1 < n_layers and point == "act":
            tokens += [exchanges.advance(gathers[l + 1, group], arr) for group in WEIGHT_GROUPS]
        return tuple(t for t in tokens if t is not None)

    reduced, ready = {}, []

    def bwd_tick(l, stage, arr, stage_grads):
        tokens = exchanges.advance_all(arr)
        names = tuple(stage_grads)

        def publish(halves):
            reduced.update({(l, n): h for n, h in zip(names, halves)})
            ready.append((l, names))

        grads4 = [stage_grads[n].reshape(N_CHIPS, -1, stage_grads[n].shape[-1]) for n in names]
        return tokens + (exchanges.launch(_reduce_steps(grads4, "rs%d_%s" % (l, stage), publish)),)

    sq, dx, grads, last_tokens = _local_step(x[0], loss_target[0], n_layers, weights_of, dims, fwd_tick, bwd_tick)
    loss = lax.psum(sq[0, 0], MESH_AXES) * (0.5 / d_model)

    small_shapes = [w[n].shape for n in SMALL_WEIGHTS]
    small_local = [jnp.stack([grads[l][n].reshape(w[n].shape[1:]) for l in range(n_layers)]) for n in SMALL_WEIGHTS]
    shared = []
    tokens = last_tokens + (exchanges.launch(_allreduce_steps(_pack(small_local), "small_grads", shared.append)),)

    adam = {n: None for n in BIG_WEIGHTS}
    grad, delta, new_m, new_v = {}, {}, {}, {}
    while exchanges.running or ready or shared:
        after = [dx]
        for l, names in ready[:2]:
            for n in names:
                mine, sibling = reduced[l, n]
                adam[n] = _adamw_sharded(w[n], m[n], v[n], mine, sibling, l, adam[n], name="adamw%d_%s" % (l, n), deps=tokens)
                after.append(adam[n][1])
        del ready[:2]
        if shared:
            grad.update(zip(SMALL_WEIGHTS, _unpack(shared.pop(), small_shapes)))
            for n in SMALL_WEIGHTS:
                flat = lambda a: a.reshape(-1, a.shape[-1])
                res = _adamw(flat(w[n]), flat(grad[n]), flat(m[n]), flat(v[n]), name="adamw_" + n)
                delta[n], new_m[n], new_v[n] = [r.reshape(w[n].shape) for r in res]
            after += [delta[n] for n in SMALL_WEIGHTS]
        tokens = exchanges.advance_all(tuple(after))
    for n in BIG_WEIGHTS:
        grad[n], delta[n], new_m[n], new_v[n] = adam[n]

    return (loss, dx[None], *[grad[n] for n in WEIGHT_NAMES], *[delta[n] for n in WEIGHT_NAMES],
            *[new_m[n] for n in WEIGHT_NAMES], *[new_v[n] for n in WEIGHT_NAMES])
```
